```python
import jax
import jax.numpy as jnp
from jax import lax
import numpy as np


D_MODEL = 1024
BATCH = 8
SEQ = 8192
DEPTH = 1

CHUNK = 64
D_MIX = D_MODEL
RET_WIDTH = D_MIX // 2
RET_HEADS = 4
RET_DK = RET_WIDTH // RET_HEADS
RET_DV = RET_WIDTH // RET_HEADS
ROPE_BASE = 10000.0
GLA_WIDTH = D_MIX - RET_WIDTH
GLA_HEADS = 4
GLA_KEY_WIDTH = GLA_WIDTH // 2
GLA_DK = GLA_KEY_WIDTH // GLA_HEADS
GLA_DV = GLA_WIDTH // GLA_HEADS
GLA_GATE_RANK = 16
GLA_GATE_TAU = 16.0
D_FF = 4 * D_MODEL
LN_EPS = 1e-5
DEEPNORM_ALPHA = (2.0 * DEPTH) ** 0.25
DEEPNORM_BETA = (8.0 * DEPTH) ** -0.25
IN_SPLITS = (RET_WIDTH, RET_WIDTH, RET_WIDTH, RET_WIDTH,
             GLA_KEY_WIDTH, GLA_KEY_WIDTH, GLA_WIDTH, GLA_WIDTH, GLA_GATE_RANK)
D_IN_PROJ = sum(IN_SPLITS)
VALUE_SLOTS = (2, 6)

kernel_name = 'hybrid_retention_gla_deepnorm_adaln'


def _layer_norm(x, w=None, b=None):
    xf = x.astype(jnp.float32)
    mu = jnp.mean(xf, axis=-1, keepdims=True)
    var = jnp.mean(jnp.square(xf - mu), axis=-1, keepdims=True)
    y = (xf - mu) * lax.rsqrt(var + LN_EPS)
    if w is not None:
        y = y * w.astype(jnp.float32) + b.astype(jnp.float32)
    return y.astype(x.dtype)


def _head_norm(o, w, center):
    if center:
        o = o - jnp.mean(o, axis=-1, keepdims=True)
    o = o * lax.rsqrt(jnp.mean(jnp.square(o), axis=-1, keepdims=True) + LN_EPS)
    B, S, H, d = o.shape
    return o.reshape(B, S, H * d) * w.astype(jnp.float32)


def _rotary(x, pos):
    half = x.shape[-1] // 2
    inv = 1.0 / (ROPE_BASE ** jnp.linspace(0.0, 1.0, half, dtype=jnp.float32))
    ang = pos[:, None] * inv[None, :]
    cos = jnp.cos(ang)[:, None, :]
    sin = jnp.sin(ang)[:, None, :]
    x1, x2 = x[..., :half], x[..., half:]
    return jnp.concatenate([x1 * cos - x2 * sin, x2 * cos + x1 * sin], axis=-1)


def _retention(q, k, v):
    B, S, H, dk = q.shape
    dv = v.shape[-1]
    N = S // CHUNK
    log_gamma = jnp.log(1.0 - 2.0 ** (-5.0 - jnp.arange(H, dtype=jnp.float32)))
    pos = jnp.arange(S, dtype=jnp.float32)
    q = _rotary(q, pos) * (dk ** -0.5)
    k = _rotary(k, pos)
    qc = q.reshape(B, N, CHUNK, H, dk)
    kc = k.reshape(B, N, CHUNK, H, dk)
    vc = v.reshape(B, N, CHUNK, H, dv)
    idx = jnp.arange(CHUNK, dtype=jnp.float32)
    dist = jnp.abs(idx[:, None] - idx[None, :])
    intra_decay = jnp.exp(log_gamma[:, None, None] * dist)
    scores = jnp.einsum('bnchd,bnshd->bnhcs', qc, kc) * intra_decay
    o = jnp.einsum('bnhcs,bnshe->bnche', scores, vc)
    k_decay = jnp.exp(log_gamma[None, :] * (CHUNK - 1.0 - idx)[:, None])
    q_decay = jnp.exp(log_gamma[None, :] * (idx + 1.0)[:, None])
    chunk_decay = jnp.exp(log_gamma * CHUNK)[None, :, None, None]
    U = jnp.einsum('bnshd,bnshe->nbhde', kc * k_decay[:, :, None], vc)

    def step(R, U_i):
        return chunk_decay * R + U_i, R

    _, R_prev = lax.scan(step, jnp.zeros((B, H, dk, dv), jnp.float32), U)
    o = o + jnp.einsum('bnchd,nbhde->bnche', qc * q_decay[:, :, None], R_prev)
    return o.reshape(B, S, H, dv)


def _gla(q, k, v, log_a):
    B, S, H, dk = q.shape
    dv = v.shape[-1]
    N = S // CHUNK
    q = q * (dk ** -0.5)

    def to_chunks(t):
        return t.reshape(B, N, CHUNK, H, t.shape[-1]).transpose(1, 0, 2, 3, 4)

    def step(state, inp):
        qi, ki, vi, lai = inp
        b = jnp.cumsum(lai, axis=1)
        decay = jnp.exp(-jnp.abs(b[:, :, None] - b[:, None, :]))
        A = jnp.einsum('bthd,bshd,btshd->bhts', qi, ki, decay)
        o = jnp.einsum('bhts,bshe->bthe', A, vi)
        o = o + jnp.einsum('bthd,bhde->bthe', qi * jnp.exp(b), state)
        b_last = b[:, -1]
        new_state = jnp.exp(b_last)[..., None] * state + jnp.einsum(
            'bshd,bshe->bhde', ki * jnp.exp(b_last[:, None] - b), vi)
        return new_state, o

    _, o = lax.scan(step, jnp.zeros((B, H, dk, dv), jnp.float32),
                    (to_chunks(q), to_chunks(k), to_chunks(v), to_chunks(log_a)))
    return o.transpose(1, 0, 2, 3, 4).reshape(B, S, H, dv)


def _token_mixer(u, w_in, ret_norm_w, gla_gate_w, gla_gate_b, gla_norm_w, w_out):
    B, S, _ = u.shape
    proj = (u @ w_in).astype(jnp.float32)
    cuts = [int(v) for v in np.cumsum(IN_SPLITS)[:-1]]
    rq, rk, rv, rg, gq, gk, gv, gg, glr = jnp.split(proj, cuts, axis=-1)
    ro = _retention(rq.reshape(B, S, RET_HEADS, RET_DK),
                    rk.reshape(B, S, RET_HEADS, RET_DK),
                    rv.reshape(B, S, RET_HEADS, RET_DV))
    ro = _head_norm(ro, ret_norm_w, center=True) * jax.nn.silu(rg)
    gate_logit = glr @ gla_gate_w.astype(jnp.float32) + gla_gate_b.astype(jnp.float32)
    log_a = jax.nn.log_sigmoid(gate_logit) / GLA_GATE_TAU
    go = _gla(gq.reshape(B, S, GLA_HEADS, GLA_DK),
              gk.reshape(B, S, GLA_HEADS, GLA_DK),
              gv.reshape(B, S, GLA_HEADS, GLA_DV),
              log_a.reshape(B, S, GLA_HEADS, GLA_DK))
    go = _head_norm(go, gla_norm_w, center=False) * jax.nn.silu(gg)
    mixed = jnp.concatenate([ro, go], axis=-1).astype(u.dtype)
    return mixed @ w_out


def _fwd_setup_inputs(seed: int = 0) -> dict:
    key = jax.random.key(seed)
    ks = jax.random.split(key, 16)
    nrm = jax.random.normal
    offs = np.concatenate([[0], np.cumsum(IN_SPLITS)])
    col_scale = np.ones((D_IN_PROJ,), np.float32)
    for slot in VALUE_SLOTS:
        col_scale[int(offs[slot]):int(offs[slot + 1])] = DEEPNORM_BETA
    return {
        'x': nrm(ks[0], (BATCH, SEQ, D_MODEL), jnp.float32),
        'c': nrm(ks[1], (BATCH, D_MODEL), jnp.float32),
        'w_ada': nrm(ks[2], (DEPTH, D_MODEL, 6 * D_MODEL), jnp.float32) * (0.5 * D_MODEL ** -0.5),
        'b_ada': 0.02 * nrm(ks[3], (DEPTH, 6 * D_MODEL), jnp.float32),
        'w_in': nrm(ks[4], (DEPTH, D_MODEL, D_IN_PROJ), jnp.float32) * (D_MODEL ** -0.5) * jnp.asarray(col_scale),
        'ret_norm_w': 1.0 + 0.02 * nrm(ks[5], (DEPTH, RET_WIDTH), jnp.float32),
        'gla_gate_w': nrm(ks[6], (DEPTH, GLA_GATE_RANK, GLA_KEY_WIDTH), jnp.float32) * (GLA_GATE_RANK ** -0.5),
        'gla_gate_b': 0.1 * nrm(ks[7], (DEPTH, GLA_KEY_WIDTH), jnp.float32),
        'gla_norm_w': 1.0 + 0.02 * nrm(ks[8], (DEPTH, GLA_WIDTH), jnp.float32),
        'w_out': nrm(ks[9], (DEPTH, D_MIX, D_MODEL), jnp.float32) * (D_MIX ** -0.5) * DEEPNORM_BETA,
        'ln1_w': 1.0 + 0.02 * nrm(ks[10], (DEPTH, D_MODEL), jnp.float32),
        'ln1_b': 0.02 * nrm(ks[11], (DEPTH, D_MODEL), jnp.float32),
        'w_ff1': nrm(ks[12], (DEPTH, D_MODEL, D_FF), jnp.float32) * (D_MODEL ** -0.5) * DEEPNORM_BETA,
        'w_ff2': nrm(ks[13], (DEPTH, D_FF, D_MODEL), jnp.float32) * (D_FF ** -0.5) * DEEPNORM_BETA,
        'ln2_w': 1.0 + 0.02 * nrm(ks[14], (DEPTH, D_MODEL), jnp.float32),
        'ln2_b': 0.02 * nrm(ks[15], (DEPTH, D_MODEL), jnp.float32),
    }


def _fwd_reference(x, c, w_ada, b_ada, w_in, ret_norm_w, gla_gate_w, gla_gate_b, gla_norm_w,
              w_out, ln1_w, ln1_b, w_ff1, w_ff2, ln2_w, ln2_b):
    for l in range(DEPTH):
        mod = jax.nn.silu(c) @ w_ada[l] + b_ada[l]
        shift1, scale1, gate1, shift2, scale2, gate2 = jnp.split(mod, 6, axis=-1)
        u = _layer_norm(x) * (1.0 + scale1[:, None, :]) + shift1[:, None, :]
        m = _token_mixer(u, w_in[l], ret_norm_w[l], gla_gate_w[l], gla_gate_b[l],
                         gla_norm_w[l], w_out[l])
        x = _layer_norm(DEEPNORM_ALPHA * x + gate1[:, None, :] * m, ln1_w[l], ln1_b[l])
        u2 = _layer_norm(x) * (1.0 + scale2[:, None, :]) + shift2[:, None, :]
        f = jnp.square(jax.nn.relu(u2 @ w_ff1[l])) @ w_ff2[l]
        x = _layer_norm(DEEPNORM_ALPHA * x + gate2[:, None, :] * f, ln2_w[l], ln2_b[l])
    return x


import jax as _jax
import jax.numpy as _jnp

TWIN_FORMAT = 'train_step'
FWD_PARAMS = ['x', 'c', 'w_ada', 'b_ada', 'w_in', 'ret_norm_w', 'gla_gate_w', 'gla_gate_b', 'gla_norm_w', 'w_out', 'ln1_w', 'ln1_b', 'w_ff1', 'w_ff2', 'ln2_w', 'ln2_b']
TWIN_WEIGHTS = ['w_ada', 'b_ada', 'w_in', 'ret_norm_w', 'gla_gate_w', 'gla_gate_b', 'gla_norm_w', 'w_out', 'ln1_w', 'ln1_b', 'w_ff1', 'w_ff2', 'ln2_w', 'ln2_b']
TWIN_DIFF_INPUT = 'x'
TWIN_INPUTS = ['x', 'c', 'w_ada', 'b_ada', 'w_in', 'ret_norm_w', 'gla_gate_w', 'gla_gate_b', 'gla_norm_w', 'w_out', 'ln1_w', 'ln1_b', 'w_ff1', 'w_ff2', 'ln2_w', 'ln2_b', 'loss_target', 'm_w_ada', 'm_b_ada', 'm_w_in', 'm_ret_norm_w', 'm_gla_gate_w', 'm_gla_gate_b', 'm_gla_norm_w', 'm_w_out', 'm_ln1_w', 'm_ln1_b', 'm_w_ff1', 'm_w_ff2', 'm_ln2_w', 'm_ln2_b', 'v_w_ada', 'v_b_ada', 'v_w_in', 'v_ret_norm_w', 'v_gla_gate_w', 'v_gla_gate_b', 'v_gla_norm_w', 'v_w_out', 'v_ln1_w', 'v_ln1_b', 'v_w_ff1', 'v_w_ff2', 'v_ln2_w', 'v_ln2_b']
TWIN_OUTPUTS = ['loss', 'grad_x', 'grad_w_ada', 'grad_b_ada', 'grad_w_in', 'grad_ret_norm_w', 'grad_gla_gate_w', 'grad_gla_gate_b', 'grad_gla_norm_w', 'grad_w_out', 'grad_ln1_w', 'grad_ln1_b', 'grad_w_ff1', 'grad_w_ff2', 'grad_ln2_w', 'grad_ln2_b', 'delta_w_ada', 'delta_b_ada', 'delta_w_in', 'delta_ret_norm_w', 'delta_gla_gate_w', 'delta_gla_gate_b', 'delta_gla_norm_w', 'delta_w_out', 'delta_ln1_w', 'delta_ln1_b', 'delta_w_ff1', 'delta_w_ff2', 'delta_ln2_w', 'delta_ln2_b', 'new_m_w_ada', 'new_m_b_ada', 'new_m_w_in', 'new_m_ret_norm_w', 'new_m_gla_gate_w', 'new_m_gla_gate_b', 'new_m_gla_norm_w', 'new_m_w_out', 'new_m_ln1_w', 'new_m_ln1_b', 'new_m_w_ff1', 'new_m_w_ff2', 'new_m_ln2_w', 'new_m_ln2_b', 'new_v_w_ada', 'new_v_b_ada', 'new_v_w_in', 'new_v_ret_norm_w', 'new_v_gla_gate_w', 'new_v_gla_gate_b', 'new_v_gla_norm_w', 'new_v_w_out', 'new_v_ln1_w', 'new_v_ln1_b', 'new_v_w_ff1', 'new_v_w_ff2', 'new_v_ln2_w', 'new_v_ln2_b']
TWIN_LEAF_KINDS = {'loss': 'loss', 'grad_x': 'grad_x', 'grad_w_ada': 'grad_w', 'grad_b_ada': 'grad_w', 'grad_w_in': 'grad_w', 'grad_ret_norm_w': 'grad_w', 'grad_gla_gate_w': 'grad_w', 'grad_gla_gate_b': 'grad_w', 'grad_gla_norm_w': 'grad_w', 'grad_w_out': 'grad_w', 'grad_ln1_w': 'grad_w', 'grad_ln1_b': 'grad_w', 'grad_w_ff1': 'grad_w', 'grad_w_ff2': 'grad_w', 'grad_ln2_w': 'grad_w', 'grad_ln2_b': 'grad_w', 'delta_w_ada': 'delta_w', 'delta_b_ada': 'delta_w', 'delta_w_in': 'delta_w', 'delta_ret_norm_w': 'delta_w', 'delta_gla_gate_w': 'delta_w', 'delta_gla_gate_b': 'delta_w', 'delta_gla_norm_w': 'delta_w', 'delta_w_out': 'delta_w', 'delta_ln1_w': 'delta_w', 'delta_ln1_b': 'delta_w', 'delta_w_ff1': 'delta_w', 'delta_w_ff2': 'delta_w', 'delta_ln2_w': 'delta_w', 'delta_ln2_b': 'delta_w', 'new_m_w_ada': 'new_m', 'new_m_b_ada': 'new_m', 'new_m_w_in': 'new_m', 'new_m_ret_norm_w': 'new_m', 'new_m_gla_gate_w': 'new_m', 'new_m_gla_gate_b': 'new_m', 'new_m_gla_norm_w': 'new_m', 'new_m_w_out': 'new_m', 'new_m_ln1_w': 'new_m', 'new_m_ln1_b': 'new_m', 'new_m_w_ff1': 'new_m', 'new_m_w_ff2': 'new_m', 'new_m_ln2_w': 'new_m', 'new_m_ln2_b': 'new_m', 'new_v_w_ada': 'new_v', 'new_v_b_ada': 'new_v', 'new_v_w_in': 'new_v', 'new_v_ret_norm_w': 'new_v', 'new_v_gla_gate_w': 'new_v', 'new_v_gla_gate_b': 'new_v', 'new_v_gla_norm_w': 'new_v', 'new_v_w_out': 'new_v', 'new_v_ln1_w': 'new_v', 'new_v_ln1_b': 'new_v', 'new_v_w_ff1': 'new_v', 'new_v_w_ff2': 'new_v', 'new_v_ln2_w': 'new_v', 'new_v_ln2_b': 'new_v'}


def _forward(args):
    return _fwd_reference(*[args[k] for k in FWD_PARAMS])


def _output_shape():
    def fwd():
        inp = _fwd_setup_inputs(0)
        return _fwd_reference(*[inp[k] for k in FWD_PARAMS])
    out = _jax.eval_shape(fwd)
    return out.shape, out.dtype

N_MICROBATCH = 1
ADAM_LR = 0.001
ADAM_B1 = 0.9
ADAM_B2 = 0.999
ADAM_EPS = 1e-08
ADAM_WD = 0.01
ADAM_STEP = 10
PER_EXAMPLE_BATCH_AXIS = {'x': 0, 'c': 0, 'loss_target': 0}
SHARED_INPUTS = []
_WEIGHT_DTYPES = {'w_ada': _jnp.float32, 'b_ada': _jnp.float32, 'w_in': _jnp.float32, 'ret_norm_w': _jnp.float32, 'gla_gate_w': _jnp.float32, 'gla_gate_b': _jnp.float32, 'gla_norm_w': _jnp.float32, 'w_out': _jnp.float32, 'ln1_w': _jnp.float32, 'ln1_b': _jnp.float32, 'w_ff1': _jnp.float32, 'w_ff2': _jnp.float32, 'ln2_w': _jnp.float32, 'ln2_b': _jnp.float32}
MOMENT_SCALE = {'w_ada': 3.627607e-02, 'b_ada': 6.806094e-02, 'w_in': 3.369626e-02, 'ret_norm_w': 2.725505e-02, 'gla_gate_w': 6.799675e-03, 'gla_gate_b': 2.057941e-02, 'gla_norm_w': 2.746855e-02, 'w_out': 4.217841e-02, 'ln1_w': 2.269103e+00, 'ln1_b': 1.039753e+00, 'w_ff1': 1.916779e-02, 'w_ff2': 3.778148e-02, 'ln2_w': 6.404462e+01, 'ln2_b': 1.667511e+00}


def _to_microbatches(a, axis):
    t = _jnp.moveaxis(a, axis, 0)
    t = t.reshape((N_MICROBATCH, t.shape[0] // N_MICROBATCH) + t.shape[1:])
    return _jnp.moveaxis(t, 1, axis + 1)


def setup_inputs(seed: int = 0) -> dict:
    inp = _fwd_setup_inputs(seed)
    key = _jax.random.fold_in(_jax.random.key(seed), 7919)
    shape, _ = _output_shape()
    out = dict(inp)
    out["loss_target"] = _jax.random.normal(_jax.random.fold_in(key, 0), shape, _jnp.float32)
    for i, name in enumerate(TWIN_WEIGHTS):
        w = inp[name].astype(_jnp.float32)
        if MOMENT_SCALE is None:
            s = _jnp.sqrt(_jnp.mean(_jnp.square(w)) + 1e-30)
        else:
            s = MOMENT_SCALE[name]
        km, kv = _jax.random.split(_jax.random.fold_in(key, i + 1))
        out[name] = w
        out["m_" + name] = s * _jax.random.normal(km, w.shape, _jnp.float32)
        out["v_" + name] = (s * s) * _jax.random.uniform(kv, w.shape, _jnp.float32, 0.5, 1.5)
    if N_MICROBATCH > 1:
        for name, axis in PER_EXAMPLE_BATCH_AXIS.items():
            out[name] = _to_microbatches(out[name], axis)
    return {'x': out['x'], 'c': out['c'], 'w_ada': out['w_ada'], 'b_ada': out['b_ada'], 'w_in': out['w_in'], 'ret_norm_w': out['ret_norm_w'], 'gla_gate_w': out['gla_gate_w'], 'gla_gate_b': out['gla_gate_b'], 'gla_norm_w': out['gla_norm_w'], 'w_out': out['w_out'], 'ln1_w': out['ln1_w'], 'ln1_b': out['ln1_b'], 'w_ff1': out['w_ff1'], 'w_ff2': out['w_ff2'], 'ln2_w': out['ln2_w'], 'ln2_b': out['ln2_b'], 'loss_target': out['loss_target'], 'm_w_ada': out['m_w_ada'], 'm_b_ada': out['m_b_ada'], 'm_w_in': out['m_w_in'], 'm_ret_norm_w': out['m_ret_norm_w'], 'm_gla_gate_w': out['m_gla_gate_w'], 'm_gla_gate_b': out['m_gla_gate_b'], 'm_gla_norm_w': out['m_gla_norm_w'], 'm_w_out': out['m_w_out'], 'm_ln1_w': out['m_ln1_w'], 'm_ln1_b': out['m_ln1_b'], 'm_w_ff1': out['m_w_ff1'], 'm_w_ff2': out['m_w_ff2'], 'm_ln2_w': out['m_ln2_w'], 'm_ln2_b': out['m_ln2_b'], 'v_w_ada': out['v_w_ada'], 'v_b_ada': out['v_b_ada'], 'v_w_in': out['v_w_in'], 'v_ret_norm_w': out['v_ret_norm_w'], 'v_gla_gate_w': out['v_gla_gate_w'], 'v_gla_gate_b': out['v_gla_gate_b'], 'v_gla_norm_w': out['v_gla_norm_w'], 'v_w_out': out['v_w_out'], 'v_ln1_w': out['v_ln1_w'], 'v_ln1_b': out['v_ln1_b'], 'v_w_ff1': out['v_w_ff1'], 'v_w_ff2': out['v_w_ff2'], 'v_ln2_w': out['v_ln2_w'], 'v_ln2_b': out['v_ln2_b']}


def _loss(weights, diff, rest, loss_target):
    with _jax.named_scope("forward"):
        args = {**rest, TWIN_DIFF_INPUT: diff, **{k: w.astype(_WEIGHT_DTYPES[k]) for k, w in weights.items()}}
        y = _forward(args)
    with _jax.named_scope("loss_head"):
        err = _jnp.square(y.astype(_jnp.float32) - loss_target)
        return 0.5 * _jnp.sum(_jnp.mean(err, axis=-1)) if err.ndim else 0.5 * err


def _adamw(w, g, m, v):
    m = ADAM_B1 * m + (1.0 - ADAM_B1) * g
    v = ADAM_B2 * v + (1.0 - ADAM_B2) * _jnp.square(g)
    m_hat = m / (1.0 - ADAM_B1 ** ADAM_STEP)
    v_hat = v / (1.0 - ADAM_B2 ** ADAM_STEP)
    delta = -ADAM_LR * (m_hat / (_jnp.sqrt(v_hat) + ADAM_EPS) + ADAM_WD * w)
    return delta, m, v


def reference(x, c, w_ada, b_ada, w_in, ret_norm_w, gla_gate_w, gla_gate_b, gla_norm_w, w_out, ln1_w, ln1_b, w_ff1, w_ff2, ln2_w, ln2_b, loss_target, m_w_ada, m_b_ada, m_w_in, m_ret_norm_w, m_gla_gate_w, m_gla_gate_b, m_gla_norm_w, m_w_out, m_ln1_w, m_ln1_b, m_w_ff1, m_w_ff2, m_ln2_w, m_ln2_b, v_w_ada, v_b_ada, v_w_in, v_ret_norm_w, v_gla_gate_w, v_gla_gate_b, v_gla_norm_w, v_w_out, v_ln1_w, v_ln1_b, v_w_ff1, v_w_ff2, v_ln2_w, v_ln2_b):
    given = dict(x=x, c=c, w_ada=w_ada, b_ada=b_ada, w_in=w_in, ret_norm_w=ret_norm_w, gla_gate_w=gla_gate_w, gla_gate_b=gla_gate_b, gla_norm_w=gla_norm_w, w_out=w_out, ln1_w=ln1_w, ln1_b=ln1_b, w_ff1=w_ff1, w_ff2=w_ff2, ln2_w=ln2_w, ln2_b=ln2_b, loss_target=loss_target, m_w_ada=m_w_ada, m_b_ada=m_b_ada, m_w_in=m_w_in, m_ret_norm_w=m_ret_norm_w, m_gla_gate_w=m_gla_gate_w, m_gla_gate_b=m_gla_gate_b, m_gla_norm_w=m_gla_norm_w, m_w_out=m_w_out, m_ln1_w=m_ln1_w, m_ln1_b=m_ln1_b, m_w_ff1=m_w_ff1, m_w_ff2=m_w_ff2, m_ln2_w=m_ln2_w, m_ln2_b=m_ln2_b, v_w_ada=v_w_ada, v_b_ada=v_b_ada, v_w_in=v_w_in, v_ret_norm_w=v_ret_norm_w, v_gla_gate_w=v_gla_gate_w, v_gla_gate_b=v_gla_gate_b, v_gla_norm_w=v_gla_norm_w, v_w_out=v_w_out, v_ln1_w=v_ln1_w, v_ln1_b=v_ln1_b, v_w_ff1=v_w_ff1, v_w_ff2=v_w_ff2, v_ln2_w=v_ln2_w, v_ln2_b=v_ln2_b)
    weights = {n: given[n] for n in TWIN_WEIGHTS}
    shared = {n: given[n] for n in SHARED_INPUTS}
    per_example = {n: given[n] for n in ['x', 'c']}
    grad_fn = _jax.value_and_grad(_loss, argnums=(0, 1))

    def one_microbatch(ex, loss_target):
        ex = dict(ex)
        diff = ex.pop(TWIN_DIFF_INPUT)
        return grad_fn(weights, diff, {**shared, **ex}, loss_target)

    if N_MICROBATCH == 1:
        loss, (grad_w, grad_x) = one_microbatch(per_example, given["loss_target"])
    else:
        def body(carry, xs):
            loss_sum, grad_sum = carry
            l_k, (gw_k, gx_k) = one_microbatch(xs[0], xs[1])
            with _jax.named_scope("update"):
                return (loss_sum + l_k, _jax.tree.map(_jnp.add, grad_sum, gw_k)), gx_k

        init = (_jnp.zeros((), _jnp.float32), _jax.tree.map(_jnp.zeros_like, weights))
        (loss, grad_w), grad_x = _jax.lax.scan(body, init, (per_example, given["loss_target"]))
    with _jax.named_scope("update"):
        delta_w, new_m, new_v = {}, {}, {}
        for n in TWIN_WEIGHTS:
            delta_w[n], new_m[n], new_v[n] = _adamw(weights[n], grad_w[n], given["m_" + n], given["v_" + n])
    return (loss, grad_x, *[grad_w[n] for n in TWIN_WEIGHTS], *[delta_w[n] for n in TWIN_WEIGHTS],
            *[new_m[n] for n in TWIN_WEIGHTS], *[new_v[n] for n in TWIN_WEIGHTS])
```

```python
import functools

import numpy as np
import jax
import jax.numpy as jnp
from jax import lax
from jax.experimental import pallas as pl
from jax.experimental.pallas import tpu as pltpu

D_MODEL = 1024
D_FF = 4096
CHUNK = 64
N_HEADS = 4
HEAD_W = 128
GLA_DK = 64
GATE_RANK = 16
GATE_TAU = 16.0
LN_EPS = 1e-5
ALPHA = 2.0 ** 0.25
ROPE_BASE = 10000.0
RET_SCALE = float(HEAD_W) ** -0.5
GLA_SCALE = float(GLA_DK) ** -0.5

ADAM_LR = 0.001
ADAM_B1 = 0.9
ADAM_B2 = 0.999
ADAM_EPS = 1e-08
ADAM_WD = 0.01
ADAM_STEP = 10

OFF_RQ, OFF_RK, OFF_RV, OFF_RG = 0, 512, 1024, 1536
OFF_GQ, OFF_GK, OFF_GV, OFF_GG, OFF_LR = 2048, 2560, 3072, 3584, 4096
N_PROJ = 4224
N_PROJ_SRC = 3600

N_DEV = 8
N_CHIP = 4
MESH = pl.DeviceIdType.MESH
MXU_DTYPE = jnp.bfloat16
WIRE_DTYPE = jnp.bfloat16
VMEM_LIMIT = 60 * 1024 * 1024
TOKEN_TILE = 256
HIGHEST = lax.Precision.HIGHEST


def _mm(a, b):
    return jnp.dot(a.astype(MXU_DTYPE), b.astype(MXU_DTYPE), preferred_element_type=jnp.float32)


def _mm_nt(a, b):
    return lax.dot_general(a.astype(MXU_DTYPE), b.astype(MXU_DTYPE), (((1,), (1,)), ((), ())),
                           preferred_element_type=jnp.float32)


def _mm_tn(a, b):
    return lax.dot_general(a.astype(MXU_DTYPE), b.astype(MXU_DTYPE), (((0,), (0,)), ((), ())),
                           preferred_element_type=jnp.float32)


def _mm32(a, b):
    return jnp.dot(a, b, precision=HIGHEST, preferred_element_type=jnp.float32)


def _mm32_nt(a, b):
    return lax.dot_general(a, b, (((1,), (1,)), ((), ())), precision=HIGHEST, preferred_element_type=jnp.float32)


def _mm32_tn(a, b):
    return lax.dot_general(a, b, (((0,), (0,)), ((), ())), precision=HIGHEST, preferred_element_type=jnp.float32)


def _rowmean(a):
    return jnp.mean(a, axis=-1, keepdims=True)


def _colsum(a):
    return jnp.sum(a, axis=0, keepdims=True)


def _ln(z):
    zc = z - _rowmean(z)
    rstd = lax.rsqrt(_rowmean(zc * zc) + LN_EPS)
    return zc * rstd, rstd


def _ln_bwd(dzh, zh, rstd):
    return rstd * (dzh - _rowmean(dzh) - zh * _rowmean(dzh * zh))


def _sigmoid(a):
    return 1.0 / (1.0 + jnp.exp(-a))


def _log_sigmoid(a):
    return jnp.minimum(a, 0.0) - jnp.log(1.0 + jnp.exp(-jnp.abs(a)))


def _swap_halves(a):
    return pltpu.roll(a, HEAD_W // 2, 1)


def _tri_masks():
    row = lax.broadcasted_iota(jnp.int32, (CHUNK, CHUNK), 0)
    col = lax.broadcasted_iota(jnp.int32, (CHUNK, CHUNK), 1)
    return row, col


def _const_spec(shape):
    zeros = (0,) * len(shape)
    return pl.BlockSpec(shape, lambda *_: zeros, pipeline_mode=pl.Buffered(1))


def _params(semantics):
    return pltpu.CompilerParams(dimension_semantics=semantics, vmem_limit_bytes=VMEM_LIMIT)


def _decay_tables():
    log_gamma = np.log(1.0 - 2.0 ** (-5.0 - np.arange(N_HEADS, dtype=np.float64)))
    idx = np.arange(CHUNK, dtype=np.float64)
    dist = np.abs(idx[:, None] - idx[None, :])
    intra = np.exp(log_gamma[:, None, None] * dist)
    kdec = np.exp(log_gamma[None, :] * (CHUNK - 1.0 - idx)[:, None])
    qdec = np.exp(log_gamma[None, :] * (idx + 1.0)[:, None])
    chunk_decay = np.exp(log_gamma * CHUNK)
    lanes = lambda t: np.repeat(t, HEAD_W, axis=1).astype(np.float32)
    return (jnp.asarray(intra.astype(np.float32)), jnp.asarray(lanes(qdec)), jnp.asarray(lanes(kdec)),
            [float(np.float32(v)) for v in chunk_decay])


def _rotary_tables(seq):
    half = HEAD_W // 2
    inv = 1.0 / (ROPE_BASE ** jnp.linspace(0.0, 1.0, half, dtype=jnp.float32))
    ang = jnp.arange(seq, dtype=jnp.float32)[:, None] * inv[None, :]
    cos, sin = jnp.cos(ang), jnp.sin(ang)
    return jnp.concatenate([cos, cos], axis=1), jnp.concatenate([-sin, sin], axis=1)


def _mesh_pos():
    return lax.axis_index("x"), lax.axis_index("y"), lax.axis_index("c")


def _flip(v, bit):
    return 1 - v if bit else v


def _gather_rows(v, name):
    rows = v.shape[0]

    def body(v_ref, out_ref, send_sems, recv_sems):
        x, y, c = _mesh_pos()
        me = 4 * x + 2 * y + c
        out_ref[me] = v_ref[...]
        sends, recvs = [], []
        for k in range(1, N_DEV):
            px, py, pc = _flip(x, (k >> 2) & 1), _flip(y, (k >> 1) & 1), _flip(c, k & 1)
            peer = 4 * px + 2 * py + pc
            sends.append(pltpu.make_async_remote_copy(
                src_ref=v_ref, dst_ref=out_ref.at[me], send_sem=send_sems.at[k - 1], recv_sem=recv_sems.at[k - 1],
                device_id=(px, py, pc), device_id_type=MESH))
            recvs.append(pltpu.make_async_remote_copy(
                src_ref=v_ref, dst_ref=out_ref.at[peer], send_sem=send_sems.at[k - 1], recv_sem=recv_sems.at[k - 1],
                device_id=(px, py, pc), device_id_type=MESH))
        for cp in sends:
            cp.start()
        for cp in recvs:
            cp.wait_recv()
        for cp in sends:
            cp.wait_send()

    return pl.pallas_call(
        body, name=name,
        out_shape=jax.ShapeDtypeStruct((N_DEV, rows, 128), jnp.float32),
        in_specs=[pl.BlockSpec(memory_space=pltpu.VMEM)],
        out_specs=pl.BlockSpec(memory_space=pltpu.VMEM),
        scratch_shapes=[pltpu.SemaphoreType.DMA((N_DEV - 1,)), pltpu.SemaphoreType.DMA((N_DEV - 1,))],
    )(v)


def _chip_exchange(arrays, name, gather):
    n = len(arrays)

    def body(*refs):
        ins, outs = refs[:n], refs[n:2 * n]
        send_sems, recv_sems, local_sems = refs[2 * n:]
        x, y, c = _mesh_pos()
        chip = 2 * x + y
        local = []
        for i in range(n):
            src = ins[i] if gather else ins[i].at[chip]
            local.append(pltpu.make_async_copy(src, outs[i].at[chip], local_sems.at[i]))
            local[-1].start()
        sends, recvs = [], []
        for i in range(n):
            for k in range(1, N_CHIP):
                px, py = _flip(x, (k >> 1) & 1), _flip(y, k & 1)
                peer_chip = 2 * px + py
                sem = i * (N_CHIP - 1) + k - 1
                src = ins[i] if gather else ins[i].at[peer_chip]
                sends.append(pltpu.make_async_remote_copy(
                    src_ref=src, dst_ref=outs[i].at[chip], send_sem=send_sems.at[sem], recv_sem=recv_sems.at[sem],
                    device_id=(px, py, c), device_id_type=MESH))
                recvs.append(pltpu.make_async_remote_copy(
                    src_ref=src, dst_ref=outs[i].at[peer_chip], send_sem=send_sems.at[sem], recv_sem=recv_sems.at[sem],
                    device_id=(px, py, c), device_id_type=MESH))
        for cp in sends:
            cp.start()
        for cp in recvs:
            cp.wait_recv()
        for cp in sends:
            cp.wait_send()
        for cp in local:
            cp.wait()

    def out_struct(a):
        shape = (N_CHIP,) + a.shape if gather else a.shape
        return jax.ShapeDtypeStruct(shape, a.dtype)

    n_sem = n * (N_CHIP - 1)
    return pl.pallas_call(
        body, name=name,
        out_shape=tuple(out_struct(a) for a in arrays),
        in_specs=[pl.BlockSpec(memory_space=pl.ANY)] * n,
        out_specs=tuple(pl.BlockSpec(memory_space=pl.ANY) for _ in arrays),
        scratch_shapes=[pltpu.SemaphoreType.DMA((n_sem,)), pltpu.SemaphoreType.DMA((n_sem,)),
                        pltpu.SemaphoreType.DMA((n,))],
    )(*arrays)


def _sibling_swap(arrays, name):
    n = len(arrays)

    def body(*refs):
        ins, outs = refs[:n], refs[n:2 * n]
        send_sems, recv_sems = refs[2 * n:]
        x, y, c = _mesh_pos()
        copies = [pltpu.make_async_remote_copy(
            src_ref=ins[i], dst_ref=outs[i], send_sem=send_sems.at[i], recv_sem=recv_sems.at[i],
            device_id=(x, y, 1 - c), device_id_type=MESH) for i in range(n)]
        for cp in copies:
            cp.start()
        for cp in copies:
            cp.wait_recv()
        for cp in copies:
            cp.wait_send()

    return pl.pallas_call(
        body, name=name,
        out_shape=tuple(jax.ShapeDtypeStruct(a.shape, a.dtype) for a in arrays),
        in_specs=[pl.BlockSpec(memory_space=pl.ANY)] * n,
        out_specs=tuple(pl.BlockSpec(memory_space=pl.ANY) for _ in arrays),
        scratch_shapes=[pltpu.SemaphoreType.DMA((n,)), pltpu.SemaphoreType.DMA((n,))],
    )(*arrays)


def _ada_fwd(c_all, w_ada_blk, b_blk):
    cols = w_ada_blk.shape[1]

    def body(c_ref, w_ref, b_ref, out_ref):
        cv = c_ref[...]
        out_ref[...] = _mm32(cv * _sigmoid(cv), w_ref[...]) + b_ref[...]

    return pl.pallas_call(
        body, name="ada_fwd",
        out_shape=jax.ShapeDtypeStruct((N_DEV, cols), jnp.float32),
        compiler_params=pltpu.CompilerParams(vmem_limit_bytes=VMEM_LIMIT),
    )(c_all, w_ada_blk, b_blk)


def _adam(w, g, m, v):
    m2 = ADAM_B1 * m + (1.0 - ADAM_B1) * g
    v2 = ADAM_B2 * v + (1.0 - ADAM_B2) * (g * g)
    m_hat = m2 / (1.0 - ADAM_B1 ** ADAM_STEP)
    v_hat = v2 / (1.0 - ADAM_B2 ** ADAM_STEP)
    delta = -ADAM_LR * (m_hat / (jnp.sqrt(v_hat) + ADAM_EPS) + ADAM_WD * w)
    return delta, m2, v2


def _ada_bwd_adam(c_t, dmod_blk, w, m, v):
    rows, cols = w.shape
    tile = 512
    assert cols % tile == 0

    def body(c_ref, d_ref, w_ref, m_ref, v_ref, g_ref, dl_ref, m2_ref, v2_ref):
        sc = c_ref[...]
        sc = sc * _sigmoid(sc)
        dm = d_ref[...]
        g = sc[:, 0:1] * dm[0:1, :]
        for b in range(1, N_DEV):
            g = g + sc[:, b:b + 1] * dm[b:b + 1, :]
        delta, m2, v2 = _adam(w_ref[...], g, m_ref[...], v_ref[...])
        g_ref[...] = g
        dl_ref[...] = delta
        m2_ref[...] = m2
        v2_ref[...] = v2

    blk = pl.BlockSpec((rows, tile), lambda j: (0, j))
    out = jax.ShapeDtypeStruct((rows, cols), jnp.float32)
    return pl.pallas_call(
        body, name="ada_bwd_adam", grid=(cols // tile,),
        out_shape=(out, out, out, out),
        in_specs=[pl.BlockSpec((rows, N_DEV), lambda j: (0, 0)), pl.BlockSpec((N_DEV, tile), lambda j: (0, j)),
                  blk, blk, blk],
        out_specs=(blk, blk, blk, blk),
        compiler_params=_params(("arbitrary",)),
    )(c_t, dmod_blk, w, m, v)


def _inproj_fwd(x2, vecs, w_in_p, tm):
    seq = x2.shape[0]

    def body(x_ref, vec_ref, w_ref, p_ref, u_ref):
        xh, _ = _ln(x_ref[...])
        u = (xh * (1.0 + vec_ref[1:2, :]) + vec_ref[0:1, :]).astype(MXU_DTYPE)
        u_ref[...] = u
        p_ref[...] = _mm(u, w_ref[...])

    return pl.pallas_call(
        body, name="inproj_fwd", grid=(seq // tm,),
        out_shape=(jax.ShapeDtypeStruct((seq, N_PROJ), jnp.float32), jax.ShapeDtypeStruct((seq, D_MODEL), MXU_DTYPE)),
        in_specs=[pl.BlockSpec((tm, D_MODEL), lambda i: (i, 0)), _const_spec(vecs.shape), _const_spec(w_in_p.shape)],
        out_specs=(pl.BlockSpec((tm, N_PROJ), lambda i: (i, 0)), pl.BlockSpec((tm, D_MODEL), lambda i: (i, 0))),
        compiler_params=_params(("arbitrary",)),
    )(x2, vecs, w_in_p)


def _inproj_bwd(dproj, x2, dxa, vecs, w_in_p, tm):
    seq = x2.shape[0]

    def body(dp_ref, x_ref, dxa_ref, vec_ref, w_ref, gx_ref, sums_ref):
        @pl.when(pl.program_id(0) == 0)
        def _():
            sums_ref[...] = jnp.zeros_like(sums_ref)

        du = _mm_nt(dp_ref[...], w_ref[...])
        xh, rstd = _ln(x_ref[...])
        sums_ref[0:1, :] += _colsum(du)
        sums_ref[1:2, :] += _colsum(du * xh)
        gx_ref[...] = dxa_ref[...] + _ln_bwd(du * (1.0 + vec_ref[1:2, :]), xh, rstd)

    tile = pl.BlockSpec((tm, D_MODEL), lambda i: (i, 0))
    return pl.pallas_call(
        body, name="inproj_bwd", grid=(seq // tm,),
        out_shape=(jax.ShapeDtypeStruct((seq, D_MODEL), jnp.float32), jax.ShapeDtypeStruct((8, D_MODEL), jnp.float32)),
        in_specs=[pl.BlockSpec((tm, N_PROJ), lambda i: (i, 0)), tile, tile, _const_spec(vecs.shape),
                  _const_spec(w_in_p.shape)],
        out_specs=(tile, pl.BlockSpec((8, D_MODEL), lambda i: (0, 0))),
        compiler_params=_params(("arbitrary",)),
    )(dproj, x2, dxa, vecs, w_in_p)


def _head(h):
    return slice(h * HEAD_W, (h + 1) * HEAD_W)


def _ret_head_fwd(p_ref, h, cc, ss, dm_ref, qdec_ref, kdec_ref, state):
    hs = _head(h)
    q = p_ref[:, OFF_RQ + h * HEAD_W:OFF_RQ + (h + 1) * HEAD_W]
    k = p_ref[:, OFF_RK + h * HEAD_W:OFF_RK + (h + 1) * HEAD_W]
    v = p_ref[:, OFF_RV + h * HEAD_W:OFF_RV + (h + 1) * HEAD_W]
    qr = (q * cc + _swap_halves(q) * ss) * RET_SCALE
    kr = k * cc + _swap_halves(k) * ss
    scores = _mm_nt(qr, kr) * dm_ref[h]
    qd = qr * qdec_ref[:, hs]
    kd = kr * kdec_ref[:, hs]
    o = _mm(scores, v) + _mm(qd, state)
    return qr, kr, v, scores, qd, kd, o


def _gla_gates(p_ref, wg_ref, bg_ref):
    row, col = _tri_masks()
    glr = p_ref[:, OFF_LR:OFF_LR + HEAD_W]
    logit = _mm32(glr, wg_ref[...]) + bg_ref[...]
    la = _log_sigmoid(logit) * (1.0 / GATE_TAU)
    b = _mm32((row >= col).astype(jnp.float32), la)
    b_last = b[CHUNK - 1:CHUNK, :]
    b_mid = b[CHUNK // 2 - 1:CHUNK // 2, :]
    return dict(glr=glr, logit=logit, b=b, e=jnp.exp(b - b_mid), ei=jnp.exp(b_mid - b), eb=jnp.exp(b),
                ek=jnp.exp(b_last - b), ebl=jnp.exp(b_last))


def _gla_head_fwd(p_ref, h, gt, state_t):
    hs = _head(h)
    row, col = _tri_masks()
    q = p_ref[:, OFF_GQ + h * HEAD_W:OFF_GQ + (h + 1) * HEAD_W] * GLA_SCALE
    k = p_ref[:, OFF_GK + h * HEAD_W:OFF_GK + (h + 1) * HEAD_W]
    v = p_ref[:, OFF_GV + h * HEAD_W:OFF_GV + (h + 1) * HEAD_W]
    q_e, q_i = q * gt["e"][:, hs], q * gt["ei"][:, hs]
    k_e, k_i = k * gt["e"][:, hs], k * gt["ei"][:, hs]
    att = jnp.where(row >= col, _mm_nt(q_e, k_i), _mm_nt(q_i, k_e))
    qb = q * gt["eb"][:, hs]
    kb = k * gt["ek"][:, hs]
    o = _mm(att, v) + _mm_nt(qb, state_t)
    return q_e, q_i, k_e, k_i, v, att, qb, kb, o


def _mixer_fwd(proj, tables, wg_p, bg_p, ret_norm_w, gla_norm_w):
    seq = proj.shape[0]
    n_chunks = seq // CHUNK
    cc_t, ss_t, dm_t, qdec_t, kdec_t, chunk_decay = tables

    def body(p_ref, cc_ref, ss_ref, dm_ref, qdec_ref, kdec_ref, wg_ref, bg_ref, wr_ref, wl_ref,
             mix_ref, rsave_ref, ssave_ref, r_sc, s_sc):
        @pl.when(pl.program_id(0) == 0)
        def _():
            r_sc[...] = jnp.zeros_like(r_sc)
            s_sc[...] = jnp.zeros_like(s_sc)

        cc, ss = cc_ref[...], ss_ref[...]
        for h in range(N_HEADS):
            hs = _head(h)
            state = r_sc[h]
            rsave_ref[0, h] = state
            _, _, v, _, _, kd, o = _ret_head_fwd(p_ref, h, cc, ss, dm_ref, qdec_ref, kdec_ref, state)
            r_sc[h] = chunk_decay[h] * state + _mm_tn(kd, v)
            on, _ = _ln(o)
            g = p_ref[:, OFF_RG + h * HEAD_W:OFF_RG + (h + 1) * HEAD_W]
            mix_ref[:, hs] = (on * wr_ref[:, hs] * (g * _sigmoid(g))).astype(mix_ref.dtype)

        gt = _gla_gates(p_ref, wg_ref, bg_ref)
        for h in range(N_HEADS):
            hs = _head(h)
            state_t = s_sc[h]
            ssave_ref[0, h] = state_t
            _, _, _, _, v, _, _, kb, o = _gla_head_fwd(p_ref, h, gt, state_t)
            s_sc[h] = state_t * gt["ebl"][:, hs] + _mm_tn(v, kb)
            on = o * lax.rsqrt(_rowmean(o * o) + LN_EPS)
            g = p_ref[:, OFF_GG + h * HEAD_W:OFF_GG + (h + 1) * HEAD_W]
            mix_ref[:, N_HEADS * HEAD_W + h * HEAD_W:N_HEADS * HEAD_W + (h + 1) * HEAD_W] = (
                on * wl_ref[:, hs] * (g * _sigmoid(g))).astype(mix_ref.dtype)

    state_shape = (n_chunks, N_HEADS, HEAD_W, HEAD_W)
    state_blk = pl.BlockSpec((1, N_HEADS, HEAD_W, HEAD_W), lambda i: (i, 0, 0, 0))
    rot_blk = pl.BlockSpec((CHUNK, HEAD_W), lambda i: (i, 0))
    return pl.pallas_call(
        body, name="mixer_fwd", grid=(n_chunks,),
        out_shape=(jax.ShapeDtypeStruct((seq, D_MODEL), MXU_DTYPE),
                   jax.ShapeDtypeStruct(state_shape, jnp.float32), jax.ShapeDtypeStruct(state_shape, jnp.float32)),
        in_specs=[pl.BlockSpec((CHUNK, N_PROJ), lambda i: (i, 0)), rot_blk, rot_blk,
                  _const_spec(dm_t.shape), _const_spec(qdec_t.shape), _const_spec(kdec_t.shape),
                  _const_spec(wg_p.shape), _const_spec(bg_p.shape), _const_spec(ret_norm_w.shape),
                  _const_spec(gla_norm_w.shape)],
        out_specs=(pl.BlockSpec((CHUNK, D_MODEL), lambda i: (i, 0)), state_blk, state_blk),
        scratch_shapes=[pltpu.VMEM((N_HEADS, HEAD_W, HEAD_W), jnp.float32),
                        pltpu.VMEM((N_HEADS, HEAD_W, HEAD_W), jnp.float32)],
        compiler_params=_params(("arbitrary",)),
    )(proj, cc_t, ss_t, dm_t, qdec_t, kdec_t, wg_p, bg_p, ret_norm_w, gla_norm_w)


def _mixer_bwd(proj, dmixed, rsave, ssave, tables, wg_p, bg_p, ret_norm_w, gla_norm_w):
    seq = proj.shape[0]
    n_chunks = seq // CHUNK
    cc_t, ss_t, dm_t, qdec_t, kdec_t, chunk_decay = tables
    last = n_chunks - 1

    def body(p_ref, dmx_ref, rsave_ref, ssave_ref, cc_ref, ss_ref, dm_ref, qdec_ref, kdec_ref, wg_ref, bg_ref,
             wr_ref, wl_ref, dp_ref, dwr_ref, dwl_ref, dwg_ref, dbg_ref, dr_sc, ds_sc):
        @pl.when(pl.program_id(0) == 0)
        def _():
            dr_sc[...] = jnp.zeros_like(dr_sc)
            ds_sc[...] = jnp.zeros_like(ds_sc)
            dwr_ref[...] = jnp.zeros_like(dwr_ref)
            dwl_ref[...] = jnp.zeros_like(dwl_ref)
            dwg_ref[...] = jnp.zeros_like(dwg_ref)
            dbg_ref[...] = jnp.zeros_like(dbg_ref)

        cc, ss = cc_ref[...], ss_ref[...]
        row, col = _tri_masks()
        for h in range(N_HEADS):
            hs = _head(h)
            state = rsave_ref[0, h]
            qr, kr, v, scores, qd, kd, o = _ret_head_fwd(p_ref, h, cc, ss, dm_ref, qdec_ref, kdec_ref, state)
            on, rstd = _ln(o)
            g = p_ref[:, OFF_RG + h * HEAD_W:OFF_RG + (h + 1) * HEAD_W]
            sg = _sigmoid(g)
            dy = dmx_ref[:, hs].astype(jnp.float32)
            wr = wr_ref[:, hs]
            dwr_ref[:, hs] += _colsum(dy * on * (g * sg))
            dp_ref[:, OFF_RG + h * HEAD_W:OFF_RG + (h + 1) * HEAD_W] = dy * on * wr * (sg * (1.0 + g * (1.0 - sg)))
            do = _ln_bwd(dy * wr * (g * sg), on, rstd)
            d_state_new = dr_sc[h]
            ds_raw = _mm_nt(do, v) * dm_ref[h]
            dqr = _mm(ds_raw, kr) + _mm_nt(do, state) * qdec_ref[:, hs]
            dkr = _mm_tn(ds_raw, qr) + _mm_nt(v, d_state_new) * kdec_ref[:, hs]
            dp_ref[:, OFF_RV + h * HEAD_W:OFF_RV + (h + 1) * HEAD_W] = _mm_tn(scores, do) + _mm(kd, d_state_new)
            dr_sc[h] = chunk_decay[h] * d_state_new + _mm_tn(qd, do)
            dp_ref[:, OFF_RQ + h * HEAD_W:OFF_RQ + (h + 1) * HEAD_W] = (dqr * cc + _swap_halves(dqr * ss)) * RET_SCALE
            dp_ref[:, OFF_RK + h * HEAD_W:OFF_RK + (h + 1) * HEAD_W] = dkr * cc + _swap_halves(dkr * ss)

        gt = _gla_gates(p_ref, wg_ref, bg_ref)
        db_heads = []
        for h in range(N_HEADS):
            hs = _head(h)
            state_t = ssave_ref[0, h]
            q_e, q_i, k_e, k_i, v, att, qb, kb, o = _gla_head_fwd(p_ref, h, gt, state_t)
            rstd = lax.rsqrt(_rowmean(o * o) + LN_EPS)
            on = o * rstd
            g = p_ref[:, OFF_GG + h * HEAD_W:OFF_GG + (h + 1) * HEAD_W]
            sg = _sigmoid(g)
            dy = dmx_ref[:, N_HEADS * HEAD_W + h * HEAD_W:N_HEADS * HEAD_W + (h + 1) * HEAD_W].astype(jnp.float32)
            wl = wl_ref[:, hs]
            dwl_ref[:, hs] += _colsum(dy * on * (g * sg))
            dp_ref[:, OFF_GG + h * HEAD_W:OFF_GG + (h + 1) * HEAD_W] = dy * on * wl * (sg * (1.0 + g * (1.0 - sg)))
            don = dy * wl * (g * sg)
            do = rstd * (don - on * _rowmean(don * on))
            d_att = _mm_nt(do, v)
            d_low = jnp.where(row >= col, d_att, 0.0)
            d_up = jnp.where(row < col, d_att, 0.0)
            dq_e, dk_i = _mm(d_low, k_i), _mm_tn(d_low, q_e)
            dq_i, dk_e = _mm(d_up, k_e), _mm_tn(d_up, q_i)
            d_state_new = ds_sc[h]
            dqb = _mm(do, state_t)
            dkb = _mm(v, d_state_new)
            dp_ref[:, OFF_GV + h * HEAD_W:OFF_GV + (h + 1) * HEAD_W] = _mm_tn(att, do) + _mm_nt(kb, d_state_new)
            ebl = gt["ebl"][:, hs]
            ds_sc[h] = d_state_new * ebl + _mm_tn(do, qb)
            db_last = _colsum(dkb * kb) + ebl * _colsum(state_t * d_state_new)
            e, ei, eb, ek = gt["e"][:, hs], gt["ei"][:, hs], gt["eb"][:, hs], gt["ek"][:, hs]
            dp_ref[:, OFF_GQ + h * HEAD_W:OFF_GQ + (h + 1) * HEAD_W] = (dq_e * e + dq_i * ei + dqb * eb) * GLA_SCALE
            dp_ref[:, OFF_GK + h * HEAD_W:OFF_GK + (h + 1) * HEAD_W] = dk_e * e + dk_i * ei + dkb * ek
            db = dq_e * q_e - dq_i * q_i + dk_e * k_e - dk_i * k_i + dqb * qb - dkb * kb
            row_id = lax.broadcasted_iota(jnp.int32, (CHUNK, HEAD_W), 0)
            db_heads.append(db + jnp.where(row_id == CHUNK - 1, db_last, 0.0))
        db = jnp.concatenate(db_heads, axis=1)
        d_la = _mm32((col >= row).astype(jnp.float32), db)
        d_logit = d_la * (1.0 / GATE_TAU) * (1.0 - _sigmoid(gt["logit"]))
        dp_ref[:, OFF_LR:OFF_LR + HEAD_W] = _mm32_nt(d_logit, wg_ref[...])
        dwg_ref[...] += _mm32_tn(gt["glr"], d_logit)
        dbg_ref[...] += _colsum(d_logit)

    state_blk = pl.BlockSpec((1, N_HEADS, HEAD_W, HEAD_W), lambda i: (last - i, 0, 0, 0))
    rot_blk = pl.BlockSpec((CHUNK, HEAD_W), lambda i: (last - i, 0))
    width = N_HEADS * HEAD_W
    vec_out = pl.BlockSpec((1, width), lambda i: (0, 0))
    return pl.pallas_call(
        body, name="mixer_bwd", grid=(n_chunks,),
        out_shape=(jax.ShapeDtypeStruct((seq, N_PROJ), jnp.float32),
                   jax.ShapeDtypeStruct((1, width), jnp.float32), jax.ShapeDtypeStruct((1, width), jnp.float32),
                   jax.ShapeDtypeStruct((HEAD_W, width), jnp.float32), jax.ShapeDtypeStruct((1, width), jnp.float32)),
        in_specs=[pl.BlockSpec((CHUNK, N_PROJ), lambda i: (last - i, 0)),
                  pl.BlockSpec((CHUNK, D_MODEL), lambda i: (last - i, 0)), state_blk, state_blk, rot_blk, rot_blk,
                  _const_spec(dm_t.shape), _const_spec(qdec_t.shape), _const_spec(kdec_t.shape),
                  _const_spec(wg_p.shape), _const_spec(bg_p.shape), _const_spec(ret_norm_w.shape),
                  _const_spec(gla_norm_w.shape)],
        out_specs=(pl.BlockSpec((CHUNK, N_PROJ), lambda i: (last - i, 0)), vec_out, vec_out,
                   pl.BlockSpec((HEAD_W, width), lambda i: (0, 0)), vec_out),
        scratch_shapes=[pltpu.VMEM((N_HEADS, HEAD_W, HEAD_W), jnp.float32),
                        pltpu.VMEM((N_HEADS, HEAD_W, HEAD_W), jnp.float32)],
        compiler_params=_params(("arbitrary",)),
    )(proj, dmixed, rsave, ssave, cc_t, ss_t, dm_t, qdec_t, kdec_t, wg_p, bg_p, ret_norm_w, gla_norm_w)


V_GATE1, V_SCALE2, V_SHIFT2, V_GATE2, V_LN1W, V_LN1B, V_LN2W, V_LN2B = range(8)
S_GATE1, S_SCALE2, S_SHIFT2, S_GATE2, S_LN1W, S_LN1B, S_LN2W, S_LN2B, S_LOSS = range(9)


def _mlp_fwd_bwd(x2, mixed, target, vecs, w_out, w1_chunks, w2_chunks, tm):
    seq = x2.shape[0]
    n_fc, _, fc = w1_chunks.shape

    def body(x_ref, mx_ref, t_ref, vec_ref, wo_ref, w1_ref, w2_ref,
             dmx_ref, dxa_ref, a_ref, dh_ref, u2_ref, df_ref, dm_ref, sums_ref, relu_sc):
        @pl.when(pl.program_id(0) == 0)
        def _():
            sums_ref[...] = jnp.zeros_like(sums_ref)

        vec = lambda r: vec_ref[r:r + 1, :]

        def acc(r, val):
            sums_ref[r:r + 1, :] += _colsum(val)

        xx = x_ref[...]
        m = _mm(mx_ref[...], wo_ref[...])
        z1h, rstd1 = _ln(ALPHA * xx + vec(V_GATE1) * m)
        x1 = z1h * vec(V_LN1W) + vec(V_LN1B)
        x1h, rstd0 = _ln(x1)
        u2 = (x1h * (1.0 + vec(V_SCALE2)) + vec(V_SHIFT2)).astype(MXU_DTYPE)
        u2_ref[...] = u2
        f = jnp.zeros((tm, D_MODEL), jnp.float32)
        for j in range(n_fc):
            r = jnp.maximum(_mm(u2, w1_ref[j]), 0.0)
            relu_sc[:, j * fc:(j + 1) * fc] = r
            a = (r * r).astype(MXU_DTYPE)
            a_ref[:, j * fc:(j + 1) * fc] = a
            f = f + _mm(a, w2_ref[j])
        z2h, rstd2 = _ln(ALPHA * x1 + vec(V_GATE2) * f)
        err = z2h * vec(V_LN2W) + vec(V_LN2B) - t_ref[...]
        acc(S_LOSS, err * err)
        dy = err * (1.0 / D_MODEL)
        acc(S_LN2W, dy * z2h)
        acc(S_LN2B, dy)
        dz2 = _ln_bwd(dy * vec(V_LN2W), z2h, rstd2)
        acc(S_GATE2, dz2 * f)
        df = (vec(V_GATE2) * dz2).astype(MXU_DTYPE)
        df_ref[...] = df
        du2 = jnp.zeros((tm, D_MODEL), jnp.float32)
        for j in range(n_fc):
            dh = (_mm_nt(df, w2_ref[j]) * (2.0 * relu_sc[:, j * fc:(j + 1) * fc])).astype(MXU_DTYPE)
            dh_ref[:, j * fc:(j + 1) * fc] = dh
            du2 = du2 + _mm_nt(dh, w1_ref[j])
        acc(S_SCALE2, du2 * x1h)
        acc(S_SHIFT2, du2)
        dx1 = ALPHA * dz2 + _ln_bwd(du2 * (1.0 + vec(V_SCALE2)), x1h, rstd0)
        acc(S_LN1W, dx1 * z1h)
        acc(S_LN1B, dx1)
        dz1 = _ln_bwd(dx1 * vec(V_LN1W), z1h, rstd1)
        acc(S_GATE1, dz1 * m)
        dxa_ref[...] = ALPHA * dz1
        dm = (vec(V_GATE1) * dz1).astype(MXU_DTYPE)
        dm_ref[...] = dm
        dmx_ref[...] = _mm_nt(dm, wo_ref[...])

    tile = lambda width: pl.BlockSpec((tm, width), lambda i: (i, 0))
    f32 = lambda width: jax.ShapeDtypeStruct((seq, width), jnp.float32)
    b16 = lambda width: jax.ShapeDtypeStruct((seq, width), MXU_DTYPE)
    return pl.pallas_call(
        body, name="mlp_fwd_bwd", grid=(seq // tm,),
        out_shape=(f32(D_MODEL), f32(D_MODEL), b16(D_FF), b16(D_FF), b16(D_MODEL), b16(D_MODEL), b16(D_MODEL),
                   jax.ShapeDtypeStruct((16, D_MODEL), jnp.float32)),
        in_specs=[tile(D_MODEL), tile(D_MODEL), tile(D_MODEL), _const_spec(vecs.shape), _const_spec(w_out.shape),
                  _const_spec(w1_chunks.shape), _const_spec(w2_chunks.shape)],
        out_specs=(tile(D_MODEL), tile(D_MODEL), tile(D_FF), tile(D_FF), tile(D_MODEL), tile(D_MODEL),
                   tile(D_MODEL), pl.BlockSpec((16, D_MODEL), lambda i: (0, 0))),
        scratch_shapes=[pltpu.VMEM((tm, D_FF), jnp.float32)],
        compiler_params=_params(("arbitrary",)),
    )(x2, mixed, target, vecs, w_out, w1_chunks, w2_chunks)


def _grad_matmul(a, b, name, tn, blocks_are_rows):
    seq, m_dim = a.shape
    n_dim = b.shape[1]
    tk = min(seq, 512)
    nk = seq // tk
    if blocks_are_rows:
        tm = m_dim // N_CHIP
        assert tn == n_dim
        grid = (N_CHIP, 1, nk)
        out_map = lambda i, j, k: (i, 0, 0)
    else:
        tm = m_dim
        assert tn * N_CHIP == n_dim
        grid = (1, N_CHIP, nk)
        out_map = lambda i, j, k: (j, 0, 0)

    def body(a_ref, b_ref, o_ref, acc_sc):
        k = pl.program_id(2)

        @pl.when(k == 0)
        def _():
            acc_sc[...] = jnp.zeros_like(acc_sc)

        acc_sc[...] += _mm_tn(a_ref[...], b_ref[...])

        @pl.when(k == nk - 1)
        def _():
            o_ref[0] = acc_sc[...].astype(o_ref.dtype)

    return pl.pallas_call(
        body, name=name, grid=grid,
        out_shape=jax.ShapeDtypeStruct((N_CHIP, tm, tn), WIRE_DTYPE),
        in_specs=[pl.BlockSpec((tk, tm), lambda i, j, k: (k, i)), pl.BlockSpec((tk, tn), lambda i, j, k: (k, j))],
        out_specs=pl.BlockSpec((1, tm, tn), out_map),
        scratch_shapes=[pltpu.VMEM((tm, tn), jnp.float32)],
        compiler_params=_params(("arbitrary", "arbitrary", "arbitrary")),
    )(a, b)


def _grad_matmul_full(a, b, name, tn):
    seq, m_dim = a.shape
    n_dim = b.shape[1]
    tk = min(seq, 512)
    nk = seq // tk
    assert n_dim % tn == 0

    def body(a_ref, b_ref, o_ref):
        @pl.when(pl.program_id(1) == 0)
        def _():
            o_ref[...] = jnp.zeros_like(o_ref)

        o_ref[...] += _mm_tn(a_ref[...], b_ref[...])

    return pl.pallas_call(
        body, name=name, grid=(n_dim // tn, nk),
        out_shape=jax.ShapeDtypeStruct((m_dim, n_dim), jnp.float32),
        in_specs=[pl.BlockSpec((tk, m_dim), lambda j, k: (k, 0)), pl.BlockSpec((tk, tn), lambda j, k: (k, j))],
        out_specs=pl.BlockSpec((m_dim, tn), lambda j, k: (0, j)),
        compiler_params=_params(("arbitrary", "arbitrary")),
    )(a, b)


def _sum_chips(stack, name):
    _, rows, cols = stack.shape
    tr = min(rows, 256)

    def body(s_ref, o_ref):
        total = s_ref[0].astype(jnp.float32)
        for j in range(1, N_CHIP):
            total = total + s_ref[j].astype(jnp.float32)
        o_ref[...] = total

    return pl.pallas_call(
        body, name=name, grid=(rows // tr,),
        out_shape=jax.ShapeDtypeStruct((rows, cols), jnp.float32),
        in_specs=[pl.BlockSpec((N_CHIP, tr, cols), lambda i: (0, i, 0))],
        out_specs=pl.BlockSpec((tr, cols), lambda i: (i, 0)),
        compiler_params=_params(("arbitrary",)),
    )(stack)


def _adam_pair(w, g_mine, g_sibling, m, v, name):
    rows, cols = w.shape
    tr = min(rows, 256)

    def body(w_ref, ga_ref, gb_ref, m_ref, v_ref, g_ref, dl_ref, m2_ref, v2_ref):
        g = ga_ref[...] + gb_ref[...]
        delta, m2, v2 = _adam(w_ref[...], g, m_ref[...], v_ref[...])
        g_ref[...] = g
        dl_ref[...] = delta
        m2_ref[...] = m2
        v2_ref[...] = v2

    blk = pl.BlockSpec((tr, cols), lambda i: (i, 0))
    out = jax.ShapeDtypeStruct((rows, cols), jnp.float32)
    return pl.pallas_call(
        body, name=name, grid=(rows // tr,),
        out_shape=(out, out, out, out),
        in_specs=[blk] * 5, out_specs=(blk,) * 4,
        compiler_params=_params(("arbitrary",)),
    )(w, g_mine, g_sibling, m, v)


def _sum_devices(gathered):
    _, rows, _ = gathered.shape

    def body(g_ref, o_ref):
        total = g_ref[0]
        for d in range(1, N_DEV):
            total = total + g_ref[d]
        o_ref[...] = total

    return pl.pallas_call(
        body, name="sum_devices",
        out_shape=jax.ShapeDtypeStruct((rows, 128), jnp.float32),
    )(gathered)


def _adam_small(w, g, m, v):
    def body(w_ref, g_ref, m_ref, v_ref, dl_ref, m2_ref, v2_ref):
        delta, m2, v2 = _adam(w_ref[...], g_ref[...], m_ref[...], v_ref[...])
        dl_ref[...] = delta
        m2_ref[...] = m2
        v2_ref[...] = v2

    out = jax.ShapeDtypeStruct(w.shape, jnp.float32)
    return pl.pallas_call(body, name="adam_small", out_shape=(out, out, out))(w, g, m, v)


def _pad_heads(w):
    lead = w.shape[:-1]
    w = w.reshape(lead + (N_HEADS, GLA_DK))
    w = jnp.pad(w, [(0, 0)] * len(lead) + [(0, 0), (0, HEAD_W - GLA_DK)])
    return w.reshape(lead + (N_HEADS * HEAD_W,))


def _unpad_heads(w):
    lead = w.shape[:-1]
    return w.reshape(lead + (N_HEADS, HEAD_W))[..., :GLA_DK].reshape(lead + (N_HEADS * GLA_DK,))


def _pad_w_in(w):
    return jnp.concatenate([
        w[:, :2048], _pad_heads(w[:, 2048:2304]), _pad_heads(w[:, 2304:2560]), w[:, 2560:3584],
        jnp.pad(w[:, 3584:3600], ((0, 0), (0, HEAD_W - GATE_RANK)))], axis=1)


def _unpad_w_in(g):
    return jnp.concatenate([
        g[:, :2048], _unpad_heads(g[:, OFF_GQ:OFF_GQ + 512]), _unpad_heads(g[:, OFF_GK:OFF_GK + 512]),
        g[:, OFF_GV:OFF_LR], g[:, OFF_LR:OFF_LR + GATE_RANK]], axis=1)


def _rows128(a):
    return a.reshape(-1, 128)


def kernel(x, c, w_ada, b_ada, w_in, ret_norm_w, gla_gate_w, gla_gate_b, gla_norm_w, w_out, ln1_w, ln1_b, w_ff1, w_ff2, ln2_w, ln2_b, loss_target, m_w_ada, m_b_ada, m_w_in, m_ret_norm_w, m_gla_gate_w, m_gla_gate_b, m_gla_norm_w, m_w_out, m_ln1_w, m_ln1_b, m_w_ff1, m_w_ff2, m_ln2_w, m_ln2_b, v_w_ada, v_b_ada, v_w_in, v_ret_norm_w, v_gla_gate_w, v_gla_gate_b, v_gla_norm_w, v_w_out, v_ln1_w, v_ln1_b, v_w_ff1, v_w_ff2, v_ln2_w, v_ln2_b):
    seq = x.shape[1]
    tm = min(seq, TOKEN_TILE)
    xi, yi, ci = _mesh_pos()
    dev = 4 * xi + 2 * yi + ci
    chip = 2 * xi + yi
    x2, target = x[0], loss_target[0]
    ada_cols = w_ada.shape[2]
    in_cols = w_in.shape[2]
    gate_cols = gla_gate_w.shape[2]

    g0 = _gather_rows(jnp.concatenate([_rows128(c), _rows128(gla_gate_w[0])], axis=0), "gather_cond")
    c_all = g0[:, :8].reshape(N_DEV, D_MODEL)
    gate_w_full = jnp.concatenate([g0[2 * j, 8:16].reshape(GATE_RANK, gate_cols) for j in range(N_CHIP)], axis=1)
    wg_p = jnp.pad(_pad_heads(gate_w_full), ((0, HEAD_W - GATE_RANK), (0, 0)))
    bg_p = _pad_heads(gla_gate_b)

    b_blk = lax.dynamic_slice(b_ada, (0, chip * ada_cols), (1, ada_cols))
    mod_blk = _ada_fwd(c_all, w_ada[0], b_blk)
    g1 = _gather_rows(_rows128(mod_blk), "gather_mod")
    mod_all = jnp.concatenate([g1[2 * j].reshape(N_DEV, ada_cols) for j in range(N_CHIP)], axis=1)
    mod = lax.dynamic_slice(mod_all, (dev, 0), (1, 6 * D_MODEL))
    shift1, scale1, gate1, shift2, scale2, gate2 = [mod[:, i * D_MODEL:(i + 1) * D_MODEL] for i in range(6)]

    gathered = _chip_exchange([w_in[0].astype(WIRE_DTYPE), w_out[0].astype(WIRE_DTYPE),
                               w_ff1[0].astype(WIRE_DTYPE), w_ff2[0].astype(WIRE_DTYPE)], "gather_weights", True)
    w_in_p = _pad_w_in(jnp.transpose(gathered[0], (1, 0, 2)).reshape(D_MODEL, N_PROJ_SRC)).astype(MXU_DTYPE)
    w_out_full = gathered[1].reshape(D_MODEL, D_MODEL).astype(MXU_DTYPE)
    w1_chunks = gathered[2].astype(MXU_DTYPE)
    w2_chunks = gathered[3].astype(MXU_DTYPE)

    zeros_row = jnp.zeros((1, D_MODEL), jnp.float32)
    vecs1 = jnp.concatenate([shift1, scale1] + [zeros_row] * 6, axis=0)
    proj, u = _inproj_fwd(x2, vecs1, w_in_p, tm)
    cc_t, ss_t = _rotary_tables(seq)
    dm_t, qdec_t, kdec_t, chunk_decay = _decay_tables()
    tables = (cc_t, ss_t, dm_t, qdec_t, kdec_t, chunk_decay)
    mixed, rsave, ssave = _mixer_fwd(proj, tables, wg_p, bg_p, ret_norm_w, gla_norm_w)

    vecs2 = jnp.concatenate([gate1, scale2, shift2, gate2, ln1_w, ln1_b, ln2_w, ln2_b], axis=0)
    dmixed, dxa, act, dh, u2, df, dm, sums2 = _mlp_fwd_bwd(x2, mixed, target, vecs2, w_out_full, w1_chunks,
                                                           w2_chunks, tm)

    dproj, d_ret_norm, d_gla_norm, d_wg_p, d_bg_p = _mixer_bwd(proj, dmixed, rsave, ssave, tables, wg_p, bg_p,
                                                               ret_norm_w, gla_norm_w)
    grad_x, sums1 = _inproj_bwd(dproj, x2, dxa, vecs1, w_in_p, tm)

    g_in_full = _grad_matmul_full(u, dproj, "grad_w_in", 1408)
    g_in_stack = jnp.transpose(_unpad_w_in(g_in_full).reshape(D_MODEL, N_CHIP, in_cols), (1, 0, 2)).astype(WIRE_DTYPE)
    g_out_stack = _grad_matmul(mixed, dm, "grad_w_out", D_MODEL, True)
    g_ff1_stack = _grad_matmul(u2, dh, "grad_w_ff1", D_FF // N_CHIP, False)
    g_ff2_stack = _grad_matmul(act, df, "grad_w_ff2", D_MODEL, True)

    dmod = jnp.concatenate([sums1[0:1], sums1[1:2], sums2[S_GATE1:S_GATE1 + 1], sums2[S_SHIFT2:S_SHIFT2 + 1],
                            sums2[S_SCALE2:S_SCALE2 + 1], sums2[S_GATE2:S_GATE2 + 1]], axis=1)
    d_gate_w_full = _unpad_heads(d_wg_p[:GATE_RANK])
    small = jnp.concatenate([
        _rows128(dmod), _rows128(sums2[S_LN1W:S_LN1W + 1]), _rows128(sums2[S_LN1B:S_LN1B + 1]),
        _rows128(sums2[S_LN2W:S_LN2W + 1]), _rows128(sums2[S_LN2B:S_LN2B + 1]), _rows128(d_ret_norm),
        _rows128(_unpad_heads(d_bg_p)), _rows128(d_gla_norm), jnp.zeros((6, 128), jnp.float32),
        _rows128(d_gate_w_full), _rows128(sums2[S_LOSS:S_LOSS + 1])], axis=0)
    g2 = _gather_rows(small, "gather_small")
    tot = _sum_devices(g2)
    loss = 0.5 / D_MODEL * jnp.sum(tot[128:136])
    grad_b_ada = tot[0:48].reshape(1, 6 * D_MODEL)
    grad_ln1_w, grad_ln1_b = tot[48:56].reshape(1, D_MODEL), tot[56:64].reshape(1, D_MODEL)
    grad_ln2_w, grad_ln2_b = tot[64:72].reshape(1, D_MODEL), tot[72:80].reshape(1, D_MODEL)
    grad_ret_norm = tot[80:84].reshape(1, 512)
    grad_gate_b = tot[84:86].reshape(1, 256)
    grad_gla_norm = tot[86:90].reshape(1, 512)
    grad_gate_w = lax.dynamic_slice(tot[96:128].reshape(GATE_RANK, 256), (0, chip * gate_cols), (GATE_RANK, gate_cols))

    flat = lambda parts: jnp.concatenate([_rows128(p) for p in parts], axis=0)
    small_w = flat([b_ada, ln1_w, ln1_b, ln2_w, ln2_b, ret_norm_w, gla_gate_b, gla_norm_w, gla_gate_w[0]])
    small_g = flat([grad_b_ada, grad_ln1_w, grad_ln1_b, grad_ln2_w, grad_ln2_b, grad_ret_norm, grad_gate_b,
                    grad_gla_norm, grad_gate_w])
    small_m = flat([m_b_ada, m_ln1_w, m_ln1_b, m_ln2_w, m_ln2_b, m_ret_norm_w, m_gla_gate_b, m_gla_norm_w,
                    m_gla_gate_w[0]])
    small_v = flat([v_b_ada, v_ln1_w, v_ln1_b, v_ln2_w, v_ln2_b, v_ret_norm_w, v_gla_gate_b, v_gla_norm_w,
                    v_gla_gate_w[0]])
    small_out = _adam_small(small_w, small_g, small_m, small_v)

    def unflat(t):
        pieces, row = [], 0
        for shape in [(1, 6 * D_MODEL)] + [(1, D_MODEL)] * 4 + [(1, 512), (1, 256), (1, 512), (1, GATE_RANK, gate_cols)]:
            n = int(np.prod(shape)) // 128
            pieces.append(t[row:row + n].reshape(shape))
            row += n
        return pieces

    sm_delta, sm_m, sm_v = [unflat(t) for t in small_out]

    dmod_all = g2[:, 0:48].reshape(N_DEV, 6 * D_MODEL)
    dmod_blk = lax.dynamic_slice(dmod_all, (0, chip * ada_cols), (N_DEV, ada_cols))
    ada_out = _ada_bwd_adam(jnp.transpose(c_all), dmod_blk, w_ada[0], m_w_ada[0], v_w_ada[0])
    ada_g, ada_delta, ada_m, ada_v = [t[None] for t in ada_out]

    received = _chip_exchange([g_in_stack, g_out_stack, g_ff1_stack, g_ff2_stack], "scatter_grads", False)
    names = ["w_in", "w_out", "w_ff1", "w_ff2"]
    partial = [_sum_chips(r, "sum_" + n) for r, n in zip(received, names)]
    swapped = _sibling_swap(partial, "swap_partials")
    big = {}
    for n, w, mine, theirs, m, v in zip(names, [w_in, w_out, w_ff1, w_ff2], partial, swapped,
                                        [m_w_in, m_w_out, m_w_ff1, m_w_ff2], [v_w_in, v_w_out, v_w_ff1, v_w_ff2]):
        big[n] = [t[None] for t in _adam_pair(w[0], mine, theirs, m[0], v[0], "adam_" + n)]

    def assemble(ada, smalls, k):
        b_ada_o, ln1w_o, ln1b_o, ln2w_o, ln2b_o, ret_o, gb_o, gln_o, gw_o = smalls
        return [ada, b_ada_o, big["w_in"][k], ret_o, gw_o, gb_o, gln_o, big["w_out"][k], ln1w_o, ln1b_o,
                big["w_ff1"][k], big["w_ff2"][k], ln2w_o, ln2b_o]

    small_grads = [grad_b_ada, grad_ln1_w, grad_ln1_b, grad_ln2_w, grad_ln2_b, grad_ret_norm, grad_gate_b,
                   grad_gla_norm, grad_gate_w[None]]
    grads = assemble(ada_g, small_grads, 0)
    deltas = assemble(ada_delta, sm_delta, 1)
    new_m = assemble(ada_m, sm_m, 2)
    new_v = assemble(ada_v, sm_v, 3)
    return (loss, grad_x[None], *grads, *deltas, *new_m, *new_v)
```

```python
import functools

import numpy as np
import jax
import jax.numpy as jnp
from jax import lax
from jax.experimental import pallas as pl
from jax.experimental.pallas import tpu as pltpu

D_MODEL = 1024
D_FF = 4096
CHUNK = 64
N_HEADS = 4
HEAD_W = 128
GLA_DK = 64
GATE_RANK = 16
GATE_TAU = 16.0
LN_EPS = 1e-5
ALPHA = 2.0 ** 0.25
ROPE_BASE = 10000.0
RET_SCALE = float(HEAD_W) ** -0.5
GLA_SCALE = float(GLA_DK) ** -0.5

ADAM_LR = 0.001
ADAM_B1 = 0.9
ADAM_B2 = 0.999
ADAM_EPS = 1e-08
ADAM_WD = 0.01
ADAM_STEP = 10

OFF_RQ, OFF_RK, OFF_RV, OFF_RG = 0, 512, 1024, 1536
OFF_GQ, OFF_GK, OFF_GV, OFF_GG, OFF_LR = 2048, 2560, 3072, 3584, 4096
N_PROJ = 4224
N_PROJ_SRC = 3600

N_DEV = 8
N_CHIP = 4
MESH = pl.DeviceIdType.MESH
MXU_DTYPE = jnp.bfloat16
WIRE_DTYPE = jnp.bfloat16
VMEM_LIMIT = 60 * 1024 * 1024
TOKEN_TILE = 256
HIGHEST = lax.Precision.HIGHEST


def _mm(a, b):
    return jnp.dot(a.astype(MXU_DTYPE), b.astype(MXU_DTYPE), preferred_element_type=jnp.float32)


def _mm_nt(a, b):
    return lax.dot_general(a.astype(MXU_DTYPE), b.astype(MXU_DTYPE), (((1,), (1,)), ((), ())),
                           preferred_element_type=jnp.float32)


def _mm_tn(a, b):
    return lax.dot_general(a.astype(MXU_DTYPE), b.astype(MXU_DTYPE), (((0,), (0,)), ((), ())),
                           preferred_element_type=jnp.float32)


def _mm32(a, b):
    return jnp.dot(a, b, precision=HIGHEST, preferred_element_type=jnp.float32)


def _mm32_nt(a, b):
    return lax.dot_general(a, b, (((1,), (1,)), ((), ())), precision=HIGHEST, preferred_element_type=jnp.float32)


def _mm32_tn(a, b):
    return lax.dot_general(a, b, (((0,), (0,)), ((), ())), precision=HIGHEST, preferred_element_type=jnp.float32)


def _rowmean(a):
    return jnp.mean(a, axis=-1, keepdims=True)


def _colsum(a):
    return jnp.sum(a, axis=0, keepdims=True)


def _ln(z):
    zc = z - _rowmean(z)
    rstd = lax.rsqrt(_rowmean(zc * zc) + LN_EPS)
    return zc * rstd, rstd


def _ln_bwd(dzh, zh, rstd):
    return rstd * (dzh - _rowmean(dzh) - zh * _rowmean(dzh * zh))


def _sigmoid(a):
    return 1.0 / (1.0 + jnp.exp(-a))


def _log_sigmoid(a):
    return jnp.minimum(a, 0.0) - jnp.log(1.0 + jnp.exp(-jnp.abs(a)))


def _swap_halves(a):
    return pltpu.roll(a, HEAD_W // 2, 1)


def _tri_masks():
    row = lax.broadcasted_iota(jnp.int32, (CHUNK, CHUNK), 0)
    col = lax.broadcasted_iota(jnp.int32, (CHUNK, CHUNK), 1)
    return row, col


def _const_spec(shape):
    zeros = (0,) * len(shape)
    return pl.BlockSpec(shape, lambda *_: zeros, pipeline_mode=pl.Buffered(1))


def _params(semantics):
    return pltpu.CompilerParams(dimension_semantics=semantics, vmem_limit_bytes=VMEM_LIMIT)


def _decay_tables():
    log_gamma = np.log(1.0 - 2.0 ** (-5.0 - np.arange(N_HEADS, dtype=np.float64)))
    idx = np.arange(CHUNK, dtype=np.float64)
    dist = np.abs(idx[:, None] - idx[None, :])
    intra = np.exp(log_gamma[:, None, None] * dist)
    kdec = np.exp(log_gamma[None, :] * (CHUNK - 1.0 - idx)[:, None])
    qdec = np.exp(log_gamma[None, :] * (idx + 1.0)[:, None])
    chunk_decay = np.exp(log_gamma * CHUNK)
    lanes = lambda t: np.repeat(t, HEAD_W, axis=1).astype(np.float32)
    return (jnp.asarray(intra.astype(np.float32)), jnp.asarray(lanes(qdec)), jnp.asarray(lanes(kdec)),
            [float(np.float32(v)) for v in chunk_decay])


def _rotary_tables(seq):
    half = HEAD_W // 2
    inv = 1.0 / (ROPE_BASE ** jnp.linspace(0.0, 1.0, half, dtype=jnp.float32))
    ang = jnp.arange(seq, dtype=jnp.float32)[:, None] * inv[None, :]
    cos, sin = jnp.cos(ang), jnp.sin(ang)
    return jnp.concatenate([cos, cos], axis=1), jnp.concatenate([-sin, sin], axis=1)


def _mesh_pos():
    return lax.axis_index("x"), lax.axis_index("y"), lax.axis_index("c")


def _flip(v, bit):
    return 1 - v if bit else v


def _gather_rows(v, name):
    rows = v.shape[0]

    def body(v_ref, out_ref, send_sems, recv_sems):
        x, y, c = _mesh_pos()
        me = 4 * x + 2 * y + c
        out_ref[me] = v_ref[...]
        sends, recvs = [], []
        for k in range(1, N_DEV):
            px, py, pc = _flip(x, (k >> 2) & 1), _flip(y, (k >> 1) & 1), _flip(c, k & 1)
            peer = 4 * px + 2 * py + pc
            sends.append(pltpu.make_async_remote_copy(
                src_ref=v_ref, dst_ref=out_ref.at[me], send_sem=send_sems.at[k - 1], recv_sem=recv_sems.at[k - 1],
                device_id=(px, py, pc), device_id_type=MESH))
            recvs.append(pltpu.make_async_remote_copy(
                src_ref=v_ref, dst_ref=out_ref.at[peer], send_sem=send_sems.at[k - 1], recv_sem=recv_sems.at[k - 1],
                device_id=(px, py, pc), device_id_type=MESH))
        for cp in sends:
            cp.start()
        for cp in recvs:
            cp.wait_recv()
        for cp in sends:
            cp.wait_send()

    return pl.pallas_call(
        body, name=name,
        out_shape=jax.ShapeDtypeStruct((N_DEV, rows, 128), jnp.float32),
        in_specs=[pl.BlockSpec(memory_space=pltpu.VMEM)],
        out_specs=pl.BlockSpec(memory_space=pltpu.VMEM),
        scratch_shapes=[pltpu.SemaphoreType.DMA((N_DEV - 1,)), pltpu.SemaphoreType.DMA((N_DEV - 1,))],
    )(v)


def _chip_exchange(arrays, name, gather):
    n = len(arrays)

    def body(*refs):
        ins, outs = refs[:n], refs[n:2 * n]
        send_sems, recv_sems, local_sems = refs[2 * n:]
        x, y, c = _mesh_pos()
        chip = 2 * x + y
        local = []
        for i in range(n):
            src = ins[i] if gather else ins[i].at[chip]
            local.append(pltpu.make_async_copy(src, outs[i].at[chip], local_sems.at[i]))
            local[-1].start()
        sends, recvs = [], []
        for i in range(n):
            for k in range(1, N_CHIP):
                px, py = _flip(x, (k >> 1) & 1), _flip(y, k & 1)
                peer_chip = 2 * px + py
                sem = i * (N_CHIP - 1) + k - 1
                src = ins[i] if gather else ins[i].at[peer_chip]
                sends.append(pltpu.make_async_remote_copy(
                    src_ref=src, dst_ref=outs[i].at[chip], send_sem=send_sems.at[sem], recv_sem=recv_sems.at[sem],
                    device_id=(px, py, c), device_id_type=MESH))
                recvs.append(pltpu.make_async_remote_copy(
                    src_ref=src, dst_ref=outs[i].at[peer_chip], send_sem=send_sems.at[sem], recv_sem=recv_sems.at[sem],
                    device_id=(px, py, c), device_id_type=MESH))
        for cp in sends:
            cp.start()
        for cp in recvs:
            cp.wait_recv()
        for cp in sends:
            cp.wait_send()
        for cp in local:
            cp.wait()

    def out_struct(a):
        shape = (N_CHIP,) + a.shape if gather else a.shape
        return jax.ShapeDtypeStruct(shape, a.dtype)

    n_sem = n * (N_CHIP - 1)
    return pl.pallas_call(
        body, name=name,
        out_shape=tuple(out_struct(a) for a in arrays),
        in_specs=[pl.BlockSpec(memory_space=pl.ANY)] * n,
        out_specs=tuple(pl.BlockSpec(memory_space=pl.ANY) for _ in arrays),
        scratch_shapes=[pltpu.SemaphoreType.DMA((n_sem,)), pltpu.SemaphoreType.DMA((n_sem,)),
                        pltpu.SemaphoreType.DMA((n,))],
    )(*arrays)


def _sibling_swap(arrays, name):
    n = len(arrays)

    def body(*refs):
        ins, outs = refs[:n], refs[n:2 * n]
        send_sems, recv_sems = refs[2 * n:]
        x, y, c = _mesh_pos()
        copies = [pltpu.make_async_remote_copy(
            src_ref=ins[i], dst_ref=outs[i], send_sem=send_sems.at[i], recv_sem=recv_sems.at[i],
            device_id=(x, y, 1 - c), device_id_type=MESH) for i in range(n)]
        for cp in copies:
            cp.start()
        for cp in copies:
            cp.wait_recv()
        for cp in copies:
            cp.wait_send()

    return pl.pallas_call(
        body, name=name,
        out_shape=tuple(jax.ShapeDtypeStruct(a.shape, a.dtype) for a in arrays),
        in_specs=[pl.BlockSpec(memory_space=pl.ANY)] * n,
        out_specs=tuple(pl.BlockSpec(memory_space=pl.ANY) for _ in arrays),
        scratch_shapes=[pltpu.SemaphoreType.DMA((n,)), pltpu.SemaphoreType.DMA((n,))],
    )(*arrays)


def _ada_fwd(c_all, w_ada_blk, b_blk):
    cols = w_ada_blk.shape[1]

    def body(c_ref, w_ref, b_ref, out_ref):
        cv = c_ref[...]
        out_ref[...] = _mm32(cv * _sigmoid(cv), w_ref[...]) + b_ref[...]

    return pl.pallas_call(
        body, name="ada_fwd",
        out_shape=jax.ShapeDtypeStruct((N_DEV, cols), jnp.float32),
        compiler_params=pltpu.CompilerParams(vmem_limit_bytes=VMEM_LIMIT),
    )(c_all, w_ada_blk, b_blk)


def _adam(w, g, m, v):
    m2 = ADAM_B1 * m + (1.0 - ADAM_B1) * g
    v2 = ADAM_B2 * v + (1.0 - ADAM_B2) * (g * g)
    m_hat = m2 / (1.0 - ADAM_B1 ** ADAM_STEP)
    v_hat = v2 / (1.0 - ADAM_B2 ** ADAM_STEP)
    delta = -ADAM_LR * (m_hat / (jnp.sqrt(v_hat) + ADAM_EPS) + ADAM_WD * w)
    return delta, m2, v2


def _ada_bwd_adam(c_t, dmod_blk, w, m, v):
    rows, cols = w.shape
    tile = 512
    assert cols % tile == 0

    def body(c_ref, d_ref, w_ref, m_ref, v_ref, g_ref, dl_ref, m2_ref, v2_ref):
        sc = c_ref[...]
        sc = sc * _sigmoid(sc)
        dm = d_ref[...]
        g = sc[:, 0:1] * dm[0:1, :]
        for b in range(1, N_DEV):
            g = g + sc[:, b:b + 1] * dm[b:b + 1, :]
        delta, m2, v2 = _adam(w_ref[...], g, m_ref[...], v_ref[...])
        g_ref[...] = g
        dl_ref[...] = delta
        m2_ref[...] = m2
        v2_ref[...] = v2

    blk = pl.BlockSpec((rows, tile), lambda j: (0, j))
    out = jax.ShapeDtypeStruct((rows, cols), jnp.float32)
    return pl.pallas_call(
        body, name="ada_bwd_adam", grid=(cols // tile,),
        out_shape=(out, out, out, out),
        in_specs=[pl.BlockSpec((rows, N_DEV), lambda j: (0, 0)), pl.BlockSpec((N_DEV, tile), lambda j: (0, j)),
                  blk, blk, blk],
        out_specs=(blk, blk, blk, blk),
        compiler_params=_params(("arbitrary",)),
    )(c_t, dmod_blk, w, m, v)


def _inproj_fwd(x2, vecs, w_in_p, tm):
    seq = x2.shape[0]

    def body(x_ref, vec_ref, w_ref, p_ref, u_ref):
        xh, _ = _ln(x_ref[...])
        u = (xh * (1.0 + vec_ref[1:2, :]) + vec_ref[0:1, :]).astype(MXU_DTYPE)
        u_ref[...] = u
        p_ref[...] = _mm(u, w_ref[...])

    return pl.pallas_call(
        body, name="inproj_fwd", grid=(seq // tm,),
        out_shape=(jax.ShapeDtypeStruct((seq, N_PROJ), jnp.float32), jax.ShapeDtypeStruct((seq, D_MODEL), MXU_DTYPE)),
        in_specs=[pl.BlockSpec((tm, D_MODEL), lambda i: (i, 0)), _const_spec(vecs.shape), _const_spec(w_in_p.shape)],
        out_specs=(pl.BlockSpec((tm, N_PROJ), lambda i: (i, 0)), pl.BlockSpec((tm, D_MODEL), lambda i: (i, 0))),
        compiler_params=_params(("arbitrary",)),
    )(x2, vecs, w_in_p)


def _inproj_bwd(dproj, x2, dxa, vecs, w_in_p, tm):
    seq = x2.shape[0]

    def body(dp_ref, x_ref, dxa_ref, vec_ref, w_ref, gx_ref, sums_ref):
        @pl.when(pl.program_id(0) == 0)
        def _():
            sums_ref[...] = jnp.zeros_like(sums_ref)

        du = _mm_nt(dp_ref[...], w_ref[...])
        xh, rstd = _ln(x_ref[...])
        sums_ref[0:1, :] += _colsum(du)
        sums_ref[1:2, :] += _colsum(du * xh)
        gx_ref[...] = dxa_ref[...] + _ln_bwd(du * (1.0 + vec_ref[1:2, :]), xh, rstd)

    tile = pl.BlockSpec((tm, D_MODEL), lambda i: (i, 0))
    return pl.pallas_call(
        body, name="inproj_bwd", grid=(seq // tm,),
        out_shape=(jax.ShapeDtypeStruct((seq, D_MODEL), jnp.float32), jax.ShapeDtypeStruct((8, D_MODEL), jnp.float32)),
        in_specs=[pl.BlockSpec((tm, N_PROJ), lambda i: (i, 0)), tile, tile, _const_spec(vecs.shape),
                  _const_spec(w_in_p.shape)],
        out_specs=(tile, pl.BlockSpec((8, D_MODEL), lambda i: (0, 0))),
        compiler_params=_params(("arbitrary",)),
    )(dproj, x2, dxa, vecs, w_in_p)


def _head(h):
    return slice(h * HEAD_W, (h + 1) * HEAD_W)


def _cols(ref, off, h):
    return ref[:, off + h * HEAD_W:off + (h + 1) * HEAD_W]


HEADS = range(N_HEADS)


def _mixer_chunk_forward(p_ref, cc, ss, dm_ref, qdec_ref, kdec_ref, wg_ref, bg_ref, ret_state, gla_state_t):
    row, col = _tri_masks()
    lower = row >= col
    f = {}
    f["glr"] = p_ref[:, OFF_LR:OFF_LR + HEAD_W]
    f["logit"] = _mm32(f["glr"], wg_ref[...]) + bg_ref[...]
    rq = [_cols(p_ref, OFF_RQ, h) for h in HEADS]
    rk = [_cols(p_ref, OFF_RK, h) for h in HEADS]
    f["rv"] = [_cols(p_ref, OFF_RV, h) for h in HEADS]
    f["qr"] = [(rq[h] * cc + _swap_halves(rq[h]) * ss) * RET_SCALE for h in HEADS]
    f["kr"] = [rk[h] * cc + _swap_halves(rk[h]) * ss for h in HEADS]
    s_raw = [_mm_nt(f["qr"][h], f["kr"][h]) for h in HEADS]
    la = _log_sigmoid(f["logit"]) * (1.0 / GATE_TAU)
    b = _mm32(lower.astype(jnp.float32), la)
    f["qd"] = [f["qr"][h] * qdec_ref[:, _head(h)] for h in HEADS]
    f["kd"] = [f["kr"][h] * kdec_ref[:, _head(h)] for h in HEADS]
    f["scores"] = [s_raw[h] * dm_ref[h] for h in HEADS]
    f["o_ret"] = [_mm(f["scores"][h], f["rv"][h]) + _mm(f["qd"][h], ret_state[h]) for h in HEADS]
    b_last = b[CHUNK - 1:CHUNK, :]
    b_mid = b[CHUNK // 2 - 1:CHUNK // 2, :]
    f["e"], f["ei"] = jnp.exp(b - b_mid), jnp.exp(b_mid - b)
    f["eb"], f["ek"], f["ebl"] = jnp.exp(b), jnp.exp(b_last - b), jnp.exp(b_last)
    gq = [_cols(p_ref, OFF_GQ, h) * GLA_SCALE for h in HEADS]
    gk = [_cols(p_ref, OFF_GK, h) for h in HEADS]
    f["gv"] = [_cols(p_ref, OFF_GV, h) for h in HEADS]
    f["q_e"] = [gq[h] * f["e"][:, _head(h)] for h in HEADS]
    f["q_i"] = [gq[h] * f["ei"][:, _head(h)] for h in HEADS]
    f["k_e"] = [gk[h] * f["e"][:, _head(h)] for h in HEADS]
    f["k_i"] = [gk[h] * f["ei"][:, _head(h)] for h in HEADS]
    low = [_mm_nt(f["q_e"][h], f["k_i"][h]) for h in HEADS]
    up = [_mm_nt(f["q_i"][h], f["k_e"][h]) for h in HEADS]
    f["att"] = [jnp.where(lower, low[h], up[h]) for h in HEADS]
    f["qb"] = [gq[h] * f["eb"][:, _head(h)] for h in HEADS]
    f["kb"] = [gk[h] * f["ek"][:, _head(h)] for h in HEADS]
    f["o_gla"] = [_mm(f["att"][h], f["gv"][h]) + _mm_nt(f["qb"][h], gla_state_t[h]) for h in HEADS]
    return f


def _mixer_fwd(proj, tables, wg_p, bg_p, ret_norm_w, gla_norm_w):
    seq = proj.shape[0]
    n_chunks = seq // CHUNK
    cc_t, ss_t, dm_t, qdec_t, kdec_t, chunk_decay = tables

    def body(p_ref, cc_ref, ss_ref, dm_ref, qdec_ref, kdec_ref, wg_ref, bg_ref, wr_ref, wl_ref,
             mix_ref, rsave_ref, ssave_ref, r_sc, s_sc):
        @pl.when(pl.program_id(0) == 0)
        def _():
            r_sc[...] = jnp.zeros_like(r_sc)
            s_sc[...] = jnp.zeros_like(s_sc)

        ret_state = [r_sc[h] for h in HEADS]
        gla_state_t = [s_sc[h] for h in HEADS]
        for h in HEADS:
            rsave_ref[0, h] = ret_state[h]
            ssave_ref[0, h] = gla_state_t[h]
        f = _mixer_chunk_forward(p_ref, cc_ref[...], ss_ref[...], dm_ref, qdec_ref, kdec_ref, wg_ref, bg_ref,
                                 ret_state, gla_state_t)
        for h in HEADS:
            r_sc[h] = chunk_decay[h] * ret_state[h] + _mm_tn(f["kd"][h], f["rv"][h])
        for h in HEADS:
            s_sc[h] = gla_state_t[h] * f["ebl"][:, _head(h)] + _mm_tn(f["gv"][h], f["kb"][h])
        for h in HEADS:
            on, _ = _ln(f["o_ret"][h])
            g = _cols(p_ref, OFF_RG, h)
            mix_ref[:, _head(h)] = (on * wr_ref[:, _head(h)] * (g * _sigmoid(g))).astype(mix_ref.dtype)
        for h in HEADS:
            o = f["o_gla"][h]
            on = o * lax.rsqrt(_rowmean(o * o) + LN_EPS)
            g = _cols(p_ref, OFF_GG, h)
            mix_ref[:, _head(N_HEADS + h)] = (on * wl_ref[:, _head(h)] * (g * _sigmoid(g))).astype(mix_ref.dtype)

    state_shape = (n_chunks, N_HEADS, HEAD_W, HEAD_W)
    state_blk = pl.BlockSpec((1, N_HEADS, HEAD_W, HEAD_W), lambda i: (i, 0, 0, 0))
    rot_blk = pl.BlockSpec((CHUNK, HEAD_W), lambda i: (i, 0))
    return pl.pallas_call(
        body, name="mixer_fwd", grid=(n_chunks,),
        out_shape=(jax.ShapeDtypeStruct((seq, D_MODEL), MXU_DTYPE),
                   jax.ShapeDtypeStruct(state_shape, jnp.float32), jax.ShapeDtypeStruct(state_shape, jnp.float32)),
        in_specs=[pl.BlockSpec((CHUNK, N_PROJ), lambda i: (i, 0)), rot_blk, rot_blk,
                  _const_spec(dm_t.shape), _const_spec(qdec_t.shape), _const_spec(kdec_t.shape),
                  _const_spec(wg_p.shape), _const_spec(bg_p.shape), _const_spec(ret_norm_w.shape),
                  _const_spec(gla_norm_w.shape)],
        out_specs=(pl.BlockSpec((CHUNK, D_MODEL), lambda i: (i, 0)), state_blk, state_blk),
        scratch_shapes=[pltpu.VMEM((N_HEADS, HEAD_W, HEAD_W), jnp.float32),
                        pltpu.VMEM((N_HEADS, HEAD_W, HEAD_W), jnp.float32)],
        compiler_params=_params(("arbitrary",)),
    )(proj, cc_t, ss_t, dm_t, qdec_t, kdec_t, wg_p, bg_p, ret_norm_w, gla_norm_w)


def _mixer_bwd(proj, dmixed, rsave, ssave, tables, wg_p, bg_p, ret_norm_w, gla_norm_w):
    seq = proj.shape[0]
    n_chunks = seq // CHUNK
    cc_t, ss_t, dm_t, qdec_t, kdec_t, chunk_decay = tables
    last = n_chunks - 1

    def body(p_ref, dmx_ref, rsave_ref, ssave_ref, cc_ref, ss_ref, dm_ref, qdec_ref, kdec_ref, wg_ref, bg_ref,
             wr_ref, wl_ref, dp_ref, dwr_ref, dwl_ref, dwg_ref, dbg_ref, dr_sc, ds_sc):
        @pl.when(pl.program_id(0) == 0)
        def _():
            dr_sc[...] = jnp.zeros_like(dr_sc)
            ds_sc[...] = jnp.zeros_like(ds_sc)
            dwr_ref[...] = jnp.zeros_like(dwr_ref)
            dwl_ref[...] = jnp.zeros_like(dwl_ref)
            dwg_ref[...] = jnp.zeros_like(dwg_ref)
            dbg_ref[...] = jnp.zeros_like(dbg_ref)

        cc, ss = cc_ref[...], ss_ref[...]
        row, col = _tri_masks()
        ret_state = [rsave_ref[0, h] for h in HEADS]
        gla_state_t = [ssave_ref[0, h] for h in HEADS]
        d_ret_new = [dr_sc[h] for h in HEADS]
        d_gla_new = [ds_sc[h] for h in HEADS]
        f = _mixer_chunk_forward(p_ref, cc, ss, dm_ref, qdec_ref, kdec_ref, wg_ref, bg_ref, ret_state, gla_state_t)

        do_ret, do_gla = [], []
        for h in HEADS:
            on, rstd = _ln(f["o_ret"][h])
            g = _cols(p_ref, OFF_RG, h)
            sg = _sigmoid(g)
            dy = dmx_ref[:, _head(h)].astype(jnp.float32)
            wr = wr_ref[:, _head(h)]
            dwr_ref[:, _head(h)] += _colsum(dy * on * (g * sg))
            dp_ref[:, OFF_RG + h * HEAD_W:OFF_RG + (h + 1) * HEAD_W] = dy * on * wr * (sg * (1.0 + g * (1.0 - sg)))
            do_ret.append(_ln_bwd(dy * wr * (g * sg), on, rstd))
        for h in HEADS:
            o = f["o_gla"][h]
            rstd = lax.rsqrt(_rowmean(o * o) + LN_EPS)
            on = o * rstd
            g = _cols(p_ref, OFF_GG, h)
            sg = _sigmoid(g)
            dy = dmx_ref[:, _head(N_HEADS + h)].astype(jnp.float32)
            wl = wl_ref[:, _head(h)]
            dwl_ref[:, _head(h)] += _colsum(dy * on * (g * sg))
            dp_ref[:, OFF_GG + h * HEAD_W:OFF_GG + (h + 1) * HEAD_W] = dy * on * wl * (sg * (1.0 + g * (1.0 - sg)))
            don = dy * wl * (g * sg)
            do_gla.append(rstd * (don - on * _rowmean(don * on)))

        ds_raw = [_mm_nt(do_ret[h], f["rv"][h]) * dm_ref[h] for h in HEADS]
        d_att = [_mm_nt(do_gla[h], f["gv"][h]) for h in HEADS]
        dq_state = [_mm_nt(do_ret[h], ret_state[h]) for h in HEADS]
        dk_state = [_mm_nt(f["rv"][h], d_ret_new[h]) for h in HEADS]
        dqb = [_mm(do_gla[h], gla_state_t[h]) for h in HEADS]
        dkb = [_mm(f["gv"][h], d_gla_new[h]) for h in HEADS]
        for h in HEADS:
            dp_ref[:, OFF_RV + h * HEAD_W:OFF_RV + (h + 1) * HEAD_W] = (
                _mm_tn(f["scores"][h], do_ret[h]) + _mm(f["kd"][h], d_ret_new[h]))
        for h in HEADS:
            dp_ref[:, OFF_GV + h * HEAD_W:OFF_GV + (h + 1) * HEAD_W] = (
                _mm_tn(f["att"][h], do_gla[h]) + _mm_nt(f["kb"][h], d_gla_new[h]))
        for h in HEADS:
            dr_sc[h] = chunk_decay[h] * d_ret_new[h] + _mm_tn(f["qd"][h], do_ret[h])
        for h in HEADS:
            ds_sc[h] = d_gla_new[h] * f["ebl"][:, _head(h)] + _mm_tn(do_gla[h], f["qb"][h])

        dqr = [_mm(ds_raw[h], f["kr"][h]) + dq_state[h] * qdec_ref[:, _head(h)] for h in HEADS]
        dkr = [_mm_tn(ds_raw[h], f["qr"][h]) + dk_state[h] * kdec_ref[:, _head(h)] for h in HEADS]
        d_low = [jnp.where(row >= col, d_att[h], 0.0) for h in HEADS]
        d_up = [jnp.where(row < col, d_att[h], 0.0) for h in HEADS]
        dq_e = [_mm(d_low[h], f["k_i"][h]) for h in HEADS]
        dk_i = [_mm_tn(d_low[h], f["q_e"][h]) for h in HEADS]
        dq_i = [_mm(d_up[h], f["k_e"][h]) for h in HEADS]
        dk_e = [_mm_tn(d_up[h], f["q_i"][h]) for h in HEADS]
        for h in HEADS:
            dp_ref[:, OFF_RQ + h * HEAD_W:OFF_RQ + (h + 1) * HEAD_W] = (
                (dqr[h] * cc + _swap_halves(dqr[h] * ss)) * RET_SCALE)
            dp_ref[:, OFF_RK + h * HEAD_W:OFF_RK + (h + 1) * HEAD_W] = dkr[h] * cc + _swap_halves(dkr[h] * ss)
        row_id = lax.broadcasted_iota(jnp.int32, (CHUNK, HEAD_W), 0)
        db_heads = []
        for h in HEADS:
            hs = _head(h)
            e, ei, eb, ek, ebl = f["e"][:, hs], f["ei"][:, hs], f["eb"][:, hs], f["ek"][:, hs], f["ebl"][:, hs]
            dp_ref[:, OFF_GQ + h * HEAD_W:OFF_GQ + (h + 1) * HEAD_W] = (
                (dq_e[h] * e + dq_i[h] * ei + dqb[h] * eb) * GLA_SCALE)
            dp_ref[:, OFF_GK + h * HEAD_W:OFF_GK + (h + 1) * HEAD_W] = dk_e[h] * e + dk_i[h] * ei + dkb[h] * ek
            db = (dq_e[h] * f["q_e"][h] - dq_i[h] * f["q_i"][h] + dk_e[h] * f["k_e"][h] - dk_i[h] * f["k_i"][h]
                  + dqb[h] * f["qb"][h] - dkb[h] * f["kb"][h])
            db_last = _colsum(dkb[h] * f["kb"][h]) + ebl * _colsum(gla_state_t[h] * d_gla_new[h])
            db_heads.append(db + jnp.where(row_id == CHUNK - 1, db_last, 0.0))
        db = jnp.concatenate(db_heads, axis=1)
        d_la = _mm32((col >= row).astype(jnp.float32), db)
        d_logit = d_la * (1.0 / GATE_TAU) * (1.0 - _sigmoid(f["logit"]))
        dp_ref[:, OFF_LR:OFF_LR + HEAD_W] = _mm32_nt(d_logit, wg_ref[...])
        dwg_ref[...] += _mm32_tn(f["glr"], d_logit)
        dbg_ref[...] += _colsum(d_logit)

    state_blk = pl.BlockSpec((1, N_HEADS, HEAD_W, HEAD_W), lambda i: (last - i, 0, 0, 0))
    rot_blk = pl.BlockSpec((CHUNK, HEAD_W), lambda i: (last - i, 0))
    width = N_HEADS * HEAD_W
    vec_out = pl.BlockSpec((1, width), lambda i: (0, 0))
    return pl.pallas_call(
        body, name="mixer_bwd", grid=(n_chunks,),
        out_shape=(jax.ShapeDtypeStruct((seq, N_PROJ), jnp.float32),
                   jax.ShapeDtypeStruct((1, width), jnp.float32), jax.ShapeDtypeStruct((1, width), jnp.float32),
                   jax.ShapeDtypeStruct((HEAD_W, width), jnp.float32), jax.ShapeDtypeStruct((1, width), jnp.float32)),
        in_specs=[pl.BlockSpec((CHUNK, N_PROJ), lambda i: (last - i, 0)),
                  pl.BlockSpec((CHUNK, D_MODEL), lambda i: (last - i, 0)), state_blk, state_blk, rot_blk, rot_blk,
                  _const_spec(dm_t.shape), _const_spec(qdec_t.shape), _const_spec(kdec_t.shape),
                  _const_spec(wg_p.shape), _const_spec(bg_p.shape), _const_spec(ret_norm_w.shape),
                  _const_spec(gla_norm_w.shape)],
        out_specs=(pl.BlockSpec((CHUNK, N_PROJ), lambda i: (last - i, 0)), vec_out, vec_out,
                   pl.BlockSpec((HEAD_W, width), lambda i: (0, 0)), vec_out),
        scratch_shapes=[pltpu.VMEM((N_HEADS, HEAD_W, HEAD_W), jnp.float32),
                        pltpu.VMEM((N_HEADS, HEAD_W, HEAD_W), jnp.float32)],
        compiler_params=_params(("arbitrary",)),
    )(proj, dmixed, rsave, ssave, cc_t, ss_t, dm_t, qdec_t, kdec_t, wg_p, bg_p, ret_norm_w, gla_norm_w)


V_GATE1, V_SCALE2, V_SHIFT2, V_GATE2, V_LN1W, V_LN1B, V_LN2W, V_LN2B = range(8)
S_GATE1, S_SCALE2, S_SHIFT2, S_GATE2, S_LN1W, S_LN1B, S_LN2W, S_LN2B, S_LOSS = range(9)


def _mlp_fwd_bwd(x2, mixed, target, vecs, w_out, w1_chunks, w2_chunks, tm):
    seq = x2.shape[0]
    n_fc, _, fc = w1_chunks.shape

    def body(x_ref, mx_ref, t_ref, vec_ref, wo_ref, w1_ref, w2_ref,
             dmx_ref, dxa_ref, a_ref, dh_ref, u2_ref, df_ref, dm_ref, sums_ref, relu_sc):
        @pl.when(pl.program_id(0) == 0)
        def _():
            sums_ref[...] = jnp.zeros_like(sums_ref)

        vec = lambda r: vec_ref[r:r + 1, :]

        def acc(r, val):
            sums_ref[r:r + 1, :] += _colsum(val)

        xx = x_ref[...]
        m = _mm(mx_ref[...], wo_ref[...])
        z1h, rstd1 = _ln(ALPHA * xx + vec(V_GATE1) * m)
        x1 = z1h * vec(V_LN1W) + vec(V_LN1B)
        x1h, rstd0 = _ln(x1)
        u2 = (x1h * (1.0 + vec(V_SCALE2)) + vec(V_SHIFT2)).astype(MXU_DTYPE)
        u2_ref[...] = u2
        f = jnp.zeros((tm, D_MODEL), jnp.float32)
        for j in range(n_fc):
            r = jnp.maximum(_mm(u2, w1_ref[j]), 0.0)
            relu_sc[:, j * fc:(j + 1) * fc] = r
            a = (r * r).astype(MXU_DTYPE)
            a_ref[:, j * fc:(j + 1) * fc] = a
            f = f + _mm(a, w2_ref[j])
        z2h, rstd2 = _ln(ALPHA * x1 + vec(V_GATE2) * f)
        err = z2h * vec(V_LN2W) + vec(V_LN2B) - t_ref[...]
        acc(S_LOSS, err * err)
        dy = err * (1.0 / D_MODEL)
        acc(S_LN2W, dy * z2h)
        acc(S_LN2B, dy)
        dz2 = _ln_bwd(dy * vec(V_LN2W), z2h, rstd2)
        acc(S_GATE2, dz2 * f)
        df = (vec(V_GATE2) * dz2).astype(MXU_DTYPE)
        df_ref[...] = df
        du2 = jnp.zeros((tm, D_MODEL), jnp.float32)
        for j in range(n_fc):
            dh = (_mm_nt(df, w2_ref[j]) * (2.0 * relu_sc[:, j * fc:(j + 1) * fc])).astype(MXU_DTYPE)
            dh_ref[:, j * fc:(j + 1) * fc] = dh
            du2 = du2 + _mm_nt(dh, w1_ref[j])
        acc(S_SCALE2, du2 * x1h)
        acc(S_SHIFT2, du2)
        dx1 = ALPHA * dz2 + _ln_bwd(du2 * (1.0 + vec(V_SCALE2)), x1h, rstd0)
        acc(S_LN1W, dx1 * z1h)
        acc(S_LN1B, dx1)
        dz1 = _ln_bwd(dx1 * vec(V_LN1W), z1h, rstd1)
        acc(S_GATE1, dz1 * m)
        dxa_ref[...] = ALPHA * dz1
        dm = (vec(V_GATE1) * dz1).astype(MXU_DTYPE)
        dm_ref[...] = dm
        dmx_ref[...] = _mm_nt(dm, wo_ref[...])

    tile = lambda width: pl.BlockSpec((tm, width), lambda i: (i, 0))
    f32 = lambda width: jax.ShapeDtypeStruct((seq, width), jnp.float32)
    b16 = lambda width: jax.ShapeDtypeStruct((seq, width), MXU_DTYPE)
    return pl.pallas_call(
        body, name="mlp_fwd_bwd", grid=(seq // tm,),
        out_shape=(f32(D_MODEL), f32(D_MODEL), b16(D_FF), b16(D_FF), b16(D_MODEL), b16(D_MODEL), b16(D_MODEL),
                   jax.ShapeDtypeStruct((16, D_MODEL), jnp.float32)),
        in_specs=[tile(D_MODEL), tile(D_MODEL), tile(D_MODEL), _const_spec(vecs.shape), _const_spec(w_out.shape),
                  _const_spec(w1_chunks.shape), _const_spec(w2_chunks.shape)],
        out_specs=(tile(D_MODEL), tile(D_MODEL), tile(D_FF), tile(D_FF), tile(D_MODEL), tile(D_MODEL),
                   tile(D_MODEL), pl.BlockSpec((16, D_MODEL), lambda i: (0, 0))),
        scratch_shapes=[pltpu.VMEM((tm, D_FF), jnp.float32)],
        compiler_params=_params(("arbitrary",)),
    )(x2, mixed, target, vecs, w_out, w1_chunks, w2_chunks)


def _grad_matmul(a, b, name, tn, blocks_are_rows):
    seq, m_dim = a.shape
    n_dim = b.shape[1]
    tk = min(seq, 512)
    nk = seq // tk
    if blocks_are_rows:
        tm = m_dim // N_CHIP
        assert tn == n_dim
        grid = (N_CHIP, 1, nk)
        out_map = lambda i, j, k: (i, 0, 0)
    else:
        tm = m_dim
        assert tn * N_CHIP == n_dim
        grid = (1, N_CHIP, nk)
        out_map = lambda i, j, k: (j, 0, 0)

    def body(a_ref, b_ref, o_ref, acc_sc):
        k = pl.program_id(2)

        @pl.when(k == 0)
        def _():
            acc_sc[...] = jnp.zeros_like(acc_sc)

        acc_sc[...] += _mm_tn(a_ref[...], b_ref[...])

        @pl.when(k == nk - 1)
        def _():
            o_ref[0] = acc_sc[...].astype(o_ref.dtype)

    return pl.pallas_call(
        body, name=name, grid=grid,
        out_shape=jax.ShapeDtypeStruct((N_CHIP, tm, tn), WIRE_DTYPE),
        in_specs=[pl.BlockSpec((tk, tm), lambda i, j, k: (k, i)), pl.BlockSpec((tk, tn), lambda i, j, k: (k, j))],
        out_specs=pl.BlockSpec((1, tm, tn), out_map),
        scratch_shapes=[pltpu.VMEM((tm, tn), jnp.float32)],
        compiler_params=_params(("arbitrary", "arbitrary", "arbitrary")),
    )(a, b)


def _grad_matmul_full(a, b, name, tn):
    seq, m_dim = a.shape
    n_dim = b.shape[1]
    tk = min(seq, 512)
    nk = seq // tk
    assert n_dim % tn == 0

    def body(a_ref, b_ref, o_ref):
        @pl.when(pl.program_id(1) == 0)
        def _():
            o_ref[...] = jnp.zeros_like(o_ref)

        o_ref[...] += _mm_tn(a_ref[...], b_ref[...])

    return pl.pallas_call(
        body, name=name, grid=(n_dim // tn, nk),
        out_shape=jax.ShapeDtypeStruct((m_dim, n_dim), jnp.float32),
        in_specs=[pl.BlockSpec((tk, m_dim), lambda j, k: (k, 0)), pl.BlockSpec((tk, tn), lambda j, k: (k, j))],
        out_specs=pl.BlockSpec((m_dim, tn), lambda j, k: (0, j)),
        compiler_params=_params(("arbitrary", "arbitrary")),
    )(a, b)


def _sum_chips(stack, name):
    _, rows, cols = stack.shape
    tr = min(rows, 256)

    def body(s_ref, o_ref):
        total = s_ref[0].astype(jnp.float32)
        for j in range(1, N_CHIP):
            total = total + s_ref[j].astype(jnp.float32)
        o_ref[...] = total

    return pl.pallas_call(
        body, name=name, grid=(rows // tr,),
        out_shape=jax.ShapeDtypeStruct((rows, cols), jnp.float32),
        in_specs=[pl.BlockSpec((N_CHIP, tr, cols), lambda i: (0, i, 0))],
        out_specs=pl.BlockSpec((tr, cols), lambda i: (i, 0)),
        compiler_params=_params(("arbitrary",)),
    )(stack)


def _adam_pair(w, g_mine, g_sibling, m, v, name):
    rows, cols = w.shape
    tr = min(rows, 256)

    def body(w_ref, ga_ref, gb_ref, m_ref, v_ref, g_ref, dl_ref, m2_ref, v2_ref):
        g = ga_ref[...] + gb_ref[...]
        delta, m2, v2 = _adam(w_ref[...], g, m_ref[...], v_ref[...])
        g_ref[...] = g
        dl_ref[...] = delta
        m2_ref[...] = m2
        v2_ref[...] = v2

    blk = pl.BlockSpec((tr, cols), lambda i: (i, 0))
    out = jax.ShapeDtypeStruct((rows, cols), jnp.float32)
    return pl.pallas_call(
        body, name=name, grid=(rows // tr,),
        out_shape=(out, out, out, out),
        in_specs=[blk] * 5, out_specs=(blk,) * 4,
        compiler_params=_params(("arbitrary",)),
    )(w, g_mine, g_sibling, m, v)


def _sum_devices(gathered):
    _, rows, _ = gathered.shape

    def body(g_ref, o_ref):
        total = g_ref[0]
        for d in range(1, N_DEV):
            total = total + g_ref[d]
        o_ref[...] = total

    return pl.pallas_call(
        body, name="sum_devices",
        out_shape=jax.ShapeDtypeStruct((rows, 128), jnp.float32),
    )(gathered)


def _adam_small(w, g, m, v):
    def body(w_ref, g_ref, m_ref, v_ref, dl_ref, m2_ref, v2_ref):
        delta, m2, v2 = _adam(w_ref[...], g_ref[...], m_ref[...], v_ref[...])
        dl_ref[...] = delta
        m2_ref[...] = m2
        v2_ref[...] = v2

    out = jax.ShapeDtypeStruct(w.shape, jnp.float32)
    return pl.pallas_call(body, name="adam_small", out_shape=(out, out, out))(w, g, m, v)


def _pad_heads(w):
    lead = w.shape[:-1]
    w = w.reshape(lead + (N_HEADS, GLA_DK))
    w = jnp.pad(w, [(0, 0)] * len(lead) + [(0, 0), (0, HEAD_W - GLA_DK)])
    return w.reshape(lead + (N_HEADS * HEAD_W,))


def _unpad_heads(w):
    lead = w.shape[:-1]
    return w.reshape(lead + (N_HEADS, HEAD_W))[..., :GLA_DK].reshape(lead + (N_HEADS * GLA_DK,))


def _pad_w_in(w):
    return jnp.concatenate([
        w[:, :2048], _pad_heads(w[:, 2048:2304]), _pad_heads(w[:, 2304:2560]), w[:, 2560:3584],
        jnp.pad(w[:, 3584:3600], ((0, 0), (0, HEAD_W - GATE_RANK)))], axis=1)


def _unpad_w_in(g):
    return jnp.concatenate([
        g[:, :2048], _unpad_heads(g[:, OFF_GQ:OFF_GQ + 512]), _unpad_heads(g[:, OFF_GK:OFF_GK + 512]),
        g[:, OFF_GV:OFF_LR], g[:, OFF_LR:OFF_LR + GATE_RANK]], axis=1)


def _rows128(a):
    return a.reshape(-1, 128)


def _rows8(a):
    a = a.reshape(-1, 128)
    return jnp.pad(a, ((0, -a.shape[0] % 8), (0, 0)))


def kernel(x, c, w_ada, b_ada, w_in, ret_norm_w, gla_gate_w, gla_gate_b, gla_norm_w, w_out, ln1_w, ln1_b, w_ff1, w_ff2, ln2_w, ln2_b, loss_target, m_w_ada, m_b_ada, m_w_in, m_ret_norm_w, m_gla_gate_w, m_gla_gate_b, m_gla_norm_w, m_w_out, m_ln1_w, m_ln1_b, m_w_ff1, m_w_ff2, m_ln2_w, m_ln2_b, v_w_ada, v_b_ada, v_w_in, v_ret_norm_w, v_gla_gate_w, v_gla_gate_b, v_gla_norm_w, v_w_out, v_ln1_w, v_ln1_b, v_w_ff1, v_w_ff2, v_ln2_w, v_ln2_b):
    seq = x.shape[1]
    tm = min(seq, TOKEN_TILE)
    xi, yi, ci = _mesh_pos()
    dev = 4 * xi + 2 * yi + ci
    chip = 2 * xi + yi
    x2, target = x[0], loss_target[0]
    ada_cols = w_ada.shape[2]
    in_cols = w_in.shape[2]
    gate_cols = gla_gate_w.shape[2]

    g0 = _gather_rows(jnp.concatenate([_rows128(c), _rows128(gla_gate_w[0])], axis=0), "gather_cond")
    c_all = g0[:, :8].reshape(N_DEV, D_MODEL)
    gate_w_full = jnp.concatenate([g0[2 * j, 8:16].reshape(GATE_RANK, gate_cols) for j in range(N_CHIP)], axis=1)
    wg_p = jnp.pad(_pad_heads(gate_w_full), ((0, HEAD_W - GATE_RANK), (0, 0)))
    bg_p = _pad_heads(gla_gate_b)

    b_blk = lax.dynamic_slice(b_ada, (0, chip * ada_cols), (1, ada_cols))
    mod_blk = _ada_fwd(c_all, w_ada[0], b_blk)
    g1 = _gather_rows(_rows128(mod_blk), "gather_mod")
    mod_all = jnp.concatenate([g1[2 * j].reshape(N_DEV, ada_cols) for j in range(N_CHIP)], axis=1)
    mod = lax.dynamic_slice(mod_all, (dev, 0), (1, 6 * D_MODEL))
    shift1, scale1, gate1, shift2, scale2, gate2 = [mod[:, i * D_MODEL:(i + 1) * D_MODEL] for i in range(6)]

    gathered = _chip_exchange([w_in[0].astype(WIRE_DTYPE), w_out[0].astype(WIRE_DTYPE),
                               w_ff1[0].astype(WIRE_DTYPE), w_ff2[0].astype(WIRE_DTYPE)], "gather_weights", True)
    w_in_p = _pad_w_in(jnp.transpose(gathered[0], (1, 0, 2)).reshape(D_MODEL, N_PROJ_SRC)).astype(MXU_DTYPE)
    w_out_full = gathered[1].reshape(D_MODEL, D_MODEL).astype(MXU_DTYPE)
    w1_chunks = gathered[2].astype(MXU_DTYPE)
    w2_chunks = gathered[3].astype(MXU_DTYPE)

    zeros_row = jnp.zeros((1, D_MODEL), jnp.float32)
    vecs1 = jnp.concatenate([shift1, scale1] + [zeros_row] * 6, axis=0)
    proj, u = _inproj_fwd(x2, vecs1, w_in_p, tm)
    cc_t, ss_t = _rotary_tables(seq)
    dm_t, qdec_t, kdec_t, chunk_decay = _decay_tables()
    tables = (cc_t, ss_t, dm_t, qdec_t, kdec_t, chunk_decay)
    mixed, rsave, ssave = _mixer_fwd(proj, tables, wg_p, bg_p, ret_norm_w, gla_norm_w)

    vecs2 = jnp.concatenate([gate1, scale2, shift2, gate2, ln1_w, ln1_b, ln2_w, ln2_b], axis=0)
    dmixed, dxa, act, dh, u2, df, dm, sums2 = _mlp_fwd_bwd(x2, mixed, target, vecs2, w_out_full, w1_chunks,
                                                           w2_chunks, tm)

    dproj, d_ret_norm, d_gla_norm, d_wg_p, d_bg_p = _mixer_bwd(proj, dmixed, rsave, ssave, tables, wg_p, bg_p,
                                                               ret_norm_w, gla_norm_w)
    grad_x, sums1 = _inproj_bwd(dproj, x2, dxa, vecs1, w_in_p, tm)

    g_in_full = _grad_matmul_full(u, dproj, "grad_w_in", 1408)
    g_in_stack = jnp.transpose(_unpad_w_in(g_in_full).reshape(D_MODEL, N_CHIP, in_cols), (1, 0, 2)).astype(WIRE_DTYPE)
    g_out_stack = _grad_matmul(mixed, dm, "grad_w_out", D_MODEL, True)
    g_ff1_stack = _grad_matmul(u2, dh, "grad_w_ff1", D_FF // N_CHIP, False)
    g_ff2_stack = _grad_matmul(act, df, "grad_w_ff2", D_MODEL, True)

    dmod = jnp.concatenate([sums1[0:1], sums1[1:2], sums2[S_GATE1:S_GATE1 + 1], sums2[S_SHIFT2:S_SHIFT2 + 1],
                            sums2[S_SCALE2:S_SCALE2 + 1], sums2[S_GATE2:S_GATE2 + 1]], axis=1)
    d_gate_w_full = _unpad_heads(d_wg_p[:GATE_RANK])
    flat = lambda parts: jnp.concatenate([_rows8(p) for p in parts], axis=0)
    small = flat([dmod, sums2[S_LN1W:S_LN1W + 1], sums2[S_LN1B:S_LN1B + 1], sums2[S_LN2W:S_LN2W + 1],
                  sums2[S_LN2B:S_LN2B + 1], d_ret_norm, _unpad_heads(d_bg_p), d_gla_norm, d_gate_w_full,
                  sums2[S_LOSS:S_LOSS + 1]])
    g2 = _gather_rows(small, "gather_small")
    tot = _sum_devices(g2)
    loss = 0.5 / D_MODEL * jnp.sum(tot[136:144])
    grad_b_ada = tot[0:48].reshape(1, 6 * D_MODEL)
    grad_ln1_w, grad_ln1_b = tot[48:56].reshape(1, D_MODEL), tot[56:64].reshape(1, D_MODEL)
    grad_ln2_w, grad_ln2_b = tot[64:72].reshape(1, D_MODEL), tot[72:80].reshape(1, D_MODEL)
    grad_ret_norm = tot[80:84].reshape(1, 512)
    grad_gate_b = tot[88:90].reshape(1, 256)
    grad_gla_norm = tot[96:100].reshape(1, 512)
    grad_gate_w = lax.dynamic_slice(tot[104:136].reshape(GATE_RANK, 256), (0, chip * gate_cols),
                                    (GATE_RANK, gate_cols))

    small_w = flat([b_ada, ln1_w, ln1_b, ln2_w, ln2_b, ret_norm_w, gla_gate_b, gla_norm_w, gla_gate_w[0]])
    small_g = flat([grad_b_ada, grad_ln1_w, grad_ln1_b, grad_ln2_w, grad_ln2_b, grad_ret_norm, grad_gate_b,
                    grad_gla_norm, grad_gate_w])
    small_m = flat([m_b_ada, m_ln1_w, m_ln1_b, m_ln2_w, m_ln2_b, m_ret_norm_w, m_gla_gate_b, m_gla_norm_w,
                    m_gla_gate_w[0]])
    small_v = flat([v_b_ada, v_ln1_w, v_ln1_b, v_ln2_w, v_ln2_b, v_ret_norm_w, v_gla_gate_b, v_gla_norm_w,
                    v_gla_gate_w[0]])
    small_out = _adam_small(small_w, small_g, small_m, small_v)

    def unflat(t):
        pieces, row = [], 0
        for shape in [(1, 6 * D_MODEL)] + [(1, D_MODEL)] * 4 + [(1, 512), (1, 256), (1, 512), (1, GATE_RANK, gate_cols)]:
            n = int(np.prod(shape)) // 128
            pieces.append(t[row:row + n].reshape(shape))
            row += -(-n // 8) * 8
        return pieces

    sm_delta, sm_m, sm_v = [unflat(t) for t in small_out]

    dmod_all = g2[:, 0:48].reshape(N_DEV, 6 * D_MODEL)
    dmod_blk = lax.dynamic_slice(dmod_all, (0, chip * ada_cols), (N_DEV, ada_cols))
    ada_out = _ada_bwd_adam(jnp.transpose(c_all), dmod_blk, w_ada[0], m_w_ada[0], v_w_ada[0])
    ada_g, ada_delta, ada_m, ada_v = [t[None] for t in ada_out]

    received = _chip_exchange([g_in_stack, g_out_stack, g_ff1_stack, g_ff2_stack], "scatter_grads", False)
    names = ["w_in", "w_out", "w_ff1", "w_ff2"]
    partial = [_sum_chips(r, "sum_" + n) for r, n in zip(received, names)]
    swapped = _sibling_swap(partial, "swap_partials")
    big = {}
    for n, w, mine, theirs, m, v in zip(names, [w_in, w_out, w_ff1, w_ff2], partial, swapped,
                                        [m_w_in, m_w_out, m_w_ff1, m_w_ff2], [v_w_in, v_w_out, v_w_ff1, v_w_ff2]):
        big[n] = [t[None] for t in _adam_pair(w[0], mine, theirs, m[0], v[0], "adam_" + n)]

    def assemble(ada, smalls, k):
        b_ada_o, ln1w_o, ln1b_o, ln2w_o, ln2b_o, ret_o, gb_o, gln_o, gw_o = smalls
        return [ada, b_ada_o, big["w_in"][k], ret_o, gw_o, gb_o, gln_o, big["w_out"][k], ln1w_o, ln1b_o,
                big["w_ff1"][k], big["w_ff2"][k], ln2w_o, ln2b_o]

    small_grads = [grad_b_ada, grad_ln1_w, grad_ln1_b, grad_ln2_w, grad_ln2_b, grad_ret_norm, grad_gate_b,
                   grad_gla_norm, grad_gate_w[None]]
    grads = assemble(ada_g, small_grads, 0)
    deltas = assemble(ada_delta, sm_delta, 1)
    new_m = assemble(ada_m, sm_m, 2)
    new_v = assemble(ada_v, sm_v, 3)
    return (loss, grad_x[None], *grads, *deltas, *new_m, *new_v)
```

```python
import functools

import numpy as np
import jax
import jax.numpy as jnp
from jax import lax
from jax.experimental import pallas as pl
from jax.experimental.pallas import tpu as pltpu

D_MODEL = 1024
D_FF = 4096
CHUNK = 64
N_HEADS = 4
HEAD_W = 128
GLA_DK = 64
GATE_RANK = 16
GATE_TAU = 16.0
LN_EPS = 1e-5
ALPHA = 2.0 ** 0.25
ROPE_BASE = 10000.0
RET_SCALE = float(HEAD_W) ** -0.5
GLA_SCALE = float(GLA_DK) ** -0.5

ADAM_LR = 0.001
ADAM_B1 = 0.9
ADAM_B2 = 0.999
ADAM_EPS = 1e-08
ADAM_WD = 0.01
ADAM_STEP = 10

OFF_RQ, OFF_RK, OFF_RV, OFF_RG = 0, 512, 1024, 1536
OFF_GQ, OFF_GK, OFF_GV, OFF_GG, OFF_LR = 2048, 2560, 3072, 3584, 4096
N_PROJ = 4224
N_PROJ_SRC = 3600

N_DEV = 8
N_CHIP = 4
MESH = pl.DeviceIdType.MESH
MXU_DTYPE = jnp.bfloat16
WIRE_DTYPE = jnp.bfloat16
VMEM_LIMIT = 60 * 1024 * 1024
TOKEN_TILE = 256
HIGHEST = lax.Precision.HIGHEST


def _mm(a, b):
    return jnp.dot(a.astype(MXU_DTYPE), b.astype(MXU_DTYPE), preferred_element_type=jnp.float32)


def _mm_nt(a, b):
    return lax.dot_general(a.astype(MXU_DTYPE), b.astype(MXU_DTYPE), (((1,), (1,)), ((), ())),
                           preferred_element_type=jnp.float32)


def _mm_tn(a, b):
    return lax.dot_general(a.astype(MXU_DTYPE), b.astype(MXU_DTYPE), (((0,), (0,)), ((), ())),
                           preferred_element_type=jnp.float32)


def _mm32(a, b):
    return jnp.dot(a, b, precision=HIGHEST, preferred_element_type=jnp.float32)


def _mm32_nt(a, b):
    return lax.dot_general(a, b, (((1,), (1,)), ((), ())), precision=HIGHEST, preferred_element_type=jnp.float32)


def _mm32_tn(a, b):
    return lax.dot_general(a, b, (((0,), (0,)), ((), ())), precision=HIGHEST, preferred_element_type=jnp.float32)


def _rowmean(a):
    return jnp.mean(a, axis=-1, keepdims=True)


def _colsum(a):
    return jnp.sum(a, axis=0, keepdims=True)


def _ln(z):
    zc = z - _rowmean(z)
    rstd = lax.rsqrt(_rowmean(zc * zc) + LN_EPS)
    return zc * rstd, rstd


def _ln_bwd(dzh, zh, rstd):
    return rstd * (dzh - _rowmean(dzh) - zh * _rowmean(dzh * zh))


def _sigmoid(a):
    return 1.0 / (1.0 + jnp.exp(-a))


def _log_sigmoid(a):
    return jnp.minimum(a, 0.0) - jnp.log(1.0 + jnp.exp(-jnp.abs(a)))


def _swap_halves(a):
    return pltpu.roll(a, HEAD_W // 2, 1)


def _tri_masks():
    row = lax.broadcasted_iota(jnp.int32, (CHUNK, CHUNK), 0)
    col = lax.broadcasted_iota(jnp.int32, (CHUNK, CHUNK), 1)
    return row, col


def _const_spec(shape):
    zeros = (0,) * len(shape)
    return pl.BlockSpec(shape, lambda *_: zeros, pipeline_mode=pl.Buffered(1))


def _params(semantics):
    return pltpu.CompilerParams(dimension_semantics=semantics, vmem_limit_bytes=VMEM_LIMIT)


def _decay_tables():
    log_gamma = np.log(1.0 - 2.0 ** (-5.0 - np.arange(N_HEADS, dtype=np.float64)))
    idx = np.arange(CHUNK, dtype=np.float64)
    dist = np.abs(idx[:, None] - idx[None, :])
    intra = np.exp(log_gamma[:, None, None] * dist)
    kdec = np.exp(log_gamma[None, :] * (CHUNK - 1.0 - idx)[:, None])
    qdec = np.exp(log_gamma[None, :] * (idx + 1.0)[:, None])
    chunk_decay = np.exp(log_gamma * CHUNK)
    lanes = lambda t: np.repeat(t, HEAD_W, axis=1).astype(np.float32)
    return (jnp.asarray(intra.astype(np.float32)), jnp.asarray(lanes(qdec)), jnp.asarray(lanes(kdec)),
            [float(np.float32(v)) for v in chunk_decay])


def _rotary_tables(seq):
    half = HEAD_W // 2
    inv = 1.0 / (ROPE_BASE ** jnp.linspace(0.0, 1.0, half, dtype=jnp.float32))
    ang = jnp.arange(seq, dtype=jnp.float32)[:, None] * inv[None, :]
    cos, sin = jnp.cos(ang), jnp.sin(ang)
    return jnp.concatenate([cos, cos], axis=1), jnp.concatenate([-sin, sin], axis=1)


def _mesh_pos():
    return lax.axis_index("x"), lax.axis_index("y"), lax.axis_index("c")


def _flip(v, bit):
    return 1 - v if bit else v


def _gather_rows(v, name):
    rows = v.shape[0]

    def body(v_ref, out_ref, send_sems, recv_sems):
        x, y, c = _mesh_pos()
        me = 4 * x + 2 * y + c
        out_ref[me] = v_ref[...]
        sends, recvs = [], []
        for k in range(1, N_DEV):
            px, py, pc = _flip(x, (k >> 2) & 1), _flip(y, (k >> 1) & 1), _flip(c, k & 1)
            peer = 4 * px + 2 * py + pc
            sends.append(pltpu.make_async_remote_copy(
                src_ref=v_ref, dst_ref=out_ref.at[me], send_sem=send_sems.at[k - 1], recv_sem=recv_sems.at[k - 1],
                device_id=(px, py, pc), device_id_type=MESH))
            recvs.append(pltpu.make_async_remote_copy(
                src_ref=v_ref, dst_ref=out_ref.at[peer], send_sem=send_sems.at[k - 1], recv_sem=recv_sems.at[k - 1],
                device_id=(px, py, pc), device_id_type=MESH))
        for cp in sends:
            cp.start()
        for cp in recvs:
            cp.wait_recv()
        for cp in sends:
            cp.wait_send()

    return pl.pallas_call(
        body, name=name,
        out_shape=jax.ShapeDtypeStruct((N_DEV, rows, 128), jnp.float32),
        in_specs=[pl.BlockSpec(memory_space=pltpu.VMEM)],
        out_specs=pl.BlockSpec(memory_space=pltpu.VMEM),
        scratch_shapes=[pltpu.SemaphoreType.DMA((N_DEV - 1,)), pltpu.SemaphoreType.DMA((N_DEV - 1,))],
    )(v)


def _chip_exchange(arrays, name, gather):
    n = len(arrays)

    def body(*refs):
        exchange = _ChipExchange(refs[:n], refs[n:2 * n], refs[2 * n:], gather)
        exchange.start()
        exchange.wait()

    return pl.pallas_call(
        body, name=name,
        out_shape=_exchange_out_shapes(arrays, gather),
        in_specs=[pl.BlockSpec(memory_space=pl.ANY)] * n,
        out_specs=tuple(pl.BlockSpec(memory_space=pl.ANY) for _ in arrays),
        scratch_shapes=_exchange_sems(n),
    )(*arrays)


def _exchange_out_shapes(arrays, gather):
    return tuple(jax.ShapeDtypeStruct((N_CHIP,) + a.shape if gather else a.shape, a.dtype) for a in arrays)


def _exchange_sems(n):
    n_sem = n * (N_CHIP - 1)
    return [pltpu.SemaphoreType.DMA((n_sem,)), pltpu.SemaphoreType.DMA((n_sem,)), pltpu.SemaphoreType.DMA((n,))]


class _ChipExchange:
    def __init__(self, ins, outs, sems, gather):
        send_sems, recv_sems, local_sems = sems
        x, y, c = _mesh_pos()
        chip = 2 * x + y
        self.local, self.sends, self.recvs = [], [], []
        for i in range(len(ins)):
            src = ins[i] if gather else ins[i].at[chip]
            self.local.append(pltpu.make_async_copy(src, outs[i].at[chip], local_sems.at[i]))
            for k in range(1, N_CHIP):
                px, py = _flip(x, (k >> 1) & 1), _flip(y, k & 1)
                peer_chip = 2 * px + py
                sem = i * (N_CHIP - 1) + k - 1
                src = ins[i] if gather else ins[i].at[peer_chip]
                self.sends.append(pltpu.make_async_remote_copy(
                    src_ref=src, dst_ref=outs[i].at[chip], send_sem=send_sems.at[sem], recv_sem=recv_sems.at[sem],
                    device_id=(px, py, c), device_id_type=MESH))
                self.recvs.append(pltpu.make_async_remote_copy(
                    src_ref=src, dst_ref=outs[i].at[peer_chip], send_sem=send_sems.at[sem], recv_sem=recv_sems.at[sem],
                    device_id=(px, py, c), device_id_type=MESH))

    def start(self):
        for cp in self.local + self.sends:
            cp.start()

    def wait(self):
        for cp in self.recvs:
            cp.wait_recv()
        for cp in self.sends:
            cp.wait_send()
        for cp in self.local:
            cp.wait()


def _sibling_swap(arrays, name):
    n = len(arrays)

    def body(*refs):
        ins, outs = refs[:n], refs[n:2 * n]
        send_sems, recv_sems = refs[2 * n:]
        x, y, c = _mesh_pos()
        copies = [pltpu.make_async_remote_copy(
            src_ref=ins[i], dst_ref=outs[i], send_sem=send_sems.at[i], recv_sem=recv_sems.at[i],
            device_id=(x, y, 1 - c), device_id_type=MESH) for i in range(n)]
        for cp in copies:
            cp.start()
        for cp in copies:
            cp.wait_recv()
        for cp in copies:
            cp.wait_send()

    return pl.pallas_call(
        body, name=name,
        out_shape=tuple(jax.ShapeDtypeStruct(a.shape, a.dtype) for a in arrays),
        in_specs=[pl.BlockSpec(memory_space=pl.ANY)] * n,
        out_specs=tuple(pl.BlockSpec(memory_space=pl.ANY) for _ in arrays),
        scratch_shapes=[pltpu.SemaphoreType.DMA((n,)), pltpu.SemaphoreType.DMA((n,))],
    )(*arrays)


def _ada_fwd(c_all, w_ada_blk, b_blk):
    cols = w_ada_blk.shape[1]

    def body(c_ref, w_ref, b_ref, out_ref):
        cv = c_ref[...]
        out_ref[...] = _mm32(cv * _sigmoid(cv), w_ref[...]) + b_ref[...]

    return pl.pallas_call(
        body, name="ada_fwd",
        out_shape=jax.ShapeDtypeStruct((N_DEV, cols), jnp.float32),
        compiler_params=pltpu.CompilerParams(vmem_limit_bytes=VMEM_LIMIT),
    )(c_all, w_ada_blk, b_blk)


def _adam(w, g, m, v):
    m2 = ADAM_B1 * m + (1.0 - ADAM_B1) * g
    v2 = ADAM_B2 * v + (1.0 - ADAM_B2) * (g * g)
    m_hat = m2 / (1.0 - ADAM_B1 ** ADAM_STEP)
    v_hat = v2 / (1.0 - ADAM_B2 ** ADAM_STEP)
    delta = -ADAM_LR * (m_hat / (jnp.sqrt(v_hat) + ADAM_EPS) + ADAM_WD * w)
    return delta, m2, v2


def _ada_bwd_adam(c_t, dmod_blk, w, m, v):
    rows, cols = w.shape
    tile = 512
    assert cols % tile == 0

    def body(c_ref, d_ref, w_ref, m_ref, v_ref, g_ref, dl_ref, m2_ref, v2_ref):
        sc = c_ref[...]
        sc = sc * _sigmoid(sc)
        dm = d_ref[...]
        g = sc[:, 0:1] * dm[0:1, :]
        for b in range(1, N_DEV):
            g = g + sc[:, b:b + 1] * dm[b:b + 1, :]
        delta, m2, v2 = _adam(w_ref[...], g, m_ref[...], v_ref[...])
        g_ref[...] = g
        dl_ref[...] = delta
        m2_ref[...] = m2
        v2_ref[...] = v2

    blk = pl.BlockSpec((rows, tile), lambda j: (0, j))
    out = jax.ShapeDtypeStruct((rows, cols), jnp.float32)
    return pl.pallas_call(
        body, name="ada_bwd_adam", grid=(cols // tile,),
        out_shape=(out, out, out, out),
        in_specs=[pl.BlockSpec((rows, N_DEV), lambda j: (0, 0)), pl.BlockSpec((N_DEV, tile), lambda j: (0, j)),
                  blk, blk, blk],
        out_specs=(blk, blk, blk, blk),
        compiler_params=_params(("arbitrary",)),
    )(c_t, dmod_blk, w, m, v)


def _inproj_fwd(x2, vecs, w_in_p, tm):
    seq = x2.shape[0]

    def body(x_ref, vec_ref, w_ref, p_ref, u_ref):
        xh, _ = _ln(x_ref[...])
        u = (xh * (1.0 + vec_ref[1:2, :]) + vec_ref[0:1, :]).astype(MXU_DTYPE)
        u_ref[...] = u
        p_ref[...] = _mm(u, w_ref[...])

    return pl.pallas_call(
        body, name="inproj_fwd", grid=(seq // tm,),
        out_shape=(jax.ShapeDtypeStruct((seq, N_PROJ), jnp.float32), jax.ShapeDtypeStruct((seq, D_MODEL), MXU_DTYPE)),
        in_specs=[pl.BlockSpec((tm, D_MODEL), lambda i: (i, 0)), _const_spec(vecs.shape), _const_spec(w_in_p.shape)],
        out_specs=(pl.BlockSpec((tm, N_PROJ), lambda i: (i, 0)), pl.BlockSpec((tm, D_MODEL), lambda i: (i, 0))),
        compiler_params=_params(("arbitrary",)),
    )(x2, vecs, w_in_p)


def _inproj_bwd(dproj, x2, dxa, vecs, w_in_p, tm, riders):
    seq = x2.shape[0]
    n_tiles = seq // tm
    n_ride = len(riders)

    def body(*refs):
        dp_ref, x_ref, dxa_ref, vec_ref, w_ref = refs[:5]
        ride_in, refs = refs[5:5 + n_ride], refs[5 + n_ride:]
        gx_ref, sums_ref = refs[:2]
        ride_out, sems = refs[2:2 + n_ride], refs[2 + n_ride:]
        exchange = _ChipExchange(ride_in, ride_out, sems, False)

        @pl.when(pl.program_id(0) == 0)
        def _():
            exchange.start()
            sums_ref[...] = jnp.zeros_like(sums_ref)

        du = _mm_nt(dp_ref[...], w_ref[...])
        xh, rstd = _ln(x_ref[...])
        sums_ref[0:1, :] += _colsum(du)
        sums_ref[1:2, :] += _colsum(du * xh)
        gx_ref[...] = dxa_ref[...] + _ln_bwd(du * (1.0 + vec_ref[1:2, :]), xh, rstd)

        @pl.when(pl.program_id(0) == n_tiles - 1)
        def _():
            exchange.wait()

    tile = pl.BlockSpec((tm, D_MODEL), lambda i: (i, 0))
    hbm = pl.BlockSpec(memory_space=pl.ANY)
    return pl.pallas_call(
        body, name="inproj_bwd", grid=(n_tiles,),
        out_shape=(jax.ShapeDtypeStruct((seq, D_MODEL), jnp.float32), jax.ShapeDtypeStruct((8, D_MODEL), jnp.float32))
        + _exchange_out_shapes(riders, False),
        in_specs=[pl.BlockSpec((tm, N_PROJ), lambda i: (i, 0)), tile, tile, _const_spec(vecs.shape),
                  _const_spec(w_in_p.shape)] + [hbm] * n_ride,
        out_specs=(tile, pl.BlockSpec((8, D_MODEL), lambda i: (0, 0))) + (hbm,) * n_ride,
        scratch_shapes=_exchange_sems(n_ride),
        compiler_params=_params(("arbitrary",)),
    )(dproj, x2, dxa, vecs, w_in_p, *riders)


def _head(h):
    return slice(h * HEAD_W, (h + 1) * HEAD_W)


def _cols(ref, off, h):
    return ref[:, off + h * HEAD_W:off + (h + 1) * HEAD_W]


HEADS = range(N_HEADS)


def _mixer_chunk_forward(p_ref, cc, ss, dm_ref, qdec_ref, kdec_ref, wg_ref, bg_ref, ret_state, gla_state_t):
    row, col = _tri_masks()
    lower = row >= col
    f = {}
    f["glr"] = p_ref[:, OFF_LR:OFF_LR + HEAD_W]
    f["logit"] = _mm32(f["glr"], wg_ref[...]) + bg_ref[...]
    rq = [_cols(p_ref, OFF_RQ, h) for h in HEADS]
    rk = [_cols(p_ref, OFF_RK, h) for h in HEADS]
    f["rv"] = [_cols(p_ref, OFF_RV, h) for h in HEADS]
    f["qr"] = [(rq[h] * cc + _swap_halves(rq[h]) * ss) * RET_SCALE for h in HEADS]
    f["kr"] = [rk[h] * cc + _swap_halves(rk[h]) * ss for h in HEADS]
    s_raw = [_mm_nt(f["qr"][h], f["kr"][h]) for h in HEADS]
    la = _log_sigmoid(f["logit"]) * (1.0 / GATE_TAU)
    b = _mm32(lower.astype(jnp.float32), la)
    f["qd"] = [f["qr"][h] * qdec_ref[:, _head(h)] for h in HEADS]
    f["kd"] = [f["kr"][h] * kdec_ref[:, _head(h)] for h in HEADS]
    f["scores"] = [s_raw[h] * dm_ref[h] for h in HEADS]
    f["o_ret"] = [_mm(f["scores"][h], f["rv"][h]) + _mm(f["qd"][h], ret_state[h]) for h in HEADS]
    b_last = b[CHUNK - 1:CHUNK, :]
    b_mid = b[CHUNK // 2 - 1:CHUNK // 2, :]
    f["e"], f["ei"] = jnp.exp(b - b_mid), jnp.exp(b_mid - b)
    f["eb"], f["ek"], f["ebl"] = jnp.exp(b), jnp.exp(b_last - b), jnp.exp(b_last)
    gq = [_cols(p_ref, OFF_GQ, h) * GLA_SCALE for h in HEADS]
    gk = [_cols(p_ref, OFF_GK, h) for h in HEADS]
    f["gv"] = [_cols(p_ref, OFF_GV, h) for h in HEADS]
    f["q_e"] = [gq[h] * f["e"][:, _head(h)] for h in HEADS]
    f["q_i"] = [gq[h] * f["ei"][:, _head(h)] for h in HEADS]
    f["k_e"] = [gk[h] * f["e"][:, _head(h)] for h in HEADS]
    f["k_i"] = [gk[h] * f["ei"][:, _head(h)] for h in HEADS]
    low = [_mm_nt(f["q_e"][h], f["k_i"][h]) for h in HEADS]
    up = [_mm_nt(f["q_i"][h], f["k_e"][h]) for h in HEADS]
    f["att"] = [jnp.where(lower, low[h], up[h]) for h in HEADS]
    f["qb"] = [gq[h] * f["eb"][:, _head(h)] for h in HEADS]
    f["kb"] = [gk[h] * f["ek"][:, _head(h)] for h in HEADS]
    f["o_gla"] = [_mm(f["att"][h], f["gv"][h]) + _mm_nt(f["qb"][h], gla_state_t[h]) for h in HEADS]
    return f


def _mixer_fwd(proj, tables, wg_p, bg_p, ret_norm_w, gla_norm_w, riders):
    seq = proj.shape[0]
    n_chunks = seq // CHUNK
    n_ride = len(riders)
    cc_t, ss_t, dm_t, qdec_t, kdec_t, chunk_decay = tables

    def body(*refs):
        p_ref, cc_ref, ss_ref, dm_ref, qdec_ref, kdec_ref, wg_ref, bg_ref, wr_ref, wl_ref = refs[:10]
        ride_in, refs = refs[10:10 + n_ride], refs[10 + n_ride:]
        mix_ref, rsave_ref, ssave_ref = refs[:3]
        ride_out, refs = refs[3:3 + n_ride], refs[3 + n_ride:]
        r_sc, s_sc = refs[:2]
        exchange = _ChipExchange(ride_in, ride_out, refs[2:], True)

        @pl.when(pl.program_id(0) == 0)
        def _():
            exchange.start()
            r_sc[...] = jnp.zeros_like(r_sc)
            s_sc[...] = jnp.zeros_like(s_sc)

        ret_state = [r_sc[h] for h in HEADS]
        gla_state_t = [s_sc[h] for h in HEADS]
        for h in HEADS:
            rsave_ref[0, h] = ret_state[h]
            ssave_ref[0, h] = gla_state_t[h]
        f = _mixer_chunk_forward(p_ref, cc_ref[...], ss_ref[...], dm_ref, qdec_ref, kdec_ref, wg_ref, bg_ref,
                                 ret_state, gla_state_t)
        for h in HEADS:
            r_sc[h] = chunk_decay[h] * ret_state[h] + _mm_tn(f["kd"][h], f["rv"][h])
        for h in HEADS:
            s_sc[h] = gla_state_t[h] * f["ebl"][:, _head(h)] + _mm_tn(f["gv"][h], f["kb"][h])
        for h in HEADS:
            on, _ = _ln(f["o_ret"][h])
            g = _cols(p_ref, OFF_RG, h)
            mix_ref[:, _head(h)] = (on * wr_ref[:, _head(h)] * (g * _sigmoid(g))).astype(mix_ref.dtype)
        for h in HEADS:
            o = f["o_gla"][h]
            on = o * lax.rsqrt(_rowmean(o * o) + LN_EPS)
            g = _cols(p_ref, OFF_GG, h)
            mix_ref[:, _head(N_HEADS + h)] = (on * wl_ref[:, _head(h)] * (g * _sigmoid(g))).astype(mix_ref.dtype)

        @pl.when(pl.program_id(0) == n_chunks - 1)
        def _():
            exchange.wait()

    state_shape = (n_chunks, N_HEADS, HEAD_W, HEAD_W)
    state_blk = pl.BlockSpec((1, N_HEADS, HEAD_W, HEAD_W), lambda i: (i, 0, 0, 0))
    rot_blk = pl.BlockSpec((CHUNK, HEAD_W), lambda i: (i, 0))
    hbm = pl.BlockSpec(memory_space=pl.ANY)
    return pl.pallas_call(
        body, name="mixer_fwd", grid=(n_chunks,),
        out_shape=(jax.ShapeDtypeStruct((seq, D_MODEL), MXU_DTYPE),
                   jax.ShapeDtypeStruct(state_shape, jnp.float32), jax.ShapeDtypeStruct(state_shape, jnp.float32))
        + _exchange_out_shapes(riders, True),
        in_specs=[pl.BlockSpec((CHUNK, N_PROJ), lambda i: (i, 0)), rot_blk, rot_blk,
                  _const_spec(dm_t.shape), _const_spec(qdec_t.shape), _const_spec(kdec_t.shape),
                  _const_spec(wg_p.shape), _const_spec(bg_p.shape), _const_spec(ret_norm_w.shape),
                  _const_spec(gla_norm_w.shape)] + [hbm] * n_ride,
        out_specs=(pl.BlockSpec((CHUNK, D_MODEL), lambda i: (i, 0)), state_blk, state_blk) + (hbm,) * n_ride,
        scratch_shapes=[pltpu.VMEM((N_HEADS, HEAD_W, HEAD_W), jnp.float32),
                        pltpu.VMEM((N_HEADS, HEAD_W, HEAD_W), jnp.float32)] + _exchange_sems(n_ride),
        compiler_params=_params(("arbitrary",)),
    )(proj, cc_t, ss_t, dm_t, qdec_t, kdec_t, wg_p, bg_p, ret_norm_w, gla_norm_w, *riders)


def _mixer_bwd(proj, dmixed, rsave, ssave, tables, wg_p, bg_p, ret_norm_w, gla_norm_w, riders):
    seq = proj.shape[0]
    n_chunks = seq // CHUNK
    n_ride = len(riders)
    cc_t, ss_t, dm_t, qdec_t, kdec_t, chunk_decay = tables
    last = n_chunks - 1

    def body(*refs):
        (p_ref, dmx_ref, rsave_ref, ssave_ref, cc_ref, ss_ref, dm_ref, qdec_ref, kdec_ref, wg_ref, bg_ref,
         wr_ref, wl_ref) = refs[:13]
        ride_in, refs = refs[13:13 + n_ride], refs[13 + n_ride:]
        dp_ref, dwr_ref, dwl_ref, dwg_ref, dbg_ref = refs[:5]
        ride_out, refs = refs[5:5 + n_ride], refs[5 + n_ride:]
        dr_sc, ds_sc = refs[:2]
        exchange = _ChipExchange(ride_in, ride_out, refs[2:], False)

        @pl.when(pl.program_id(0) == 0)
        def _():
            exchange.start()
            dr_sc[...] = jnp.zeros_like(dr_sc)
            ds_sc[...] = jnp.zeros_like(ds_sc)
            dwr_ref[...] = jnp.zeros_like(dwr_ref)
            dwl_ref[...] = jnp.zeros_like(dwl_ref)
            dwg_ref[...] = jnp.zeros_like(dwg_ref)
            dbg_ref[...] = jnp.zeros_like(dbg_ref)

        cc, ss = cc_ref[...], ss_ref[...]
        row, col = _tri_masks()
        ret_state = [rsave_ref[0, h] for h in HEADS]
        gla_state_t = [ssave_ref[0, h] for h in HEADS]
        d_ret_new = [dr_sc[h] for h in HEADS]
        d_gla_new = [ds_sc[h] for h in HEADS]
        f = _mixer_chunk_forward(p_ref, cc, ss, dm_ref, qdec_ref, kdec_ref, wg_ref, bg_ref, ret_state, gla_state_t)

        do_ret, do_gla = [], []
        for h in HEADS:
            on, rstd = _ln(f["o_ret"][h])
            g = _cols(p_ref, OFF_RG, h)
            sg = _sigmoid(g)
            dy = dmx_ref[:, _head(h)].astype(jnp.float32)
            wr = wr_ref[:, _head(h)]
            dwr_ref[:, _head(h)] += _colsum(dy * on * (g * sg))
            dp_ref[:, OFF_RG + h * HEAD_W:OFF_RG + (h + 1) * HEAD_W] = dy * on * wr * (sg * (1.0 + g * (1.0 - sg)))
            do_ret.append(_ln_bwd(dy * wr * (g * sg), on, rstd))
        for h in HEADS:
            o = f["o_gla"][h]
            rstd = lax.rsqrt(_rowmean(o * o) + LN_EPS)
            on = o * rstd
            g = _cols(p_ref, OFF_GG, h)
            sg = _sigmoid(g)
            dy = dmx_ref[:, _head(N_HEADS + h)].astype(jnp.float32)
            wl = wl_ref[:, _head(h)]
            dwl_ref[:, _head(h)] += _colsum(dy * on * (g * sg))
            dp_ref[:, OFF_GG + h * HEAD_W:OFF_GG + (h + 1) * HEAD_W] = dy * on * wl * (sg * (1.0 + g * (1.0 - sg)))
            don = dy * wl * (g * sg)
            do_gla.append(rstd * (don - on * _rowmean(don * on)))

        ds_raw = [_mm_nt(do_ret[h], f["rv"][h]) * dm_ref[h] for h in HEADS]
        d_att = [_mm_nt(do_gla[h], f["gv"][h]) for h in HEADS]
        dq_state = [_mm_nt(do_ret[h], ret_state[h]) for h in HEADS]
        dk_state = [_mm_nt(f["rv"][h], d_ret_new[h]) for h in HEADS]
        dqb = [_mm(do_gla[h], gla_state_t[h]) for h in HEADS]
        dkb = [_mm(f["gv"][h], d_gla_new[h]) for h in HEADS]
        for h in HEADS:
            dp_ref[:, OFF_RV + h * HEAD_W:OFF_RV + (h + 1) * HEAD_W] = (
                _mm_tn(f["scores"][h], do_ret[h]) + _mm(f["kd"][h], d_ret_new[h]))
        for h in HEADS:
            dp_ref[:, OFF_GV + h * HEAD_W:OFF_GV + (h + 1) * HEAD_W] = (
                _mm_tn(f["att"][h], do_gla[h]) + _mm_nt(f["kb"][h], d_gla_new[h]))
        for h in HEADS:
            dr_sc[h] = chunk_decay[h] * d_ret_new[h] + _mm_tn(f["qd"][h], do_ret[h])
        for h in HEADS:
            ds_sc[h] = d_gla_new[h] * f["ebl"][:, _head(h)] + _mm_tn(do_gla[h], f["qb"][h])

        dqr = [_mm(ds_raw[h], f["kr"][h]) + dq_state[h] * qdec_ref[:, _head(h)] for h in HEADS]
        dkr = [_mm_tn(ds_raw[h], f["qr"][h]) + dk_state[h] * kdec_ref[:, _head(h)] for h in HEADS]
        d_low = [jnp.where(row >= col, d_att[h], 0.0) for h in HEADS]
        d_up = [jnp.where(row < col, d_att[h], 0.0) for h in HEADS]
        dq_e = [_mm(d_low[h], f["k_i"][h]) for h in HEADS]
        dk_i = [_mm_tn(d_low[h], f["q_e"][h]) for h in HEADS]
        dq_i = [_mm(d_up[h], f["k_e"][h]) for h in HEADS]
        dk_e = [_mm_tn(d_up[h], f["q_i"][h]) for h in HEADS]
        for h in HEADS:
            dp_ref[:, OFF_RQ + h * HEAD_W:OFF_RQ + (h + 1) * HEAD_W] = (
                (dqr[h] * cc + _swap_halves(dqr[h] * ss)) * RET_SCALE)
            dp_ref[:, OFF_RK + h * HEAD_W:OFF_RK + (h + 1) * HEAD_W] = dkr[h] * cc + _swap_halves(dkr[h] * ss)
        row_id = lax.broadcasted_iota(jnp.int32, (CHUNK, HEAD_W), 0)
        db_heads = []
        for h in HEADS:
            hs = _head(h)
            e, ei, eb, ek, ebl = f["e"][:, hs], f["ei"][:, hs], f["eb"][:, hs], f["ek"][:, hs], f["ebl"][:, hs]
            dp_ref[:, OFF_GQ + h * HEAD_W:OFF_GQ + (h + 1) * HEAD_W] = (
                (dq_e[h] * e + dq_i[h] * ei + dqb[h] * eb) * GLA_SCALE)
            dp_ref[:, OFF_GK + h * HEAD_W:OFF_GK + (h + 1) * HEAD_W] = dk_e[h] * e + dk_i[h] * ei + dkb[h] * ek
            db = (dq_e[h] * f["q_e"][h] - dq_i[h] * f["q_i"][h] + dk_e[h] * f["k_e"][h] - dk_i[h] * f["k_i"][h]
                  + dqb[h] * f["qb"][h] - dkb[h] * f["kb"][h])
            db_last = _colsum(dkb[h] * f["kb"][h]) + ebl * _colsum(gla_state_t[h] * d_gla_new[h])
            db_heads.append(db + jnp.where(row_id == CHUNK - 1, db_last, 0.0))
        db = jnp.concatenate(db_heads, axis=1)
        d_la = _mm32((col >= row).astype(jnp.float32), db)
        d_logit = d_la * (1.0 / GATE_TAU) * (1.0 - _sigmoid(f["logit"]))
        dp_ref[:, OFF_LR:OFF_LR + HEAD_W] = _mm32_nt(d_logit, wg_ref[...])
        dwg_ref[...] += _mm32_tn(f["glr"], d_logit)
        dbg_ref[...] += _colsum(d_logit)

        @pl.when(pl.program_id(0) == last)
        def _():
            exchange.wait()

    state_blk = pl.BlockSpec((1, N_HEADS, HEAD_W, HEAD_W), lambda i: (last - i, 0, 0, 0))
    rot_blk = pl.BlockSpec((CHUNK, HEAD_W), lambda i: (last - i, 0))
    width = N_HEADS * HEAD_W
    vec_out = pl.BlockSpec((1, width), lambda i: (0, 0))
    hbm = pl.BlockSpec(memory_space=pl.ANY)
    return pl.pallas_call(
        body, name="mixer_bwd", grid=(n_chunks,),
        out_shape=(jax.ShapeDtypeStruct((seq, N_PROJ), jnp.float32),
                   jax.ShapeDtypeStruct((1, width), jnp.float32), jax.ShapeDtypeStruct((1, width), jnp.float32),
                   jax.ShapeDtypeStruct((HEAD_W, width), jnp.float32), jax.ShapeDtypeStruct((1, width), jnp.float32))
        + _exchange_out_shapes(riders, False),
        in_specs=[pl.BlockSpec((CHUNK, N_PROJ), lambda i: (last - i, 0)),
                  pl.BlockSpec((CHUNK, D_MODEL), lambda i: (last - i, 0)), state_blk, state_blk, rot_blk, rot_blk,
                  _const_spec(dm_t.shape), _const_spec(qdec_t.shape), _const_spec(kdec_t.shape),
                  _const_spec(wg_p.shape), _const_spec(bg_p.shape), _const_spec(ret_norm_w.shape),
                  _const_spec(gla_norm_w.shape)] + [hbm] * n_ride,
        out_specs=(pl.BlockSpec((CHUNK, N_PROJ), lambda i: (last - i, 0)), vec_out, vec_out,
                   pl.BlockSpec((HEAD_W, width), lambda i: (0, 0)), vec_out) + (hbm,) * n_ride,
        scratch_shapes=[pltpu.VMEM((N_HEADS, HEAD_W, HEAD_W), jnp.float32),
                        pltpu.VMEM((N_HEADS, HEAD_W, HEAD_W), jnp.float32)] + _exchange_sems(n_ride),
        compiler_params=_params(("arbitrary",)),
    )(proj, dmixed, rsave, ssave, cc_t, ss_t, dm_t, qdec_t, kdec_t, wg_p, bg_p, ret_norm_w, gla_norm_w, *riders)


V_GATE1, V_SCALE2, V_SHIFT2, V_GATE2, V_LN1W, V_LN1B, V_LN2W, V_LN2B = range(8)
S_GATE1, S_SCALE2, S_SHIFT2, S_GATE2, S_LN1W, S_LN1B, S_LN2W, S_LN2B, S_LOSS = range(9)


def _mlp_fwd_bwd(x2, mixed, target, vecs, w_out, w1_chunks, w2_chunks, tm):
    seq = x2.shape[0]
    n_fc, _, fc = w1_chunks.shape

    def body(x_ref, mx_ref, t_ref, vec_ref, wo_ref, w1_ref, w2_ref,
             dmx_ref, dxa_ref, a_ref, dh_ref, u2_ref, df_ref, dm_ref, sums_ref, relu_sc):
        @pl.when(pl.program_id(0) == 0)
        def _():
            sums_ref[...] = jnp.zeros_like(sums_ref)

        vec = lambda r: vec_ref[r:r + 1, :]

        def acc(r, val):
            sums_ref[r:r + 1, :] += _colsum(val)

        xx = x_ref[...]
        m = _mm(mx_ref[...], wo_ref[...])
        z1h, rstd1 = _ln(ALPHA * xx + vec(V_GATE1) * m)
        x1 = z1h * vec(V_LN1W) + vec(V_LN1B)
        x1h, rstd0 = _ln(x1)
        u2 = (x1h * (1.0 + vec(V_SCALE2)) + vec(V_SHIFT2)).astype(MXU_DTYPE)
        u2_ref[...] = u2
        f = jnp.zeros((tm, D_MODEL), jnp.float32)
        for j in range(n_fc):
            r = jnp.maximum(_mm(u2, w1_ref[j]), 0.0)
            relu_sc[:, j * fc:(j + 1) * fc] = r
            a = (r * r).astype(MXU_DTYPE)
            a_ref[:, j * fc:(j + 1) * fc] = a
            f = f + _mm(a, w2_ref[j])
        z2h, rstd2 = _ln(ALPHA * x1 + vec(V_GATE2) * f)
        err = z2h * vec(V_LN2W) + vec(V_LN2B) - t_ref[...]
        acc(S_LOSS, err * err)
        dy = err * (1.0 / D_MODEL)
        acc(S_LN2W, dy * z2h)
        acc(S_LN2B, dy)
        dz2 = _ln_bwd(dy * vec(V_LN2W), z2h, rstd2)
        acc(S_GATE2, dz2 * f)
        df = (vec(V_GATE2) * dz2).astype(MXU_DTYPE)
        df_ref[...] = df
        du2 = jnp.zeros((tm, D_MODEL), jnp.float32)
        for j in range(n_fc):
            dh = (_mm_nt(df, w2_ref[j]) * (2.0 * relu_sc[:, j * fc:(j + 1) * fc])).astype(MXU_DTYPE)
            dh_ref[:, j * fc:(j + 1) * fc] = dh
            du2 = du2 + _mm_nt(dh, w1_ref[j])
        acc(S_SCALE2, du2 * x1h)
        acc(S_SHIFT2, du2)
        dx1 = ALPHA * dz2 + _ln_bwd(du2 * (1.0 + vec(V_SCALE2)), x1h, rstd0)
        acc(S_LN1W, dx1 * z1h)
        acc(S_LN1B, dx1)
        dz1 = _ln_bwd(dx1 * vec(V_LN1W), z1h, rstd1)
        acc(S_GATE1, dz1 * m)
        dxa_ref[...] = ALPHA * dz1
        dm = (vec(V_GATE1) * dz1).astype(MXU_DTYPE)
        dm_ref[...] = dm
        dmx_ref[...] = _mm_nt(dm, wo_ref[...])

    tile = lambda width: pl.BlockSpec((tm, width), lambda i: (i, 0))
    f32 = lambda width: jax.ShapeDtypeStruct((seq, width), jnp.float32)
    b16 = lambda width: jax.ShapeDtypeStruct((seq, width), MXU_DTYPE)
    return pl.pallas_call(
        body, name="mlp_fwd_bwd", grid=(seq // tm,),
        out_shape=(f32(D_MODEL), f32(D_MODEL), b16(D_FF), b16(D_FF), b16(D_MODEL), b16(D_MODEL), b16(D_MODEL),
                   jax.ShapeDtypeStruct((16, D_MODEL), jnp.float32)),
        in_specs=[tile(D_MODEL), tile(D_MODEL), tile(D_MODEL), _const_spec(vecs.shape), _const_spec(w_out.shape),
                  _const_spec(w1_chunks.shape), _const_spec(w2_chunks.shape)],
        out_specs=(tile(D_MODEL), tile(D_MODEL), tile(D_FF), tile(D_FF), tile(D_MODEL), tile(D_MODEL),
                   tile(D_MODEL), pl.BlockSpec((16, D_MODEL), lambda i: (0, 0))),
        scratch_shapes=[pltpu.VMEM((tm, D_FF), jnp.float32)],
        compiler_params=_params(("arbitrary",)),
    )(x2, mixed, target, vecs, w_out, w1_chunks, w2_chunks)


def _grad_matmul(a, b, name, tn, blocks_are_rows):
    seq, m_dim = a.shape
    n_dim = b.shape[1]
    tk = min(seq, 512)
    nk = seq // tk
    if blocks_are_rows:
        tm = m_dim // N_CHIP
        assert tn == n_dim
        grid = (N_CHIP, 1, nk)
        out_map = lambda i, j, k: (i, 0, 0)
    else:
        tm = m_dim
        assert tn * N_CHIP == n_dim
        grid = (1, N_CHIP, nk)
        out_map = lambda i, j, k: (j, 0, 0)

    def body(a_ref, b_ref, o_ref, acc_sc):
        k = pl.program_id(2)

        @pl.when(k == 0)
        def _():
            acc_sc[...] = jnp.zeros_like(acc_sc)

        acc_sc[...] += _mm_tn(a_ref[...], b_ref[...])

        @pl.when(k == nk - 1)
        def _():
            o_ref[0] = acc_sc[...].astype(o_ref.dtype)

    return pl.pallas_call(
        body, name=name, grid=grid,
        out_shape=jax.ShapeDtypeStruct((N_CHIP, tm, tn), WIRE_DTYPE),
        in_specs=[pl.BlockSpec((tk, tm), lambda i, j, k: (k, i)), pl.BlockSpec((tk, tn), lambda i, j, k: (k, j))],
        out_specs=pl.BlockSpec((1, tm, tn), out_map),
        scratch_shapes=[pltpu.VMEM((tm, tn), jnp.float32)],
        compiler_params=_params(("arbitrary", "arbitrary", "arbitrary")),
    )(a, b)


def _grad_matmul_full(a, b, name, tn):
    seq, m_dim = a.shape
    n_dim = b.shape[1]
    tk = min(seq, 512)
    nk = seq // tk
    assert n_dim % tn == 0

    def body(a_ref, b_ref, o_ref):
        @pl.when(pl.program_id(1) == 0)
        def _():
            o_ref[...] = jnp.zeros_like(o_ref)

        o_ref[...] += _mm_tn(a_ref[...], b_ref[...])

    return pl.pallas_call(
        body, name=name, grid=(n_dim // tn, nk),
        out_shape=jax.ShapeDtypeStruct((m_dim, n_dim), jnp.float32),
        in_specs=[pl.BlockSpec((tk, m_dim), lambda j, k: (k, 0)), pl.BlockSpec((tk, tn), lambda j, k: (k, j))],
        out_specs=pl.BlockSpec((m_dim, tn), lambda j, k: (0, j)),
        compiler_params=_params(("arbitrary", "arbitrary")),
    )(a, b)


def _sum_chips(stack, name):
    _, rows, cols = stack.shape
    tr = min(rows, 256)

    def body(s_ref, o_ref):
        total = s_ref[0].astype(jnp.float32)
        for j in range(1, N_CHIP):
            total = total + s_ref[j].astype(jnp.float32)
        o_ref[...] = total

    return pl.pallas_call(
        body, name=name, grid=(rows // tr,),
        out_shape=jax.ShapeDtypeStruct((rows, cols), jnp.float32),
        in_specs=[pl.BlockSpec((N_CHIP, tr, cols), lambda i: (0, i, 0))],
        out_specs=pl.BlockSpec((tr, cols), lambda i: (i, 0)),
        compiler_params=_params(("arbitrary",)),
    )(stack)


def _adam_pair(w, g_mine, g_sibling, m, v, name):
    rows, cols = w.shape
    tr = min(rows, 256)

    def body(w_ref, ga_ref, gb_ref, m_ref, v_ref, g_ref, dl_ref, m2_ref, v2_ref):
        g = ga_ref[...] + gb_ref[...]
        delta, m2, v2 = _adam(w_ref[...], g, m_ref[...], v_ref[...])
        g_ref[...] = g
        dl_ref[...] = delta
        m2_ref[...] = m2
        v2_ref[...] = v2

    blk = pl.BlockSpec((tr, cols), lambda i: (i, 0))
    out = jax.ShapeDtypeStruct((rows, cols), jnp.float32)
    return pl.pallas_call(
        body, name=name, grid=(rows // tr,),
        out_shape=(out, out, out, out),
        in_specs=[blk] * 5, out_specs=(blk,) * 4,
        compiler_params=_params(("arbitrary",)),
    )(w, g_mine, g_sibling, m, v)


def _sum_devices(gathered):
    _, rows, _ = gathered.shape

    def body(g_ref, o_ref):
        total = g_ref[0]
        for d in range(1, N_DEV):
            total = total + g_ref[d]
        o_ref[...] = total

    return pl.pallas_call(
        body, name="sum_devices",
        out_shape=jax.ShapeDtypeStruct((rows, 128), jnp.float32),
    )(gathered)


def _adam_small(w, g, m, v):
    def body(w_ref, g_ref, m_ref, v_ref, dl_ref, m2_ref, v2_ref):
        delta, m2, v2 = _adam(w_ref[...], g_ref[...], m_ref[...], v_ref[...])
        dl_ref[...] = delta
        m2_ref[...] = m2
        v2_ref[...] = v2

    out = jax.ShapeDtypeStruct(w.shape, jnp.float32)
    return pl.pallas_call(body, name="adam_small", out_shape=(out, out, out))(w, g, m, v)


def _pad_heads(w):
    lead = w.shape[:-1]
    w = w.reshape(lead + (N_HEADS, GLA_DK))
    w = jnp.pad(w, [(0, 0)] * len(lead) + [(0, 0), (0, HEAD_W - GLA_DK)])
    return w.reshape(lead + (N_HEADS * HEAD_W,))


def _unpad_heads(w):
    lead = w.shape[:-1]
    return w.reshape(lead + (N_HEADS, HEAD_W))[..., :GLA_DK].reshape(lead + (N_HEADS * GLA_DK,))


def _pad_w_in(w):
    return jnp.concatenate([
        w[:, :2048], _pad_heads(w[:, 2048:2304]), _pad_heads(w[:, 2304:2560]), w[:, 2560:3584],
        jnp.pad(w[:, 3584:3600], ((0, 0), (0, HEAD_W - GATE_RANK)))], axis=1)


def _unpad_w_in(g):
    return jnp.concatenate([
        g[:, :2048], _unpad_heads(g[:, OFF_GQ:OFF_GQ + 512]), _unpad_heads(g[:, OFF_GK:OFF_GK + 512]),
        g[:, OFF_GV:OFF_LR], g[:, OFF_LR:OFF_LR + GATE_RANK]], axis=1)


def _rows128(a):
    return a.reshape(-1, 128)


def _rows8(a):
    a = a.reshape(-1, 128)
    return jnp.pad(a, ((0, -a.shape[0] % 8), (0, 0)))


def kernel(x, c, w_ada, b_ada, w_in, ret_norm_w, gla_gate_w, gla_gate_b, gla_norm_w, w_out, ln1_w, ln1_b, w_ff1, w_ff2, ln2_w, ln2_b, loss_target, m_w_ada, m_b_ada, m_w_in, m_ret_norm_w, m_gla_gate_w, m_gla_gate_b, m_gla_norm_w, m_w_out, m_ln1_w, m_ln1_b, m_w_ff1, m_w_ff2, m_ln2_w, m_ln2_b, v_w_ada, v_b_ada, v_w_in, v_ret_norm_w, v_gla_gate_w, v_gla_gate_b, v_gla_norm_w, v_w_out, v_ln1_w, v_ln1_b, v_w_ff1, v_w_ff2, v_ln2_w, v_ln2_b):
    seq = x.shape[1]
    tm = min(seq, TOKEN_TILE)
    xi, yi, ci = _mesh_pos()
    dev = 4 * xi + 2 * yi + ci
    chip = 2 * xi + yi
    x2, target = x[0], loss_target[0]
    ada_cols = w_ada.shape[2]
    in_cols = w_in.shape[2]
    gate_cols = gla_gate_w.shape[2]

    g0 = _gather_rows(jnp.concatenate([_rows128(c), _rows128(gla_gate_w[0])], axis=0), "gather_cond")
    c_all = g0[:, :8].reshape(N_DEV, D_MODEL)
    gate_w_full = jnp.concatenate([g0[2 * j, 8:16].reshape(GATE_RANK, gate_cols) for j in range(N_CHIP)], axis=1)
    wg_p = jnp.pad(_pad_heads(gate_w_full), ((0, HEAD_W - GATE_RANK), (0, 0)))
    bg_p = _pad_heads(gla_gate_b)

    b_blk = lax.dynamic_slice(b_ada, (0, chip * ada_cols), (1, ada_cols))
    mod_blk = _ada_fwd(c_all, w_ada[0], b_blk)
    g1 = _gather_rows(_rows128(mod_blk), "gather_mod")
    mod_all = jnp.concatenate([g1[2 * j].reshape(N_DEV, ada_cols) for j in range(N_CHIP)], axis=1)
    mod = lax.dynamic_slice(mod_all, (dev, 0), (1, 6 * D_MODEL))
    shift1, scale1, gate1, shift2, scale2, gate2 = [mod[:, i * D_MODEL:(i + 1) * D_MODEL] for i in range(6)]

    (w_in_stack,) = _chip_exchange([w_in[0].astype(WIRE_DTYPE)], "gather_w_in", True)
    w_in_p = _pad_w_in(jnp.transpose(w_in_stack, (1, 0, 2)).reshape(D_MODEL, N_PROJ_SRC)).astype(MXU_DTYPE)

    zeros_row = jnp.zeros((1, D_MODEL), jnp.float32)
    vecs1 = jnp.concatenate([shift1, scale1] + [zeros_row] * 6, axis=0)
    proj, u = _inproj_fwd(x2, vecs1, w_in_p, tm)
    cc_t, ss_t = _rotary_tables(seq)
    dm_t, qdec_t, kdec_t, chunk_decay = _decay_tables()
    tables = (cc_t, ss_t, dm_t, qdec_t, kdec_t, chunk_decay)
    mixed, rsave, ssave, w_out_stack, w1_stack, w2_stack = _mixer_fwd(
        proj, tables, wg_p, bg_p, ret_norm_w, gla_norm_w,
        [w_out[0].astype(WIRE_DTYPE), w_ff1[0].astype(WIRE_DTYPE), w_ff2[0].astype(WIRE_DTYPE)])
    w_out_full = w_out_stack.reshape(D_MODEL, D_MODEL).astype(MXU_DTYPE)
    w1_chunks = w1_stack.astype(MXU_DTYPE)
    w2_chunks = w2_stack.astype(MXU_DTYPE)

    vecs2 = jnp.concatenate([gate1, scale2, shift2, gate2, ln1_w, ln1_b, ln2_w, ln2_b], axis=0)
    dmixed, dxa, act, dh, u2, df, dm, sums2 = _mlp_fwd_bwd(x2, mixed, target, vecs2, w_out_full, w1_chunks,
                                                           w2_chunks, tm)

    g_out_stack = _grad_matmul(mixed, dm, "grad_w_out", D_MODEL, True)
    g_ff1_stack = _grad_matmul(u2, dh, "grad_w_ff1", D_FF // N_CHIP, False)
    g_ff2_stack = _grad_matmul(act, df, "grad_w_ff2", D_MODEL, True)
    dproj, d_ret_norm, d_gla_norm, d_wg_p, d_bg_p, r_out, r_ff1, r_ff2 = _mixer_bwd(
        proj, dmixed, rsave, ssave, tables, wg_p, bg_p, ret_norm_w, gla_norm_w,
        [g_out_stack, g_ff1_stack, g_ff2_stack])
    g_in_full = _grad_matmul_full(u, dproj, "grad_w_in", 1408)
    g_in_stack = jnp.transpose(_unpad_w_in(g_in_full).reshape(D_MODEL, N_CHIP, in_cols), (1, 0, 2)).astype(WIRE_DTYPE)
    grad_x, sums1, r_in = _inproj_bwd(dproj, x2, dxa, vecs1, w_in_p, tm, [g_in_stack])

    dmod = jnp.concatenate([sums1[0:1], sums1[1:2], sums2[S_GATE1:S_GATE1 + 1], sums2[S_SHIFT2:S_SHIFT2 + 1],
                            sums2[S_SCALE2:S_SCALE2 + 1], sums2[S_GATE2:S_GATE2 + 1]], axis=1)
    d_gate_w_full = _unpad_heads(d_wg_p[:GATE_RANK])
    flat = lambda parts: jnp.concatenate([_rows8(p) for p in parts], axis=0)
    small = flat([dmod, sums2[S_LN1W:S_LN1W + 1], sums2[S_LN1B:S_LN1B + 1], sums2[S_LN2W:S_LN2W + 1],
                  sums2[S_LN2B:S_LN2B + 1], d_ret_norm, _unpad_heads(d_bg_p), d_gla_norm, d_gate_w_full,
                  sums2[S_LOSS:S_LOSS + 1]])
    g2 = _gather_rows(small, "gather_small")
    tot = _sum_devices(g2)
    loss = 0.5 / D_MODEL * jnp.sum(tot[136:144])
    grad_b_ada = tot[0:48].reshape(1, 6 * D_MODEL)
    grad_ln1_w, grad_ln1_b = tot[48:56].reshape(1, D_MODEL), tot[56:64].reshape(1, D_MODEL)
    grad_ln2_w, grad_ln2_b = tot[64:72].reshape(1, D_MODEL), tot[72:80].reshape(1, D_MODEL)
    grad_ret_norm = tot[80:84].reshape(1, 512)
    grad_gate_b = tot[88:90].reshape(1, 256)
    grad_gla_norm = tot[96:100].reshape(1, 512)
    grad_gate_w = lax.dynamic_slice(tot[104:136].reshape(GATE_RANK, 256), (0, chip * gate_cols),
                                    (GATE_RANK, gate_cols))

    small_w = flat([b_ada, ln1_w, ln1_b, ln2_w, ln2_b, ret_norm_w, gla_gate_b, gla_norm_w, gla_gate_w[0]])
    small_g = flat([grad_b_ada, grad_ln1_w, grad_ln1_b, grad_ln2_w, grad_ln2_b, grad_ret_norm, grad_gate_b,
                    grad_gla_norm, grad_gate_w])
    small_m = flat([m_b_ada, m_ln1_w, m_ln1_b, m_ln2_w, m_ln2_b, m_ret_norm_w, m_gla_gate_b, m_gla_norm_w,
                    m_gla_gate_w[0]])
    small_v = flat([v_b_ada, v_ln1_w, v_ln1_b, v_ln2_w, v_ln2_b, v_ret_norm_w, v_gla_gate_b, v_gla_norm_w,
                    v_gla_gate_w[0]])
    small_out = _adam_small(small_w, small_g, small_m, small_v)

    def unflat(t):
        pieces, row = [], 0
        for shape in [(1, 6 * D_MODEL)] + [(1, D_MODEL)] * 4 + [(1, 512), (1, 256), (1, 512), (1, GATE_RANK, gate_cols)]:
            n = int(np.prod(shape)) // 128
            pieces.append(t[row:row + n].reshape(shape))
            row += -(-n // 8) * 8
        return pieces

    sm_delta, sm_m, sm_v = [unflat(t) for t in small_out]

    dmod_all = g2[:, 0:48].reshape(N_DEV, 6 * D_MODEL)
    dmod_blk = lax.dynamic_slice(dmod_all, (0, chip * ada_cols), (N_DEV, ada_cols))
    ada_out = _ada_bwd_adam(jnp.transpose(c_all), dmod_blk, w_ada[0], m_w_ada[0], v_w_ada[0])
    ada_g, ada_delta, ada_m, ada_v = [t[None] for t in ada_out]

    received = [r_in, r_out, r_ff1, r_ff2]
    names = ["w_in", "w_out", "w_ff1", "w_ff2"]
    partial = [_sum_chips(r, "sum_" + n) for r, n in zip(received, names)]
    swapped = _sibling_swap(partial, "swap_partials")
    big = {}
    for n, w, mine, theirs, m, v in zip(names, [w_in, w_out, w_ff1, w_ff2], partial, swapped,
                                        [m_w_in, m_w_out, m_w_ff1, m_w_ff2], [v_w_in, v_w_out, v_w_ff1, v_w_ff2]):
        big[n] = [t[None] for t in _adam_pair(w[0], mine, theirs, m[0], v[0], "adam_" + n)]

    def assemble(ada, smalls, k):
        b_ada_o, ln1w_o, ln1b_o, ln2w_o, ln2b_o, ret_o, gb_o, gln_o, gw_o = smalls
        return [ada, b_ada_o, big["w_in"][k], ret_o, gw_o, gb_o, gln_o, big["w_out"][k], ln1w_o, ln1b_o,
                big["w_ff1"][k], big["w_ff2"][k], ln2w_o, ln2b_o]

    small_grads = [grad_b_ada, grad_ln1_w, grad_ln1_b, grad_ln2_w, grad_ln2_b, grad_ret_norm, grad_gate_b,
                   grad_gla_norm, grad_gate_w[None]]
    grads = assemble(ada_g, small_grads, 0)
    deltas = assemble(ada_delta, sm_delta, 1)
    new_m = assemble(ada_m, sm_m, 2)
    new_v = assemble(ada_v, sm_v, 3)
    return (loss, grad_x[None], *grads, *deltas, *new_m, *new_v)
```

```python
import functools

import numpy as np
import jax
import jax.numpy as jnp
from jax import lax
from jax.experimental import pallas as pl
from jax.experimental.pallas import tpu as pltpu

D_MODEL = 1024
D_FF = 4096
CHUNK = 64
N_HEADS = 4
HEAD_W = 128
GLA_DK = 64
GATE_RANK = 16
GATE_TAU = 16.0
LN_EPS = 1e-5
ALPHA = 2.0 ** 0.25
ROPE_BASE = 10000.0
RET_SCALE = float(HEAD_W) ** -0.5
GLA_SCALE = float(GLA_DK) ** -0.5

ADAM_LR = 0.001
ADAM_B1 = 0.9
ADAM_B2 = 0.999
ADAM_EPS = 1e-08
ADAM_WD = 0.01
ADAM_STEP = 10

OFF_RQ, OFF_RK, OFF_RV, OFF_RG = 0, 512, 1024, 1536
OFF_GQ, OFF_GK, OFF_GV, OFF_GG, OFF_LR = 2048, 2560, 3072, 3584, 4096
N_PROJ = 4224
N_PROJ_SRC = 3600

N_DEV = 8
N_CHIP = 4
MESH = pl.DeviceIdType.MESH
MXU_DTYPE = jnp.bfloat16
WIRE_DTYPE = jnp.bfloat16
VMEM_LIMIT = 60 * 1024 * 1024
TOKEN_TILE = 256
HIGHEST = lax.Precision.HIGHEST


def _mm(a, b):
    return jnp.dot(a.astype(MXU_DTYPE), b.astype(MXU_DTYPE), preferred_element_type=jnp.float32)


def _mm_nt(a, b):
    return lax.dot_general(a.astype(MXU_DTYPE), b.astype(MXU_DTYPE), (((1,), (1,)), ((), ())),
                           preferred_element_type=jnp.float32)


def _mm_tn(a, b):
    return lax.dot_general(a.astype(MXU_DTYPE), b.astype(MXU_DTYPE), (((0,), (0,)), ((), ())),
                           preferred_element_type=jnp.float32)


def _mm32(a, b):
    return jnp.dot(a, b, precision=HIGHEST, preferred_element_type=jnp.float32)


def _mm32_nt(a, b):
    return lax.dot_general(a, b, (((1,), (1,)), ((), ())), precision=HIGHEST, preferred_element_type=jnp.float32)


def _mm32_tn(a, b):
    return lax.dot_general(a, b, (((0,), (0,)), ((), ())), precision=HIGHEST, preferred_element_type=jnp.float32)


def _rowmean(a):
    return jnp.mean(a, axis=-1, keepdims=True)


def _colsum(a):
    return jnp.sum(a, axis=0, keepdims=True)


def _ln(z):
    zc = z - _rowmean(z)
    rstd = lax.rsqrt(_rowmean(zc * zc) + LN_EPS)
    return zc * rstd, rstd


def _ln_bwd(dzh, zh, rstd):
    return rstd * (dzh - _rowmean(dzh) - zh * _rowmean(dzh * zh))


def _sigmoid(a):
    return 1.0 / (1.0 + jnp.exp(-a))


def _log_sigmoid(a):
    return jnp.minimum(a, 0.0) - jnp.log(1.0 + jnp.exp(-jnp.abs(a)))


def _swap_halves(a):
    return pltpu.roll(a, HEAD_W // 2, 1)


def _tri_masks():
    row = lax.broadcasted_iota(jnp.int32, (CHUNK, CHUNK), 0)
    col = lax.broadcasted_iota(jnp.int32, (CHUNK, CHUNK), 1)
    return row, col


def _const_spec(shape):
    zeros = (0,) * len(shape)
    return pl.BlockSpec(shape, lambda *_: zeros, pipeline_mode=pl.Buffered(1))


def _params(semantics):
    return pltpu.CompilerParams(dimension_semantics=semantics, vmem_limit_bytes=VMEM_LIMIT)


def _decay_tables():
    log_gamma = np.log(1.0 - 2.0 ** (-5.0 - np.arange(N_HEADS, dtype=np.float64)))
    idx = np.arange(CHUNK, dtype=np.float64)
    dist = np.abs(idx[:, None] - idx[None, :])
    intra = np.exp(log_gamma[:, None, None] * dist)
    kdec = np.exp(log_gamma[None, :] * (CHUNK - 1.0 - idx)[:, None])
    qdec = np.exp(log_gamma[None, :] * (idx + 1.0)[:, None])
    chunk_decay = np.exp(log_gamma * CHUNK)
    lanes = lambda t: np.repeat(t, HEAD_W, axis=1).astype(np.float32)
    return (jnp.asarray(intra.astype(np.float32)), jnp.asarray(lanes(qdec)), jnp.asarray(lanes(kdec)),
            [float(np.float32(v)) for v in chunk_decay])


def _rotary_tables(seq):
    half = HEAD_W // 2
    inv = 1.0 / (ROPE_BASE ** jnp.linspace(0.0, 1.0, half, dtype=jnp.float32))
    both = lambda t: jnp.concatenate([t, t], axis=-1)
    ang_a = jnp.arange(0, seq, CHUNK, dtype=jnp.float32)[:, None] * inv[None, :]
    rot_a = jnp.stack([both(jnp.cos(ang_a)), both(jnp.sin(ang_a))], axis=1)
    rot_a = jnp.pad(rot_a, ((0, 0), (0, 6), (0, 0)))
    ang_b = jnp.arange(CHUNK, dtype=jnp.float32)[:, None] * inv[None, :]
    cos_b, sin_b = both(jnp.cos(ang_b)), both(jnp.sin(ang_b))
    sign = jnp.concatenate([-jnp.ones((half,), jnp.float32), jnp.ones((half,), jnp.float32)])
    return rot_a, jnp.stack([cos_b, sin_b, cos_b * sign, sin_b * sign])


def _rotary_chunk(ra_ref, rb_ref):
    cos_a, sin_a = ra_ref[0, 0:1, :], ra_ref[0, 1:2, :]
    return cos_a * rb_ref[0] - sin_a * rb_ref[1], sin_a * rb_ref[2] + cos_a * rb_ref[3]


def _mesh_pos():
    return lax.axis_index("x"), lax.axis_index("y"), lax.axis_index("c")


def _flip(v, bit):
    return 1 - v if bit else v


def _gather_rows(v, name):
    rows = v.shape[0]

    def body(v_ref, out_ref, send_sems, recv_sems):
        x, y, c = _mesh_pos()
        me = 4 * x + 2 * y + c
        out_ref[me] = v_ref[...]
        sends, recvs = [], []
        for k in range(1, N_DEV):
            px, py, pc = _flip(x, (k >> 2) & 1), _flip(y, (k >> 1) & 1), _flip(c, k & 1)
            peer = 4 * px + 2 * py + pc
            sends.append(pltpu.make_async_remote_copy(
                src_ref=v_ref, dst_ref=out_ref.at[me], send_sem=send_sems.at[k - 1], recv_sem=recv_sems.at[k - 1],
                device_id=(px, py, pc), device_id_type=MESH))
            recvs.append(pltpu.make_async_remote_copy(
                src_ref=v_ref, dst_ref=out_ref.at[peer], send_sem=send_sems.at[k - 1], recv_sem=recv_sems.at[k - 1],
                device_id=(px, py, pc), device_id_type=MESH))
        for cp in sends:
            cp.start()
        for cp in recvs:
            cp.wait_recv()
        for cp in sends:
            cp.wait_send()

    return pl.pallas_call(
        body, name=name,
        out_shape=jax.ShapeDtypeStruct((N_DEV, rows, 128), jnp.float32),
        in_specs=[pl.BlockSpec(memory_space=pltpu.VMEM)],
        out_specs=pl.BlockSpec(memory_space=pltpu.VMEM),
        scratch_shapes=[pltpu.SemaphoreType.DMA((N_DEV - 1,)), pltpu.SemaphoreType.DMA((N_DEV - 1,))],
    )(v)


def _chip_exchange(arrays, name, gather):
    n = len(arrays)

    def body(*refs):
        exchange = _ChipExchange(refs[:n], refs[n:2 * n], refs[2 * n:], gather)
        exchange.start()
        exchange.wait()

    return pl.pallas_call(
        body, name=name,
        out_shape=_exchange_out_shapes(arrays, gather),
        in_specs=[pl.BlockSpec(memory_space=pl.ANY)] * n,
        out_specs=tuple(pl.BlockSpec(memory_space=pl.ANY) for _ in arrays),
        scratch_shapes=_exchange_sems(n),
    )(*arrays)


def _exchange_out_shapes(arrays, gather):
    return tuple(jax.ShapeDtypeStruct((N_CHIP,) + a.shape if gather else a.shape, a.dtype) for a in arrays)


def _exchange_sems(n):
    n_sem = n * (N_CHIP - 1)
    return [pltpu.SemaphoreType.DMA((n_sem,)), pltpu.SemaphoreType.DMA((n_sem,)), pltpu.SemaphoreType.DMA((n,))]


class _ChipExchange:
    def __init__(self, ins, outs, sems, gather):
        send_sems, recv_sems, local_sems = sems
        x, y, c = _mesh_pos()
        chip = 2 * x + y
        self.local, self.sends, self.recvs = [], [], []
        for i in range(len(ins)):
            src = ins[i] if gather else ins[i].at[chip]
            self.local.append(pltpu.make_async_copy(src, outs[i].at[chip], local_sems.at[i]))
            for k in range(1, N_CHIP):
                px, py = _flip(x, (k >> 1) & 1), _flip(y, k & 1)
                peer_chip = 2 * px + py
                sem = i * (N_CHIP - 1) + k - 1
                src = ins[i] if gather else ins[i].at[peer_chip]
                self.sends.append(pltpu.make_async_remote_copy(
                    src_ref=src, dst_ref=outs[i].at[chip], send_sem=send_sems.at[sem], recv_sem=recv_sems.at[sem],
                    device_id=(px, py, c), device_id_type=MESH))
                self.recvs.append(pltpu.make_async_remote_copy(
                    src_ref=src, dst_ref=outs[i].at[peer_chip], send_sem=send_sems.at[sem], recv_sem=recv_sems.at[sem],
                    device_id=(px, py, c), device_id_type=MESH))

    def start(self):
        for cp in self.local + self.sends:
            cp.start()

    def wait(self):
        for cp in self.recvs:
            cp.wait_recv()
        for cp in self.sends:
            cp.wait_send()
        for cp in self.local:
            cp.wait()


def _sibling_swap(arrays, name):
    n = len(arrays)

    def body(*refs):
        ins, outs = refs[:n], refs[n:2 * n]
        send_sems, recv_sems = refs[2 * n:]
        x, y, c = _mesh_pos()
        copies = [pltpu.make_async_remote_copy(
            src_ref=ins[i], dst_ref=outs[i], send_sem=send_sems.at[i], recv_sem=recv_sems.at[i],
            device_id=(x, y, 1 - c), device_id_type=MESH) for i in range(n)]
        for cp in copies:
            cp.start()
        for cp in copies:
            cp.wait_recv()
        for cp in copies:
            cp.wait_send()

    return pl.pallas_call(
        body, name=name,
        out_shape=tuple(jax.ShapeDtypeStruct(a.shape, a.dtype) for a in arrays),
        in_specs=[pl.BlockSpec(memory_space=pl.ANY)] * n,
        out_specs=tuple(pl.BlockSpec(memory_space=pl.ANY) for _ in arrays),
        scratch_shapes=[pltpu.SemaphoreType.DMA((n,)), pltpu.SemaphoreType.DMA((n,))],
    )(*arrays)


def _ada_fwd(c_all, w_ada_blk, b_blk):
    cols = w_ada_blk.shape[1]

    def body(c_ref, w_ref, b_ref, out_ref):
        cv = c_ref[...]
        out_ref[...] = _mm32(cv * _sigmoid(cv), w_ref[...]) + b_ref[...]

    return pl.pallas_call(
        body, name="ada_fwd",
        out_shape=jax.ShapeDtypeStruct((N_DEV, cols), jnp.float32),
        compiler_params=pltpu.CompilerParams(vmem_limit_bytes=VMEM_LIMIT),
    )(c_all, w_ada_blk, b_blk)


def _adam(w, g, m, v):
    m2 = ADAM_B1 * m + (1.0 - ADAM_B1) * g
    v2 = ADAM_B2 * v + (1.0 - ADAM_B2) * (g * g)
    m_hat = m2 / (1.0 - ADAM_B1 ** ADAM_STEP)
    v_hat = v2 / (1.0 - ADAM_B2 ** ADAM_STEP)
    delta = -ADAM_LR * (m_hat / (jnp.sqrt(v_hat) + ADAM_EPS) + ADAM_WD * w)
    return delta, m2, v2


def _ada_bwd_adam(c_t, dmod_blk, w, m, v):
    rows, cols = w.shape
    tile = 512
    assert cols % tile == 0

    def body(c_ref, d_ref, w_ref, m_ref, v_ref, g_ref, dl_ref, m2_ref, v2_ref):
        sc = c_ref[...]
        sc = sc * _sigmoid(sc)
        dm = d_ref[...]
        g = sc[:, 0:1] * dm[0:1, :]
        for b in range(1, N_DEV):
            g = g + sc[:, b:b + 1] * dm[b:b + 1, :]
        delta, m2, v2 = _adam(w_ref[...], g, m_ref[...], v_ref[...])
        g_ref[...] = g
        dl_ref[...] = delta
        m2_ref[...] = m2
        v2_ref[...] = v2

    blk = pl.BlockSpec((rows, tile), lambda j: (0, j))
    out = jax.ShapeDtypeStruct((rows, cols), jnp.float32)
    return pl.pallas_call(
        body, name="ada_bwd_adam", grid=(cols // tile,),
        out_shape=(out, out, out, out),
        in_specs=[pl.BlockSpec((rows, N_DEV), lambda j: (0, 0)), pl.BlockSpec((N_DEV, tile), lambda j: (0, j)),
                  blk, blk, blk],
        out_specs=(blk, blk, blk, blk),
        compiler_params=_params(("arbitrary",)),
    )(c_t, dmod_blk, w, m, v)


def _inproj_fwd(x2, vecs, w_in_p, tm):
    seq = x2.shape[0]

    def body(x_ref, vec_ref, w_ref, p_ref, u_ref):
        xh, _ = _ln(x_ref[...])
        u = (xh * (1.0 + vec_ref[1:2, :]) + vec_ref[0:1, :]).astype(MXU_DTYPE)
        u_ref[...] = u
        p_ref[...] = _mm(u, w_ref[...])

    return pl.pallas_call(
        body, name="inproj_fwd", grid=(seq // tm,),
        out_shape=(jax.ShapeDtypeStruct((seq, N_PROJ), jnp.float32), jax.ShapeDtypeStruct((seq, D_MODEL), MXU_DTYPE)),
        in_specs=[pl.BlockSpec((tm, D_MODEL), lambda i: (i, 0)), _const_spec(vecs.shape), _const_spec(w_in_p.shape)],
        out_specs=(pl.BlockSpec((tm, N_PROJ), lambda i: (i, 0)), pl.BlockSpec((tm, D_MODEL), lambda i: (i, 0))),
        compiler_params=_params(("arbitrary",)),
    )(x2, vecs, w_in_p)


def _inproj_bwd(dproj, x2, dxa, vecs, w_in_p, tm, riders):
    seq = x2.shape[0]
    n_tiles = seq // tm
    n_ride = len(riders)

    def body(*refs):
        dp_ref, x_ref, dxa_ref, vec_ref, w_ref = refs[:5]
        ride_in, refs = refs[5:5 + n_ride], refs[5 + n_ride:]
        gx_ref, sums_ref = refs[:2]
        ride_out, sems = refs[2:2 + n_ride], refs[2 + n_ride:]
        exchange = _ChipExchange(ride_in, ride_out, sems, False)

        @pl.when(pl.program_id(0) == 0)
        def _():
            exchange.start()
            sums_ref[...] = jnp.zeros_like(sums_ref)

        du = _mm_nt(dp_ref[...], w_ref[...])
        xh, rstd = _ln(x_ref[...])
        sums_ref[0:1, :] += _colsum(du)
        sums_ref[1:2, :] += _colsum(du * xh)
        gx_ref[...] = dxa_ref[...] + _ln_bwd(du * (1.0 + vec_ref[1:2, :]), xh, rstd)

        @pl.when(pl.program_id(0) == n_tiles - 1)
        def _():
            exchange.wait()

    tile = pl.BlockSpec((tm, D_MODEL), lambda i: (i, 0))
    hbm = pl.BlockSpec(memory_space=pl.ANY)
    return pl.pallas_call(
        body, name="inproj_bwd", grid=(n_tiles,),
        out_shape=(jax.ShapeDtypeStruct((seq, D_MODEL), jnp.float32), jax.ShapeDtypeStruct((8, D_MODEL), jnp.float32))
        + _exchange_out_shapes(riders, False),
        in_specs=[pl.BlockSpec((tm, N_PROJ), lambda i: (i, 0)), tile, tile, _const_spec(vecs.shape),
                  _const_spec(w_in_p.shape)] + [hbm] * n_ride,
        out_specs=(tile, pl.BlockSpec((8, D_MODEL), lambda i: (0, 0))) + (hbm,) * n_ride,
        scratch_shapes=_exchange_sems(n_ride),
        compiler_params=_params(("arbitrary",)),
    )(dproj, x2, dxa, vecs, w_in_p, *riders)


def _head(h):
    return slice(h * HEAD_W, (h + 1) * HEAD_W)


def _cols(ref, off, h):
    return ref[:, off + h * HEAD_W:off + (h + 1) * HEAD_W]


HEADS = range(N_HEADS)


def _mixer_chunk_forward(p_ref, cc, ss, dm_ref, qdec_ref, kdec_ref, wg_ref, bg_ref, ret_state, gla_state_t):
    row, col = _tri_masks()
    lower = row >= col
    f = {}
    f["glr"] = p_ref[:, OFF_LR:OFF_LR + HEAD_W]
    f["logit"] = _mm32(f["glr"], wg_ref[...]) + bg_ref[...]
    rq = [_cols(p_ref, OFF_RQ, h) for h in HEADS]
    rk = [_cols(p_ref, OFF_RK, h) for h in HEADS]
    f["rv"] = [_cols(p_ref, OFF_RV, h) for h in HEADS]
    f["qr"] = [(rq[h] * cc + _swap_halves(rq[h]) * ss) * RET_SCALE for h in HEADS]
    f["kr"] = [rk[h] * cc + _swap_halves(rk[h]) * ss for h in HEADS]
    s_raw = [_mm_nt(f["qr"][h], f["kr"][h]) for h in HEADS]
    la = _log_sigmoid(f["logit"]) * (1.0 / GATE_TAU)
    b = _mm32(lower.astype(jnp.float32), la)
    f["qd"] = [f["qr"][h] * qdec_ref[:, _head(h)] for h in HEADS]
    f["kd"] = [f["kr"][h] * kdec_ref[:, _head(h)] for h in HEADS]
    f["scores"] = [s_raw[h] * dm_ref[h] for h in HEADS]
    f["o_ret"] = [_mm(f["scores"][h], f["rv"][h]) + _mm(f["qd"][h], ret_state[h]) for h in HEADS]
    b_last = b[CHUNK - 1:CHUNK, :]
    b_mid = b[CHUNK // 2 - 1:CHUNK // 2, :]
    f["e"], f["ei"] = jnp.exp(b - b_mid), jnp.exp(b_mid - b)
    f["eb"], f["ek"], f["ebl"] = jnp.exp(b), jnp.exp(b_last - b), jnp.exp(b_last)
    gq = [_cols(p_ref, OFF_GQ, h) * GLA_SCALE for h in HEADS]
    gk = [_cols(p_ref, OFF_GK, h) for h in HEADS]
    f["gv"] = [_cols(p_ref, OFF_GV, h) for h in HEADS]
    f["q_e"] = [gq[h] * f["e"][:, _head(h)] for h in HEADS]
    f["q_i"] = [gq[h] * f["ei"][:, _head(h)] for h in HEADS]
    f["k_e"] = [gk[h] * f["e"][:, _head(h)] for h in HEADS]
    f["k_i"] = [gk[h] * f["ei"][:, _head(h)] for h in HEADS]
    low = [_mm_nt(f["q_e"][h], f["k_i"][h]) for h in HEADS]
    up = [_mm_nt(f["q_i"][h], f["k_e"][h]) for h in HEADS]
    f["att"] = [jnp.where(lower, low[h], up[h]) for h in HEADS]
    f["qb"] = [gq[h] * f["eb"][:, _head(h)] for h in HEADS]
    f["kb"] = [gk[h] * f["ek"][:, _head(h)] for h in HEADS]
    f["o_gla"] = [_mm(f["att"][h], f["gv"][h]) + _mm_nt(f["qb"][h], gla_state_t[h]) for h in HEADS]
    return f


def _mixer_fwd(proj, tables, wg_p, bg_p, ret_norm_w, gla_norm_w, riders):
    seq = proj.shape[0]
    n_chunks = seq // CHUNK
    n_ride = len(riders)
    rot_a, rot_b, dm_t, qdec_t, kdec_t, chunk_decay = tables

    def body(*refs):
        p_ref, ra_ref, rb_ref, dm_ref, qdec_ref, kdec_ref, wg_ref, bg_ref, wr_ref, wl_ref = refs[:10]
        ride_in, refs = refs[10:10 + n_ride], refs[10 + n_ride:]
        mix_ref, rsave_ref, ssave_ref = refs[:3]
        ride_out, refs = refs[3:3 + n_ride], refs[3 + n_ride:]
        r_sc, s_sc = refs[:2]
        exchange = _ChipExchange(ride_in, ride_out, refs[2:], True)

        @pl.when(pl.program_id(0) == 0)
        def _():
            exchange.start()
            r_sc[...] = jnp.zeros_like(r_sc)
            s_sc[...] = jnp.zeros_like(s_sc)

        ret_state = [r_sc[h] for h in HEADS]
        gla_state_t = [s_sc[h] for h in HEADS]
        for h in HEADS:
            rsave_ref[0, h] = ret_state[h]
            ssave_ref[0, h] = gla_state_t[h]
        cc, ss = _rotary_chunk(ra_ref, rb_ref)
        f = _mixer_chunk_forward(p_ref, cc, ss, dm_ref, qdec_ref, kdec_ref, wg_ref, bg_ref, ret_state, gla_state_t)
        for h in HEADS:
            r_sc[h] = chunk_decay[h] * ret_state[h] + _mm_tn(f["kd"][h], f["rv"][h])
        for h in HEADS:
            s_sc[h] = gla_state_t[h] * f["ebl"][:, _head(h)] + _mm_tn(f["gv"][h], f["kb"][h])
        for h in HEADS:
            on, _ = _ln(f["o_ret"][h])
            g = _cols(p_ref, OFF_RG, h)
            mix_ref[:, _head(h)] = (on * wr_ref[:, _head(h)] * (g * _sigmoid(g))).astype(mix_ref.dtype)
        for h in HEADS:
            o = f["o_gla"][h]
            on = o * lax.rsqrt(_rowmean(o * o) + LN_EPS)
            g = _cols(p_ref, OFF_GG, h)
            mix_ref[:, _head(N_HEADS + h)] = (on * wl_ref[:, _head(h)] * (g * _sigmoid(g))).astype(mix_ref.dtype)

        @pl.when(pl.program_id(0) == n_chunks - 1)
        def _():
            exchange.wait()

    state_shape = (n_chunks, N_HEADS, HEAD_W, HEAD_W)
    state_blk = pl.BlockSpec((1, N_HEADS, HEAD_W, HEAD_W), lambda i: (i, 0, 0, 0))
    rot_blk = pl.BlockSpec((1, 8, HEAD_W), lambda i: (i, 0, 0))
    hbm = pl.BlockSpec(memory_space=pl.ANY)
    return pl.pallas_call(
        body, name="mixer_fwd", grid=(n_chunks,),
        out_shape=(jax.ShapeDtypeStruct((seq, D_MODEL), MXU_DTYPE),
                   jax.ShapeDtypeStruct(state_shape, jnp.float32), jax.ShapeDtypeStruct(state_shape, jnp.float32))
        + _exchange_out_shapes(riders, True),
        in_specs=[pl.BlockSpec((CHUNK, N_PROJ), lambda i: (i, 0)), rot_blk, _const_spec(rot_b.shape),
                  _const_spec(dm_t.shape), _const_spec(qdec_t.shape), _const_spec(kdec_t.shape),
                  _const_spec(wg_p.shape), _const_spec(bg_p.shape), _const_spec(ret_norm_w.shape),
                  _const_spec(gla_norm_w.shape)] + [hbm] * n_ride,
        out_specs=(pl.BlockSpec((CHUNK, D_MODEL), lambda i: (i, 0)), state_blk, state_blk) + (hbm,) * n_ride,
        scratch_shapes=[pltpu.VMEM((N_HEADS, HEAD_W, HEAD_W), jnp.float32),
                        pltpu.VMEM((N_HEADS, HEAD_W, HEAD_W), jnp.float32)] + _exchange_sems(n_ride),
        compiler_params=_params(("arbitrary",)),
    )(proj, rot_a, rot_b, dm_t, qdec_t, kdec_t, wg_p, bg_p, ret_norm_w, gla_norm_w, *riders)


def _mixer_bwd(proj, dmixed, rsave, ssave, tables, wg_p, bg_p, ret_norm_w, gla_norm_w, riders):
    seq = proj.shape[0]
    n_chunks = seq // CHUNK
    n_ride = len(riders)
    rot_a, rot_b, dm_t, qdec_t, kdec_t, chunk_decay = tables
    last = n_chunks - 1

    def body(*refs):
        (p_ref, dmx_ref, rsave_ref, ssave_ref, ra_ref, rb_ref, dm_ref, qdec_ref, kdec_ref, wg_ref, bg_ref,
         wr_ref, wl_ref) = refs[:13]
        ride_in, refs = refs[13:13 + n_ride], refs[13 + n_ride:]
        dp_ref, dwr_ref, dwl_ref, dwg_ref, dbg_ref = refs[:5]
        ride_out, refs = refs[5:5 + n_ride], refs[5 + n_ride:]
        dr_sc, ds_sc = refs[:2]
        exchange = _ChipExchange(ride_in, ride_out, refs[2:], False)

        @pl.when(pl.program_id(0) == 0)
        def _():
            exchange.start()
            dr_sc[...] = jnp.zeros_like(dr_sc)
            ds_sc[...] = jnp.zeros_like(ds_sc)
            dwr_ref[...] = jnp.zeros_like(dwr_ref)
            dwl_ref[...] = jnp.zeros_like(dwl_ref)
            dwg_ref[...] = jnp.zeros_like(dwg_ref)
            dbg_ref[...] = jnp.zeros_like(dbg_ref)

        cc, ss = _rotary_chunk(ra_ref, rb_ref)
        row, col = _tri_masks()
        ret_state = [rsave_ref[0, h] for h in HEADS]
        gla_state_t = [ssave_ref[0, h] for h in HEADS]
        d_ret_new = [dr_sc[h] for h in HEADS]
        d_gla_new = [ds_sc[h] for h in HEADS]
        f = _mixer_chunk_forward(p_ref, cc, ss, dm_ref, qdec_ref, kdec_ref, wg_ref, bg_ref, ret_state, gla_state_t)

        do_ret, do_gla = [], []
        for h in HEADS:
            on, rstd = _ln(f["o_ret"][h])
            g = _cols(p_ref, OFF_RG, h)
            sg = _sigmoid(g)
            dy = dmx_ref[:, _head(h)].astype(jnp.float32)
            wr = wr_ref[:, _head(h)]
            dwr_ref[:, _head(h)] += _colsum(dy * on * (g * sg))
            dp_ref[:, OFF_RG + h * HEAD_W:OFF_RG + (h + 1) * HEAD_W] = dy * on * wr * (sg * (1.0 + g * (1.0 - sg)))
            do_ret.append(_ln_bwd(dy * wr * (g * sg), on, rstd))
        for h in HEADS:
            o = f["o_gla"][h]
            rstd = lax.rsqrt(_rowmean(o * o) + LN_EPS)
            on = o * rstd
            g = _cols(p_ref, OFF_GG, h)
            sg = _sigmoid(g)
            dy = dmx_ref[:, _head(N_HEADS + h)].astype(jnp.float32)
            wl = wl_ref[:, _head(h)]
            dwl_ref[:, _head(h)] += _colsum(dy * on * (g * sg))
            dp_ref[:, OFF_GG + h * HEAD_W:OFF_GG + (h + 1) * HEAD_W] = dy * on * wl * (sg * (1.0 + g * (1.0 - sg)))
            don = dy * wl * (g * sg)
            do_gla.append(rstd * (don - on * _rowmean(don * on)))

        ds_raw = [_mm_nt(do_ret[h], f["rv"][h]) * dm_ref[h] for h in HEADS]
        d_att = [_mm_nt(do_gla[h], f["gv"][h]) for h in HEADS]
        dq_state = [_mm_nt(do_ret[h], ret_state[h]) for h in HEADS]
        dk_state = [_mm_nt(f["rv"][h], d_ret_new[h]) for h in HEADS]
        dqb = [_mm(do_gla[h], gla_state_t[h]) for h in HEADS]
        dkb = [_mm(f["gv"][h], d_gla_new[h]) for h in HEADS]
        for h in HEADS:
            dp_ref[:, OFF_RV + h * HEAD_W:OFF_RV + (h + 1) * HEAD_W] = (
                _mm_tn(f["scores"][h], do_ret[h]) + _mm(f["kd"][h], d_ret_new[h]))
        for h in HEADS:
            dp_ref[:, OFF_GV + h * HEAD_W:OFF_GV + (h + 1) * HEAD_W] = (
                _mm_tn(f["att"][h], do_gla[h]) + _mm_nt(f["kb"][h], d_gla_new[h]))
        for h in HEADS:
            dr_sc[h] = chunk_decay[h] * d_ret_new[h] + _mm_tn(f["qd"][h], do_ret[h])
        for h in HEADS:
            ds_sc[h] = d_gla_new[h] * f["ebl"][:, _head(h)] + _mm_tn(do_gla[h], f["qb"][h])

        dqr = [_mm(ds_raw[h], f["kr"][h]) + dq_state[h] * qdec_ref[:, _head(h)] for h in HEADS]
        dkr = [_mm_tn(ds_raw[h], f["qr"][h]) + dk_state[h] * kdec_ref[:, _head(h)] for h in HEADS]
        d_low = [jnp.where(row >= col, d_att[h], 0.0) for h in HEADS]
        d_up = [jnp.where(row < col, d_att[h], 0.0) for h in HEADS]
        dq_e = [_mm(d_low[h], f["k_i"][h]) for h in HEADS]
        dk_i = [_mm_tn(d_low[h], f["q_e"][h]) for h in HEADS]
        dq_i = [_mm(d_up[h], f["k_e"][h]) for h in HEADS]
        dk_e = [_mm_tn(d_up[h], f["q_i"][h]) for h in HEADS]
        for h in HEADS:
            dp_ref[:, OFF_RQ + h * HEAD_W:OFF_RQ + (h + 1) * HEAD_W] = (
                (dqr[h] * cc + _swap_halves(dqr[h] * ss)) * RET_SCALE)
            dp_ref[:, OFF_RK + h * HEAD_W:OFF_RK + (h + 1) * HEAD_W] = dkr[h] * cc + _swap_halves(dkr[h] * ss)
        row_id = lax.broadcasted_iota(jnp.int32, (CHUNK, HEAD_W), 0)
        db_heads = []
        for h in HEADS:
            hs = _head(h)
            e, ei, eb, ek, ebl = f["e"][:, hs], f["ei"][:, hs], f["eb"][:, hs], f["ek"][:, hs], f["ebl"][:, hs]
            dp_ref[:, OFF_GQ + h * HEAD_W:OFF_GQ + (h + 1) * HEAD_W] = (
                (dq_e[h] * e + dq_i[h] * ei + dqb[h] * eb) * GLA_SCALE)
            dp_ref[:, OFF_GK + h * HEAD_W:OFF_GK + (h + 1) * HEAD_W] = dk_e[h] * e + dk_i[h] * ei + dkb[h] * ek
            db = (dq_e[h] * f["q_e"][h] - dq_i[h] * f["q_i"][h] + dk_e[h] * f["k_e"][h] - dk_i[h] * f["k_i"][h]
                  + dqb[h] * f["qb"][h] - dkb[h] * f["kb"][h])
            db_last = _colsum(dkb[h] * f["kb"][h]) + ebl * _colsum(gla_state_t[h] * d_gla_new[h])
            db_heads.append(db + jnp.where(row_id == CHUNK - 1, db_last, 0.0))
        db = jnp.concatenate(db_heads, axis=1)
        d_la = _mm32((col >= row).astype(jnp.float32), db)
        d_logit = d_la * (1.0 / GATE_TAU) * (1.0 - _sigmoid(f["logit"]))
        dp_ref[:, OFF_LR:OFF_LR + HEAD_W] = _mm32_nt(d_logit, wg_ref[...])
        dwg_ref[...] += _mm32_tn(f["glr"], d_logit)
        dbg_ref[...] += _colsum(d_logit)

        @pl.when(pl.program_id(0) == last)
        def _():
            exchange.wait()

    state_blk = pl.BlockSpec((1, N_HEADS, HEAD_W, HEAD_W), lambda i: (last - i, 0, 0, 0))
    rot_blk = pl.BlockSpec((1, 8, HEAD_W), lambda i: (last - i, 0, 0))
    width = N_HEADS * HEAD_W
    vec_out = pl.BlockSpec((1, width), lambda i: (0, 0))
    hbm = pl.BlockSpec(memory_space=pl.ANY)
    return pl.pallas_call(
        body, name="mixer_bwd", grid=(n_chunks,),
        out_shape=(jax.ShapeDtypeStruct((seq, N_PROJ), jnp.float32),
                   jax.ShapeDtypeStruct((1, width), jnp.float32), jax.ShapeDtypeStruct((1, width), jnp.float32),
                   jax.ShapeDtypeStruct((HEAD_W, width), jnp.float32), jax.ShapeDtypeStruct((1, width), jnp.float32))
        + _exchange_out_shapes(riders, False),
        in_specs=[pl.BlockSpec((CHUNK, N_PROJ), lambda i: (last - i, 0)),
                  pl.BlockSpec((CHUNK, D_MODEL), lambda i: (last - i, 0)), state_blk, state_blk, rot_blk,
                  _const_spec(rot_b.shape),
                  _const_spec(dm_t.shape), _const_spec(qdec_t.shape), _const_spec(kdec_t.shape),
                  _const_spec(wg_p.shape), _const_spec(bg_p.shape), _const_spec(ret_norm_w.shape),
                  _const_spec(gla_norm_w.shape)] + [hbm] * n_ride,
        out_specs=(pl.BlockSpec((CHUNK, N_PROJ), lambda i: (last - i, 0)), vec_out, vec_out,
                   pl.BlockSpec((HEAD_W, width), lambda i: (0, 0)), vec_out) + (hbm,) * n_ride,
        scratch_shapes=[pltpu.VMEM((N_HEADS, HEAD_W, HEAD_W), jnp.float32),
                        pltpu.VMEM((N_HEADS, HEAD_W, HEAD_W), jnp.float32)] + _exchange_sems(n_ride),
        compiler_params=_params(("arbitrary",)),
    )(proj, dmixed, rsave, ssave, rot_a, rot_b, dm_t, qdec_t, kdec_t, wg_p, bg_p, ret_norm_w, gla_norm_w, *riders)


V_GATE1, V_SCALE2, V_SHIFT2, V_GATE2, V_LN1W, V_LN1B, V_LN2W, V_LN2B = range(8)
S_GATE1, S_SCALE2, S_SHIFT2, S_GATE2, S_LN1W, S_LN1B, S_LN2W, S_LN2B, S_LOSS = range(9)


def _mlp_fwd_bwd(x2, mixed, target, vecs, w_out, w1_chunks, w2_chunks, tm):
    seq = x2.shape[0]
    n_fc, _, fc = w1_chunks.shape

    def body(x_ref, mx_ref, t_ref, vec_ref, wo_ref, w1_ref, w2_ref,
             dmx_ref, dxa_ref, a_ref, dh_ref, u2_ref, df_ref, dm_ref, sums_ref, relu_sc):
        @pl.when(pl.program_id(0) == 0)
        def _():
            sums_ref[...] = jnp.zeros_like(sums_ref)

        vec = lambda r: vec_ref[r:r + 1, :]

        def acc(r, val):
            sums_ref[r:r + 1, :] += _colsum(val)

        xx = x_ref[...]
        m = _mm(mx_ref[...], wo_ref[...])
        z1h, rstd1 = _ln(ALPHA * xx + vec(V_GATE1) * m)
        x1 = z1h * vec(V_LN1W) + vec(V_LN1B)
        x1h, rstd0 = _ln(x1)
        u2 = (x1h * (1.0 + vec(V_SCALE2)) + vec(V_SHIFT2)).astype(MXU_DTYPE)
        u2_ref[...] = u2
        f = jnp.zeros((tm, D_MODEL), jnp.float32)
        for j in range(n_fc):
            r = jnp.maximum(_mm(u2, w1_ref[j]), 0.0)
            relu_sc[:, j * fc:(j + 1) * fc] = r
            a = (r * r).astype(MXU_DTYPE)
            a_ref[:, j * fc:(j + 1) * fc] = a
            f = f + _mm(a, w2_ref[j])
        z2h, rstd2 = _ln(ALPHA * x1 + vec(V_GATE2) * f)
        err = z2h * vec(V_LN2W) + vec(V_LN2B) - t_ref[...]
        acc(S_LOSS, err * err)
        dy = err * (1.0 / D_MODEL)
        acc(S_LN2W, dy * z2h)
        acc(S_LN2B, dy)
        dz2 = _ln_bwd(dy * vec(V_LN2W), z2h, rstd2)
        acc(S_GATE2, dz2 * f)
        df = (vec(V_GATE2) * dz2).astype(MXU_DTYPE)
        df_ref[...] = df
        du2 = jnp.zeros((tm, D_MODEL), jnp.float32)
        for j in range(n_fc):
            dh = (_mm_nt(df, w2_ref[j]) * (2.0 * relu_sc[:, j * fc:(j + 1) * fc])).astype(MXU_DTYPE)
            dh_ref[:, j * fc:(j + 1) * fc] = dh
            du2 = du2 + _mm_nt(dh, w1_ref[j])
        acc(S_SCALE2, du2 * x1h)
        acc(S_SHIFT2, du2)
        dx1 = ALPHA * dz2 + _ln_bwd(du2 * (1.0 + vec(V_SCALE2)), x1h, rstd0)
        acc(S_LN1W, dx1 * z1h)
        acc(S_LN1B, dx1)
        dz1 = _ln_bwd(dx1 * vec(V_LN1W), z1h, rstd1)
        acc(S_GATE1, dz1 * m)
        dxa_ref[...] = ALPHA * dz1
        dm = (vec(V_GATE1) * dz1).astype(MXU_DTYPE)
        dm_ref[...] = dm
        dmx_ref[...] = _mm_nt(dm, wo_ref[...])

    tile = lambda width: pl.BlockSpec((tm, width), lambda i: (i, 0))
    f32 = lambda width: jax.ShapeDtypeStruct((seq, width), jnp.float32)
    b16 = lambda width: jax.ShapeDtypeStruct((seq, width), MXU_DTYPE)
    return pl.pallas_call(
        body, name="mlp_fwd_bwd", grid=(seq // tm,),
        out_shape=(f32(D_MODEL), f32(D_MODEL), b16(D_FF), b16(D_FF), b16(D_MODEL), b16(D_MODEL), b16(D_MODEL),
                   jax.ShapeDtypeStruct((16, D_MODEL), jnp.float32)),
        in_specs=[tile(D_MODEL), tile(D_MODEL), tile(D_MODEL), _const_spec(vecs.shape), _const_spec(w_out.shape),
                  _const_spec(w1_chunks.shape), _const_spec(w2_chunks.shape)],
        out_specs=(tile(D_MODEL), tile(D_MODEL), tile(D_FF), tile(D_FF), tile(D_MODEL), tile(D_MODEL),
                   tile(D_MODEL), pl.BlockSpec((16, D_MODEL), lambda i: (0, 0))),
        scratch_shapes=[pltpu.VMEM((tm, D_FF), jnp.float32)],
        compiler_params=_params(("arbitrary",)),
    )(x2, mixed, target, vecs, w_out, w1_chunks, w2_chunks)


def _grad_matmul(a, b, name, tn, blocks_are_rows):
    seq, m_dim = a.shape
    n_dim = b.shape[1]
    tk = min(seq, 512)
    nk = seq // tk
    if blocks_are_rows:
        tm = m_dim // N_CHIP
        assert tn == n_dim
        grid = (N_CHIP, 1, nk)
        out_map = lambda i, j, k: (i, 0, 0)
    else:
        tm = m_dim
        assert tn * N_CHIP == n_dim
        grid = (1, N_CHIP, nk)
        out_map = lambda i, j, k: (j, 0, 0)

    def body(a_ref, b_ref, o_ref, acc_sc):
        k = pl.program_id(2)

        @pl.when(k == 0)
        def _():
            acc_sc[...] = jnp.zeros_like(acc_sc)

        acc_sc[...] += _mm_tn(a_ref[...], b_ref[...])

        @pl.when(k == nk - 1)
        def _():
            o_ref[0] = acc_sc[...].astype(o_ref.dtype)

    return pl.pallas_call(
        body, name=name, grid=grid,
        out_shape=jax.ShapeDtypeStruct((N_CHIP, tm, tn), WIRE_DTYPE),
        in_specs=[pl.BlockSpec((tk, tm), lambda i, j, k: (k, i)), pl.BlockSpec((tk, tn), lambda i, j, k: (k, j))],
        out_specs=pl.BlockSpec((1, tm, tn), out_map),
        scratch_shapes=[pltpu.VMEM((tm, tn), jnp.float32)],
        compiler_params=_params(("arbitrary", "arbitrary", "arbitrary")),
    )(a, b)


def _grad_matmul_full(a, b, name, tn):
    seq, m_dim = a.shape
    n_dim = b.shape[1]
    tk = min(seq, 512)
    nk = seq // tk
    assert n_dim % tn == 0

    def body(a_ref, b_ref, o_ref, acc_sc):
        k = pl.program_id(1)

        @pl.when(k == 0)
        def _():
            acc_sc[...] = jnp.zeros_like(acc_sc)

        acc_sc[...] += _mm_tn(a_ref[...], b_ref[...])

        @pl.when(k == nk - 1)
        def _():
            o_ref[...] = acc_sc[...].astype(o_ref.dtype)

    return pl.pallas_call(
        body, name=name, grid=(n_dim // tn, nk),
        out_shape=jax.ShapeDtypeStruct((m_dim, n_dim), WIRE_DTYPE),
        in_specs=[pl.BlockSpec((tk, m_dim), lambda j, k: (k, 0)), pl.BlockSpec((tk, tn), lambda j, k: (k, j))],
        out_specs=pl.BlockSpec((m_dim, tn), lambda j, k: (0, j)),
        scratch_shapes=[pltpu.VMEM((m_dim, tn), jnp.float32)],
        compiler_params=_params(("arbitrary", "arbitrary")),
    )(a, b)


def _sum_chips(stack, name):
    _, rows, cols = stack.shape
    tr = min(rows, 256)

    def body(s_ref, o_ref):
        total = s_ref[0].astype(jnp.float32)
        for j in range(1, N_CHIP):
            total = total + s_ref[j].astype(jnp.float32)
        o_ref[...] = total

    return pl.pallas_call(
        body, name=name, grid=(rows // tr,),
        out_shape=jax.ShapeDtypeStruct((rows, cols), jnp.float32),
        in_specs=[pl.BlockSpec((N_CHIP, tr, cols), lambda i: (0, i, 0))],
        out_specs=pl.BlockSpec((tr, cols), lambda i: (i, 0)),
        compiler_params=_params(("arbitrary",)),
    )(stack)


def _adam_pair(w, g_mine, g_sibling, m, v, name):
    _, rows, cols = w.shape
    tr = min(rows, 256)

    def body(w_ref, ga_ref, gb_ref, m_ref, v_ref, g_ref, dl_ref, m2_ref, v2_ref):
        g = ga_ref[...] + gb_ref[...]
        delta, m2, v2 = _adam(w_ref[0], g, m_ref[0], v_ref[0])
        g_ref[0] = g
        dl_ref[0] = delta
        m2_ref[0] = m2
        v2_ref[0] = v2

    blk = pl.BlockSpec((tr, cols), lambda i: (i, 0))
    blk3 = pl.BlockSpec((1, tr, cols), lambda i: (0, i, 0))
    out = jax.ShapeDtypeStruct((1, rows, cols), jnp.float32)
    return pl.pallas_call(
        body, name=name, grid=(rows // tr,),
        out_shape=(out, out, out, out),
        in_specs=[blk3, blk, blk, blk3, blk3], out_specs=(blk3,) * 4,
        compiler_params=_params(("arbitrary",)),
    )(w, g_mine, g_sibling, m, v)


def _sum_devices(gathered):
    _, rows, _ = gathered.shape

    def body(g_ref, o_ref):
        total = g_ref[0]
        for d in range(1, N_DEV):
            total = total + g_ref[d]
        o_ref[...] = total

    return pl.pallas_call(
        body, name="sum_devices",
        out_shape=jax.ShapeDtypeStruct((rows, 128), jnp.float32),
    )(gathered)


def _adam_small(w, g, m, v):
    def body(w_ref, g_ref, m_ref, v_ref, dl_ref, m2_ref, v2_ref):
        delta, m2, v2 = _adam(w_ref[...], g_ref[...], m_ref[...], v_ref[...])
        dl_ref[...] = delta
        m2_ref[...] = m2
        v2_ref[...] = v2

    out = jax.ShapeDtypeStruct(w.shape, jnp.float32)
    return pl.pallas_call(body, name="adam_small", out_shape=(out, out, out))(w, g, m, v)


def _pad_heads(w):
    lead = w.shape[:-1]
    w = w.reshape(lead + (N_HEADS, GLA_DK))
    w = jnp.pad(w, [(0, 0)] * len(lead) + [(0, 0), (0, HEAD_W - GLA_DK)])
    return w.reshape(lead + (N_HEADS * HEAD_W,))


def _unpad_heads(w):
    lead = w.shape[:-1]
    return w.reshape(lead + (N_HEADS, HEAD_W))[..., :GLA_DK].reshape(lead + (N_HEADS * GLA_DK,))


def _pad_w_in(w):
    return jnp.concatenate([
        w[:, :2048], _pad_heads(w[:, 2048:2304]), _pad_heads(w[:, 2304:2560]), w[:, 2560:3584],
        jnp.pad(w[:, 3584:3600], ((0, 0), (0, HEAD_W - GATE_RANK)))], axis=1)


def _unpad_w_in(g):
    return jnp.concatenate([
        g[:, :2048], _unpad_heads(g[:, OFF_GQ:OFF_GQ + 512]), _unpad_heads(g[:, OFF_GK:OFF_GK + 512]),
        g[:, OFF_GV:OFF_LR], g[:, OFF_LR:OFF_LR + GATE_RANK]], axis=1)


def _rows128(a):
    return a.reshape(-1, 128)


def _rows8(a):
    a = a.reshape(-1, 128)
    return jnp.pad(a, ((0, -a.shape[0] % 8), (0, 0)))


def kernel(x, c, w_ada, b_ada, w_in, ret_norm_w, gla_gate_w, gla_gate_b, gla_norm_w, w_out, ln1_w, ln1_b, w_ff1, w_ff2, ln2_w, ln2_b, loss_target, m_w_ada, m_b_ada, m_w_in, m_ret_norm_w, m_gla_gate_w, m_gla_gate_b, m_gla_norm_w, m_w_out, m_ln1_w, m_ln1_b, m_w_ff1, m_w_ff2, m_ln2_w, m_ln2_b, v_w_ada, v_b_ada, v_w_in, v_ret_norm_w, v_gla_gate_w, v_gla_gate_b, v_gla_norm_w, v_w_out, v_ln1_w, v_ln1_b, v_w_ff1, v_w_ff2, v_ln2_w, v_ln2_b):
    seq = x.shape[1]
    tm = min(seq, TOKEN_TILE)
    xi, yi, ci = _mesh_pos()
    dev = 4 * xi + 2 * yi + ci
    chip = 2 * xi + yi
    x2, target = x[0], loss_target[0]
    ada_cols = w_ada.shape[2]
    in_cols = w_in.shape[2]
    gate_cols = gla_gate_w.shape[2]

    g0 = _gather_rows(jnp.concatenate([_rows128(c), _rows128(gla_gate_w[0])], axis=0), "gather_cond")
    c_all = g0[:, :8].reshape(N_DEV, D_MODEL)
    gate_w_full = jnp.concatenate([g0[2 * j, 8:16].reshape(GATE_RANK, gate_cols) for j in range(N_CHIP)], axis=1)
    wg_p = jnp.pad(_pad_heads(gate_w_full), ((0, HEAD_W - GATE_RANK), (0, 0)))
    bg_p = _pad_heads(gla_gate_b)

    b_blk = lax.dynamic_slice(b_ada, (0, chip * ada_cols), (1, ada_cols))
    mod_blk = _ada_fwd(c_all, w_ada[0], b_blk)
    g1 = _gather_rows(_rows128(mod_blk), "gather_mod")
    mod_all = jnp.concatenate([g1[2 * j].reshape(N_DEV, ada_cols) for j in range(N_CHIP)], axis=1)
    mod = lax.dynamic_slice(mod_all, (dev, 0), (1, 6 * D_MODEL))
    shift1, scale1, gate1, shift2, scale2, gate2 = [mod[:, i * D_MODEL:(i + 1) * D_MODEL] for i in range(6)]

    (w_in_stack,) = _chip_exchange([w_in[0].astype(WIRE_DTYPE)], "gather_w_in", True)
    w_in_p = _pad_w_in(jnp.transpose(w_in_stack, (1, 0, 2)).reshape(D_MODEL, N_PROJ_SRC)).astype(MXU_DTYPE)

    zeros_row = jnp.zeros((1, D_MODEL), jnp.float32)
    vecs1 = jnp.concatenate([shift1, scale1] + [zeros_row] * 6, axis=0)
    proj, u = _inproj_fwd(x2, vecs1, w_in_p, tm)
    rot_a, rot_b = _rotary_tables(seq)
    dm_t, qdec_t, kdec_t, chunk_decay = _decay_tables()
    tables = (rot_a, rot_b, dm_t, qdec_t, kdec_t, chunk_decay)
    mixed, rsave, ssave, w_out_stack, w1_stack, w2_stack = _mixer_fwd(
        proj, tables, wg_p, bg_p, ret_norm_w, gla_norm_w,
        [w_out[0].astype(WIRE_DTYPE), w_ff1[0].astype(WIRE_DTYPE), w_ff2[0].astype(WIRE_DTYPE)])
    w_out_full = w_out_stack.reshape(D_MODEL, D_MODEL).astype(MXU_DTYPE)
    w1_chunks = w1_stack.astype(MXU_DTYPE)
    w2_chunks = w2_stack.astype(MXU_DTYPE)

    vecs2 = jnp.concatenate([gate1, scale2, shift2, gate2, ln1_w, ln1_b, ln2_w, ln2_b], axis=0)
    dmixed, dxa, act, dh, u2, df, dm, sums2 = _mlp_fwd_bwd(x2, mixed, target, vecs2, w_out_full, w1_chunks,
                                                           w2_chunks, tm)

    g_out_stack = _grad_matmul(mixed, dm, "grad_w_out", D_MODEL, True)
    g_ff1_stack = _grad_matmul(u2, dh, "grad_w_ff1", D_FF // N_CHIP, False)
    g_ff2_stack = _grad_matmul(act, df, "grad_w_ff2", D_MODEL, True)
    dproj, d_ret_norm, d_gla_norm, d_wg_p, d_bg_p, r_out, r_ff1, r_ff2 = _mixer_bwd(
        proj, dmixed, rsave, ssave, tables, wg_p, bg_p, ret_norm_w, gla_norm_w,
        [g_out_stack, g_ff1_stack, g_ff2_stack])
    g_in_full = _grad_matmul_full(u, dproj, "grad_w_in", 1408)
    g_in_stack = jnp.transpose(_unpad_w_in(g_in_full).reshape(D_MODEL, N_CHIP, in_cols), (1, 0, 2)).astype(WIRE_DTYPE)
    grad_x, sums1, r_in = _inproj_bwd(dproj, x2, dxa, vecs1, w_in_p, tm, [g_in_stack])

    dmod = jnp.concatenate([sums1[0:1], sums1[1:2], sums2[S_GATE1:S_GATE1 + 1], sums2[S_SHIFT2:S_SHIFT2 + 1],
                            sums2[S_SCALE2:S_SCALE2 + 1], sums2[S_GATE2:S_GATE2 + 1]], axis=1)
    d_gate_w_full = _unpad_heads(d_wg_p[:GATE_RANK])
    flat = lambda parts: jnp.concatenate([_rows8(p) for p in parts], axis=0)
    small = flat([dmod, sums2[S_LN1W:S_LN1W + 1], sums2[S_LN1B:S_LN1B + 1], sums2[S_LN2W:S_LN2W + 1],
                  sums2[S_LN2B:S_LN2B + 1], d_ret_norm, _unpad_heads(d_bg_p), d_gla_norm, d_gate_w_full,
                  sums2[S_LOSS:S_LOSS + 1]])
    g2 = _gather_rows(small, "gather_small")
    tot = _sum_devices(g2)
    loss = 0.5 / D_MODEL * jnp.sum(tot[136:144])
    grad_b_ada = tot[0:48].reshape(1, 6 * D_MODEL)
    grad_ln1_w, grad_ln1_b = tot[48:56].reshape(1, D_MODEL), tot[56:64].reshape(1, D_MODEL)
    grad_ln2_w, grad_ln2_b = tot[64:72].reshape(1, D_MODEL), tot[72:80].reshape(1, D_MODEL)
    grad_ret_norm = tot[80:84].reshape(1, 512)
    grad_gate_b = tot[88:90].reshape(1, 256)
    grad_gla_norm = tot[96:100].reshape(1, 512)
    grad_gate_w = lax.dynamic_slice(tot[104:136].reshape(GATE_RANK, 256), (0, chip * gate_cols),
                                    (GATE_RANK, gate_cols))

    small_w = flat([b_ada, ln1_w, ln1_b, ln2_w, ln2_b, ret_norm_w, gla_gate_b, gla_norm_w, gla_gate_w[0]])
    small_g = flat([grad_b_ada, grad_ln1_w, grad_ln1_b, grad_ln2_w, grad_ln2_b, grad_ret_norm, grad_gate_b,
                    grad_gla_norm, grad_gate_w])
    small_m = flat([m_b_ada, m_ln1_w, m_ln1_b, m_ln2_w, m_ln2_b, m_ret_norm_w, m_gla_gate_b, m_gla_norm_w,
                    m_gla_gate_w[0]])
    small_v = flat([v_b_ada, v_ln1_w, v_ln1_b, v_ln2_w, v_ln2_b, v_ret_norm_w, v_gla_gate_b, v_gla_norm_w,
                    v_gla_gate_w[0]])
    small_out = _adam_small(small_w, small_g, small_m, small_v)

    def unflat(t):
        pieces, row = [], 0
        for shape in [(1, 6 * D_MODEL)] + [(1, D_MODEL)] * 4 + [(1, 512), (1, 256), (1, 512), (1, GATE_RANK, gate_cols)]:
            n = int(np.prod(shape)) // 128
            pieces.append(t[row:row + n].reshape(shape))
            row += -(-n // 8) * 8
        return pieces

    sm_delta, sm_m, sm_v = [unflat(t) for t in small_out]

    dmod_all = g2[:, 0:48].reshape(N_DEV, 6 * D_MODEL)
    dmod_blk = lax.dynamic_slice(dmod_all, (0, chip * ada_cols), (N_DEV, ada_cols))
    ada_out = _ada_bwd_adam(jnp.transpose(c_all), dmod_blk, w_ada[0], m_w_ada[0], v_w_ada[0])
    ada_g, ada_delta, ada_m, ada_v = [t[None] for t in ada_out]

    received = [r_in, r_out, r_ff1, r_ff2]
    names = ["w_in", "w_out", "w_ff1", "w_ff2"]
    partial = [_sum_chips(r, "sum_" + n) for r, n in zip(received, names)]
    swapped = _sibling_swap(partial, "swap_partials")
    big = {}
    for n, w, mine, theirs, m, v in zip(names, [w_in, w_out, w_ff1, w_ff2], partial, swapped,
                                        [m_w_in, m_w_out, m_w_ff1, m_w_ff2], [v_w_in, v_w_out, v_w_ff1, v_w_ff2]):
        big[n] = _adam_pair(w, mine, theirs, m, v, "adam_" + n)

    def assemble(ada, smalls, k):
        b_ada_o, ln1w_o, ln1b_o, ln2w_o, ln2b_o, ret_o, gb_o, gln_o, gw_o = smalls
        return [ada, b_ada_o, big["w_in"][k], ret_o, gw_o, gb_o, gln_o, big["w_out"][k], ln1w_o, ln1b_o,
                big["w_ff1"][k], big["w_ff2"][k], ln2w_o, ln2b_o]

    small_grads = [grad_b_ada, grad_ln1_w, grad_ln1_b, grad_ln2_w, grad_ln2_b, grad_ret_norm, grad_gate_b,
                   grad_gla_norm, grad_gate_w[None]]
    grads = assemble(ada_g, small_grads, 0)
    deltas = assemble(ada_delta, sm_delta, 1)
    new_m = assemble(ada_m, sm_m, 2)
    new_v = assemble(ada_v, sm_v, 3)
    return (loss, grad_x[None], *grads, *deltas, *new_m, *new_v)
```

```python
import functools

import numpy as np
import jax
import jax.numpy as jnp
from jax import lax
from jax.experimental import pallas as pl
from jax.experimental.pallas import tpu as pltpu

D_MODEL = 1024
D_FF = 4096
CHUNK = 64
N_HEADS = 4
HEAD_W = 128
GLA_DK = 64
GATE_RANK = 16
GATE_TAU = 16.0
LN_EPS = 1e-5
ALPHA = 2.0 ** 0.25
ROPE_BASE = 10000.0
RET_SCALE = float(HEAD_W) ** -0.5
GLA_SCALE = float(GLA_DK) ** -0.5

ADAM_LR = 0.001
ADAM_B1 = 0.9
ADAM_B2 = 0.999
ADAM_EPS = 1e-08
ADAM_WD = 0.01
ADAM_STEP = 10

OFF_RQ, OFF_RK, OFF_RV, OFF_RG = 0, 512, 1024, 1536
OFF_GQ, OFF_GK, OFF_GV, OFF_GG, OFF_LR = 2048, 2560, 3072, 3584, 4096
N_PROJ = 4224
N_PROJ_SRC = 3600

N_DEV = 8
N_CHIP = 4
MESH = pl.DeviceIdType.MESH
MXU_DTYPE = jnp.bfloat16
WIRE_DTYPE = jnp.bfloat16
VMEM_LIMIT = 60 * 1024 * 1024
TOKEN_TILE = 256
ELEMENTWISE_COLS = 256
HIGHEST = lax.Precision.HIGHEST


def _mm(a, b):
    return jnp.dot(a.astype(MXU_DTYPE), b.astype(MXU_DTYPE), preferred_element_type=jnp.float32)


def _mm_nt(a, b):
    return lax.dot_general(a.astype(MXU_DTYPE), b.astype(MXU_DTYPE), (((1,), (1,)), ((), ())),
                           preferred_element_type=jnp.float32)


def _mm_tn(a, b):
    return lax.dot_general(a.astype(MXU_DTYPE), b.astype(MXU_DTYPE), (((0,), (0,)), ((), ())),
                           preferred_element_type=jnp.float32)


def _mm32(a, b):
    return jnp.dot(a, b, precision=HIGHEST, preferred_element_type=jnp.float32)


def _mm32_nt(a, b):
    return lax.dot_general(a, b, (((1,), (1,)), ((), ())), precision=HIGHEST, preferred_element_type=jnp.float32)


def _mm32_tn(a, b):
    return lax.dot_general(a, b, (((0,), (0,)), ((), ())), precision=HIGHEST, preferred_element_type=jnp.float32)


def _rowmean(a):
    return jnp.mean(a, axis=-1, keepdims=True)


def _colsum(a):
    return jnp.sum(a, axis=0, keepdims=True)


def _ln(z):
    zc = z - _rowmean(z)
    rstd = lax.rsqrt(_rowmean(zc * zc) + LN_EPS)
    return zc * rstd, rstd


def _ln_bwd(dzh, zh, rstd):
    return rstd * (dzh - _rowmean(dzh) - zh * _rowmean(dzh * zh))


def _sigmoid(a):
    return 1.0 / (1.0 + jnp.exp(-a))


def _log_sigmoid(a):
    return jnp.minimum(a, 0.0) - jnp.log(1.0 + jnp.exp(-jnp.abs(a)))


def _swap_halves(a):
    return pltpu.roll(a, HEAD_W // 2, 1)


def _tri_masks():
    row = lax.broadcasted_iota(jnp.int32, (CHUNK, CHUNK), 0)
    col = lax.broadcasted_iota(jnp.int32, (CHUNK, CHUNK), 1)
    return row, col


def _const_spec(shape):
    zeros = (0,) * len(shape)
    return pl.BlockSpec(shape, lambda *_: zeros, pipeline_mode=pl.Buffered(1))


def _params(semantics):
    return pltpu.CompilerParams(dimension_semantics=semantics, vmem_limit_bytes=VMEM_LIMIT)


def _decay_tables():
    log_gamma = np.log(1.0 - 2.0 ** (-5.0 - np.arange(N_HEADS, dtype=np.float64)))
    idx = np.arange(CHUNK, dtype=np.float64)
    dist = np.abs(idx[:, None] - idx[None, :])
    intra = np.exp(log_gamma[:, None, None] * dist)
    kdec = np.exp(log_gamma[None, :] * (CHUNK - 1.0 - idx)[:, None])
    qdec = np.exp(log_gamma[None, :] * (idx + 1.0)[:, None])
    chunk_decay = np.exp(log_gamma * CHUNK)
    lanes = lambda t: np.repeat(t, HEAD_W, axis=1).astype(np.float32)
    return (jnp.asarray(intra.astype(np.float32)), jnp.asarray(lanes(qdec)), jnp.asarray(lanes(kdec)),
            [float(np.float32(v)) for v in chunk_decay])


def _rotary_tables(seq):
    half = HEAD_W // 2
    inv = 1.0 / (ROPE_BASE ** jnp.linspace(0.0, 1.0, half, dtype=jnp.float32))
    both = lambda t: jnp.concatenate([t, t], axis=-1)
    ang_a = jnp.arange(0, seq, CHUNK, dtype=jnp.float32)[:, None] * inv[None, :]
    rot_a = jnp.stack([both(jnp.cos(ang_a)), both(jnp.sin(ang_a))], axis=1)
    rot_a = jnp.pad(rot_a, ((0, 0), (0, 6), (0, 0)))
    ang_b = jnp.arange(CHUNK, dtype=jnp.float32)[:, None] * inv[None, :]
    cos_b, sin_b = both(jnp.cos(ang_b)), both(jnp.sin(ang_b))
    sign = jnp.concatenate([-jnp.ones((half,), jnp.float32), jnp.ones((half,), jnp.float32)])
    return rot_a, jnp.stack([cos_b, sin_b, cos_b * sign, sin_b * sign])


def _rotary_chunk(ra_ref, rb_ref):
    cos_a, sin_a = ra_ref[0, 0:1, :], ra_ref[0, 1:2, :]
    return cos_a * rb_ref[0] - sin_a * rb_ref[1], sin_a * rb_ref[2] + cos_a * rb_ref[3]


def _mesh_pos():
    return lax.axis_index("x"), lax.axis_index("y"), lax.axis_index("c")


def _flip(v, bit):
    return 1 - v if bit else v


def _gather_rows(v, name):
    rows = v.shape[0]

    def body(v_ref, out_ref, send_sems, recv_sems):
        x, y, c = _mesh_pos()
        me = 4 * x + 2 * y + c
        out_ref[me] = v_ref[...]
        sends, recvs = [], []
        for k in range(1, N_DEV):
            px, py, pc = _flip(x, (k >> 2) & 1), _flip(y, (k >> 1) & 1), _flip(c, k & 1)
            peer = 4 * px + 2 * py + pc
            sends.append(pltpu.make_async_remote_copy(
                src_ref=v_ref, dst_ref=out_ref.at[me], send_sem=send_sems.at[k - 1], recv_sem=recv_sems.at[k - 1],
                device_id=(px, py, pc), device_id_type=MESH))
            recvs.append(pltpu.make_async_remote_copy(
                src_ref=v_ref, dst_ref=out_ref.at[peer], send_sem=send_sems.at[k - 1], recv_sem=recv_sems.at[k - 1],
                device_id=(px, py, pc), device_id_type=MESH))
        for cp in sends:
            cp.start()
        for cp in recvs:
            cp.wait_recv()
        for cp in sends:
            cp.wait_send()

    return pl.pallas_call(
        body, name=name,
        out_shape=jax.ShapeDtypeStruct((N_DEV, rows, 128), jnp.float32),
        in_specs=[pl.BlockSpec(memory_space=pltpu.VMEM)],
        out_specs=pl.BlockSpec(memory_space=pltpu.VMEM),
        scratch_shapes=[pltpu.SemaphoreType.DMA((N_DEV - 1,)), pltpu.SemaphoreType.DMA((N_DEV - 1,))],
    )(v)


def _chip_gather(arrays, name):
    n = len(arrays)

    def body(*refs):
        gather = _ChipGather(refs[:n], refs[n:2 * n], refs[2 * n:])
        gather.start()
        gather.forward()
        gather.finish()

    return pl.pallas_call(
        body, name=name,
        out_shape=_exchange_out_shapes(arrays, True),
        in_specs=[pl.BlockSpec(memory_space=pl.ANY)] * n,
        out_specs=tuple(pl.BlockSpec(memory_space=pl.ANY) for _ in arrays),
        scratch_shapes=_gather_sems(n),
    )(*arrays)


def _exchange_out_shapes(arrays, gather):
    return tuple(jax.ShapeDtypeStruct((N_CHIP,) + a.shape if gather else a.shape, a.dtype) for a in arrays)


def _scatter_sems(n):
    n_sem = n * (N_CHIP - 1)
    return [pltpu.SemaphoreType.DMA((n_sem,)), pltpu.SemaphoreType.DMA((n_sem,)), pltpu.SemaphoreType.DMA((n,))]


def _gather_sems(n):
    n_sem = n * (N_CHIP - 1)
    return [pltpu.SemaphoreType.DMA((n_sem,))] * 4 + [pltpu.SemaphoreType.DMA((n,))]


def _peer_chips(x, y):
    out = []
    for k in range(1, N_CHIP):
        px, py = _flip(x, (k >> 1) & 1), _flip(y, k & 1)
        out.append((px, py, 2 * px + py))
    return out


class _ChipScatter:
    def __init__(self, ins, outs, sems):
        send_sems, recv_sems, local_sems = sems
        x, y, c = _mesh_pos()
        chip = 2 * x + y
        self.local, self.sends, self.recvs = [], [], []
        for i in range(len(ins)):
            self.local.append(pltpu.make_async_copy(ins[i].at[chip], outs[i].at[chip], local_sems.at[i]))
            for k, (px, py, peer_chip) in enumerate(_peer_chips(x, y)):
                sem = i * (N_CHIP - 1) + k
                src = ins[i].at[peer_chip]
                self.sends.append(pltpu.make_async_remote_copy(
                    src_ref=src, dst_ref=outs[i].at[chip], send_sem=send_sems.at[sem], recv_sem=recv_sems.at[sem],
                    device_id=(px, py, c), device_id_type=MESH))
                self.recvs.append(pltpu.make_async_remote_copy(
                    src_ref=src, dst_ref=outs[i].at[peer_chip], send_sem=send_sems.at[sem], recv_sem=recv_sems.at[sem],
                    device_id=(px, py, c), device_id_type=MESH))

    def start(self):
        for cp in self.local + self.sends:
            cp.start()

    def wait(self):
        for cp in self.recvs:
            cp.wait_recv()
        for cp in self.sends:
            cp.wait_send()
        for cp in self.local:
            cp.wait()


class _ChipGather:
    def __init__(self, ins, outs, sems):
        ici_send, ici_recv, d2d_send, d2d_recv, local_sems = sems
        x, y, c = _mesh_pos()
        chip = 2 * x + y
        self.local, self.ici_sends, self.ici_recvs, self.d2d_sends, self.d2d_recvs = [], [], [], [], []
        for i in range(len(ins)):
            half = ins[i].shape[-1] // 2
            assert half % 128 == 0
            lead = (slice(None),) * (len(ins[i].shape) - 1)
            mine = lead + (pl.ds(pl.multiple_of(c * half, 128), half),)
            theirs = lead + (pl.ds(pl.multiple_of((1 - c) * half, 128), half),)
            self.local.append(pltpu.make_async_copy(ins[i], outs[i].at[chip], local_sems.at[i]))
            for k, (px, py, peer_chip) in enumerate(_peer_chips(x, y)):
                sem = i * (N_CHIP - 1) + k
                self.ici_sends.append(pltpu.make_async_remote_copy(
                    src_ref=ins[i].at[mine], dst_ref=outs[i].at[chip].at[mine],
                    send_sem=ici_send.at[sem], recv_sem=ici_recv.at[sem], device_id=(px, py, c), device_id_type=MESH))
                landed = outs[i].at[peer_chip].at[mine]
                self.ici_recvs.append(pltpu.make_async_remote_copy(
                    src_ref=ins[i].at[mine], dst_ref=landed,
                    send_sem=ici_send.at[sem], recv_sem=ici_recv.at[sem], device_id=(px, py, c), device_id_type=MESH))
                self.d2d_sends.append(pltpu.make_async_remote_copy(
                    src_ref=landed, dst_ref=landed,
                    send_sem=d2d_send.at[sem], recv_sem=d2d_recv.at[sem], device_id=(x, y, 1 - c), device_id_type=MESH))
                self.d2d_recvs.append(pltpu.make_async_remote_copy(
                    src_ref=landed, dst_ref=outs[i].at[peer_chip].at[theirs],
                    send_sem=d2d_send.at[sem], recv_sem=d2d_recv.at[sem], device_id=(x, y, 1 - c), device_id_type=MESH))

    def start(self):
        for cp in self.local + self.ici_sends:
            cp.start()

    def forward(self):
        for landed, onward in zip(self.ici_recvs, self.d2d_sends):
            landed.wait_recv()
            onward.start()

    def finish(self):
        for cp in self.d2d_recvs:
            cp.wait_recv()
        for cp in self.d2d_sends + self.ici_sends:
            cp.wait_send()
        for cp in self.local:
            cp.wait()


def _sibling_swap(arrays, name):
    n = len(arrays)

    def body(*refs):
        ins, outs = refs[:n], refs[n:2 * n]
        send_sems, recv_sems = refs[2 * n:]
        x, y, c = _mesh_pos()
        copies = [pltpu.make_async_remote_copy(
            src_ref=ins[i], dst_ref=outs[i], send_sem=send_sems.at[i], recv_sem=recv_sems.at[i],
            device_id=(x, y, 1 - c), device_id_type=MESH) for i in range(n)]
        for cp in copies:
            cp.start()
        for cp in copies:
            cp.wait_recv()
        for cp in copies:
            cp.wait_send()

    return pl.pallas_call(
        body, name=name,
        out_shape=tuple(jax.ShapeDtypeStruct(a.shape, a.dtype) for a in arrays),
        in_specs=[pl.BlockSpec(memory_space=pl.ANY)] * n,
        out_specs=tuple(pl.BlockSpec(memory_space=pl.ANY) for _ in arrays),
        scratch_shapes=[pltpu.SemaphoreType.DMA((n,)), pltpu.SemaphoreType.DMA((n,))],
    )(*arrays)


def _ada_fwd(c_all, w_ada_blk, b_blk):
    cols = w_ada_blk.shape[1]

    def body(c_ref, w_ref, b_ref, out_ref):
        cv = c_ref[...]
        out_ref[...] = _mm32(cv * _sigmoid(cv), w_ref[...]) + b_ref[...]

    return pl.pallas_call(
        body, name="ada_fwd",
        out_shape=jax.ShapeDtypeStruct((N_DEV, cols), jnp.float32),
        compiler_params=pltpu.CompilerParams(vmem_limit_bytes=VMEM_LIMIT),
    )(c_all, w_ada_blk, b_blk)


def _adam(w, g, m, v):
    m2 = ADAM_B1 * m + (1.0 - ADAM_B1) * g
    v2 = ADAM_B2 * v + (1.0 - ADAM_B2) * (g * g)
    m_hat = m2 / (1.0 - ADAM_B1 ** ADAM_STEP)
    v_hat = v2 / (1.0 - ADAM_B2 ** ADAM_STEP)
    delta = -ADAM_LR * (m_hat / (jnp.sqrt(v_hat) + ADAM_EPS) + ADAM_WD * w)
    return delta, m2, v2


def _ada_bwd_adam(c_t, dmod_blk, w, m, v):
    rows, cols = w.shape
    tile = 512
    assert cols % tile == 0

    def body(c_ref, d_ref, w_ref, m_ref, v_ref, g_ref, dl_ref, m2_ref, v2_ref):
        sc = c_ref[...]
        sc = sc * _sigmoid(sc)
        dm = d_ref[...]
        g = sc[:, 0:1] * dm[0:1, :]
        for b in range(1, N_DEV):
            g = g + sc[:, b:b + 1] * dm[b:b + 1, :]
        delta, m2, v2 = _adam(w_ref[...], g, m_ref[...], v_ref[...])
        g_ref[...] = g
        dl_ref[...] = delta
        m2_ref[...] = m2
        v2_ref[...] = v2

    blk = pl.BlockSpec((rows, tile), lambda j: (0, j))
    out = jax.ShapeDtypeStruct((rows, cols), jnp.float32)
    return pl.pallas_call(
        body, name="ada_bwd_adam", grid=(cols // tile,),
        out_shape=(out, out, out, out),
        in_specs=[pl.BlockSpec((rows, N_DEV), lambda j: (0, 0)), pl.BlockSpec((N_DEV, tile), lambda j: (0, j)),
                  blk, blk, blk],
        out_specs=(blk, blk, blk, blk),
        compiler_params=_params(("arbitrary",)),
    )(c_t, dmod_blk, w, m, v)


def _inproj_fwd(x2, vecs, w_in_p, tm):
    seq = x2.shape[0]

    def body(x_ref, vec_ref, w_ref, p_ref, u_ref):
        xh, _ = _ln(x_ref[...])
        u = (xh * (1.0 + vec_ref[1:2, :]) + vec_ref[0:1, :]).astype(MXU_DTYPE)
        u_ref[...] = u
        p_ref[...] = _mm(u, w_ref[...])

    return pl.pallas_call(
        body, name="inproj_fwd", grid=(seq // tm,),
        out_shape=(jax.ShapeDtypeStruct((seq, N_PROJ), jnp.float32), jax.ShapeDtypeStruct((seq, D_MODEL), MXU_DTYPE)),
        in_specs=[pl.BlockSpec((tm, D_MODEL), lambda i: (i, 0)), _const_spec(vecs.shape), _const_spec(w_in_p.shape)],
        out_specs=(pl.BlockSpec((tm, N_PROJ), lambda i: (i, 0)), pl.BlockSpec((tm, D_MODEL), lambda i: (i, 0))),
        compiler_params=_params(("arbitrary",)),
    )(x2, vecs, w_in_p)


def _inproj_bwd(dproj, x2, dxa, vecs, w_in_pt, tm, riders):
    seq = x2.shape[0]
    n_tiles = seq // tm
    n_ride = len(riders)

    def body(*refs):
        dp_ref, x_ref, dxa_ref, vec_ref, w_ref = refs[:5]
        ride_in, refs = refs[5:5 + n_ride], refs[5 + n_ride:]
        gx_ref, sums_ref = refs[:2]
        ride_out, sems = refs[2:2 + n_ride], refs[2 + n_ride:]
        exchange = _ChipScatter(ride_in, ride_out, sems)

        @pl.when(pl.program_id(0) == 0)
        def _():
            exchange.start()
            sums_ref[...] = jnp.zeros_like(sums_ref)

        du = _mm(dp_ref[...], w_ref[...])
        xh, rstd = _ln(x_ref[...])
        sums_ref[0:1, :] += _colsum(du)
        sums_ref[1:2, :] += _colsum(du * xh)
        gx_ref[...] = dxa_ref[...] + _ln_bwd(du * (1.0 + vec_ref[1:2, :]), xh, rstd)

        @pl.when(pl.program_id(0) == n_tiles - 1)
        def _():
            exchange.wait()

    tile = pl.BlockSpec((tm, D_MODEL), lambda i: (i, 0))
    hbm = pl.BlockSpec(memory_space=pl.ANY)
    return pl.pallas_call(
        body, name="inproj_bwd", grid=(n_tiles,),
        out_shape=(jax.ShapeDtypeStruct((seq, D_MODEL), jnp.float32), jax.ShapeDtypeStruct((8, D_MODEL), jnp.float32))
        + _exchange_out_shapes(riders, False),
        in_specs=[pl.BlockSpec((tm, N_PROJ), lambda i: (i, 0)), tile, tile, _const_spec(vecs.shape),
                  _const_spec(w_in_pt.shape)] + [hbm] * n_ride,
        out_specs=(tile, pl.BlockSpec((8, D_MODEL), lambda i: (0, 0))) + (hbm,) * n_ride,
        scratch_shapes=_scatter_sems(n_ride),
        compiler_params=_params(("arbitrary",)),
    )(dproj, x2, dxa, vecs, w_in_pt, *riders)


def _head(h):
    return slice(h * HEAD_W, (h + 1) * HEAD_W)


def _cols(ref, off, h):
    return ref[:, off + h * HEAD_W:off + (h + 1) * HEAD_W]


HEADS = range(N_HEADS)


def _mixer_chunk_forward(p_ref, cc, ss, dm_ref, qdec_ref, kdec_ref, wg_ref, bg_ref, ret_state, gla_state_t):
    row, col = _tri_masks()
    lower = row >= col
    f = {}
    f["glr"] = p_ref[:, OFF_LR:OFF_LR + HEAD_W]
    f["logit"] = _mm32(f["glr"], wg_ref[...]) + bg_ref[...]
    rq = [_cols(p_ref, OFF_RQ, h) for h in HEADS]
    rk = [_cols(p_ref, OFF_RK, h) for h in HEADS]
    f["rv"] = [_cols(p_ref, OFF_RV, h) for h in HEADS]
    f["qr"] = [(rq[h] * cc + _swap_halves(rq[h]) * ss) * RET_SCALE for h in HEADS]
    f["kr"] = [rk[h] * cc + _swap_halves(rk[h]) * ss for h in HEADS]
    s_raw = [_mm_nt(f["qr"][h], f["kr"][h]) for h in HEADS]
    la = _log_sigmoid(f["logit"]) * (1.0 / GATE_TAU)
    b = _mm32(lower.astype(jnp.float32), la)
    f["qd"] = [f["qr"][h] * qdec_ref[:, _head(h)] for h in HEADS]
    f["kd"] = [f["kr"][h] * kdec_ref[:, _head(h)] for h in HEADS]
    f["scores"] = [s_raw[h] * dm_ref[h] for h in HEADS]
    f["o_ret"] = [_mm(f["scores"][h], f["rv"][h]) + _mm(f["qd"][h], ret_state[h]) for h in HEADS]
    b_last = b[CHUNK - 1:CHUNK, :]
    b_mid = b[CHUNK // 2 - 1:CHUNK // 2, :]
    f["e"], f["ei"] = jnp.exp(b - b_mid), jnp.exp(b_mid - b)
    f["eb"], f["ek"], f["ebl"] = jnp.exp(b), jnp.exp(b_last - b), jnp.exp(b_last)
    gq = [_cols(p_ref, OFF_GQ, h) * GLA_SCALE for h in HEADS]
    gk = [_cols(p_ref, OFF_GK, h) for h in HEADS]
    f["gv"] = [_cols(p_ref, OFF_GV, h) for h in HEADS]
    f["q_e"] = [gq[h] * f["e"][:, _head(h)] for h in HEADS]
    f["q_i"] = [gq[h] * f["ei"][:, _head(h)] for h in HEADS]
    f["k_e"] = [gk[h] * f["e"][:, _head(h)] for h in HEADS]
    f["k_i"] = [gk[h] * f["ei"][:, _head(h)] for h in HEADS]
    low = [_mm_nt(f["q_e"][h], f["k_i"][h]) for h in HEADS]
    up = [_mm_nt(f["q_i"][h], f["k_e"][h]) for h in HEADS]
    f["att"] = [jnp.where(lower, low[h], up[h]) for h in HEADS]
    f["qb"] = [gq[h] * f["eb"][:, _head(h)] for h in HEADS]
    f["kb"] = [gk[h] * f["ek"][:, _head(h)] for h in HEADS]
    f["o_gla"] = [_mm(f["att"][h], f["gv"][h]) + _mm_nt(f["qb"][h], gla_state_t[h]) for h in HEADS]
    return f


def _mixer_fwd(proj, tables, wg_p, bg_p, ret_norm_w, gla_norm_w, riders):
    seq = proj.shape[0]
    n_chunks = seq // CHUNK
    n_ride = len(riders)
    rot_a, rot_b, dm_t, qdec_t, kdec_t, chunk_decay = tables

    def body(*refs):
        p_ref, ra_ref, rb_ref, dm_ref, qdec_ref, kdec_ref, wg_ref, bg_ref, wr_ref, wl_ref = refs[:10]
        ride_in, refs = refs[10:10 + n_ride], refs[10 + n_ride:]
        mix_ref, rsave_ref, ssave_ref = refs[:3]
        ride_out, refs = refs[3:3 + n_ride], refs[3 + n_ride:]
        r_sc, s_sc = refs[:2]
        gather = _ChipGather(ride_in, ride_out, refs[2:])

        @pl.when(pl.program_id(0) == 0)
        def _():
            gather.start()
            r_sc[...] = jnp.zeros_like(r_sc)
            s_sc[...] = jnp.zeros_like(s_sc)

        ret_state = [r_sc[h] for h in HEADS]
        gla_state_t = [s_sc[h] for h in HEADS]
        for h in HEADS:
            rsave_ref[0, h] = ret_state[h]
            ssave_ref[0, h] = gla_state_t[h]
        cc, ss = _rotary_chunk(ra_ref, rb_ref)
        f = _mixer_chunk_forward(p_ref, cc, ss, dm_ref, qdec_ref, kdec_ref, wg_ref, bg_ref, ret_state, gla_state_t)
        for h in HEADS:
            r_sc[h] = chunk_decay[h] * ret_state[h] + _mm_tn(f["kd"][h], f["rv"][h])
        for h in HEADS:
            s_sc[h] = gla_state_t[h] * f["ebl"][:, _head(h)] + _mm_tn(f["gv"][h], f["kb"][h])
        for h in HEADS:
            on, _ = _ln(f["o_ret"][h])
            g = _cols(p_ref, OFF_RG, h)
            mix_ref[:, _head(h)] = (on * wr_ref[:, _head(h)] * (g * _sigmoid(g))).astype(mix_ref.dtype)
        for h in HEADS:
            o = f["o_gla"][h]
            on = o * lax.rsqrt(_rowmean(o * o) + LN_EPS)
            g = _cols(p_ref, OFF_GG, h)
            mix_ref[:, _head(N_HEADS + h)] = (on * wl_ref[:, _head(h)] * (g * _sigmoid(g))).astype(mix_ref.dtype)

        @pl.when(pl.program_id(0) == (3 * n_chunks) // 4)
        def _():
            gather.forward()

        @pl.when(pl.program_id(0) == n_chunks - 1)
        def _():
            gather.finish()

    state_shape = (n_chunks, N_HEADS, HEAD_W, HEAD_W)
    state_blk = pl.BlockSpec((1, N_HEADS, HEAD_W, HEAD_W), lambda i: (i, 0, 0, 0))
    rot_blk = pl.BlockSpec((1, 8, HEAD_W), lambda i: (i, 0, 0))
    hbm = pl.BlockSpec(memory_space=pl.ANY)
    return pl.pallas_call(
        body, name="mixer_fwd", grid=(n_chunks,),
        out_shape=(jax.ShapeDtypeStruct((seq, D_MODEL), MXU_DTYPE),
                   jax.ShapeDtypeStruct(state_shape, jnp.float32), jax.ShapeDtypeStruct(state_shape, jnp.float32))
        + _exchange_out_shapes(riders, True),
        in_specs=[pl.BlockSpec((CHUNK, N_PROJ), lambda i: (i, 0)), rot_blk, _const_spec(rot_b.shape),
                  _const_spec(dm_t.shape), _const_spec(qdec_t.shape), _const_spec(kdec_t.shape),
                  _const_spec(wg_p.shape), _const_spec(bg_p.shape), _const_spec(ret_norm_w.shape),
                  _const_spec(gla_norm_w.shape)] + [hbm] * n_ride,
        out_specs=(pl.BlockSpec((CHUNK, D_MODEL), lambda i: (i, 0)), state_blk, state_blk) + (hbm,) * n_ride,
        scratch_shapes=[pltpu.VMEM((N_HEADS, HEAD_W, HEAD_W), jnp.float32),
                        pltpu.VMEM((N_HEADS, HEAD_W, HEAD_W), jnp.float32)] + _gather_sems(n_ride),
        compiler_params=_params(("arbitrary",)),
    )(proj, rot_a, rot_b, dm_t, qdec_t, kdec_t, wg_p, bg_p, ret_norm_w, gla_norm_w, *riders)


def _mixer_bwd(proj, dmixed, rsave, ssave, tables, wg_p, bg_p, ret_norm_w, gla_norm_w, riders):
    seq = proj.shape[0]
    n_chunks = seq // CHUNK
    n_ride = len(riders)
    rot_a, rot_b, dm_t, qdec_t, kdec_t, chunk_decay = tables
    last = n_chunks - 1

    def body(*refs):
        (p_ref, dmx_ref, rsave_ref, ssave_ref, ra_ref, rb_ref, dm_ref, qdec_ref, kdec_ref, wg_ref, bg_ref,
         wr_ref, wl_ref) = refs[:13]
        ride_in, refs = refs[13:13 + n_ride], refs[13 + n_ride:]
        dp_ref, dwr_ref, dwl_ref, dwg_ref, dbg_ref = refs[:5]
        ride_out, refs = refs[5:5 + n_ride], refs[5 + n_ride:]
        dr_sc, ds_sc = refs[:2]
        exchange = _ChipScatter(ride_in, ride_out, refs[2:])

        @pl.when(pl.program_id(0) == 0)
        def _():
            exchange.start()
            dr_sc[...] = jnp.zeros_like(dr_sc)
            ds_sc[...] = jnp.zeros_like(ds_sc)
            dwr_ref[...] = jnp.zeros_like(dwr_ref)
            dwl_ref[...] = jnp.zeros_like(dwl_ref)
            dwg_ref[...] = jnp.zeros_like(dwg_ref)
            dbg_ref[...] = jnp.zeros_like(dbg_ref)

        cc, ss = _rotary_chunk(ra_ref, rb_ref)
        row, col = _tri_masks()
        ret_state = [rsave_ref[0, h] for h in HEADS]
        gla_state_t = [ssave_ref[0, h] for h in HEADS]
        d_ret_new = [dr_sc[h] for h in HEADS]
        d_gla_new = [ds_sc[h] for h in HEADS]
        f = _mixer_chunk_forward(p_ref, cc, ss, dm_ref, qdec_ref, kdec_ref, wg_ref, bg_ref, ret_state, gla_state_t)

        do_ret, do_gla = [], []
        for h in HEADS:
            on, rstd = _ln(f["o_ret"][h])
            g = _cols(p_ref, OFF_RG, h)
            sg = _sigmoid(g)
            dy = dmx_ref[:, _head(h)].astype(jnp.float32)
            wr = wr_ref[:, _head(h)]
            dwr_ref[:, _head(h)] += _colsum(dy * on * (g * sg))
            dp_ref[:, OFF_RG + h * HEAD_W:OFF_RG + (h + 1) * HEAD_W] = dy * on * wr * (sg * (1.0 + g * (1.0 - sg)))
            do_ret.append(_ln_bwd(dy * wr * (g * sg), on, rstd))
        for h in HEADS:
            o = f["o_gla"][h]
            rstd = lax.rsqrt(_rowmean(o * o) + LN_EPS)
            on = o * rstd
            g = _cols(p_ref, OFF_GG, h)
            sg = _sigmoid(g)
            dy = dmx_ref[:, _head(N_HEADS + h)].astype(jnp.float32)
            wl = wl_ref[:, _head(h)]
            dwl_ref[:, _head(h)] += _colsum(dy * on * (g * sg))
            dp_ref[:, OFF_GG + h * HEAD_W:OFF_GG + (h + 1) * HEAD_W] = dy * on * wl * (sg * (1.0 + g * (1.0 - sg)))
            don = dy * wl * (g * sg)
            do_gla.append(rstd * (don - on * _rowmean(don * on)))

        ds_raw = [_mm_nt(do_ret[h], f["rv"][h]) * dm_ref[h] for h in HEADS]
        d_att = [_mm_nt(do_gla[h], f["gv"][h]) for h in HEADS]
        dq_state = [_mm_nt(do_ret[h], ret_state[h]) for h in HEADS]
        dk_state = [_mm_nt(f["rv"][h], d_ret_new[h]) for h in HEADS]
        dqb = [_mm(do_gla[h], gla_state_t[h]) for h in HEADS]
        dkb = [_mm(f["gv"][h], d_gla_new[h]) for h in HEADS]
        for h in HEADS:
            dp_ref[:, OFF_RV + h * HEAD_W:OFF_RV + (h + 1) * HEAD_W] = (
                _mm_tn(f["scores"][h], do_ret[h]) + _mm(f["kd"][h], d_ret_new[h]))
        for h in HEADS:
            dp_ref[:, OFF_GV + h * HEAD_W:OFF_GV + (h + 1) * HEAD_W] = (
                _mm_tn(f["att"][h], do_gla[h]) + _mm_nt(f["kb"][h], d_gla_new[h]))
        for h in HEADS:
            dr_sc[h] = chunk_decay[h] * d_ret_new[h] + _mm_tn(f["qd"][h], do_ret[h])
        for h in HEADS:
            ds_sc[h] = d_gla_new[h] * f["ebl"][:, _head(h)] + _mm_tn(do_gla[h], f["qb"][h])

        dqr = [_mm(ds_raw[h], f["kr"][h]) + dq_state[h] * qdec_ref[:, _head(h)] for h in HEADS]
        dkr = [_mm_tn(ds_raw[h], f["qr"][h]) + dk_state[h] * kdec_ref[:, _head(h)] for h in HEADS]
        d_low = [jnp.where(row >= col, d_att[h], 0.0) for h in HEADS]
        d_up = [jnp.where(row < col, d_att[h], 0.0) for h in HEADS]
        dq_e = [_mm(d_low[h], f["k_i"][h]) for h in HEADS]
        dk_i = [_mm_tn(d_low[h], f["q_e"][h]) for h in HEADS]
        dq_i = [_mm(d_up[h], f["k_e"][h]) for h in HEADS]
        dk_e = [_mm_tn(d_up[h], f["q_i"][h]) for h in HEADS]
        for h in HEADS:
            dp_ref[:, OFF_RQ + h * HEAD_W:OFF_RQ + (h + 1) * HEAD_W] = (
                (dqr[h] * cc + _swap_halves(dqr[h] * ss)) * RET_SCALE)
            dp_ref[:, OFF_RK + h * HEAD_W:OFF_RK + (h + 1) * HEAD_W] = dkr[h] * cc + _swap_halves(dkr[h] * ss)
        row_id = lax.broadcasted_iota(jnp.int32, (CHUNK, HEAD_W), 0)
        db_heads = []
        for h in HEADS:
            hs = _head(h)
            e, ei, eb, ek, ebl = f["e"][:, hs], f["ei"][:, hs], f["eb"][:, hs], f["ek"][:, hs], f["ebl"][:, hs]
            dp_ref[:, OFF_GQ + h * HEAD_W:OFF_GQ + (h + 1) * HEAD_W] = (
                (dq_e[h] * e + dq_i[h] * ei + dqb[h] * eb) * GLA_SCALE)
            dp_ref[:, OFF_GK + h * HEAD_W:OFF_GK + (h + 1) * HEAD_W] = dk_e[h] * e + dk_i[h] * ei + dkb[h] * ek
            db = (dq_e[h] * f["q_e"][h] - dq_i[h] * f["q_i"][h] + dk_e[h] * f["k_e"][h] - dk_i[h] * f["k_i"][h]
                  + dqb[h] * f["qb"][h] - dkb[h] * f["kb"][h])
            db_last = _colsum(dkb[h] * f["kb"][h]) + ebl * _colsum(gla_state_t[h] * d_gla_new[h])
            db_heads.append(db + jnp.where(row_id == CHUNK - 1, db_last, 0.0))
        db = jnp.concatenate(db_heads, axis=1)
        d_la = _mm32((col >= row).astype(jnp.float32), db)
        d_logit = d_la * (1.0 / GATE_TAU) * (1.0 - _sigmoid(f["logit"]))
        dp_ref[:, OFF_LR:OFF_LR + HEAD_W] = _mm32_nt(d_logit, wg_ref[...])
        dwg_ref[...] += _mm32_tn(f["glr"], d_logit)
        dbg_ref[...] += _colsum(d_logit)

        @pl.when(pl.program_id(0) == last)
        def _():
            exchange.wait()

    state_blk = pl.BlockSpec((1, N_HEADS, HEAD_W, HEAD_W), lambda i: (last - i, 0, 0, 0))
    rot_blk = pl.BlockSpec((1, 8, HEAD_W), lambda i: (last - i, 0, 0))
    width = N_HEADS * HEAD_W
    vec_out = pl.BlockSpec((1, width), lambda i: (0, 0))
    hbm = pl.BlockSpec(memory_space=pl.ANY)
    return pl.pallas_call(
        body, name="mixer_bwd", grid=(n_chunks,),
        out_shape=(jax.ShapeDtypeStruct((seq, N_PROJ), jnp.float32),
                   jax.ShapeDtypeStruct((1, width), jnp.float32), jax.ShapeDtypeStruct((1, width), jnp.float32),
                   jax.ShapeDtypeStruct((HEAD_W, width), jnp.float32), jax.ShapeDtypeStruct((1, width), jnp.float32))
        + _exchange_out_shapes(riders, False),
        in_specs=[pl.BlockSpec((CHUNK, N_PROJ), lambda i: (last - i, 0)),
                  pl.BlockSpec((CHUNK, D_MODEL), lambda i: (last - i, 0)), state_blk, state_blk, rot_blk,
                  _const_spec(rot_b.shape),
                  _const_spec(dm_t.shape), _const_spec(qdec_t.shape), _const_spec(kdec_t.shape),
                  _const_spec(wg_p.shape), _const_spec(bg_p.shape), _const_spec(ret_norm_w.shape),
                  _const_spec(gla_norm_w.shape)] + [hbm] * n_ride,
        out_specs=(pl.BlockSpec((CHUNK, N_PROJ), lambda i: (last - i, 0)), vec_out, vec_out,
                   pl.BlockSpec((HEAD_W, width), lambda i: (0, 0)), vec_out) + (hbm,) * n_ride,
        scratch_shapes=[pltpu.VMEM((N_HEADS, HEAD_W, HEAD_W), jnp.float32),
                        pltpu.VMEM((N_HEADS, HEAD_W, HEAD_W), jnp.float32)] + _scatter_sems(n_ride),
        compiler_params=_params(("arbitrary",)),
    )(proj, dmixed, rsave, ssave, rot_a, rot_b, dm_t, qdec_t, kdec_t, wg_p, bg_p, ret_norm_w, gla_norm_w, *riders)


V_GATE1, V_SCALE2, V_SHIFT2, V_GATE2, V_LN1W, V_LN1B, V_LN2W, V_LN2B = range(8)
S_GATE1, S_SCALE2, S_SHIFT2, S_GATE2, S_LN1W, S_LN1B, S_LN2W, S_LN2B, S_LOSS = range(9)


def _mlp_fwd_bwd(x2, mixed, target, vecs, w_out, w1_chunks, w2_chunks, tm):
    seq = x2.shape[0]
    n_fc, _, fc = w1_chunks.shape

    def body(x_ref, mx_ref, t_ref, vec_ref, wo_ref, w1_ref, w2_ref,
             dmx_ref, dxa_ref, a_ref, dh_ref, u2_ref, df_ref, dm_ref, sums_ref, relu_sc):
        @pl.when(pl.program_id(0) == 0)
        def _():
            sums_ref[...] = jnp.zeros_like(sums_ref)

        vec = lambda r: vec_ref[r:r + 1, :]

        def acc(r, val):
            sums_ref[r:r + 1, :] += _colsum(val)

        xx = x_ref[...]
        m = _mm(mx_ref[...], wo_ref[...])
        z1h, rstd1 = _ln(ALPHA * xx + vec(V_GATE1) * m)
        x1 = z1h * vec(V_LN1W) + vec(V_LN1B)
        x1h, rstd0 = _ln(x1)
        u2 = (x1h * (1.0 + vec(V_SCALE2)) + vec(V_SHIFT2)).astype(MXU_DTYPE)
        u2_ref[...] = u2
        f = jnp.zeros((tm, D_MODEL), jnp.float32)
        for j in range(n_fc):
            r = jnp.maximum(_mm(u2, w1_ref[j]), 0.0)
            relu_sc[:, j * fc:(j + 1) * fc] = r
            a = (r * r).astype(MXU_DTYPE)
            a_ref[:, j * fc:(j + 1) * fc] = a
            f = f + _mm(a, w2_ref[j])
        z2h, rstd2 = _ln(ALPHA * x1 + vec(V_GATE2) * f)
        err = z2h * vec(V_LN2W) + vec(V_LN2B) - t_ref[...]
        acc(S_LOSS, err * err)
        dy = err * (1.0 / D_MODEL)
        acc(S_LN2W, dy * z2h)
        acc(S_LN2B, dy)
        dz2 = _ln_bwd(dy * vec(V_LN2W), z2h, rstd2)
        acc(S_GATE2, dz2 * f)
        df = (vec(V_GATE2) * dz2).astype(MXU_DTYPE)
        df_ref[...] = df
        du2 = jnp.zeros((tm, D_MODEL), jnp.float32)
        for j in range(n_fc):
            dh = (_mm_nt(df, w2_ref[j]) * (2.0 * relu_sc[:, j * fc:(j + 1) * fc])).astype(MXU_DTYPE)
            dh_ref[:, j * fc:(j + 1) * fc] = dh
            du2 = du2 + _mm_nt(dh, w1_ref[j])
        acc(S_SCALE2, du2 * x1h)
        acc(S_SHIFT2, du2)
        dx1 = ALPHA * dz2 + _ln_bwd(du2 * (1.0 + vec(V_SCALE2)), x1h, rstd0)
        acc(S_LN1W, dx1 * z1h)
        acc(S_LN1B, dx1)
        dz1 = _ln_bwd(dx1 * vec(V_LN1W), z1h, rstd1)
        acc(S_GATE1, dz1 * m)
        dxa_ref[...] = ALPHA * dz1
        dm = (vec(V_GATE1) * dz1).astype(MXU_DTYPE)
        dm_ref[...] = dm
        dmx_ref[...] = _mm_nt(dm, wo_ref[...])

    tile = lambda width: pl.BlockSpec((tm, width), lambda i: (i, 0))
    f32 = lambda width: jax.ShapeDtypeStruct((seq, width), jnp.float32)
    b16 = lambda width: jax.ShapeDtypeStruct((seq, width), MXU_DTYPE)
    return pl.pallas_call(
        body, name="mlp_fwd_bwd", grid=(seq // tm,),
        out_shape=(f32(D_MODEL), f32(D_MODEL), b16(D_FF), b16(D_FF), b16(D_MODEL), b16(D_MODEL), b16(D_MODEL),
                   jax.ShapeDtypeStruct((16, D_MODEL), jnp.float32)),
        in_specs=[tile(D_MODEL), tile(D_MODEL), tile(D_MODEL), _const_spec(vecs.shape), _const_spec(w_out.shape),
                  _const_spec(w1_chunks.shape), _const_spec(w2_chunks.shape)],
        out_specs=(tile(D_MODEL), tile(D_MODEL), tile(D_FF), tile(D_FF), tile(D_MODEL), tile(D_MODEL),
                   tile(D_MODEL), pl.BlockSpec((16, D_MODEL), lambda i: (0, 0))),
        scratch_shapes=[pltpu.VMEM((tm, D_FF), jnp.float32)],
        compiler_params=_params(("arbitrary",)),
    )(x2, mixed, target, vecs, w_out, w1_chunks, w2_chunks)


def _grad_matmul(a, b, name, tn, blocks_are_rows):
    seq, m_dim = a.shape
    n_dim = b.shape[1]
    tk = min(seq, 512)
    nk = seq // tk
    if blocks_are_rows:
        tm = m_dim // N_CHIP
        assert tn == n_dim
        grid = (N_CHIP, 1, nk)
        out_map = lambda i, j, k: (i, 0, 0)
    else:
        tm = m_dim
        assert tn * N_CHIP == n_dim
        grid = (1, N_CHIP, nk)
        out_map = lambda i, j, k: (j, 0, 0)

    def body(a_ref, b_ref, o_ref, acc_sc):
        k = pl.program_id(2)

        @pl.when(k == 0)
        def _():
            acc_sc[...] = jnp.zeros_like(acc_sc)

        acc_sc[...] += _mm_tn(a_ref[...], b_ref[...])

        @pl.when(k == nk - 1)
        def _():
            o_ref[0] = acc_sc[...].astype(o_ref.dtype)

    return pl.pallas_call(
        body, name=name, grid=grid,
        out_shape=jax.ShapeDtypeStruct((N_CHIP, tm, tn), WIRE_DTYPE),
        in_specs=[pl.BlockSpec((tk, tm), lambda i, j, k: (k, i)), pl.BlockSpec((tk, tn), lambda i, j, k: (k, j))],
        out_specs=pl.BlockSpec((1, tm, tn), out_map),
        scratch_shapes=[pltpu.VMEM((tm, tn), jnp.float32)],
        compiler_params=_params(("arbitrary", "arbitrary", "arbitrary")),
    )(a, b)


def _grad_matmul_full(a, b, name, tm):
    seq, m_dim = a.shape
    n_dim = b.shape[1]
    tk = min(seq, 512)
    nk = seq // tk
    assert m_dim % tm == 0

    def body(a_ref, b_ref, o_ref, acc_sc):
        k = pl.program_id(1)

        @pl.when(k == 0)
        def _():
            acc_sc[...] = jnp.zeros_like(acc_sc)

        acc_sc[...] += _mm_tn(a_ref[...], b_ref[...])

        @pl.when(k == nk - 1)
        def _():
            o_ref[...] = acc_sc[...].astype(o_ref.dtype)

    return pl.pallas_call(
        body, name=name, grid=(m_dim // tm, nk),
        out_shape=jax.ShapeDtypeStruct((m_dim, n_dim), WIRE_DTYPE),
        in_specs=[pl.BlockSpec((tk, tm), lambda i, k: (k, i)), pl.BlockSpec((tk, n_dim), lambda i, k: (k, 0))],
        out_specs=pl.BlockSpec((tm, n_dim), lambda i, k: (i, 0)),
        scratch_shapes=[pltpu.VMEM((tm, n_dim), jnp.float32)],
        compiler_params=_params(("arbitrary", "arbitrary")),
    )(a, b)


def _sum_chips(stack, name):
    _, rows, cols = stack.shape
    tc = min(cols, ELEMENTWISE_COLS)

    def body(s_ref, o_ref):
        total = s_ref[0].astype(jnp.float32)
        for j in range(1, N_CHIP):
            total = total + s_ref[j].astype(jnp.float32)
        o_ref[...] = total

    return pl.pallas_call(
        body, name=name, grid=(cols // tc,),
        out_shape=jax.ShapeDtypeStruct((rows, cols), jnp.float32),
        in_specs=[pl.BlockSpec((N_CHIP, rows, tc), lambda i: (0, 0, i))],
        out_specs=pl.BlockSpec((rows, tc), lambda i: (0, i)),
        compiler_params=_params(("arbitrary",)),
    )(stack)


def _adam_pair(w, g_mine, g_sibling, m, v, name):
    rows, cols = w.shape
    tc = min(cols, ELEMENTWISE_COLS)

    def body(w_ref, ga_ref, gb_ref, m_ref, v_ref, g_ref, dl_ref, m2_ref, v2_ref):
        g = ga_ref[...] + gb_ref[...]
        delta, m2, v2 = _adam(w_ref[...], g, m_ref[...], v_ref[...])
        g_ref[...] = g
        dl_ref[...] = delta
        m2_ref[...] = m2
        v2_ref[...] = v2

    blk = pl.BlockSpec((rows, tc), lambda i: (0, i))
    out = jax.ShapeDtypeStruct((rows, cols), jnp.float32)
    return pl.pallas_call(
        body, name=name, grid=(cols // tc,),
        out_shape=(out, out, out, out),
        in_specs=[blk] * 5, out_specs=(blk,) * 4,
        compiler_params=_params(("arbitrary",)),
    )(w, g_mine, g_sibling, m, v)


def _sum_devices(gathered):
    _, rows, _ = gathered.shape

    def body(g_ref, o_ref):
        total = g_ref[0]
        for d in range(1, N_DEV):
            total = total + g_ref[d]
        o_ref[...] = total

    return pl.pallas_call(
        body, name="sum_devices",
        out_shape=jax.ShapeDtypeStruct((rows, 128), jnp.float32),
    )(gathered)


def _adam_small(w, g, m, v):
    def body(w_ref, g_ref, m_ref, v_ref, dl_ref, m2_ref, v2_ref):
        delta, m2, v2 = _adam(w_ref[...], g_ref[...], m_ref[...], v_ref[...])
        dl_ref[...] = delta
        m2_ref[...] = m2
        v2_ref[...] = v2

    out = jax.ShapeDtypeStruct(w.shape, jnp.float32)
    return pl.pallas_call(body, name="adam_small", out_shape=(out, out, out))(w, g, m, v)


def _pad_heads(w):
    lead = w.shape[:-1]
    w = w.reshape(lead + (N_HEADS, GLA_DK))
    w = jnp.pad(w, [(0, 0)] * len(lead) + [(0, 0), (0, HEAD_W - GLA_DK)])
    return w.reshape(lead + (N_HEADS * HEAD_W,))


def _unpad_heads(w):
    lead = w.shape[:-1]
    return w.reshape(lead + (N_HEADS, HEAD_W))[..., :GLA_DK].reshape(lead + (N_HEADS * GLA_DK,))


def _pad_head_rows(w):
    w = w.reshape(N_HEADS, GLA_DK, w.shape[-1])
    return jnp.pad(w, ((0, 0), (0, HEAD_W - GLA_DK), (0, 0))).reshape(N_HEADS * HEAD_W, w.shape[-1])


def _unpad_head_rows(w):
    return w.reshape(N_HEADS, HEAD_W, w.shape[-1])[:, :GLA_DK].reshape(N_HEADS * GLA_DK, w.shape[-1])


def _pad_w_in_rows(w):
    return jnp.concatenate([
        w[:2048], _pad_head_rows(w[2048:2304]), _pad_head_rows(w[2304:2560]), w[2560:3584],
        jnp.pad(w[3584:3600], ((0, HEAD_W - GATE_RANK), (0, 0)))], axis=0)


def _unpad_w_in_rows(g):
    return jnp.concatenate([
        g[:2048], _unpad_head_rows(g[OFF_GQ:OFF_GQ + 512]), _unpad_head_rows(g[OFF_GK:OFF_GK + 512]),
        g[OFF_GV:OFF_LR], g[OFF_LR:OFF_LR + GATE_RANK]], axis=0)


def _rows128(a):
    return a.reshape(-1, 128)


def _rows8(a):
    a = a.reshape(-1, 128)
    return jnp.pad(a, ((0, -a.shape[0] % 8), (0, 0)))


def kernel(x, c, w_ada, b_ada, w_in, ret_norm_w, gla_gate_w, gla_gate_b, gla_norm_w, w_out, ln1_w, ln1_b, w_ff1, w_ff2, ln2_w, ln2_b, loss_target, m_w_ada, m_b_ada, m_w_in, m_ret_norm_w, m_gla_gate_w, m_gla_gate_b, m_gla_norm_w, m_w_out, m_ln1_w, m_ln1_b, m_w_ff1, m_w_ff2, m_ln2_w, m_ln2_b, v_w_ada, v_b_ada, v_w_in, v_ret_norm_w, v_gla_gate_w, v_gla_gate_b, v_gla_norm_w, v_w_out, v_ln1_w, v_ln1_b, v_w_ff1, v_w_ff2, v_ln2_w, v_ln2_b):
    seq = x.shape[1]
    tm = min(seq, TOKEN_TILE)
    xi, yi, ci = _mesh_pos()
    dev = 4 * xi + 2 * yi + ci
    chip = 2 * xi + yi
    x2, target = x[0], loss_target[0]
    ada_cols = w_ada.shape[2]
    in_cols = w_in.shape[2]
    gate_cols = gla_gate_w.shape[2]

    g0 = _gather_rows(jnp.concatenate([_rows128(c), _rows128(gla_gate_w[0])], axis=0), "gather_cond")
    c_all = g0[:, :8].reshape(N_DEV, D_MODEL)
    gate_w_full = jnp.concatenate([g0[2 * j, 8:16].reshape(GATE_RANK, gate_cols) for j in range(N_CHIP)], axis=1)
    wg_p = jnp.pad(_pad_heads(gate_w_full), ((0, HEAD_W - GATE_RANK), (0, 0)))
    bg_p = _pad_heads(gla_gate_b)

    b_blk = lax.dynamic_slice(b_ada, (0, chip * ada_cols), (1, ada_cols))
    mod_blk = _ada_fwd(c_all, w_ada[0], b_blk)
    g1 = _gather_rows(_rows128(mod_blk), "gather_mod")
    mod_all = jnp.concatenate([g1[2 * j].reshape(N_DEV, ada_cols) for j in range(N_CHIP)], axis=1)
    mod = lax.dynamic_slice(mod_all, (dev, 0), (1, 6 * D_MODEL))
    shift1, scale1, gate1, shift2, scale2, gate2 = [mod[:, i * D_MODEL:(i + 1) * D_MODEL] for i in range(6)]

    (w_in_stack,) = _chip_gather([jnp.transpose(w_in[0]).astype(WIRE_DTYPE)], "gather_w_in")
    w_in_pt = _pad_w_in_rows(w_in_stack.reshape(N_PROJ_SRC, D_MODEL)).astype(MXU_DTYPE)
    w_in_p = jnp.transpose(w_in_pt)

    zeros_row = jnp.zeros((1, D_MODEL), jnp.float32)
    vecs1 = jnp.concatenate([shift1, scale1] + [zeros_row] * 6, axis=0)
    proj, u = _inproj_fwd(x2, vecs1, w_in_p, tm)
    rot_a, rot_b = _rotary_tables(seq)
    dm_t, qdec_t, kdec_t, chunk_decay = _decay_tables()
    tables = (rot_a, rot_b, dm_t, qdec_t, kdec_t, chunk_decay)
    mixed, rsave, ssave, w_out_stack, w1_stack, w2_stack = _mixer_fwd(
        proj, tables, wg_p, bg_p, ret_norm_w, gla_norm_w,
        [w_out[0].astype(WIRE_DTYPE), w_ff1[0].astype(WIRE_DTYPE), w_ff2[0].astype(WIRE_DTYPE)])
    w_out_full = w_out_stack.reshape(D_MODEL, D_MODEL).astype(MXU_DTYPE)
    w1_chunks = w1_stack.astype(MXU_DTYPE)
    w2_chunks = w2_stack.astype(MXU_DTYPE)

    vecs2 = jnp.concatenate([gate1, scale2, shift2, gate2, ln1_w, ln1_b, ln2_w, ln2_b], axis=0)
    dmixed, dxa, act, dh, u2, df, dm, sums2 = _mlp_fwd_bwd(x2, mixed, target, vecs2, w_out_full, w1_chunks,
                                                           w2_chunks, tm)

    g_out_stack = _grad_matmul(mixed, dm, "grad_w_out", D_MODEL, True)
    g_ff1_stack = _grad_matmul(u2, dh, "grad_w_ff1", D_FF // N_CHIP, False)
    g_ff2_stack = _grad_matmul(act, df, "grad_w_ff2", D_MODEL, True)
    dproj, d_ret_norm, d_gla_norm, d_wg_p, d_bg_p, r_out, r_ff1, r_ff2 = _mixer_bwd(
        proj, dmixed, rsave, ssave, tables, wg_p, bg_p, ret_norm_w, gla_norm_w,
        [g_out_stack, g_ff1_stack, g_ff2_stack])
    g_in_t = _grad_matmul_full(dproj, u, "grad_w_in", N_PROJ // 3)
    g_in_stack = _unpad_w_in_rows(g_in_t).reshape(N_CHIP, in_cols, D_MODEL)
    grad_x, sums1, r_in = _inproj_bwd(dproj, x2, dxa, vecs1, w_in_pt, tm, [g_in_stack])

    dmod = jnp.concatenate([sums1[0:1], sums1[1:2], sums2[S_GATE1:S_GATE1 + 1], sums2[S_SHIFT2:S_SHIFT2 + 1],
                            sums2[S_SCALE2:S_SCALE2 + 1], sums2[S_GATE2:S_GATE2 + 1]], axis=1)
    d_gate_w_full = _unpad_heads(d_wg_p[:GATE_RANK])
    flat = lambda parts: jnp.concatenate([_rows8(p) for p in parts], axis=0)
    small = flat([dmod, sums2[S_LN1W:S_LN1W + 1], sums2[S_LN1B:S_LN1B + 1], sums2[S_LN2W:S_LN2W + 1],
                  sums2[S_LN2B:S_LN2B + 1], d_ret_norm, _unpad_heads(d_bg_p), d_gla_norm, d_gate_w_full,
                  sums2[S_LOSS:S_LOSS + 1]])
    g2 = _gather_rows(small, "gather_small")
    tot = _sum_devices(g2)
    loss = 0.5 / D_MODEL * jnp.sum(tot[136:144])
    grad_b_ada = tot[0:48].reshape(1, 6 * D_MODEL)
    grad_ln1_w, grad_ln1_b = tot[48:56].reshape(1, D_MODEL), tot[56:64].reshape(1, D_MODEL)
    grad_ln2_w, grad_ln2_b = tot[64:72].reshape(1, D_MODEL), tot[72:80].reshape(1, D_MODEL)
    grad_ret_norm = tot[80:84].reshape(1, 512)
    grad_gate_b = tot[88:90].reshape(1, 256)
    grad_gla_norm = tot[96:100].reshape(1, 512)
    grad_gate_w = lax.dynamic_slice(tot[104:136].reshape(GATE_RANK, 256), (0, chip * gate_cols),
                                    (GATE_RANK, gate_cols))

    small_w = flat([b_ada, ln1_w, ln1_b, ln2_w, ln2_b, ret_norm_w, gla_gate_b, gla_norm_w, gla_gate_w[0]])
    small_g = flat([grad_b_ada, grad_ln1_w, grad_ln1_b, grad_ln2_w, grad_ln2_b, grad_ret_norm, grad_gate_b,
                    grad_gla_norm, grad_gate_w])
    small_m = flat([m_b_ada, m_ln1_w, m_ln1_b, m_ln2_w, m_ln2_b, m_ret_norm_w, m_gla_gate_b, m_gla_norm_w,
                    m_gla_gate_w[0]])
    small_v = flat([v_b_ada, v_ln1_w, v_ln1_b, v_ln2_w, v_ln2_b, v_ret_norm_w, v_gla_gate_b, v_gla_norm_w,
                    v_gla_gate_w[0]])
    small_out = _adam_small(small_w, small_g, small_m, small_v)

    def unflat(t):
        pieces, row = [], 0
        for shape in [(1, 6 * D_MODEL)] + [(1, D_MODEL)] * 4 + [(1, 512), (1, 256), (1, 512), (1, GATE_RANK, gate_cols)]:
            n = int(np.prod(shape)) // 128
            pieces.append(t[row:row + n].reshape(shape))
            row += -(-n // 8) * 8
        return pieces

    sm_delta, sm_m, sm_v = [unflat(t) for t in small_out]

    dmod_all = g2[:, 0:48].reshape(N_DEV, 6 * D_MODEL)
    dmod_blk = lax.dynamic_slice(dmod_all, (0, chip * ada_cols), (N_DEV, ada_cols))
    ada_out = _ada_bwd_adam(jnp.transpose(c_all), dmod_blk, w_ada[0], m_w_ada[0], v_w_ada[0])
    ada_g, ada_delta, ada_m, ada_v = [t[None] for t in ada_out]

    received = [r_in, r_out, r_ff1, r_ff2]
    names = ["w_in", "w_out", "w_ff1", "w_ff2"]
    partial = [_sum_chips(r, "sum_" + n) for r, n in zip(received, names)]
    swapped = _sibling_swap(partial, "swap_partials")
    big = {}
    for n, w, mine, theirs, m, v in zip(names, [w_in, w_out, w_ff1, w_ff2], partial, swapped,
                                        [m_w_in, m_w_out, m_w_ff1, m_w_ff2], [v_w_in, v_w_out, v_w_ff1, v_w_ff2]):
        if n == "w_in":
            out = _adam_pair(jnp.transpose(w[0]), mine, theirs, jnp.transpose(m[0]), jnp.transpose(v[0]), "adam_" + n)
            big[n] = [jnp.transpose(t)[None] for t in out]
        else:
            big[n] = [t[None] for t in _adam_pair(w[0], mine, theirs, m[0], v[0], "adam_" + n)]

    def assemble(ada, smalls, k):
        b_ada_o, ln1w_o, ln1b_o, ln2w_o, ln2b_o, ret_o, gb_o, gln_o, gw_o = smalls
        return [ada, b_ada_o, big["w_in"][k], ret_o, gw_o, gb_o, gln_o, big["w_out"][k], ln1w_o, ln1b_o,
                big["w_ff1"][k], big["w_ff2"][k], ln2w_o, ln2b_o]

    small_grads = [grad_b_ada, grad_ln1_w, grad_ln1_b, grad_ln2_w, grad_ln2_b, grad_ret_norm, grad_gate_b,
                   grad_gla_norm, grad_gate_w[None]]
    grads = assemble(ada_g, small_grads, 0)
    deltas = assemble(ada_delta, sm_delta, 1)
    new_m = assemble(ada_m, sm_m, 2)
    new_v = assemble(ada_v, sm_v, 3)
    return (loss, grad_x[None], *grads, *deltas, *new_m, *new_v)
```

```python
import functools

import numpy as np
import jax
import jax.numpy as jnp
from jax import lax
from jax.experimental import pallas as pl
from jax.experimental.pallas import tpu as pltpu

D_MODEL = 1024
D_FF = 4096
CHUNK = 64
N_HEADS = 4
HEAD_W = 128
GLA_DK = 64
GATE_RANK = 16
GATE_TAU = 16.0
LN_EPS = 1e-5
ALPHA = 2.0 ** 0.25
ROPE_BASE = 10000.0
RET_SCALE = float(HEAD_W) ** -0.5
GLA_SCALE = float(GLA_DK) ** -0.5

ADAM_LR = 0.001
ADAM_B1 = 0.9
ADAM_B2 = 0.999
ADAM_EPS = 1e-08
ADAM_WD = 0.01
ADAM_STEP = 10

OFF_RQ, OFF_RK, OFF_RV, OFF_RG = 0, 512, 1024, 1536
OFF_GQ, OFF_GK, OFF_GV, OFF_GG, OFF_LR = 2048, 2560, 3072, 3584, 4096
N_PROJ = 4224
N_PROJ_SRC = 3600

N_DEV = 8
N_CHIP = 4
MESH = pl.DeviceIdType.MESH
MXU_DTYPE = jnp.bfloat16
WIRE_DTYPE = jnp.bfloat16
VMEM_LIMIT = 60 * 1024 * 1024
TOKEN_TILE = 256
INPROJ_TOKEN_TILE = 512
GRAD_TOKEN_TILE = 2048
ELEMENTWISE_COLS = 256
HIGHEST = lax.Precision.HIGHEST


def _mm(a, b):
    return jnp.dot(a.astype(MXU_DTYPE), b.astype(MXU_DTYPE), preferred_element_type=jnp.float32)


def _mm_nt(a, b):
    return lax.dot_general(a.astype(MXU_DTYPE), b.astype(MXU_DTYPE), (((1,), (1,)), ((), ())),
                           preferred_element_type=jnp.float32)


def _mm_tn(a, b):
    return lax.dot_general(a.astype(MXU_DTYPE), b.astype(MXU_DTYPE), (((0,), (0,)), ((), ())),
                           preferred_element_type=jnp.float32)


def _mm32(a, b):
    return jnp.dot(a, b, precision=HIGHEST, preferred_element_type=jnp.float32)


def _mm32_nt(a, b):
    return lax.dot_general(a, b, (((1,), (1,)), ((), ())), precision=HIGHEST, preferred_element_type=jnp.float32)


def _mm32_tn(a, b):
    return lax.dot_general(a, b, (((0,), (0,)), ((), ())), precision=HIGHEST, preferred_element_type=jnp.float32)


def _rowmean(a):
    return jnp.mean(a, axis=-1, keepdims=True)


def _colsum(a):
    return jnp.sum(a, axis=0, keepdims=True)


def _ln(z):
    zc = z - _rowmean(z)
    rstd = lax.rsqrt(_rowmean(zc * zc) + LN_EPS)
    return zc * rstd, rstd


def _ln_bwd(dzh, zh, rstd):
    return rstd * (dzh - _rowmean(dzh) - zh * _rowmean(dzh * zh))


def _sigmoid(a):
    return 1.0 / (1.0 + jnp.exp(-a))


def _log_sigmoid(a):
    return jnp.minimum(a, 0.0) - jnp.log(1.0 + jnp.exp(-jnp.abs(a)))


def _swap_halves(a):
    return pltpu.roll(a, HEAD_W // 2, 1)


def _tri_masks():
    row = lax.broadcasted_iota(jnp.int32, (CHUNK, CHUNK), 0)
    col = lax.broadcasted_iota(jnp.int32, (CHUNK, CHUNK), 1)
    return row, col


def _const_spec(shape):
    zeros = (0,) * len(shape)
    return pl.BlockSpec(shape, lambda *_: zeros, pipeline_mode=pl.Buffered(1))


def _params(semantics):
    return pltpu.CompilerParams(dimension_semantics=semantics, vmem_limit_bytes=VMEM_LIMIT)


def _decay_tables():
    log_gamma = np.log(1.0 - 2.0 ** (-5.0 - np.arange(N_HEADS, dtype=np.float64)))
    idx = np.arange(CHUNK, dtype=np.float64)
    dist = np.abs(idx[:, None] - idx[None, :])
    intra = np.exp(log_gamma[:, None, None] * dist)
    kdec = np.exp(log_gamma[None, :] * (CHUNK - 1.0 - idx)[:, None])
    qdec = np.exp(log_gamma[None, :] * (idx + 1.0)[:, None])
    chunk_decay = np.exp(log_gamma * CHUNK)
    lanes = lambda t: np.repeat(t, HEAD_W, axis=1).astype(np.float32)
    return (jnp.asarray(intra.astype(np.float32)), jnp.asarray(lanes(qdec)), jnp.asarray(lanes(kdec)),
            [float(np.float32(v)) for v in chunk_decay])


def _rotary_tables(seq):
    half = HEAD_W // 2
    inv = 1.0 / (ROPE_BASE ** jnp.linspace(0.0, 1.0, half, dtype=jnp.float32))
    both = lambda t: jnp.concatenate([t, t], axis=-1)
    ang_a = jnp.arange(0, seq, CHUNK, dtype=jnp.float32)[:, None] * inv[None, :]
    rot_a = jnp.stack([both(jnp.cos(ang_a)), both(jnp.sin(ang_a))], axis=1)
    rot_a = jnp.pad(rot_a, ((0, 0), (0, 6), (0, 0)))
    ang_b = jnp.arange(CHUNK, dtype=jnp.float32)[:, None] * inv[None, :]
    cos_b, sin_b = both(jnp.cos(ang_b)), both(jnp.sin(ang_b))
    sign = jnp.concatenate([-jnp.ones((half,), jnp.float32), jnp.ones((half,), jnp.float32)])
    return rot_a, jnp.stack([cos_b, sin_b, cos_b * sign, sin_b * sign])


def _rotary_chunk(ra_ref, rb_ref):
    cos_a, sin_a = ra_ref[0, 0:1, :], ra_ref[0, 1:2, :]
    return cos_a * rb_ref[0] - sin_a * rb_ref[1], sin_a * rb_ref[2] + cos_a * rb_ref[3]


def _mesh_pos():
    return lax.axis_index("x"), lax.axis_index("y"), lax.axis_index("c")


def _flip(v, bit):
    return 1 - v if bit else v


def _gather_rows(v, name):
    rows = v.shape[0]

    def body(v_ref, out_ref, send_sems, recv_sems):
        x, y, c = _mesh_pos()
        me = 4 * x + 2 * y + c
        out_ref[me] = v_ref[...]
        sends, recvs = [], []
        for k in range(1, N_DEV):
            px, py, pc = _flip(x, (k >> 2) & 1), _flip(y, (k >> 1) & 1), _flip(c, k & 1)
            peer = 4 * px + 2 * py + pc
            sends.append(pltpu.make_async_remote_copy(
                src_ref=v_ref, dst_ref=out_ref.at[me], send_sem=send_sems.at[k - 1], recv_sem=recv_sems.at[k - 1],
                device_id=(px, py, pc), device_id_type=MESH))
            recvs.append(pltpu.make_async_remote_copy(
                src_ref=v_ref, dst_ref=out_ref.at[peer], send_sem=send_sems.at[k - 1], recv_sem=recv_sems.at[k - 1],
                device_id=(px, py, pc), device_id_type=MESH))
        for cp in sends:
            cp.start()
        for cp in recvs:
            cp.wait_recv()
        for cp in sends:
            cp.wait_send()

    return pl.pallas_call(
        body, name=name,
        out_shape=jax.ShapeDtypeStruct((N_DEV, rows, 128), jnp.float32),
        in_specs=[pl.BlockSpec(memory_space=pltpu.VMEM)],
        out_specs=pl.BlockSpec(memory_space=pltpu.VMEM),
        scratch_shapes=[pltpu.SemaphoreType.DMA((N_DEV - 1,)), pltpu.SemaphoreType.DMA((N_DEV - 1,))],
    )(v)


def _chip_gather(arrays, name):
    n = len(arrays)

    def body(*refs):
        gather = _ChipGather(refs[:n], refs[n:2 * n], refs[2 * n:])
        gather.start()
        gather.forward()
        gather.finish()

    return pl.pallas_call(
        body, name=name,
        out_shape=_exchange_out_shapes(arrays, True),
        in_specs=[pl.BlockSpec(memory_space=pl.ANY)] * n,
        out_specs=tuple(pl.BlockSpec(memory_space=pl.ANY) for _ in arrays),
        scratch_shapes=_gather_sems(n),
    )(*arrays)


def _exchange_out_shapes(arrays, gather):
    return tuple(jax.ShapeDtypeStruct((N_CHIP,) + a.shape if gather else a.shape, a.dtype) for a in arrays)


def _scatter_sems(n):
    n_sem = n * (N_CHIP - 1)
    return [pltpu.SemaphoreType.DMA((n_sem,)), pltpu.SemaphoreType.DMA((n_sem,)), pltpu.SemaphoreType.DMA((n,))]


def _gather_sems(n):
    n_sem = n * (N_CHIP - 1)
    return [pltpu.SemaphoreType.DMA((n_sem,))] * 4 + [pltpu.SemaphoreType.DMA((n,))]


def _peer_chips(x, y):
    out = []
    for k in range(1, N_CHIP):
        px, py = _flip(x, (k >> 1) & 1), _flip(y, k & 1)
        out.append((px, py, 2 * px + py))
    return out


class _ChipScatter:
    def __init__(self, ins, outs, sems):
        send_sems, recv_sems, local_sems = sems
        x, y, c = _mesh_pos()
        chip = 2 * x + y
        self.local, self.sends, self.recvs = [], [], []
        for i in range(len(ins)):
            self.local.append(pltpu.make_async_copy(ins[i].at[chip], outs[i].at[chip], local_sems.at[i]))
            for k, (px, py, peer_chip) in enumerate(_peer_chips(x, y)):
                sem = i * (N_CHIP - 1) + k
                src = ins[i].at[peer_chip]
                self.sends.append(pltpu.make_async_remote_copy(
                    src_ref=src, dst_ref=outs[i].at[chip], send_sem=send_sems.at[sem], recv_sem=recv_sems.at[sem],
                    device_id=(px, py, c), device_id_type=MESH))
                self.recvs.append(pltpu.make_async_remote_copy(
                    src_ref=src, dst_ref=outs[i].at[peer_chip], send_sem=send_sems.at[sem], recv_sem=recv_sems.at[sem],
                    device_id=(px, py, c), device_id_type=MESH))

    def start(self):
        for cp in self.local + self.sends:
            cp.start()

    def wait(self):
        for cp in self.recvs:
            cp.wait_recv()
        for cp in self.sends:
            cp.wait_send()
        for cp in self.local:
            cp.wait()


class _ChipGather:
    def __init__(self, ins, outs, sems):
        ici_send, ici_recv, d2d_send, d2d_recv, local_sems = sems
        x, y, c = _mesh_pos()
        chip = 2 * x + y
        self.local, self.ici_sends, self.ici_recvs, self.d2d_sends, self.d2d_recvs = [], [], [], [], []
        for i in range(len(ins)):
            half = ins[i].shape[-1] // 2
            assert half % 128 == 0
            lead = (slice(None),) * (len(ins[i].shape) - 1)
            mine = lead + (pl.ds(pl.multiple_of(c * half, 128), half),)
            theirs = lead + (pl.ds(pl.multiple_of((1 - c) * half, 128), half),)
            self.local.append(pltpu.make_async_copy(ins[i], outs[i].at[chip], local_sems.at[i]))
            for k, (px, py, peer_chip) in enumerate(_peer_chips(x, y)):
                sem = i * (N_CHIP - 1) + k
                self.ici_sends.append(pltpu.make_async_remote_copy(
                    src_ref=ins[i].at[mine], dst_ref=outs[i].at[chip].at[mine],
                    send_sem=ici_send.at[sem], recv_sem=ici_recv.at[sem], device_id=(px, py, c), device_id_type=MESH))
                landed = outs[i].at[peer_chip].at[mine]
                self.ici_recvs.append(pltpu.make_async_remote_copy(
                    src_ref=ins[i].at[mine], dst_ref=landed,
                    send_sem=ici_send.at[sem], recv_sem=ici_recv.at[sem], device_id=(px, py, c), device_id_type=MESH))
                self.d2d_sends.append(pltpu.make_async_remote_copy(
                    src_ref=landed, dst_ref=landed,
                    send_sem=d2d_send.at[sem], recv_sem=d2d_recv.at[sem], device_id=(x, y, 1 - c), device_id_type=MESH))
                self.d2d_recvs.append(pltpu.make_async_remote_copy(
                    src_ref=landed, dst_ref=outs[i].at[peer_chip].at[theirs],
                    send_sem=d2d_send.at[sem], recv_sem=d2d_recv.at[sem], device_id=(x, y, 1 - c), device_id_type=MESH))

    def start(self):
        for cp in self.local + self.ici_sends:
            cp.start()

    def forward(self):
        for landed, onward in zip(self.ici_recvs, self.d2d_sends):
            landed.wait_recv()
            onward.start()

    def finish(self):
        for cp in self.d2d_recvs:
            cp.wait_recv()
        for cp in self.d2d_sends + self.ici_sends:
            cp.wait_send()
        for cp in self.local:
            cp.wait()


def _sibling_swap(arrays, name):
    n = len(arrays)

    def body(*refs):
        ins, outs = refs[:n], refs[n:2 * n]
        send_sems, recv_sems = refs[2 * n:]
        x, y, c = _mesh_pos()
        copies = [pltpu.make_async_remote_copy(
            src_ref=ins[i], dst_ref=outs[i], send_sem=send_sems.at[i], recv_sem=recv_sems.at[i],
            device_id=(x, y, 1 - c), device_id_type=MESH) for i in range(n)]
        for cp in copies:
            cp.start()
        for cp in copies:
            cp.wait_recv()
        for cp in copies:
            cp.wait_send()

    return pl.pallas_call(
        body, name=name,
        out_shape=tuple(jax.ShapeDtypeStruct(a.shape, a.dtype) for a in arrays),
        in_specs=[pl.BlockSpec(memory_space=pl.ANY)] * n,
        out_specs=tuple(pl.BlockSpec(memory_space=pl.ANY) for _ in arrays),
        scratch_shapes=[pltpu.SemaphoreType.DMA((n,)), pltpu.SemaphoreType.DMA((n,))],
    )(*arrays)


def _ada_fwd(c_all, w_ada_blk, b_blk):
    cols = w_ada_blk.shape[1]

    def body(c_ref, w_ref, b_ref, out_ref):
        cv = c_ref[...]
        out_ref[...] = _mm32(cv * _sigmoid(cv), w_ref[...]) + b_ref[...]

    return pl.pallas_call(
        body, name="ada_fwd",
        out_shape=jax.ShapeDtypeStruct((N_DEV, cols), jnp.float32),
        compiler_params=pltpu.CompilerParams(vmem_limit_bytes=VMEM_LIMIT),
    )(c_all, w_ada_blk, b_blk)


def _adam(w, g, m, v):
    m2 = ADAM_B1 * m + (1.0 - ADAM_B1) * g
    v2 = ADAM_B2 * v + (1.0 - ADAM_B2) * (g * g)
    m_hat = m2 / (1.0 - ADAM_B1 ** ADAM_STEP)
    v_hat = v2 / (1.0 - ADAM_B2 ** ADAM_STEP)
    delta = -ADAM_LR * (m_hat / (jnp.sqrt(v_hat) + ADAM_EPS) + ADAM_WD * w)
    return delta, m2, v2


def _ada_bwd_adam(c_t, dmod_blk, w, m, v):
    rows, cols = w.shape
    tile = 512
    assert cols % tile == 0

    def body(c_ref, d_ref, w_ref, m_ref, v_ref, g_ref, dl_ref, m2_ref, v2_ref):
        sc = c_ref[...]
        sc = sc * _sigmoid(sc)
        dm = d_ref[...]
        g = sc[:, 0:1] * dm[0:1, :]
        for b in range(1, N_DEV):
            g = g + sc[:, b:b + 1] * dm[b:b + 1, :]
        delta, m2, v2 = _adam(w_ref[...], g, m_ref[...], v_ref[...])
        g_ref[...] = g
        dl_ref[...] = delta
        m2_ref[...] = m2
        v2_ref[...] = v2

    blk = pl.BlockSpec((rows, tile), lambda j: (0, j))
    out = jax.ShapeDtypeStruct((rows, cols), jnp.float32)
    return pl.pallas_call(
        body, name="ada_bwd_adam", grid=(cols // tile,),
        out_shape=(out, out, out, out),
        in_specs=[pl.BlockSpec((rows, N_DEV), lambda j: (0, 0)), pl.BlockSpec((N_DEV, tile), lambda j: (0, j)),
                  blk, blk, blk],
        out_specs=(blk, blk, blk, blk),
        compiler_params=_params(("arbitrary",)),
    )(c_t, dmod_blk, w, m, v)


def _inproj_fwd(x2, vecs, w_in_p, tm):
    seq = x2.shape[0]

    def body(x_ref, vec_ref, w_ref, p_ref, u_ref):
        xh, _ = _ln(x_ref[...])
        u = (xh * (1.0 + vec_ref[1:2, :]) + vec_ref[0:1, :]).astype(MXU_DTYPE)
        u_ref[...] = u
        p_ref[...] = _mm(u, w_ref[...])

    return pl.pallas_call(
        body, name="inproj_fwd", grid=(seq // tm,),
        out_shape=(jax.ShapeDtypeStruct((seq, N_PROJ), jnp.float32), jax.ShapeDtypeStruct((seq, D_MODEL), MXU_DTYPE)),
        in_specs=[pl.BlockSpec((tm, D_MODEL), lambda i: (i, 0)), _const_spec(vecs.shape), _const_spec(w_in_p.shape)],
        out_specs=(pl.BlockSpec((tm, N_PROJ), lambda i: (i, 0)), pl.BlockSpec((tm, D_MODEL), lambda i: (i, 0))),
        compiler_params=_params(("arbitrary",)),
    )(x2, vecs, w_in_p)


def _inproj_bwd(dproj, x2, dxa, vecs, w_in_pt, tm, riders):
    seq = x2.shape[0]
    n_tiles = seq // tm
    n_ride = len(riders)

    def body(*refs):
        dp_ref, x_ref, dxa_ref, vec_ref, w_ref = refs[:5]
        ride_in, refs = refs[5:5 + n_ride], refs[5 + n_ride:]
        gx_ref, sums_ref = refs[:2]
        ride_out, sems = refs[2:2 + n_ride], refs[2 + n_ride:]
        exchange = _ChipScatter(ride_in, ride_out, sems)

        @pl.when(pl.program_id(0) == 0)
        def _():
            exchange.start()
            sums_ref[...] = jnp.zeros_like(sums_ref)

        du = _mm(dp_ref[...], w_ref[...])
        xh, rstd = _ln(x_ref[...])
        sums_ref[0:1, :] += _colsum(du)
        sums_ref[1:2, :] += _colsum(du * xh)
        gx_ref[...] = dxa_ref[...] + _ln_bwd(du * (1.0 + vec_ref[1:2, :]), xh, rstd)

        @pl.when(pl.program_id(0) == n_tiles - 1)
        def _():
            exchange.wait()

    tile = pl.BlockSpec((tm, D_MODEL), lambda i: (i, 0))
    hbm = pl.BlockSpec(memory_space=pl.ANY)
    return pl.pallas_call(
        body, name="inproj_bwd", grid=(n_tiles,),
        out_shape=(jax.ShapeDtypeStruct((seq, D_MODEL), jnp.float32), jax.ShapeDtypeStruct((8, D_MODEL), jnp.float32))
        + _exchange_out_shapes(riders, False),
        in_specs=[pl.BlockSpec((tm, N_PROJ), lambda i: (i, 0)), tile, tile, _const_spec(vecs.shape),
                  _const_spec(w_in_pt.shape)] + [hbm] * n_ride,
        out_specs=(tile, pl.BlockSpec((8, D_MODEL), lambda i: (0, 0))) + (hbm,) * n_ride,
        scratch_shapes=_scatter_sems(n_ride),
        compiler_params=_params(("arbitrary",)),
    )(dproj, x2, dxa, vecs, w_in_pt, *riders)


def _head(h):
    return slice(h * HEAD_W, (h + 1) * HEAD_W)


def _cols(ref, off, h):
    return ref[:, off + h * HEAD_W:off + (h + 1) * HEAD_W]


HEADS = range(N_HEADS)


def _mixer_chunk_forward(p_ref, cc, ss, dm_ref, qdec_ref, kdec_ref, wg_ref, bg_ref, ret_state, gla_state_t):
    row, col = _tri_masks()
    lower = row >= col
    f = {}
    f["glr"] = p_ref[:, OFF_LR:OFF_LR + HEAD_W]
    f["logit"] = _mm32(f["glr"], wg_ref[...]) + bg_ref[...]
    rq = [_cols(p_ref, OFF_RQ, h) for h in HEADS]
    rk = [_cols(p_ref, OFF_RK, h) for h in HEADS]
    f["rv"] = [_cols(p_ref, OFF_RV, h) for h in HEADS]
    f["qr"] = [(rq[h] * cc + _swap_halves(rq[h]) * ss) * RET_SCALE for h in HEADS]
    f["kr"] = [rk[h] * cc + _swap_halves(rk[h]) * ss for h in HEADS]
    s_raw = [_mm_nt(f["qr"][h], f["kr"][h]) for h in HEADS]
    la = _log_sigmoid(f["logit"]) * (1.0 / GATE_TAU)
    b = _mm32(lower.astype(jnp.float32), la)
    f["qd"] = [f["qr"][h] * qdec_ref[:, _head(h)] for h in HEADS]
    f["kd"] = [f["kr"][h] * kdec_ref[:, _head(h)] for h in HEADS]
    f["scores"] = [s_raw[h] * dm_ref[h] for h in HEADS]
    f["o_ret"] = [_mm(f["scores"][h], f["rv"][h]) + _mm(f["qd"][h], ret_state[h]) for h in HEADS]
    b_last = b[CHUNK - 1:CHUNK, :]
    b_mid = b[CHUNK // 2 - 1:CHUNK // 2, :]
    f["e"], f["ei"] = jnp.exp(b - b_mid), jnp.exp(b_mid - b)
    f["eb"], f["ek"], f["ebl"] = jnp.exp(b), jnp.exp(b_last - b), jnp.exp(b_last)
    gq = [_cols(p_ref, OFF_GQ, h) * GLA_SCALE for h in HEADS]
    gk = [_cols(p_ref, OFF_GK, h) for h in HEADS]
    f["gv"] = [_cols(p_ref, OFF_GV, h) for h in HEADS]
    f["q_e"] = [gq[h] * f["e"][:, _head(h)] for h in HEADS]
    f["q_i"] = [gq[h] * f["ei"][:, _head(h)] for h in HEADS]
    f["k_e"] = [gk[h] * f["e"][:, _head(h)] for h in HEADS]
    f["k_i"] = [gk[h] * f["ei"][:, _head(h)] for h in HEADS]
    low = [_mm_nt(f["q_e"][h], f["k_i"][h]) for h in HEADS]
    up = [_mm_nt(f["q_i"][h], f["k_e"][h]) for h in HEADS]
    f["att"] = [jnp.where(lower, low[h], up[h]) for h in HEADS]
    f["qb"] = [gq[h] * f["eb"][:, _head(h)] for h in HEADS]
    f["kb"] = [gk[h] * f["ek"][:, _head(h)] for h in HEADS]
    f["o_gla"] = [_mm(f["att"][h], f["gv"][h]) + _mm_nt(f["qb"][h], gla_state_t[h]) for h in HEADS]
    return f


def _mixer_fwd(proj, tables, wg_p, bg_p, ret_norm_w, gla_norm_w, riders):
    seq = proj.shape[0]
    n_chunks = seq // CHUNK
    n_ride = len(riders)
    rot_a, rot_b, dm_t, qdec_t, kdec_t, chunk_decay = tables

    def body(*refs):
        p_ref, ra_ref, rb_ref, dm_ref, qdec_ref, kdec_ref, wg_ref, bg_ref, wr_ref, wl_ref = refs[:10]
        ride_in, refs = refs[10:10 + n_ride], refs[10 + n_ride:]
        mix_ref, rsave_ref, ssave_ref = refs[:3]
        ride_out, refs = refs[3:3 + n_ride], refs[3 + n_ride:]
        r_sc, s_sc = refs[:2]
        gather = _ChipGather(ride_in, ride_out, refs[2:])

        @pl.when(pl.program_id(0) == 0)
        def _():
            gather.start()
            r_sc[...] = jnp.zeros_like(r_sc)
            s_sc[...] = jnp.zeros_like(s_sc)

        ret_state = [r_sc[h] for h in HEADS]
        gla_state_t = [s_sc[h] for h in HEADS]
        for h in HEADS:
            rsave_ref[0, h] = ret_state[h]
            ssave_ref[0, h] = gla_state_t[h]
        cc, ss = _rotary_chunk(ra_ref, rb_ref)
        f = _mixer_chunk_forward(p_ref, cc, ss, dm_ref, qdec_ref, kdec_ref, wg_ref, bg_ref, ret_state, gla_state_t)
        for h in HEADS:
            r_sc[h] = chunk_decay[h] * ret_state[h] + _mm_tn(f["kd"][h], f["rv"][h])
        for h in HEADS:
            s_sc[h] = gla_state_t[h] * f["ebl"][:, _head(h)] + _mm_tn(f["gv"][h], f["kb"][h])
        for h in HEADS:
            on, _ = _ln(f["o_ret"][h])
            g = _cols(p_ref, OFF_RG, h)
            mix_ref[:, _head(h)] = (on * wr_ref[:, _head(h)] * (g * _sigmoid(g))).astype(mix_ref.dtype)
        for h in HEADS:
            o = f["o_gla"][h]
            on = o * lax.rsqrt(_rowmean(o * o) + LN_EPS)
            g = _cols(p_ref, OFF_GG, h)
            mix_ref[:, _head(N_HEADS + h)] = (on * wl_ref[:, _head(h)] * (g * _sigmoid(g))).astype(mix_ref.dtype)

        @pl.when(pl.program_id(0) == (3 * n_chunks) // 4)
        def _():
            gather.forward()

        @pl.when(pl.program_id(0) == n_chunks - 1)
        def _():
            gather.finish()

    state_shape = (n_chunks, N_HEADS, HEAD_W, HEAD_W)
    state_blk = pl.BlockSpec((1, N_HEADS, HEAD_W, HEAD_W), lambda i: (i, 0, 0, 0))
    rot_blk = pl.BlockSpec((1, 8, HEAD_W), lambda i: (i, 0, 0))
    hbm = pl.BlockSpec(memory_space=pl.ANY)
    return pl.pallas_call(
        body, name="mixer_fwd", grid=(n_chunks,),
        out_shape=(jax.ShapeDtypeStruct((seq, D_MODEL), MXU_DTYPE),
                   jax.ShapeDtypeStruct(state_shape, jnp.float32), jax.ShapeDtypeStruct(state_shape, jnp.float32))
        + _exchange_out_shapes(riders, True),
        in_specs=[pl.BlockSpec((CHUNK, N_PROJ), lambda i: (i, 0)), rot_blk, _const_spec(rot_b.shape),
                  _const_spec(dm_t.shape), _const_spec(qdec_t.shape), _const_spec(kdec_t.shape),
                  _const_spec(wg_p.shape), _const_spec(bg_p.shape), _const_spec(ret_norm_w.shape),
                  _const_spec(gla_norm_w.shape)] + [hbm] * n_ride,
        out_specs=(pl.BlockSpec((CHUNK, D_MODEL), lambda i: (i, 0)), state_blk, state_blk) + (hbm,) * n_ride,
        scratch_shapes=[pltpu.VMEM((N_HEADS, HEAD_W, HEAD_W), jnp.float32),
                        pltpu.VMEM((N_HEADS, HEAD_W, HEAD_W), jnp.float32)] + _gather_sems(n_ride),
        compiler_params=_params(("arbitrary",)),
    )(proj, rot_a, rot_b, dm_t, qdec_t, kdec_t, wg_p, bg_p, ret_norm_w, gla_norm_w, *riders)


def _mixer_bwd(proj, dmixed, rsave, ssave, tables, wg_p, bg_p, ret_norm_w, gla_norm_w, riders):
    seq = proj.shape[0]
    n_chunks = seq // CHUNK
    n_ride = len(riders)
    rot_a, rot_b, dm_t, qdec_t, kdec_t, chunk_decay = tables
    last = n_chunks - 1

    def body(*refs):
        (p_ref, dmx_ref, rsave_ref, ssave_ref, ra_ref, rb_ref, dm_ref, qdec_ref, kdec_ref, wg_ref, bg_ref,
         wr_ref, wl_ref) = refs[:13]
        ride_in, refs = refs[13:13 + n_ride], refs[13 + n_ride:]
        dp_ref, dwr_ref, dwl_ref, dwg_ref, dbg_ref = refs[:5]
        ride_out, refs = refs[5:5 + n_ride], refs[5 + n_ride:]
        dr_sc, ds_sc = refs[:2]
        exchange = _ChipScatter(ride_in, ride_out, refs[2:])

        @pl.when(pl.program_id(0) == 0)
        def _():
            exchange.start()
            dr_sc[...] = jnp.zeros_like(dr_sc)
            ds_sc[...] = jnp.zeros_like(ds_sc)
            dwr_ref[...] = jnp.zeros_like(dwr_ref)
            dwl_ref[...] = jnp.zeros_like(dwl_ref)
            dwg_ref[...] = jnp.zeros_like(dwg_ref)
            dbg_ref[...] = jnp.zeros_like(dbg_ref)

        def put(off, h, val):
            dp_ref[:, off + h * HEAD_W:off + (h + 1) * HEAD_W] = val.astype(dp_ref.dtype)

        cc, ss = _rotary_chunk(ra_ref, rb_ref)
        row, col = _tri_masks()
        ret_state = [rsave_ref[0, h] for h in HEADS]
        gla_state_t = [ssave_ref[0, h] for h in HEADS]
        d_ret_new = [dr_sc[h] for h in HEADS]
        d_gla_new = [ds_sc[h] for h in HEADS]
        f = _mixer_chunk_forward(p_ref, cc, ss, dm_ref, qdec_ref, kdec_ref, wg_ref, bg_ref, ret_state, gla_state_t)

        do_ret, do_gla = [], []
        for h in HEADS:
            on, rstd = _ln(f["o_ret"][h])
            g = _cols(p_ref, OFF_RG, h)
            sg = _sigmoid(g)
            dy = dmx_ref[:, _head(h)].astype(jnp.float32)
            wr = wr_ref[:, _head(h)]
            dwr_ref[:, _head(h)] += _colsum(dy * on * (g * sg))
            put(OFF_RG, h, dy * on * wr * (sg * (1.0 + g * (1.0 - sg))))
            do_ret.append(_ln_bwd(dy * wr * (g * sg), on, rstd))
        for h in HEADS:
            o = f["o_gla"][h]
            rstd = lax.rsqrt(_rowmean(o * o) + LN_EPS)
            on = o * rstd
            g = _cols(p_ref, OFF_GG, h)
            sg = _sigmoid(g)
            dy = dmx_ref[:, _head(N_HEADS + h)].astype(jnp.float32)
            wl = wl_ref[:, _head(h)]
            dwl_ref[:, _head(h)] += _colsum(dy * on * (g * sg))
            put(OFF_GG, h, dy * on * wl * (sg * (1.0 + g * (1.0 - sg))))
            don = dy * wl * (g * sg)
            do_gla.append(rstd * (don - on * _rowmean(don * on)))

        ds_raw = [_mm_nt(do_ret[h], f["rv"][h]) * dm_ref[h] for h in HEADS]
        d_att = [_mm_nt(do_gla[h], f["gv"][h]) for h in HEADS]
        dq_state = [_mm_nt(do_ret[h], ret_state[h]) for h in HEADS]
        dk_state = [_mm_nt(f["rv"][h], d_ret_new[h]) for h in HEADS]
        dqb = [_mm(do_gla[h], gla_state_t[h]) for h in HEADS]
        dkb = [_mm(f["gv"][h], d_gla_new[h]) for h in HEADS]
        for h in HEADS:
            put(OFF_RV, h, _mm_tn(f["scores"][h], do_ret[h]) + _mm(f["kd"][h], d_ret_new[h]))
        for h in HEADS:
            put(OFF_GV, h, _mm_tn(f["att"][h], do_gla[h]) + _mm_nt(f["kb"][h], d_gla_new[h]))
        for h in HEADS:
            dr_sc[h] = chunk_decay[h] * d_ret_new[h] + _mm_tn(f["qd"][h], do_ret[h])
        for h in HEADS:
            ds_sc[h] = d_gla_new[h] * f["ebl"][:, _head(h)] + _mm_tn(do_gla[h], f["qb"][h])

        dqr = [_mm(ds_raw[h], f["kr"][h]) + dq_state[h] * qdec_ref[:, _head(h)] for h in HEADS]
        dkr = [_mm_tn(ds_raw[h], f["qr"][h]) + dk_state[h] * kdec_ref[:, _head(h)] for h in HEADS]
        d_low = [jnp.where(row >= col, d_att[h], 0.0) for h in HEADS]
        d_up = [jnp.where(row < col, d_att[h], 0.0) for h in HEADS]
        dq_e = [_mm(d_low[h], f["k_i"][h]) for h in HEADS]
        dk_i = [_mm_tn(d_low[h], f["q_e"][h]) for h in HEADS]
        dq_i = [_mm(d_up[h], f["k_e"][h]) for h in HEADS]
        dk_e = [_mm_tn(d_up[h], f["q_i"][h]) for h in HEADS]
        for h in HEADS:
            put(OFF_RQ, h, (dqr[h] * cc + _swap_halves(dqr[h] * ss)) * RET_SCALE)
            put(OFF_RK, h, dkr[h] * cc + _swap_halves(dkr[h] * ss))
        row_id = lax.broadcasted_iota(jnp.int32, (CHUNK, HEAD_W), 0)
        db_heads = []
        for h in HEADS:
            hs = _head(h)
            e, ei, eb, ek, ebl = f["e"][:, hs], f["ei"][:, hs], f["eb"][:, hs], f["ek"][:, hs], f["ebl"][:, hs]
            put(OFF_GQ, h, (dq_e[h] * e + dq_i[h] * ei + dqb[h] * eb) * GLA_SCALE)
            put(OFF_GK, h, dk_e[h] * e + dk_i[h] * ei + dkb[h] * ek)
            db = (dq_e[h] * f["q_e"][h] - dq_i[h] * f["q_i"][h] + dk_e[h] * f["k_e"][h] - dk_i[h] * f["k_i"][h]
                  + dqb[h] * f["qb"][h] - dkb[h] * f["kb"][h])
            db_last = _colsum(dkb[h] * f["kb"][h]) + ebl * _colsum(gla_state_t[h] * d_gla_new[h])
            db_heads.append(db + jnp.where(row_id == CHUNK - 1, db_last, 0.0))
        db = jnp.concatenate(db_heads, axis=1)
        d_la = _mm32((col >= row).astype(jnp.float32), db)
        d_logit = d_la * (1.0 / GATE_TAU) * (1.0 - _sigmoid(f["logit"]))
        put(OFF_LR, 0, _mm32_nt(d_logit, wg_ref[...]))
        dwg_ref[...] += _mm32_tn(f["glr"], d_logit)
        dbg_ref[...] += _colsum(d_logit)

        @pl.when(pl.program_id(0) == last)
        def _():
            exchange.wait()

    state_blk = pl.BlockSpec((1, N_HEADS, HEAD_W, HEAD_W), lambda i: (last - i, 0, 0, 0))
    rot_blk = pl.BlockSpec((1, 8, HEAD_W), lambda i: (last - i, 0, 0))
    width = N_HEADS * HEAD_W
    vec_out = pl.BlockSpec((1, width), lambda i: (0, 0))
    hbm = pl.BlockSpec(memory_space=pl.ANY)
    return pl.pallas_call(
        body, name="mixer_bwd", grid=(n_chunks,),
        out_shape=(jax.ShapeDtypeStruct((seq, N_PROJ), MXU_DTYPE),
                   jax.ShapeDtypeStruct((1, width), jnp.float32), jax.ShapeDtypeStruct((1, width), jnp.float32),
                   jax.ShapeDtypeStruct((HEAD_W, width), jnp.float32), jax.ShapeDtypeStruct((1, width), jnp.float32))
        + _exchange_out_shapes(riders, False),
        in_specs=[pl.BlockSpec((CHUNK, N_PROJ), lambda i: (last - i, 0)),
                  pl.BlockSpec((CHUNK, D_MODEL), lambda i: (last - i, 0)), state_blk, state_blk, rot_blk,
                  _const_spec(rot_b.shape),
                  _const_spec(dm_t.shape), _const_spec(qdec_t.shape), _const_spec(kdec_t.shape),
                  _const_spec(wg_p.shape), _const_spec(bg_p.shape), _const_spec(ret_norm_w.shape),
                  _const_spec(gla_norm_w.shape)] + [hbm] * n_ride,
        out_specs=(pl.BlockSpec((CHUNK, N_PROJ), lambda i: (last - i, 0)), vec_out, vec_out,
                   pl.BlockSpec((HEAD_W, width), lambda i: (0, 0)), vec_out) + (hbm,) * n_ride,
        scratch_shapes=[pltpu.VMEM((N_HEADS, HEAD_W, HEAD_W), jnp.float32),
                        pltpu.VMEM((N_HEADS, HEAD_W, HEAD_W), jnp.float32)] + _scatter_sems(n_ride),
        compiler_params=_params(("arbitrary",)),
    )(proj, dmixed, rsave, ssave, rot_a, rot_b, dm_t, qdec_t, kdec_t, wg_p, bg_p, ret_norm_w, gla_norm_w, *riders)


V_GATE1, V_SCALE2, V_SHIFT2, V_GATE2, V_LN1W, V_LN1B, V_LN2W, V_LN2B = range(8)
S_GATE1, S_SCALE2, S_SHIFT2, S_GATE2, S_LN1W, S_LN1B, S_LN2W, S_LN2B, S_LOSS = range(9)


def _mlp_fwd_bwd(x2, mixed, target, vecs, w_out, w1_chunks, w2_chunks, tm):
    seq = x2.shape[0]
    n_fc, _, fc = w1_chunks.shape

    def body(x_ref, mx_ref, t_ref, vec_ref, wo_ref, w1_ref, w2_ref,
             dmx_ref, dxa_ref, a_ref, dh_ref, u2_ref, df_ref, dm_ref, sums_ref, relu_sc):
        @pl.when(pl.program_id(0) == 0)
        def _():
            sums_ref[...] = jnp.zeros_like(sums_ref)

        vec = lambda r: vec_ref[r:r + 1, :]

        def acc(r, val):
            sums_ref[r:r + 1, :] += _colsum(val)

        xx = x_ref[...]
        m = _mm(mx_ref[...], wo_ref[...])
        z1h, rstd1 = _ln(ALPHA * xx + vec(V_GATE1) * m)
        x1 = z1h * vec(V_LN1W) + vec(V_LN1B)
        x1h, rstd0 = _ln(x1)
        u2 = (x1h * (1.0 + vec(V_SCALE2)) + vec(V_SHIFT2)).astype(MXU_DTYPE)
        u2_ref[...] = u2
        f = jnp.zeros((tm, D_MODEL), jnp.float32)
        for j in range(n_fc):
            r = jnp.maximum(_mm(u2, w1_ref[j]), 0.0)
            relu_sc[:, j * fc:(j + 1) * fc] = r
            a = (r * r).astype(MXU_DTYPE)
            a_ref[:, j * fc:(j + 1) * fc] = a
            f = f + _mm(a, w2_ref[j])
        z2h, rstd2 = _ln(ALPHA * x1 + vec(V_GATE2) * f)
        err = z2h * vec(V_LN2W) + vec(V_LN2B) - t_ref[...]
        acc(S_LOSS, err * err)
        dy = err * (1.0 / D_MODEL)
        acc(S_LN2W, dy * z2h)
        acc(S_LN2B, dy)
        dz2 = _ln_bwd(dy * vec(V_LN2W), z2h, rstd2)
        acc(S_GATE2, dz2 * f)
        df = (vec(V_GATE2) * dz2).astype(MXU_DTYPE)
        df_ref[...] = df
        du2 = jnp.zeros((tm, D_MODEL), jnp.float32)
        for j in range(n_fc):
            dh = (_mm_nt(df, w2_ref[j]) * (2.0 * relu_sc[:, j * fc:(j + 1) * fc])).astype(MXU_DTYPE)
            dh_ref[:, j * fc:(j + 1) * fc] = dh
            du2 = du2 + _mm_nt(dh, w1_ref[j])
        acc(S_SCALE2, du2 * x1h)
        acc(S_SHIFT2, du2)
        dx1 = ALPHA * dz2 + _ln_bwd(du2 * (1.0 + vec(V_SCALE2)), x1h, rstd0)
        acc(S_LN1W, dx1 * z1h)
        acc(S_LN1B, dx1)
        dz1 = _ln_bwd(dx1 * vec(V_LN1W), z1h, rstd1)
        acc(S_GATE1, dz1 * m)
        dxa_ref[...] = ALPHA * dz1
        dm = (vec(V_GATE1) * dz1).astype(MXU_DTYPE)
        dm_ref[...] = dm
        dmx_ref[...] = _mm_nt(dm, wo_ref[...])

    tile = lambda width: pl.BlockSpec((tm, width), lambda i: (i, 0))
    f32 = lambda width: jax.ShapeDtypeStruct((seq, width), jnp.float32)
    b16 = lambda width: jax.ShapeDtypeStruct((seq, width), MXU_DTYPE)
    return pl.pallas_call(
        body, name="mlp_fwd_bwd", grid=(seq // tm,),
        out_shape=(f32(D_MODEL), f32(D_MODEL), b16(D_FF), b16(D_FF), b16(D_MODEL), b16(D_MODEL), b16(D_MODEL),
                   jax.ShapeDtypeStruct((16, D_MODEL), jnp.float32)),
        in_specs=[tile(D_MODEL), tile(D_MODEL), tile(D_MODEL), _const_spec(vecs.shape), _const_spec(w_out.shape),
                  _const_spec(w1_chunks.shape), _const_spec(w2_chunks.shape)],
        out_specs=(tile(D_MODEL), tile(D_MODEL), tile(D_FF), tile(D_FF), tile(D_MODEL), tile(D_MODEL),
                   tile(D_MODEL), pl.BlockSpec((16, D_MODEL), lambda i: (0, 0))),
        scratch_shapes=[pltpu.VMEM((tm, D_FF), jnp.float32)],
        compiler_params=_params(("arbitrary",)),
    )(x2, mixed, target, vecs, w_out, w1_chunks, w2_chunks)


def _grad_matmul(a, b, name, tn, blocks_are_rows):
    seq, m_dim = a.shape
    n_dim = b.shape[1]
    tk = min(seq, GRAD_TOKEN_TILE)
    nk = seq // tk
    if blocks_are_rows:
        tm = m_dim // N_CHIP
        assert tn == n_dim
        grid = (N_CHIP, 1, nk)
        out_map = lambda i, j, k: (i, 0, 0)
    else:
        tm = m_dim
        assert tn * N_CHIP == n_dim
        grid = (1, N_CHIP, nk)
        out_map = lambda i, j, k: (j, 0, 0)

    def body(a_ref, b_ref, o_ref, acc_sc):
        k = pl.program_id(2)

        @pl.when(k == 0)
        def _():
            acc_sc[...] = jnp.zeros_like(acc_sc)

        acc_sc[...] += _mm_tn(a_ref[...], b_ref[...])

        @pl.when(k == nk - 1)
        def _():
            o_ref[0] = acc_sc[...].astype(o_ref.dtype)

    return pl.pallas_call(
        body, name=name, grid=grid,
        out_shape=jax.ShapeDtypeStruct((N_CHIP, tm, tn), WIRE_DTYPE),
        in_specs=[pl.BlockSpec((tk, tm), lambda i, j, k: (k, i)), pl.BlockSpec((tk, tn), lambda i, j, k: (k, j))],
        out_specs=pl.BlockSpec((1, tm, tn), out_map),
        scratch_shapes=[pltpu.VMEM((tm, tn), jnp.float32)],
        compiler_params=_params(("arbitrary", "arbitrary", "arbitrary")),
    )(a, b)


def _grad_matmul_full(a, b, name, tm):
    seq, m_dim = a.shape
    n_dim = b.shape[1]
    tk = min(seq, GRAD_TOKEN_TILE)
    nk = seq // tk
    assert m_dim % tm == 0

    def body(a_ref, b_ref, o_ref, acc_sc):
        k = pl.program_id(1)

        @pl.when(k == 0)
        def _():
            acc_sc[...] = jnp.zeros_like(acc_sc)

        acc_sc[...] += _mm_tn(a_ref[...], b_ref[...])

        @pl.when(k == nk - 1)
        def _():
            o_ref[...] = acc_sc[...].astype(o_ref.dtype)

    return pl.pallas_call(
        body, name=name, grid=(m_dim // tm, nk),
        out_shape=jax.ShapeDtypeStruct((m_dim, n_dim), WIRE_DTYPE),
        in_specs=[pl.BlockSpec((tk, tm), lambda i, k: (k, i)), pl.BlockSpec((tk, n_dim), lambda i, k: (k, 0))],
        out_specs=pl.BlockSpec((tm, n_dim), lambda i, k: (i, 0)),
        scratch_shapes=[pltpu.VMEM((tm, n_dim), jnp.float32)],
        compiler_params=_params(("arbitrary", "arbitrary")),
    )(a, b)


def _sum_chips(stack, name):
    _, rows, cols = stack.shape
    tc = min(cols, ELEMENTWISE_COLS)

    def body(s_ref, o_ref):
        total = s_ref[0].astype(jnp.float32)
        for j in range(1, N_CHIP):
            total = total + s_ref[j].astype(jnp.float32)
        o_ref[...] = total

    return pl.pallas_call(
        body, name=name, grid=(cols // tc,),
        out_shape=jax.ShapeDtypeStruct((rows, cols), jnp.float32),
        in_specs=[pl.BlockSpec((N_CHIP, rows, tc), lambda i: (0, 0, i))],
        out_specs=pl.BlockSpec((rows, tc), lambda i: (0, i)),
        compiler_params=_params(("arbitrary",)),
    )(stack)


def _adam_pair(w, g_mine, g_sibling, m, v, name):
    rows, cols = w.shape
    tc = min(cols, ELEMENTWISE_COLS)

    def body(w_ref, ga_ref, gb_ref, m_ref, v_ref, g_ref, dl_ref, m2_ref, v2_ref):
        g = ga_ref[...] + gb_ref[...]
        delta, m2, v2 = _adam(w_ref[...], g, m_ref[...], v_ref[...])
        g_ref[...] = g
        dl_ref[...] = delta
        m2_ref[...] = m2
        v2_ref[...] = v2

    blk = pl.BlockSpec((rows, tc), lambda i: (0, i))
    out = jax.ShapeDtypeStruct((rows, cols), jnp.float32)
    return pl.pallas_call(
        body, name=name, grid=(cols // tc,),
        out_shape=(out, out, out, out),
        in_specs=[blk] * 5, out_specs=(blk,) * 4,
        compiler_params=_params(("arbitrary",)),
    )(w, g_mine, g_sibling, m, v)


def _sum_devices(gathered):
    _, rows, _ = gathered.shape

    def body(g_ref, o_ref):
        total = g_ref[0]
        for d in range(1, N_DEV):
            total = total + g_ref[d]
        o_ref[...] = total

    return pl.pallas_call(
        body, name="sum_devices",
        out_shape=jax.ShapeDtypeStruct((rows, 128), jnp.float32),
    )(gathered)


def _adam_small(w, g, m, v):
    def body(w_ref, g_ref, m_ref, v_ref, dl_ref, m2_ref, v2_ref):
        delta, m2, v2 = _adam(w_ref[...], g_ref[...], m_ref[...], v_ref[...])
        dl_ref[...] = delta
        m2_ref[...] = m2
        v2_ref[...] = v2

    out = jax.ShapeDtypeStruct(w.shape, jnp.float32)
    return pl.pallas_call(body, name="adam_small", out_shape=(out, out, out))(w, g, m, v)


def _pad_heads(w):
    lead = w.shape[:-1]
    w = w.reshape(lead + (N_HEADS, GLA_DK))
    w = jnp.pad(w, [(0, 0)] * len(lead) + [(0, 0), (0, HEAD_W - GLA_DK)])
    return w.reshape(lead + (N_HEADS * HEAD_W,))


def _unpad_heads(w):
    lead = w.shape[:-1]
    return w.reshape(lead + (N_HEADS, HEAD_W))[..., :GLA_DK].reshape(lead + (N_HEADS * GLA_DK,))


def _pad_head_rows(w):
    w = w.reshape(N_HEADS, GLA_DK, w.shape[-1])
    return jnp.pad(w, ((0, 0), (0, HEAD_W - GLA_DK), (0, 0))).reshape(N_HEADS * HEAD_W, w.shape[-1])


def _unpad_head_rows(w):
    return w.reshape(N_HEADS, HEAD_W, w.shape[-1])[:, :GLA_DK].reshape(N_HEADS * GLA_DK, w.shape[-1])


def _pad_w_in_rows(w):
    return jnp.concatenate([
        w[:2048], _pad_head_rows(w[2048:2304]), _pad_head_rows(w[2304:2560]), w[2560:3584],
        jnp.pad(w[3584:3600], ((0, HEAD_W - GATE_RANK), (0, 0)))], axis=0)


def _unpad_w_in_rows(g):
    return jnp.concatenate([
        g[:2048], _unpad_head_rows(g[OFF_GQ:OFF_GQ + 512]), _unpad_head_rows(g[OFF_GK:OFF_GK + 512]),
        g[OFF_GV:OFF_LR], g[OFF_LR:OFF_LR + GATE_RANK]], axis=0)


def _rows128(a):
    return a.reshape(-1, 128)


def _rows8(a):
    a = a.reshape(-1, 128)
    return jnp.pad(a, ((0, -a.shape[0] % 8), (0, 0)))


def kernel(x, c, w_ada, b_ada, w_in, ret_norm_w, gla_gate_w, gla_gate_b, gla_norm_w, w_out, ln1_w, ln1_b, w_ff1, w_ff2, ln2_w, ln2_b, loss_target, m_w_ada, m_b_ada, m_w_in, m_ret_norm_w, m_gla_gate_w, m_gla_gate_b, m_gla_norm_w, m_w_out, m_ln1_w, m_ln1_b, m_w_ff1, m_w_ff2, m_ln2_w, m_ln2_b, v_w_ada, v_b_ada, v_w_in, v_ret_norm_w, v_gla_gate_w, v_gla_gate_b, v_gla_norm_w, v_w_out, v_ln1_w, v_ln1_b, v_w_ff1, v_w_ff2, v_ln2_w, v_ln2_b):
    seq = x.shape[1]
    tm = min(seq, TOKEN_TILE)
    tm_in = min(seq, INPROJ_TOKEN_TILE)
    xi, yi, ci = _mesh_pos()
    dev = 4 * xi + 2 * yi + ci
    chip = 2 * xi + yi
    x2, target = x[0], loss_target[0]
    ada_cols = w_ada.shape[2]
    in_cols = w_in.shape[2]
    gate_cols = gla_gate_w.shape[2]

    g0 = _gather_rows(jnp.concatenate([_rows128(c), _rows128(gla_gate_w[0])], axis=0), "gather_cond")
    c_all = g0[:, :8].reshape(N_DEV, D_MODEL)
    gate_w_full = jnp.concatenate([g0[2 * j, 8:16].reshape(GATE_RANK, gate_cols) for j in range(N_CHIP)], axis=1)
    wg_p = jnp.pad(_pad_heads(gate_w_full), ((0, HEAD_W - GATE_RANK), (0, 0)))
    bg_p = _pad_heads(gla_gate_b)

    b_blk = lax.dynamic_slice(b_ada, (0, chip * ada_cols), (1, ada_cols))
    mod_blk = _ada_fwd(c_all, w_ada[0], b_blk)
    g1 = _gather_rows(_rows128(mod_blk), "gather_mod")
    mod_all = jnp.concatenate([g1[2 * j].reshape(N_DEV, ada_cols) for j in range(N_CHIP)], axis=1)
    mod = lax.dynamic_slice(mod_all, (dev, 0), (1, 6 * D_MODEL))
    shift1, scale1, gate1, shift2, scale2, gate2 = [mod[:, i * D_MODEL:(i + 1) * D_MODEL] for i in range(6)]

    (w_in_stack,) = _chip_gather([jnp.transpose(w_in[0]).astype(WIRE_DTYPE)], "gather_w_in")
    w_in_pt = _pad_w_in_rows(w_in_stack.reshape(N_PROJ_SRC, D_MODEL)).astype(MXU_DTYPE)
    w_in_p = jnp.transpose(w_in_pt)

    zeros_row = jnp.zeros((1, D_MODEL), jnp.float32)
    vecs1 = jnp.concatenate([shift1, scale1] + [zeros_row] * 6, axis=0)
    proj, u = _inproj_fwd(x2, vecs1, w_in_p, tm_in)
    rot_a, rot_b = _rotary_tables(seq)
    dm_t, qdec_t, kdec_t, chunk_decay = _decay_tables()
    tables = (rot_a, rot_b, dm_t, qdec_t, kdec_t, chunk_decay)
    mixed, rsave, ssave, w_out_stack, w1_stack, w2_stack = _mixer_fwd(
        proj, tables, wg_p, bg_p, ret_norm_w, gla_norm_w,
        [w_out[0].astype(WIRE_DTYPE), w_ff1[0].astype(WIRE_DTYPE), w_ff2[0].astype(WIRE_DTYPE)])
    w_out_full = w_out_stack.reshape(D_MODEL, D_MODEL).astype(MXU_DTYPE)
    w1_chunks = w1_stack.astype(MXU_DTYPE)
    w2_chunks = w2_stack.astype(MXU_DTYPE)

    vecs2 = jnp.concatenate([gate1, scale2, shift2, gate2, ln1_w, ln1_b, ln2_w, ln2_b], axis=0)
    dmixed, dxa, act, dh, u2, df, dm, sums2 = _mlp_fwd_bwd(x2, mixed, target, vecs2, w_out_full, w1_chunks,
                                                           w2_chunks, tm)

    g_out_stack = _grad_matmul(mixed, dm, "grad_w_out", D_MODEL, True)
    g_ff1_stack = _grad_matmul(u2, dh, "grad_w_ff1", D_FF // N_CHIP, False)
    g_ff2_stack = _grad_matmul(act, df, "grad_w_ff2", D_MODEL, True)
    dproj, d_ret_norm, d_gla_norm, d_wg_p, d_bg_p, r_out, r_ff1, r_ff2 = _mixer_bwd(
        proj, dmixed, rsave, ssave, tables, wg_p, bg_p, ret_norm_w, gla_norm_w,
        [g_out_stack, g_ff1_stack, g_ff2_stack])
    g_in_t = _grad_matmul_full(dproj, u, "grad_w_in", N_PROJ // 3)
    g_in_stack = _unpad_w_in_rows(g_in_t).reshape(N_CHIP, in_cols, D_MODEL)
    grad_x, sums1, r_in = _inproj_bwd(dproj, x2, dxa, vecs1, w_in_pt, tm_in, [g_in_stack])

    dmod = jnp.concatenate([sums1[0:1], sums1[1:2], sums2[S_GATE1:S_GATE1 + 1], sums2[S_SHIFT2:S_SHIFT2 + 1],
                            sums2[S_SCALE2:S_SCALE2 + 1], sums2[S_GATE2:S_GATE2 + 1]], axis=1)
    d_gate_w_full = _unpad_heads(d_wg_p[:GATE_RANK])
    flat = lambda parts: jnp.concatenate([_rows8(p) for p in parts], axis=0)
    small = flat([dmod, sums2[S_LN1W:S_LN1W + 1], sums2[S_LN1B:S_LN1B + 1], sums2[S_LN2W:S_LN2W + 1],
                  sums2[S_LN2B:S_LN2B + 1], d_ret_norm, _unpad_heads(d_bg_p), d_gla_norm, d_gate_w_full,
                  sums2[S_LOSS:S_LOSS + 1]])
    g2 = _gather_rows(small, "gather_small")
    tot = _sum_devices(g2)
    loss = 0.5 / D_MODEL * jnp.sum(tot[136:144])
    grad_b_ada = tot[0:48].reshape(1, 6 * D_MODEL)
    grad_ln1_w, grad_ln1_b = tot[48:56].reshape(1, D_MODEL), tot[56:64].reshape(1, D_MODEL)
    grad_ln2_w, grad_ln2_b = tot[64:72].reshape(1, D_MODEL), tot[72:80].reshape(1, D_MODEL)
    grad_ret_norm = tot[80:84].reshape(1, 512)
    grad_gate_b = tot[88:90].reshape(1, 256)
    grad_gla_norm = tot[96:100].reshape(1, 512)
    grad_gate_w = lax.dynamic_slice(tot[104:136].reshape(GATE_RANK, 256), (0, chip * gate_cols),
                                    (GATE_RANK, gate_cols))

    small_w = flat([b_ada, ln1_w, ln1_b, ln2_w, ln2_b, ret_norm_w, gla_gate_b, gla_norm_w, gla_gate_w[0]])
    small_g = flat([grad_b_ada, grad_ln1_w, grad_ln1_b, grad_ln2_w, grad_ln2_b, grad_ret_norm, grad_gate_b,
                    grad_gla_norm, grad_gate_w])
    small_m = flat([m_b_ada, m_ln1_w, m_ln1_b, m_ln2_w, m_ln2_b, m_ret_norm_w, m_gla_gate_b, m_gla_norm_w,
                    m_gla_gate_w[0]])
    small_v = flat([v_b_ada, v_ln1_w, v_ln1_b, v_ln2_w, v_ln2_b, v_ret_norm_w, v_gla_gate_b, v_gla_norm_w,
                    v_gla_gate_w[0]])
    small_out = _adam_small(small_w, small_g, small_m, small_v)

    def unflat(t):
        pieces, row = [], 0
        for shape in [(1, 6 * D_MODEL)] + [(1, D_MODEL)] * 4 + [(1, 512), (1, 256), (1, 512), (1, GATE_RANK, gate_cols)]:
            n = int(np.prod(shape)) // 128
            pieces.append(t[row:row + n].reshape(shape))
            row += -(-n // 8) * 8
        return pieces

    sm_delta, sm_m, sm_v = [unflat(t) for t in small_out]

    dmod_all = g2[:, 0:48].reshape(N_DEV, 6 * D_MODEL)
    dmod_blk = lax.dynamic_slice(dmod_all, (0, chip * ada_cols), (N_DEV, ada_cols))
    ada_out = _ada_bwd_adam(jnp.transpose(c_all), dmod_blk, w_ada[0], m_w_ada[0], v_w_ada[0])
    ada_g, ada_delta, ada_m, ada_v = [t[None] for t in ada_out]

    received = [r_in, r_out, r_ff1, r_ff2]
    names = ["w_in", "w_out", "w_ff1", "w_ff2"]
    partial = [_sum_chips(r, "sum_" + n) for r, n in zip(received, names)]
    swapped = _sibling_swap(partial, "swap_partials")
    big = {}
    for n, w, mine, theirs, m, v in zip(names, [w_in, w_out, w_ff1, w_ff2], partial, swapped,
                                        [m_w_in, m_w_out, m_w_ff1, m_w_ff2], [v_w_in, v_w_out, v_w_ff1, v_w_ff2]):
        if n == "w_in":
            out = _adam_pair(jnp.transpose(w[0]), mine, theirs, jnp.transpose(m[0]), jnp.transpose(v[0]), "adam_" + n)
            big[n] = [jnp.transpose(t)[None] for t in out]
        else:
            big[n] = [t[None] for t in _adam_pair(w[0], mine, theirs, m[0], v[0], "adam_" + n)]

    def assemble(ada, smalls, k):
        b_ada_o, ln1w_o, ln1b_o, ln2w_o, ln2b_o, ret_o, gb_o, gln_o, gw_o = smalls
        return [ada, b_ada_o, big["w_in"][k], ret_o, gw_o, gb_o, gln_o, big["w_out"][k], ln1w_o, ln1b_o,
                big["w_ff1"][k], big["w_ff2"][k], ln2w_o, ln2b_o]

    small_grads = [grad_b_ada, grad_ln1_w, grad_ln1_b, grad_ln2_w, grad_ln2_b, grad_ret_norm, grad_gate_b,
                   grad_gla_norm, grad_gate_w[None]]
    grads = assemble(ada_g, small_grads, 0)
    deltas = assemble(ada_delta, sm_delta, 1)
    new_m = assemble(ada_m, sm_m, 2)
    new_v = assemble(ada_v, sm_v, 3)
    return (loss, grad_x[None], *grads, *deltas, *new_m, *new_v)
```

```python
import functools

import numpy as np
import jax
import jax.numpy as jnp
from jax import lax
from jax.experimental import pallas as pl
from jax.experimental.pallas import tpu as pltpu

D_MODEL = 1024
D_FF = 4096
CHUNK = 64
N_HEADS = 4
HEAD_W = 128
GLA_DK = 64
GATE_RANK = 16
GATE_TAU = 16.0
LN_EPS = 1e-5
ALPHA = 2.0 ** 0.25
ROPE_BASE = 10000.0
RET_SCALE = float(HEAD_W) ** -0.5
GLA_SCALE = float(GLA_DK) ** -0.5

ADAM_LR = 0.001
ADAM_B1 = 0.9
ADAM_B2 = 0.999
ADAM_EPS = 1e-08
ADAM_WD = 0.01
ADAM_STEP = 10

OFF_RQ, OFF_RK, OFF_RV, OFF_RG = 0, 512, 1024, 1536
OFF_GQ, OFF_GK, OFF_GV, OFF_GG, OFF_LR = 2048, 2560, 3072, 3584, 4096
N_PROJ = 4224
N_PROJ_SRC = 3600

N_DEV = 8
N_CHIP = 4
MESH = pl.DeviceIdType.MESH
MXU_DTYPE = jnp.bfloat16
WIRE_DTYPE = jnp.bfloat16
VMEM_LIMIT = 60 * 1024 * 1024
TOKEN_TILE = 256
INPROJ_TOKEN_TILE = 512
GRAD_TOKEN_TILE = 2048
ELEMENTWISE_COLS = 256
HIGHEST = lax.Precision.HIGHEST


def _mm(a, b):
    return jnp.dot(a.astype(MXU_DTYPE), b.astype(MXU_DTYPE), preferred_element_type=jnp.float32)


def _mm_nt(a, b):
    return lax.dot_general(a.astype(MXU_DTYPE), b.astype(MXU_DTYPE), (((1,), (1,)), ((), ())),
                           preferred_element_type=jnp.float32)


def _mm_tn(a, b):
    return lax.dot_general(a.astype(MXU_DTYPE), b.astype(MXU_DTYPE), (((0,), (0,)), ((), ())),
                           preferred_element_type=jnp.float32)


def _mm32(a, b):
    return jnp.dot(a, b, precision=HIGHEST, preferred_element_type=jnp.float32)


def _running_sum(mask, a):
    m = mask.astype(jnp.bfloat16)
    hi = a.astype(jnp.bfloat16)
    rest = a - hi.astype(jnp.float32)
    mid = rest.astype(jnp.bfloat16)
    lo = (rest - mid.astype(jnp.float32)).astype(jnp.bfloat16)
    dot = lambda t: jnp.dot(m, t, preferred_element_type=jnp.float32)
    return dot(hi) + dot(mid) + dot(lo)


def _rowmean(a):
    return jnp.mean(a, axis=-1, keepdims=True)


def _colsum(a):
    return jnp.sum(a, axis=0, keepdims=True)


def _ln(z):
    zc = z - _rowmean(z)
    rstd = lax.rsqrt(_rowmean(zc * zc) + LN_EPS)
    return zc * rstd, rstd


def _ln_bwd(dzh, zh, rstd):
    return rstd * (dzh - _rowmean(dzh) - zh * _rowmean(dzh * zh))


def _sigmoid(a):
    return 1.0 / (1.0 + jnp.exp(-a))


def _log_sigmoid(a):
    return jnp.minimum(a, 0.0) - jnp.log(1.0 + jnp.exp(-jnp.abs(a)))


def _swap_halves(a):
    return pltpu.roll(a, HEAD_W // 2, 1)


def _tri_masks():
    row = lax.broadcasted_iota(jnp.int32, (CHUNK, CHUNK), 0)
    col = lax.broadcasted_iota(jnp.int32, (CHUNK, CHUNK), 1)
    return row, col


def _const_spec(shape):
    zeros = (0,) * len(shape)
    return pl.BlockSpec(shape, lambda *_: zeros, pipeline_mode=pl.Buffered(1))


def _params(semantics):
    return pltpu.CompilerParams(dimension_semantics=semantics, vmem_limit_bytes=VMEM_LIMIT)


def _decay_tables():
    log_gamma = np.log(1.0 - 2.0 ** (-5.0 - np.arange(N_HEADS, dtype=np.float64)))
    idx = np.arange(CHUNK, dtype=np.float64)
    dist = np.abs(idx[:, None] - idx[None, :])
    intra = np.exp(log_gamma[:, None, None] * dist)
    kdec = np.exp(log_gamma[None, :] * (CHUNK - 1.0 - idx)[:, None])
    qdec = np.exp(log_gamma[None, :] * (idx + 1.0)[:, None])
    chunk_decay = np.exp(log_gamma * CHUNK)
    lanes = lambda t: np.repeat(t, HEAD_W, axis=1).astype(np.float32)
    return (jnp.asarray(intra.astype(np.float32)), jnp.asarray(lanes(qdec)), jnp.asarray(lanes(kdec)),
            [float(np.float32(v)) for v in chunk_decay])


def _rotary_tables(seq):
    half = HEAD_W // 2
    inv = 1.0 / (ROPE_BASE ** jnp.linspace(0.0, 1.0, half, dtype=jnp.float32))
    both = lambda t: jnp.concatenate([t, t], axis=-1)
    ang_a = jnp.arange(0, seq, CHUNK, dtype=jnp.float32)[:, None] * inv[None, :]
    rot_a = jnp.stack([both(jnp.cos(ang_a)), both(jnp.sin(ang_a))], axis=1)
    rot_a = jnp.pad(rot_a, ((0, 0), (0, 6), (0, 0)))
    ang_b = jnp.arange(CHUNK, dtype=jnp.float32)[:, None] * inv[None, :]
    cos_b, sin_b = both(jnp.cos(ang_b)), both(jnp.sin(ang_b))
    sign = jnp.concatenate([-jnp.ones((half,), jnp.float32), jnp.ones((half,), jnp.float32)])
    return rot_a, jnp.stack([cos_b, sin_b, cos_b * sign, sin_b * sign])


def _rotary_chunk(ra_ref, rb_ref):
    cos_a, sin_a = ra_ref[0, 0:1, :], ra_ref[0, 1:2, :]
    return cos_a * rb_ref[0] - sin_a * rb_ref[1], sin_a * rb_ref[2] + cos_a * rb_ref[3]


def _mesh_pos():
    return lax.axis_index("x"), lax.axis_index("y"), lax.axis_index("c")


def _flip(v, bit):
    return 1 - v if bit else v


def _gather_rows(v, name):
    rows = v.shape[0]

    def body(v_ref, out_ref, send_sems, recv_sems):
        x, y, c = _mesh_pos()
        me = 4 * x + 2 * y + c
        out_ref[me] = v_ref[...]
        sends, recvs = [], []
        for k in range(1, N_DEV):
            px, py, pc = _flip(x, (k >> 2) & 1), _flip(y, (k >> 1) & 1), _flip(c, k & 1)
            peer = 4 * px + 2 * py + pc
            sends.append(pltpu.make_async_remote_copy(
                src_ref=v_ref, dst_ref=out_ref.at[me], send_sem=send_sems.at[k - 1], recv_sem=recv_sems.at[k - 1],
                device_id=(px, py, pc), device_id_type=MESH))
            recvs.append(pltpu.make_async_remote_copy(
                src_ref=v_ref, dst_ref=out_ref.at[peer], send_sem=send_sems.at[k - 1], recv_sem=recv_sems.at[k - 1],
                device_id=(px, py, pc), device_id_type=MESH))
        for cp in sends:
            cp.start()
        for cp in recvs:
            cp.wait_recv()
        for cp in sends:
            cp.wait_send()

    return pl.pallas_call(
        body, name=name,
        out_shape=jax.ShapeDtypeStruct((N_DEV, rows, 128), jnp.float32),
        in_specs=[pl.BlockSpec(memory_space=pltpu.VMEM)],
        out_specs=pl.BlockSpec(memory_space=pltpu.VMEM),
        scratch_shapes=[pltpu.SemaphoreType.DMA((N_DEV - 1,)), pltpu.SemaphoreType.DMA((N_DEV - 1,))],
    )(v)


def _chip_gather(arrays, name):
    n = len(arrays)

    def body(*refs):
        gather = _ChipGather(refs[:n], refs[n:2 * n], refs[2 * n:])
        gather.start()
        gather.forward()
        gather.finish()

    return pl.pallas_call(
        body, name=name,
        out_shape=_exchange_out_shapes(arrays, True),
        in_specs=[pl.BlockSpec(memory_space=pl.ANY)] * n,
        out_specs=tuple(pl.BlockSpec(memory_space=pl.ANY) for _ in arrays),
        scratch_shapes=_gather_sems(n),
    )(*arrays)


def _exchange_out_shapes(arrays, gather):
    return tuple(jax.ShapeDtypeStruct((N_CHIP,) + a.shape if gather else a.shape, a.dtype) for a in arrays)


def _scatter_sems(n):
    n_sem = n * (N_CHIP - 1)
    return [pltpu.SemaphoreType.DMA((n_sem,)), pltpu.SemaphoreType.DMA((n_sem,)), pltpu.SemaphoreType.DMA((n,))]


def _gather_sems(n):
    n_sem = n * (N_CHIP - 1)
    return [pltpu.SemaphoreType.DMA((n_sem,))] * 4 + [pltpu.SemaphoreType.DMA((n,))]


def _peer_chips(x, y):
    out = []
    for k in range(1, N_CHIP):
        px, py = _flip(x, (k >> 1) & 1), _flip(y, k & 1)
        out.append((px, py, 2 * px + py))
    return out


class _ChipScatter:
    def __init__(self, ins, outs, sems):
        send_sems, recv_sems, local_sems = sems
        x, y, c = _mesh_pos()
        chip = 2 * x + y
        self.local, self.sends, self.recvs = [], [], []
        for i in range(len(ins)):
            self.local.append(pltpu.make_async_copy(ins[i].at[chip], outs[i].at[chip], local_sems.at[i]))
            for k, (px, py, peer_chip) in enumerate(_peer_chips(x, y)):
                sem = i * (N_CHIP - 1) + k
                src = ins[i].at[peer_chip]
                self.sends.append(pltpu.make_async_remote_copy(
                    src_ref=src, dst_ref=outs[i].at[chip], send_sem=send_sems.at[sem], recv_sem=recv_sems.at[sem],
                    device_id=(px, py, c), device_id_type=MESH))
                self.recvs.append(pltpu.make_async_remote_copy(
                    src_ref=src, dst_ref=outs[i].at[peer_chip], send_sem=send_sems.at[sem], recv_sem=recv_sems.at[sem],
                    device_id=(px, py, c), device_id_type=MESH))

    def start(self):
        for cp in self.local + self.sends:
            cp.start()

    def wait(self):
        for cp in self.recvs:
            cp.wait_recv()
        for cp in self.sends:
            cp.wait_send()
        for cp in self.local:
            cp.wait()


class _ChipGather:
    def __init__(self, ins, outs, sems):
        ici_send, ici_recv, d2d_send, d2d_recv, local_sems = sems
        x, y, c = _mesh_pos()
        chip = 2 * x + y
        self.local, self.ici_sends, self.ici_recvs, self.d2d_sends, self.d2d_recvs = [], [], [], [], []
        for i in range(len(ins)):
            half = ins[i].shape[-1] // 2
            assert half % 128 == 0
            lead = (slice(None),) * (len(ins[i].shape) - 1)
            mine = lead + (pl.ds(pl.multiple_of(c * half, 128), half),)
            theirs = lead + (pl.ds(pl.multiple_of((1 - c) * half, 128), half),)
            self.local.append(pltpu.make_async_copy(ins[i], outs[i].at[chip], local_sems.at[i]))
            for k, (px, py, peer_chip) in enumerate(_peer_chips(x, y)):
                sem = i * (N_CHIP - 1) + k
                self.ici_sends.append(pltpu.make_async_remote_copy(
                    src_ref=ins[i].at[mine], dst_ref=outs[i].at[chip].at[mine],
                    send_sem=ici_send.at[sem], recv_sem=ici_recv.at[sem], device_id=(px, py, c), device_id_type=MESH))
                landed = outs[i].at[peer_chip].at[mine]
                self.ici_recvs.append(pltpu.make_async_remote_copy(
                    src_ref=ins[i].at[mine], dst_ref=landed,
                    send_sem=ici_send.at[sem], recv_sem=ici_recv.at[sem], device_id=(px, py, c), device_id_type=MESH))
                self.d2d_sends.append(pltpu.make_async_remote_copy(
                    src_ref=landed, dst_ref=landed,
                    send_sem=d2d_send.at[sem], recv_sem=d2d_recv.at[sem], device_id=(x, y, 1 - c), device_id_type=MESH))
                self.d2d_recvs.append(pltpu.make_async_remote_copy(
                    src_ref=landed, dst_ref=outs[i].at[peer_chip].at[theirs],
                    send_sem=d2d_send.at[sem], recv_sem=d2d_recv.at[sem], device_id=(x, y, 1 - c), device_id_type=MESH))

    def start(self):
        for cp in self.local + self.ici_sends:
            cp.start()

    def forward(self):
        for landed, onward in zip(self.ici_recvs, self.d2d_sends):
            landed.wait_recv()
            onward.start()

    def finish(self):
        for cp in self.d2d_recvs:
            cp.wait_recv()
        for cp in self.d2d_sends + self.ici_sends:
            cp.wait_send()
        for cp in self.local:
            cp.wait()


def _sibling_swap(arrays, name):
    n = len(arrays)

    def body(*refs):
        ins, outs = refs[:n], refs[n:2 * n]
        send_sems, recv_sems = refs[2 * n:]
        x, y, c = _mesh_pos()
        copies = [pltpu.make_async_remote_copy(
            src_ref=ins[i], dst_ref=outs[i], send_sem=send_sems.at[i], recv_sem=recv_sems.at[i],
            device_id=(x, y, 1 - c), device_id_type=MESH) for i in range(n)]
        for cp in copies:
            cp.start()
        for cp in copies:
            cp.wait_recv()
        for cp in copies:
            cp.wait_send()

    return pl.pallas_call(
        body, name=name,
        out_shape=tuple(jax.ShapeDtypeStruct(a.shape, a.dtype) for a in arrays),
        in_specs=[pl.BlockSpec(memory_space=pl.ANY)] * n,
        out_specs=tuple(pl.BlockSpec(memory_space=pl.ANY) for _ in arrays),
        scratch_shapes=[pltpu.SemaphoreType.DMA((n,)), pltpu.SemaphoreType.DMA((n,))],
    )(*arrays)


def _ada_fwd(c_all, w_ada_blk, b_blk):
    cols = w_ada_blk.shape[1]

    def body(c_ref, w_ref, b_ref, out_ref):
        cv = c_ref[...]
        out_ref[...] = _mm32(cv * _sigmoid(cv), w_ref[...]) + b_ref[...]

    return pl.pallas_call(
        body, name="ada_fwd",
        out_shape=jax.ShapeDtypeStruct((N_DEV, cols), jnp.float32),
        compiler_params=pltpu.CompilerParams(vmem_limit_bytes=VMEM_LIMIT),
    )(c_all, w_ada_blk, b_blk)


def _adam(w, g, m, v):
    m2 = ADAM_B1 * m + (1.0 - ADAM_B1) * g
    v2 = ADAM_B2 * v + (1.0 - ADAM_B2) * (g * g)
    m_hat = m2 / (1.0 - ADAM_B1 ** ADAM_STEP)
    v_hat = v2 / (1.0 - ADAM_B2 ** ADAM_STEP)
    delta = -ADAM_LR * (m_hat / (jnp.sqrt(v_hat) + ADAM_EPS) + ADAM_WD * w)
    return delta, m2, v2


def _ada_bwd_adam(c_t, dmod_blk, w, m, v):
    rows, cols = w.shape
    tile = 512
    assert cols % tile == 0

    def body(c_ref, d_ref, w_ref, m_ref, v_ref, g_ref, dl_ref, m2_ref, v2_ref):
        sc = c_ref[...]
        sc = sc * _sigmoid(sc)
        dm = d_ref[...]
        g = sc[:, 0:1] * dm[0:1, :]
        for b in range(1, N_DEV):
            g = g + sc[:, b:b + 1] * dm[b:b + 1, :]
        delta, m2, v2 = _adam(w_ref[...], g, m_ref[...], v_ref[...])
        g_ref[...] = g
        dl_ref[...] = delta
        m2_ref[...] = m2
        v2_ref[...] = v2

    blk = pl.BlockSpec((rows, tile), lambda j: (0, j))
    out = jax.ShapeDtypeStruct((rows, cols), jnp.float32)
    return pl.pallas_call(
        body, name="ada_bwd_adam", grid=(cols // tile,),
        out_shape=(out, out, out, out),
        in_specs=[pl.BlockSpec((rows, N_DEV), lambda j: (0, 0)), pl.BlockSpec((N_DEV, tile), lambda j: (0, j)),
                  blk, blk, blk],
        out_specs=(blk, blk, blk, blk),
        compiler_params=_params(("arbitrary",)),
    )(c_t, dmod_blk, w, m, v)


def _inproj_fwd(x2, vecs, w_in_p, tm):
    seq = x2.shape[0]

    def body(x_ref, vec_ref, w_ref, p_ref, u_ref):
        xh, _ = _ln(x_ref[...])
        u = (xh * (1.0 + vec_ref[1:2, :]) + vec_ref[0:1, :]).astype(MXU_DTYPE)
        u_ref[...] = u
        p_ref[...] = _mm(u, w_ref[...])

    return pl.pallas_call(
        body, name="inproj_fwd", grid=(seq // tm,),
        out_shape=(jax.ShapeDtypeStruct((seq, N_PROJ), jnp.float32), jax.ShapeDtypeStruct((seq, D_MODEL), MXU_DTYPE)),
        in_specs=[pl.BlockSpec((tm, D_MODEL), lambda i: (i, 0)), _const_spec(vecs.shape), _const_spec(w_in_p.shape)],
        out_specs=(pl.BlockSpec((tm, N_PROJ), lambda i: (i, 0)), pl.BlockSpec((tm, D_MODEL), lambda i: (i, 0))),
        compiler_params=_params(("arbitrary",)),
    )(x2, vecs, w_in_p)


def _inproj_bwd(dproj, x2, dxa, vecs, w_in_pt, tm, riders):
    seq = x2.shape[0]
    n_tiles = seq // tm
    n_ride = len(riders)

    def body(*refs):
        dp_ref, x_ref, dxa_ref, vec_ref, w_ref = refs[:5]
        ride_in, refs = refs[5:5 + n_ride], refs[5 + n_ride:]
        gx_ref, sums_ref = refs[:2]
        ride_out, sems = refs[2:2 + n_ride], refs[2 + n_ride:]
        exchange = _ChipScatter(ride_in, ride_out, sems)

        @pl.when(pl.program_id(0) == 0)
        def _():
            exchange.start()
            sums_ref[...] = jnp.zeros_like(sums_ref)

        du = _mm(dp_ref[...], w_ref[...])
        xh, rstd = _ln(x_ref[...])
        sums_ref[0:1, :] += _colsum(du)
        sums_ref[1:2, :] += _colsum(du * xh)
        gx_ref[...] = dxa_ref[...] + _ln_bwd(du * (1.0 + vec_ref[1:2, :]), xh, rstd)

        @pl.when(pl.program_id(0) == n_tiles - 1)
        def _():
            exchange.wait()

    tile = pl.BlockSpec((tm, D_MODEL), lambda i: (i, 0))
    hbm = pl.BlockSpec(memory_space=pl.ANY)
    return pl.pallas_call(
        body, name="inproj_bwd", grid=(n_tiles,),
        out_shape=(jax.ShapeDtypeStruct((seq, D_MODEL), jnp.float32), jax.ShapeDtypeStruct((8, D_MODEL), jnp.float32))
        + _exchange_out_shapes(riders, False),
        in_specs=[pl.BlockSpec((tm, N_PROJ), lambda i: (i, 0)), tile, tile, _const_spec(vecs.shape),
                  _const_spec(w_in_pt.shape)] + [hbm] * n_ride,
        out_specs=(tile, pl.BlockSpec((8, D_MODEL), lambda i: (0, 0))) + (hbm,) * n_ride,
        scratch_shapes=_scatter_sems(n_ride),
        compiler_params=_params(("arbitrary",)),
    )(dproj, x2, dxa, vecs, w_in_pt, *riders)


def _head(h):
    return slice(h * HEAD_W, (h + 1) * HEAD_W)


def _cols(ref, off, h):
    return ref[:, off + h * HEAD_W:off + (h + 1) * HEAD_W]


HEADS = range(N_HEADS)


def _mixer_chunk_forward(p_ref, cc, ss, dm_ref, qdec_ref, kdec_ref, wg_ref, bg_ref, ret_state, gla_state_t):
    row, col = _tri_masks()
    lower = row >= col
    f = {}
    f["glr"] = p_ref[:, OFF_LR:OFF_LR + HEAD_W]
    f["logit"] = _mm(f["glr"], wg_ref[...]) + bg_ref[...]
    rq = [_cols(p_ref, OFF_RQ, h) for h in HEADS]
    rk = [_cols(p_ref, OFF_RK, h) for h in HEADS]
    f["rv"] = [_cols(p_ref, OFF_RV, h) for h in HEADS]
    f["qr"] = [(rq[h] * cc + _swap_halves(rq[h]) * ss) * RET_SCALE for h in HEADS]
    f["kr"] = [rk[h] * cc + _swap_halves(rk[h]) * ss for h in HEADS]
    s_raw = [_mm_nt(f["qr"][h], f["kr"][h]) for h in HEADS]
    la = _log_sigmoid(f["logit"]) * (1.0 / GATE_TAU)
    b = _running_sum(lower, la)
    f["qd"] = [f["qr"][h] * qdec_ref[:, _head(h)] for h in HEADS]
    f["kd"] = [f["kr"][h] * kdec_ref[:, _head(h)] for h in HEADS]
    f["scores"] = [s_raw[h] * dm_ref[h] for h in HEADS]
    f["o_ret"] = [_mm(f["scores"][h], f["rv"][h]) + _mm(f["qd"][h], ret_state[h]) for h in HEADS]
    b_last = b[CHUNK - 1:CHUNK, :]
    b_mid = b[CHUNK // 2 - 1:CHUNK // 2, :]
    f["e"], f["ei"] = jnp.exp(b - b_mid), jnp.exp(b_mid - b)
    f["eb"], f["ek"], f["ebl"] = jnp.exp(b), jnp.exp(b_last - b), jnp.exp(b_last)
    gq = [_cols(p_ref, OFF_GQ, h) * GLA_SCALE for h in HEADS]
    gk = [_cols(p_ref, OFF_GK, h) for h in HEADS]
    f["gv"] = [_cols(p_ref, OFF_GV, h) for h in HEADS]
    f["q_e"] = [gq[h] * f["e"][:, _head(h)] for h in HEADS]
    f["q_i"] = [gq[h] * f["ei"][:, _head(h)] for h in HEADS]
    f["k_e"] = [gk[h] * f["e"][:, _head(h)] for h in HEADS]
    f["k_i"] = [gk[h] * f["ei"][:, _head(h)] for h in HEADS]
    low = [_mm_nt(f["q_e"][h], f["k_i"][h]) for h in HEADS]
    up = [_mm_nt(f["q_i"][h], f["k_e"][h]) for h in HEADS]
    f["att"] = [jnp.where(lower, low[h], up[h]) for h in HEADS]
    f["qb"] = [gq[h] * f["eb"][:, _head(h)] for h in HEADS]
    f["kb"] = [gk[h] * f["ek"][:, _head(h)] for h in HEADS]
    f["o_gla"] = [_mm(f["att"][h], f["gv"][h]) + _mm_nt(f["qb"][h], gla_state_t[h]) for h in HEADS]
    return f


def _mixer_fwd(proj, tables, wg_p, bg_p, ret_norm_w, gla_norm_w, riders):
    seq = proj.shape[0]
    n_chunks = seq // CHUNK
    n_ride = len(riders)
    rot_a, rot_b, dm_t, qdec_t, kdec_t, chunk_decay = tables

    def body(*refs):
        p_ref, ra_ref, rb_ref, dm_ref, qdec_ref, kdec_ref, wg_ref, bg_ref, wr_ref, wl_ref = refs[:10]
        ride_in, refs = refs[10:10 + n_ride], refs[10 + n_ride:]
        mix_ref, rsave_ref, ssave_ref = refs[:3]
        ride_out, refs = refs[3:3 + n_ride], refs[3 + n_ride:]
        r_sc, s_sc = refs[:2]
        gather = _ChipGather(ride_in, ride_out, refs[2:])

        @pl.when(pl.program_id(0) == 0)
        def _():
            gather.start()
            r_sc[...] = jnp.zeros_like(r_sc)
            s_sc[...] = jnp.zeros_like(s_sc)

        ret_state = [r_sc[h] for h in HEADS]
        gla_state_t = [s_sc[h] for h in HEADS]
        for h in HEADS:
            rsave_ref[0, h] = ret_state[h]
            ssave_ref[0, h] = gla_state_t[h]
        cc, ss = _rotary_chunk(ra_ref, rb_ref)
        f = _mixer_chunk_forward(p_ref, cc, ss, dm_ref, qdec_ref, kdec_ref, wg_ref, bg_ref, ret_state, gla_state_t)
        for h in HEADS:
            r_sc[h] = chunk_decay[h] * ret_state[h] + _mm_tn(f["kd"][h], f["rv"][h])
        for h in HEADS:
            s_sc[h] = gla_state_t[h] * f["ebl"][:, _head(h)] + _mm_tn(f["gv"][h], f["kb"][h])
        for h in HEADS:
            on, _ = _ln(f["o_ret"][h])
            g = _cols(p_ref, OFF_RG, h)
            mix_ref[:, _head(h)] = (on * wr_ref[:, _head(h)] * (g * _sigmoid(g))).astype(mix_ref.dtype)
        for h in HEADS:
            o = f["o_gla"][h]
            on = o * lax.rsqrt(_rowmean(o * o) + LN_EPS)
            g = _cols(p_ref, OFF_GG, h)
            mix_ref[:, _head(N_HEADS + h)] = (on * wl_ref[:, _head(h)] * (g * _sigmoid(g))).astype(mix_ref.dtype)

        @pl.when(pl.program_id(0) == (3 * n_chunks) // 4)
        def _():
            gather.forward()

        @pl.when(pl.program_id(0) == n_chunks - 1)
        def _():
            gather.finish()

    state_shape = (n_chunks, N_HEADS, HEAD_W, HEAD_W)
    state_blk = pl.BlockSpec((1, N_HEADS, HEAD_W, HEAD_W), lambda i: (i, 0, 0, 0))
    rot_blk = pl.BlockSpec((1, 8, HEAD_W), lambda i: (i, 0, 0))
    hbm = pl.BlockSpec(memory_space=pl.ANY)
    return pl.pallas_call(
        body, name="mixer_fwd", grid=(n_chunks,),
        out_shape=(jax.ShapeDtypeStruct((seq, D_MODEL), MXU_DTYPE),
                   jax.ShapeDtypeStruct(state_shape, jnp.float32), jax.ShapeDtypeStruct(state_shape, jnp.float32))
        + _exchange_out_shapes(riders, True),
        in_specs=[pl.BlockSpec((CHUNK, N_PROJ), lambda i: (i, 0)), rot_blk, _const_spec(rot_b.shape),
                  _const_spec(dm_t.shape), _const_spec(qdec_t.shape), _const_spec(kdec_t.shape),
                  _const_spec(wg_p.shape), _const_spec(bg_p.shape), _const_spec(ret_norm_w.shape),
                  _const_spec(gla_norm_w.shape)] + [hbm] * n_ride,
        out_specs=(pl.BlockSpec((CHUNK, D_MODEL), lambda i: (i, 0)), state_blk, state_blk) + (hbm,) * n_ride,
        scratch_shapes=[pltpu.VMEM((N_HEADS, HEAD_W, HEAD_W), jnp.float32),
                        pltpu.VMEM((N_HEADS, HEAD_W, HEAD_W), jnp.float32)] + _gather_sems(n_ride),
        compiler_params=_params(("arbitrary",)),
    )(proj, rot_a, rot_b, dm_t, qdec_t, kdec_t, wg_p, bg_p, ret_norm_w, gla_norm_w, *riders)


def _mixer_bwd(proj, dmixed, rsave, ssave, tables, wg_p, bg_p, ret_norm_w, gla_norm_w, riders):
    seq = proj.shape[0]
    n_chunks = seq // CHUNK
    n_ride = len(riders)
    rot_a, rot_b, dm_t, qdec_t, kdec_t, chunk_decay = tables
    last = n_chunks - 1

    def body(*refs):
        (p_ref, dmx_ref, rsave_ref, ssave_ref, ra_ref, rb_ref, dm_ref, qdec_ref, kdec_ref, wg_ref, bg_ref,
         wr_ref, wl_ref) = refs[:13]
        ride_in, refs = refs[13:13 + n_ride], refs[13 + n_ride:]
        dp_ref, dwr_ref, dwl_ref, dwg_ref, dbg_ref = refs[:5]
        ride_out, refs = refs[5:5 + n_ride], refs[5 + n_ride:]
        dr_sc, ds_sc = refs[:2]
        exchange = _ChipScatter(ride_in, ride_out, refs[2:])

        @pl.when(pl.program_id(0) == 0)
        def _():
            exchange.start()
            dr_sc[...] = jnp.zeros_like(dr_sc)
            ds_sc[...] = jnp.zeros_like(ds_sc)
            dwr_ref[...] = jnp.zeros_like(dwr_ref)
            dwl_ref[...] = jnp.zeros_like(dwl_ref)
            dwg_ref[...] = jnp.zeros_like(dwg_ref)
            dbg_ref[...] = jnp.zeros_like(dbg_ref)

        def put(off, h, val):
            dp_ref[:, off + h * HEAD_W:off + (h + 1) * HEAD_W] = val.astype(dp_ref.dtype)

        cc, ss = _rotary_chunk(ra_ref, rb_ref)
        row, col = _tri_masks()
        ret_state = [rsave_ref[0, h] for h in HEADS]
        gla_state_t = [ssave_ref[0, h] for h in HEADS]
        d_ret_new = [dr_sc[h] for h in HEADS]
        d_gla_new = [ds_sc[h] for h in HEADS]
        f = _mixer_chunk_forward(p_ref, cc, ss, dm_ref, qdec_ref, kdec_ref, wg_ref, bg_ref, ret_state, gla_state_t)

        do_ret, do_gla = [], []
        for h in HEADS:
            on, rstd = _ln(f["o_ret"][h])
            g = _cols(p_ref, OFF_RG, h)
            sg = _sigmoid(g)
            dy = dmx_ref[:, _head(h)].astype(jnp.float32)
            wr = wr_ref[:, _head(h)]
            dwr_ref[:, _head(h)] += _colsum(dy * on * (g * sg))
            put(OFF_RG, h, dy * on * wr * (sg * (1.0 + g * (1.0 - sg))))
            do_ret.append(_ln_bwd(dy * wr * (g * sg), on, rstd))
        for h in HEADS:
            o = f["o_gla"][h]
            rstd = lax.rsqrt(_rowmean(o * o) + LN_EPS)
            on = o * rstd
            g = _cols(p_ref, OFF_GG, h)
            sg = _sigmoid(g)
            dy = dmx_ref[:, _head(N_HEADS + h)].astype(jnp.float32)
            wl = wl_ref[:, _head(h)]
            dwl_ref[:, _head(h)] += _colsum(dy * on * (g * sg))
            put(OFF_GG, h, dy * on * wl * (sg * (1.0 + g * (1.0 - sg))))
            don = dy * wl * (g * sg)
            do_gla.append(rstd * (don - on * _rowmean(don * on)))

        ds_raw = [_mm_nt(do_ret[h], f["rv"][h]) * dm_ref[h] for h in HEADS]
        d_att = [_mm_nt(do_gla[h], f["gv"][h]) for h in HEADS]
        dq_state = [_mm_nt(do_ret[h], ret_state[h]) for h in HEADS]
        dk_state = [_mm_nt(f["rv"][h], d_ret_new[h]) for h in HEADS]
        dqb = [_mm(do_gla[h], gla_state_t[h]) for h in HEADS]
        dkb = [_mm(f["gv"][h], d_gla_new[h]) for h in HEADS]
        for h in HEADS:
            put(OFF_RV, h, _mm_tn(f["scores"][h], do_ret[h]) + _mm(f["kd"][h], d_ret_new[h]))
        for h in HEADS:
            put(OFF_GV, h, _mm_tn(f["att"][h], do_gla[h]) + _mm_nt(f["kb"][h], d_gla_new[h]))
        for h in HEADS:
            dr_sc[h] = chunk_decay[h] * d_ret_new[h] + _mm_tn(f["qd"][h], do_ret[h])
        for h in HEADS:
            ds_sc[h] = d_gla_new[h] * f["ebl"][:, _head(h)] + _mm_tn(do_gla[h], f["qb"][h])

        dqr = [_mm(ds_raw[h], f["kr"][h]) + dq_state[h] * qdec_ref[:, _head(h)] for h in HEADS]
        dkr = [_mm_tn(ds_raw[h], f["qr"][h]) + dk_state[h] * kdec_ref[:, _head(h)] for h in HEADS]
        d_low = [jnp.where(row >= col, d_att[h], 0.0) for h in HEADS]
        d_up = [jnp.where(row < col, d_att[h], 0.0) for h in HEADS]
        dq_e = [_mm(d_low[h], f["k_i"][h]) for h in HEADS]
        dk_i = [_mm_tn(d_low[h], f["q_e"][h]) for h in HEADS]
        dq_i = [_mm(d_up[h], f["k_e"][h]) for h in HEADS]
        dk_e = [_mm_tn(d_up[h], f["q_i"][h]) for h in HEADS]
        for h in HEADS:
            put(OFF_RQ, h, (dqr[h] * cc + _swap_halves(dqr[h] * ss)) * RET_SCALE)
            put(OFF_RK, h, dkr[h] * cc + _swap_halves(dkr[h] * ss))
        row_id = lax.broadcasted_iota(jnp.int32, (CHUNK, HEAD_W), 0)
        db_heads = []
        for h in HEADS:
            hs = _head(h)
            e, ei, eb, ek, ebl = f["e"][:, hs], f["ei"][:, hs], f["eb"][:, hs], f["ek"][:, hs], f["ebl"][:, hs]
            put(OFF_GQ, h, (dq_e[h] * e + dq_i[h] * ei + dqb[h] * eb) * GLA_SCALE)
            put(OFF_GK, h, dk_e[h] * e + dk_i[h] * ei + dkb[h] * ek)
            db = (dq_e[h] * f["q_e"][h] - dq_i[h] * f["q_i"][h] + dk_e[h] * f["k_e"][h] - dk_i[h] * f["k_i"][h]
                  + dqb[h] * f["qb"][h] - dkb[h] * f["kb"][h])
            db_last = _colsum(dkb[h] * f["kb"][h]) + ebl * _colsum(gla_state_t[h] * d_gla_new[h])
            db_heads.append(db + jnp.where(row_id == CHUNK - 1, db_last, 0.0))
        db = jnp.concatenate(db_heads, axis=1)
        d_la = _running_sum(col >= row, db)
        d_logit = d_la * (1.0 / GATE_TAU) * (1.0 - _sigmoid(f["logit"]))
        put(OFF_LR, 0, _mm_nt(d_logit, wg_ref[...]))
        dwg_ref[...] += _mm_tn(f["glr"], d_logit)
        dbg_ref[...] += _colsum(d_logit)

        @pl.when(pl.program_id(0) == last)
        def _():
            exchange.wait()

    state_blk = pl.BlockSpec((1, N_HEADS, HEAD_W, HEAD_W), lambda i: (last - i, 0, 0, 0))
    rot_blk = pl.BlockSpec((1, 8, HEAD_W), lambda i: (last - i, 0, 0))
    width = N_HEADS * HEAD_W
    vec_out = pl.BlockSpec((1, width), lambda i: (0, 0))
    hbm = pl.BlockSpec(memory_space=pl.ANY)
    return pl.pallas_call(
        body, name="mixer_bwd", grid=(n_chunks,),
        out_shape=(jax.ShapeDtypeStruct((seq, N_PROJ), MXU_DTYPE),
                   jax.ShapeDtypeStruct((1, width), jnp.float32), jax.ShapeDtypeStruct((1, width), jnp.float32),
                   jax.ShapeDtypeStruct((HEAD_W, width), jnp.float32), jax.ShapeDtypeStruct((1, width), jnp.float32))
        + _exchange_out_shapes(riders, False),
        in_specs=[pl.BlockSpec((CHUNK, N_PROJ), lambda i: (last - i, 0)),
                  pl.BlockSpec((CHUNK, D_MODEL), lambda i: (last - i, 0)), state_blk, state_blk, rot_blk,
                  _const_spec(rot_b.shape),
                  _const_spec(dm_t.shape), _const_spec(qdec_t.shape), _const_spec(kdec_t.shape),
                  _const_spec(wg_p.shape), _const_spec(bg_p.shape), _const_spec(ret_norm_w.shape),
                  _const_spec(gla_norm_w.shape)] + [hbm] * n_ride,
        out_specs=(pl.BlockSpec((CHUNK, N_PROJ), lambda i: (last - i, 0)), vec_out, vec_out,
                   pl.BlockSpec((HEAD_W, width), lambda i: (0, 0)), vec_out) + (hbm,) * n_ride,
        scratch_shapes=[pltpu.VMEM((N_HEADS, HEAD_W, HEAD_W), jnp.float32),
                        pltpu.VMEM((N_HEADS, HEAD_W, HEAD_W), jnp.float32)] + _scatter_sems(n_ride),
        compiler_params=_params(("arbitrary",)),
    )(proj, dmixed, rsave, ssave, rot_a, rot_b, dm_t, qdec_t, kdec_t, wg_p, bg_p, ret_norm_w, gla_norm_w, *riders)


V_GATE1, V_SCALE2, V_SHIFT2, V_GATE2, V_LN1W, V_LN1B, V_LN2W, V_LN2B = range(8)
S_GATE1, S_SCALE2, S_SHIFT2, S_GATE2, S_LN1W, S_LN1B, S_LN2W, S_LN2B, S_LOSS = range(9)


def _mlp_fwd_bwd(x2, mixed, target, vecs, w_out, w1_chunks, w2_chunks, tm):
    seq = x2.shape[0]
    n_fc, _, fc = w1_chunks.shape

    def body(x_ref, mx_ref, t_ref, vec_ref, wo_ref, w1_ref, w2_ref,
             dmx_ref, dxa_ref, a_ref, dh_ref, u2_ref, df_ref, dm_ref, sums_ref, relu_sc):
        @pl.when(pl.program_id(0) == 0)
        def _():
            sums_ref[...] = jnp.zeros_like(sums_ref)

        vec = lambda r: vec_ref[r:r + 1, :]

        def acc(r, val):
            sums_ref[r:r + 1, :] += _colsum(val)

        xx = x_ref[...]
        m = _mm(mx_ref[...], wo_ref[...])
        z1h, rstd1 = _ln(ALPHA * xx + vec(V_GATE1) * m)
        x1 = z1h * vec(V_LN1W) + vec(V_LN1B)
        x1h, rstd0 = _ln(x1)
        u2 = (x1h * (1.0 + vec(V_SCALE2)) + vec(V_SHIFT2)).astype(MXU_DTYPE)
        u2_ref[...] = u2
        f = jnp.zeros((tm, D_MODEL), jnp.float32)
        for j in range(n_fc):
            r = jnp.maximum(_mm(u2, w1_ref[j]), 0.0)
            relu_sc[:, j * fc:(j + 1) * fc] = r
            a = (r * r).astype(MXU_DTYPE)
            a_ref[:, j * fc:(j + 1) * fc] = a
            f = f + _mm(a, w2_ref[j])
        z2h, rstd2 = _ln(ALPHA * x1 + vec(V_GATE2) * f)
        err = z2h * vec(V_LN2W) + vec(V_LN2B) - t_ref[...]
        acc(S_LOSS, err * err)
        dy = err * (1.0 / D_MODEL)
        acc(S_LN2W, dy * z2h)
        acc(S_LN2B, dy)
        dz2 = _ln_bwd(dy * vec(V_LN2W), z2h, rstd2)
        acc(S_GATE2, dz2 * f)
        df = (vec(V_GATE2) * dz2).astype(MXU_DTYPE)
        df_ref[...] = df
        du2 = jnp.zeros((tm, D_MODEL), jnp.float32)
        for j in range(n_fc):
            dh = (_mm_nt(df, w2_ref[j]) * (2.0 * relu_sc[:, j * fc:(j + 1) * fc])).astype(MXU_DTYPE)
            dh_ref[:, j * fc:(j + 1) * fc] = dh
            du2 = du2 + _mm_nt(dh, w1_ref[j])
        acc(S_SCALE2, du2 * x1h)
        acc(S_SHIFT2, du2)
        dx1 = ALPHA * dz2 + _ln_bwd(du2 * (1.0 + vec(V_SCALE2)), x1h, rstd0)
        acc(S_LN1W, dx1 * z1h)
        acc(S_LN1B, dx1)
        dz1 = _ln_bwd(dx1 * vec(V_LN1W), z1h, rstd1)
        acc(S_GATE1, dz1 * m)
        dxa_ref[...] = ALPHA * dz1
        dm = (vec(V_GATE1) * dz1).astype(MXU_DTYPE)
        dm_ref[...] = dm
        dmx_ref[...] = _mm_nt(dm, wo_ref[...])

    tile = lambda width: pl.BlockSpec((tm, width), lambda i: (i, 0))
    f32 = lambda width: jax.ShapeDtypeStruct((seq, width), jnp.float32)
    b16 = lambda width: jax.ShapeDtypeStruct((seq, width), MXU_DTYPE)
    return pl.pallas_call(
        body, name="mlp_fwd_bwd", grid=(seq // tm,),
        out_shape=(f32(D_MODEL), f32(D_MODEL), b16(D_FF), b16(D_FF), b16(D_MODEL), b16(D_MODEL), b16(D_MODEL),
                   jax.ShapeDtypeStruct((16, D_MODEL), jnp.float32)),
        in_specs=[tile(D_MODEL), tile(D_MODEL), tile(D_MODEL), _const_spec(vecs.shape), _const_spec(w_out.shape),
                  _const_spec(w1_chunks.shape), _const_spec(w2_chunks.shape)],
        out_specs=(tile(D_MODEL), tile(D_MODEL), tile(D_FF), tile(D_FF), tile(D_MODEL), tile(D_MODEL),
                   tile(D_MODEL), pl.BlockSpec((16, D_MODEL), lambda i: (0, 0))),
        scratch_shapes=[pltpu.VMEM((tm, D_FF), jnp.float32)],
        compiler_params=_params(("arbitrary",)),
    )(x2, mixed, target, vecs, w_out, w1_chunks, w2_chunks)


def _grad_matmul(a, b, name, tn, blocks_are_rows):
    seq, m_dim = a.shape
    n_dim = b.shape[1]
    tk = min(seq, GRAD_TOKEN_TILE)
    nk = seq // tk
    if blocks_are_rows:
        tm = m_dim // N_CHIP
        assert tn == n_dim
        grid = (N_CHIP, 1, nk)
        out_map = lambda i, j, k: (i, 0, 0)
    else:
        tm = m_dim
        assert tn * N_CHIP == n_dim
        grid = (1, N_CHIP, nk)
        out_map = lambda i, j, k: (j, 0, 0)

    def body(a_ref, b_ref, o_ref, acc_sc):
        k = pl.program_id(2)

        @pl.when(k == 0)
        def _():
            acc_sc[...] = jnp.zeros_like(acc_sc)

        acc_sc[...] += _mm_tn(a_ref[...], b_ref[...])

        @pl.when(k == nk - 1)
        def _():
            o_ref[0] = acc_sc[...].astype(o_ref.dtype)

    return pl.pallas_call(
        body, name=name, grid=grid,
        out_shape=jax.ShapeDtypeStruct((N_CHIP, tm, tn), WIRE_DTYPE),
        in_specs=[pl.BlockSpec((tk, tm), lambda i, j, k: (k, i)), pl.BlockSpec((tk, tn), lambda i, j, k: (k, j))],
        out_specs=pl.BlockSpec((1, tm, tn), out_map),
        scratch_shapes=[pltpu.VMEM((tm, tn), jnp.float32)],
        compiler_params=_params(("arbitrary", "arbitrary", "arbitrary")),
    )(a, b)


def _grad_matmul_full(a, b, name, tm):
    seq, m_dim = a.shape
    n_dim = b.shape[1]
    tk = min(seq, GRAD_TOKEN_TILE)
    nk = seq // tk
    assert m_dim % tm == 0

    def body(a_ref, b_ref, o_ref, acc_sc):
        k = pl.program_id(1)

        @pl.when(k == 0)
        def _():
            acc_sc[...] = jnp.zeros_like(acc_sc)

        acc_sc[...] += _mm_tn(a_ref[...], b_ref[...])

        @pl.when(k == nk - 1)
        def _():
            o_ref[...] = acc_sc[...].astype(o_ref.dtype)

    return pl.pallas_call(
        body, name=name, grid=(m_dim // tm, nk),
        out_shape=jax.ShapeDtypeStruct((m_dim, n_dim), WIRE_DTYPE),
        in_specs=[pl.BlockSpec((tk, tm), lambda i, k: (k, i)), pl.BlockSpec((tk, n_dim), lambda i, k: (k, 0))],
        out_specs=pl.BlockSpec((tm, n_dim), lambda i, k: (i, 0)),
        scratch_shapes=[pltpu.VMEM((tm, n_dim), jnp.float32)],
        compiler_params=_params(("arbitrary", "arbitrary")),
    )(a, b)


def _sum_chips(stack, name):
    _, rows, cols = stack.shape
    tc = min(cols, ELEMENTWISE_COLS)

    def body(s_ref, o_ref):
        total = s_ref[0].astype(jnp.float32)
        for j in range(1, N_CHIP):
            total = total + s_ref[j].astype(jnp.float32)
        o_ref[...] = total

    return pl.pallas_call(
        body, name=name, grid=(cols // tc,),
        out_shape=jax.ShapeDtypeStruct((rows, cols), jnp.float32),
        in_specs=[pl.BlockSpec((N_CHIP, rows, tc), lambda i: (0, 0, i))],
        out_specs=pl.BlockSpec((rows, tc), lambda i: (0, i)),
        compiler_params=_params(("arbitrary",)),
    )(stack)


def _adam_pair(w, g_mine, g_sibling, m, v, name):
    rows, cols = w.shape
    tc = min(cols, ELEMENTWISE_COLS)

    def body(w_ref, ga_ref, gb_ref, m_ref, v_ref, g_ref, dl_ref, m2_ref, v2_ref):
        g = ga_ref[...] + gb_ref[...]
        delta, m2, v2 = _adam(w_ref[...], g, m_ref[...], v_ref[...])
        g_ref[...] = g
        dl_ref[...] = delta
        m2_ref[...] = m2
        v2_ref[...] = v2

    blk = pl.BlockSpec((rows, tc), lambda i: (0, i))
    out = jax.ShapeDtypeStruct((rows, cols), jnp.float32)
    return pl.pallas_call(
        body, name=name, grid=(cols // tc,),
        out_shape=(out, out, out, out),
        in_specs=[blk] * 5, out_specs=(blk,) * 4,
        compiler_params=_params(("arbitrary",)),
    )(w, g_mine, g_sibling, m, v)


def _sum_devices(gathered):
    _, rows, _ = gathered.shape

    def body(g_ref, o_ref):
        total = g_ref[0]
        for d in range(1, N_DEV):
            total = total + g_ref[d]
        o_ref[...] = total

    return pl.pallas_call(
        body, name="sum_devices",
        out_shape=jax.ShapeDtypeStruct((rows, 128), jnp.float32),
    )(gathered)


def _adam_small(w, g, m, v):
    def body(w_ref, g_ref, m_ref, v_ref, dl_ref, m2_ref, v2_ref):
        delta, m2, v2 = _adam(w_ref[...], g_ref[...], m_ref[...], v_ref[...])
        dl_ref[...] = delta
        m2_ref[...] = m2
        v2_ref[...] = v2

    out = jax.ShapeDtypeStruct(w.shape, jnp.float32)
    return pl.pallas_call(body, name="adam_small", out_shape=(out, out, out))(w, g, m, v)


def _pad_heads(w):
    lead = w.shape[:-1]
    w = w.reshape(lead + (N_HEADS, GLA_DK))
    w = jnp.pad(w, [(0, 0)] * len(lead) + [(0, 0), (0, HEAD_W - GLA_DK)])
    return w.reshape(lead + (N_HEADS * HEAD_W,))


def _unpad_heads(w):
    lead = w.shape[:-1]
    return w.reshape(lead + (N_HEADS, HEAD_W))[..., :GLA_DK].reshape(lead + (N_HEADS * GLA_DK,))


def _pad_head_rows(w):
    w = w.reshape(N_HEADS, GLA_DK, w.shape[-1])
    return jnp.pad(w, ((0, 0), (0, HEAD_W - GLA_DK), (0, 0))).reshape(N_HEADS * HEAD_W, w.shape[-1])


def _unpad_head_rows(w):
    return w.reshape(N_HEADS, HEAD_W, w.shape[-1])[:, :GLA_DK].reshape(N_HEADS * GLA_DK, w.shape[-1])


def _pad_w_in_rows(w):
    return jnp.concatenate([
        w[:2048], _pad_head_rows(w[2048:2304]), _pad_head_rows(w[2304:2560]), w[2560:3584],
        jnp.pad(w[3584:3600], ((0, HEAD_W - GATE_RANK), (0, 0)))], axis=0)


def _unpad_w_in_rows(g):
    return jnp.concatenate([
        g[:2048], _unpad_head_rows(g[OFF_GQ:OFF_GQ + 512]), _unpad_head_rows(g[OFF_GK:OFF_GK + 512]),
        g[OFF_GV:OFF_LR], g[OFF_LR:OFF_LR + GATE_RANK]], axis=0)


def _rows128(a):
    return a.reshape(-1, 128)


def _rows8(a):
    a = a.reshape(-1, 128)
    return jnp.pad(a, ((0, -a.shape[0] % 8), (0, 0)))


def kernel(x, c, w_ada, b_ada, w_in, ret_norm_w, gla_gate_w, gla_gate_b, gla_norm_w, w_out, ln1_w, ln1_b, w_ff1, w_ff2, ln2_w, ln2_b, loss_target, m_w_ada, m_b_ada, m_w_in, m_ret_norm_w, m_gla_gate_w, m_gla_gate_b, m_gla_norm_w, m_w_out, m_ln1_w, m_ln1_b, m_w_ff1, m_w_ff2, m_ln2_w, m_ln2_b, v_w_ada, v_b_ada, v_w_in, v_ret_norm_w, v_gla_gate_w, v_gla_gate_b, v_gla_norm_w, v_w_out, v_ln1_w, v_ln1_b, v_w_ff1, v_w_ff2, v_ln2_w, v_ln2_b):
    seq = x.shape[1]
    tm = min(seq, TOKEN_TILE)
    tm_in = min(seq, INPROJ_TOKEN_TILE)
    xi, yi, ci = _mesh_pos()
    dev = 4 * xi + 2 * yi + ci
    chip = 2 * xi + yi
    x2, target = x[0], loss_target[0]
    ada_cols = w_ada.shape[2]
    in_cols = w_in.shape[2]
    gate_cols = gla_gate_w.shape[2]

    g0 = _gather_rows(jnp.concatenate([_rows128(c), _rows128(gla_gate_w[0])], axis=0), "gather_cond")
    c_all = g0[:, :8].reshape(N_DEV, D_MODEL)
    gate_w_full = jnp.concatenate([g0[2 * j, 8:16].reshape(GATE_RANK, gate_cols) for j in range(N_CHIP)], axis=1)
    wg_p = jnp.pad(_pad_heads(gate_w_full), ((0, HEAD_W - GATE_RANK), (0, 0)))
    bg_p = _pad_heads(gla_gate_b)

    b_blk = lax.dynamic_slice(b_ada, (0, chip * ada_cols), (1, ada_cols))
    mod_blk = _ada_fwd(c_all, w_ada[0], b_blk)
    g1 = _gather_rows(_rows128(mod_blk), "gather_mod")
    mod_all = jnp.concatenate([g1[2 * j].reshape(N_DEV, ada_cols) for j in range(N_CHIP)], axis=1)
    mod = lax.dynamic_slice(mod_all, (dev, 0), (1, 6 * D_MODEL))
    shift1, scale1, gate1, shift2, scale2, gate2 = [mod[:, i * D_MODEL:(i + 1) * D_MODEL] for i in range(6)]

    (w_in_stack,) = _chip_gather([jnp.transpose(w_in[0]).astype(WIRE_DTYPE)], "gather_w_in")
    w_in_pt = _pad_w_in_rows(w_in_stack.reshape(N_PROJ_SRC, D_MODEL)).astype(MXU_DTYPE)
    w_in_p = jnp.transpose(w_in_pt)

    zeros_row = jnp.zeros((1, D_MODEL), jnp.float32)
    vecs1 = jnp.concatenate([shift1, scale1] + [zeros_row] * 6, axis=0)
    proj, u = _inproj_fwd(x2, vecs1, w_in_p, tm_in)
    rot_a, rot_b = _rotary_tables(seq)
    dm_t, qdec_t, kdec_t, chunk_decay = _decay_tables()
    tables = (rot_a, rot_b, dm_t, qdec_t, kdec_t, chunk_decay)
    mixed, rsave, ssave, w_out_stack, w1_stack, w2_stack = _mixer_fwd(
        proj, tables, wg_p, bg_p, ret_norm_w, gla_norm_w,
        [w_out[0].astype(WIRE_DTYPE), w_ff1[0].astype(WIRE_DTYPE), w_ff2[0].astype(WIRE_DTYPE)])
    w_out_full = w_out_stack.reshape(D_MODEL, D_MODEL).astype(MXU_DTYPE)
    w1_chunks = w1_stack.astype(MXU_DTYPE)
    w2_chunks = w2_stack.astype(MXU_DTYPE)

    vecs2 = jnp.concatenate([gate1, scale2, shift2, gate2, ln1_w, ln1_b, ln2_w, ln2_b], axis=0)
    dmixed, dxa, act, dh, u2, df, dm, sums2 = _mlp_fwd_bwd(x2, mixed, target, vecs2, w_out_full, w1_chunks,
                                                           w2_chunks, tm)

    g_out_stack = _grad_matmul(mixed, dm, "grad_w_out", D_MODEL, True)
    g_ff1_stack = _grad_matmul(u2, dh, "grad_w_ff1", D_FF // N_CHIP, False)
    g_ff2_stack = _grad_matmul(act, df, "grad_w_ff2", D_MODEL, True)
    dproj, d_ret_norm, d_gla_norm, d_wg_p, d_bg_p, r_out, r_ff1, r_ff2 = _mixer_bwd(
        proj, dmixed, rsave, ssave, tables, wg_p, bg_p, ret_norm_w, gla_norm_w,
        [g_out_stack, g_ff1_stack, g_ff2_stack])
    g_in_t = _grad_matmul_full(dproj, u, "grad_w_in", N_PROJ // 3)
    g_in_stack = _unpad_w_in_rows(g_in_t).reshape(N_CHIP, in_cols, D_MODEL)
    grad_x, sums1, r_in = _inproj_bwd(dproj, x2, dxa, vecs1, w_in_pt, tm_in, [g_in_stack])

    dmod = jnp.concatenate([sums1[0:1], sums1[1:2], sums2[S_GATE1:S_GATE1 + 1], sums2[S_SHIFT2:S_SHIFT2 + 1],
                            sums2[S_SCALE2:S_SCALE2 + 1], sums2[S_GATE2:S_GATE2 + 1]], axis=1)
    d_gate_w_full = _unpad_heads(d_wg_p[:GATE_RANK])
    flat = lambda parts: jnp.concatenate([_rows8(p) for p in parts], axis=0)
    small = flat([dmod, sums2[S_LN1W:S_LN1W + 1], sums2[S_LN1B:S_LN1B + 1], sums2[S_LN2W:S_LN2W + 1],
                  sums2[S_LN2B:S_LN2B + 1], d_ret_norm, _unpad_heads(d_bg_p), d_gla_norm, d_gate_w_full,
                  sums2[S_LOSS:S_LOSS + 1]])
    g2 = _gather_rows(small, "gather_small")
    tot = _sum_devices(g2)
    loss = 0.5 / D_MODEL * jnp.sum(tot[136:144])
    grad_b_ada = tot[0:48].reshape(1, 6 * D_MODEL)
    grad_ln1_w, grad_ln1_b = tot[48:56].reshape(1, D_MODEL), tot[56:64].reshape(1, D_MODEL)
    grad_ln2_w, grad_ln2_b = tot[64:72].reshape(1, D_MODEL), tot[72:80].reshape(1, D_MODEL)
    grad_ret_norm = tot[80:84].reshape(1, 512)
    grad_gate_b = tot[88:90].reshape(1, 256)
    grad_gla_norm = tot[96:100].reshape(1, 512)
    grad_gate_w = lax.dynamic_slice(tot[104:136].reshape(GATE_RANK, 256), (0, chip * gate_cols),
                                    (GATE_RANK, gate_cols))

    small_w = flat([b_ada, ln1_w, ln1_b, ln2_w, ln2_b, ret_norm_w, gla_gate_b, gla_norm_w, gla_gate_w[0]])
    small_g = flat([grad_b_ada, grad_ln1_w, grad_ln1_b, grad_ln2_w, grad_ln2_b, grad_ret_norm, grad_gate_b,
                    grad_gla_norm, grad_gate_w])
    small_m = flat([m_b_ada, m_ln1_w, m_ln1_b, m_ln2_w, m_ln2_b, m_ret_norm_w, m_gla_gate_b, m_gla_norm_w,
                    m_gla_gate_w[0]])
    small_v = flat([v_b_ada, v_ln1_w, v_ln1_b, v_ln2_w, v_ln2_b, v_ret_norm_w, v_gla_gate_b, v_gla_norm_w,
                    v_gla_gate_w[0]])
    small_out = _adam_small(small_w, small_g, small_m, small_v)

    def unflat(t):
        pieces, row = [], 0
        for shape in [(1, 6 * D_MODEL)] + [(1, D_MODEL)] * 4 + [(1, 512), (1, 256), (1, 512), (1, GATE_RANK, gate_cols)]:
            n = int(np.prod(shape)) // 128
            pieces.append(t[row:row + n].reshape(shape))
            row += -(-n // 8) * 8
        return pieces

    sm_delta, sm_m, sm_v = [unflat(t) for t in small_out]

    dmod_all = g2[:, 0:48].reshape(N_DEV, 6 * D_MODEL)
    dmod_blk = lax.dynamic_slice(dmod_all, (0, chip * ada_cols), (N_DEV, ada_cols))
    ada_out = _ada_bwd_adam(jnp.transpose(c_all), dmod_blk, w_ada[0], m_w_ada[0], v_w_ada[0])
    ada_g, ada_delta, ada_m, ada_v = [t[None] for t in ada_out]

    received = [r_in, r_out, r_ff1, r_ff2]
    names = ["w_in", "w_out", "w_ff1", "w_ff2"]
    partial = [_sum_chips(r, "sum_" + n) for r, n in zip(received, names)]
    swapped = _sibling_swap(partial, "swap_partials")
    big = {}
    for n, w, mine, theirs, m, v in zip(names, [w_in, w_out, w_ff1, w_ff2], partial, swapped,
                                        [m_w_in, m_w_out, m_w_ff1, m_w_ff2], [v_w_in, v_w_out, v_w_ff1, v_w_ff2]):
        if n == "w_in":
            out = _adam_pair(jnp.transpose(w[0]), mine, theirs, jnp.transpose(m[0]), jnp.transpose(v[0]), "adam_" + n)
            big[n] = [jnp.transpose(t)[None] for t in out]
        else:
            big[n] = [t[None] for t in _adam_pair(w[0], mine, theirs, m[0], v[0], "adam_" + n)]

    def assemble(ada, smalls, k):
        b_ada_o, ln1w_o, ln1b_o, ln2w_o, ln2b_o, ret_o, gb_o, gln_o, gw_o = smalls
        return [ada, b_ada_o, big["w_in"][k], ret_o, gw_o, gb_o, gln_o, big["w_out"][k], ln1w_o, ln1b_o,
                big["w_ff1"][k], big["w_ff2"][k], ln2w_o, ln2b_o]

    small_grads = [grad_b_ada, grad_ln1_w, grad_ln1_b, grad_ln2_w, grad_ln2_b, grad_ret_norm, grad_gate_b,
                   grad_gla_norm, grad_gate_w[None]]
    grads = assemble(ada_g, small_grads, 0)
    deltas = assemble(ada_delta, sm_delta, 1)
    new_m = assemble(ada_m, sm_m, 2)
    new_v = assemble(ada_v, sm_v, 3)
    return (loss, grad_x[None], *grads, *deltas, *new_m, *new_v)
```

```python
import functools

import numpy as np
import jax
import jax.numpy as jnp
from jax import lax
from jax.experimental import pallas as pl
from jax.experimental.pallas import tpu as pltpu

D_MODEL = 1024
D_FF = 4096
CHUNK = 64
N_HEADS = 4
HEAD_W = 128
GLA_DK = 64
GATE_RANK = 16
GATE_TAU = 16.0
LN_EPS = 1e-5
ALPHA = 2.0 ** 0.25
ROPE_BASE = 10000.0
RET_SCALE = float(HEAD_W) ** -0.5
GLA_SCALE = float(GLA_DK) ** -0.5

ADAM_LR = 0.001
ADAM_B1 = 0.9
ADAM_B2 = 0.999
ADAM_EPS = 1e-08
ADAM_WD = 0.01
ADAM_STEP = 10

OFF_RQ, OFF_RK, OFF_RV, OFF_RG = 0, 512, 1024, 1536
OFF_GQ, OFF_GK, OFF_GV, OFF_GG, OFF_LR = 2048, 2560, 3072, 3584, 4096
N_PROJ = 4224
N_PROJ_SRC = 3600

N_DEV = 8
N_CHIP = 4
MESH = pl.DeviceIdType.MESH
MXU_DTYPE = jnp.bfloat16
WIRE_DTYPE = jnp.bfloat16
VMEM_LIMIT = 60 * 1024 * 1024
TOKEN_TILE = 256
INPROJ_TOKEN_TILE = 512
CHUNKS_PER_STEP = 4
GRAD_TOKEN_TILE = 2048
ELEMENTWISE_COLS = 256
HIGHEST = lax.Precision.HIGHEST


def _mm(a, b):
    return jnp.dot(a.astype(MXU_DTYPE), b.astype(MXU_DTYPE), preferred_element_type=jnp.float32)


def _mm_nt(a, b):
    return lax.dot_general(a.astype(MXU_DTYPE), b.astype(MXU_DTYPE), (((1,), (1,)), ((), ())),
                           preferred_element_type=jnp.float32)


def _mm_tn(a, b):
    return lax.dot_general(a.astype(MXU_DTYPE), b.astype(MXU_DTYPE), (((0,), (0,)), ((), ())),
                           preferred_element_type=jnp.float32)


def _mm32(a, b):
    return jnp.dot(a, b, precision=HIGHEST, preferred_element_type=jnp.float32)


def _running_sum(mask, a):
    m = mask.astype(jnp.bfloat16)
    hi = a.astype(jnp.bfloat16)
    rest = a - hi.astype(jnp.float32)
    mid = rest.astype(jnp.bfloat16)
    lo = (rest - mid.astype(jnp.float32)).astype(jnp.bfloat16)
    dot = lambda t: jnp.dot(m, t, preferred_element_type=jnp.float32)
    return dot(hi) + dot(mid) + dot(lo)


def _rowmean(a):
    return jnp.mean(a, axis=-1, keepdims=True)


def _colsum(a):
    return jnp.sum(a, axis=0, keepdims=True)


def _ln(z):
    zc = z - _rowmean(z)
    rstd = lax.rsqrt(_rowmean(zc * zc) + LN_EPS)
    return zc * rstd, rstd


def _ln_bwd(dzh, zh, rstd):
    return rstd * (dzh - _rowmean(dzh) - zh * _rowmean(dzh * zh))


def _sigmoid(a):
    return 1.0 / (1.0 + jnp.exp(-a))


def _log_sigmoid(a):
    return jnp.minimum(a, 0.0) - jnp.log(1.0 + jnp.exp(-jnp.abs(a)))


def _swap_halves(a):
    return pltpu.roll(a, HEAD_W // 2, 1)


def _tri_masks():
    row = lax.broadcasted_iota(jnp.int32, (CHUNK, CHUNK), 0)
    col = lax.broadcasted_iota(jnp.int32, (CHUNK, CHUNK), 1)
    return row, col


def _const_spec(shape):
    zeros = (0,) * len(shape)
    return pl.BlockSpec(shape, lambda *_: zeros, pipeline_mode=pl.Buffered(1))


def _params(semantics):
    return pltpu.CompilerParams(dimension_semantics=semantics, vmem_limit_bytes=VMEM_LIMIT)


def _decay_tables():
    log_gamma = np.log(1.0 - 2.0 ** (-5.0 - np.arange(N_HEADS, dtype=np.float64)))
    idx = np.arange(CHUNK, dtype=np.float64)
    dist = np.abs(idx[:, None] - idx[None, :])
    intra = np.exp(log_gamma[:, None, None] * dist)
    kdec = np.exp(log_gamma[None, :] * (CHUNK - 1.0 - idx)[:, None])
    qdec = np.exp(log_gamma[None, :] * (idx + 1.0)[:, None])
    chunk_decay = np.exp(log_gamma * CHUNK)
    lanes = lambda t: np.repeat(t, HEAD_W, axis=1).astype(np.float32)
    return (jnp.asarray(intra.astype(np.float32)), jnp.asarray(lanes(qdec)), jnp.asarray(lanes(kdec)),
            [float(np.float32(v)) for v in chunk_decay])


def _rotary_tables(seq):
    half = HEAD_W // 2
    inv = 1.0 / (ROPE_BASE ** jnp.linspace(0.0, 1.0, half, dtype=jnp.float32))
    both = lambda t: jnp.concatenate([t, t], axis=-1)
    ang_a = jnp.arange(0, seq, CHUNK, dtype=jnp.float32)[:, None] * inv[None, :]
    rot_a = jnp.stack([both(jnp.cos(ang_a)), both(jnp.sin(ang_a))], axis=1)
    rot_a = jnp.pad(rot_a, ((0, 0), (0, 6), (0, 0)))
    ang_b = jnp.arange(CHUNK, dtype=jnp.float32)[:, None] * inv[None, :]
    cos_b, sin_b = both(jnp.cos(ang_b)), both(jnp.sin(ang_b))
    sign = jnp.concatenate([-jnp.ones((half,), jnp.float32), jnp.ones((half,), jnp.float32)])
    return rot_a, jnp.stack([cos_b, sin_b, cos_b * sign, sin_b * sign])


def _rotary_chunk(ra_ref, c, rb_ref):
    cos_a, sin_a = ra_ref[c, 0:1, :], ra_ref[c, 1:2, :]
    return cos_a * rb_ref[0] - sin_a * rb_ref[1], sin_a * rb_ref[2] + cos_a * rb_ref[3]


def _mesh_pos():
    return lax.axis_index("x"), lax.axis_index("y"), lax.axis_index("c")


def _flip(v, bit):
    return 1 - v if bit else v


def _gather_rows(v, name):
    rows = v.shape[0]

    def body(v_ref, out_ref, send_sems, recv_sems):
        x, y, c = _mesh_pos()
        me = 4 * x + 2 * y + c
        out_ref[me] = v_ref[...]
        sends, recvs = [], []
        for k in range(1, N_DEV):
            px, py, pc = _flip(x, (k >> 2) & 1), _flip(y, (k >> 1) & 1), _flip(c, k & 1)
            peer = 4 * px + 2 * py + pc
            sends.append(pltpu.make_async_remote_copy(
                src_ref=v_ref, dst_ref=out_ref.at[me], send_sem=send_sems.at[k - 1], recv_sem=recv_sems.at[k - 1],
                device_id=(px, py, pc), device_id_type=MESH))
            recvs.append(pltpu.make_async_remote_copy(
                src_ref=v_ref, dst_ref=out_ref.at[peer], send_sem=send_sems.at[k - 1], recv_sem=recv_sems.at[k - 1],
                device_id=(px, py, pc), device_id_type=MESH))
        for cp in sends:
            cp.start()
        for cp in recvs:
            cp.wait_recv()
        for cp in sends:
            cp.wait_send()

    return pl.pallas_call(
        body, name=name,
        out_shape=jax.ShapeDtypeStruct((N_DEV, rows, 128), jnp.float32),
        in_specs=[pl.BlockSpec(memory_space=pltpu.VMEM)],
        out_specs=pl.BlockSpec(memory_space=pltpu.VMEM),
        scratch_shapes=[pltpu.SemaphoreType.DMA((N_DEV - 1,)), pltpu.SemaphoreType.DMA((N_DEV - 1,))],
    )(v)


def _chip_gather(arrays, name):
    n = len(arrays)

    def body(*refs):
        gather = _ChipGather(refs[:n], refs[n:2 * n], refs[2 * n:])
        gather.start()
        gather.forward()
        gather.finish()

    return pl.pallas_call(
        body, name=name,
        out_shape=_exchange_out_shapes(arrays, True),
        in_specs=[pl.BlockSpec(memory_space=pl.ANY)] * n,
        out_specs=tuple(pl.BlockSpec(memory_space=pl.ANY) for _ in arrays),
        scratch_shapes=_gather_sems(n),
    )(*arrays)


def _exchange_out_shapes(arrays, gather):
    return tuple(jax.ShapeDtypeStruct((N_CHIP,) + a.shape if gather else a.shape, a.dtype) for a in arrays)


def _scatter_sems(n):
    n_sem = n * (N_CHIP - 1)
    return [pltpu.SemaphoreType.DMA((n_sem,)), pltpu.SemaphoreType.DMA((n_sem,)), pltpu.SemaphoreType.DMA((n,))]


def _gather_sems(n):
    n_sem = n * (N_CHIP - 1)
    return [pltpu.SemaphoreType.DMA((n_sem,))] * 4 + [pltpu.SemaphoreType.DMA((n,))]


def _peer_chips(x, y):
    out = []
    for k in range(1, N_CHIP):
        px, py = _flip(x, (k >> 1) & 1), _flip(y, k & 1)
        out.append((px, py, 2 * px + py))
    return out


class _ChipScatter:
    def __init__(self, ins, outs, sems):
        send_sems, recv_sems, local_sems = sems
        x, y, c = _mesh_pos()
        chip = 2 * x + y
        self.local, self.sends, self.recvs = [], [], []
        for i in range(len(ins)):
            self.local.append(pltpu.make_async_copy(ins[i].at[chip], outs[i].at[chip], local_sems.at[i]))
            for k, (px, py, peer_chip) in enumerate(_peer_chips(x, y)):
                sem = i * (N_CHIP - 1) + k
                src = ins[i].at[peer_chip]
                self.sends.append(pltpu.make_async_remote_copy(
                    src_ref=src, dst_ref=outs[i].at[chip], send_sem=send_sems.at[sem], recv_sem=recv_sems.at[sem],
                    device_id=(px, py, c), device_id_type=MESH))
                self.recvs.append(pltpu.make_async_remote_copy(
                    src_ref=src, dst_ref=outs[i].at[peer_chip], send_sem=send_sems.at[sem], recv_sem=recv_sems.at[sem],
                    device_id=(px, py, c), device_id_type=MESH))

    def start(self):
        for cp in self.local + self.sends:
            cp.start()

    def wait(self):
        for cp in self.recvs:
            cp.wait_recv()
        for cp in self.sends:
            cp.wait_send()
        for cp in self.local:
            cp.wait()


class _ChipGather:
    def __init__(self, ins, outs, sems):
        ici_send, ici_recv, d2d_send, d2d_recv, local_sems = sems
        x, y, c = _mesh_pos()
        chip = 2 * x + y
        self.local, self.ici_sends, self.ici_recvs, self.d2d_sends, self.d2d_recvs = [], [], [], [], []
        for i in range(len(ins)):
            half = ins[i].shape[-1] // 2
            assert half % 128 == 0
            lead = (slice(None),) * (len(ins[i].shape) - 1)
            mine = lead + (pl.ds(pl.multiple_of(c * half, 128), half),)
            theirs = lead + (pl.ds(pl.multiple_of((1 - c) * half, 128), half),)
            self.local.append(pltpu.make_async_copy(ins[i], outs[i].at[chip], local_sems.at[i]))
            for k, (px, py, peer_chip) in enumerate(_peer_chips(x, y)):
                sem = i * (N_CHIP - 1) + k
                self.ici_sends.append(pltpu.make_async_remote_copy(
                    src_ref=ins[i].at[mine], dst_ref=outs[i].at[chip].at[mine],
                    send_sem=ici_send.at[sem], recv_sem=ici_recv.at[sem], device_id=(px, py, c), device_id_type=MESH))
                landed = outs[i].at[peer_chip].at[mine]
                self.ici_recvs.append(pltpu.make_async_remote_copy(
                    src_ref=ins[i].at[mine], dst_ref=landed,
                    send_sem=ici_send.at[sem], recv_sem=ici_recv.at[sem], device_id=(px, py, c), device_id_type=MESH))
                self.d2d_sends.append(pltpu.make_async_remote_copy(
                    src_ref=landed, dst_ref=landed,
                    send_sem=d2d_send.at[sem], recv_sem=d2d_recv.at[sem], device_id=(x, y, 1 - c), device_id_type=MESH))
                self.d2d_recvs.append(pltpu.make_async_remote_copy(
                    src_ref=landed, dst_ref=outs[i].at[peer_chip].at[theirs],
                    send_sem=d2d_send.at[sem], recv_sem=d2d_recv.at[sem], device_id=(x, y, 1 - c), device_id_type=MESH))

    def start(self):
        for cp in self.local + self.ici_sends:
            cp.start()

    def forward(self):
        for landed, onward in zip(self.ici_recvs, self.d2d_sends):
            landed.wait_recv()
            onward.start()

    def finish(self):
        for cp in self.d2d_recvs:
            cp.wait_recv()
        for cp in self.d2d_sends + self.ici_sends:
            cp.wait_send()
        for cp in self.local:
            cp.wait()


def _sibling_swap(arrays, name):
    n = len(arrays)

    def body(*refs):
        ins, outs = refs[:n], refs[n:2 * n]
        send_sems, recv_sems = refs[2 * n:]
        x, y, c = _mesh_pos()
        copies = [pltpu.make_async_remote_copy(
            src_ref=ins[i], dst_ref=outs[i], send_sem=send_sems.at[i], recv_sem=recv_sems.at[i],
            device_id=(x, y, 1 - c), device_id_type=MESH) for i in range(n)]
        for cp in copies:
            cp.start()
        for cp in copies:
            cp.wait_recv()
        for cp in copies:
            cp.wait_send()

    return pl.pallas_call(
        body, name=name,
        out_shape=tuple(jax.ShapeDtypeStruct(a.shape, a.dtype) for a in arrays),
        in_specs=[pl.BlockSpec(memory_space=pl.ANY)] * n,
        out_specs=tuple(pl.BlockSpec(memory_space=pl.ANY) for _ in arrays),
        scratch_shapes=[pltpu.SemaphoreType.DMA((n,)), pltpu.SemaphoreType.DMA((n,))],
    )(*arrays)


def _ada_fwd(c_all, w_ada_blk, b_blk):
    cols = w_ada_blk.shape[1]

    def body(c_ref, w_ref, b_ref, out_ref):
        cv = c_ref[...]
        out_ref[...] = _mm32(cv * _sigmoid(cv), w_ref[...]) + b_ref[...]

    return pl.pallas_call(
        body, name="ada_fwd",
        out_shape=jax.ShapeDtypeStruct((N_DEV, cols), jnp.float32),
        compiler_params=pltpu.CompilerParams(vmem_limit_bytes=VMEM_LIMIT),
    )(c_all, w_ada_blk, b_blk)


def _adam(w, g, m, v):
    m2 = ADAM_B1 * m + (1.0 - ADAM_B1) * g
    v2 = ADAM_B2 * v + (1.0 - ADAM_B2) * (g * g)
    m_hat = m2 / (1.0 - ADAM_B1 ** ADAM_STEP)
    v_hat = v2 / (1.0 - ADAM_B2 ** ADAM_STEP)
    delta = -ADAM_LR * (m_hat / (jnp.sqrt(v_hat) + ADAM_EPS) + ADAM_WD * w)
    return delta, m2, v2


def _ada_bwd_adam(c_t, dmod_blk, w, m, v):
    rows, cols = w.shape
    tile = 512
    assert cols % tile == 0

    def body(c_ref, d_ref, w_ref, m_ref, v_ref, g_ref, dl_ref, m2_ref, v2_ref):
        sc = c_ref[...]
        sc = sc * _sigmoid(sc)
        dm = d_ref[...]
        g = sc[:, 0:1] * dm[0:1, :]
        for b in range(1, N_DEV):
            g = g + sc[:, b:b + 1] * dm[b:b + 1, :]
        delta, m2, v2 = _adam(w_ref[...], g, m_ref[...], v_ref[...])
        g_ref[...] = g
        dl_ref[...] = delta
        m2_ref[...] = m2
        v2_ref[...] = v2

    blk = pl.BlockSpec((rows, tile), lambda j: (0, j))
    out = jax.ShapeDtypeStruct((rows, cols), jnp.float32)
    return pl.pallas_call(
        body, name="ada_bwd_adam", grid=(cols // tile,),
        out_shape=(out, out, out, out),
        in_specs=[pl.BlockSpec((rows, N_DEV), lambda j: (0, 0)), pl.BlockSpec((N_DEV, tile), lambda j: (0, j)),
                  blk, blk, blk],
        out_specs=(blk, blk, blk, blk),
        compiler_params=_params(("arbitrary",)),
    )(c_t, dmod_blk, w, m, v)


def _inproj_fwd(x2, vecs, w_in_p, tm):
    seq = x2.shape[0]

    def body(x_ref, vec_ref, w_ref, p_ref, u_ref):
        xh, _ = _ln(x_ref[...])
        u = (xh * (1.0 + vec_ref[1:2, :]) + vec_ref[0:1, :]).astype(MXU_DTYPE)
        u_ref[...] = u
        p_ref[...] = _mm(u, w_ref[...])

    return pl.pallas_call(
        body, name="inproj_fwd", grid=(seq // tm,),
        out_shape=(jax.ShapeDtypeStruct((seq, N_PROJ), jnp.float32), jax.ShapeDtypeStruct((seq, D_MODEL), MXU_DTYPE)),
        in_specs=[pl.BlockSpec((tm, D_MODEL), lambda i: (i, 0)), _const_spec(vecs.shape), _const_spec(w_in_p.shape)],
        out_specs=(pl.BlockSpec((tm, N_PROJ), lambda i: (i, 0)), pl.BlockSpec((tm, D_MODEL), lambda i: (i, 0))),
        compiler_params=_params(("arbitrary",)),
    )(x2, vecs, w_in_p)


def _inproj_bwd(dproj, x2, dxa, vecs, w_in_pt, tm, riders):
    seq = x2.shape[0]
    n_tiles = seq // tm
    n_ride = len(riders)

    def body(*refs):
        dp_ref, x_ref, dxa_ref, vec_ref, w_ref = refs[:5]
        ride_in, refs = refs[5:5 + n_ride], refs[5 + n_ride:]
        gx_ref, sums_ref = refs[:2]
        ride_out, sems = refs[2:2 + n_ride], refs[2 + n_ride:]
        exchange = _ChipScatter(ride_in, ride_out, sems)

        @pl.when(pl.program_id(0) == 0)
        def _():
            exchange.start()
            sums_ref[...] = jnp.zeros_like(sums_ref)

        du = _mm(dp_ref[...], w_ref[...])
        xh, rstd = _ln(x_ref[...])
        sums_ref[0:1, :] += _colsum(du)
        sums_ref[1:2, :] += _colsum(du * xh)
        gx_ref[...] = dxa_ref[...] + _ln_bwd(du * (1.0 + vec_ref[1:2, :]), xh, rstd)

        @pl.when(pl.program_id(0) == n_tiles - 1)
        def _():
            exchange.wait()

    tile = pl.BlockSpec((tm, D_MODEL), lambda i: (i, 0))
    hbm = pl.BlockSpec(memory_space=pl.ANY)
    return pl.pallas_call(
        body, name="inproj_bwd", grid=(n_tiles,),
        out_shape=(jax.ShapeDtypeStruct((seq, D_MODEL), jnp.float32), jax.ShapeDtypeStruct((8, D_MODEL), jnp.float32))
        + _exchange_out_shapes(riders, False),
        in_specs=[pl.BlockSpec((tm, N_PROJ), lambda i: (i, 0)), tile, tile, _const_spec(vecs.shape),
                  _const_spec(w_in_pt.shape)] + [hbm] * n_ride,
        out_specs=(tile, pl.BlockSpec((8, D_MODEL), lambda i: (0, 0))) + (hbm,) * n_ride,
        scratch_shapes=_scatter_sems(n_ride),
        compiler_params=_params(("arbitrary",)),
    )(dproj, x2, dxa, vecs, w_in_pt, *riders)


def _head(h):
    return slice(h * HEAD_W, (h + 1) * HEAD_W)


def _cols(ref, off, h):
    return ref[:, off + h * HEAD_W:off + (h + 1) * HEAD_W]


HEADS = range(N_HEADS)


def _mixer_chunk_forward(p_ref, cc, ss, dm_ref, qdec_ref, kdec_ref, wg_ref, bg_ref, ret_state, gla_state_t):
    row, col = _tri_masks()
    lower = row >= col
    f = {}
    f["glr"] = p_ref[:, OFF_LR:OFF_LR + HEAD_W]
    f["logit"] = _mm(f["glr"], wg_ref[...]) + bg_ref[...]
    rq = [_cols(p_ref, OFF_RQ, h) for h in HEADS]
    rk = [_cols(p_ref, OFF_RK, h) for h in HEADS]
    f["rv"] = [_cols(p_ref, OFF_RV, h) for h in HEADS]
    f["qr"] = [(rq[h] * cc + _swap_halves(rq[h]) * ss) * RET_SCALE for h in HEADS]
    f["kr"] = [rk[h] * cc + _swap_halves(rk[h]) * ss for h in HEADS]
    s_raw = [_mm_nt(f["qr"][h], f["kr"][h]) for h in HEADS]
    la = _log_sigmoid(f["logit"]) * (1.0 / GATE_TAU)
    b = _running_sum(lower, la)
    f["qd"] = [f["qr"][h] * qdec_ref[:, _head(h)] for h in HEADS]
    f["kd"] = [f["kr"][h] * kdec_ref[:, _head(h)] for h in HEADS]
    f["scores"] = [s_raw[h] * dm_ref[h] for h in HEADS]
    f["o_ret"] = [_mm(f["scores"][h], f["rv"][h]) + _mm(f["qd"][h], ret_state[h]) for h in HEADS]
    b_last = b[CHUNK - 1:CHUNK, :]
    b_mid = b[CHUNK // 2 - 1:CHUNK // 2, :]
    f["e"], f["ei"] = jnp.exp(b - b_mid), jnp.exp(b_mid - b)
    f["eb"], f["ek"], f["ebl"] = jnp.exp(b), jnp.exp(b_last - b), jnp.exp(b_last)
    gq = [_cols(p_ref, OFF_GQ, h) * GLA_SCALE for h in HEADS]
    gk = [_cols(p_ref, OFF_GK, h) for h in HEADS]
    f["gv"] = [_cols(p_ref, OFF_GV, h) for h in HEADS]
    f["q_e"] = [gq[h] * f["e"][:, _head(h)] for h in HEADS]
    f["q_i"] = [gq[h] * f["ei"][:, _head(h)] for h in HEADS]
    f["k_e"] = [gk[h] * f["e"][:, _head(h)] for h in HEADS]
    f["k_i"] = [gk[h] * f["ei"][:, _head(h)] for h in HEADS]
    low = [_mm_nt(f["q_e"][h], f["k_i"][h]) for h in HEADS]
    up = [_mm_nt(f["q_i"][h], f["k_e"][h]) for h in HEADS]
    f["att"] = [jnp.where(lower, low[h], up[h]) for h in HEADS]
    f["qb"] = [gq[h] * f["eb"][:, _head(h)] for h in HEADS]
    f["kb"] = [gk[h] * f["ek"][:, _head(h)] for h in HEADS]
    f["o_gla"] = [_mm(f["att"][h], f["gv"][h]) + _mm_nt(f["qb"][h], gla_state_t[h]) for h in HEADS]
    return f


def _mixer_fwd(proj, tables, wg_p, bg_p, ret_norm_w, gla_norm_w, riders):
    seq = proj.shape[0]
    n_chunks = seq // CHUNK
    per_step = min(n_chunks, CHUNKS_PER_STEP)
    n_steps = n_chunks // per_step
    n_ride = len(riders)
    rot_a, rot_b, dm_t, qdec_t, kdec_t, chunk_decay = tables

    def body(*refs):
        p_ref, ra_ref, rb_ref, dm_ref, qdec_ref, kdec_ref, wg_ref, bg_ref, wr_ref, wl_ref = refs[:10]
        ride_in, refs = refs[10:10 + n_ride], refs[10 + n_ride:]
        mix_ref, rsave_ref, ssave_ref = refs[:3]
        ride_out, refs = refs[3:3 + n_ride], refs[3 + n_ride:]
        r_sc, s_sc = refs[:2]
        gather = _ChipGather(ride_in, ride_out, refs[2:])

        @pl.when(pl.program_id(0) == 0)
        def _():
            gather.start()
            r_sc[...] = jnp.zeros_like(r_sc)
            s_sc[...] = jnp.zeros_like(s_sc)

        for c in range(per_step):
            p_c = p_ref.at[c * CHUNK:(c + 1) * CHUNK, :]
            mix_c = mix_ref.at[c * CHUNK:(c + 1) * CHUNK, :]
            ret_state = [r_sc[h] for h in HEADS]
            gla_state_t = [s_sc[h] for h in HEADS]
            for h in HEADS:
                rsave_ref[c, h] = ret_state[h]
                ssave_ref[c, h] = gla_state_t[h]
            cc, ss = _rotary_chunk(ra_ref, c, rb_ref)
            f = _mixer_chunk_forward(p_c, cc, ss, dm_ref, qdec_ref, kdec_ref, wg_ref, bg_ref, ret_state, gla_state_t)
            for h in HEADS:
                r_sc[h] = chunk_decay[h] * ret_state[h] + _mm_tn(f["kd"][h], f["rv"][h])
            for h in HEADS:
                s_sc[h] = gla_state_t[h] * f["ebl"][:, _head(h)] + _mm_tn(f["gv"][h], f["kb"][h])
            for h in HEADS:
                on, _ = _ln(f["o_ret"][h])
                g = _cols(p_c, OFF_RG, h)
                mix_c[:, _head(h)] = (on * wr_ref[:, _head(h)] * (g * _sigmoid(g))).astype(mix_ref.dtype)
            for h in HEADS:
                o = f["o_gla"][h]
                on = o * lax.rsqrt(_rowmean(o * o) + LN_EPS)
                g = _cols(p_c, OFF_GG, h)
                mix_c[:, _head(N_HEADS + h)] = (on * wl_ref[:, _head(h)] * (g * _sigmoid(g))).astype(mix_ref.dtype)

        @pl.when(pl.program_id(0) == (3 * n_steps) // 4)
        def _():
            gather.forward()

        @pl.when(pl.program_id(0) == n_steps - 1)
        def _():
            gather.finish()

    state_shape = (n_chunks, N_HEADS, HEAD_W, HEAD_W)
    state_blk = pl.BlockSpec((per_step, N_HEADS, HEAD_W, HEAD_W), lambda i: (i, 0, 0, 0))
    rot_blk = pl.BlockSpec((per_step, 8, HEAD_W), lambda i: (i, 0, 0))
    rows = per_step * CHUNK
    hbm = pl.BlockSpec(memory_space=pl.ANY)
    return pl.pallas_call(
        body, name="mixer_fwd", grid=(n_steps,),
        out_shape=(jax.ShapeDtypeStruct((seq, D_MODEL), MXU_DTYPE),
                   jax.ShapeDtypeStruct(state_shape, jnp.float32), jax.ShapeDtypeStruct(state_shape, jnp.float32))
        + _exchange_out_shapes(riders, True),
        in_specs=[pl.BlockSpec((rows, N_PROJ), lambda i: (i, 0)), rot_blk, _const_spec(rot_b.shape),
                  _const_spec(dm_t.shape), _const_spec(qdec_t.shape), _const_spec(kdec_t.shape),
                  _const_spec(wg_p.shape), _const_spec(bg_p.shape), _const_spec(ret_norm_w.shape),
                  _const_spec(gla_norm_w.shape)] + [hbm] * n_ride,
        out_specs=(pl.BlockSpec((rows, D_MODEL), lambda i: (i, 0)), state_blk, state_blk) + (hbm,) * n_ride,
        scratch_shapes=[pltpu.VMEM((N_HEADS, HEAD_W, HEAD_W), jnp.float32),
                        pltpu.VMEM((N_HEADS, HEAD_W, HEAD_W), jnp.float32)] + _gather_sems(n_ride),
        compiler_params=_params(("arbitrary",)),
    )(proj, rot_a, rot_b, dm_t, qdec_t, kdec_t, wg_p, bg_p, ret_norm_w, gla_norm_w, *riders)


def _mixer_bwd(proj, dmixed, rsave, ssave, tables, wg_p, bg_p, ret_norm_w, gla_norm_w, riders):
    seq = proj.shape[0]
    n_chunks = seq // CHUNK
    per_step = min(n_chunks, CHUNKS_PER_STEP)
    n_steps = n_chunks // per_step
    n_ride = len(riders)
    rot_a, rot_b, dm_t, qdec_t, kdec_t, chunk_decay = tables
    last = n_steps - 1

    def body(*refs):
        p_blk, dmx_blk = refs[:2]
        shared_in = refs[2:13]
        ride_in, refs = refs[13:13 + n_ride], refs[13 + n_ride:]
        dp_blk, dwr_ref, dwl_ref, dwg_ref, dbg_ref = refs[:5]
        ride_out, refs = refs[5:5 + n_ride], refs[5 + n_ride:]
        dr_sc, ds_sc = refs[:2]
        exchange = _ChipScatter(ride_in, ride_out, refs[2:])

        @pl.when(pl.program_id(0) == 0)
        def _():
            exchange.start()
            dr_sc[...] = jnp.zeros_like(dr_sc)
            ds_sc[...] = jnp.zeros_like(ds_sc)
            dwr_ref[...] = jnp.zeros_like(dwr_ref)
            dwl_ref[...] = jnp.zeros_like(dwl_ref)
            dwg_ref[...] = jnp.zeros_like(dwg_ref)
            dbg_ref[...] = jnp.zeros_like(dbg_ref)

        for c in reversed(range(per_step)):
            rows = slice(c * CHUNK, (c + 1) * CHUNK)
            one_chunk(c, p_blk.at[rows, :], dmx_blk.at[rows, :], dp_blk.at[rows, :], *shared_in,
                      dwr_ref, dwl_ref, dwg_ref, dbg_ref, dr_sc, ds_sc)

        @pl.when(pl.program_id(0) == last)
        def _():
            exchange.wait()

    def one_chunk(c, p_ref, dmx_ref, dp_ref, rsave_ref, ssave_ref, ra_ref, rb_ref, dm_ref, qdec_ref, kdec_ref,
                  wg_ref, bg_ref, wr_ref, wl_ref, dwr_ref, dwl_ref, dwg_ref, dbg_ref, dr_sc, ds_sc):
        def put(off, h, val):
            dp_ref[:, off + h * HEAD_W:off + (h + 1) * HEAD_W] = val.astype(dp_ref.dtype)

        cc, ss = _rotary_chunk(ra_ref, c, rb_ref)
        row, col = _tri_masks()
        ret_state = [rsave_ref[c, h] for h in HEADS]
        gla_state_t = [ssave_ref[c, h] for h in HEADS]
        d_ret_new = [dr_sc[h] for h in HEADS]
        d_gla_new = [ds_sc[h] for h in HEADS]
        f = _mixer_chunk_forward(p_ref, cc, ss, dm_ref, qdec_ref, kdec_ref, wg_ref, bg_ref, ret_state, gla_state_t)

        do_ret, do_gla = [], []
        for h in HEADS:
            on, rstd = _ln(f["o_ret"][h])
            g = _cols(p_ref, OFF_RG, h)
            sg = _sigmoid(g)
            dy = dmx_ref[:, _head(h)].astype(jnp.float32)
            wr = wr_ref[:, _head(h)]
            dwr_ref[:, _head(h)] += _colsum(dy * on * (g * sg))
            put(OFF_RG, h, dy * on * wr * (sg * (1.0 + g * (1.0 - sg))))
            do_ret.append(_ln_bwd(dy * wr * (g * sg), on, rstd))
        for h in HEADS:
            o = f["o_gla"][h]
            rstd = lax.rsqrt(_rowmean(o * o) + LN_EPS)
            on = o * rstd
            g = _cols(p_ref, OFF_GG, h)
            sg = _sigmoid(g)
            dy = dmx_ref[:, _head(N_HEADS + h)].astype(jnp.float32)
            wl = wl_ref[:, _head(h)]
            dwl_ref[:, _head(h)] += _colsum(dy * on * (g * sg))
            put(OFF_GG, h, dy * on * wl * (sg * (1.0 + g * (1.0 - sg))))
            don = dy * wl * (g * sg)
            do_gla.append(rstd * (don - on * _rowmean(don * on)))

        ds_raw = [_mm_nt(do_ret[h], f["rv"][h]) * dm_ref[h] for h in HEADS]
        d_att = [_mm_nt(do_gla[h], f["gv"][h]) for h in HEADS]
        dq_state = [_mm_nt(do_ret[h], ret_state[h]) for h in HEADS]
        dk_state = [_mm_nt(f["rv"][h], d_ret_new[h]) for h in HEADS]
        dqb = [_mm(do_gla[h], gla_state_t[h]) for h in HEADS]
        dkb = [_mm(f["gv"][h], d_gla_new[h]) for h in HEADS]
        for h in HEADS:
            put(OFF_RV, h, _mm_tn(f["scores"][h], do_ret[h]) + _mm(f["kd"][h], d_ret_new[h]))
        for h in HEADS:
            put(OFF_GV, h, _mm_tn(f["att"][h], do_gla[h]) + _mm_nt(f["kb"][h], d_gla_new[h]))
        for h in HEADS:
            dr_sc[h] = chunk_decay[h] * d_ret_new[h] + _mm_tn(f["qd"][h], do_ret[h])
        for h in HEADS:
            ds_sc[h] = d_gla_new[h] * f["ebl"][:, _head(h)] + _mm_tn(do_gla[h], f["qb"][h])

        dqr = [_mm(ds_raw[h], f["kr"][h]) + dq_state[h] * qdec_ref[:, _head(h)] for h in HEADS]
        dkr = [_mm_tn(ds_raw[h], f["qr"][h]) + dk_state[h] * kdec_ref[:, _head(h)] for h in HEADS]
        d_low = [jnp.where(row >= col, d_att[h], 0.0) for h in HEADS]
        d_up = [jnp.where(row < col, d_att[h], 0.0) for h in HEADS]
        dq_e = [_mm(d_low[h], f["k_i"][h]) for h in HEADS]
        dk_i = [_mm_tn(d_low[h], f["q_e"][h]) for h in HEADS]
        dq_i = [_mm(d_up[h], f["k_e"][h]) for h in HEADS]
        dk_e = [_mm_tn(d_up[h], f["q_i"][h]) for h in HEADS]
        for h in HEADS:
            put(OFF_RQ, h, (dqr[h] * cc + _swap_halves(dqr[h] * ss)) * RET_SCALE)
            put(OFF_RK, h, dkr[h] * cc + _swap_halves(dkr[h] * ss))
        row_id = lax.broadcasted_iota(jnp.int32, (CHUNK, HEAD_W), 0)
        db_heads = []
        for h in HEADS:
            hs = _head(h)
            e, ei, eb, ek, ebl = f["e"][:, hs], f["ei"][:, hs], f["eb"][:, hs], f["ek"][:, hs], f["ebl"][:, hs]
            put(OFF_GQ, h, (dq_e[h] * e + dq_i[h] * ei + dqb[h] * eb) * GLA_SCALE)
            put(OFF_GK, h, dk_e[h] * e + dk_i[h] * ei + dkb[h] * ek)
            db = (dq_e[h] * f["q_e"][h] - dq_i[h] * f["q_i"][h] + dk_e[h] * f["k_e"][h] - dk_i[h] * f["k_i"][h]
                  + dqb[h] * f["qb"][h] - dkb[h] * f["kb"][h])
            db_last = _colsum(dkb[h] * f["kb"][h]) + ebl * _colsum(gla_state_t[h] * d_gla_new[h])
            db_heads.append(db + jnp.where(row_id == CHUNK - 1, db_last, 0.0))
        db = jnp.concatenate(db_heads, axis=1)
        d_la = _running_sum(col >= row, db)
        d_logit = d_la * (1.0 / GATE_TAU) * (1.0 - _sigmoid(f["logit"]))
        put(OFF_LR, 0, _mm_nt(d_logit, wg_ref[...]))
        dwg_ref[...] += _mm_tn(f["glr"], d_logit)
        dbg_ref[...] += _colsum(d_logit)

    state_blk = pl.BlockSpec((per_step, N_HEADS, HEAD_W, HEAD_W), lambda i: (last - i, 0, 0, 0))
    rot_blk = pl.BlockSpec((per_step, 8, HEAD_W), lambda i: (last - i, 0, 0))
    width = N_HEADS * HEAD_W
    vec_out = pl.BlockSpec((1, width), lambda i: (0, 0))
    hbm = pl.BlockSpec(memory_space=pl.ANY)
    rows_blk = per_step * CHUNK
    return pl.pallas_call(
        body, name="mixer_bwd", grid=(n_steps,),
        out_shape=(jax.ShapeDtypeStruct((seq, N_PROJ), MXU_DTYPE),
                   jax.ShapeDtypeStruct((1, width), jnp.float32), jax.ShapeDtypeStruct((1, width), jnp.float32),
                   jax.ShapeDtypeStruct((HEAD_W, width), jnp.float32), jax.ShapeDtypeStruct((1, width), jnp.float32))
        + _exchange_out_shapes(riders, False),
        in_specs=[pl.BlockSpec((rows_blk, N_PROJ), lambda i: (last - i, 0)),
                  pl.BlockSpec((rows_blk, D_MODEL), lambda i: (last - i, 0)), state_blk, state_blk, rot_blk,
                  _const_spec(rot_b.shape),
                  _const_spec(dm_t.shape), _const_spec(qdec_t.shape), _const_spec(kdec_t.shape),
                  _const_spec(wg_p.shape), _const_spec(bg_p.shape), _const_spec(ret_norm_w.shape),
                  _const_spec(gla_norm_w.shape)] + [hbm] * n_ride,
        out_specs=(pl.BlockSpec((rows_blk, N_PROJ), lambda i: (last - i, 0)), vec_out, vec_out,
                   pl.BlockSpec((HEAD_W, width), lambda i: (0, 0)), vec_out) + (hbm,) * n_ride,
        scratch_shapes=[pltpu.VMEM((N_HEADS, HEAD_W, HEAD_W), jnp.float32),
                        pltpu.VMEM((N_HEADS, HEAD_W, HEAD_W), jnp.float32)] + _scatter_sems(n_ride),
        compiler_params=_params(("arbitrary",)),
    )(proj, dmixed, rsave, ssave, rot_a, rot_b, dm_t, qdec_t, kdec_t, wg_p, bg_p, ret_norm_w, gla_norm_w, *riders)


V_GATE1, V_SCALE2, V_SHIFT2, V_GATE2, V_LN1W, V_LN1B, V_LN2W, V_LN2B = range(8)
S_GATE1, S_SCALE2, S_SHIFT2, S_GATE2, S_LN1W, S_LN1B, S_LN2W, S_LN2B, S_LOSS = range(9)


def _mlp_fwd_bwd(x2, mixed, target, vecs, w_out, w1_chunks, w2_chunks, tm):
    seq = x2.shape[0]
    n_fc, _, fc = w1_chunks.shape

    def body(x_ref, mx_ref, t_ref, vec_ref, wo_ref, w1_ref, w2_ref,
             dmx_ref, dxa_ref, a_ref, dh_ref, u2_ref, df_ref, dm_ref, sums_ref, relu_sc):
        @pl.when(pl.program_id(0) == 0)
        def _():
            sums_ref[...] = jnp.zeros_like(sums_ref)

        vec = lambda r: vec_ref[r:r + 1, :]

        def acc(r, val):
            sums_ref[r:r + 1, :] += _colsum(val)

        xx = x_ref[...]
        m = _mm(mx_ref[...], wo_ref[...])
        z1h, rstd1 = _ln(ALPHA * xx + vec(V_GATE1) * m)
        x1 = z1h * vec(V_LN1W) + vec(V_LN1B)
        x1h, rstd0 = _ln(x1)
        u2 = (x1h * (1.0 + vec(V_SCALE2)) + vec(V_SHIFT2)).astype(MXU_DTYPE)
        u2_ref[...] = u2
        f = jnp.zeros((tm, D_MODEL), jnp.float32)
        for j in range(n_fc):
            r = jnp.maximum(_mm(u2, w1_ref[j]), 0.0)
            relu_sc[:, j * fc:(j + 1) * fc] = r
            a = (r * r).astype(MXU_DTYPE)
            a_ref[:, j * fc:(j + 1) * fc] = a
            f = f + _mm(a, w2_ref[j])
        z2h, rstd2 = _ln(ALPHA * x1 + vec(V_GATE2) * f)
        err = z2h * vec(V_LN2W) + vec(V_LN2B) - t_ref[...]
        acc(S_LOSS, err * err)
        dy = err * (1.0 / D_MODEL)
        acc(S_LN2W, dy * z2h)
        acc(S_LN2B, dy)
        dz2 = _ln_bwd(dy * vec(V_LN2W), z2h, rstd2)
        acc(S_GATE2, dz2 * f)
        df = (vec(V_GATE2) * dz2).astype(MXU_DTYPE)
        df_ref[...] = df
        du2 = jnp.zeros((tm, D_MODEL), jnp.float32)
        for j in range(n_fc):
            dh = (_mm_nt(df, w2_ref[j]) * (2.0 * relu_sc[:, j * fc:(j + 1) * fc])).astype(MXU_DTYPE)
            dh_ref[:, j * fc:(j + 1) * fc] = dh
            du2 = du2 + _mm_nt(dh, w1_ref[j])
        acc(S_SCALE2, du2 * x1h)
        acc(S_SHIFT2, du2)
        dx1 = ALPHA * dz2 + _ln_bwd(du2 * (1.0 + vec(V_SCALE2)), x1h, rstd0)
        acc(S_LN1W, dx1 * z1h)
        acc(S_LN1B, dx1)
        dz1 = _ln_bwd(dx1 * vec(V_LN1W), z1h, rstd1)
        acc(S_GATE1, dz1 * m)
        dxa_ref[...] = ALPHA * dz1
        dm = (vec(V_GATE1) * dz1).astype(MXU_DTYPE)
        dm_ref[...] = dm
        dmx_ref[...] = _mm_nt(dm, wo_ref[...])

    tile = lambda width: pl.BlockSpec((tm, width), lambda i: (i, 0))
    f32 = lambda width: jax.ShapeDtypeStruct((seq, width), jnp.float32)
    b16 = lambda width: jax.ShapeDtypeStruct((seq, width), MXU_DTYPE)
    return pl.pallas_call(
        body, name="mlp_fwd_bwd", grid=(seq // tm,),
        out_shape=(f32(D_MODEL), f32(D_MODEL), b16(D_FF), b16(D_FF), b16(D_MODEL), b16(D_MODEL), b16(D_MODEL),
                   jax.ShapeDtypeStruct((16, D_MODEL), jnp.float32)),
        in_specs=[tile(D_MODEL), tile(D_MODEL), tile(D_MODEL), _const_spec(vecs.shape), _const_spec(w_out.shape),
                  _const_spec(w1_chunks.shape), _const_spec(w2_chunks.shape)],
        out_specs=(tile(D_MODEL), tile(D_MODEL), tile(D_FF), tile(D_FF), tile(D_MODEL), tile(D_MODEL),
                   tile(D_MODEL), pl.BlockSpec((16, D_MODEL), lambda i: (0, 0))),
        scratch_shapes=[pltpu.VMEM((tm, D_FF), jnp.float32)],
        compiler_params=_params(("arbitrary",)),
    )(x2, mixed, target, vecs, w_out, w1_chunks, w2_chunks)


def _grad_matmul(a, b, name, tn, blocks_are_rows):
    seq, m_dim = a.shape
    n_dim = b.shape[1]
    tk = min(seq, GRAD_TOKEN_TILE)
    nk = seq // tk
    if blocks_are_rows:
        tm = m_dim // N_CHIP
        assert tn == n_dim
        grid = (N_CHIP, 1, nk)
        out_map = lambda i, j, k: (i, 0, 0)
    else:
        tm = m_dim
        assert tn * N_CHIP == n_dim
        grid = (1, N_CHIP, nk)
        out_map = lambda i, j, k: (j, 0, 0)

    def body(a_ref, b_ref, o_ref, acc_sc):
        k = pl.program_id(2)

        @pl.when(k == 0)
        def _():
            acc_sc[...] = jnp.zeros_like(acc_sc)

        acc_sc[...] += _mm_tn(a_ref[...], b_ref[...])

        @pl.when(k == nk - 1)
        def _():
            o_ref[0] = acc_sc[...].astype(o_ref.dtype)

    return pl.pallas_call(
        body, name=name, grid=grid,
        out_shape=jax.ShapeDtypeStruct((N_CHIP, tm, tn), WIRE_DTYPE),
        in_specs=[pl.BlockSpec((tk, tm), lambda i, j, k: (k, i)), pl.BlockSpec((tk, tn), lambda i, j, k: (k, j))],
        out_specs=pl.BlockSpec((1, tm, tn), out_map),
        scratch_shapes=[pltpu.VMEM((tm, tn), jnp.float32)],
        compiler_params=_params(("arbitrary", "arbitrary", "arbitrary")),
    )(a, b)


def _grad_matmul_full(a, b, name, tm):
    seq, m_dim = a.shape
    n_dim = b.shape[1]
    tk = min(seq, GRAD_TOKEN_TILE)
    nk = seq // tk
    assert m_dim % tm == 0

    def body(a_ref, b_ref, o_ref, acc_sc):
        k = pl.program_id(1)

        @pl.when(k == 0)
        def _():
            acc_sc[...] = jnp.zeros_like(acc_sc)

        acc_sc[...] += _mm_tn(a_ref[...], b_ref[...])

        @pl.when(k == nk - 1)
        def _():
            o_ref[...] = acc_sc[...].astype(o_ref.dtype)

    return pl.pallas_call(
        body, name=name, grid=(m_dim // tm, nk),
        out_shape=jax.ShapeDtypeStruct((m_dim, n_dim), WIRE_DTYPE),
        in_specs=[pl.BlockSpec((tk, tm), lambda i, k: (k, i)), pl.BlockSpec((tk, n_dim), lambda i, k: (k, 0))],
        out_specs=pl.BlockSpec((tm, n_dim), lambda i, k: (i, 0)),
        scratch_shapes=[pltpu.VMEM((tm, n_dim), jnp.float32)],
        compiler_params=_params(("arbitrary", "arbitrary")),
    )(a, b)


def _sum_chips(stack, name):
    _, rows, cols = stack.shape
    tc = min(cols, ELEMENTWISE_COLS)

    def body(s_ref, o_ref):
        total = s_ref[0].astype(jnp.float32)
        for j in range(1, N_CHIP):
            total = total + s_ref[j].astype(jnp.float32)
        o_ref[...] = total

    return pl.pallas_call(
        body, name=name, grid=(cols // tc,),
        out_shape=jax.ShapeDtypeStruct((rows, cols), jnp.float32),
        in_specs=[pl.BlockSpec((N_CHIP, rows, tc), lambda i: (0, 0, i))],
        out_specs=pl.BlockSpec((rows, tc), lambda i: (0, i)),
        compiler_params=_params(("arbitrary",)),
    )(stack)


def _adam_pair(w, g_mine, g_sibling, m, v, name):
    rows, cols = w.shape
    tc = min(cols, ELEMENTWISE_COLS)

    def body(w_ref, ga_ref, gb_ref, m_ref, v_ref, g_ref, dl_ref, m2_ref, v2_ref):
        g = ga_ref[...] + gb_ref[...]
        delta, m2, v2 = _adam(w_ref[...], g, m_ref[...], v_ref[...])
        g_ref[...] = g
        dl_ref[...] = delta
        m2_ref[...] = m2
        v2_ref[...] = v2

    blk = pl.BlockSpec((rows, tc), lambda i: (0, i))
    out = jax.ShapeDtypeStruct((rows, cols), jnp.float32)
    return pl.pallas_call(
        body, name=name, grid=(cols // tc,),
        out_shape=(out, out, out, out),
        in_specs=[blk] * 5, out_specs=(blk,) * 4,
        compiler_params=_params(("arbitrary",)),
    )(w, g_mine, g_sibling, m, v)


def _sum_devices(gathered):
    _, rows, _ = gathered.shape

    def body(g_ref, o_ref):
        total = g_ref[0]
        for d in range(1, N_DEV):
            total = total + g_ref[d]
        o_ref[...] = total

    return pl.pallas_call(
        body, name="sum_devices",
        out_shape=jax.ShapeDtypeStruct((rows, 128), jnp.float32),
    )(gathered)


def _adam_small(w, g, m, v):
    def body(w_ref, g_ref, m_ref, v_ref, dl_ref, m2_ref, v2_ref):
        delta, m2, v2 = _adam(w_ref[...], g_ref[...], m_ref[...], v_ref[...])
        dl_ref[...] = delta
        m2_ref[...] = m2
        v2_ref[...] = v2

    out = jax.ShapeDtypeStruct(w.shape, jnp.float32)
    return pl.pallas_call(body, name="adam_small", out_shape=(out, out, out))(w, g, m, v)


def _pad_heads(w):
    lead = w.shape[:-1]
    w = w.reshape(lead + (N_HEADS, GLA_DK))
    w = jnp.pad(w, [(0, 0)] * len(lead) + [(0, 0), (0, HEAD_W - GLA_DK)])
    return w.reshape(lead + (N_HEADS * HEAD_W,))


def _unpad_heads(w):
    lead = w.shape[:-1]
    return w.reshape(lead + (N_HEADS, HEAD_W))[..., :GLA_DK].reshape(lead + (N_HEADS * GLA_DK,))


def _pad_head_rows(w):
    w = w.reshape(N_HEADS, GLA_DK, w.shape[-1])
    return jnp.pad(w, ((0, 0), (0, HEAD_W - GLA_DK), (0, 0))).reshape(N_HEADS * HEAD_W, w.shape[-1])


def _unpad_head_rows(w):
    return w.reshape(N_HEADS, HEAD_W, w.shape[-1])[:, :GLA_DK].reshape(N_HEADS * GLA_DK, w.shape[-1])


def _pad_w_in_rows(w):
    return jnp.concatenate([
        w[:2048], _pad_head_rows(w[2048:2304]), _pad_head_rows(w[2304:2560]), w[2560:3584],
        jnp.pad(w[3584:3600], ((0, HEAD_W - GATE_RANK), (0, 0)))], axis=0)


def _unpad_w_in_rows(g):
    return jnp.concatenate([
        g[:2048], _unpad_head_rows(g[OFF_GQ:OFF_GQ + 512]), _unpad_head_rows(g[OFF_GK:OFF_GK + 512]),
        g[OFF_GV:OFF_LR], g[OFF_LR:OFF_LR + GATE_RANK]], axis=0)


def _rows128(a):
    return a.reshape(-1, 128)


def _rows8(a):
    a = a.reshape(-1, 128)
    return jnp.pad(a, ((0, -a.shape[0] % 8), (0, 0)))


def kernel(x, c, w_ada, b_ada, w_in, ret_norm_w, gla_gate_w, gla_gate_b, gla_norm_w, w_out, ln1_w, ln1_b, w_ff1, w_ff2, ln2_w, ln2_b, loss_target, m_w_ada, m_b_ada, m_w_in, m_ret_norm_w, m_gla_gate_w, m_gla_gate_b, m_gla_norm_w, m_w_out, m_ln1_w, m_ln1_b, m_w_ff1, m_w_ff2, m_ln2_w, m_ln2_b, v_w_ada, v_b_ada, v_w_in, v_ret_norm_w, v_gla_gate_w, v_gla_gate_b, v_gla_norm_w, v_w_out, v_ln1_w, v_ln1_b, v_w_ff1, v_w_ff2, v_ln2_w, v_ln2_b):
    seq = x.shape[1]
    tm = min(seq, TOKEN_TILE)
    tm_in = min(seq, INPROJ_TOKEN_TILE)
    xi, yi, ci = _mesh_pos()
    dev = 4 * xi + 2 * yi + ci
    chip = 2 * xi + yi
    x2, target = x[0], loss_target[0]
    ada_cols = w_ada.shape[2]
    in_cols = w_in.shape[2]
    gate_cols = gla_gate_w.shape[2]

    g0 = _gather_rows(jnp.concatenate([_rows128(c), _rows128(gla_gate_w[0])], axis=0), "gather_cond")
    c_all = g0[:, :8].reshape(N_DEV, D_MODEL)
    gate_w_full = jnp.concatenate([g0[2 * j, 8:16].reshape(GATE_RANK, gate_cols) for j in range(N_CHIP)], axis=1)
    wg_p = jnp.pad(_pad_heads(gate_w_full), ((0, HEAD_W - GATE_RANK), (0, 0)))
    bg_p = _pad_heads(gla_gate_b)

    b_blk = lax.dynamic_slice(b_ada, (0, chip * ada_cols), (1, ada_cols))
    mod_blk = _ada_fwd(c_all, w_ada[0], b_blk)
    g1 = _gather_rows(_rows128(mod_blk), "gather_mod")
    mod_all = jnp.concatenate([g1[2 * j].reshape(N_DEV, ada_cols) for j in range(N_CHIP)], axis=1)
    mod = lax.dynamic_slice(mod_all, (dev, 0), (1, 6 * D_MODEL))
    shift1, scale1, gate1, shift2, scale2, gate2 = [mod[:, i * D_MODEL:(i + 1) * D_MODEL] for i in range(6)]

    (w_in_stack,) = _chip_gather([jnp.transpose(w_in[0]).astype(WIRE_DTYPE)], "gather_w_in")
    w_in_pt = _pad_w_in_rows(w_in_stack.reshape(N_PROJ_SRC, D_MODEL)).astype(MXU_DTYPE)
    w_in_p = jnp.transpose(w_in_pt)

    zeros_row = jnp.zeros((1, D_MODEL), jnp.float32)
    vecs1 = jnp.concatenate([shift1, scale1] + [zeros_row] * 6, axis=0)
    proj, u = _inproj_fwd(x2, vecs1, w_in_p, tm_in)
    rot_a, rot_b = _rotary_tables(seq)
    dm_t, qdec_t, kdec_t, chunk_decay = _decay_tables()
    tables = (rot_a, rot_b, dm_t, qdec_t, kdec_t, chunk_decay)
    mixed, rsave, ssave, w_out_stack, w1_stack, w2_stack = _mixer_fwd(
        proj, tables, wg_p, bg_p, ret_norm_w, gla_norm_w,
        [w_out[0].astype(WIRE_DTYPE), w_ff1[0].astype(WIRE_DTYPE), w_ff2[0].astype(WIRE_DTYPE)])
    w_out_full = w_out_stack.reshape(D_MODEL, D_MODEL).astype(MXU_DTYPE)
    w1_chunks = w1_stack.astype(MXU_DTYPE)
    w2_chunks = w2_stack.astype(MXU_DTYPE)

    vecs2 = jnp.concatenate([gate1, scale2, shift2, gate2, ln1_w, ln1_b, ln2_w, ln2_b], axis=0)
    dmixed, dxa, act, dh, u2, df, dm, sums2 = _mlp_fwd_bwd(x2, mixed, target, vecs2, w_out_full, w1_chunks,
                                                           w2_chunks, tm)

    g_out_stack = _grad_matmul(mixed, dm, "grad_w_out", D_MODEL, True)
    g_ff1_stack = _grad_matmul(u2, dh, "grad_w_ff1", D_FF // N_CHIP, False)
    g_ff2_stack = _grad_matmul(act, df, "grad_w_ff2", D_MODEL, True)
    dproj, d_ret_norm, d_gla_norm, d_wg_p, d_bg_p, r_out, r_ff1, r_ff2 = _mixer_bwd(
        proj, dmixed, rsave, ssave, tables, wg_p, bg_p, ret_norm_w, gla_norm_w,
        [g_out_stack, g_ff1_stack, g_ff2_stack])
    g_in_t = _grad_matmul_full(dproj, u, "grad_w_in", N_PROJ // 3)
    g_in_stack = _unpad_w_in_rows(g_in_t).reshape(N_CHIP, in_cols, D_MODEL)
    grad_x, sums1, r_in = _inproj_bwd(dproj, x2, dxa, vecs1, w_in_pt, tm_in, [g_in_stack])

    dmod = jnp.concatenate([sums1[0:1], sums1[1:2], sums2[S_GATE1:S_GATE1 + 1], sums2[S_SHIFT2:S_SHIFT2 + 1],
                            sums2[S_SCALE2:S_SCALE2 + 1], sums2[S_GATE2:S_GATE2 + 1]], axis=1)
    d_gate_w_full = _unpad_heads(d_wg_p[:GATE_RANK])
    flat = lambda parts: jnp.concatenate([_rows8(p) for p in parts], axis=0)
    small = flat([dmod, sums2[S_LN1W:S_LN1W + 1], sums2[S_LN1B:S_LN1B + 1], sums2[S_LN2W:S_LN2W + 1],
                  sums2[S_LN2B:S_LN2B + 1], d_ret_norm, _unpad_heads(d_bg_p), d_gla_norm, d_gate_w_full,
                  sums2[S_LOSS:S_LOSS + 1]])
    g2 = _gather_rows(small, "gather_small")
    tot = _sum_devices(g2)
    loss = 0.5 / D_MODEL * jnp.sum(tot[136:144])
    grad_b_ada = tot[0:48].reshape(1, 6 * D_MODEL)
    grad_ln1_w, grad_ln1_b = tot[48:56].reshape(1, D_MODEL), tot[56:64].reshape(1, D_MODEL)
    grad_ln2_w, grad_ln2_b = tot[64:72].reshape(1, D_MODEL), tot[72:80].reshape(1, D_MODEL)
    grad_ret_norm = tot[80:84].reshape(1, 512)
    grad_gate_b = tot[88:90].reshape(1, 256)
    grad_gla_norm = tot[96:100].reshape(1, 512)
    grad_gate_w = lax.dynamic_slice(tot[104:136].reshape(GATE_RANK, 256), (0, chip * gate_cols),
                                    (GATE_RANK, gate_cols))

    small_w = flat([b_ada, ln1_w, ln1_b, ln2_w, ln2_b, ret_norm_w, gla_gate_b, gla_norm_w, gla_gate_w[0]])
    small_g = flat([grad_b_ada, grad_ln1_w, grad_ln1_b, grad_ln2_w, grad_ln2_b, grad_ret_norm, grad_gate_b,
                    grad_gla_norm, grad_gate_w])
    small_m = flat([m_b_ada, m_ln1_w, m_ln1_b, m_ln2_w, m_ln2_b, m_ret_norm_w, m_gla_gate_b, m_gla_norm_w,
                    m_gla_gate_w[0]])
    small_v = flat([v_b_ada, v_ln1_w, v_ln1_b, v_ln2_w, v_ln2_b, v_ret_norm_w, v_gla_gate_b, v_gla_norm_w,
                    v_gla_gate_w[0]])
    small_out = _adam_small(small_w, small_g, small_m, small_v)

    def unflat(t):
        pieces, row = [], 0
        for shape in [(1, 6 * D_MODEL)] + [(1, D_MODEL)] * 4 + [(1, 512), (1, 256), (1, 512), (1, GATE_RANK, gate_cols)]:
            n = int(np.prod(shape)) // 128
            pieces.append(t[row:row + n].reshape(shape))
            row += -(-n // 8) * 8
        return pieces

    sm_delta, sm_m, sm_v = [unflat(t) for t in small_out]

    dmod_all = g2[:, 0:48].reshape(N_DEV, 6 * D_MODEL)
    dmod_blk = lax.dynamic_slice(dmod_all, (0, chip * ada_cols), (N_DEV, ada_cols))
    ada_out = _ada_bwd_adam(jnp.transpose(c_all), dmod_blk, w_ada[0], m_w_ada[0], v_w_ada[0])
    ada_g, ada_delta, ada_m, ada_v = [t[None] for t in ada_out]

    received = [r_in, r_out, r_ff1, r_ff2]
    names = ["w_in", "w_out", "w_ff1", "w_ff2"]
    partial = [_sum_chips(r, "sum_" + n) for r, n in zip(received, names)]
    swapped = _sibling_swap(partial, "swap_partials")
    big = {}
    for n, w, mine, theirs, m, v in zip(names, [w_in, w_out, w_ff1, w_ff2], partial, swapped,
                                        [m_w_in, m_w_out, m_w_ff1, m_w_ff2], [v_w_in, v_w_out, v_w_ff1, v_w_ff2]):
        if n == "w_in":
            out = _adam_pair(jnp.transpose(w[0]), mine, theirs, jnp.transpose(m[0]), jnp.transpose(v[0]), "adam_" + n)
            big[n] = [jnp.transpose(t)[None] for t in out]
        else:
            big[n] = [t[None] for t in _adam_pair(w[0], mine, theirs, m[0], v[0], "adam_" + n)]

    def assemble(ada, smalls, k):
        b_ada_o, ln1w_o, ln1b_o, ln2w_o, ln2b_o, ret_o, gb_o, gln_o, gw_o = smalls
        return [ada, b_ada_o, big["w_in"][k], ret_o, gw_o, gb_o, gln_o, big["w_out"][k], ln1w_o, ln1b_o,
                big["w_ff1"][k], big["w_ff2"][k], ln2w_o, ln2b_o]

    small_grads = [grad_b_ada, grad_ln1_w, grad_ln1_b, grad_ln2_w, grad_ln2_b, grad_ret_norm, grad_gate_b,
                   grad_gla_norm, grad_gate_w[None]]
    grads = assemble(ada_g, small_grads, 0)
    deltas = assemble(ada_delta, sm_delta, 1)
    new_m = assemble(ada_m, sm_m, 2)
    new_v = assemble(ada_v, sm_v, 3)
    return (loss, grad_x[None], *grads, *deltas, *new_m, *new_v)
```

```python
import functools

import numpy as np
import jax
import jax.numpy as jnp
from jax import lax
from jax.experimental import pallas as pl
from jax.experimental.pallas import tpu as pltpu

D_MODEL = 1024
D_FF = 4096
CHUNK = 64
N_HEADS = 4
HEAD_W = 128
GLA_DK = 64
GATE_RANK = 16
GATE_TAU = 16.0
LN_EPS = 1e-5
ALPHA = 2.0 ** 0.25
ROPE_BASE = 10000.0
RET_SCALE = float(HEAD_W) ** -0.5
GLA_SCALE = float(GLA_DK) ** -0.5

ADAM_LR = 0.001
ADAM_B1 = 0.9
ADAM_B2 = 0.999
ADAM_EPS = 1e-08
ADAM_WD = 0.01
ADAM_STEP = 10

OFF_RQ, OFF_RK, OFF_RV, OFF_RG = 0, 512, 1024, 1536
OFF_GQ, OFF_GK, OFF_GV, OFF_GG, OFF_LR = 2048, 2560, 3072, 3584, 4096
N_PROJ = 4224
N_PROJ_SRC = 3600

N_DEV = 8
N_CHIP = 4
MESH = pl.DeviceIdType.MESH
MXU_DTYPE = jnp.bfloat16
WIRE_DTYPE = jnp.bfloat16
VMEM_LIMIT = 60 * 1024 * 1024
TOKEN_TILE = 256
INPROJ_TOKEN_TILE = 512
CHUNKS_PER_STEP = 4
GRAD_TOKEN_TILE = 2048
ELEMENTWISE_COLS = 256
HIGHEST = lax.Precision.HIGHEST


def _mm(a, b):
    return jnp.dot(a.astype(MXU_DTYPE), b.astype(MXU_DTYPE), preferred_element_type=jnp.float32)


def _mm_nt(a, b):
    return lax.dot_general(a.astype(MXU_DTYPE), b.astype(MXU_DTYPE), (((1,), (1,)), ((), ())),
                           preferred_element_type=jnp.float32)


def _mm_tn(a, b):
    return lax.dot_general(a.astype(MXU_DTYPE), b.astype(MXU_DTYPE), (((0,), (0,)), ((), ())),
                           preferred_element_type=jnp.float32)


def _mm32(a, b):
    return jnp.dot(a, b, precision=HIGHEST, preferred_element_type=jnp.float32)


def _running_sum(mask, a):
    m = mask.astype(jnp.bfloat16)
    hi = a.astype(jnp.bfloat16)
    rest = a - hi.astype(jnp.float32)
    mid = rest.astype(jnp.bfloat16)
    lo = (rest - mid.astype(jnp.float32)).astype(jnp.bfloat16)
    dot = lambda t: jnp.dot(m, t, preferred_element_type=jnp.float32)
    return dot(hi) + dot(mid) + dot(lo)


def _rowmean(a):
    return jnp.mean(a, axis=-1, keepdims=True)


def _colsum(a):
    return jnp.sum(a, axis=0, keepdims=True)


def _ln(z):
    zc = z - _rowmean(z)
    rstd = lax.rsqrt(_rowmean(zc * zc) + LN_EPS)
    return zc * rstd, rstd


def _ln_bwd(dzh, zh, rstd):
    return rstd * (dzh - _rowmean(dzh) - zh * _rowmean(dzh * zh))


def _sigmoid(a):
    return 1.0 / (1.0 + jnp.exp(-a))


def _log_sigmoid(a):
    return jnp.minimum(a, 0.0) - jnp.log(1.0 + jnp.exp(-jnp.abs(a)))


def _swap_halves(a):
    return pltpu.roll(a, HEAD_W // 2, 1)


def _tri_masks():
    row = lax.broadcasted_iota(jnp.int32, (CHUNK, CHUNK), 0)
    col = lax.broadcasted_iota(jnp.int32, (CHUNK, CHUNK), 1)
    return row, col


def _const_spec(shape):
    zeros = (0,) * len(shape)
    return pl.BlockSpec(shape, lambda *_: zeros, pipeline_mode=pl.Buffered(1))


def _params(semantics):
    return pltpu.CompilerParams(dimension_semantics=semantics, vmem_limit_bytes=VMEM_LIMIT)


def _decay_tables():
    log_gamma = np.log(1.0 - 2.0 ** (-5.0 - np.arange(N_HEADS, dtype=np.float64)))
    idx = np.arange(CHUNK, dtype=np.float64)
    dist = np.abs(idx[:, None] - idx[None, :])
    intra = np.exp(log_gamma[:, None, None] * dist)
    kdec = np.exp(log_gamma[None, :] * (CHUNK - 1.0 - idx)[:, None])
    qdec = np.exp(log_gamma[None, :] * (idx + 1.0)[:, None])
    chunk_decay = np.exp(log_gamma * CHUNK)
    lanes = lambda t: np.repeat(t, HEAD_W, axis=1).astype(np.float32)
    return (jnp.asarray(intra.astype(np.float32)), jnp.asarray(lanes(qdec)), jnp.asarray(lanes(kdec)),
            [float(np.float32(v)) for v in chunk_decay])


def _rotary_tables(seq):
    half = HEAD_W // 2
    inv = 1.0 / (ROPE_BASE ** jnp.linspace(0.0, 1.0, half, dtype=jnp.float32))
    both = lambda t: jnp.concatenate([t, t], axis=-1)
    ang_a = jnp.arange(0, seq, CHUNK, dtype=jnp.float32)[:, None] * inv[None, :]
    rot_a = jnp.stack([both(jnp.cos(ang_a)), both(jnp.sin(ang_a))], axis=1)
    rot_a = jnp.pad(rot_a, ((0, 0), (0, 6), (0, 0)))
    ang_b = jnp.arange(CHUNK, dtype=jnp.float32)[:, None] * inv[None, :]
    cos_b, sin_b = both(jnp.cos(ang_b)), both(jnp.sin(ang_b))
    sign = jnp.concatenate([-jnp.ones((half,), jnp.float32), jnp.ones((half,), jnp.float32)])
    return rot_a, jnp.stack([cos_b, sin_b, cos_b * sign, sin_b * sign])


def _rotary_chunk(ra_ref, c, rb_ref):
    cos_a, sin_a = ra_ref[c, 0:1, :], ra_ref[c, 1:2, :]
    return cos_a * rb_ref[0] - sin_a * rb_ref[1], sin_a * rb_ref[2] + cos_a * rb_ref[3]


def _mesh_pos():
    return lax.axis_index("x"), lax.axis_index("y"), lax.axis_index("c")


def _flip(v, bit):
    return 1 - v if bit else v


def _gather_rows(v, name):
    rows = v.shape[0]

    def body(v_ref, out_ref, send_sems, recv_sems):
        x, y, c = _mesh_pos()
        me = 4 * x + 2 * y + c
        out_ref[me] = v_ref[...]
        sends, recvs = [], []
        for k in range(1, N_DEV):
            px, py, pc = _flip(x, (k >> 2) & 1), _flip(y, (k >> 1) & 1), _flip(c, k & 1)
            peer = 4 * px + 2 * py + pc
            sends.append(pltpu.make_async_remote_copy(
                src_ref=v_ref, dst_ref=out_ref.at[me], send_sem=send_sems.at[k - 1], recv_sem=recv_sems.at[k - 1],
                device_id=(px, py, pc), device_id_type=MESH))
            recvs.append(pltpu.make_async_remote_copy(
                src_ref=v_ref, dst_ref=out_ref.at[peer], send_sem=send_sems.at[k - 1], recv_sem=recv_sems.at[k - 1],
                device_id=(px, py, pc), device_id_type=MESH))
        for cp in sends:
            cp.start()
        for cp in recvs:
            cp.wait_recv()
        for cp in sends:
            cp.wait_send()

    return pl.pallas_call(
        body, name=name,
        out_shape=jax.ShapeDtypeStruct((N_DEV, rows, 128), jnp.float32),
        in_specs=[pl.BlockSpec(memory_space=pltpu.VMEM)],
        out_specs=pl.BlockSpec(memory_space=pltpu.VMEM),
        scratch_shapes=[pltpu.SemaphoreType.DMA((N_DEV - 1,)), pltpu.SemaphoreType.DMA((N_DEV - 1,))],
    )(v)


def _chip_gather(arrays, name):
    n = len(arrays)

    def body(*refs):
        gather = _ChipGather(refs[:n], refs[n:2 * n], refs[2 * n:])
        gather.start()
        gather.forward()
        gather.finish()

    return pl.pallas_call(
        body, name=name,
        out_shape=_exchange_out_shapes(arrays, True),
        in_specs=[pl.BlockSpec(memory_space=pl.ANY)] * n,
        out_specs=tuple(pl.BlockSpec(memory_space=pl.ANY) for _ in arrays),
        scratch_shapes=_gather_sems(n),
    )(*arrays)


def _exchange_out_shapes(arrays, gather):
    return tuple(jax.ShapeDtypeStruct((N_CHIP,) + a.shape if gather else a.shape, a.dtype) for a in arrays)


def _scatter_sems(n):
    n_sem = n * (N_CHIP - 1)
    return [pltpu.SemaphoreType.DMA((n_sem,)), pltpu.SemaphoreType.DMA((n_sem,)), pltpu.SemaphoreType.DMA((n,))]


def _gather_sems(n):
    n_sem = n * (N_CHIP - 1)
    return [pltpu.SemaphoreType.DMA((n_sem,))] * 4 + [pltpu.SemaphoreType.DMA((n,))]


def _peer_chips(x, y):
    out = []
    for k in range(1, N_CHIP):
        px, py = _flip(x, (k >> 1) & 1), _flip(y, k & 1)
        out.append((px, py, 2 * px + py))
    return out


class _ChipScatter:
    def __init__(self, ins, outs, sems):
        send_sems, recv_sems, local_sems = sems
        x, y, c = _mesh_pos()
        chip = 2 * x + y
        self.local, self.sends, self.recvs = [], [], []
        for i in range(len(ins)):
            self.local.append(pltpu.make_async_copy(ins[i].at[chip], outs[i].at[chip], local_sems.at[i]))
            for k, (px, py, peer_chip) in enumerate(_peer_chips(x, y)):
                sem = i * (N_CHIP - 1) + k
                src = ins[i].at[peer_chip]
                self.sends.append(pltpu.make_async_remote_copy(
                    src_ref=src, dst_ref=outs[i].at[chip], send_sem=send_sems.at[sem], recv_sem=recv_sems.at[sem],
                    device_id=(px, py, c), device_id_type=MESH))
                self.recvs.append(pltpu.make_async_remote_copy(
                    src_ref=src, dst_ref=outs[i].at[peer_chip], send_sem=send_sems.at[sem], recv_sem=recv_sems.at[sem],
                    device_id=(px, py, c), device_id_type=MESH))

    def start(self):
        for cp in self.local + self.sends:
            cp.start()

    def wait(self):
        for cp in self.recvs:
            cp.wait_recv()
        for cp in self.sends:
            cp.wait_send()
        for cp in self.local:
            cp.wait()


class _ChipGather:
    def __init__(self, ins, outs, sems):
        ici_send, ici_recv, d2d_send, d2d_recv, local_sems = sems
        x, y, c = _mesh_pos()
        chip = 2 * x + y
        self.local, self.ici_sends, self.ici_recvs, self.d2d_sends, self.d2d_recvs = [], [], [], [], []
        for i in range(len(ins)):
            half = ins[i].shape[-1] // 2
            assert half % 128 == 0
            lead = (slice(None),) * (len(ins[i].shape) - 1)
            mine = lead + (pl.ds(pl.multiple_of(c * half, 128), half),)
            theirs = lead + (pl.ds(pl.multiple_of((1 - c) * half, 128), half),)
            self.local.append(pltpu.make_async_copy(ins[i], outs[i].at[chip], local_sems.at[i]))
            for k, (px, py, peer_chip) in enumerate(_peer_chips(x, y)):
                sem = i * (N_CHIP - 1) + k
                self.ici_sends.append(pltpu.make_async_remote_copy(
                    src_ref=ins[i].at[mine], dst_ref=outs[i].at[chip].at[mine],
                    send_sem=ici_send.at[sem], recv_sem=ici_recv.at[sem], device_id=(px, py, c), device_id_type=MESH))
                landed = outs[i].at[peer_chip].at[mine]
                self.ici_recvs.append(pltpu.make_async_remote_copy(
                    src_ref=ins[i].at[mine], dst_ref=landed,
                    send_sem=ici_send.at[sem], recv_sem=ici_recv.at[sem], device_id=(px, py, c), device_id_type=MESH))
                self.d2d_sends.append(pltpu.make_async_remote_copy(
                    src_ref=landed, dst_ref=landed,
                    send_sem=d2d_send.at[sem], recv_sem=d2d_recv.at[sem], device_id=(x, y, 1 - c), device_id_type=MESH))
                self.d2d_recvs.append(pltpu.make_async_remote_copy(
                    src_ref=landed, dst_ref=outs[i].at[peer_chip].at[theirs],
                    send_sem=d2d_send.at[sem], recv_sem=d2d_recv.at[sem], device_id=(x, y, 1 - c), device_id_type=MESH))

    def start(self):
        for cp in self.local + self.ici_sends:
            cp.start()

    def forward(self):
        for landed, onward in zip(self.ici_recvs, self.d2d_sends):
            landed.wait_recv()
            onward.start()

    def finish(self):
        for cp in self.d2d_recvs:
            cp.wait_recv()
        for cp in self.d2d_sends + self.ici_sends:
            cp.wait_send()
        for cp in self.local:
            cp.wait()


def _sibling_swap(arrays, name):
    n = len(arrays)

    def body(*refs):
        swap = _SiblingSwap(refs[:n], refs[n:2 * n], refs[2 * n:])
        swap.start()
        swap.wait()

    return pl.pallas_call(
        body, name=name,
        out_shape=tuple(jax.ShapeDtypeStruct(a.shape, a.dtype) for a in arrays),
        in_specs=[pl.BlockSpec(memory_space=pl.ANY)] * n,
        out_specs=tuple(pl.BlockSpec(memory_space=pl.ANY) for _ in arrays),
        scratch_shapes=_swap_sems(n),
    )(*arrays)


def _swap_sems(n):
    return [pltpu.SemaphoreType.DMA((n,)), pltpu.SemaphoreType.DMA((n,))]


class _SiblingSwap:
    def __init__(self, ins, outs, sems):
        send_sems, recv_sems = sems
        x, y, c = _mesh_pos()
        self.copies = [pltpu.make_async_remote_copy(
            src_ref=ins[i], dst_ref=outs[i], send_sem=send_sems.at[i], recv_sem=recv_sems.at[i],
            device_id=(x, y, 1 - c), device_id_type=MESH) for i in range(len(ins))]

    def start(self):
        for cp in self.copies:
            cp.start()

    def wait(self):
        for cp in self.copies:
            cp.wait_recv()
        for cp in self.copies:
            cp.wait_send()


def _ada_fwd(c_all, w_ada_blk, b_blk):
    cols = w_ada_blk.shape[1]

    def body(c_ref, w_ref, b_ref, out_ref):
        cv = c_ref[...]
        out_ref[...] = _mm32(cv * _sigmoid(cv), w_ref[...]) + b_ref[...]

    return pl.pallas_call(
        body, name="ada_fwd",
        out_shape=jax.ShapeDtypeStruct((N_DEV, cols), jnp.float32),
        compiler_params=pltpu.CompilerParams(vmem_limit_bytes=VMEM_LIMIT),
    )(c_all, w_ada_blk, b_blk)


def _adam(w, g, m, v):
    m2 = ADAM_B1 * m + (1.0 - ADAM_B1) * g
    v2 = ADAM_B2 * v + (1.0 - ADAM_B2) * (g * g)
    m_hat = m2 / (1.0 - ADAM_B1 ** ADAM_STEP)
    v_hat = v2 / (1.0 - ADAM_B2 ** ADAM_STEP)
    delta = -ADAM_LR * (m_hat / (jnp.sqrt(v_hat) + ADAM_EPS) + ADAM_WD * w)
    return delta, m2, v2


def _ada_bwd_adam(c_t, dmod_blk, w, m, v):
    rows, cols = w.shape
    tile = 512
    assert cols % tile == 0

    def body(c_ref, d_ref, w_ref, m_ref, v_ref, g_ref, dl_ref, m2_ref, v2_ref):
        sc = c_ref[...]
        sc = sc * _sigmoid(sc)
        dm = d_ref[...]
        g = sc[:, 0:1] * dm[0:1, :]
        for b in range(1, N_DEV):
            g = g + sc[:, b:b + 1] * dm[b:b + 1, :]
        delta, m2, v2 = _adam(w_ref[...], g, m_ref[...], v_ref[...])
        g_ref[...] = g
        dl_ref[...] = delta
        m2_ref[...] = m2
        v2_ref[...] = v2

    blk = pl.BlockSpec((rows, tile), lambda j: (0, j))
    out = jax.ShapeDtypeStruct((rows, cols), jnp.float32)
    return pl.pallas_call(
        body, name="ada_bwd_adam", grid=(cols // tile,),
        out_shape=(out, out, out, out),
        in_specs=[pl.BlockSpec((rows, N_DEV), lambda j: (0, 0)), pl.BlockSpec((N_DEV, tile), lambda j: (0, j)),
                  blk, blk, blk],
        out_specs=(blk, blk, blk, blk),
        compiler_params=_params(("arbitrary",)),
    )(c_t, dmod_blk, w, m, v)


def _inproj_fwd(x2, vecs, w_in_p, tm, riders):
    seq = x2.shape[0]
    n_tiles = seq // tm
    n_ride = len(riders)

    def body(*refs):
        x_ref, vec_ref, w_ref = refs[:3]
        ride_in, refs = refs[3:3 + n_ride], refs[3 + n_ride:]
        p_ref, u_ref = refs[:2]
        ride_out, sems = refs[2:2 + n_ride], refs[2 + n_ride:]
        gather = _ChipGather(ride_in, ride_out, sems)

        @pl.when(pl.program_id(0) == 0)
        def _():
            gather.start()

        xh, _ = _ln(x_ref[...])
        u = (xh * (1.0 + vec_ref[1:2, :]) + vec_ref[0:1, :]).astype(MXU_DTYPE)
        u_ref[...] = u
        p_ref[...] = _mm(u, w_ref[...])

        @pl.when(pl.program_id(0) == (3 * n_tiles) // 4)
        def _():
            gather.forward()

        @pl.when(pl.program_id(0) == n_tiles - 1)
        def _():
            gather.finish()

    hbm = pl.BlockSpec(memory_space=pl.ANY)
    return pl.pallas_call(
        body, name="inproj_fwd", grid=(n_tiles,),
        out_shape=(jax.ShapeDtypeStruct((seq, N_PROJ), jnp.float32), jax.ShapeDtypeStruct((seq, D_MODEL), MXU_DTYPE))
        + _exchange_out_shapes(riders, True),
        in_specs=[pl.BlockSpec((tm, D_MODEL), lambda i: (i, 0)), _const_spec(vecs.shape), _const_spec(w_in_p.shape)]
        + [hbm] * n_ride,
        out_specs=(pl.BlockSpec((tm, N_PROJ), lambda i: (i, 0)), pl.BlockSpec((tm, D_MODEL), lambda i: (i, 0)))
        + (hbm,) * n_ride,
        scratch_shapes=_gather_sems(n_ride),
        compiler_params=_params(("arbitrary",)),
    )(x2, vecs, w_in_p, *riders)


def _inproj_bwd(dproj, x2, dxa, vecs, w_in_pt, tm, riders):
    seq = x2.shape[0]
    n_tiles = seq // tm
    n_ride = len(riders)

    def body(*refs):
        dp_ref, x_ref, dxa_ref, vec_ref, w_ref = refs[:5]
        ride_in, refs = refs[5:5 + n_ride], refs[5 + n_ride:]
        gx_ref, sums_ref = refs[:2]
        ride_out, sems = refs[2:2 + n_ride], refs[2 + n_ride:]
        exchange = _ChipScatter(ride_in, ride_out, sems)

        @pl.when(pl.program_id(0) == 0)
        def _():
            exchange.start()
            sums_ref[...] = jnp.zeros_like(sums_ref)

        du = _mm(dp_ref[...], w_ref[...])
        xh, rstd = _ln(x_ref[...])
        sums_ref[0:1, :] += _colsum(du)
        sums_ref[1:2, :] += _colsum(du * xh)
        gx_ref[...] = dxa_ref[...] + _ln_bwd(du * (1.0 + vec_ref[1:2, :]), xh, rstd)

        @pl.when(pl.program_id(0) == n_tiles - 1)
        def _():
            exchange.wait()

    tile = pl.BlockSpec((tm, D_MODEL), lambda i: (i, 0))
    hbm = pl.BlockSpec(memory_space=pl.ANY)
    return pl.pallas_call(
        body, name="inproj_bwd", grid=(n_tiles,),
        out_shape=(jax.ShapeDtypeStruct((seq, D_MODEL), jnp.float32), jax.ShapeDtypeStruct((8, D_MODEL), jnp.float32))
        + _exchange_out_shapes(riders, False),
        in_specs=[pl.BlockSpec((tm, N_PROJ), lambda i: (i, 0)), tile, tile, _const_spec(vecs.shape),
                  _const_spec(w_in_pt.shape)] + [hbm] * n_ride,
        out_specs=(tile, pl.BlockSpec((8, D_MODEL), lambda i: (0, 0))) + (hbm,) * n_ride,
        scratch_shapes=_scatter_sems(n_ride),
        compiler_params=_params(("arbitrary",)),
    )(dproj, x2, dxa, vecs, w_in_pt, *riders)


def _head(h):
    return slice(h * HEAD_W, (h + 1) * HEAD_W)


def _cols(ref, off, h):
    return ref[:, off + h * HEAD_W:off + (h + 1) * HEAD_W]


HEADS = range(N_HEADS)


def _mixer_chunk_forward(p_ref, cc, ss, dm_ref, qdec_ref, kdec_ref, wg_ref, bg_ref, ret_state, gla_state_t):
    row, col = _tri_masks()
    lower = row >= col
    f = {}
    f["glr"] = p_ref[:, OFF_LR:OFF_LR + HEAD_W]
    f["logit"] = _mm(f["glr"], wg_ref[...]) + bg_ref[...]
    rq = [_cols(p_ref, OFF_RQ, h) for h in HEADS]
    rk = [_cols(p_ref, OFF_RK, h) for h in HEADS]
    f["rv"] = [_cols(p_ref, OFF_RV, h) for h in HEADS]
    f["qr"] = [(rq[h] * cc + _swap_halves(rq[h]) * ss) * RET_SCALE for h in HEADS]
    f["kr"] = [rk[h] * cc + _swap_halves(rk[h]) * ss for h in HEADS]
    s_raw = [_mm_nt(f["qr"][h], f["kr"][h]) for h in HEADS]
    la = _log_sigmoid(f["logit"]) * (1.0 / GATE_TAU)
    b = _running_sum(lower, la)
    f["qd"] = [f["qr"][h] * qdec_ref[:, _head(h)] for h in HEADS]
    f["kd"] = [f["kr"][h] * kdec_ref[:, _head(h)] for h in HEADS]
    f["scores"] = [s_raw[h] * dm_ref[h] for h in HEADS]
    f["o_ret"] = [_mm(f["scores"][h], f["rv"][h]) + _mm(f["qd"][h], ret_state[h]) for h in HEADS]
    b_last = b[CHUNK - 1:CHUNK, :]
    b_mid = b[CHUNK // 2 - 1:CHUNK // 2, :]
    f["e"], f["ei"] = jnp.exp(b - b_mid), jnp.exp(b_mid - b)
    f["eb"], f["ek"], f["ebl"] = jnp.exp(b), jnp.exp(b_last - b), jnp.exp(b_last)
    gq = [_cols(p_ref, OFF_GQ, h) * GLA_SCALE for h in HEADS]
    gk = [_cols(p_ref, OFF_GK, h) for h in HEADS]
    f["gv"] = [_cols(p_ref, OFF_GV, h) for h in HEADS]
    f["q_e"] = [gq[h] * f["e"][:, _head(h)] for h in HEADS]
    f["q_i"] = [gq[h] * f["ei"][:, _head(h)] for h in HEADS]
    f["k_e"] = [gk[h] * f["e"][:, _head(h)] for h in HEADS]
    f["k_i"] = [gk[h] * f["ei"][:, _head(h)] for h in HEADS]
    low = [_mm_nt(f["q_e"][h], f["k_i"][h]) for h in HEADS]
    up = [_mm_nt(f["q_i"][h], f["k_e"][h]) for h in HEADS]
    f["att"] = [jnp.where(lower, low[h], up[h]) for h in HEADS]
    f["qb"] = [gq[h] * f["eb"][:, _head(h)] for h in HEADS]
    f["kb"] = [gk[h] * f["ek"][:, _head(h)] for h in HEADS]
    f["o_gla"] = [_mm(f["att"][h], f["gv"][h]) + _mm_nt(f["qb"][h], gla_state_t[h]) for h in HEADS]
    return f


def _mixer_fwd(proj, tables, wg_p, bg_p, ret_norm_w, gla_norm_w, riders):
    seq = proj.shape[0]
    n_chunks = seq // CHUNK
    per_step = min(n_chunks, CHUNKS_PER_STEP)
    n_steps = n_chunks // per_step
    n_ride = len(riders)
    rot_a, rot_b, dm_t, qdec_t, kdec_t, chunk_decay = tables

    def body(*refs):
        p_ref, ra_ref, rb_ref, dm_ref, qdec_ref, kdec_ref, wg_ref, bg_ref, wr_ref, wl_ref = refs[:10]
        ride_in, refs = refs[10:10 + n_ride], refs[10 + n_ride:]
        mix_ref, rsave_ref, ssave_ref = refs[:3]
        ride_out, refs = refs[3:3 + n_ride], refs[3 + n_ride:]
        r_sc, s_sc = refs[:2]
        gather = _ChipGather(ride_in, ride_out, refs[2:])

        @pl.when(pl.program_id(0) == 0)
        def _():
            gather.start()
            r_sc[...] = jnp.zeros_like(r_sc)
            s_sc[...] = jnp.zeros_like(s_sc)

        for c in range(per_step):
            p_c = p_ref.at[c * CHUNK:(c + 1) * CHUNK, :]
            mix_c = mix_ref.at[c * CHUNK:(c + 1) * CHUNK, :]
            ret_state = [r_sc[h] for h in HEADS]
            gla_state_t = [s_sc[h] for h in HEADS]
            for h in HEADS:
                rsave_ref[c, h] = ret_state[h]
                ssave_ref[c, h] = gla_state_t[h]
            cc, ss = _rotary_chunk(ra_ref, c, rb_ref)
            f = _mixer_chunk_forward(p_c, cc, ss, dm_ref, qdec_ref, kdec_ref, wg_ref, bg_ref, ret_state, gla_state_t)
            for h in HEADS:
                r_sc[h] = chunk_decay[h] * ret_state[h] + _mm_tn(f["kd"][h], f["rv"][h])
            for h in HEADS:
                s_sc[h] = gla_state_t[h] * f["ebl"][:, _head(h)] + _mm_tn(f["gv"][h], f["kb"][h])
            for h in HEADS:
                on, _ = _ln(f["o_ret"][h])
                g = _cols(p_c, OFF_RG, h)
                mix_c[:, _head(h)] = (on * wr_ref[:, _head(h)] * (g * _sigmoid(g))).astype(mix_ref.dtype)
            for h in HEADS:
                o = f["o_gla"][h]
                on = o * lax.rsqrt(_rowmean(o * o) + LN_EPS)
                g = _cols(p_c, OFF_GG, h)
                mix_c[:, _head(N_HEADS + h)] = (on * wl_ref[:, _head(h)] * (g * _sigmoid(g))).astype(mix_ref.dtype)

        @pl.when(pl.program_id(0) == (3 * n_steps) // 4)
        def _():
            gather.forward()

        @pl.when(pl.program_id(0) == n_steps - 1)
        def _():
            gather.finish()

    state_shape = (n_chunks, N_HEADS, HEAD_W, HEAD_W)
    state_blk = pl.BlockSpec((per_step, N_HEADS, HEAD_W, HEAD_W), lambda i: (i, 0, 0, 0))
    rot_blk = pl.BlockSpec((per_step, 8, HEAD_W), lambda i: (i, 0, 0))
    rows = per_step * CHUNK
    hbm = pl.BlockSpec(memory_space=pl.ANY)
    return pl.pallas_call(
        body, name="mixer_fwd", grid=(n_steps,),
        out_shape=(jax.ShapeDtypeStruct((seq, D_MODEL), MXU_DTYPE),
                   jax.ShapeDtypeStruct(state_shape, jnp.float32), jax.ShapeDtypeStruct(state_shape, jnp.float32))
        + _exchange_out_shapes(riders, True),
        in_specs=[pl.BlockSpec((rows, N_PROJ), lambda i: (i, 0)), rot_blk, _const_spec(rot_b.shape),
                  _const_spec(dm_t.shape), _const_spec(qdec_t.shape), _const_spec(kdec_t.shape),
                  _const_spec(wg_p.shape), _const_spec(bg_p.shape), _const_spec(ret_norm_w.shape),
                  _const_spec(gla_norm_w.shape)] + [hbm] * n_ride,
        out_specs=(pl.BlockSpec((rows, D_MODEL), lambda i: (i, 0)), state_blk, state_blk) + (hbm,) * n_ride,
        scratch_shapes=[pltpu.VMEM((N_HEADS, HEAD_W, HEAD_W), jnp.float32),
                        pltpu.VMEM((N_HEADS, HEAD_W, HEAD_W), jnp.float32)] + _gather_sems(n_ride),
        compiler_params=_params(("arbitrary",)),
    )(proj, rot_a, rot_b, dm_t, qdec_t, kdec_t, wg_p, bg_p, ret_norm_w, gla_norm_w, *riders)


def _mixer_bwd(proj, dmixed, rsave, ssave, tables, wg_p, bg_p, ret_norm_w, gla_norm_w, riders):
    seq = proj.shape[0]
    n_chunks = seq // CHUNK
    per_step = min(n_chunks, CHUNKS_PER_STEP)
    n_steps = n_chunks // per_step
    n_ride = len(riders)
    rot_a, rot_b, dm_t, qdec_t, kdec_t, chunk_decay = tables
    last = n_steps - 1

    def body(*refs):
        p_blk, dmx_blk = refs[:2]
        shared_in = refs[2:13]
        ride_in, refs = refs[13:13 + n_ride], refs[13 + n_ride:]
        dp_blk, dwr_ref, dwl_ref, dwg_ref, dbg_ref = refs[:5]
        ride_out, refs = refs[5:5 + n_ride], refs[5 + n_ride:]
        dr_sc, ds_sc = refs[:2]
        exchange = _ChipScatter(ride_in, ride_out, refs[2:])

        @pl.when(pl.program_id(0) == 0)
        def _():
            exchange.start()
            dr_sc[...] = jnp.zeros_like(dr_sc)
            ds_sc[...] = jnp.zeros_like(ds_sc)
            dwr_ref[...] = jnp.zeros_like(dwr_ref)
            dwl_ref[...] = jnp.zeros_like(dwl_ref)
            dwg_ref[...] = jnp.zeros_like(dwg_ref)
            dbg_ref[...] = jnp.zeros_like(dbg_ref)

        for c in reversed(range(per_step)):
            rows = slice(c * CHUNK, (c + 1) * CHUNK)
            one_chunk(c, p_blk.at[rows, :], dmx_blk.at[rows, :], dp_blk.at[rows, :], *shared_in,
                      dwr_ref, dwl_ref, dwg_ref, dbg_ref, dr_sc, ds_sc)

        @pl.when(pl.program_id(0) == last)
        def _():
            exchange.wait()

    def one_chunk(c, p_ref, dmx_ref, dp_ref, rsave_ref, ssave_ref, ra_ref, rb_ref, dm_ref, qdec_ref, kdec_ref,
                  wg_ref, bg_ref, wr_ref, wl_ref, dwr_ref, dwl_ref, dwg_ref, dbg_ref, dr_sc, ds_sc):
        def put(off, h, val):
            dp_ref[:, off + h * HEAD_W:off + (h + 1) * HEAD_W] = val.astype(dp_ref.dtype)

        cc, ss = _rotary_chunk(ra_ref, c, rb_ref)
        row, col = _tri_masks()
        ret_state = [rsave_ref[c, h] for h in HEADS]
        gla_state_t = [ssave_ref[c, h] for h in HEADS]
        d_ret_new = [dr_sc[h] for h in HEADS]
        d_gla_new = [ds_sc[h] for h in HEADS]
        f = _mixer_chunk_forward(p_ref, cc, ss, dm_ref, qdec_ref, kdec_ref, wg_ref, bg_ref, ret_state, gla_state_t)

        do_ret, do_gla = [], []
        for h in HEADS:
            on, rstd = _ln(f["o_ret"][h])
            g = _cols(p_ref, OFF_RG, h)
            sg = _sigmoid(g)
            dy = dmx_ref[:, _head(h)].astype(jnp.float32)
            wr = wr_ref[:, _head(h)]
            dwr_ref[:, _head(h)] += _colsum(dy * on * (g * sg))
            put(OFF_RG, h, dy * on * wr * (sg * (1.0 + g * (1.0 - sg))))
            do_ret.append(_ln_bwd(dy * wr * (g * sg), on, rstd))
        for h in HEADS:
            o = f["o_gla"][h]
            rstd = lax.rsqrt(_rowmean(o * o) + LN_EPS)
            on = o * rstd
            g = _cols(p_ref, OFF_GG, h)
            sg = _sigmoid(g)
            dy = dmx_ref[:, _head(N_HEADS + h)].astype(jnp.float32)
            wl = wl_ref[:, _head(h)]
            dwl_ref[:, _head(h)] += _colsum(dy * on * (g * sg))
            put(OFF_GG, h, dy * on * wl * (sg * (1.0 + g * (1.0 - sg))))
            don = dy * wl * (g * sg)
            do_gla.append(rstd * (don - on * _rowmean(don * on)))

        ds_raw = [_mm_nt(do_ret[h], f["rv"][h]) * dm_ref[h] for h in HEADS]
        d_att = [_mm_nt(do_gla[h], f["gv"][h]) for h in HEADS]
        dq_state = [_mm_nt(do_ret[h], ret_state[h]) for h in HEADS]
        dk_state = [_mm_nt(f["rv"][h], d_ret_new[h]) for h in HEADS]
        dqb = [_mm(do_gla[h], gla_state_t[h]) for h in HEADS]
        dkb = [_mm(f["gv"][h], d_gla_new[h]) for h in HEADS]
        for h in HEADS:
            put(OFF_RV, h, _mm_tn(f["scores"][h], do_ret[h]) + _mm(f["kd"][h], d_ret_new[h]))
        for h in HEADS:
            put(OFF_GV, h, _mm_tn(f["att"][h], do_gla[h]) + _mm_nt(f["kb"][h], d_gla_new[h]))
        for h in HEADS:
            dr_sc[h] = chunk_decay[h] * d_ret_new[h] + _mm_tn(f["qd"][h], do_ret[h])
        for h in HEADS:
            ds_sc[h] = d_gla_new[h] * f["ebl"][:, _head(h)] + _mm_tn(do_gla[h], f["qb"][h])

        dqr = [_mm(ds_raw[h], f["kr"][h]) + dq_state[h] * qdec_ref[:, _head(h)] for h in HEADS]
        dkr = [_mm_tn(ds_raw[h], f["qr"][h]) + dk_state[h] * kdec_ref[:, _head(h)] for h in HEADS]
        d_low = [jnp.where(row >= col, d_att[h], 0.0) for h in HEADS]
        d_up = [jnp.where(row < col, d_att[h], 0.0) for h in HEADS]
        dq_e = [_mm(d_low[h], f["k_i"][h]) for h in HEADS]
        dk_i = [_mm_tn(d_low[h], f["q_e"][h]) for h in HEADS]
        dq_i = [_mm(d_up[h], f["k_e"][h]) for h in HEADS]
        dk_e = [_mm_tn(d_up[h], f["q_i"][h]) for h in HEADS]
        for h in HEADS:
            put(OFF_RQ, h, (dqr[h] * cc + _swap_halves(dqr[h] * ss)) * RET_SCALE)
            put(OFF_RK, h, dkr[h] * cc + _swap_halves(dkr[h] * ss))
        row_id = lax.broadcasted_iota(jnp.int32, (CHUNK, HEAD_W), 0)
        db_heads = []
        for h in HEADS:
            hs = _head(h)
            e, ei, eb, ek, ebl = f["e"][:, hs], f["ei"][:, hs], f["eb"][:, hs], f["ek"][:, hs], f["ebl"][:, hs]
            put(OFF_GQ, h, (dq_e[h] * e + dq_i[h] * ei + dqb[h] * eb) * GLA_SCALE)
            put(OFF_GK, h, dk_e[h] * e + dk_i[h] * ei + dkb[h] * ek)
            db = (dq_e[h] * f["q_e"][h] - dq_i[h] * f["q_i"][h] + dk_e[h] * f["k_e"][h] - dk_i[h] * f["k_i"][h]
                  + dqb[h] * f["qb"][h] - dkb[h] * f["kb"][h])
            db_last = _colsum(dkb[h] * f["kb"][h]) + ebl * _colsum(gla_state_t[h] * d_gla_new[h])
            db_heads.append(db + jnp.where(row_id == CHUNK - 1, db_last, 0.0))
        db = jnp.concatenate(db_heads, axis=1)
        d_la = _running_sum(col >= row, db)
        d_logit = d_la * (1.0 / GATE_TAU) * (1.0 - _sigmoid(f["logit"]))
        put(OFF_LR, 0, _mm_nt(d_logit, wg_ref[...]))
        dwg_ref[...] += _mm_tn(f["glr"], d_logit)
        dbg_ref[...] += _colsum(d_logit)

    state_blk = pl.BlockSpec((per_step, N_HEADS, HEAD_W, HEAD_W), lambda i: (last - i, 0, 0, 0))
    rot_blk = pl.BlockSpec((per_step, 8, HEAD_W), lambda i: (last - i, 0, 0))
    width = N_HEADS * HEAD_W
    vec_out = pl.BlockSpec((1, width), lambda i: (0, 0))
    hbm = pl.BlockSpec(memory_space=pl.ANY)
    rows_blk = per_step * CHUNK
    return pl.pallas_call(
        body, name="mixer_bwd", grid=(n_steps,),
        out_shape=(jax.ShapeDtypeStruct((seq, N_PROJ), MXU_DTYPE),
                   jax.ShapeDtypeStruct((1, width), jnp.float32), jax.ShapeDtypeStruct((1, width), jnp.float32),
                   jax.ShapeDtypeStruct((HEAD_W, width), jnp.float32), jax.ShapeDtypeStruct((1, width), jnp.float32))
        + _exchange_out_shapes(riders, False),
        in_specs=[pl.BlockSpec((rows_blk, N_PROJ), lambda i: (last - i, 0)),
                  pl.BlockSpec((rows_blk, D_MODEL), lambda i: (last - i, 0)), state_blk, state_blk, rot_blk,
                  _const_spec(rot_b.shape),
                  _const_spec(dm_t.shape), _const_spec(qdec_t.shape), _const_spec(kdec_t.shape),
                  _const_spec(wg_p.shape), _const_spec(bg_p.shape), _const_spec(ret_norm_w.shape),
                  _const_spec(gla_norm_w.shape)] + [hbm] * n_ride,
        out_specs=(pl.BlockSpec((rows_blk, N_PROJ), lambda i: (last - i, 0)), vec_out, vec_out,
                   pl.BlockSpec((HEAD_W, width), lambda i: (0, 0)), vec_out) + (hbm,) * n_ride,
        scratch_shapes=[pltpu.VMEM((N_HEADS, HEAD_W, HEAD_W), jnp.float32),
                        pltpu.VMEM((N_HEADS, HEAD_W, HEAD_W), jnp.float32)] + _scatter_sems(n_ride),
        compiler_params=_params(("arbitrary",)),
    )(proj, dmixed, rsave, ssave, rot_a, rot_b, dm_t, qdec_t, kdec_t, wg_p, bg_p, ret_norm_w, gla_norm_w, *riders)


V_GATE1, V_SCALE2, V_SHIFT2, V_GATE2, V_LN1W, V_LN1B, V_LN2W, V_LN2B = range(8)
S_GATE1, S_SCALE2, S_SHIFT2, S_GATE2, S_LN1W, S_LN1B, S_LN2W, S_LN2B, S_LOSS = range(9)


def _mlp_fwd_bwd(x2, mixed, target, vecs, w_out, w1_chunks, w2_chunks, tm):
    seq = x2.shape[0]
    n_fc, _, fc = w1_chunks.shape

    def body(x_ref, mx_ref, t_ref, vec_ref, wo_ref, w1_ref, w2_ref,
             dmx_ref, dxa_ref, a_ref, dh_ref, u2_ref, df_ref, dm_ref, sums_ref, relu_sc):
        @pl.when(pl.program_id(0) == 0)
        def _():
            sums_ref[...] = jnp.zeros_like(sums_ref)

        vec = lambda r: vec_ref[r:r + 1, :]

        def acc(r, val):
            sums_ref[r:r + 1, :] += _colsum(val)

        xx = x_ref[...]
        m = _mm(mx_ref[...], wo_ref[...])
        z1h, rstd1 = _ln(ALPHA * xx + vec(V_GATE1) * m)
        x1 = z1h * vec(V_LN1W) + vec(V_LN1B)
        x1h, rstd0 = _ln(x1)
        u2 = (x1h * (1.0 + vec(V_SCALE2)) + vec(V_SHIFT2)).astype(MXU_DTYPE)
        u2_ref[...] = u2
        f = jnp.zeros((tm, D_MODEL), jnp.float32)
        for j in range(n_fc):
            r = jnp.maximum(_mm(u2, w1_ref[j]), 0.0)
            relu_sc[:, j * fc:(j + 1) * fc] = r
            a = (r * r).astype(MXU_DTYPE)
            a_ref[:, j * fc:(j + 1) * fc] = a
            f = f + _mm(a, w2_ref[j])
        z2h, rstd2 = _ln(ALPHA * x1 + vec(V_GATE2) * f)
        err = z2h * vec(V_LN2W) + vec(V_LN2B) - t_ref[...]
        acc(S_LOSS, err * err)
        dy = err * (1.0 / D_MODEL)
        acc(S_LN2W, dy * z2h)
        acc(S_LN2B, dy)
        dz2 = _ln_bwd(dy * vec(V_LN2W), z2h, rstd2)
        acc(S_GATE2, dz2 * f)
        df = (vec(V_GATE2) * dz2).astype(MXU_DTYPE)
        df_ref[...] = df
        du2 = jnp.zeros((tm, D_MODEL), jnp.float32)
        for j in range(n_fc):
            dh = (_mm_nt(df, w2_ref[j]) * (2.0 * relu_sc[:, j * fc:(j + 1) * fc])).astype(MXU_DTYPE)
            dh_ref[:, j * fc:(j + 1) * fc] = dh
            du2 = du2 + _mm_nt(dh, w1_ref[j])
        acc(S_SCALE2, du2 * x1h)
        acc(S_SHIFT2, du2)
        dx1 = ALPHA * dz2 + _ln_bwd(du2 * (1.0 + vec(V_SCALE2)), x1h, rstd0)
        acc(S_LN1W, dx1 * z1h)
        acc(S_LN1B, dx1)
        dz1 = _ln_bwd(dx1 * vec(V_LN1W), z1h, rstd1)
        acc(S_GATE1, dz1 * m)
        dxa_ref[...] = ALPHA * dz1
        dm = (vec(V_GATE1) * dz1).astype(MXU_DTYPE)
        dm_ref[...] = dm
        dmx_ref[...] = _mm_nt(dm, wo_ref[...])

    tile = lambda width: pl.BlockSpec((tm, width), lambda i: (i, 0))
    f32 = lambda width: jax.ShapeDtypeStruct((seq, width), jnp.float32)
    b16 = lambda width: jax.ShapeDtypeStruct((seq, width), MXU_DTYPE)
    return pl.pallas_call(
        body, name="mlp_fwd_bwd", grid=(seq // tm,),
        out_shape=(f32(D_MODEL), f32(D_MODEL), b16(D_FF), b16(D_FF), b16(D_MODEL), b16(D_MODEL), b16(D_MODEL),
                   jax.ShapeDtypeStruct((16, D_MODEL), jnp.float32)),
        in_specs=[tile(D_MODEL), tile(D_MODEL), tile(D_MODEL), _const_spec(vecs.shape), _const_spec(w_out.shape),
                  _const_spec(w1_chunks.shape), _const_spec(w2_chunks.shape)],
        out_specs=(tile(D_MODEL), tile(D_MODEL), tile(D_FF), tile(D_FF), tile(D_MODEL), tile(D_MODEL),
                   tile(D_MODEL), pl.BlockSpec((16, D_MODEL), lambda i: (0, 0))),
        scratch_shapes=[pltpu.VMEM((tm, D_FF), jnp.float32)],
        compiler_params=_params(("arbitrary",)),
    )(x2, mixed, target, vecs, w_out, w1_chunks, w2_chunks)


def _grad_matmul(a, b, name, tn, blocks_are_rows):
    seq, m_dim = a.shape
    n_dim = b.shape[1]
    tk = min(seq, GRAD_TOKEN_TILE)
    nk = seq // tk
    if blocks_are_rows:
        tm = m_dim // N_CHIP
        assert tn == n_dim
        grid = (N_CHIP, 1, nk)
        out_map = lambda i, j, k: (i, 0, 0)
    else:
        tm = m_dim
        assert tn * N_CHIP == n_dim
        grid = (1, N_CHIP, nk)
        out_map = lambda i, j, k: (j, 0, 0)

    def body(a_ref, b_ref, o_ref, acc_sc):
        k = pl.program_id(2)

        @pl.when(k == 0)
        def _():
            acc_sc[...] = jnp.zeros_like(acc_sc)

        acc_sc[...] += _mm_tn(a_ref[...], b_ref[...])

        @pl.when(k == nk - 1)
        def _():
            o_ref[0] = acc_sc[...].astype(o_ref.dtype)

    return pl.pallas_call(
        body, name=name, grid=grid,
        out_shape=jax.ShapeDtypeStruct((N_CHIP, tm, tn), WIRE_DTYPE),
        in_specs=[pl.BlockSpec((tk, tm), lambda i, j, k: (k, i)), pl.BlockSpec((tk, tn), lambda i, j, k: (k, j))],
        out_specs=pl.BlockSpec((1, tm, tn), out_map),
        scratch_shapes=[pltpu.VMEM((tm, tn), jnp.float32)],
        compiler_params=_params(("arbitrary", "arbitrary", "arbitrary")),
    )(a, b)


def _grad_matmul_full(a, b, name, tm, riders):
    seq, m_dim = a.shape
    n_dim = b.shape[1]
    tk = min(seq, GRAD_TOKEN_TILE)
    nk = seq // tk
    n_blocks = m_dim // tm
    n_ride = len(riders)
    assert m_dim % tm == 0

    def body(*refs):
        a_ref, b_ref = refs[:2]
        ride_in, refs = refs[2:2 + n_ride], refs[2 + n_ride:]
        o_ref = refs[0]
        ride_out, refs = refs[1:1 + n_ride], refs[1 + n_ride:]
        acc_sc = refs[0]
        swap = _SiblingSwap(ride_in, ride_out, refs[1:])
        i, k = pl.program_id(0), pl.program_id(1)

        @pl.when((i == 0) & (k == 0))
        def _():
            swap.start()

        @pl.when(k == 0)
        def _():
            acc_sc[...] = jnp.zeros_like(acc_sc)

        acc_sc[...] += _mm_tn(a_ref[...], b_ref[...])

        @pl.when(k == nk - 1)
        def _():
            o_ref[...] = acc_sc[...].astype(o_ref.dtype)

        @pl.when((i == n_blocks - 1) & (k == nk - 1))
        def _():
            swap.wait()

    hbm = pl.BlockSpec(memory_space=pl.ANY)
    return pl.pallas_call(
        body, name=name, grid=(n_blocks, nk),
        out_shape=(jax.ShapeDtypeStruct((m_dim, n_dim), WIRE_DTYPE),)
        + tuple(jax.ShapeDtypeStruct(r.shape, r.dtype) for r in riders),
        in_specs=[pl.BlockSpec((tk, tm), lambda i, k: (k, i)), pl.BlockSpec((tk, n_dim), lambda i, k: (k, 0))]
        + [hbm] * n_ride,
        out_specs=(pl.BlockSpec((tm, n_dim), lambda i, k: (i, 0)),) + (hbm,) * n_ride,
        scratch_shapes=[pltpu.VMEM((tm, n_dim), jnp.float32)] + _swap_sems(n_ride),
        compiler_params=_params(("arbitrary", "arbitrary")),
    )(a, b, *riders)


def _sum_chips(stack, name):
    _, rows, cols = stack.shape
    tc = min(cols, ELEMENTWISE_COLS)

    def body(s_ref, o_ref):
        total = s_ref[0].astype(jnp.float32)
        for j in range(1, N_CHIP):
            total = total + s_ref[j].astype(jnp.float32)
        o_ref[...] = total

    return pl.pallas_call(
        body, name=name, grid=(cols // tc,),
        out_shape=jax.ShapeDtypeStruct((rows, cols), jnp.float32),
        in_specs=[pl.BlockSpec((N_CHIP, rows, tc), lambda i: (0, 0, i))],
        out_specs=pl.BlockSpec((rows, tc), lambda i: (0, i)),
        compiler_params=_params(("arbitrary",)),
    )(stack)


def _adam_pair(w, g_mine, g_sibling, m, v, name):
    rows, cols = w.shape
    tc = min(cols, ELEMENTWISE_COLS)

    def body(w_ref, ga_ref, gb_ref, m_ref, v_ref, g_ref, dl_ref, m2_ref, v2_ref):
        g = ga_ref[...] + gb_ref[...]
        delta, m2, v2 = _adam(w_ref[...], g, m_ref[...], v_ref[...])
        g_ref[...] = g
        dl_ref[...] = delta
        m2_ref[...] = m2
        v2_ref[...] = v2

    blk = pl.BlockSpec((rows, tc), lambda i: (0, i))
    out = jax.ShapeDtypeStruct((rows, cols), jnp.float32)
    return pl.pallas_call(
        body, name=name, grid=(cols // tc,),
        out_shape=(out, out, out, out),
        in_specs=[blk] * 5, out_specs=(blk,) * 4,
        compiler_params=_params(("arbitrary",)),
    )(w, g_mine, g_sibling, m, v)


def _sum_devices(gathered):
    _, rows, _ = gathered.shape

    def body(g_ref, o_ref):
        total = g_ref[0]
        for d in range(1, N_DEV):
            total = total + g_ref[d]
        o_ref[...] = total

    return pl.pallas_call(
        body, name="sum_devices",
        out_shape=jax.ShapeDtypeStruct((rows, 128), jnp.float32),
    )(gathered)


def _adam_small(params):
    n = len(params)

    def body(*refs):
        ins, outs = refs[:4 * n], refs[4 * n:]
        for i in range(n):
            w_ref, g_ref, m_ref, v_ref = ins[4 * i:4 * i + 4]
            delta, m2, v2 = _adam(w_ref[...], g_ref[...], m_ref[...], v_ref[...])
            outs[3 * i][...] = delta
            outs[3 * i + 1][...] = m2
            outs[3 * i + 2][...] = v2

    out_shape = tuple(jax.ShapeDtypeStruct(p[0].shape, jnp.float32) for p in params for _ in range(3))
    out = pl.pallas_call(body, name="adam_small", out_shape=out_shape)(*[t for p in params for t in p])
    return [out[3 * i:3 * i + 3] for i in range(n)]


def _pad_heads(w):
    lead = w.shape[:-1]
    w = w.reshape(lead + (N_HEADS, GLA_DK))
    w = jnp.pad(w, [(0, 0)] * len(lead) + [(0, 0), (0, HEAD_W - GLA_DK)])
    return w.reshape(lead + (N_HEADS * HEAD_W,))


def _unpad_heads(w):
    lead = w.shape[:-1]
    return w.reshape(lead + (N_HEADS, HEAD_W))[..., :GLA_DK].reshape(lead + (N_HEADS * GLA_DK,))


def _pad_head_rows(w):
    w = w.reshape(N_HEADS, GLA_DK, w.shape[-1])
    return jnp.pad(w, ((0, 0), (0, HEAD_W - GLA_DK), (0, 0))).reshape(N_HEADS * HEAD_W, w.shape[-1])


def _unpad_head_rows(w):
    return w.reshape(N_HEADS, HEAD_W, w.shape[-1])[:, :GLA_DK].reshape(N_HEADS * GLA_DK, w.shape[-1])


def _pad_w_in_rows(w):
    return jnp.concatenate([
        w[:2048], _pad_head_rows(w[2048:2304]), _pad_head_rows(w[2304:2560]), w[2560:3584],
        jnp.pad(w[3584:3600], ((0, HEAD_W - GATE_RANK), (0, 0)))], axis=0)


def _unpad_w_in_rows(g):
    return jnp.concatenate([
        g[:2048], _unpad_head_rows(g[OFF_GQ:OFF_GQ + 512]), _unpad_head_rows(g[OFF_GK:OFF_GK + 512]),
        g[OFF_GV:OFF_LR], g[OFF_LR:OFF_LR + GATE_RANK]], axis=0)


def _rows128(a):
    return a.reshape(-1, 128)


def _rows8(a):
    a = a.reshape(-1, 128)
    return jnp.pad(a, ((0, -a.shape[0] % 8), (0, 0)))


def kernel(x, c, w_ada, b_ada, w_in, ret_norm_w, gla_gate_w, gla_gate_b, gla_norm_w, w_out, ln1_w, ln1_b, w_ff1, w_ff2, ln2_w, ln2_b, loss_target, m_w_ada, m_b_ada, m_w_in, m_ret_norm_w, m_gla_gate_w, m_gla_gate_b, m_gla_norm_w, m_w_out, m_ln1_w, m_ln1_b, m_w_ff1, m_w_ff2, m_ln2_w, m_ln2_b, v_w_ada, v_b_ada, v_w_in, v_ret_norm_w, v_gla_gate_w, v_gla_gate_b, v_gla_norm_w, v_w_out, v_ln1_w, v_ln1_b, v_w_ff1, v_w_ff2, v_ln2_w, v_ln2_b):
    seq = x.shape[1]
    tm = min(seq, TOKEN_TILE)
    tm_in = min(seq, INPROJ_TOKEN_TILE)
    xi, yi, ci = _mesh_pos()
    dev = 4 * xi + 2 * yi + ci
    chip = 2 * xi + yi
    x2, target = x[0], loss_target[0]
    ada_cols = w_ada.shape[2]
    in_cols = w_in.shape[2]
    gate_cols = gla_gate_w.shape[2]

    g0 = _gather_rows(jnp.concatenate([_rows128(c), _rows128(gla_gate_w[0])], axis=0), "gather_cond")
    c_all = g0[:, :8].reshape(N_DEV, D_MODEL)
    gate_w_full = jnp.concatenate([g0[2 * j, 8:16].reshape(GATE_RANK, gate_cols) for j in range(N_CHIP)], axis=1)
    wg_p = jnp.pad(_pad_heads(gate_w_full), ((0, HEAD_W - GATE_RANK), (0, 0)))
    bg_p = _pad_heads(gla_gate_b)

    b_blk = lax.dynamic_slice(b_ada, (0, chip * ada_cols), (1, ada_cols))
    mod_blk = _ada_fwd(c_all, w_ada[0], b_blk)
    g1 = _gather_rows(_rows128(mod_blk), "gather_mod")
    mod_all = jnp.concatenate([g1[2 * j].reshape(N_DEV, ada_cols) for j in range(N_CHIP)], axis=1)
    mod = lax.dynamic_slice(mod_all, (dev, 0), (1, 6 * D_MODEL))
    shift1, scale1, gate1, shift2, scale2, gate2 = [mod[:, i * D_MODEL:(i + 1) * D_MODEL] for i in range(6)]

    (w_in_stack,) = _chip_gather([jnp.transpose(w_in[0]).astype(WIRE_DTYPE)], "gather_w_in")
    w_in_pt = _pad_w_in_rows(w_in_stack.reshape(N_PROJ_SRC, D_MODEL)).astype(MXU_DTYPE)
    w_in_p = jnp.transpose(w_in_pt)

    zeros_row = jnp.zeros((1, D_MODEL), jnp.float32)
    vecs1 = jnp.concatenate([shift1, scale1] + [zeros_row] * 6, axis=0)
    proj, u, w2_stack = _inproj_fwd(x2, vecs1, w_in_p, tm_in, [w_ff2[0].astype(WIRE_DTYPE)])
    rot_a, rot_b = _rotary_tables(seq)
    dm_t, qdec_t, kdec_t, chunk_decay = _decay_tables()
    tables = (rot_a, rot_b, dm_t, qdec_t, kdec_t, chunk_decay)
    mixed, rsave, ssave, w_out_stack, w1_stack = _mixer_fwd(
        proj, tables, wg_p, bg_p, ret_norm_w, gla_norm_w,
        [w_out[0].astype(WIRE_DTYPE), w_ff1[0].astype(WIRE_DTYPE)])
    w_out_full = w_out_stack.reshape(D_MODEL, D_MODEL).astype(MXU_DTYPE)
    w1_chunks = w1_stack.astype(MXU_DTYPE)
    w2_chunks = w2_stack.astype(MXU_DTYPE)

    vecs2 = jnp.concatenate([gate1, scale2, shift2, gate2, ln1_w, ln1_b, ln2_w, ln2_b], axis=0)
    dmixed, dxa, act, dh, u2, df, dm, sums2 = _mlp_fwd_bwd(x2, mixed, target, vecs2, w_out_full, w1_chunks,
                                                           w2_chunks, tm)

    g_out_stack = _grad_matmul(mixed, dm, "grad_w_out", D_MODEL, True)
    g_ff1_stack = _grad_matmul(u2, dh, "grad_w_ff1", D_FF // N_CHIP, False)
    g_ff2_stack = _grad_matmul(act, df, "grad_w_ff2", D_MODEL, True)
    dproj, d_ret_norm, d_gla_norm, d_wg_p, d_bg_p, r_out, r_ff1, r_ff2 = _mixer_bwd(
        proj, dmixed, rsave, ssave, tables, wg_p, bg_p, ret_norm_w, gla_norm_w,
        [g_out_stack, g_ff1_stack, g_ff2_stack])
    early = ["w_out", "w_ff1", "w_ff2"]
    partial = {n: _sum_chips(r, "sum_" + n) for n, r in zip(early, [r_out, r_ff1, r_ff2])}
    g_in_t, *swapped_early = _grad_matmul_full(dproj, u, "grad_w_in", N_PROJ // 3, [partial[n] for n in early])
    swapped = dict(zip(early, swapped_early))
    g_in_stack = _unpad_w_in_rows(g_in_t).reshape(N_CHIP, in_cols, D_MODEL)
    grad_x, sums1, r_in = _inproj_bwd(dproj, x2, dxa, vecs1, w_in_pt, tm_in, [g_in_stack])

    dmod = jnp.concatenate([sums1[0:1], sums1[1:2], sums2[S_GATE1:S_GATE1 + 1], sums2[S_SHIFT2:S_SHIFT2 + 1],
                            sums2[S_SCALE2:S_SCALE2 + 1], sums2[S_GATE2:S_GATE2 + 1]], axis=1)
    d_gate_w_full = _unpad_heads(d_wg_p[:GATE_RANK])
    flat = lambda parts: jnp.concatenate([_rows8(p) for p in parts], axis=0)
    small = flat([dmod, sums2[S_LN1W:S_LN1W + 1], sums2[S_LN1B:S_LN1B + 1], sums2[S_LN2W:S_LN2W + 1],
                  sums2[S_LN2B:S_LN2B + 1], d_ret_norm, _unpad_heads(d_bg_p), d_gla_norm, d_gate_w_full,
                  sums2[S_LOSS:S_LOSS + 1]])
    g2 = _gather_rows(small, "gather_small")
    tot = _sum_devices(g2)
    loss = 0.5 / D_MODEL * jnp.sum(tot[136:144])
    grad_b_ada = tot[0:48].reshape(1, 6 * D_MODEL)
    grad_ln1_w, grad_ln1_b = tot[48:56].reshape(1, D_MODEL), tot[56:64].reshape(1, D_MODEL)
    grad_ln2_w, grad_ln2_b = tot[64:72].reshape(1, D_MODEL), tot[72:80].reshape(1, D_MODEL)
    grad_ret_norm = tot[80:84].reshape(1, 512)
    grad_gate_b = tot[88:90].reshape(1, 256)
    grad_gla_norm = tot[96:100].reshape(1, 512)
    grad_gate_w = lax.dynamic_slice(tot[104:136].reshape(GATE_RANK, 256), (0, chip * gate_cols),
                                    (GATE_RANK, gate_cols))

    small_grads = [grad_b_ada, grad_ln1_w, grad_ln1_b, grad_ln2_w, grad_ln2_b, grad_ret_norm, grad_gate_b,
                   grad_gla_norm, grad_gate_w[None]]
    small_out = _adam_small(list(zip(
        [b_ada, ln1_w, ln1_b, ln2_w, ln2_b, ret_norm_w, gla_gate_b, gla_norm_w, gla_gate_w], small_grads,
        [m_b_ada, m_ln1_w, m_ln1_b, m_ln2_w, m_ln2_b, m_ret_norm_w, m_gla_gate_b, m_gla_norm_w, m_gla_gate_w],
        [v_b_ada, v_ln1_w, v_ln1_b, v_ln2_w, v_ln2_b, v_ret_norm_w, v_gla_gate_b, v_gla_norm_w, v_gla_gate_w])))
    sm_delta, sm_m, sm_v = [[o[k] for o in small_out] for k in range(3)]

    dmod_all = g2[:, 0:48].reshape(N_DEV, 6 * D_MODEL)
    dmod_blk = lax.dynamic_slice(dmod_all, (0, chip * ada_cols), (N_DEV, ada_cols))
    ada_out = _ada_bwd_adam(jnp.transpose(c_all), dmod_blk, w_ada[0], m_w_ada[0], v_w_ada[0])
    ada_g, ada_delta, ada_m, ada_v = [t[None] for t in ada_out]

    partial["w_in"] = _sum_chips(r_in, "sum_w_in")
    (swapped["w_in"],) = _sibling_swap([partial["w_in"]], "swap_w_in")
    big = {}
    for n, w, m, v in zip(["w_in", "w_out", "w_ff1", "w_ff2"], [w_in, w_out, w_ff1, w_ff2],
                          [m_w_in, m_w_out, m_w_ff1, m_w_ff2], [v_w_in, v_w_out, v_w_ff1, v_w_ff2]):
        mine, theirs = partial[n], swapped[n]
        if n == "w_in":
            out = _adam_pair(jnp.transpose(w[0]), mine, theirs, jnp.transpose(m[0]), jnp.transpose(v[0]), "adam_" + n)
            big[n] = [jnp.transpose(t)[None] for t in out]
        else:
            big[n] = [t[None] for t in _adam_pair(w[0], mine, theirs, m[0], v[0], "adam_" + n)]

    def assemble(ada, smalls, k):
        b_ada_o, ln1w_o, ln1b_o, ln2w_o, ln2b_o, ret_o, gb_o, gln_o, gw_o = smalls
        return [ada, b_ada_o, big["w_in"][k], ret_o, gw_o, gb_o, gln_o, big["w_out"][k], ln1w_o, ln1b_o,
                big["w_ff1"][k], big["w_ff2"][k], ln2w_o, ln2b_o]

    grads = assemble(ada_g, small_grads, 0)
    deltas = assemble(ada_delta, sm_delta, 1)
    new_m = assemble(ada_m, sm_m, 2)
    new_v = assemble(ada_v, sm_v, 3)
    return (loss, grad_x[None], *grads, *deltas, *new_m, *new_v)
```

```python
import functools

import numpy as np
import jax
import jax.numpy as jnp
from jax import lax
from jax.experimental import pallas as pl
from jax.experimental.pallas import tpu as pltpu

D_MODEL = 1024
D_FF = 4096
CHUNK = 64
N_HEADS = 4
HEAD_W = 128
GLA_DK = 64
GATE_RANK = 16
GATE_TAU = 16.0
LN_EPS = 1e-5
ALPHA = 2.0 ** 0.25
ROPE_BASE = 10000.0
RET_SCALE = float(HEAD_W) ** -0.5
GLA_SCALE = float(GLA_DK) ** -0.5

ADAM_LR = 0.001
ADAM_B1 = 0.9
ADAM_B2 = 0.999
ADAM_EPS = 1e-08
ADAM_WD = 0.01
ADAM_STEP = 10

OFF_RQ, OFF_RK, OFF_RV, OFF_RG = 0, 512, 1024, 1536
OFF_GQ, OFF_GK, OFF_GV, OFF_GG, OFF_LR = 2048, 2560, 3072, 3584, 4096
N_PROJ = 4224
N_PROJ_SRC = 3600

N_DEV = 8
N_CHIP = 4
MESH = pl.DeviceIdType.MESH
MXU_DTYPE = jnp.bfloat16
WIRE_DTYPE = jnp.bfloat16
VMEM_LIMIT = 60 * 1024 * 1024
TOKEN_TILE = 256
INPROJ_TOKEN_TILE = 512
CHUNKS_PER_STEP = 8
CHUNKS_IN_LOCKSTEP = 4
GRAD_TOKEN_TILE = 2048
ELEMENTWISE_COLS = 256
HIGHEST = lax.Precision.HIGHEST


def _mm(a, b):
    return jnp.dot(a.astype(MXU_DTYPE), b.astype(MXU_DTYPE), preferred_element_type=jnp.float32)


def _mm_nt(a, b):
    return lax.dot_general(a.astype(MXU_DTYPE), b.astype(MXU_DTYPE), (((1,), (1,)), ((), ())),
                           preferred_element_type=jnp.float32)


def _mm_tn(a, b):
    return lax.dot_general(a.astype(MXU_DTYPE), b.astype(MXU_DTYPE), (((0,), (0,)), ((), ())),
                           preferred_element_type=jnp.float32)


def _mm32(a, b):
    return jnp.dot(a, b, precision=HIGHEST, preferred_element_type=jnp.float32)


def _running_sum(mask, a):
    m = mask.astype(jnp.bfloat16)
    hi = a.astype(jnp.bfloat16)
    rest = a - hi.astype(jnp.float32)
    mid = rest.astype(jnp.bfloat16)
    lo = (rest - mid.astype(jnp.float32)).astype(jnp.bfloat16)
    dot = lambda t: jnp.dot(m, t, preferred_element_type=jnp.float32)
    return dot(hi) + dot(mid) + dot(lo)


def _rowmean(a):
    return jnp.mean(a, axis=-1, keepdims=True)


def _colsum(a):
    return jnp.sum(a, axis=0, keepdims=True)


def _ln(z):
    zc = z - _rowmean(z)
    rstd = lax.rsqrt(_rowmean(zc * zc) + LN_EPS)
    return zc * rstd, rstd


def _ln_bwd(dzh, zh, rstd):
    return rstd * (dzh - _rowmean(dzh) - zh * _rowmean(dzh * zh))


def _sigmoid(a):
    return 1.0 / (1.0 + jnp.exp(-a))


def _log_sigmoid(a):
    return jnp.minimum(a, 0.0) - jnp.log(1.0 + jnp.exp(-jnp.abs(a)))


def _swap_halves(a):
    return pltpu.roll(a, HEAD_W // 2, 1)


def _tri_masks():
    row = lax.broadcasted_iota(jnp.int32, (CHUNK, CHUNK), 0)
    col = lax.broadcasted_iota(jnp.int32, (CHUNK, CHUNK), 1)
    return row, col


def _const_spec(shape):
    zeros = (0,) * len(shape)
    return pl.BlockSpec(shape, lambda *_: zeros, pipeline_mode=pl.Buffered(1))


def _params(semantics):
    return pltpu.CompilerParams(dimension_semantics=semantics, vmem_limit_bytes=VMEM_LIMIT)


def _decay_tables():
    log_gamma = np.log(1.0 - 2.0 ** (-5.0 - np.arange(N_HEADS, dtype=np.float64)))
    idx = np.arange(CHUNK, dtype=np.float64)
    dist = np.abs(idx[:, None] - idx[None, :])
    intra = np.exp(log_gamma[:, None, None] * dist)
    kdec = np.exp(log_gamma[None, :] * (CHUNK - 1.0 - idx)[:, None])
    qdec = np.exp(log_gamma[None, :] * (idx + 1.0)[:, None])
    chunk_decay = np.exp(log_gamma * CHUNK)
    lanes = lambda t: np.repeat(t, HEAD_W, axis=1).astype(np.float32)
    return (jnp.asarray(intra.astype(np.float32)), jnp.asarray(lanes(qdec)), jnp.asarray(lanes(kdec)),
            [float(np.float32(v)) for v in chunk_decay])


def _rotary_tables(seq):
    half = HEAD_W // 2
    inv = 1.0 / (ROPE_BASE ** jnp.linspace(0.0, 1.0, half, dtype=jnp.float32))
    both = lambda t: jnp.concatenate([t, t], axis=-1)
    ang_a = jnp.arange(0, seq, CHUNK, dtype=jnp.float32)[:, None] * inv[None, :]
    rot_a = jnp.stack([both(jnp.cos(ang_a)), both(jnp.sin(ang_a))], axis=1)
    rot_a = jnp.pad(rot_a, ((0, 0), (0, 6), (0, 0)))
    ang_b = jnp.arange(CHUNK, dtype=jnp.float32)[:, None] * inv[None, :]
    cos_b, sin_b = both(jnp.cos(ang_b)), both(jnp.sin(ang_b))
    sign = jnp.concatenate([-jnp.ones((half,), jnp.float32), jnp.ones((half,), jnp.float32)])
    return rot_a, jnp.stack([cos_b, sin_b, cos_b * sign, sin_b * sign])


def _rotary_chunk(ra_ref, c, rb_ref):
    cos_a, sin_a = ra_ref[c, 0:1, :], ra_ref[c, 1:2, :]
    return cos_a * rb_ref[0] - sin_a * rb_ref[1], sin_a * rb_ref[2] + cos_a * rb_ref[3]


def _mesh_pos():
    return lax.axis_index("x"), lax.axis_index("y"), lax.axis_index("c")


def _flip(v, bit):
    return 1 - v if bit else v


def _gather_rows(v, name):
    rows = v.shape[0]

    def body(v_ref, out_ref, send_sems, recv_sems):
        x, y, c = _mesh_pos()
        me = 4 * x + 2 * y + c
        out_ref[me] = v_ref[...]
        sends, recvs = [], []
        for k in range(1, N_DEV):
            px, py, pc = _flip(x, (k >> 2) & 1), _flip(y, (k >> 1) & 1), _flip(c, k & 1)
            peer = 4 * px + 2 * py + pc
            sends.append(pltpu.make_async_remote_copy(
                src_ref=v_ref, dst_ref=out_ref.at[me], send_sem=send_sems.at[k - 1], recv_sem=recv_sems.at[k - 1],
                device_id=(px, py, pc), device_id_type=MESH))
            recvs.append(pltpu.make_async_remote_copy(
                src_ref=v_ref, dst_ref=out_ref.at[peer], send_sem=send_sems.at[k - 1], recv_sem=recv_sems.at[k - 1],
                device_id=(px, py, pc), device_id_type=MESH))
        for cp in sends:
            cp.start()
        for cp in recvs:
            cp.wait_recv()
        for cp in sends:
            cp.wait_send()

    return pl.pallas_call(
        body, name=name,
        out_shape=jax.ShapeDtypeStruct((N_DEV, rows, 128), jnp.float32),
        in_specs=[pl.BlockSpec(memory_space=pltpu.VMEM)],
        out_specs=pl.BlockSpec(memory_space=pltpu.VMEM),
        scratch_shapes=[pltpu.SemaphoreType.DMA((N_DEV - 1,)), pltpu.SemaphoreType.DMA((N_DEV - 1,))],
    )(v)


def _chip_gather(arrays, name):
    n = len(arrays)

    def body(*refs):
        gather = _ChipGather(refs[:n], refs[n:2 * n], refs[2 * n:])
        gather.start()
        gather.forward()
        gather.finish()

    return pl.pallas_call(
        body, name=name,
        out_shape=_exchange_out_shapes(arrays, True),
        in_specs=[pl.BlockSpec(memory_space=pl.ANY)] * n,
        out_specs=tuple(pl.BlockSpec(memory_space=pl.ANY) for _ in arrays),
        scratch_shapes=_gather_sems(n),
    )(*arrays)


def _exchange_out_shapes(arrays, gather):
    return tuple(jax.ShapeDtypeStruct((N_CHIP,) + a.shape if gather else a.shape, a.dtype) for a in arrays)


def _scatter_sems(n):
    n_sem = n * (N_CHIP - 1)
    return [pltpu.SemaphoreType.DMA((n_sem,)), pltpu.SemaphoreType.DMA((n_sem,)), pltpu.SemaphoreType.DMA((n,))]


def _gather_sems(n):
    n_sem = n * (N_CHIP - 1)
    return [pltpu.SemaphoreType.DMA((n_sem,))] * 4 + [pltpu.SemaphoreType.DMA((n,))]


def _peer_chips(x, y):
    out = []
    for k in range(1, N_CHIP):
        px, py = _flip(x, (k >> 1) & 1), _flip(y, k & 1)
        out.append((px, py, 2 * px + py))
    return out


class _ChipScatter:
    def __init__(self, ins, outs, sems):
        send_sems, recv_sems, local_sems = sems
        x, y, c = _mesh_pos()
        chip = 2 * x + y
        self.local, self.sends, self.recvs = [], [], []
        for i in range(len(ins)):
            self.local.append(pltpu.make_async_copy(ins[i].at[chip], outs[i].at[chip], local_sems.at[i]))
            for k, (px, py, peer_chip) in enumerate(_peer_chips(x, y)):
                sem = i * (N_CHIP - 1) + k
                src = ins[i].at[peer_chip]
                self.sends.append(pltpu.make_async_remote_copy(
                    src_ref=src, dst_ref=outs[i].at[chip], send_sem=send_sems.at[sem], recv_sem=recv_sems.at[sem],
                    device_id=(px, py, c), device_id_type=MESH))
                self.recvs.append(pltpu.make_async_remote_copy(
                    src_ref=src, dst_ref=outs[i].at[peer_chip], send_sem=send_sems.at[sem], recv_sem=recv_sems.at[sem],
                    device_id=(px, py, c), device_id_type=MESH))

    def start(self):
        for cp in self.local + self.sends:
            cp.start()

    def wait(self):
        for cp in self.recvs:
            cp.wait_recv()
        for cp in self.sends:
            cp.wait_send()
        for cp in self.local:
            cp.wait()


class _ChipGather:
    def __init__(self, ins, outs, sems):
        ici_send, ici_recv, d2d_send, d2d_recv, local_sems = sems
        x, y, c = _mesh_pos()
        chip = 2 * x + y
        self.local, self.ici_sends, self.ici_recvs, self.d2d_sends, self.d2d_recvs = [], [], [], [], []
        for i in range(len(ins)):
            half = ins[i].shape[-1] // 2
            assert half % 128 == 0
            lead = (slice(None),) * (len(ins[i].shape) - 1)
            mine = lead + (pl.ds(pl.multiple_of(c * half, 128), half),)
            theirs = lead + (pl.ds(pl.multiple_of((1 - c) * half, 128), half),)
            self.local.append(pltpu.make_async_copy(ins[i], outs[i].at[chip], local_sems.at[i]))
            for k, (px, py, peer_chip) in enumerate(_peer_chips(x, y)):
                sem = i * (N_CHIP - 1) + k
                self.ici_sends.append(pltpu.make_async_remote_copy(
                    src_ref=ins[i].at[mine], dst_ref=outs[i].at[chip].at[mine],
                    send_sem=ici_send.at[sem], recv_sem=ici_recv.at[sem], device_id=(px, py, c), device_id_type=MESH))
                landed = outs[i].at[peer_chip].at[mine]
                self.ici_recvs.append(pltpu.make_async_remote_copy(
                    src_ref=ins[i].at[mine], dst_ref=landed,
                    send_sem=ici_send.at[sem], recv_sem=ici_recv.at[sem], device_id=(px, py, c), device_id_type=MESH))
                self.d2d_sends.append(pltpu.make_async_remote_copy(
                    src_ref=landed, dst_ref=landed,
                    send_sem=d2d_send.at[sem], recv_sem=d2d_recv.at[sem], device_id=(x, y, 1 - c), device_id_type=MESH))
                self.d2d_recvs.append(pltpu.make_async_remote_copy(
                    src_ref=landed, dst_ref=outs[i].at[peer_chip].at[theirs],
                    send_sem=d2d_send.at[sem], recv_sem=d2d_recv.at[sem], device_id=(x, y, 1 - c), device_id_type=MESH))

    def start(self):
        for cp in self.local + self.ici_sends:
            cp.start()

    def forward(self):
        for landed, onward in zip(self.ici_recvs, self.d2d_sends):
            landed.wait_recv()
            onward.start()

    def finish(self):
        for cp in self.d2d_recvs:
            cp.wait_recv()
        for cp in self.d2d_sends + self.ici_sends:
            cp.wait_send()
        for cp in self.local:
            cp.wait()


def _sibling_swap(arrays, name):
    n = len(arrays)

    def body(*refs):
        swap = _SiblingSwap(refs[:n], refs[n:2 * n], refs[2 * n:])
        swap.start()
        swap.wait()

    return pl.pallas_call(
        body, name=name,
        out_shape=tuple(jax.ShapeDtypeStruct(a.shape, a.dtype) for a in arrays),
        in_specs=[pl.BlockSpec(memory_space=pl.ANY)] * n,
        out_specs=tuple(pl.BlockSpec(memory_space=pl.ANY) for _ in arrays),
        scratch_shapes=_swap_sems(n),
    )(*arrays)


def _swap_sems(n):
    return [pltpu.SemaphoreType.DMA((n,)), pltpu.SemaphoreType.DMA((n,))]


class _SiblingSwap:
    def __init__(self, ins, outs, sems):
        send_sems, recv_sems = sems
        x, y, c = _mesh_pos()
        self.copies = [pltpu.make_async_remote_copy(
            src_ref=ins[i], dst_ref=outs[i], send_sem=send_sems.at[i], recv_sem=recv_sems.at[i],
            device_id=(x, y, 1 - c), device_id_type=MESH) for i in range(len(ins))]

    def start(self):
        for cp in self.copies:
            cp.start()

    def wait(self):
        for cp in self.copies:
            cp.wait_recv()
        for cp in self.copies:
            cp.wait_send()


def _ada_fwd(c_all, w_ada_blk, b_blk):
    cols = w_ada_blk.shape[1]

    def body(c_ref, w_ref, b_ref, out_ref):
        cv = c_ref[...]
        out_ref[...] = _mm32(cv * _sigmoid(cv), w_ref[...]) + b_ref[...]

    return pl.pallas_call(
        body, name="ada_fwd",
        out_shape=jax.ShapeDtypeStruct((N_DEV, cols), jnp.float32),
        compiler_params=pltpu.CompilerParams(vmem_limit_bytes=VMEM_LIMIT),
    )(c_all, w_ada_blk, b_blk)


def _adam(w, g, m, v):
    m2 = ADAM_B1 * m + (1.0 - ADAM_B1) * g
    v2 = ADAM_B2 * v + (1.0 - ADAM_B2) * (g * g)
    m_hat = m2 / (1.0 - ADAM_B1 ** ADAM_STEP)
    v_hat = v2 / (1.0 - ADAM_B2 ** ADAM_STEP)
    delta = -ADAM_LR * (m_hat / (jnp.sqrt(v_hat) + ADAM_EPS) + ADAM_WD * w)
    return delta, m2, v2


def _ada_bwd_adam(c_t, dmod_blk, w, m, v):
    rows, cols = w.shape
    tile = 512
    assert cols % tile == 0

    def body(c_ref, d_ref, w_ref, m_ref, v_ref, g_ref, dl_ref, m2_ref, v2_ref):
        sc = c_ref[...]
        sc = sc * _sigmoid(sc)
        dm = d_ref[...]
        g = sc[:, 0:1] * dm[0:1, :]
        for b in range(1, N_DEV):
            g = g + sc[:, b:b + 1] * dm[b:b + 1, :]
        delta, m2, v2 = _adam(w_ref[...], g, m_ref[...], v_ref[...])
        g_ref[...] = g
        dl_ref[...] = delta
        m2_ref[...] = m2
        v2_ref[...] = v2

    blk = pl.BlockSpec((rows, tile), lambda j: (0, j))
    out = jax.ShapeDtypeStruct((rows, cols), jnp.float32)
    return pl.pallas_call(
        body, name="ada_bwd_adam", grid=(cols // tile,),
        out_shape=(out, out, out, out),
        in_specs=[pl.BlockSpec((rows, N_DEV), lambda j: (0, 0)), pl.BlockSpec((N_DEV, tile), lambda j: (0, j)),
                  blk, blk, blk],
        out_specs=(blk, blk, blk, blk),
        compiler_params=_params(("arbitrary",)),
    )(c_t, dmod_blk, w, m, v)


def _inproj_fwd(x2, vecs, w_in_p, tm, riders):
    seq = x2.shape[0]
    n_tiles = seq // tm
    n_ride = len(riders)

    def body(*refs):
        x_ref, vec_ref, w_ref = refs[:3]
        ride_in, refs = refs[3:3 + n_ride], refs[3 + n_ride:]
        p_ref, u_ref = refs[:2]
        ride_out, sems = refs[2:2 + n_ride], refs[2 + n_ride:]
        gather = _ChipGather(ride_in, ride_out, sems)

        @pl.when(pl.program_id(0) == 0)
        def _():
            gather.start()

        xh, _ = _ln(x_ref[...])
        u = (xh * (1.0 + vec_ref[1:2, :]) + vec_ref[0:1, :]).astype(MXU_DTYPE)
        u_ref[...] = u
        p_ref[...] = _mm(u, w_ref[...])

        @pl.when(pl.program_id(0) == (3 * n_tiles) // 4)
        def _():
            gather.forward()

        @pl.when(pl.program_id(0) == n_tiles - 1)
        def _():
            gather.finish()

    hbm = pl.BlockSpec(memory_space=pl.ANY)
    return pl.pallas_call(
        body, name="inproj_fwd", grid=(n_tiles,),
        out_shape=(jax.ShapeDtypeStruct((seq, N_PROJ), jnp.float32), jax.ShapeDtypeStruct((seq, D_MODEL), MXU_DTYPE))
        + _exchange_out_shapes(riders, True),
        in_specs=[pl.BlockSpec((tm, D_MODEL), lambda i: (i, 0)), _const_spec(vecs.shape), _const_spec(w_in_p.shape)]
        + [hbm] * n_ride,
        out_specs=(pl.BlockSpec((tm, N_PROJ), lambda i: (i, 0)), pl.BlockSpec((tm, D_MODEL), lambda i: (i, 0)))
        + (hbm,) * n_ride,
        scratch_shapes=_gather_sems(n_ride),
        compiler_params=_params(("arbitrary",)),
    )(x2, vecs, w_in_p, *riders)


def _inproj_bwd(dproj, x2, dxa, vecs, w_in_pt, tm, riders):
    seq = x2.shape[0]
    n_tiles = seq // tm
    n_ride = len(riders)

    def body(*refs):
        dp_ref, x_ref, dxa_ref, vec_ref, w_ref = refs[:5]
        ride_in, refs = refs[5:5 + n_ride], refs[5 + n_ride:]
        gx_ref, sums_ref = refs[:2]
        ride_out, sems = refs[2:2 + n_ride], refs[2 + n_ride:]
        exchange = _ChipScatter(ride_in, ride_out, sems)

        @pl.when(pl.program_id(0) == 0)
        def _():
            exchange.start()
            sums_ref[...] = jnp.zeros_like(sums_ref)

        du = _mm(dp_ref[...], w_ref[...])
        xh, rstd = _ln(x_ref[...])
        sums_ref[0:1, :] += _colsum(du)
        sums_ref[1:2, :] += _colsum(du * xh)
        gx_ref[...] = dxa_ref[...] + _ln_bwd(du * (1.0 + vec_ref[1:2, :]), xh, rstd)

        @pl.when(pl.program_id(0) == n_tiles - 1)
        def _():
            exchange.wait()

    tile = pl.BlockSpec((tm, D_MODEL), lambda i: (i, 0))
    hbm = pl.BlockSpec(memory_space=pl.ANY)
    return pl.pallas_call(
        body, name="inproj_bwd", grid=(n_tiles,),
        out_shape=(jax.ShapeDtypeStruct((seq, D_MODEL), jnp.float32), jax.ShapeDtypeStruct((8, D_MODEL), jnp.float32))
        + _exchange_out_shapes(riders, False),
        in_specs=[pl.BlockSpec((tm, N_PROJ), lambda i: (i, 0)), tile, tile, _const_spec(vecs.shape),
                  _const_spec(w_in_pt.shape)] + [hbm] * n_ride,
        out_specs=(tile, pl.BlockSpec((8, D_MODEL), lambda i: (0, 0))) + (hbm,) * n_ride,
        scratch_shapes=_scatter_sems(n_ride),
        compiler_params=_params(("arbitrary",)),
    )(dproj, x2, dxa, vecs, w_in_pt, *riders)


def _head(h):
    return slice(h * HEAD_W, (h + 1) * HEAD_W)


def _cols(ref, off, h):
    return ref[:, off + h * HEAD_W:off + (h + 1) * HEAD_W]


HEADS = range(N_HEADS)


def _mixer_chunk_forward(p_ref, cc, ss, dm_ref, qdec_ref, kdec_ref, wg_ref, bg_ref, states):
    row, col = _tri_masks()
    lower = row >= col
    f = {}
    f["glr"] = p_ref[:, OFF_LR:OFF_LR + HEAD_W]
    f["logit"] = _mm(f["glr"], wg_ref[...]) + bg_ref[...]
    rq = [_cols(p_ref, OFF_RQ, h) for h in HEADS]
    rk = [_cols(p_ref, OFF_RK, h) for h in HEADS]
    f["rv"] = [_cols(p_ref, OFF_RV, h) for h in HEADS]
    f["qr"] = [(rq[h] * cc + _swap_halves(rq[h]) * ss) * RET_SCALE for h in HEADS]
    f["kr"] = [rk[h] * cc + _swap_halves(rk[h]) * ss for h in HEADS]
    s_raw = [_mm_nt(f["qr"][h], f["kr"][h]) for h in HEADS]
    yield
    la = _log_sigmoid(f["logit"]) * (1.0 / GATE_TAU)
    b = _running_sum(lower, la)
    f["qd"] = [f["qr"][h] * qdec_ref[:, _head(h)] for h in HEADS]
    f["kd"] = [f["kr"][h] * kdec_ref[:, _head(h)] for h in HEADS]
    f["scores"] = [s_raw[h] * dm_ref[h] for h in HEADS]
    yield
    b_last = b[CHUNK - 1:CHUNK, :]
    b_mid = b[CHUNK // 2 - 1:CHUNK // 2, :]
    f["e"], f["ei"] = jnp.exp(b - b_mid), jnp.exp(b_mid - b)
    f["eb"], f["ek"], f["ebl"] = jnp.exp(b), jnp.exp(b_last - b), jnp.exp(b_last)
    gq = [_cols(p_ref, OFF_GQ, h) * GLA_SCALE for h in HEADS]
    gk = [_cols(p_ref, OFF_GK, h) for h in HEADS]
    f["gv"] = [_cols(p_ref, OFF_GV, h) for h in HEADS]
    f["q_e"] = [gq[h] * f["e"][:, _head(h)] for h in HEADS]
    f["q_i"] = [gq[h] * f["ei"][:, _head(h)] for h in HEADS]
    f["k_e"] = [gk[h] * f["e"][:, _head(h)] for h in HEADS]
    f["k_i"] = [gk[h] * f["ei"][:, _head(h)] for h in HEADS]
    low = [_mm_nt(f["q_e"][h], f["k_i"][h]) for h in HEADS]
    up = [_mm_nt(f["q_i"][h], f["k_e"][h]) for h in HEADS]
    yield
    f["att"] = [jnp.where(lower, low[h], up[h]) for h in HEADS]
    f["qb"] = [gq[h] * f["eb"][:, _head(h)] for h in HEADS]
    f["kb"] = [gk[h] * f["ek"][:, _head(h)] for h in HEADS]
    ret_state, gla_state_t = states()
    f["o_ret"] = [_mm(f["scores"][h], f["rv"][h]) + _mm(f["qd"][h], ret_state[h]) for h in HEADS]
    f["o_gla"] = [_mm(f["att"][h], f["gv"][h]) + _mm_nt(f["qb"][h], gla_state_t[h]) for h in HEADS]
    return f


def _interleave(generators):
    live = list(generators)
    while live:
        for g in list(live):
            try:
                next(g)
            except StopIteration:
                live.remove(g)


def _mixer_fwd(proj, tables, wg_p, bg_p, ret_norm_w, gla_norm_w, riders):
    seq = proj.shape[0]
    n_chunks = seq // CHUNK
    per_step = min(n_chunks, CHUNKS_PER_STEP)
    n_steps = n_chunks // per_step
    n_ride = len(riders)
    rot_a, rot_b, dm_t, qdec_t, kdec_t, chunk_decay = tables

    def body(*refs):
        p_ref, ra_ref, rb_ref, dm_ref, qdec_ref, kdec_ref, wg_ref, bg_ref, wr_ref, wl_ref = refs[:10]
        ride_in, refs = refs[10:10 + n_ride], refs[10 + n_ride:]
        mix_ref, rsave_ref, ssave_ref = refs[:3]
        ride_out, refs = refs[3:3 + n_ride], refs[3 + n_ride:]
        r_sc, s_sc = refs[:2]
        gather = _ChipGather(ride_in, ride_out, refs[2:])

        @pl.when(pl.program_id(0) == 0)
        def _():
            gather.start()
            r_sc[...] = jnp.zeros_like(r_sc)
            s_sc[...] = jnp.zeros_like(s_sc)

        def one_chunk(c):
            p_c = p_ref.at[c * CHUNK:(c + 1) * CHUNK, :]
            mix_c = mix_ref.at[c * CHUNK:(c + 1) * CHUNK, :]
            before = {}

            def states():
                before["ret"] = [r_sc[h] for h in HEADS]
                before["gla"] = [s_sc[h] for h in HEADS]
                for h in HEADS:
                    rsave_ref[c, h] = before["ret"][h]
                    ssave_ref[c, h] = before["gla"][h]
                return before["ret"], before["gla"]

            cc, ss = _rotary_chunk(ra_ref, c, rb_ref)
            f = yield from _mixer_chunk_forward(p_c, cc, ss, dm_ref, qdec_ref, kdec_ref, wg_ref, bg_ref, states)
            for h in HEADS:
                r_sc[h] = chunk_decay[h] * before["ret"][h] + _mm_tn(f["kd"][h], f["rv"][h])
            for h in HEADS:
                s_sc[h] = before["gla"][h] * f["ebl"][:, _head(h)] + _mm_tn(f["gv"][h], f["kb"][h])
            yield
            for h in HEADS:
                on, _ = _ln(f["o_ret"][h])
                g = _cols(p_c, OFF_RG, h)
                mix_c[:, _head(h)] = (on * wr_ref[:, _head(h)] * (g * _sigmoid(g))).astype(mix_ref.dtype)
            for h in HEADS:
                o = f["o_gla"][h]
                on = o * lax.rsqrt(_rowmean(o * o) + LN_EPS)
                g = _cols(p_c, OFF_GG, h)
                mix_c[:, _head(N_HEADS + h)] = (on * wl_ref[:, _head(h)] * (g * _sigmoid(g))).astype(mix_ref.dtype)

        for c0 in range(0, per_step, CHUNKS_IN_LOCKSTEP):
            _interleave([one_chunk(c) for c in range(c0, min(per_step, c0 + CHUNKS_IN_LOCKSTEP))])

        @pl.when(pl.program_id(0) == (3 * n_steps) // 4)
        def _():
            gather.forward()

        @pl.when(pl.program_id(0) == n_steps - 1)
        def _():
            gather.finish()

    state_shape = (n_chunks, N_HEADS, HEAD_W, HEAD_W)
    state_blk = pl.BlockSpec((per_step, N_HEADS, HEAD_W, HEAD_W), lambda i: (i, 0, 0, 0))
    rot_blk = pl.BlockSpec((per_step, 8, HEAD_W), lambda i: (i, 0, 0))
    rows = per_step * CHUNK
    hbm = pl.BlockSpec(memory_space=pl.ANY)
    return pl.pallas_call(
        body, name="mixer_fwd", grid=(n_steps,),
        out_shape=(jax.ShapeDtypeStruct((seq, D_MODEL), MXU_DTYPE),
                   jax.ShapeDtypeStruct(state_shape, jnp.float32), jax.ShapeDtypeStruct(state_shape, jnp.float32))
        + _exchange_out_shapes(riders, True),
        in_specs=[pl.BlockSpec((rows, N_PROJ), lambda i: (i, 0)), rot_blk, _const_spec(rot_b.shape),
                  _const_spec(dm_t.shape), _const_spec(qdec_t.shape), _const_spec(kdec_t.shape),
                  _const_spec(wg_p.shape), _const_spec(bg_p.shape), _const_spec(ret_norm_w.shape),
                  _const_spec(gla_norm_w.shape)] + [hbm] * n_ride,
        out_specs=(pl.BlockSpec((rows, D_MODEL), lambda i: (i, 0)), state_blk, state_blk) + (hbm,) * n_ride,
        scratch_shapes=[pltpu.VMEM((N_HEADS, HEAD_W, HEAD_W), jnp.float32),
                        pltpu.VMEM((N_HEADS, HEAD_W, HEAD_W), jnp.float32)] + _gather_sems(n_ride),
        compiler_params=_params(("arbitrary",)),
    )(proj, rot_a, rot_b, dm_t, qdec_t, kdec_t, wg_p, bg_p, ret_norm_w, gla_norm_w, *riders)


def _mixer_bwd(proj, dmixed, rsave, ssave, tables, wg_p, bg_p, ret_norm_w, gla_norm_w, riders):
    seq = proj.shape[0]
    n_chunks = seq // CHUNK
    per_step = min(n_chunks, CHUNKS_PER_STEP)
    n_steps = n_chunks // per_step
    n_ride = len(riders)
    rot_a, rot_b, dm_t, qdec_t, kdec_t, chunk_decay = tables
    last = n_steps - 1

    def body(*refs):
        p_blk, dmx_blk = refs[:2]
        shared_in = refs[2:13]
        ride_in, refs = refs[13:13 + n_ride], refs[13 + n_ride:]
        dp_blk, dwr_ref, dwl_ref, dwg_ref, dbg_ref = refs[:5]
        ride_out, refs = refs[5:5 + n_ride], refs[5 + n_ride:]
        dr_sc, ds_sc = refs[:2]
        exchange = _ChipScatter(ride_in, ride_out, refs[2:])

        @pl.when(pl.program_id(0) == 0)
        def _():
            exchange.start()
            dr_sc[...] = jnp.zeros_like(dr_sc)
            ds_sc[...] = jnp.zeros_like(ds_sc)
            dwr_ref[...] = jnp.zeros_like(dwr_ref)
            dwl_ref[...] = jnp.zeros_like(dwl_ref)
            dwg_ref[...] = jnp.zeros_like(dwg_ref)
            dbg_ref[...] = jnp.zeros_like(dbg_ref)

        def chunk_stages(c):
            rows = slice(c * CHUNK, (c + 1) * CHUNK)
            return one_chunk(c, p_blk.at[rows, :], dmx_blk.at[rows, :], dp_blk.at[rows, :], *shared_in,
                             dwr_ref, dwl_ref, dwg_ref, dbg_ref, dr_sc, ds_sc)

        for c0 in range(per_step, 0, -CHUNKS_IN_LOCKSTEP):
            _interleave([chunk_stages(c) for c in reversed(range(max(0, c0 - CHUNKS_IN_LOCKSTEP), c0))])

        @pl.when(pl.program_id(0) == last)
        def _():
            exchange.wait()

    def one_chunk(c, p_ref, dmx_ref, dp_ref, rsave_ref, ssave_ref, ra_ref, rb_ref, dm_ref, qdec_ref, kdec_ref,
                  wg_ref, bg_ref, wr_ref, wl_ref, dwr_ref, dwl_ref, dwg_ref, dbg_ref, dr_sc, ds_sc):
        def put(off, h, val):
            dp_ref[:, off + h * HEAD_W:off + (h + 1) * HEAD_W] = val.astype(dp_ref.dtype)

        cc, ss = _rotary_chunk(ra_ref, c, rb_ref)
        row, col = _tri_masks()
        ret_state = [rsave_ref[c, h] for h in HEADS]
        gla_state_t = [ssave_ref[c, h] for h in HEADS]
        f = yield from _mixer_chunk_forward(p_ref, cc, ss, dm_ref, qdec_ref, kdec_ref, wg_ref, bg_ref,
                                            lambda: (ret_state, gla_state_t))
        yield

        do_ret, do_gla = [], []
        for h in HEADS:
            on, rstd = _ln(f["o_ret"][h])
            g = _cols(p_ref, OFF_RG, h)
            sg = _sigmoid(g)
            dy = dmx_ref[:, _head(h)].astype(jnp.float32)
            wr = wr_ref[:, _head(h)]
            dwr_ref[:, _head(h)] += _colsum(dy * on * (g * sg))
            put(OFF_RG, h, dy * on * wr * (sg * (1.0 + g * (1.0 - sg))))
            do_ret.append(_ln_bwd(dy * wr * (g * sg), on, rstd))
        for h in HEADS:
            o = f["o_gla"][h]
            rstd = lax.rsqrt(_rowmean(o * o) + LN_EPS)
            on = o * rstd
            g = _cols(p_ref, OFF_GG, h)
            sg = _sigmoid(g)
            dy = dmx_ref[:, _head(N_HEADS + h)].astype(jnp.float32)
            wl = wl_ref[:, _head(h)]
            dwl_ref[:, _head(h)] += _colsum(dy * on * (g * sg))
            put(OFF_GG, h, dy * on * wl * (sg * (1.0 + g * (1.0 - sg))))
            don = dy * wl * (g * sg)
            do_gla.append(rstd * (don - on * _rowmean(don * on)))

        yield

        d_ret_new = [dr_sc[h] for h in HEADS]
        d_gla_new = [ds_sc[h] for h in HEADS]
        ds_raw = [_mm_nt(do_ret[h], f["rv"][h]) * dm_ref[h] for h in HEADS]
        d_att = [_mm_nt(do_gla[h], f["gv"][h]) for h in HEADS]
        dq_state = [_mm_nt(do_ret[h], ret_state[h]) for h in HEADS]
        dk_state = [_mm_nt(f["rv"][h], d_ret_new[h]) for h in HEADS]
        dqb = [_mm(do_gla[h], gla_state_t[h]) for h in HEADS]
        dkb = [_mm(f["gv"][h], d_gla_new[h]) for h in HEADS]
        for h in HEADS:
            put(OFF_RV, h, _mm_tn(f["scores"][h], do_ret[h]) + _mm(f["kd"][h], d_ret_new[h]))
        for h in HEADS:
            put(OFF_GV, h, _mm_tn(f["att"][h], do_gla[h]) + _mm_nt(f["kb"][h], d_gla_new[h]))
        for h in HEADS:
            dr_sc[h] = chunk_decay[h] * d_ret_new[h] + _mm_tn(f["qd"][h], do_ret[h])
        for h in HEADS:
            ds_sc[h] = d_gla_new[h] * f["ebl"][:, _head(h)] + _mm_tn(do_gla[h], f["qb"][h])
        yield

        dqr = [_mm(ds_raw[h], f["kr"][h]) + dq_state[h] * qdec_ref[:, _head(h)] for h in HEADS]
        dkr = [_mm_tn(ds_raw[h], f["qr"][h]) + dk_state[h] * kdec_ref[:, _head(h)] for h in HEADS]
        d_low = [jnp.where(row >= col, d_att[h], 0.0) for h in HEADS]
        d_up = [jnp.where(row < col, d_att[h], 0.0) for h in HEADS]
        dq_e = [_mm(d_low[h], f["k_i"][h]) for h in HEADS]
        dk_i = [_mm_tn(d_low[h], f["q_e"][h]) for h in HEADS]
        dq_i = [_mm(d_up[h], f["k_e"][h]) for h in HEADS]
        dk_e = [_mm_tn(d_up[h], f["q_i"][h]) for h in HEADS]
        yield
        for h in HEADS:
            put(OFF_RQ, h, (dqr[h] * cc + _swap_halves(dqr[h] * ss)) * RET_SCALE)
            put(OFF_RK, h, dkr[h] * cc + _swap_halves(dkr[h] * ss))
        row_id = lax.broadcasted_iota(jnp.int32, (CHUNK, HEAD_W), 0)
        db_heads = []
        for h in HEADS:
            hs = _head(h)
            e, ei, eb, ek, ebl = f["e"][:, hs], f["ei"][:, hs], f["eb"][:, hs], f["ek"][:, hs], f["ebl"][:, hs]
            put(OFF_GQ, h, (dq_e[h] * e + dq_i[h] * ei + dqb[h] * eb) * GLA_SCALE)
            put(OFF_GK, h, dk_e[h] * e + dk_i[h] * ei + dkb[h] * ek)
            db = (dq_e[h] * f["q_e"][h] - dq_i[h] * f["q_i"][h] + dk_e[h] * f["k_e"][h] - dk_i[h] * f["k_i"][h]
                  + dqb[h] * f["qb"][h] - dkb[h] * f["kb"][h])
            db_last = _colsum(dkb[h] * f["kb"][h]) + ebl * _colsum(gla_state_t[h] * d_gla_new[h])
            db_heads.append(db + jnp.where(row_id == CHUNK - 1, db_last, 0.0))
        db = jnp.concatenate(db_heads, axis=1)
        d_la = _running_sum(col >= row, db)
        d_logit = d_la * (1.0 / GATE_TAU) * (1.0 - _sigmoid(f["logit"]))
        put(OFF_LR, 0, _mm_nt(d_logit, wg_ref[...]))
        dwg_ref[...] += _mm_tn(f["glr"], d_logit)
        dbg_ref[...] += _colsum(d_logit)

    state_blk = pl.BlockSpec((per_step, N_HEADS, HEAD_W, HEAD_W), lambda i: (last - i, 0, 0, 0))
    rot_blk = pl.BlockSpec((per_step, 8, HEAD_W), lambda i: (last - i, 0, 0))
    width = N_HEADS * HEAD_W
    vec_out = pl.BlockSpec((1, width), lambda i: (0, 0))
    hbm = pl.BlockSpec(memory_space=pl.ANY)
    rows_blk = per_step * CHUNK
    return pl.pallas_call(
        body, name="mixer_bwd", grid=(n_steps,),
        out_shape=(jax.ShapeDtypeStruct((seq, N_PROJ), MXU_DTYPE),
                   jax.ShapeDtypeStruct((1, width), jnp.float32), jax.ShapeDtypeStruct((1, width), jnp.float32),
                   jax.ShapeDtypeStruct((HEAD_W, width), jnp.float32), jax.ShapeDtypeStruct((1, width), jnp.float32))
        + _exchange_out_shapes(riders, False),
        in_specs=[pl.BlockSpec((rows_blk, N_PROJ), lambda i: (last - i, 0)),
                  pl.BlockSpec((rows_blk, D_MODEL), lambda i: (last - i, 0)), state_blk, state_blk, rot_blk,
                  _const_spec(rot_b.shape),
                  _const_spec(dm_t.shape), _const_spec(qdec_t.shape), _const_spec(kdec_t.shape),
                  _const_spec(wg_p.shape), _const_spec(bg_p.shape), _const_spec(ret_norm_w.shape),
                  _const_spec(gla_norm_w.shape)] + [hbm] * n_ride,
        out_specs=(pl.BlockSpec((rows_blk, N_PROJ), lambda i: (last - i, 0)), vec_out, vec_out,
                   pl.BlockSpec((HEAD_W, width), lambda i: (0, 0)), vec_out) + (hbm,) * n_ride,
        scratch_shapes=[pltpu.VMEM((N_HEADS, HEAD_W, HEAD_W), jnp.float32),
                        pltpu.VMEM((N_HEADS, HEAD_W, HEAD_W), jnp.float32)] + _scatter_sems(n_ride),
        compiler_params=_params(("arbitrary",)),
    )(proj, dmixed, rsave, ssave, rot_a, rot_b, dm_t, qdec_t, kdec_t, wg_p, bg_p, ret_norm_w, gla_norm_w, *riders)


V_GATE1, V_SCALE2, V_SHIFT2, V_GATE2, V_LN1W, V_LN1B, V_LN2W, V_LN2B = range(8)
S_GATE1, S_SCALE2, S_SHIFT2, S_GATE2, S_LN1W, S_LN1B, S_LN2W, S_LN2B, S_LOSS = range(9)


def _mlp_fwd_bwd(x2, mixed, target, vecs, w_out, w1_chunks, w2_chunks, tm):
    seq = x2.shape[0]
    n_fc, _, fc = w1_chunks.shape

    def body(x_ref, mx_ref, t_ref, vec_ref, wo_ref, w1_ref, w2_ref,
             dmx_ref, dxa_ref, a_ref, dh_ref, u2_ref, df_ref, dm_ref, sums_ref, relu_sc):
        @pl.when(pl.program_id(0) == 0)
        def _():
            sums_ref[...] = jnp.zeros_like(sums_ref)

        vec = lambda r: vec_ref[r:r + 1, :]

        def acc(r, val):
            sums_ref[r:r + 1, :] += _colsum(val)

        xx = x_ref[...]
        m = _mm(mx_ref[...], wo_ref[...])
        z1h, rstd1 = _ln(ALPHA * xx + vec(V_GATE1) * m)
        x1 = z1h * vec(V_LN1W) + vec(V_LN1B)
        x1h, rstd0 = _ln(x1)
        u2 = (x1h * (1.0 + vec(V_SCALE2)) + vec(V_SHIFT2)).astype(MXU_DTYPE)
        u2_ref[...] = u2
        f = jnp.zeros((tm, D_MODEL), jnp.float32)
        for j in range(n_fc):
            r = jnp.maximum(_mm(u2, w1_ref[j]), 0.0)
            relu_sc[:, j * fc:(j + 1) * fc] = r
            a = (r * r).astype(MXU_DTYPE)
            a_ref[:, j * fc:(j + 1) * fc] = a
            f = f + _mm(a, w2_ref[j])
        z2h, rstd2 = _ln(ALPHA * x1 + vec(V_GATE2) * f)
        err = z2h * vec(V_LN2W) + vec(V_LN2B) - t_ref[...]
        acc(S_LOSS, err * err)
        dy = err * (1.0 / D_MODEL)
        acc(S_LN2W, dy * z2h)
        acc(S_LN2B, dy)
        dz2 = _ln_bwd(dy * vec(V_LN2W), z2h, rstd2)
        acc(S_GATE2, dz2 * f)
        df = (vec(V_GATE2) * dz2).astype(MXU_DTYPE)
        df_ref[...] = df
        du2 = jnp.zeros((tm, D_MODEL), jnp.float32)
        for j in range(n_fc):
            dh = (_mm_nt(df, w2_ref[j]) * (2.0 * relu_sc[:, j * fc:(j + 1) * fc])).astype(MXU_DTYPE)
            dh_ref[:, j * fc:(j + 1) * fc] = dh
            du2 = du2 + _mm_nt(dh, w1_ref[j])
        acc(S_SCALE2, du2 * x1h)
        acc(S_SHIFT2, du2)
        dx1 = ALPHA * dz2 + _ln_bwd(du2 * (1.0 + vec(V_SCALE2)), x1h, rstd0)
        acc(S_LN1W, dx1 * z1h)
        acc(S_LN1B, dx1)
        dz1 = _ln_bwd(dx1 * vec(V_LN1W), z1h, rstd1)
        acc(S_GATE1, dz1 * m)
        dxa_ref[...] = ALPHA * dz1
        dm = (vec(V_GATE1) * dz1).astype(MXU_DTYPE)
        dm_ref[...] = dm
        dmx_ref[...] = _mm_nt(dm, wo_ref[...])

    tile = lambda width: pl.BlockSpec((tm, width), lambda i: (i, 0))
    f32 = lambda width: jax.ShapeDtypeStruct((seq, width), jnp.float32)
    b16 = lambda width: jax.ShapeDtypeStruct((seq, width), MXU_DTYPE)
    return pl.pallas_call(
        body, name="mlp_fwd_bwd", grid=(seq // tm,),
        out_shape=(f32(D_MODEL), f32(D_MODEL), b16(D_FF), b16(D_FF), b16(D_MODEL), b16(D_MODEL), b16(D_MODEL),
                   jax.ShapeDtypeStruct((16, D_MODEL), jnp.float32)),
        in_specs=[tile(D_MODEL), tile(D_MODEL), tile(D_MODEL), _const_spec(vecs.shape), _const_spec(w_out.shape),
                  _const_spec(w1_chunks.shape), _const_spec(w2_chunks.shape)],
        out_specs=(tile(D_MODEL), tile(D_MODEL), tile(D_FF), tile(D_FF), tile(D_MODEL), tile(D_MODEL),
                   tile(D_MODEL), pl.BlockSpec((16, D_MODEL), lambda i: (0, 0))),
        scratch_shapes=[pltpu.VMEM((tm, D_FF), jnp.float32)],
        compiler_params=_params(("arbitrary",)),
    )(x2, mixed, target, vecs, w_out, w1_chunks, w2_chunks)


def _grad_matmul(a, b, name, tn, blocks_are_rows):
    seq, m_dim = a.shape
    n_dim = b.shape[1]
    tk = min(seq, GRAD_TOKEN_TILE)
    nk = seq // tk
    if blocks_are_rows:
        tm = m_dim // N_CHIP
        assert tn == n_dim
        grid = (N_CHIP, 1, nk)
        out_map = lambda i, j, k: (i, 0, 0)
    else:
        tm = m_dim
        assert tn * N_CHIP == n_dim
        grid = (1, N_CHIP, nk)
        out_map = lambda i, j, k: (j, 0, 0)

    def body(a_ref, b_ref, o_ref, acc_sc):
        k = pl.program_id(2)

        @pl.when(k == 0)
        def _():
            acc_sc[...] = jnp.zeros_like(acc_sc)

        acc_sc[...] += _mm_tn(a_ref[...], b_ref[...])

        @pl.when(k == nk - 1)
        def _():
            o_ref[0] = acc_sc[...].astype(o_ref.dtype)

    return pl.pallas_call(
        body, name=name, grid=grid,
        out_shape=jax.ShapeDtypeStruct((N_CHIP, tm, tn), WIRE_DTYPE),
        in_specs=[pl.BlockSpec((tk, tm), lambda i, j, k: (k, i)), pl.BlockSpec((tk, tn), lambda i, j, k: (k, j))],
        out_specs=pl.BlockSpec((1, tm, tn), out_map),
        scratch_shapes=[pltpu.VMEM((tm, tn), jnp.float32)],
        compiler_params=_params(("arbitrary", "arbitrary", "arbitrary")),
    )(a, b)


def _grad_matmul_full(a, b, name, tm, riders):
    seq, m_dim = a.shape
    n_dim = b.shape[1]
    tk = min(seq, GRAD_TOKEN_TILE)
    nk = seq // tk
    n_blocks = m_dim // tm
    n_ride = len(riders)
    assert m_dim % tm == 0

    def body(*refs):
        a_ref, b_ref = refs[:2]
        ride_in, refs = refs[2:2 + n_ride], refs[2 + n_ride:]
        o_ref = refs[0]
        ride_out, refs = refs[1:1 + n_ride], refs[1 + n_ride:]
        acc_sc = refs[0]
        swap = _SiblingSwap(ride_in, ride_out, refs[1:])
        i, k = pl.program_id(0), pl.program_id(1)

        @pl.when((i == 0) & (k == 0))
        def _():
            swap.start()

        @pl.when(k == 0)
        def _():
            acc_sc[...] = jnp.zeros_like(acc_sc)

        acc_sc[...] += _mm_tn(a_ref[...], b_ref[...])

        @pl.when(k == nk - 1)
        def _():
            o_ref[...] = acc_sc[...].astype(o_ref.dtype)

        @pl.when((i == n_blocks - 1) & (k == nk - 1))
        def _():
            swap.wait()

    hbm = pl.BlockSpec(memory_space=pl.ANY)
    return pl.pallas_call(
        body, name=name, grid=(n_blocks, nk),
        out_shape=(jax.ShapeDtypeStruct((m_dim, n_dim), WIRE_DTYPE),)
        + tuple(jax.ShapeDtypeStruct(r.shape, r.dtype) for r in riders),
        in_specs=[pl.BlockSpec((tk, tm), lambda i, k: (k, i)), pl.BlockSpec((tk, n_dim), lambda i, k: (k, 0))]
        + [hbm] * n_ride,
        out_specs=(pl.BlockSpec((tm, n_dim), lambda i, k: (i, 0)),) + (hbm,) * n_ride,
        scratch_shapes=[pltpu.VMEM((tm, n_dim), jnp.float32)] + _swap_sems(n_ride),
        compiler_params=_params(("arbitrary", "arbitrary")),
    )(a, b, *riders)


def _sum_chips(stack, name):
    _, rows, cols = stack.shape
    tc = min(cols, ELEMENTWISE_COLS)

    def body(s_ref, o_ref):
        total = s_ref[0].astype(jnp.float32)
        for j in range(1, N_CHIP):
            total = total + s_ref[j].astype(jnp.float32)
        o_ref[...] = total

    return pl.pallas_call(
        body, name=name, grid=(cols // tc,),
        out_shape=jax.ShapeDtypeStruct((rows, cols), jnp.float32),
        in_specs=[pl.BlockSpec((N_CHIP, rows, tc), lambda i: (0, 0, i))],
        out_specs=pl.BlockSpec((rows, tc), lambda i: (0, i)),
        compiler_params=_params(("arbitrary",)),
    )(stack)


def _adam_pair(w, g_mine, g_sibling, m, v, name):
    rows, cols = w.shape
    tc = min(cols, ELEMENTWISE_COLS)

    def body(w_ref, ga_ref, gb_ref, m_ref, v_ref, g_ref, dl_ref, m2_ref, v2_ref):
        g = ga_ref[...] + gb_ref[...]
        delta, m2, v2 = _adam(w_ref[...], g, m_ref[...], v_ref[...])
        g_ref[...] = g
        dl_ref[...] = delta
        m2_ref[...] = m2
        v2_ref[...] = v2

    blk = pl.BlockSpec((rows, tc), lambda i: (0, i))
    out = jax.ShapeDtypeStruct((rows, cols), jnp.float32)
    return pl.pallas_call(
        body, name=name, grid=(cols // tc,),
        out_shape=(out, out, out, out),
        in_specs=[blk] * 5, out_specs=(blk,) * 4,
        compiler_params=_params(("arbitrary",)),
    )(w, g_mine, g_sibling, m, v)


def _sum_devices(gathered):
    _, rows, _ = gathered.shape

    def body(g_ref, o_ref):
        total = g_ref[0]
        for d in range(1, N_DEV):
            total = total + g_ref[d]
        o_ref[...] = total

    return pl.pallas_call(
        body, name="sum_devices",
        out_shape=jax.ShapeDtypeStruct((rows, 128), jnp.float32),
    )(gathered)


def _adam_small(params):
    n = len(params)

    def body(*refs):
        ins, outs = refs[:4 * n], refs[4 * n:]
        for i in range(n):
            w_ref, g_ref, m_ref, v_ref = ins[4 * i:4 * i + 4]
            delta, m2, v2 = _adam(w_ref[...], g_ref[...], m_ref[...], v_ref[...])
            outs[3 * i][...] = delta
            outs[3 * i + 1][...] = m2
            outs[3 * i + 2][...] = v2

    out_shape = tuple(jax.ShapeDtypeStruct(p[0].shape, jnp.float32) for p in params for _ in range(3))
    out = pl.pallas_call(body, name="adam_small", out_shape=out_shape)(*[t for p in params for t in p])
    return [out[3 * i:3 * i + 3] for i in range(n)]


def _pad_heads(w):
    lead = w.shape[:-1]
    w = w.reshape(lead + (N_HEADS, GLA_DK))
    w = jnp.pad(w, [(0, 0)] * len(lead) + [(0, 0), (0, HEAD_W - GLA_DK)])
    return w.reshape(lead + (N_HEADS * HEAD_W,))


def _unpad_heads(w):
    lead = w.shape[:-1]
    return w.reshape(lead + (N_HEADS, HEAD_W))[..., :GLA_DK].reshape(lead + (N_HEADS * GLA_DK,))


def _pad_head_rows(w):
    w = w.reshape(N_HEADS, GLA_DK, w.shape[-1])
    return jnp.pad(w, ((0, 0), (0, HEAD_W - GLA_DK), (0, 0))).reshape(N_HEADS * HEAD_W, w.shape[-1])


def _unpad_head_rows(w):
    return w.reshape(N_HEADS, HEAD_W, w.shape[-1])[:, :GLA_DK].reshape(N_HEADS * GLA_DK, w.shape[-1])


def _pad_w_in_rows(w):
    return jnp.concatenate([
        w[:2048], _pad_head_rows(w[2048:2304]), _pad_head_rows(w[2304:2560]), w[2560:3584],
        jnp.pad(w[3584:3600], ((0, HEAD_W - GATE_RANK), (0, 0)))], axis=0)


def _unpad_w_in_rows(g):
    return jnp.concatenate([
        g[:2048], _unpad_head_rows(g[OFF_GQ:OFF_GQ + 512]), _unpad_head_rows(g[OFF_GK:OFF_GK + 512]),
        g[OFF_GV:OFF_LR], g[OFF_LR:OFF_LR + GATE_RANK]], axis=0)


def _rows128(a):
    return a.reshape(-1, 128)


def _rows8(a):
    a = a.reshape(-1, 128)
    return jnp.pad(a, ((0, -a.shape[0] % 8), (0, 0)))


def kernel(x, c, w_ada, b_ada, w_in, ret_norm_w, gla_gate_w, gla_gate_b, gla_norm_w, w_out, ln1_w, ln1_b, w_ff1, w_ff2, ln2_w, ln2_b, loss_target, m_w_ada, m_b_ada, m_w_in, m_ret_norm_w, m_gla_gate_w, m_gla_gate_b, m_gla_norm_w, m_w_out, m_ln1_w, m_ln1_b, m_w_ff1, m_w_ff2, m_ln2_w, m_ln2_b, v_w_ada, v_b_ada, v_w_in, v_ret_norm_w, v_gla_gate_w, v_gla_gate_b, v_gla_norm_w, v_w_out, v_ln1_w, v_ln1_b, v_w_ff1, v_w_ff2, v_ln2_w, v_ln2_b):
    seq = x.shape[1]
    tm = min(seq, TOKEN_TILE)
    tm_in = min(seq, INPROJ_TOKEN_TILE)
    xi, yi, ci = _mesh_pos()
    dev = 4 * xi + 2 * yi + ci
    chip = 2 * xi + yi
    x2, target = x[0], loss_target[0]
    ada_cols = w_ada.shape[2]
    in_cols = w_in.shape[2]
    gate_cols = gla_gate_w.shape[2]

    g0 = _gather_rows(jnp.concatenate([_rows128(c), _rows128(gla_gate_w[0])], axis=0), "gather_cond")
    c_all = g0[:, :8].reshape(N_DEV, D_MODEL)
    gate_w_full = jnp.concatenate([g0[2 * j, 8:16].reshape(GATE_RANK, gate_cols) for j in range(N_CHIP)], axis=1)
    wg_p = jnp.pad(_pad_heads(gate_w_full), ((0, HEAD_W - GATE_RANK), (0, 0)))
    bg_p = _pad_heads(gla_gate_b)

    b_blk = lax.dynamic_slice(b_ada, (0, chip * ada_cols), (1, ada_cols))
    mod_blk = _ada_fwd(c_all, w_ada[0], b_blk)
    g1 = _gather_rows(_rows128(mod_blk), "gather_mod")
    mod_all = jnp.concatenate([g1[2 * j].reshape(N_DEV, ada_cols) for j in range(N_CHIP)], axis=1)
    mod = lax.dynamic_slice(mod_all, (dev, 0), (1, 6 * D_MODEL))
    shift1, scale1, gate1, shift2, scale2, gate2 = [mod[:, i * D_MODEL:(i + 1) * D_MODEL] for i in range(6)]

    (w_in_stack,) = _chip_gather([jnp.transpose(w_in[0]).astype(WIRE_DTYPE)], "gather_w_in")
    w_in_pt = _pad_w_in_rows(w_in_stack.reshape(N_PROJ_SRC, D_MODEL)).astype(MXU_DTYPE)
    w_in_p = jnp.transpose(w_in_pt)

    zeros_row = jnp.zeros((1, D_MODEL), jnp.float32)
    vecs1 = jnp.concatenate([shift1, scale1] + [zeros_row] * 6, axis=0)
    proj, u, w2_stack = _inproj_fwd(x2, vecs1, w_in_p, tm_in, [w_ff2[0].astype(WIRE_DTYPE)])
    rot_a, rot_b = _rotary_tables(seq)
    dm_t, qdec_t, kdec_t, chunk_decay = _decay_tables()
    tables = (rot_a, rot_b, dm_t, qdec_t, kdec_t, chunk_decay)
    mixed, rsave, ssave, w_out_stack, w1_stack = _mixer_fwd(
        proj, tables, wg_p, bg_p, ret_norm_w, gla_norm_w,
        [w_out[0].astype(WIRE_DTYPE), w_ff1[0].astype(WIRE_DTYPE)])
    w_out_full = w_out_stack.reshape(D_MODEL, D_MODEL).astype(MXU_DTYPE)
    w1_chunks = w1_stack.astype(MXU_DTYPE)
    w2_chunks = w2_stack.astype(MXU_DTYPE)

    vecs2 = jnp.concatenate([gate1, scale2, shift2, gate2, ln1_w, ln1_b, ln2_w, ln2_b], axis=0)
    dmixed, dxa, act, dh, u2, df, dm, sums2 = _mlp_fwd_bwd(x2, mixed, target, vecs2, w_out_full, w1_chunks,
                                                           w2_chunks, tm)

    g_out_stack = _grad_matmul(mixed, dm, "grad_w_out", D_MODEL, True)
    g_ff1_stack = _grad_matmul(u2, dh, "grad_w_ff1", D_FF // N_CHIP, False)
    g_ff2_stack = _grad_matmul(act, df, "grad_w_ff2", D_MODEL, True)
    dproj, d_ret_norm, d_gla_norm, d_wg_p, d_bg_p, r_out, r_ff1, r_ff2 = _mixer_bwd(
        proj, dmixed, rsave, ssave, tables, wg_p, bg_p, ret_norm_w, gla_norm_w,
        [g_out_stack, g_ff1_stack, g_ff2_stack])
    early = ["w_out", "w_ff1", "w_ff2"]
    partial = {n: _sum_chips(r, "sum_" + n) for n, r in zip(early, [r_out, r_ff1, r_ff2])}
    g_in_t, *swapped_early = _grad_matmul_full(dproj, u, "grad_w_in", N_PROJ // 3, [partial[n] for n in early])
    swapped = dict(zip(early, swapped_early))
    g_in_stack = _unpad_w_in_rows(g_in_t).reshape(N_CHIP, in_cols, D_MODEL)
    grad_x, sums1, r_in = _inproj_bwd(dproj, x2, dxa, vecs1, w_in_pt, tm_in, [g_in_stack])

    dmod = jnp.concatenate([sums1[0:1], sums1[1:2], sums2[S_GATE1:S_GATE1 + 1], sums2[S_SHIFT2:S_SHIFT2 + 1],
                            sums2[S_SCALE2:S_SCALE2 + 1], sums2[S_GATE2:S_GATE2 + 1]], axis=1)
    d_gate_w_full = _unpad_heads(d_wg_p[:GATE_RANK])
    flat = lambda parts: jnp.concatenate([_rows8(p) for p in parts], axis=0)
    small = flat([dmod, sums2[S_LN1W:S_LN1W + 1], sums2[S_LN1B:S_LN1B + 1], sums2[S_LN2W:S_LN2W + 1],
                  sums2[S_LN2B:S_LN2B + 1], d_ret_norm, _unpad_heads(d_bg_p), d_gla_norm, d_gate_w_full,
                  sums2[S_LOSS:S_LOSS + 1]])
    g2 = _gather_rows(small, "gather_small")
    tot = _sum_devices(g2)
    loss = 0.5 / D_MODEL * jnp.sum(tot[136:144])
    grad_b_ada = tot[0:48].reshape(1, 6 * D_MODEL)
    grad_ln1_w, grad_ln1_b = tot[48:56].reshape(1, D_MODEL), tot[56:64].reshape(1, D_MODEL)
    grad_ln2_w, grad_ln2_b = tot[64:72].reshape(1, D_MODEL), tot[72:80].reshape(1, D_MODEL)
    grad_ret_norm = tot[80:84].reshape(1, 512)
    grad_gate_b = tot[88:90].reshape(1, 256)
    grad_gla_norm = tot[96:100].reshape(1, 512)
    grad_gate_w = lax.dynamic_slice(tot[104:136].reshape(GATE_RANK, 256), (0, chip * gate_cols),
                                    (GATE_RANK, gate_cols))

    small_grads = [grad_b_ada, grad_ln1_w, grad_ln1_b, grad_ln2_w, grad_ln2_b, grad_ret_norm, grad_gate_b,
                   grad_gla_norm, grad_gate_w[None]]
    small_out = _adam_small(list(zip(
        [b_ada, ln1_w, ln1_b, ln2_w, ln2_b, ret_norm_w, gla_gate_b, gla_norm_w, gla_gate_w], small_grads,
        [m_b_ada, m_ln1_w, m_ln1_b, m_ln2_w, m_ln2_b, m_ret_norm_w, m_gla_gate_b, m_gla_norm_w, m_gla_gate_w],
        [v_b_ada, v_ln1_w, v_ln1_b, v_ln2_w, v_ln2_b, v_ret_norm_w, v_gla_gate_b, v_gla_norm_w, v_gla_gate_w])))
    sm_delta, sm_m, sm_v = [[o[k] for o in small_out] for k in range(3)]

    dmod_all = g2[:, 0:48].reshape(N_DEV, 6 * D_MODEL)
    dmod_blk = lax.dynamic_slice(dmod_all, (0, chip * ada_cols), (N_DEV, ada_cols))
    ada_out = _ada_bwd_adam(jnp.transpose(c_all), dmod_blk, w_ada[0], m_w_ada[0], v_w_ada[0])
    ada_g, ada_delta, ada_m, ada_v = [t[None] for t in ada_out]

    partial["w_in"] = _sum_chips(r_in, "sum_w_in")
    (swapped["w_in"],) = _sibling_swap([partial["w_in"]], "swap_w_in")
    big = {}
    for n, w, m, v in zip(["w_in", "w_out", "w_ff1", "w_ff2"], [w_in, w_out, w_ff1, w_ff2],
                          [m_w_in, m_w_out, m_w_ff1, m_w_ff2], [v_w_in, v_w_out, v_w_ff1, v_w_ff2]):
        mine, theirs = partial[n], swapped[n]
        if n == "w_in":
            out = _adam_pair(jnp.transpose(w[0]), mine, theirs, jnp.transpose(m[0]), jnp.transpose(v[0]), "adam_" + n)
            big[n] = [jnp.transpose(t)[None] for t in out]
        else:
            big[n] = [t[None] for t in _adam_pair(w[0], mine, theirs, m[0], v[0], "adam_" + n)]

    def assemble(ada, smalls, k):
        b_ada_o, ln1w_o, ln1b_o, ln2w_o, ln2b_o, ret_o, gb_o, gln_o, gw_o = smalls
        return [ada, b_ada_o, big["w_in"][k], ret_o, gw_o, gb_o, gln_o, big["w_out"][k], ln1w_o, ln1b_o,
                big["w_ff1"][k], big["w_ff2"][k], ln2w_o, ln2b_o]

    grads = assemble(ada_g, small_grads, 0)
    deltas = assemble(ada_delta, sm_delta, 1)
    new_m = assemble(ada_m, sm_m, 2)
    new_v = assemble(ada_v, sm_v, 3)
    return (loss, grad_x[None], *grads, *deltas, *new_m, *new_v)
```

```python
import functools

import numpy as np
import jax
import jax.numpy as jnp
from jax import lax
from jax.experimental import pallas as pl
from jax.experimental.pallas import tpu as pltpu

D_MODEL = 1024
D_FF = 4096
CHUNK = 64
N_HEADS = 4
HEAD_W = 128
GLA_DK = 64
GATE_RANK = 16
GATE_TAU = 16.0
LN_EPS = 1e-5
ALPHA = 2.0 ** 0.25
ROPE_BASE = 10000.0
RET_SCALE = float(HEAD_W) ** -0.5
GLA_SCALE = float(GLA_DK) ** -0.5

ADAM_LR = 0.001
ADAM_B1 = 0.9
ADAM_B2 = 0.999
ADAM_EPS = 1e-08
ADAM_WD = 0.01
ADAM_STEP = 10

OFF_RQ, OFF_RK, OFF_RV, OFF_RG = 0, 512, 1024, 1536
OFF_GQ, OFF_GK, OFF_GV, OFF_GG, OFF_LR = 2048, 2560, 3072, 3584, 4096
N_PROJ = 4224
N_PROJ_SRC = 3600

N_DEV = 8
N_CHIP = 4
MESH = pl.DeviceIdType.MESH
MXU_DTYPE = jnp.bfloat16
WIRE_DTYPE = jnp.bfloat16
VMEM_LIMIT = 60 * 1024 * 1024
TOKEN_TILE = 256
INPROJ_TOKEN_TILE = 512
CHUNKS_PER_STEP = 8
CHUNKS_IN_LOCKSTEP = 4
GRAD_TOKEN_TILE = 2048
ELEMENTWISE_COLS = 256
HIGHEST = lax.Precision.HIGHEST


def _mm(a, b):
    return jnp.dot(a.astype(MXU_DTYPE), b.astype(MXU_DTYPE), preferred_element_type=jnp.float32)


def _mm_nt(a, b):
    return lax.dot_general(a.astype(MXU_DTYPE), b.astype(MXU_DTYPE), (((1,), (1,)), ((), ())),
                           preferred_element_type=jnp.float32)


def _mm_tn(a, b):
    return lax.dot_general(a.astype(MXU_DTYPE), b.astype(MXU_DTYPE), (((0,), (0,)), ((), ())),
                           preferred_element_type=jnp.float32)


def _mm32(a, b):
    return jnp.dot(a, b, precision=HIGHEST, preferred_element_type=jnp.float32)


def _running_sum(mask, a):
    m = mask.astype(jnp.bfloat16)
    hi = a.astype(jnp.bfloat16)
    rest = a - hi.astype(jnp.float32)
    mid = rest.astype(jnp.bfloat16)
    lo = (rest - mid.astype(jnp.float32)).astype(jnp.bfloat16)
    dot = lambda t: jnp.dot(m, t, preferred_element_type=jnp.float32)
    return dot(hi) + dot(mid) + dot(lo)


def _rowmean(a):
    return jnp.mean(a, axis=-1, keepdims=True)


def _colsum(a):
    return jnp.sum(a, axis=0, keepdims=True)


def _ln(z):
    zc = z - _rowmean(z)
    rstd = lax.rsqrt(_rowmean(zc * zc) + LN_EPS)
    return zc * rstd, rstd


def _ln_bwd(dzh, zh, rstd):
    return rstd * (dzh - _rowmean(dzh) - zh * _rowmean(dzh * zh))


def _sigmoid(a):
    return 1.0 / (1.0 + jnp.exp(-a))


def _log_sigmoid(a):
    return jnp.minimum(a, 0.0) - jnp.log(1.0 + jnp.exp(-jnp.abs(a)))


def _swap_halves(a):
    return pltpu.roll(a, HEAD_W // 2, 1)


def _tri_masks():
    row = lax.broadcasted_iota(jnp.int32, (CHUNK, CHUNK), 0)
    col = lax.broadcasted_iota(jnp.int32, (CHUNK, CHUNK), 1)
    return row, col


def _const_spec(shape):
    zeros = (0,) * len(shape)
    return pl.BlockSpec(shape, lambda *_: zeros, pipeline_mode=pl.Buffered(1))


def _params(semantics):
    return pltpu.CompilerParams(dimension_semantics=semantics, vmem_limit_bytes=VMEM_LIMIT)


def _decay_tables():
    log_gamma = np.log(1.0 - 2.0 ** (-5.0 - np.arange(N_HEADS, dtype=np.float64)))
    idx = np.arange(CHUNK, dtype=np.float64)
    dist = np.abs(idx[:, None] - idx[None, :])
    intra = np.exp(log_gamma[:, None, None] * dist)
    kdec = np.exp(log_gamma[None, :] * (CHUNK - 1.0 - idx)[:, None])
    qdec = np.exp(log_gamma[None, :] * (idx + 1.0)[:, None])
    chunk_decay = np.exp(log_gamma * CHUNK)
    lanes = lambda t: np.repeat(t, HEAD_W, axis=1).astype(np.float32)
    return (jnp.asarray(intra.astype(np.float32)), jnp.asarray(lanes(qdec)), jnp.asarray(lanes(kdec)),
            [float(np.float32(v)) for v in chunk_decay])


def _rotary_tables(seq):
    half = HEAD_W // 2
    inv = 1.0 / (ROPE_BASE ** jnp.linspace(0.0, 1.0, half, dtype=jnp.float32))
    both = lambda t: jnp.concatenate([t, t], axis=-1)
    ang_a = jnp.arange(0, seq, CHUNK, dtype=jnp.float32)[:, None] * inv[None, :]
    rot_a = jnp.stack([both(jnp.cos(ang_a)), both(jnp.sin(ang_a))], axis=1)
    rot_a = jnp.pad(rot_a, ((0, 0), (0, 6), (0, 0)))
    ang_b = jnp.arange(CHUNK, dtype=jnp.float32)[:, None] * inv[None, :]
    cos_b, sin_b = both(jnp.cos(ang_b)), both(jnp.sin(ang_b))
    sign = jnp.concatenate([-jnp.ones((half,), jnp.float32), jnp.ones((half,), jnp.float32)])
    return rot_a, jnp.stack([cos_b, sin_b, cos_b * sign, sin_b * sign])


def _rotary_chunk(ra_ref, c, rb_ref):
    cos_a, sin_a = ra_ref[c, 0:1, :], ra_ref[c, 1:2, :]
    return cos_a * rb_ref[0] - sin_a * rb_ref[1], sin_a * rb_ref[2] + cos_a * rb_ref[3]


def _mesh_pos():
    return lax.axis_index("x"), lax.axis_index("y"), lax.axis_index("c")


def _flip(v, bit):
    return 1 - v if bit else v


def _gather_rows(v, name):
    rows = v.shape[0]

    def body(v_ref, out_ref, send_sems, recv_sems):
        x, y, c = _mesh_pos()
        me = 4 * x + 2 * y + c
        out_ref[me] = v_ref[...]
        sends, recvs = [], []
        for k in range(1, N_DEV):
            px, py, pc = _flip(x, (k >> 2) & 1), _flip(y, (k >> 1) & 1), _flip(c, k & 1)
            peer = 4 * px + 2 * py + pc
            sends.append(pltpu.make_async_remote_copy(
                src_ref=v_ref, dst_ref=out_ref.at[me], send_sem=send_sems.at[k - 1], recv_sem=recv_sems.at[k - 1],
                device_id=(px, py, pc), device_id_type=MESH))
            recvs.append(pltpu.make_async_remote_copy(
                src_ref=v_ref, dst_ref=out_ref.at[peer], send_sem=send_sems.at[k - 1], recv_sem=recv_sems.at[k - 1],
                device_id=(px, py, pc), device_id_type=MESH))
        for cp in sends:
            cp.start()
        for cp in recvs:
            cp.wait_recv()
        for cp in sends:
            cp.wait_send()

    return pl.pallas_call(
        body, name=name,
        out_shape=jax.ShapeDtypeStruct((N_DEV, rows, 128), jnp.float32),
        in_specs=[pl.BlockSpec(memory_space=pltpu.VMEM)],
        out_specs=pl.BlockSpec(memory_space=pltpu.VMEM),
        scratch_shapes=[pltpu.SemaphoreType.DMA((N_DEV - 1,)), pltpu.SemaphoreType.DMA((N_DEV - 1,))],
    )(v)


def _chip_gather(arrays, name):
    n = len(arrays)

    def body(*refs):
        gather = _ChipGather(refs[:n], refs[n:2 * n], refs[2 * n:])
        gather.start()
        gather.forward()
        gather.finish()

    return pl.pallas_call(
        body, name=name,
        out_shape=_exchange_out_shapes(arrays, True),
        in_specs=[pl.BlockSpec(memory_space=pl.ANY)] * n,
        out_specs=tuple(pl.BlockSpec(memory_space=pl.ANY) for _ in arrays),
        scratch_shapes=_gather_sems(n),
    )(*arrays)


def _exchange_out_shapes(arrays, gather):
    return tuple(jax.ShapeDtypeStruct((N_CHIP,) + a.shape if gather else a.shape, a.dtype) for a in arrays)


def _scatter_sems(n):
    n_sem = n * (N_CHIP - 1)
    return [pltpu.SemaphoreType.DMA((n_sem,)), pltpu.SemaphoreType.DMA((n_sem,)), pltpu.SemaphoreType.DMA((n,))]


def _gather_sems(n):
    n_sem = n * (N_CHIP - 1)
    return [pltpu.SemaphoreType.DMA((n_sem,))] * 4 + [pltpu.SemaphoreType.DMA((n,))]


def _peer_chips(x, y):
    out = []
    for k in range(1, N_CHIP):
        px, py = _flip(x, (k >> 1) & 1), _flip(y, k & 1)
        out.append((px, py, 2 * px + py))
    return out


class _ChipScatter:
    def __init__(self, ins, outs, sems):
        send_sems, recv_sems, local_sems = sems
        x, y, c = _mesh_pos()
        chip = 2 * x + y
        self.local, self.sends, self.recvs = [], [], []
        for i in range(len(ins)):
            self.local.append(pltpu.make_async_copy(ins[i].at[chip], outs[i].at[chip], local_sems.at[i]))
            for k, (px, py, peer_chip) in enumerate(_peer_chips(x, y)):
                sem = i * (N_CHIP - 1) + k
                src = ins[i].at[peer_chip]
                self.sends.append(pltpu.make_async_remote_copy(
                    src_ref=src, dst_ref=outs[i].at[chip], send_sem=send_sems.at[sem], recv_sem=recv_sems.at[sem],
                    device_id=(px, py, c), device_id_type=MESH))
                self.recvs.append(pltpu.make_async_remote_copy(
                    src_ref=src, dst_ref=outs[i].at[peer_chip], send_sem=send_sems.at[sem], recv_sem=recv_sems.at[sem],
                    device_id=(px, py, c), device_id_type=MESH))

    def start(self):
        for cp in self.local + self.sends:
            cp.start()

    def wait(self):
        for cp in self.recvs:
            cp.wait_recv()
        for cp in self.sends:
            cp.wait_send()
        for cp in self.local:
            cp.wait()


class _ChipGather:
    def __init__(self, ins, outs, sems):
        ici_send, ici_recv, d2d_send, d2d_recv, local_sems = sems
        x, y, c = _mesh_pos()
        chip = 2 * x + y
        self.local, self.ici_sends, self.ici_recvs, self.d2d_sends, self.d2d_recvs = [], [], [], [], []
        for i in range(len(ins)):
            half = ins[i].shape[-1] // 2
            assert half % 128 == 0
            lead = (slice(None),) * (len(ins[i].shape) - 1)
            mine = lead + (pl.ds(pl.multiple_of(c * half, 128), half),)
            theirs = lead + (pl.ds(pl.multiple_of((1 - c) * half, 128), half),)
            self.local.append(pltpu.make_async_copy(ins[i], outs[i].at[chip], local_sems.at[i]))
            for k, (px, py, peer_chip) in enumerate(_peer_chips(x, y)):
                sem = i * (N_CHIP - 1) + k
                self.ici_sends.append(pltpu.make_async_remote_copy(
                    src_ref=ins[i].at[mine], dst_ref=outs[i].at[chip].at[mine],
                    send_sem=ici_send.at[sem], recv_sem=ici_recv.at[sem], device_id=(px, py, c), device_id_type=MESH))
                landed = outs[i].at[peer_chip].at[mine]
                self.ici_recvs.append(pltpu.make_async_remote_copy(
                    src_ref=ins[i].at[mine], dst_ref=landed,
                    send_sem=ici_send.at[sem], recv_sem=ici_recv.at[sem], device_id=(px, py, c), device_id_type=MESH))
                self.d2d_sends.append(pltpu.make_async_remote_copy(
                    src_ref=landed, dst_ref=landed,
                    send_sem=d2d_send.at[sem], recv_sem=d2d_recv.at[sem], device_id=(x, y, 1 - c), device_id_type=MESH))
                self.d2d_recvs.append(pltpu.make_async_remote_copy(
                    src_ref=landed, dst_ref=outs[i].at[peer_chip].at[theirs],
                    send_sem=d2d_send.at[sem], recv_sem=d2d_recv.at[sem], device_id=(x, y, 1 - c), device_id_type=MESH))

    def start(self):
        for cp in self.local + self.ici_sends:
            cp.start()

    def forward(self):
        for landed, onward in zip(self.ici_recvs, self.d2d_sends):
            landed.wait_recv()
            onward.start()

    def finish(self):
        for cp in self.d2d_recvs:
            cp.wait_recv()
        for cp in self.d2d_sends + self.ici_sends:
            cp.wait_send()
        for cp in self.local:
            cp.wait()


def _sibling_swap(arrays, name):
    n = len(arrays)

    def body(*refs):
        swap = _SiblingSwap(refs[:n], refs[n:2 * n], refs[2 * n:])
        swap.start()
        swap.wait()

    return pl.pallas_call(
        body, name=name,
        out_shape=tuple(jax.ShapeDtypeStruct(a.shape, a.dtype) for a in arrays),
        in_specs=[pl.BlockSpec(memory_space=pl.ANY)] * n,
        out_specs=tuple(pl.BlockSpec(memory_space=pl.ANY) for _ in arrays),
        scratch_shapes=_swap_sems(n),
    )(*arrays)


def _swap_sems(n):
    return [pltpu.SemaphoreType.DMA((n,)), pltpu.SemaphoreType.DMA((n,))]


class _SiblingSwap:
    def __init__(self, ins, outs, sems):
        send_sems, recv_sems = sems
        x, y, c = _mesh_pos()
        self.copies = [pltpu.make_async_remote_copy(
            src_ref=ins[i], dst_ref=outs[i], send_sem=send_sems.at[i], recv_sem=recv_sems.at[i],
            device_id=(x, y, 1 - c), device_id_type=MESH) for i in range(len(ins))]

    def start(self):
        for cp in self.copies:
            cp.start()

    def wait(self):
        for cp in self.copies:
            cp.wait_recv()
        for cp in self.copies:
            cp.wait_send()


def _ada_fwd(c_all, w_ada_blk, b_blk):
    cols = w_ada_blk.shape[1]

    def body(c_ref, w_ref, b_ref, out_ref):
        cv = c_ref[...]
        out_ref[...] = _mm32(cv * _sigmoid(cv), w_ref[...]) + b_ref[...]

    return pl.pallas_call(
        body, name="ada_fwd",
        out_shape=jax.ShapeDtypeStruct((N_DEV, cols), jnp.float32),
        compiler_params=pltpu.CompilerParams(vmem_limit_bytes=VMEM_LIMIT),
    )(c_all, w_ada_blk, b_blk)


def _adam(w, g, m, v):
    m2 = ADAM_B1 * m + (1.0 - ADAM_B1) * g
    v2 = ADAM_B2 * v + (1.0 - ADAM_B2) * (g * g)
    m_hat = m2 / (1.0 - ADAM_B1 ** ADAM_STEP)
    v_hat = v2 / (1.0 - ADAM_B2 ** ADAM_STEP)
    delta = -ADAM_LR * (m_hat / (jnp.sqrt(v_hat) + ADAM_EPS) + ADAM_WD * w)
    return delta, m2, v2


def _ada_bwd_adam(c_t, dmod_blk, w, m, v):
    rows, cols = w.shape
    tile = 512
    assert cols % tile == 0

    def body(c_ref, d_ref, w_ref, m_ref, v_ref, g_ref, dl_ref, m2_ref, v2_ref):
        sc = c_ref[...]
        sc = sc * _sigmoid(sc)
        dm = d_ref[...]
        g = sc[:, 0:1] * dm[0:1, :]
        for b in range(1, N_DEV):
            g = g + sc[:, b:b + 1] * dm[b:b + 1, :]
        delta, m2, v2 = _adam(w_ref[...], g, m_ref[...], v_ref[...])
        g_ref[...] = g
        dl_ref[...] = delta
        m2_ref[...] = m2
        v2_ref[...] = v2

    blk = pl.BlockSpec((rows, tile), lambda j: (0, j))
    out = jax.ShapeDtypeStruct((rows, cols), jnp.float32)
    return pl.pallas_call(
        body, name="ada_bwd_adam", grid=(cols // tile,),
        out_shape=(out, out, out, out),
        in_specs=[pl.BlockSpec((rows, N_DEV), lambda j: (0, 0)), pl.BlockSpec((N_DEV, tile), lambda j: (0, j)),
                  blk, blk, blk],
        out_specs=(blk, blk, blk, blk),
        compiler_params=_params(("arbitrary",)),
    )(c_t, dmod_blk, w, m, v)


def _inproj_fwd(x2, vecs, w_in_p, tm, riders):
    seq = x2.shape[0]
    n_tiles = seq // tm
    n_ride = len(riders)

    def body(*refs):
        x_ref, vec_ref, w_ref = refs[:3]
        ride_in, refs = refs[3:3 + n_ride], refs[3 + n_ride:]
        p_ref, u_ref = refs[:2]
        ride_out, sems = refs[2:2 + n_ride], refs[2 + n_ride:]
        gather = _ChipGather(ride_in, ride_out, sems)

        @pl.when(pl.program_id(0) == 0)
        def _():
            gather.start()

        xh, _ = _ln(x_ref[...])
        u = (xh * (1.0 + vec_ref[1:2, :]) + vec_ref[0:1, :]).astype(MXU_DTYPE)
        u_ref[...] = u
        p_ref[...] = _mm(u, w_ref[...])

        @pl.when(pl.program_id(0) == (3 * n_tiles) // 4)
        def _():
            gather.forward()

        @pl.when(pl.program_id(0) == n_tiles - 1)
        def _():
            gather.finish()

    hbm = pl.BlockSpec(memory_space=pl.ANY)
    return pl.pallas_call(
        body, name="inproj_fwd", grid=(n_tiles,),
        out_shape=(jax.ShapeDtypeStruct((seq, N_PROJ), jnp.float32), jax.ShapeDtypeStruct((seq, D_MODEL), MXU_DTYPE))
        + _exchange_out_shapes(riders, True),
        in_specs=[pl.BlockSpec((tm, D_MODEL), lambda i: (i, 0)), _const_spec(vecs.shape), _const_spec(w_in_p.shape)]
        + [hbm] * n_ride,
        out_specs=(pl.BlockSpec((tm, N_PROJ), lambda i: (i, 0)), pl.BlockSpec((tm, D_MODEL), lambda i: (i, 0)))
        + (hbm,) * n_ride,
        scratch_shapes=_gather_sems(n_ride),
        compiler_params=_params(("arbitrary",)),
    )(x2, vecs, w_in_p, *riders)


def _inproj_bwd(dproj, x2, dxa, vecs, w_in_pt, tm, riders):
    seq = x2.shape[0]
    n_tiles = seq // tm
    n_ride = len(riders)

    def body(*refs):
        dp_ref, x_ref, dxa_ref, vec_ref, w_ref = refs[:5]
        ride_in, refs = refs[5:5 + n_ride], refs[5 + n_ride:]
        gx_ref, sums_ref = refs[:2]
        ride_out, sems = refs[2:2 + n_ride], refs[2 + n_ride:]
        exchange = _ChipScatter(ride_in, ride_out, sems)

        @pl.when(pl.program_id(0) == 0)
        def _():
            exchange.start()
            sums_ref[...] = jnp.zeros_like(sums_ref)

        du = _mm(dp_ref[...], w_ref[...])
        xh, rstd = _ln(x_ref[...])
        sums_ref[0:1, :] += _colsum(du)
        sums_ref[1:2, :] += _colsum(du * xh)
        gx_ref[...] = dxa_ref[...] + _ln_bwd(du * (1.0 + vec_ref[1:2, :]), xh, rstd)

        @pl.when(pl.program_id(0) == n_tiles - 1)
        def _():
            exchange.wait()

    tile = pl.BlockSpec((tm, D_MODEL), lambda i: (i, 0))
    hbm = pl.BlockSpec(memory_space=pl.ANY)
    return pl.pallas_call(
        body, name="inproj_bwd", grid=(n_tiles,),
        out_shape=(jax.ShapeDtypeStruct((seq, D_MODEL), jnp.float32), jax.ShapeDtypeStruct((8, D_MODEL), jnp.float32))
        + _exchange_out_shapes(riders, False),
        in_specs=[pl.BlockSpec((tm, N_PROJ), lambda i: (i, 0)), tile, tile, _const_spec(vecs.shape),
                  _const_spec(w_in_pt.shape)] + [hbm] * n_ride,
        out_specs=(tile, pl.BlockSpec((8, D_MODEL), lambda i: (0, 0))) + (hbm,) * n_ride,
        scratch_shapes=_scatter_sems(n_ride),
        compiler_params=_params(("arbitrary",)),
    )(dproj, x2, dxa, vecs, w_in_pt, *riders)


def _head(h):
    return slice(h * HEAD_W, (h + 1) * HEAD_W)


def _cols(ref, off, h):
    return ref[:, off + h * HEAD_W:off + (h + 1) * HEAD_W]


HEADS = range(N_HEADS)


def _mixer_chunk_forward(p_ref, cc, ss, dm_ref, qdec_ref, kdec_ref, wg_ref, bg_ref, states):
    row, col = _tri_masks()
    lower = row >= col
    f = {}
    f["glr"] = p_ref[:, OFF_LR:OFF_LR + HEAD_W]
    f["logit"] = _mm(f["glr"], wg_ref[...]) + bg_ref[...]
    rq = [_cols(p_ref, OFF_RQ, h) for h in HEADS]
    rk = [_cols(p_ref, OFF_RK, h) for h in HEADS]
    f["rv"] = [_cols(p_ref, OFF_RV, h) for h in HEADS]
    f["qr"] = [(rq[h] * cc + _swap_halves(rq[h]) * ss) * RET_SCALE for h in HEADS]
    f["kr"] = [rk[h] * cc + _swap_halves(rk[h]) * ss for h in HEADS]
    s_raw = [_mm_nt(f["qr"][h], f["kr"][h]) for h in HEADS]
    yield
    la = _log_sigmoid(f["logit"]) * (1.0 / GATE_TAU)
    b = _running_sum(lower, la)
    f["qd"] = [f["qr"][h] * qdec_ref[:, _head(h)] for h in HEADS]
    f["kd"] = [f["kr"][h] * kdec_ref[:, _head(h)] for h in HEADS]
    f["scores"] = [s_raw[h] * dm_ref[h] for h in HEADS]
    yield
    b_last = b[CHUNK - 1:CHUNK, :]
    b_mid = b[CHUNK // 2 - 1:CHUNK // 2, :]
    f["e"], f["ei"] = jnp.exp(b - b_mid), jnp.exp(b_mid - b)
    f["eb"], f["ek"], f["ebl"] = jnp.exp(b), jnp.exp(b_last - b), jnp.exp(b_last)
    gq = [_cols(p_ref, OFF_GQ, h) * GLA_SCALE for h in HEADS]
    gk = [_cols(p_ref, OFF_GK, h) for h in HEADS]
    f["gv"] = [_cols(p_ref, OFF_GV, h) for h in HEADS]
    f["q_e"] = [gq[h] * f["e"][:, _head(h)] for h in HEADS]
    f["q_i"] = [gq[h] * f["ei"][:, _head(h)] for h in HEADS]
    f["k_e"] = [gk[h] * f["e"][:, _head(h)] for h in HEADS]
    f["k_i"] = [gk[h] * f["ei"][:, _head(h)] for h in HEADS]
    low = [_mm_nt(f["q_e"][h], f["k_i"][h]) for h in HEADS]
    up = [_mm_nt(f["q_i"][h], f["k_e"][h]) for h in HEADS]
    yield
    f["att"] = [jnp.where(lower, low[h], up[h]) for h in HEADS]
    f["qb"] = [gq[h] * f["eb"][:, _head(h)] for h in HEADS]
    f["kb"] = [gk[h] * f["ek"][:, _head(h)] for h in HEADS]
    ret_state, gla_state_t = states()
    f["o_ret"] = [_mm(f["scores"][h], f["rv"][h]) + _mm(f["qd"][h], ret_state[h]) for h in HEADS]
    f["o_gla"] = [_mm(f["att"][h], f["gv"][h]) + _mm_nt(f["qb"][h], gla_state_t[h]) for h in HEADS]
    return f


def _interleave(generators):
    live = list(generators)
    while live:
        for g in list(live):
            try:
                next(g)
            except StopIteration:
                live.remove(g)


def _mixer_fwd(proj, tables, wg_p, bg_p, ret_norm_w, gla_norm_w, riders):
    seq = proj.shape[0]
    n_chunks = seq // CHUNK
    per_step = min(n_chunks, CHUNKS_PER_STEP)
    n_steps = n_chunks // per_step
    n_ride = len(riders)
    rot_a, rot_b, dm_t, qdec_t, kdec_t, chunk_decay = tables

    def body(*refs):
        p_ref, ra_ref, rb_ref, dm_ref, qdec_ref, kdec_ref, wg_ref, bg_ref, wr_ref, wl_ref = refs[:10]
        ride_in, refs = refs[10:10 + n_ride], refs[10 + n_ride:]
        mix_ref, rsave_ref, ssave_ref = refs[:3]
        ride_out, refs = refs[3:3 + n_ride], refs[3 + n_ride:]
        r_sc, s_sc = refs[:2]
        gather = _ChipGather(ride_in, ride_out, refs[2:])

        @pl.when(pl.program_id(0) == 0)
        def _():
            gather.start()
            r_sc[...] = jnp.zeros_like(r_sc)
            s_sc[...] = jnp.zeros_like(s_sc)

        def one_chunk(c):
            p_c = p_ref.at[c * CHUNK:(c + 1) * CHUNK, :]
            mix_c = mix_ref.at[c * CHUNK:(c + 1) * CHUNK, :]
            before = {}

            def states():
                before["ret"] = [r_sc[h] for h in HEADS]
                before["gla"] = [s_sc[h] for h in HEADS]
                for h in HEADS:
                    rsave_ref[c, h] = before["ret"][h]
                    ssave_ref[c, h] = before["gla"][h]
                return before["ret"], before["gla"]

            cc, ss = _rotary_chunk(ra_ref, c, rb_ref)
            f = yield from _mixer_chunk_forward(p_c, cc, ss, dm_ref, qdec_ref, kdec_ref, wg_ref, bg_ref, states)
            for h in HEADS:
                r_sc[h] = chunk_decay[h] * before["ret"][h] + _mm_tn(f["kd"][h], f["rv"][h])
            for h in HEADS:
                s_sc[h] = before["gla"][h] * f["ebl"][:, _head(h)] + _mm_tn(f["gv"][h], f["kb"][h])
            yield
            for h in HEADS:
                on, _ = _ln(f["o_ret"][h])
                g = _cols(p_c, OFF_RG, h)
                mix_c[:, _head(h)] = (on * wr_ref[:, _head(h)] * (g * _sigmoid(g))).astype(mix_ref.dtype)
            for h in HEADS:
                o = f["o_gla"][h]
                on = o * lax.rsqrt(_rowmean(o * o) + LN_EPS)
                g = _cols(p_c, OFF_GG, h)
                mix_c[:, _head(N_HEADS + h)] = (on * wl_ref[:, _head(h)] * (g * _sigmoid(g))).astype(mix_ref.dtype)

        for c0 in range(0, per_step, CHUNKS_IN_LOCKSTEP):
            _interleave([one_chunk(c) for c in range(c0, min(per_step, c0 + CHUNKS_IN_LOCKSTEP))])

        @pl.when(pl.program_id(0) == (3 * n_steps) // 4)
        def _():
            gather.forward()

        @pl.when(pl.program_id(0) == n_steps - 1)
        def _():
            gather.finish()

    state_shape = (n_chunks, N_HEADS, HEAD_W, HEAD_W)
    state_blk = pl.BlockSpec((per_step, N_HEADS, HEAD_W, HEAD_W), lambda i: (i, 0, 0, 0))
    rot_blk = pl.BlockSpec((per_step, 8, HEAD_W), lambda i: (i, 0, 0))
    rows = per_step * CHUNK
    hbm = pl.BlockSpec(memory_space=pl.ANY)
    return pl.pallas_call(
        body, name="mixer_fwd", grid=(n_steps,),
        out_shape=(jax.ShapeDtypeStruct((seq, D_MODEL), MXU_DTYPE),
                   jax.ShapeDtypeStruct(state_shape, jnp.float32), jax.ShapeDtypeStruct(state_shape, jnp.float32))
        + _exchange_out_shapes(riders, True),
        in_specs=[pl.BlockSpec((rows, N_PROJ), lambda i: (i, 0)), rot_blk, _const_spec(rot_b.shape),
                  _const_spec(dm_t.shape), _const_spec(qdec_t.shape), _const_spec(kdec_t.shape),
                  _const_spec(wg_p.shape), _const_spec(bg_p.shape), _const_spec(ret_norm_w.shape),
                  _const_spec(gla_norm_w.shape)] + [hbm] * n_ride,
        out_specs=(pl.BlockSpec((rows, D_MODEL), lambda i: (i, 0)), state_blk, state_blk) + (hbm,) * n_ride,
        scratch_shapes=[pltpu.VMEM((N_HEADS, HEAD_W, HEAD_W), jnp.float32),
                        pltpu.VMEM((N_HEADS, HEAD_W, HEAD_W), jnp.float32)] + _gather_sems(n_ride),
        compiler_params=_params(("arbitrary",)),
    )(proj, rot_a, rot_b, dm_t, qdec_t, kdec_t, wg_p, bg_p, ret_norm_w, gla_norm_w, *riders)


def _mixer_bwd(proj, dmixed, rsave, ssave, tables, wg_p, bg_p, ret_norm_w, gla_norm_w, riders):
    seq = proj.shape[0]
    n_chunks = seq // CHUNK
    per_step = min(n_chunks, CHUNKS_PER_STEP)
    n_steps = n_chunks // per_step
    n_ride = len(riders)
    rot_a, rot_b, dm_t, qdec_t, kdec_t, chunk_decay = tables
    last = n_steps - 1

    def body(*refs):
        p_blk, dmx_blk = refs[:2]
        shared_in = refs[2:13]
        ride_in, refs = refs[13:13 + n_ride], refs[13 + n_ride:]
        dp_blk, dwr_ref, dwl_ref, dwg_ref, dbg_ref = refs[:5]
        ride_out, refs = refs[5:5 + n_ride], refs[5 + n_ride:]
        dr_sc, ds_sc = refs[:2]
        exchange = _ChipScatter(ride_in, ride_out, refs[2:])

        @pl.when(pl.program_id(0) == 0)
        def _():
            exchange.start()
            dr_sc[...] = jnp.zeros_like(dr_sc)
            ds_sc[...] = jnp.zeros_like(ds_sc)
            dwr_ref[...] = jnp.zeros_like(dwr_ref)
            dwl_ref[...] = jnp.zeros_like(dwl_ref)
            dwg_ref[...] = jnp.zeros_like(dwg_ref)
            dbg_ref[...] = jnp.zeros_like(dbg_ref)

        def chunk_stages(c):
            rows = slice(c * CHUNK, (c + 1) * CHUNK)
            return one_chunk(c, p_blk.at[rows, :], dmx_blk.at[rows, :], dp_blk.at[rows, :], *shared_in,
                             dwr_ref, dwl_ref, dwg_ref, dbg_ref, dr_sc, ds_sc)

        for c0 in range(per_step, 0, -CHUNKS_IN_LOCKSTEP):
            _interleave([chunk_stages(c) for c in reversed(range(max(0, c0 - CHUNKS_IN_LOCKSTEP), c0))])

        @pl.when(pl.program_id(0) == last)
        def _():
            exchange.wait()

    def one_chunk(c, p_ref, dmx_ref, dp_ref, rsave_ref, ssave_ref, ra_ref, rb_ref, dm_ref, qdec_ref, kdec_ref,
                  wg_ref, bg_ref, wr_ref, wl_ref, dwr_ref, dwl_ref, dwg_ref, dbg_ref, dr_sc, ds_sc):
        def put(off, h, val):
            dp_ref[:, off + h * HEAD_W:off + (h + 1) * HEAD_W] = val.astype(dp_ref.dtype)

        cc, ss = _rotary_chunk(ra_ref, c, rb_ref)
        row, col = _tri_masks()
        ret_state = [rsave_ref[c, h] for h in HEADS]
        gla_state_t = [ssave_ref[c, h] for h in HEADS]
        f = yield from _mixer_chunk_forward(p_ref, cc, ss, dm_ref, qdec_ref, kdec_ref, wg_ref, bg_ref,
                                            lambda: (ret_state, gla_state_t))
        yield

        do_ret, do_gla = [], []
        for h in HEADS:
            on, rstd = _ln(f["o_ret"][h])
            g = _cols(p_ref, OFF_RG, h)
            sg = _sigmoid(g)
            dy = dmx_ref[:, _head(h)].astype(jnp.float32)
            wr = wr_ref[:, _head(h)]
            dwr_ref[:, _head(h)] += _colsum(dy * on * (g * sg))
            put(OFF_RG, h, dy * on * wr * (sg * (1.0 + g * (1.0 - sg))))
            do_ret.append(_ln_bwd(dy * wr * (g * sg), on, rstd))
        for h in HEADS:
            o = f["o_gla"][h]
            rstd = lax.rsqrt(_rowmean(o * o) + LN_EPS)
            on = o * rstd
            g = _cols(p_ref, OFF_GG, h)
            sg = _sigmoid(g)
            dy = dmx_ref[:, _head(N_HEADS + h)].astype(jnp.float32)
            wl = wl_ref[:, _head(h)]
            dwl_ref[:, _head(h)] += _colsum(dy * on * (g * sg))
            put(OFF_GG, h, dy * on * wl * (sg * (1.0 + g * (1.0 - sg))))
            don = dy * wl * (g * sg)
            do_gla.append(rstd * (don - on * _rowmean(don * on)))

        yield

        d_ret_new = [dr_sc[h] for h in HEADS]
        d_gla_new = [ds_sc[h] for h in HEADS]
        ds_raw = [_mm_nt(do_ret[h], f["rv"][h]) * dm_ref[h] for h in HEADS]
        d_att = [_mm_nt(do_gla[h], f["gv"][h]) for h in HEADS]
        dq_state = [_mm_nt(do_ret[h], ret_state[h]) for h in HEADS]
        dk_state = [_mm_nt(f["rv"][h], d_ret_new[h]) for h in HEADS]
        dqb = [_mm(do_gla[h], gla_state_t[h]) for h in HEADS]
        dkb = [_mm(f["gv"][h], d_gla_new[h]) for h in HEADS]
        for h in HEADS:
            put(OFF_RV, h, _mm_tn(f["scores"][h], do_ret[h]) + _mm(f["kd"][h], d_ret_new[h]))
        for h in HEADS:
            put(OFF_GV, h, _mm_tn(f["att"][h], do_gla[h]) + _mm_nt(f["kb"][h], d_gla_new[h]))
        for h in HEADS:
            dr_sc[h] = chunk_decay[h] * d_ret_new[h] + _mm_tn(f["qd"][h], do_ret[h])
        for h in HEADS:
            ds_sc[h] = d_gla_new[h] * f["ebl"][:, _head(h)] + _mm_tn(do_gla[h], f["qb"][h])
        yield

        dqr = [_mm(ds_raw[h], f["kr"][h]) + dq_state[h] * qdec_ref[:, _head(h)] for h in HEADS]
        dkr = [_mm_tn(ds_raw[h], f["qr"][h]) + dk_state[h] * kdec_ref[:, _head(h)] for h in HEADS]
        d_low = [jnp.where(row >= col, d_att[h], 0.0) for h in HEADS]
        d_up = [jnp.where(row < col, d_att[h], 0.0) for h in HEADS]
        dq_e = [_mm(d_low[h], f["k_i"][h]) for h in HEADS]
        dk_i = [_mm_tn(d_low[h], f["q_e"][h]) for h in HEADS]
        dq_i = [_mm(d_up[h], f["k_e"][h]) for h in HEADS]
        dk_e = [_mm_tn(d_up[h], f["q_i"][h]) for h in HEADS]
        yield
        for h in HEADS:
            put(OFF_RQ, h, (dqr[h] * cc + _swap_halves(dqr[h] * ss)) * RET_SCALE)
            put(OFF_RK, h, dkr[h] * cc + _swap_halves(dkr[h] * ss))
        row_id = lax.broadcasted_iota(jnp.int32, (CHUNK, HEAD_W), 0)
        db_heads = []
        for h in HEADS:
            hs = _head(h)
            e, ei, eb, ek, ebl = f["e"][:, hs], f["ei"][:, hs], f["eb"][:, hs], f["ek"][:, hs], f["ebl"][:, hs]
            put(OFF_GQ, h, (dq_e[h] * e + dq_i[h] * ei + dqb[h] * eb) * GLA_SCALE)
            put(OFF_GK, h, dk_e[h] * e + dk_i[h] * ei + dkb[h] * ek)
            db = (dq_e[h] * f["q_e"][h] - dq_i[h] * f["q_i"][h] + dk_e[h] * f["k_e"][h] - dk_i[h] * f["k_i"][h]
                  + dqb[h] * f["qb"][h] - dkb[h] * f["kb"][h])
            db_last = _colsum(dkb[h] * f["kb"][h]) + ebl * _colsum(gla_state_t[h] * d_gla_new[h])
            db_heads.append(db + jnp.where(row_id == CHUNK - 1, db_last, 0.0))
        db = jnp.concatenate(db_heads, axis=1)
        d_la = _running_sum(col >= row, db)
        d_logit = d_la * (1.0 / GATE_TAU) * (1.0 - _sigmoid(f["logit"]))
        put(OFF_LR, 0, _mm_nt(d_logit, wg_ref[...]))
        dwg_ref[...] += _mm_tn(f["glr"], d_logit)
        dbg_ref[...] += _colsum(d_logit)

    state_blk = pl.BlockSpec((per_step, N_HEADS, HEAD_W, HEAD_W), lambda i: (last - i, 0, 0, 0))
    rot_blk = pl.BlockSpec((per_step, 8, HEAD_W), lambda i: (last - i, 0, 0))
    width = N_HEADS * HEAD_W
    vec_out = pl.BlockSpec((1, width), lambda i: (0, 0))
    hbm = pl.BlockSpec(memory_space=pl.ANY)
    rows_blk = per_step * CHUNK
    return pl.pallas_call(
        body, name="mixer_bwd", grid=(n_steps,),
        out_shape=(jax.ShapeDtypeStruct((seq, N_PROJ), MXU_DTYPE),
                   jax.ShapeDtypeStruct((1, width), jnp.float32), jax.ShapeDtypeStruct((1, width), jnp.float32),
                   jax.ShapeDtypeStruct((HEAD_W, width), jnp.float32), jax.ShapeDtypeStruct((1, width), jnp.float32))
        + _exchange_out_shapes(riders, False),
        in_specs=[pl.BlockSpec((rows_blk, N_PROJ), lambda i: (last - i, 0)),
                  pl.BlockSpec((rows_blk, D_MODEL), lambda i: (last - i, 0)), state_blk, state_blk, rot_blk,
                  _const_spec(rot_b.shape),
                  _const_spec(dm_t.shape), _const_spec(qdec_t.shape), _const_spec(kdec_t.shape),
                  _const_spec(wg_p.shape), _const_spec(bg_p.shape), _const_spec(ret_norm_w.shape),
                  _const_spec(gla_norm_w.shape)] + [hbm] * n_ride,
        out_specs=(pl.BlockSpec((rows_blk, N_PROJ), lambda i: (last - i, 0)), vec_out, vec_out,
                   pl.BlockSpec((HEAD_W, width), lambda i: (0, 0)), vec_out) + (hbm,) * n_ride,
        scratch_shapes=[pltpu.VMEM((N_HEADS, HEAD_W, HEAD_W), jnp.float32),
                        pltpu.VMEM((N_HEADS, HEAD_W, HEAD_W), jnp.float32)] + _scatter_sems(n_ride),
        compiler_params=_params(("arbitrary",)),
    )(proj, dmixed, rsave, ssave, rot_a, rot_b, dm_t, qdec_t, kdec_t, wg_p, bg_p, ret_norm_w, gla_norm_w, *riders)


V_GATE1, V_SCALE2, V_SHIFT2, V_GATE2, V_LN1W, V_LN1B, V_LN2W, V_LN2B = range(8)
S_GATE1, S_SCALE2, S_SHIFT2, S_GATE2, S_LN1W, S_LN1B, S_LN2W, S_LN2B, S_LOSS = range(9)


def _mlp_fwd_bwd(x2, mixed, target, vecs, w_out, w1_chunks, w2_chunks, tm):
    seq = x2.shape[0]
    n_fc, _, fc = w1_chunks.shape

    def body(x_ref, mx_ref, t_ref, vec_ref, wo_ref, w1_ref, w2_ref,
             dmx_ref, dxa_ref, a_ref, dh_ref, u2_ref, df_ref, dm_ref, sums_ref, relu_sc):
        @pl.when(pl.program_id(0) == 0)
        def _():
            sums_ref[...] = jnp.zeros_like(sums_ref)

        vec = lambda r: vec_ref[r:r + 1, :]

        def acc(r, val):
            sums_ref[r:r + 1, :] += _colsum(val)

        xx = x_ref[...]
        m = _mm(mx_ref[...], wo_ref[...])
        z1h, rstd1 = _ln(ALPHA * xx + vec(V_GATE1) * m)
        x1 = z1h * vec(V_LN1W) + vec(V_LN1B)
        x1h, rstd0 = _ln(x1)
        u2 = (x1h * (1.0 + vec(V_SCALE2)) + vec(V_SHIFT2)).astype(MXU_DTYPE)
        u2_ref[...] = u2
        f = jnp.zeros((tm, D_MODEL), jnp.float32)
        for j in range(n_fc):
            r = jnp.maximum(_mm(u2, w1_ref[j]), 0.0)
            relu_sc[:, j * fc:(j + 1) * fc] = r
            a = (r * r).astype(MXU_DTYPE)
            a_ref[:, j * fc:(j + 1) * fc] = a
            f = f + _mm(a, w2_ref[j])
        z2h, rstd2 = _ln(ALPHA * x1 + vec(V_GATE2) * f)
        err = z2h * vec(V_LN2W) + vec(V_LN2B) - t_ref[...]
        acc(S_LOSS, err * err)
        dy = err * (1.0 / D_MODEL)
        acc(S_LN2W, dy * z2h)
        acc(S_LN2B, dy)
        dz2 = _ln_bwd(dy * vec(V_LN2W), z2h, rstd2)
        acc(S_GATE2, dz2 * f)
        df = (vec(V_GATE2) * dz2).astype(MXU_DTYPE)
        df_ref[...] = df
        du2 = jnp.zeros((tm, D_MODEL), jnp.float32)
        for j in range(n_fc):
            dh = (_mm_nt(df, w2_ref[j]) * (2.0 * relu_sc[:, j * fc:(j + 1) * fc])).astype(MXU_DTYPE)
            dh_ref[:, j * fc:(j + 1) * fc] = dh
            du2 = du2 + _mm_nt(dh, w1_ref[j])
        acc(S_SCALE2, du2 * x1h)
        acc(S_SHIFT2, du2)
        dx1 = ALPHA * dz2 + _ln_bwd(du2 * (1.0 + vec(V_SCALE2)), x1h, rstd0)
        acc(S_LN1W, dx1 * z1h)
        acc(S_LN1B, dx1)
        dz1 = _ln_bwd(dx1 * vec(V_LN1W), z1h, rstd1)
        acc(S_GATE1, dz1 * m)
        dxa_ref[...] = ALPHA * dz1
        dm = (vec(V_GATE1) * dz1).astype(MXU_DTYPE)
        dm_ref[...] = dm
        dmx_ref[...] = _mm_nt(dm, wo_ref[...])

    tile = lambda width: pl.BlockSpec((tm, width), lambda i: (i, 0))
    f32 = lambda width: jax.ShapeDtypeStruct((seq, width), jnp.float32)
    b16 = lambda width: jax.ShapeDtypeStruct((seq, width), MXU_DTYPE)
    return pl.pallas_call(
        body, name="mlp_fwd_bwd", grid=(seq // tm,),
        out_shape=(f32(D_MODEL), f32(D_MODEL), b16(D_FF), b16(D_FF), b16(D_MODEL), b16(D_MODEL), b16(D_MODEL),
                   jax.ShapeDtypeStruct((16, D_MODEL), jnp.float32)),
        in_specs=[tile(D_MODEL), tile(D_MODEL), tile(D_MODEL), _const_spec(vecs.shape), _const_spec(w_out.shape),
                  _const_spec(w1_chunks.shape), _const_spec(w2_chunks.shape)],
        out_specs=(tile(D_MODEL), tile(D_MODEL), tile(D_FF), tile(D_FF), tile(D_MODEL), tile(D_MODEL),
                   tile(D_MODEL), pl.BlockSpec((16, D_MODEL), lambda i: (0, 0))),
        scratch_shapes=[pltpu.VMEM((tm, D_FF), jnp.float32)],
        compiler_params=_params(("arbitrary",)),
    )(x2, mixed, target, vecs, w_out, w1_chunks, w2_chunks)


def _grad_matmul(a, b, name, tn, blocks_are_rows, riders=()):
    seq, m_dim = a.shape
    n_dim = b.shape[1]
    tk = min(seq, GRAD_TOKEN_TILE)
    nk = seq // tk
    n_ride = len(riders)
    if blocks_are_rows:
        tm = m_dim // N_CHIP
        assert tn == n_dim
        grid = (N_CHIP, 1, nk)
        out_map = lambda i, j, k: (i, 0, 0)
    else:
        tm = m_dim
        assert tn * N_CHIP == n_dim
        grid = (1, N_CHIP, nk)
        out_map = lambda i, j, k: (j, 0, 0)

    def body(*refs):
        a_ref, b_ref = refs[:2]
        ride_in, refs = refs[2:2 + n_ride], refs[2 + n_ride:]
        o_ref = refs[0]
        ride_out, refs = refs[1:1 + n_ride], refs[1 + n_ride:]
        acc_sc = refs[0]
        exchange = _ChipScatter(ride_in, ride_out, refs[1:]) if n_ride else None
        block = pl.program_id(0) + pl.program_id(1)
        k = pl.program_id(2)

        if exchange is not None:
            @pl.when((block == 0) & (k == 0))
            def _():
                exchange.start()

        @pl.when(k == 0)
        def _():
            acc_sc[...] = jnp.zeros_like(acc_sc)

        acc_sc[...] += _mm_tn(a_ref[...], b_ref[...])

        @pl.when(k == nk - 1)
        def _():
            o_ref[0] = acc_sc[...].astype(o_ref.dtype)

        if exchange is not None:
            @pl.when((block == N_CHIP - 1) & (k == nk - 1))
            def _():
                exchange.wait()

    hbm = pl.BlockSpec(memory_space=pl.ANY)
    out = pl.pallas_call(
        body, name=name, grid=grid,
        out_shape=(jax.ShapeDtypeStruct((N_CHIP, tm, tn), WIRE_DTYPE),) + _exchange_out_shapes(riders, False),
        in_specs=[pl.BlockSpec((tk, tm), lambda i, j, k: (k, i)), pl.BlockSpec((tk, tn), lambda i, j, k: (k, j))]
        + [hbm] * n_ride,
        out_specs=(pl.BlockSpec((1, tm, tn), out_map),) + (hbm,) * n_ride,
        scratch_shapes=[pltpu.VMEM((tm, tn), jnp.float32)] + (_scatter_sems(n_ride) if n_ride else []),
        compiler_params=_params(("arbitrary", "arbitrary", "arbitrary")),
    )(a, b, *riders)
    return out if n_ride else out[0]


def _grad_matmul_full(a, b, name, tm, riders):
    seq, m_dim = a.shape
    n_dim = b.shape[1]
    tk = min(seq, GRAD_TOKEN_TILE)
    nk = seq // tk
    n_blocks = m_dim // tm
    n_ride = len(riders)
    assert m_dim % tm == 0

    def body(*refs):
        a_ref, b_ref = refs[:2]
        ride_in, refs = refs[2:2 + n_ride], refs[2 + n_ride:]
        o_ref = refs[0]
        ride_out, refs = refs[1:1 + n_ride], refs[1 + n_ride:]
        acc_sc = refs[0]
        swap = _SiblingSwap(ride_in, ride_out, refs[1:])
        i, k = pl.program_id(0), pl.program_id(1)

        @pl.when((i == 0) & (k == 0))
        def _():
            swap.start()

        @pl.when(k == 0)
        def _():
            acc_sc[...] = jnp.zeros_like(acc_sc)

        acc_sc[...] += _mm_tn(a_ref[...], b_ref[...])

        @pl.when(k == nk - 1)
        def _():
            o_ref[...] = acc_sc[...].astype(o_ref.dtype)

        @pl.when((i == n_blocks - 1) & (k == nk - 1))
        def _():
            swap.wait()

    hbm = pl.BlockSpec(memory_space=pl.ANY)
    return pl.pallas_call(
        body, name=name, grid=(n_blocks, nk),
        out_shape=(jax.ShapeDtypeStruct((m_dim, n_dim), WIRE_DTYPE),)
        + tuple(jax.ShapeDtypeStruct(r.shape, r.dtype) for r in riders),
        in_specs=[pl.BlockSpec((tk, tm), lambda i, k: (k, i)), pl.BlockSpec((tk, n_dim), lambda i, k: (k, 0))]
        + [hbm] * n_ride,
        out_specs=(pl.BlockSpec((tm, n_dim), lambda i, k: (i, 0)),) + (hbm,) * n_ride,
        scratch_shapes=[pltpu.VMEM((tm, n_dim), jnp.float32)] + _swap_sems(n_ride),
        compiler_params=_params(("arbitrary", "arbitrary")),
    )(a, b, *riders)


def _sum_chips(stack, name):
    _, rows, cols = stack.shape
    tc = min(cols, ELEMENTWISE_COLS)

    def body(s_ref, o_ref):
        total = s_ref[0].astype(jnp.float32)
        for j in range(1, N_CHIP):
            total = total + s_ref[j].astype(jnp.float32)
        o_ref[...] = total

    return pl.pallas_call(
        body, name=name, grid=(cols // tc,),
        out_shape=jax.ShapeDtypeStruct((rows, cols), jnp.float32),
        in_specs=[pl.BlockSpec((N_CHIP, rows, tc), lambda i: (0, 0, i))],
        out_specs=pl.BlockSpec((rows, tc), lambda i: (0, i)),
        compiler_params=_params(("arbitrary",)),
    )(stack)


def _adam_pair(w, g_mine, g_sibling, m, v, name):
    rows, cols = w.shape
    tc = min(cols, ELEMENTWISE_COLS)

    def total(ref):
        if len(ref.shape) == 2:
            return ref[...]
        acc = ref[0].astype(jnp.float32)
        for j in range(1, ref.shape[0]):
            acc = acc + ref[j].astype(jnp.float32)
        return acc

    def body(w_ref, ga_ref, gb_ref, m_ref, v_ref, g_ref, dl_ref, m2_ref, v2_ref):
        g = total(ga_ref) + total(gb_ref)
        delta, m2, v2 = _adam(w_ref[...], g, m_ref[...], v_ref[...])
        g_ref[...] = g
        dl_ref[...] = delta
        m2_ref[...] = m2
        v2_ref[...] = v2

    blk = pl.BlockSpec((rows, tc), lambda i: (0, i))
    g_blk = lambda a: blk if a.ndim == 2 else pl.BlockSpec((a.shape[0], rows, tc), lambda i: (0, 0, i))
    out = jax.ShapeDtypeStruct((rows, cols), jnp.float32)
    return pl.pallas_call(
        body, name=name, grid=(cols // tc,),
        out_shape=(out, out, out, out),
        in_specs=[blk, g_blk(g_mine), g_blk(g_sibling), blk, blk], out_specs=(blk,) * 4,
        compiler_params=_params(("arbitrary",)),
    )(w, g_mine, g_sibling, m, v)


def _sum_devices(gathered):
    _, rows, _ = gathered.shape

    def body(g_ref, o_ref):
        total = g_ref[0]
        for d in range(1, N_DEV):
            total = total + g_ref[d]
        o_ref[...] = total

    return pl.pallas_call(
        body, name="sum_devices",
        out_shape=jax.ShapeDtypeStruct((rows, 128), jnp.float32),
    )(gathered)


def _adam_small(params):
    n = len(params)

    def body(*refs):
        ins, outs = refs[:4 * n], refs[4 * n:]
        for i in range(n):
            w_ref, g_ref, m_ref, v_ref = ins[4 * i:4 * i + 4]
            delta, m2, v2 = _adam(w_ref[...], g_ref[...], m_ref[...], v_ref[...])
            outs[3 * i][...] = delta
            outs[3 * i + 1][...] = m2
            outs[3 * i + 2][...] = v2

    out_shape = tuple(jax.ShapeDtypeStruct(p[0].shape, jnp.float32) for p in params for _ in range(3))
    out = pl.pallas_call(body, name="adam_small", out_shape=out_shape)(*[t for p in params for t in p])
    return [out[3 * i:3 * i + 3] for i in range(n)]


def _pad_heads(w):
    lead = w.shape[:-1]
    w = w.reshape(lead + (N_HEADS, GLA_DK))
    w = jnp.pad(w, [(0, 0)] * len(lead) + [(0, 0), (0, HEAD_W - GLA_DK)])
    return w.reshape(lead + (N_HEADS * HEAD_W,))


def _unpad_heads(w):
    lead = w.shape[:-1]
    return w.reshape(lead + (N_HEADS, HEAD_W))[..., :GLA_DK].reshape(lead + (N_HEADS * GLA_DK,))


def _pad_head_rows(w):
    w = w.reshape(N_HEADS, GLA_DK, w.shape[-1])
    return jnp.pad(w, ((0, 0), (0, HEAD_W - GLA_DK), (0, 0))).reshape(N_HEADS * HEAD_W, w.shape[-1])


def _unpad_head_rows(w):
    return w.reshape(N_HEADS, HEAD_W, w.shape[-1])[:, :GLA_DK].reshape(N_HEADS * GLA_DK, w.shape[-1])


def _pad_w_in_rows(w):
    return jnp.concatenate([
        w[:2048], _pad_head_rows(w[2048:2304]), _pad_head_rows(w[2304:2560]), w[2560:3584],
        jnp.pad(w[3584:3600], ((0, HEAD_W - GATE_RANK), (0, 0)))], axis=0)


def _unpad_w_in_rows(g):
    return jnp.concatenate([
        g[:2048], _unpad_head_rows(g[OFF_GQ:OFF_GQ + 512]), _unpad_head_rows(g[OFF_GK:OFF_GK + 512]),
        g[OFF_GV:OFF_LR], g[OFF_LR:OFF_LR + GATE_RANK]], axis=0)


def _col_major(w):
    return jnp.transpose(w, (2, 0, 1)).reshape(w.shape[2], w.shape[1])


def _rows128(a):
    return a.reshape(-1, 128)


def _rows8(a):
    a = a.reshape(-1, 128)
    return jnp.pad(a, ((0, -a.shape[0] % 8), (0, 0)))


def kernel(x, c, w_ada, b_ada, w_in, ret_norm_w, gla_gate_w, gla_gate_b, gla_norm_w, w_out, ln1_w, ln1_b, w_ff1, w_ff2, ln2_w, ln2_b, loss_target, m_w_ada, m_b_ada, m_w_in, m_ret_norm_w, m_gla_gate_w, m_gla_gate_b, m_gla_norm_w, m_w_out, m_ln1_w, m_ln1_b, m_w_ff1, m_w_ff2, m_ln2_w, m_ln2_b, v_w_ada, v_b_ada, v_w_in, v_ret_norm_w, v_gla_gate_w, v_gla_gate_b, v_gla_norm_w, v_w_out, v_ln1_w, v_ln1_b, v_w_ff1, v_w_ff2, v_ln2_w, v_ln2_b):
    seq = x.shape[1]
    tm = min(seq, TOKEN_TILE)
    tm_in = min(seq, INPROJ_TOKEN_TILE)
    xi, yi, ci = _mesh_pos()
    dev = 4 * xi + 2 * yi + ci
    chip = 2 * xi + yi
    x2, target = x[0], loss_target[0]
    ada_cols = w_ada.shape[2]
    in_cols = w_in.shape[2]
    gate_cols = gla_gate_w.shape[2]

    g0 = _gather_rows(jnp.concatenate([_rows128(c), _rows128(gla_gate_w[0])], axis=0), "gather_cond")
    c_all = g0[:, :8].reshape(N_DEV, D_MODEL)
    gate_w_full = jnp.concatenate([g0[2 * j, 8:16].reshape(GATE_RANK, gate_cols) for j in range(N_CHIP)], axis=1)
    wg_p = jnp.pad(_pad_heads(gate_w_full), ((0, HEAD_W - GATE_RANK), (0, 0)))
    bg_p = _pad_heads(gla_gate_b)

    b_blk = lax.dynamic_slice(b_ada, (0, chip * ada_cols), (1, ada_cols))
    mod_blk = _ada_fwd(c_all, w_ada[0], b_blk)
    g1 = _gather_rows(_rows128(mod_blk), "gather_mod")
    mod_all = jnp.concatenate([g1[2 * j].reshape(N_DEV, ada_cols) for j in range(N_CHIP)], axis=1)
    mod = lax.dynamic_slice(mod_all, (dev, 0), (1, 6 * D_MODEL))
    shift1, scale1, gate1, shift2, scale2, gate2 = [mod[:, i * D_MODEL:(i + 1) * D_MODEL] for i in range(6)]

    (w_in_stack,) = _chip_gather([_col_major(w_in).astype(WIRE_DTYPE)], "gather_w_in")
    w_in_pt = _pad_w_in_rows(w_in_stack.reshape(N_PROJ_SRC, D_MODEL)).astype(MXU_DTYPE)
    w_in_p = jnp.transpose(w_in_pt)

    zeros_row = jnp.zeros((1, D_MODEL), jnp.float32)
    vecs1 = jnp.concatenate([shift1, scale1] + [zeros_row] * 6, axis=0)
    proj, u, w2_stack = _inproj_fwd(x2, vecs1, w_in_p, tm_in, [w_ff2[0].astype(WIRE_DTYPE)])
    rot_a, rot_b = _rotary_tables(seq)
    dm_t, qdec_t, kdec_t, chunk_decay = _decay_tables()
    tables = (rot_a, rot_b, dm_t, qdec_t, kdec_t, chunk_decay)
    mixed, rsave, ssave, w_out_stack, w1_stack = _mixer_fwd(
        proj, tables, wg_p, bg_p, ret_norm_w, gla_norm_w,
        [w_out[0].astype(WIRE_DTYPE), w_ff1[0].astype(WIRE_DTYPE)])
    w_out_full = w_out_stack.reshape(D_MODEL, D_MODEL).astype(MXU_DTYPE)
    w1_chunks = w1_stack.astype(MXU_DTYPE)
    w2_chunks = w2_stack.astype(MXU_DTYPE)

    vecs2 = jnp.concatenate([gate1, scale2, shift2, gate2, ln1_w, ln1_b, ln2_w, ln2_b], axis=0)
    dmixed, dxa, act, dh, u2, df, dm, sums2 = _mlp_fwd_bwd(x2, mixed, target, vecs2, w_out_full, w1_chunks,
                                                           w2_chunks, tm)

    g_out_stack = _grad_matmul(mixed, dm, "grad_w_out", D_MODEL, True)
    g_ff1_stack, r_out = _grad_matmul(u2, dh, "grad_w_ff1", D_FF // N_CHIP, False, [g_out_stack])
    g_ff2_stack, r_ff1 = _grad_matmul(act, df, "grad_w_ff2", D_MODEL, True, [g_ff1_stack])
    dproj, d_ret_norm, d_gla_norm, d_wg_p, d_bg_p, r_ff2 = _mixer_bwd(
        proj, dmixed, rsave, ssave, tables, wg_p, bg_p, ret_norm_w, gla_norm_w, [g_ff2_stack])
    early = ["w_out", "w_ff1", "w_ff2"]
    partial = dict(zip(early, [r_out, r_ff1, r_ff2]))
    g_in_t, *swapped_early = _grad_matmul_full(dproj, u, "grad_w_in", N_PROJ // 3, [partial[n] for n in early])
    swapped = dict(zip(early, swapped_early))
    g_in_stack = _unpad_w_in_rows(g_in_t).reshape(N_CHIP, in_cols, D_MODEL)
    grad_x, sums1, r_in = _inproj_bwd(dproj, x2, dxa, vecs1, w_in_pt, tm_in, [g_in_stack])

    dmod = jnp.concatenate([sums1[0:1], sums1[1:2], sums2[S_GATE1:S_GATE1 + 1], sums2[S_SHIFT2:S_SHIFT2 + 1],
                            sums2[S_SCALE2:S_SCALE2 + 1], sums2[S_GATE2:S_GATE2 + 1]], axis=1)
    d_gate_w_full = _unpad_heads(d_wg_p[:GATE_RANK])
    flat = lambda parts: jnp.concatenate([_rows8(p) for p in parts], axis=0)
    small = flat([dmod, sums2[S_LN1W:S_LN1W + 1], sums2[S_LN1B:S_LN1B + 1], sums2[S_LN2W:S_LN2W + 1],
                  sums2[S_LN2B:S_LN2B + 1], d_ret_norm, _unpad_heads(d_bg_p), d_gla_norm, d_gate_w_full,
                  sums2[S_LOSS:S_LOSS + 1]])
    g2 = _gather_rows(small, "gather_small")
    tot = _sum_devices(g2)
    loss = 0.5 / D_MODEL * jnp.sum(tot[136:144])
    grad_b_ada = tot[0:48].reshape(1, 6 * D_MODEL)
    grad_ln1_w, grad_ln1_b = tot[48:56].reshape(1, D_MODEL), tot[56:64].reshape(1, D_MODEL)
    grad_ln2_w, grad_ln2_b = tot[64:72].reshape(1, D_MODEL), tot[72:80].reshape(1, D_MODEL)
    grad_ret_norm = tot[80:84].reshape(1, 512)
    grad_gate_b = tot[88:90].reshape(1, 256)
    grad_gla_norm = tot[96:100].reshape(1, 512)
    grad_gate_w = lax.dynamic_slice(tot[104:136].reshape(GATE_RANK, 256), (0, chip * gate_cols),
                                    (GATE_RANK, gate_cols))

    small_grads = [grad_b_ada, grad_ln1_w, grad_ln1_b, grad_ln2_w, grad_ln2_b, grad_ret_norm, grad_gate_b,
                   grad_gla_norm, grad_gate_w[None]]
    small_out = _adam_small(list(zip(
        [b_ada, ln1_w, ln1_b, ln2_w, ln2_b, ret_norm_w, gla_gate_b, gla_norm_w, gla_gate_w], small_grads,
        [m_b_ada, m_ln1_w, m_ln1_b, m_ln2_w, m_ln2_b, m_ret_norm_w, m_gla_gate_b, m_gla_norm_w, m_gla_gate_w],
        [v_b_ada, v_ln1_w, v_ln1_b, v_ln2_w, v_ln2_b, v_ret_norm_w, v_gla_gate_b, v_gla_norm_w, v_gla_gate_w])))
    sm_delta, sm_m, sm_v = [[o[k] for o in small_out] for k in range(3)]

    dmod_all = g2[:, 0:48].reshape(N_DEV, 6 * D_MODEL)
    dmod_blk = lax.dynamic_slice(dmod_all, (0, chip * ada_cols), (N_DEV, ada_cols))
    ada_out = _ada_bwd_adam(jnp.transpose(c_all), dmod_blk, w_ada[0], m_w_ada[0], v_w_ada[0])
    ada_g, ada_delta, ada_m, ada_v = [t[None] for t in ada_out]

    partial["w_in"] = _sum_chips(r_in, "sum_w_in")
    (swapped["w_in"],) = _sibling_swap([partial["w_in"]], "swap_w_in")
    big = {}
    for n, w, m, v in zip(["w_in", "w_out", "w_ff1", "w_ff2"], [w_in, w_out, w_ff1, w_ff2],
                          [m_w_in, m_w_out, m_w_ff1, m_w_ff2], [v_w_in, v_w_out, v_w_ff1, v_w_ff2]):
        mine, theirs = partial[n], swapped[n]
        if n == "w_in":
            out = _adam_pair(_col_major(w), mine, theirs, _col_major(m), _col_major(v), "adam_" + n)
            big[n] = [jnp.transpose(t.reshape(t.shape[0], 1, t.shape[1]), (1, 2, 0)) for t in out]
        else:
            big[n] = [t[None] for t in _adam_pair(w[0], mine, theirs, m[0], v[0], "adam_" + n)]

    def assemble(ada, smalls, k):
        b_ada_o, ln1w_o, ln1b_o, ln2w_o, ln2b_o, ret_o, gb_o, gln_o, gw_o = smalls
        return [ada, b_ada_o, big["w_in"][k], ret_o, gw_o, gb_o, gln_o, big["w_out"][k], ln1w_o, ln1b_o,
                big["w_ff1"][k], big["w_ff2"][k], ln2w_o, ln2b_o]

    grads = assemble(ada_g, small_grads, 0)
    deltas = assemble(ada_delta, sm_delta, 1)
    new_m = assemble(ada_m, sm_m, 2)
    new_v = assemble(ada_v, sm_v, 3)
    return (loss, grad_x[None], *grads, *deltas, *new_m, *new_v)
```

```python
import functools

import numpy as np
import jax
import jax.numpy as jnp
from jax import lax
from jax.experimental import pallas as pl
from jax.experimental.pallas import tpu as pltpu

D_MODEL = 1024
D_FF = 4096
CHUNK = 64
N_HEADS = 4
HEAD_W = 128
GLA_DK = 64
GATE_RANK = 16
GATE_TAU = 16.0
LN_EPS = 1e-5
ALPHA = 2.0 ** 0.25
ROPE_BASE = 10000.0
RET_SCALE = float(HEAD_W) ** -0.5
GLA_SCALE = float(GLA_DK) ** -0.5

ADAM_LR = 0.001
ADAM_B1 = 0.9
ADAM_B2 = 0.999
ADAM_EPS = 1e-08
ADAM_WD = 0.01
ADAM_STEP = 10

OFF_RQ, OFF_RK, OFF_RV, OFF_RG = 0, 512, 1024, 1536
OFF_GQ, OFF_GK, OFF_GV, OFF_GG, OFF_LR = 2048, 2560, 3072, 3584, 4096
N_PROJ = 4224
N_PROJ_SRC = 3600

N_DEV = 8
N_CHIP = 4
MESH = pl.DeviceIdType.MESH
MXU_DTYPE = jnp.bfloat16
WIRE_DTYPE = jnp.bfloat16
VMEM_LIMIT = 60 * 1024 * 1024
TOKEN_TILE = 256
INPROJ_TOKEN_TILE = 512
CHUNKS_PER_STEP = 8
CHUNKS_IN_LOCKSTEP = 4
GRAD_TOKEN_TILE = 2048
ELEMENTWISE_COLS = 256
HIGHEST = lax.Precision.HIGHEST


def _mm(a, b):
    return jnp.dot(a.astype(MXU_DTYPE), b.astype(MXU_DTYPE), preferred_element_type=jnp.float32)


def _mm_nt(a, b):
    return lax.dot_general(a.astype(MXU_DTYPE), b.astype(MXU_DTYPE), (((1,), (1,)), ((), ())),
                           preferred_element_type=jnp.float32)


def _mm_tn(a, b):
    return lax.dot_general(a.astype(MXU_DTYPE), b.astype(MXU_DTYPE), (((0,), (0,)), ((), ())),
                           preferred_element_type=jnp.float32)


def _mm32(a, b):
    return jnp.dot(a, b, precision=HIGHEST, preferred_element_type=jnp.float32)


def _running_sum(mask, a):
    m = mask.astype(jnp.bfloat16)
    hi = a.astype(jnp.bfloat16)
    rest = a - hi.astype(jnp.float32)
    mid = rest.astype(jnp.bfloat16)
    lo = (rest - mid.astype(jnp.float32)).astype(jnp.bfloat16)
    dot = lambda t: jnp.dot(m, t, preferred_element_type=jnp.float32)
    return dot(hi) + dot(mid) + dot(lo)


def _rowmean(a):
    return jnp.mean(a, axis=-1, keepdims=True)


def _colsum(a):
    return jnp.sum(a, axis=0, keepdims=True)


def _ln(z):
    zc = z - _rowmean(z)
    rstd = lax.rsqrt(_rowmean(zc * zc) + LN_EPS)
    return zc * rstd, rstd


def _ln_bwd(dzh, zh, rstd):
    return rstd * (dzh - _rowmean(dzh) - zh * _rowmean(dzh * zh))


def _sigmoid(a):
    return 1.0 / (1.0 + jnp.exp(-a))


def _log_sigmoid(a):
    return jnp.minimum(a, 0.0) - jnp.log(1.0 + jnp.exp(-jnp.abs(a)))


def _swap_halves(a):
    return pltpu.roll(a, HEAD_W // 2, 1)


def _tri_masks():
    row = lax.broadcasted_iota(jnp.int32, (CHUNK, CHUNK), 0)
    col = lax.broadcasted_iota(jnp.int32, (CHUNK, CHUNK), 1)
    return row, col


def _const_spec(shape):
    zeros = (0,) * len(shape)
    return pl.BlockSpec(shape, lambda *_: zeros, pipeline_mode=pl.Buffered(1))


def _params(semantics):
    return pltpu.CompilerParams(dimension_semantics=semantics, vmem_limit_bytes=VMEM_LIMIT)


def _decay_tables():
    log_gamma = np.log(1.0 - 2.0 ** (-5.0 - np.arange(N_HEADS, dtype=np.float64)))
    idx = np.arange(CHUNK, dtype=np.float64)
    dist = np.abs(idx[:, None] - idx[None, :])
    intra = np.exp(log_gamma[:, None, None] * dist)
    kdec = np.exp(log_gamma[None, :] * (CHUNK - 1.0 - idx)[:, None])
    qdec = np.exp(log_gamma[None, :] * (idx + 1.0)[:, None])
    chunk_decay = np.exp(log_gamma * CHUNK)
    lanes = lambda t: np.repeat(t, HEAD_W, axis=1).astype(np.float32)
    return (jnp.asarray(intra.astype(np.float32)), jnp.asarray(lanes(qdec)), jnp.asarray(lanes(kdec)),
            [float(np.float32(v)) for v in chunk_decay])


def _rotary_tables(seq):
    half = HEAD_W // 2
    inv = 1.0 / (ROPE_BASE ** jnp.linspace(0.0, 1.0, half, dtype=jnp.float32))
    both = lambda t: jnp.concatenate([t, t], axis=-1)
    ang_a = jnp.arange(0, seq, CHUNK, dtype=jnp.float32)[:, None] * inv[None, :]
    rot_a = jnp.stack([both(jnp.cos(ang_a)), both(jnp.sin(ang_a))], axis=1)
    rot_a = jnp.pad(rot_a, ((0, 0), (0, 6), (0, 0)))
    ang_b = jnp.arange(CHUNK, dtype=jnp.float32)[:, None] * inv[None, :]
    cos_b, sin_b = both(jnp.cos(ang_b)), both(jnp.sin(ang_b))
    sign = jnp.concatenate([-jnp.ones((half,), jnp.float32), jnp.ones((half,), jnp.float32)])
    return rot_a, jnp.stack([cos_b, sin_b, cos_b * sign, sin_b * sign])


def _rotary_chunk(ra_ref, c, rb_ref):
    cos_a, sin_a = ra_ref[c, 0:1, :], ra_ref[c, 1:2, :]
    return cos_a * rb_ref[0] - sin_a * rb_ref[1], sin_a * rb_ref[2] + cos_a * rb_ref[3]


def _mesh_pos():
    return lax.axis_index("x"), lax.axis_index("y"), lax.axis_index("c")


def _flip(v, bit):
    return 1 - v if bit else v


def _gather_rows(v, name):
    rows = v.shape[0]

    def body(v_ref, out_ref, send_sems, recv_sems):
        _all_devices_exchange(v_ref, out_ref, send_sems, recv_sems)

    return pl.pallas_call(
        body, name=name,
        out_shape=jax.ShapeDtypeStruct((N_DEV, rows, 128), jnp.float32),
        in_specs=[pl.BlockSpec(memory_space=pltpu.VMEM)],
        out_specs=pl.BlockSpec(memory_space=pltpu.VMEM),
        scratch_shapes=_all_devices_sems(),
    )(v)


def _all_devices_sems():
    return [pltpu.SemaphoreType.DMA((N_DEV - 1,)), pltpu.SemaphoreType.DMA((N_DEV - 1,))]


def _all_devices_exchange(v_ref, out_ref, send_sems, recv_sems):
    x, y, c = _mesh_pos()
    me = 4 * x + 2 * y + c
    out_ref[me] = v_ref[...]
    sends, recvs = [], []
    for k in range(1, N_DEV):
        px, py, pc = _flip(x, (k >> 2) & 1), _flip(y, (k >> 1) & 1), _flip(c, k & 1)
        peer = 4 * px + 2 * py + pc
        sends.append(pltpu.make_async_remote_copy(
            src_ref=v_ref, dst_ref=out_ref.at[me], send_sem=send_sems.at[k - 1], recv_sem=recv_sems.at[k - 1],
            device_id=(px, py, pc), device_id_type=MESH))
        recvs.append(pltpu.make_async_remote_copy(
            src_ref=v_ref, dst_ref=out_ref.at[peer], send_sem=send_sems.at[k - 1], recv_sem=recv_sems.at[k - 1],
            device_id=(px, py, pc), device_id_type=MESH))
    for cp in sends:
        cp.start()
    for cp in recvs:
        cp.wait_recv()
    for cp in sends:
        cp.wait_send()


def _prologue(cond_rows, w_ada_blk, b_blk, w_in_t):
    cols = w_ada_blk.shape[1]
    groups = cols // 128
    c_rows = D_MODEL // 128

    def body(cond_ref, w_ref, b_ref, win_ref, cond_all_ref, mod_all_ref, stack_ref, mod_sc, *sems):
        gather = _ChipGather([win_ref], [stack_ref], sems[:5])
        gather.start()
        _all_devices_exchange(cond_ref, cond_all_ref, sems[5], sems[6])
        acc = jnp.broadcast_to(b_ref[...], (N_DEV, cols))
        for r in range(c_rows):
            cv = cond_all_ref[:, r, :]
            acc = acc + _mm32(cv * _sigmoid(cv), w_ref[r * 128:(r + 1) * 128, :])
        for k in range(groups):
            mod_sc[k] = acc[:, k * 128:(k + 1) * 128]
        _all_devices_exchange(mod_sc, mod_all_ref, sems[7], sems[8])
        gather.forward()
        gather.finish()

    vmem = pl.BlockSpec(memory_space=pltpu.VMEM)
    hbm = pl.BlockSpec(memory_space=pl.ANY)
    return pl.pallas_call(
        body, name="prologue",
        out_shape=(jax.ShapeDtypeStruct((N_DEV,) + cond_rows.shape, jnp.float32),
                   jax.ShapeDtypeStruct((N_DEV, groups, N_DEV, 128), jnp.float32))
        + _exchange_out_shapes([w_in_t], True),
        in_specs=[vmem, vmem, vmem, hbm],
        out_specs=(vmem, vmem, hbm),
        scratch_shapes=[pltpu.VMEM((groups, N_DEV, 128), jnp.float32)] + _gather_sems(1)
        + _all_devices_sems() + _all_devices_sems(),
        compiler_params=pltpu.CompilerParams(vmem_limit_bytes=VMEM_LIMIT),
    )(cond_rows, w_ada_blk, b_blk, w_in_t)


def _exchange_out_shapes(arrays, gather):
    return tuple(jax.ShapeDtypeStruct((N_CHIP,) + a.shape if gather else a.shape, a.dtype) for a in arrays)


def _scatter_sems(n):
    n_sem = n * (N_CHIP - 1)
    return [pltpu.SemaphoreType.DMA((n_sem,)), pltpu.SemaphoreType.DMA((n_sem,)), pltpu.SemaphoreType.DMA((n,))]


def _gather_sems(n):
    n_sem = n * (N_CHIP - 1)
    return [pltpu.SemaphoreType.DMA((n_sem,))] * 4 + [pltpu.SemaphoreType.DMA((n,))]


def _peer_chips(x, y):
    out = []
    for k in range(1, N_CHIP):
        px, py = _flip(x, (k >> 1) & 1), _flip(y, k & 1)
        out.append((px, py, 2 * px + py))
    return out


class _ChipScatter:
    def __init__(self, ins, outs, sems):
        send_sems, recv_sems, local_sems = sems
        x, y, c = _mesh_pos()
        chip = 2 * x + y
        self.local, self.sends, self.recvs = [], [], []
        for i in range(len(ins)):
            self.local.append(pltpu.make_async_copy(ins[i].at[chip], outs[i].at[chip], local_sems.at[i]))
            for k, (px, py, peer_chip) in enumerate(_peer_chips(x, y)):
                sem = i * (N_CHIP - 1) + k
                src = ins[i].at[peer_chip]
                self.sends.append(pltpu.make_async_remote_copy(
                    src_ref=src, dst_ref=outs[i].at[chip], send_sem=send_sems.at[sem], recv_sem=recv_sems.at[sem],
                    device_id=(px, py, c), device_id_type=MESH))
                self.recvs.append(pltpu.make_async_remote_copy(
                    src_ref=src, dst_ref=outs[i].at[peer_chip], send_sem=send_sems.at[sem], recv_sem=recv_sems.at[sem],
                    device_id=(px, py, c), device_id_type=MESH))

    def start(self):
        for cp in self.local + self.sends:
            cp.start()

    def wait(self):
        for cp in self.recvs:
            cp.wait_recv()
        for cp in self.sends:
            cp.wait_send()
        for cp in self.local:
            cp.wait()


class _ChipGather:
    def __init__(self, ins, outs, sems):
        ici_send, ici_recv, d2d_send, d2d_recv, local_sems = sems
        x, y, c = _mesh_pos()
        chip = 2 * x + y
        self.local, self.ici_sends, self.ici_recvs, self.d2d_sends, self.d2d_recvs = [], [], [], [], []
        for i in range(len(ins)):
            half = ins[i].shape[-1] // 2
            assert half % 128 == 0
            lead = (slice(None),) * (len(ins[i].shape) - 1)
            mine = lead + (pl.ds(pl.multiple_of(c * half, 128), half),)
            theirs = lead + (pl.ds(pl.multiple_of((1 - c) * half, 128), half),)
            self.local.append(pltpu.make_async_copy(ins[i], outs[i].at[chip], local_sems.at[i]))
            for k, (px, py, peer_chip) in enumerate(_peer_chips(x, y)):
                sem = i * (N_CHIP - 1) + k
                self.ici_sends.append(pltpu.make_async_remote_copy(
                    src_ref=ins[i].at[mine], dst_ref=outs[i].at[chip].at[mine],
                    send_sem=ici_send.at[sem], recv_sem=ici_recv.at[sem], device_id=(px, py, c), device_id_type=MESH))
                landed = outs[i].at[peer_chip].at[mine]
                self.ici_recvs.append(pltpu.make_async_remote_copy(
                    src_ref=ins[i].at[mine], dst_ref=landed,
                    send_sem=ici_send.at[sem], recv_sem=ici_recv.at[sem], device_id=(px, py, c), device_id_type=MESH))
                self.d2d_sends.append(pltpu.make_async_remote_copy(
                    src_ref=landed, dst_ref=landed,
                    send_sem=d2d_send.at[sem], recv_sem=d2d_recv.at[sem], device_id=(x, y, 1 - c), device_id_type=MESH))
                self.d2d_recvs.append(pltpu.make_async_remote_copy(
                    src_ref=landed, dst_ref=outs[i].at[peer_chip].at[theirs],
                    send_sem=d2d_send.at[sem], recv_sem=d2d_recv.at[sem], device_id=(x, y, 1 - c), device_id_type=MESH))

    def start(self):
        for cp in self.local + self.ici_sends:
            cp.start()

    def forward(self):
        for landed, onward in zip(self.ici_recvs, self.d2d_sends):
            landed.wait_recv()
            onward.start()

    def finish(self):
        for cp in self.d2d_recvs:
            cp.wait_recv()
        for cp in self.d2d_sends + self.ici_sends:
            cp.wait_send()
        for cp in self.local:
            cp.wait()


def _sibling_swap(arrays, name):
    n = len(arrays)

    def body(*refs):
        swap = _SiblingSwap(refs[:n], refs[n:2 * n], refs[2 * n:])
        swap.start()
        swap.wait()

    return pl.pallas_call(
        body, name=name,
        out_shape=tuple(jax.ShapeDtypeStruct(a.shape, a.dtype) for a in arrays),
        in_specs=[pl.BlockSpec(memory_space=pl.ANY)] * n,
        out_specs=tuple(pl.BlockSpec(memory_space=pl.ANY) for _ in arrays),
        scratch_shapes=_swap_sems(n),
    )(*arrays)


def _swap_sems(n):
    return [pltpu.SemaphoreType.DMA((n,)), pltpu.SemaphoreType.DMA((n,))]


class _SiblingSwap:
    def __init__(self, ins, outs, sems):
        send_sems, recv_sems = sems
        x, y, c = _mesh_pos()
        self.copies = [pltpu.make_async_remote_copy(
            src_ref=ins[i], dst_ref=outs[i], send_sem=send_sems.at[i], recv_sem=recv_sems.at[i],
            device_id=(x, y, 1 - c), device_id_type=MESH) for i in range(len(ins))]

    def start(self):
        for cp in self.copies:
            cp.start()

    def wait(self):
        for cp in self.copies:
            cp.wait_recv()
        for cp in self.copies:
            cp.wait_send()


def _adam(w, g, m, v):
    m2 = ADAM_B1 * m + (1.0 - ADAM_B1) * g
    v2 = ADAM_B2 * v + (1.0 - ADAM_B2) * (g * g)
    m_hat = m2 / (1.0 - ADAM_B1 ** ADAM_STEP)
    v_hat = v2 / (1.0 - ADAM_B2 ** ADAM_STEP)
    delta = -ADAM_LR * (m_hat / (jnp.sqrt(v_hat) + ADAM_EPS) + ADAM_WD * w)
    return delta, m2, v2


def _ada_bwd_adam(c_t, dmod_blk, w, m, v):
    rows, cols = w.shape
    tile = 512
    assert cols % tile == 0

    def body(c_ref, d_ref, w_ref, m_ref, v_ref, g_ref, dl_ref, m2_ref, v2_ref):
        sc = c_ref[...]
        sc = sc * _sigmoid(sc)
        dm = d_ref[...]
        g = sc[:, 0:1] * dm[0:1, :]
        for b in range(1, N_DEV):
            g = g + sc[:, b:b + 1] * dm[b:b + 1, :]
        delta, m2, v2 = _adam(w_ref[...], g, m_ref[...], v_ref[...])
        g_ref[...] = g
        dl_ref[...] = delta
        m2_ref[...] = m2
        v2_ref[...] = v2

    blk = pl.BlockSpec((rows, tile), lambda j: (0, j))
    out = jax.ShapeDtypeStruct((rows, cols), jnp.float32)
    return pl.pallas_call(
        body, name="ada_bwd_adam", grid=(cols // tile,),
        out_shape=(out, out, out, out),
        in_specs=[pl.BlockSpec((rows, N_DEV), lambda j: (0, 0)), pl.BlockSpec((N_DEV, tile), lambda j: (0, j)),
                  blk, blk, blk],
        out_specs=(blk, blk, blk, blk),
        compiler_params=_params(("arbitrary",)),
    )(c_t, dmod_blk, w, m, v)


def _inproj_fwd(x2, vecs, w_in_p, tm, riders):
    seq = x2.shape[0]
    n_tiles = seq // tm
    n_ride = len(riders)

    def body(*refs):
        x_ref, vec_ref, w_ref = refs[:3]
        ride_in, refs = refs[3:3 + n_ride], refs[3 + n_ride:]
        p_ref, u_ref = refs[:2]
        ride_out, sems = refs[2:2 + n_ride], refs[2 + n_ride:]
        gather = _ChipGather(ride_in, ride_out, sems)

        @pl.when(pl.program_id(0) == 0)
        def _():
            gather.start()

        xh, _ = _ln(x_ref[...])
        u = (xh * (1.0 + vec_ref[1:2, :]) + vec_ref[0:1, :]).astype(MXU_DTYPE)
        u_ref[...] = u
        p_ref[...] = _mm(u, w_ref[...])

        @pl.when(pl.program_id(0) == (3 * n_tiles) // 4)
        def _():
            gather.forward()

        @pl.when(pl.program_id(0) == n_tiles - 1)
        def _():
            gather.finish()

    hbm = pl.BlockSpec(memory_space=pl.ANY)
    return pl.pallas_call(
        body, name="inproj_fwd", grid=(n_tiles,),
        out_shape=(jax.ShapeDtypeStruct((seq, N_PROJ), jnp.float32), jax.ShapeDtypeStruct((seq, D_MODEL), MXU_DTYPE))
        + _exchange_out_shapes(riders, True),
        in_specs=[pl.BlockSpec((tm, D_MODEL), lambda i: (i, 0)), _const_spec(vecs.shape), _const_spec(w_in_p.shape)]
        + [hbm] * n_ride,
        out_specs=(pl.BlockSpec((tm, N_PROJ), lambda i: (i, 0)), pl.BlockSpec((tm, D_MODEL), lambda i: (i, 0)))
        + (hbm,) * n_ride,
        scratch_shapes=_gather_sems(n_ride),
        compiler_params=_params(("arbitrary",)),
    )(x2, vecs, w_in_p, *riders)


def _inproj_bwd(dproj, x2, dxa, vecs, w_in_pt, tm, riders):
    seq = x2.shape[0]
    n_tiles = seq // tm
    n_ride = len(riders)

    def body(*refs):
        dp_ref, x_ref, dxa_ref, vec_ref, w_ref = refs[:5]
        ride_in, refs = refs[5:5 + n_ride], refs[5 + n_ride:]
        gx_ref, sums_ref = refs[:2]
        ride_out, sems = refs[2:2 + n_ride], refs[2 + n_ride:]
        exchange = _ChipScatter(ride_in, ride_out, sems)

        @pl.when(pl.program_id(0) == 0)
        def _():
            exchange.start()
            sums_ref[...] = jnp.zeros_like(sums_ref)

        du = _mm(dp_ref[...], w_ref[...])
        xh, rstd = _ln(x_ref[...])
        sums_ref[0:1, :] += _colsum(du)
        sums_ref[1:2, :] += _colsum(du * xh)
        gx_ref[...] = dxa_ref[...] + _ln_bwd(du * (1.0 + vec_ref[1:2, :]), xh, rstd)

        @pl.when(pl.program_id(0) == n_tiles - 1)
        def _():
            exchange.wait()

    tile = pl.BlockSpec((tm, D_MODEL), lambda i: (i, 0))
    hbm = pl.BlockSpec(memory_space=pl.ANY)
    return pl.pallas_call(
        body, name="inproj_bwd", grid=(n_tiles,),
        out_shape=(jax.ShapeDtypeStruct((seq, D_MODEL), jnp.float32), jax.ShapeDtypeStruct((8, D_MODEL), jnp.float32))
        + _exchange_out_shapes(riders, False),
        in_specs=[pl.BlockSpec((tm, N_PROJ), lambda i: (i, 0)), tile, tile, _const_spec(vecs.shape),
                  _const_spec(w_in_pt.shape)] + [hbm] * n_ride,
        out_specs=(tile, pl.BlockSpec((8, D_MODEL), lambda i: (0, 0))) + (hbm,) * n_ride,
        scratch_shapes=_scatter_sems(n_ride),
        compiler_params=_params(("arbitrary",)),
    )(dproj, x2, dxa, vecs, w_in_pt, *riders)


def _head(h):
    return slice(h * HEAD_W, (h + 1) * HEAD_W)


def _cols(ref, off, h):
    return ref[:, off + h * HEAD_W:off + (h + 1) * HEAD_W]


HEADS = range(N_HEADS)


def _mixer_chunk_forward(p_ref, cc, ss, dm_ref, qdec_ref, kdec_ref, wg_ref, bg_ref, states):
    row, col = _tri_masks()
    lower = row >= col
    f = {}
    f["glr"] = p_ref[:, OFF_LR:OFF_LR + HEAD_W]
    f["logit"] = _mm(f["glr"], wg_ref[...]) + bg_ref[...]
    rq = [_cols(p_ref, OFF_RQ, h) for h in HEADS]
    rk = [_cols(p_ref, OFF_RK, h) for h in HEADS]
    f["rv"] = [_cols(p_ref, OFF_RV, h) for h in HEADS]
    f["qr"] = [(rq[h] * cc + _swap_halves(rq[h]) * ss) * RET_SCALE for h in HEADS]
    f["kr"] = [rk[h] * cc + _swap_halves(rk[h]) * ss for h in HEADS]
    s_raw = [_mm_nt(f["qr"][h], f["kr"][h]) for h in HEADS]
    yield
    la = _log_sigmoid(f["logit"]) * (1.0 / GATE_TAU)
    b = _running_sum(lower, la)
    f["qd"] = [f["qr"][h] * qdec_ref[:, _head(h)] for h in HEADS]
    f["kd"] = [f["kr"][h] * kdec_ref[:, _head(h)] for h in HEADS]
    f["scores"] = [s_raw[h] * dm_ref[h] for h in HEADS]
    yield
    b_last = b[CHUNK - 1:CHUNK, :]
    b_mid = b[CHUNK // 2 - 1:CHUNK // 2, :]
    f["e"], f["ei"] = jnp.exp(b - b_mid), jnp.exp(b_mid - b)
    f["eb"], f["ek"], f["ebl"] = jnp.exp(b), jnp.exp(b_last - b), jnp.exp(b_last)
    gq = [_cols(p_ref, OFF_GQ, h) * GLA_SCALE for h in HEADS]
    gk = [_cols(p_ref, OFF_GK, h) for h in HEADS]
    f["gv"] = [_cols(p_ref, OFF_GV, h) for h in HEADS]
    f["q_e"] = [gq[h] * f["e"][:, _head(h)] for h in HEADS]
    f["q_i"] = [gq[h] * f["ei"][:, _head(h)] for h in HEADS]
    f["k_e"] = [gk[h] * f["e"][:, _head(h)] for h in HEADS]
    f["k_i"] = [gk[h] * f["ei"][:, _head(h)] for h in HEADS]
    low = [_mm_nt(f["q_e"][h], f["k_i"][h]) for h in HEADS]
    up = [_mm_nt(f["q_i"][h], f["k_e"][h]) for h in HEADS]
    yield
    f["att"] = [jnp.where(lower, low[h], up[h]) for h in HEADS]
    f["qb"] = [gq[h] * f["eb"][:, _head(h)] for h in HEADS]
    f["kb"] = [gk[h] * f["ek"][:, _head(h)] for h in HEADS]
    ret_state, gla_state_t = states()
    f["o_ret"] = [_mm(f["scores"][h], f["rv"][h]) + _mm(f["qd"][h], ret_state[h]) for h in HEADS]
    f["o_gla"] = [_mm(f["att"][h], f["gv"][h]) + _mm_nt(f["qb"][h], gla_state_t[h]) for h in HEADS]
    return f


def _interleave(generators):
    live = list(generators)
    while live:
        for g in list(live):
            try:
                next(g)
            except StopIteration:
                live.remove(g)


def _mixer_fwd(proj, tables, wg_p, bg_p, ret_norm_w, gla_norm_w, riders):
    seq = proj.shape[0]
    n_chunks = seq // CHUNK
    per_step = min(n_chunks, CHUNKS_PER_STEP)
    n_steps = n_chunks // per_step
    n_ride = len(riders)
    rot_a, rot_b, dm_t, qdec_t, kdec_t, chunk_decay = tables

    def body(*refs):
        p_ref, ra_ref, rb_ref, dm_ref, qdec_ref, kdec_ref, wg_ref, bg_ref, wr_ref, wl_ref = refs[:10]
        ride_in, refs = refs[10:10 + n_ride], refs[10 + n_ride:]
        mix_ref, rsave_ref, ssave_ref = refs[:3]
        ride_out, refs = refs[3:3 + n_ride], refs[3 + n_ride:]
        r_sc, s_sc = refs[:2]
        gather = _ChipGather(ride_in, ride_out, refs[2:])

        @pl.when(pl.program_id(0) == 0)
        def _():
            gather.start()
            r_sc[...] = jnp.zeros_like(r_sc)
            s_sc[...] = jnp.zeros_like(s_sc)

        def one_chunk(c):
            p_c = p_ref.at[c * CHUNK:(c + 1) * CHUNK, :]
            mix_c = mix_ref.at[c * CHUNK:(c + 1) * CHUNK, :]
            before = {}

            def states():
                before["ret"] = [r_sc[h] for h in HEADS]
                before["gla"] = [s_sc[h] for h in HEADS]
                for h in HEADS:
                    rsave_ref[c, h] = before["ret"][h]
                    ssave_ref[c, h] = before["gla"][h]
                return before["ret"], before["gla"]

            cc, ss = _rotary_chunk(ra_ref, c, rb_ref)
            f = yield from _mixer_chunk_forward(p_c, cc, ss, dm_ref, qdec_ref, kdec_ref, wg_ref, bg_ref, states)
            for h in HEADS:
                r_sc[h] = chunk_decay[h] * before["ret"][h] + _mm_tn(f["kd"][h], f["rv"][h])
            for h in HEADS:
                s_sc[h] = before["gla"][h] * f["ebl"][:, _head(h)] + _mm_tn(f["gv"][h], f["kb"][h])
            yield
            for h in HEADS:
                on, _ = _ln(f["o_ret"][h])
                g = _cols(p_c, OFF_RG, h)
                mix_c[:, _head(h)] = (on * wr_ref[:, _head(h)] * (g * _sigmoid(g))).astype(mix_ref.dtype)
            for h in HEADS:
                o = f["o_gla"][h]
                on = o * lax.rsqrt(_rowmean(o * o) + LN_EPS)
                g = _cols(p_c, OFF_GG, h)
                mix_c[:, _head(N_HEADS + h)] = (on * wl_ref[:, _head(h)] * (g * _sigmoid(g))).astype(mix_ref.dtype)

        for c0 in range(0, per_step, CHUNKS_IN_LOCKSTEP):
            _interleave([one_chunk(c) for c in range(c0, min(per_step, c0 + CHUNKS_IN_LOCKSTEP))])

        @pl.when(pl.program_id(0) == (3 * n_steps) // 4)
        def _():
            gather.forward()

        @pl.when(pl.program_id(0) == n_steps - 1)
        def _():
            gather.finish()

    state_shape = (n_chunks, N_HEADS, HEAD_W, HEAD_W)
    state_blk = pl.BlockSpec((per_step, N_HEADS, HEAD_W, HEAD_W), lambda i: (i, 0, 0, 0))
    rot_blk = pl.BlockSpec((per_step, 8, HEAD_W), lambda i: (i, 0, 0))
    rows = per_step * CHUNK
    hbm = pl.BlockSpec(memory_space=pl.ANY)
    return pl.pallas_call(
        body, name="mixer_fwd", grid=(n_steps,),
        out_shape=(jax.ShapeDtypeStruct((seq, D_MODEL), MXU_DTYPE),
                   jax.ShapeDtypeStruct(state_shape, jnp.float32), jax.ShapeDtypeStruct(state_shape, jnp.float32))
        + _exchange_out_shapes(riders, True),
        in_specs=[pl.BlockSpec((rows, N_PROJ), lambda i: (i, 0)), rot_blk, _const_spec(rot_b.shape),
                  _const_spec(dm_t.shape), _const_spec(qdec_t.shape), _const_spec(kdec_t.shape),
                  _const_spec(wg_p.shape), _const_spec(bg_p.shape), _const_spec(ret_norm_w.shape),
                  _const_spec(gla_norm_w.shape)] + [hbm] * n_ride,
        out_specs=(pl.BlockSpec((rows, D_MODEL), lambda i: (i, 0)), state_blk, state_blk) + (hbm,) * n_ride,
        scratch_shapes=[pltpu.VMEM((N_HEADS, HEAD_W, HEAD_W), jnp.float32),
                        pltpu.VMEM((N_HEADS, HEAD_W, HEAD_W), jnp.float32)] + _gather_sems(n_ride),
        compiler_params=_params(("arbitrary",)),
    )(proj, rot_a, rot_b, dm_t, qdec_t, kdec_t, wg_p, bg_p, ret_norm_w, gla_norm_w, *riders)


def _mixer_bwd(proj, dmixed, rsave, ssave, tables, wg_p, bg_p, ret_norm_w, gla_norm_w, riders):
    seq = proj.shape[0]
    n_chunks = seq // CHUNK
    per_step = min(n_chunks, CHUNKS_PER_STEP)
    n_steps = n_chunks // per_step
    n_ride = len(riders)
    rot_a, rot_b, dm_t, qdec_t, kdec_t, chunk_decay = tables
    last = n_steps - 1

    def body(*refs):
        p_blk, dmx_blk = refs[:2]
        shared_in = refs[2:13]
        ride_in, refs = refs[13:13 + n_ride], refs[13 + n_ride:]
        dp_blk, dwr_ref, dwl_ref, dwg_ref, dbg_ref = refs[:5]
        ride_out, refs = refs[5:5 + n_ride], refs[5 + n_ride:]
        dr_sc, ds_sc = refs[:2]
        exchange = _ChipScatter(ride_in, ride_out, refs[2:])

        @pl.when(pl.program_id(0) == 0)
        def _():
            exchange.start()
            dr_sc[...] = jnp.zeros_like(dr_sc)
            ds_sc[...] = jnp.zeros_like(ds_sc)
            dwr_ref[...] = jnp.zeros_like(dwr_ref)
            dwl_ref[...] = jnp.zeros_like(dwl_ref)
            dwg_ref[...] = jnp.zeros_like(dwg_ref)
            dbg_ref[...] = jnp.zeros_like(dbg_ref)

        def chunk_stages(c):
            rows = slice(c * CHUNK, (c + 1) * CHUNK)
            return one_chunk(c, p_blk.at[rows, :], dmx_blk.at[rows, :], dp_blk.at[rows, :], *shared_in,
                             dwr_ref, dwl_ref, dwg_ref, dbg_ref, dr_sc, ds_sc)

        for c0 in range(per_step, 0, -CHUNKS_IN_LOCKSTEP):
            _interleave([chunk_stages(c) for c in reversed(range(max(0, c0 - CHUNKS_IN_LOCKSTEP), c0))])

        @pl.when(pl.program_id(0) == last)
        def _():
            exchange.wait()

    def one_chunk(c, p_ref, dmx_ref, dp_ref, rsave_ref, ssave_ref, ra_ref, rb_ref, dm_ref, qdec_ref, kdec_ref,
                  wg_ref, bg_ref, wr_ref, wl_ref, dwr_ref, dwl_ref, dwg_ref, dbg_ref, dr_sc, ds_sc):
        def put(off, h, val):
            dp_ref[:, off + h * HEAD_W:off + (h + 1) * HEAD_W] = val.astype(dp_ref.dtype)

        cc, ss = _rotary_chunk(ra_ref, c, rb_ref)
        row, col = _tri_masks()
        ret_state = [rsave_ref[c, h] for h in HEADS]
        gla_state_t = [ssave_ref[c, h] for h in HEADS]
        f = yield from _mixer_chunk_forward(p_ref, cc, ss, dm_ref, qdec_ref, kdec_ref, wg_ref, bg_ref,
                                            lambda: (ret_state, gla_state_t))
        yield

        do_ret, do_gla = [], []
        for h in HEADS:
            on, rstd = _ln(f["o_ret"][h])
            g = _cols(p_ref, OFF_RG, h)
            sg = _sigmoid(g)
            dy = dmx_ref[:, _head(h)].astype(jnp.float32)
            wr = wr_ref[:, _head(h)]
            dwr_ref[:, _head(h)] += _colsum(dy * on * (g * sg))
            put(OFF_RG, h, dy * on * wr * (sg * (1.0 + g * (1.0 - sg))))
            do_ret.append(_ln_bwd(dy * wr * (g * sg), on, rstd))
        for h in HEADS:
            o = f["o_gla"][h]
            rstd = lax.rsqrt(_rowmean(o * o) + LN_EPS)
            on = o * rstd
            g = _cols(p_ref, OFF_GG, h)
            sg = _sigmoid(g)
            dy = dmx_ref[:, _head(N_HEADS + h)].astype(jnp.float32)
            wl = wl_ref[:, _head(h)]
            dwl_ref[:, _head(h)] += _colsum(dy * on * (g * sg))
            put(OFF_GG, h, dy * on * wl * (sg * (1.0 + g * (1.0 - sg))))
            don = dy * wl * (g * sg)
            do_gla.append(rstd * (don - on * _rowmean(don * on)))

        yield

        d_ret_new = [dr_sc[h] for h in HEADS]
        d_gla_new = [ds_sc[h] for h in HEADS]
        ds_raw = [_mm_nt(do_ret[h], f["rv"][h]) * dm_ref[h] for h in HEADS]
        d_att = [_mm_nt(do_gla[h], f["gv"][h]) for h in HEADS]
        dq_state = [_mm_nt(do_ret[h], ret_state[h]) for h in HEADS]
        dk_state = [_mm_nt(f["rv"][h], d_ret_new[h]) for h in HEADS]
        dqb = [_mm(do_gla[h], gla_state_t[h]) for h in HEADS]
        dkb = [_mm(f["gv"][h], d_gla_new[h]) for h in HEADS]
        for h in HEADS:
            put(OFF_RV, h, _mm_tn(f["scores"][h], do_ret[h]) + _mm(f["kd"][h], d_ret_new[h]))
        for h in HEADS:
            put(OFF_GV, h, _mm_tn(f["att"][h], do_gla[h]) + _mm_nt(f["kb"][h], d_gla_new[h]))
        for h in HEADS:
            dr_sc[h] = chunk_decay[h] * d_ret_new[h] + _mm_tn(f["qd"][h], do_ret[h])
        for h in HEADS:
            ds_sc[h] = d_gla_new[h] * f["ebl"][:, _head(h)] + _mm_tn(do_gla[h], f["qb"][h])
        yield

        dqr = [_mm(ds_raw[h], f["kr"][h]) + dq_state[h] * qdec_ref[:, _head(h)] for h in HEADS]
        dkr = [_mm_tn(ds_raw[h], f["qr"][h]) + dk_state[h] * kdec_ref[:, _head(h)] for h in HEADS]
        d_low = [jnp.where(row >= col, d_att[h], 0.0) for h in HEADS]
        d_up = [jnp.where(row < col, d_att[h], 0.0) for h in HEADS]
        dq_e = [_mm(d_low[h], f["k_i"][h]) for h in HEADS]
        dk_i = [_mm_tn(d_low[h], f["q_e"][h]) for h in HEADS]
        dq_i = [_mm(d_up[h], f["k_e"][h]) for h in HEADS]
        dk_e = [_mm_tn(d_up[h], f["q_i"][h]) for h in HEADS]
        yield
        for h in HEADS:
            put(OFF_RQ, h, (dqr[h] * cc + _swap_halves(dqr[h] * ss)) * RET_SCALE)
            put(OFF_RK, h, dkr[h] * cc + _swap_halves(dkr[h] * ss))
        row_id = lax.broadcasted_iota(jnp.int32, (CHUNK, HEAD_W), 0)
        db_heads = []
        for h in HEADS:
            hs = _head(h)
            e, ei, eb, ek, ebl = f["e"][:, hs], f["ei"][:, hs], f["eb"][:, hs], f["ek"][:, hs], f["ebl"][:, hs]
            put(OFF_GQ, h, (dq_e[h] * e + dq_i[h] * ei + dqb[h] * eb) * GLA_SCALE)
            put(OFF_GK, h, dk_e[h] * e + dk_i[h] * ei + dkb[h] * ek)
            db = (dq_e[h] * f["q_e"][h] - dq_i[h] * f["q_i"][h] + dk_e[h] * f["k_e"][h] - dk_i[h] * f["k_i"][h]
                  + dqb[h] * f["qb"][h] - dkb[h] * f["kb"][h])
            db_last = _colsum(dkb[h] * f["kb"][h]) + ebl * _colsum(gla_state_t[h] * d_gla_new[h])
            db_heads.append(db + jnp.where(row_id == CHUNK - 1, db_last, 0.0))
        db = jnp.concatenate(db_heads, axis=1)
        d_la = _running_sum(col >= row, db)
        d_logit = d_la * (1.0 / GATE_TAU) * (1.0 - _sigmoid(f["logit"]))
        put(OFF_LR, 0, _mm_nt(d_logit, wg_ref[...]))
        dwg_ref[...] += _mm_tn(f["glr"], d_logit)
        dbg_ref[...] += _colsum(d_logit)

    state_blk = pl.BlockSpec((per_step, N_HEADS, HEAD_W, HEAD_W), lambda i: (last - i, 0, 0, 0))
    rot_blk = pl.BlockSpec((per_step, 8, HEAD_W), lambda i: (last - i, 0, 0))
    width = N_HEADS * HEAD_W
    vec_out = pl.BlockSpec((1, width), lambda i: (0, 0))
    hbm = pl.BlockSpec(memory_space=pl.ANY)
    rows_blk = per_step * CHUNK
    return pl.pallas_call(
        body, name="mixer_bwd", grid=(n_steps,),
        out_shape=(jax.ShapeDtypeStruct((seq, N_PROJ), MXU_DTYPE),
                   jax.ShapeDtypeStruct((1, width), jnp.float32), jax.ShapeDtypeStruct((1, width), jnp.float32),
                   jax.ShapeDtypeStruct((HEAD_W, width), jnp.float32), jax.ShapeDtypeStruct((1, width), jnp.float32))
        + _exchange_out_shapes(riders, False),
        in_specs=[pl.BlockSpec((rows_blk, N_PROJ), lambda i: (last - i, 0)),
                  pl.BlockSpec((rows_blk, D_MODEL), lambda i: (last - i, 0)), state_blk, state_blk, rot_blk,
                  _const_spec(rot_b.shape),
                  _const_spec(dm_t.shape), _const_spec(qdec_t.shape), _const_spec(kdec_t.shape),
                  _const_spec(wg_p.shape), _const_spec(bg_p.shape), _const_spec(ret_norm_w.shape),
                  _const_spec(gla_norm_w.shape)] + [hbm] * n_ride,
        out_specs=(pl.BlockSpec((rows_blk, N_PROJ), lambda i: (last - i, 0)), vec_out, vec_out,
                   pl.BlockSpec((HEAD_W, width), lambda i: (0, 0)), vec_out) + (hbm,) * n_ride,
        scratch_shapes=[pltpu.VMEM((N_HEADS, HEAD_W, HEAD_W), jnp.float32),
                        pltpu.VMEM((N_HEADS, HEAD_W, HEAD_W), jnp.float32)] + _scatter_sems(n_ride),
        compiler_params=_params(("arbitrary",)),
    )(proj, dmixed, rsave, ssave, rot_a, rot_b, dm_t, qdec_t, kdec_t, wg_p, bg_p, ret_norm_w, gla_norm_w, *riders)


V_GATE1, V_SCALE2, V_SHIFT2, V_GATE2, V_LN1W, V_LN1B, V_LN2W, V_LN2B = range(8)
S_GATE1, S_SCALE2, S_SHIFT2, S_GATE2, S_LN1W, S_LN1B, S_LN2W, S_LN2B, S_LOSS = range(9)


def _mlp_fwd_bwd(x2, mixed, target, vecs, w_out, w1_chunks, w2_chunks, tm):
    seq = x2.shape[0]
    n_fc, _, fc = w1_chunks.shape

    def body(x_ref, mx_ref, t_ref, vec_ref, wo_ref, w1_ref, w2_ref,
             dmx_ref, dxa_ref, a_ref, dh_ref, u2_ref, df_ref, dm_ref, sums_ref, relu_sc):
        @pl.when(pl.program_id(0) == 0)
        def _():
            sums_ref[...] = jnp.zeros_like(sums_ref)

        vec = lambda r: vec_ref[r:r + 1, :]

        def acc(r, val):
            sums_ref[r:r + 1, :] += _colsum(val)

        xx = x_ref[...]
        m = _mm(mx_ref[...], wo_ref[...])
        z1h, rstd1 = _ln(ALPHA * xx + vec(V_GATE1) * m)
        x1 = z1h * vec(V_LN1W) + vec(V_LN1B)
        x1h, rstd0 = _ln(x1)
        u2 = (x1h * (1.0 + vec(V_SCALE2)) + vec(V_SHIFT2)).astype(MXU_DTYPE)
        u2_ref[...] = u2
        f = jnp.zeros((tm, D_MODEL), jnp.float32)
        for j in range(n_fc):
            r = jnp.maximum(_mm(u2, w1_ref[j]), 0.0)
            relu_sc[:, j * fc:(j + 1) * fc] = r
            a = (r * r).astype(MXU_DTYPE)
            a_ref[:, j * fc:(j + 1) * fc] = a
            f = f + _mm(a, w2_ref[j])
        z2h, rstd2 = _ln(ALPHA * x1 + vec(V_GATE2) * f)
        err = z2h * vec(V_LN2W) + vec(V_LN2B) - t_ref[...]
        acc(S_LOSS, err * err)
        dy = err * (1.0 / D_MODEL)
        acc(S_LN2W, dy * z2h)
        acc(S_LN2B, dy)
        dz2 = _ln_bwd(dy * vec(V_LN2W), z2h, rstd2)
        acc(S_GATE2, dz2 * f)
        df = (vec(V_GATE2) * dz2).astype(MXU_DTYPE)
        df_ref[...] = df
        du2 = jnp.zeros((tm, D_MODEL), jnp.float32)
        for j in range(n_fc):
            dh = (_mm_nt(df, w2_ref[j]) * (2.0 * relu_sc[:, j * fc:(j + 1) * fc])).astype(MXU_DTYPE)
            dh_ref[:, j * fc:(j + 1) * fc] = dh
            du2 = du2 + _mm_nt(dh, w1_ref[j])
        acc(S_SCALE2, du2 * x1h)
        acc(S_SHIFT2, du2)
        dx1 = ALPHA * dz2 + _ln_bwd(du2 * (1.0 + vec(V_SCALE2)), x1h, rstd0)
        acc(S_LN1W, dx1 * z1h)
        acc(S_LN1B, dx1)
        dz1 = _ln_bwd(dx1 * vec(V_LN1W), z1h, rstd1)
        acc(S_GATE1, dz1 * m)
        dxa_ref[...] = ALPHA * dz1
        dm = (vec(V_GATE1) * dz1).astype(MXU_DTYPE)
        dm_ref[...] = dm
        dmx_ref[...] = _mm_nt(dm, wo_ref[...])

    tile = lambda width: pl.BlockSpec((tm, width), lambda i: (i, 0))
    f32 = lambda width: jax.ShapeDtypeStruct((seq, width), jnp.float32)
    b16 = lambda width: jax.ShapeDtypeStruct((seq, width), MXU_DTYPE)
    return pl.pallas_call(
        body, name="mlp_fwd_bwd", grid=(seq // tm,),
        out_shape=(f32(D_MODEL), f32(D_MODEL), b16(D_FF), b16(D_FF), b16(D_MODEL), b16(D_MODEL), b16(D_MODEL),
                   jax.ShapeDtypeStruct((16, D_MODEL), jnp.float32)),
        in_specs=[tile(D_MODEL), tile(D_MODEL), tile(D_MODEL), _const_spec(vecs.shape), _const_spec(w_out.shape),
                  _const_spec(w1_chunks.shape), _const_spec(w2_chunks.shape)],
        out_specs=(tile(D_MODEL), tile(D_MODEL), tile(D_FF), tile(D_FF), tile(D_MODEL), tile(D_MODEL),
                   tile(D_MODEL), pl.BlockSpec((16, D_MODEL), lambda i: (0, 0))),
        scratch_shapes=[pltpu.VMEM((tm, D_FF), jnp.float32)],
        compiler_params=_params(("arbitrary",)),
    )(x2, mixed, target, vecs, w_out, w1_chunks, w2_chunks)


def _grad_matmul(a, b, name, tn, blocks_are_rows, riders=()):
    seq, m_dim = a.shape
    n_dim = b.shape[1]
    tk = min(seq, GRAD_TOKEN_TILE)
    nk = seq // tk
    n_ride = len(riders)
    if blocks_are_rows:
        tm = m_dim // N_CHIP
        assert tn == n_dim
        grid = (N_CHIP, 1, nk)
        out_map = lambda i, j, k: (i, 0, 0)
    else:
        tm = m_dim
        assert tn * N_CHIP == n_dim
        grid = (1, N_CHIP, nk)
        out_map = lambda i, j, k: (j, 0, 0)

    def body(*refs):
        a_ref, b_ref = refs[:2]
        ride_in, refs = refs[2:2 + n_ride], refs[2 + n_ride:]
        o_ref = refs[0]
        ride_out, refs = refs[1:1 + n_ride], refs[1 + n_ride:]
        acc_sc = refs[0]
        exchange = _ChipScatter(ride_in, ride_out, refs[1:]) if n_ride else None
        block = pl.program_id(0) + pl.program_id(1)
        k = pl.program_id(2)

        if exchange is not None:
            @pl.when((block == 0) & (k == 0))
            def _():
                exchange.start()

        @pl.when(k == 0)
        def _():
            acc_sc[...] = jnp.zeros_like(acc_sc)

        acc_sc[...] += _mm_tn(a_ref[...], b_ref[...])

        @pl.when(k == nk - 1)
        def _():
            o_ref[0] = acc_sc[...].astype(o_ref.dtype)

        if exchange is not None:
            @pl.when((block == N_CHIP - 1) & (k == nk - 1))
            def _():
                exchange.wait()

    hbm = pl.BlockSpec(memory_space=pl.ANY)
    out = pl.pallas_call(
        body, name=name, grid=grid,
        out_shape=(jax.ShapeDtypeStruct((N_CHIP, tm, tn), WIRE_DTYPE),) + _exchange_out_shapes(riders, False),
        in_specs=[pl.BlockSpec((tk, tm), lambda i, j, k: (k, i)), pl.BlockSpec((tk, tn), lambda i, j, k: (k, j))]
        + [hbm] * n_ride,
        out_specs=(pl.BlockSpec((1, tm, tn), out_map),) + (hbm,) * n_ride,
        scratch_shapes=[pltpu.VMEM((tm, tn), jnp.float32)] + (_scatter_sems(n_ride) if n_ride else []),
        compiler_params=_params(("arbitrary", "arbitrary", "arbitrary")),
    )(a, b, *riders)
    return out if n_ride else out[0]


def _grad_matmul_full(a, b, name, tm, riders):
    seq, m_dim = a.shape
    n_dim = b.shape[1]
    tk = min(seq, GRAD_TOKEN_TILE)
    nk = seq // tk
    n_blocks = m_dim // tm
    n_ride = len(riders)
    assert m_dim % tm == 0

    def body(*refs):
        a_ref, b_ref = refs[:2]
        ride_in, refs = refs[2:2 + n_ride], refs[2 + n_ride:]
        o_ref = refs[0]
        ride_out, refs = refs[1:1 + n_ride], refs[1 + n_ride:]
        acc_sc = refs[0]
        swap = _SiblingSwap(ride_in, ride_out, refs[1:])
        i, k = pl.program_id(0), pl.program_id(1)

        @pl.when((i == 0) & (k == 0))
        def _():
            swap.start()

        @pl.when(k == 0)
        def _():
            acc_sc[...] = jnp.zeros_like(acc_sc)

        acc_sc[...] += _mm_tn(a_ref[...], b_ref[...])

        @pl.when(k == nk - 1)
        def _():
            o_ref[...] = acc_sc[...].astype(o_ref.dtype)

        @pl.when((i == n_blocks - 1) & (k == nk - 1))
        def _():
            swap.wait()

    hbm = pl.BlockSpec(memory_space=pl.ANY)
    return pl.pallas_call(
        body, name=name, grid=(n_blocks, nk),
        out_shape=(jax.ShapeDtypeStruct((m_dim, n_dim), WIRE_DTYPE),)
        + tuple(jax.ShapeDtypeStruct(r.shape, r.dtype) for r in riders),
        in_specs=[pl.BlockSpec((tk, tm), lambda i, k: (k, i)), pl.BlockSpec((tk, n_dim), lambda i, k: (k, 0))]
        + [hbm] * n_ride,
        out_specs=(pl.BlockSpec((tm, n_dim), lambda i, k: (i, 0)),) + (hbm,) * n_ride,
        scratch_shapes=[pltpu.VMEM((tm, n_dim), jnp.float32)] + _swap_sems(n_ride),
        compiler_params=_params(("arbitrary", "arbitrary")),
    )(a, b, *riders)


def _sum_chips(stack, name):
    _, rows, cols = stack.shape
    tc = min(cols, ELEMENTWISE_COLS)

    def body(s_ref, o_ref):
        total = s_ref[0].astype(jnp.float32)
        for j in range(1, N_CHIP):
            total = total + s_ref[j].astype(jnp.float32)
        o_ref[...] = total

    return pl.pallas_call(
        body, name=name, grid=(cols // tc,),
        out_shape=jax.ShapeDtypeStruct((rows, cols), jnp.float32),
        in_specs=[pl.BlockSpec((N_CHIP, rows, tc), lambda i: (0, 0, i))],
        out_specs=pl.BlockSpec((rows, tc), lambda i: (0, i)),
        compiler_params=_params(("arbitrary",)),
    )(stack)


def _adam_pair(w, g_mine, g_sibling, m, v, name):
    rows, cols = w.shape
    tc = min(cols, ELEMENTWISE_COLS)

    def total(ref):
        if len(ref.shape) == 2:
            return ref[...]
        acc = ref[0].astype(jnp.float32)
        for j in range(1, ref.shape[0]):
            acc = acc + ref[j].astype(jnp.float32)
        return acc

    def body(w_ref, ga_ref, gb_ref, m_ref, v_ref, g_ref, dl_ref, m2_ref, v2_ref):
        g = total(ga_ref) + total(gb_ref)
        delta, m2, v2 = _adam(w_ref[...], g, m_ref[...], v_ref[...])
        g_ref[...] = g
        dl_ref[...] = delta
        m2_ref[...] = m2
        v2_ref[...] = v2

    blk = pl.BlockSpec((rows, tc), lambda i: (0, i))
    g_blk = lambda a: blk if a.ndim == 2 else pl.BlockSpec((a.shape[0], rows, tc), lambda i: (0, 0, i))
    out = jax.ShapeDtypeStruct((rows, cols), jnp.float32)
    return pl.pallas_call(
        body, name=name, grid=(cols // tc,),
        out_shape=(out, out, out, out),
        in_specs=[blk, g_blk(g_mine), g_blk(g_sibling), blk, blk], out_specs=(blk,) * 4,
        compiler_params=_params(("arbitrary",)),
    )(w, g_mine, g_sibling, m, v)


def _sum_devices(gathered):
    _, rows, _ = gathered.shape

    def body(g_ref, o_ref):
        total = g_ref[0]
        for d in range(1, N_DEV):
            total = total + g_ref[d]
        o_ref[...] = total

    return pl.pallas_call(
        body, name="sum_devices",
        out_shape=jax.ShapeDtypeStruct((rows, 128), jnp.float32),
    )(gathered)


def _adam_small(params):
    n = len(params)

    def body(*refs):
        ins, outs = refs[:4 * n], refs[4 * n:]
        for i in range(n):
            w_ref, g_ref, m_ref, v_ref = ins[4 * i:4 * i + 4]
            delta, m2, v2 = _adam(w_ref[...], g_ref[...], m_ref[...], v_ref[...])
            outs[3 * i][...] = delta
            outs[3 * i + 1][...] = m2
            outs[3 * i + 2][...] = v2

    out_shape = tuple(jax.ShapeDtypeStruct(p[0].shape, jnp.float32) for p in params for _ in range(3))
    out = pl.pallas_call(body, name="adam_small", out_shape=out_shape)(*[t for p in params for t in p])
    return [out[3 * i:3 * i + 3] for i in range(n)]


def _pad_heads(w):
    lead = w.shape[:-1]
    w = w.reshape(lead + (N_HEADS, GLA_DK))
    w = jnp.pad(w, [(0, 0)] * len(lead) + [(0, 0), (0, HEAD_W - GLA_DK)])
    return w.reshape(lead + (N_HEADS * HEAD_W,))


def _unpad_heads(w):
    lead = w.shape[:-1]
    return w.reshape(lead + (N_HEADS, HEAD_W))[..., :GLA_DK].reshape(lead + (N_HEADS * GLA_DK,))


def _pad_head_rows(w):
    w = w.reshape(N_HEADS, GLA_DK, w.shape[-1])
    return jnp.pad(w, ((0, 0), (0, HEAD_W - GLA_DK), (0, 0))).reshape(N_HEADS * HEAD_W, w.shape[-1])


def _unpad_head_rows(w):
    return w.reshape(N_HEADS, HEAD_W, w.shape[-1])[:, :GLA_DK].reshape(N_HEADS * GLA_DK, w.shape[-1])


def _pad_w_in_rows(w):
    return jnp.concatenate([
        w[:2048], _pad_head_rows(w[2048:2304]), _pad_head_rows(w[2304:2560]), w[2560:3584],
        jnp.pad(w[3584:3600], ((0, HEAD_W - GATE_RANK), (0, 0)))], axis=0)


def _unpad_w_in_rows(g):
    return jnp.concatenate([
        g[:2048], _unpad_head_rows(g[OFF_GQ:OFF_GQ + 512]), _unpad_head_rows(g[OFF_GK:OFF_GK + 512]),
        g[OFF_GV:OFF_LR], g[OFF_LR:OFF_LR + GATE_RANK]], axis=0)


def _col_major(w):
    return jnp.transpose(w, (2, 0, 1)).reshape(w.shape[2], w.shape[1])


def _rows128(a):
    return a.reshape(-1, 128)


def _rows8(a):
    a = a.reshape(-1, 128)
    return jnp.pad(a, ((0, -a.shape[0] % 8), (0, 0)))


def kernel(x, c, w_ada, b_ada, w_in, ret_norm_w, gla_gate_w, gla_gate_b, gla_norm_w, w_out, ln1_w, ln1_b, w_ff1, w_ff2, ln2_w, ln2_b, loss_target, m_w_ada, m_b_ada, m_w_in, m_ret_norm_w, m_gla_gate_w, m_gla_gate_b, m_gla_norm_w, m_w_out, m_ln1_w, m_ln1_b, m_w_ff1, m_w_ff2, m_ln2_w, m_ln2_b, v_w_ada, v_b_ada, v_w_in, v_ret_norm_w, v_gla_gate_w, v_gla_gate_b, v_gla_norm_w, v_w_out, v_ln1_w, v_ln1_b, v_w_ff1, v_w_ff2, v_ln2_w, v_ln2_b):
    seq = x.shape[1]
    tm = min(seq, TOKEN_TILE)
    tm_in = min(seq, INPROJ_TOKEN_TILE)
    xi, yi, ci = _mesh_pos()
    dev = 4 * xi + 2 * yi + ci
    chip = 2 * xi + yi
    x2, target = x[0], loss_target[0]
    ada_cols = w_ada.shape[2]
    in_cols = w_in.shape[2]
    gate_cols = gla_gate_w.shape[2]

    b_blk = lax.dynamic_slice(b_ada, (0, chip * ada_cols), (1, ada_cols))
    g0, g1, w_in_stack = _prologue(jnp.concatenate([_rows128(c), _rows128(gla_gate_w[0])], axis=0), w_ada[0], b_blk,
                                   _col_major(w_in).astype(WIRE_DTYPE))
    c_all = g0[:, :8].reshape(N_DEV, D_MODEL)
    gate_w_full = jnp.concatenate([g0[2 * j, 8:16].reshape(GATE_RANK, gate_cols) for j in range(N_CHIP)], axis=1)
    wg_p = jnp.pad(_pad_heads(gate_w_full), ((0, HEAD_W - GATE_RANK), (0, 0)))
    bg_p = _pad_heads(gla_gate_b)
    mine = lax.dynamic_index_in_dim(g1, dev, axis=2, keepdims=False)
    mod = jnp.concatenate([mine[2 * j].reshape(1, ada_cols) for j in range(N_CHIP)], axis=1)
    shift1, scale1, gate1, shift2, scale2, gate2 = [mod[:, i * D_MODEL:(i + 1) * D_MODEL] for i in range(6)]
    w_in_pt = _pad_w_in_rows(w_in_stack.reshape(N_PROJ_SRC, D_MODEL)).astype(MXU_DTYPE)
    w_in_p = jnp.transpose(w_in_pt)

    zeros_row = jnp.zeros((1, D_MODEL), jnp.float32)
    vecs1 = jnp.concatenate([shift1, scale1] + [zeros_row] * 6, axis=0)
    proj, u, w2_stack = _inproj_fwd(x2, vecs1, w_in_p, tm_in, [w_ff2[0].astype(WIRE_DTYPE)])
    rot_a, rot_b = _rotary_tables(seq)
    dm_t, qdec_t, kdec_t, chunk_decay = _decay_tables()
    tables = (rot_a, rot_b, dm_t, qdec_t, kdec_t, chunk_decay)
    mixed, rsave, ssave, w_out_stack, w1_stack = _mixer_fwd(
        proj, tables, wg_p, bg_p, ret_norm_w, gla_norm_w,
        [w_out[0].astype(WIRE_DTYPE), w_ff1[0].astype(WIRE_DTYPE)])
    w_out_full = w_out_stack.reshape(D_MODEL, D_MODEL).astype(MXU_DTYPE)
    w1_chunks = w1_stack.astype(MXU_DTYPE)
    w2_chunks = w2_stack.astype(MXU_DTYPE)

    vecs2 = jnp.concatenate([gate1, scale2, shift2, gate2, ln1_w, ln1_b, ln2_w, ln2_b], axis=0)
    dmixed, dxa, act, dh, u2, df, dm, sums2 = _mlp_fwd_bwd(x2, mixed, target, vecs2, w_out_full, w1_chunks,
                                                           w2_chunks, tm)

    g_out_stack = _grad_matmul(mixed, dm, "grad_w_out", D_MODEL, True)
    g_ff1_stack, r_out = _grad_matmul(u2, dh, "grad_w_ff1", D_FF // N_CHIP, False, [g_out_stack])
    g_ff2_stack = _grad_matmul(act, df, "grad_w_ff2", D_MODEL, True)
    dproj, d_ret_norm, d_gla_norm, d_wg_p, d_bg_p, r_ff1, r_ff2 = _mixer_bwd(
        proj, dmixed, rsave, ssave, tables, wg_p, bg_p, ret_norm_w, gla_norm_w, [g_ff1_stack, g_ff2_stack])
    early = ["w_out", "w_ff1", "w_ff2"]
    partial = dict(zip(early, [r_out, r_ff1, r_ff2]))
    g_in_t, *swapped_early = _grad_matmul_full(dproj, u, "grad_w_in", N_PROJ // 3, [partial[n] for n in early])
    swapped = dict(zip(early, swapped_early))
    g_in_stack = _unpad_w_in_rows(g_in_t).reshape(N_CHIP, in_cols, D_MODEL)
    grad_x, sums1, r_in = _inproj_bwd(dproj, x2, dxa, vecs1, w_in_pt, tm_in, [g_in_stack])

    dmod = jnp.concatenate([sums1[0:1], sums1[1:2], sums2[S_GATE1:S_GATE1 + 1], sums2[S_SHIFT2:S_SHIFT2 + 1],
                            sums2[S_SCALE2:S_SCALE2 + 1], sums2[S_GATE2:S_GATE2 + 1]], axis=1)
    d_gate_w_full = _unpad_heads(d_wg_p[:GATE_RANK])
    flat = lambda parts: jnp.concatenate([_rows8(p) for p in parts], axis=0)
    small = flat([dmod, sums2[S_LN1W:S_LN1W + 1], sums2[S_LN1B:S_LN1B + 1], sums2[S_LN2W:S_LN2W + 1],
                  sums2[S_LN2B:S_LN2B + 1], d_ret_norm, _unpad_heads(d_bg_p), d_gla_norm, d_gate_w_full,
                  sums2[S_LOSS:S_LOSS + 1]])
    g2 = _gather_rows(small, "gather_small")
    tot = _sum_devices(g2)
    loss = 0.5 / D_MODEL * jnp.sum(tot[136:144])
    grad_b_ada = tot[0:48].reshape(1, 6 * D_MODEL)
    grad_ln1_w, grad_ln1_b = tot[48:56].reshape(1, D_MODEL), tot[56:64].reshape(1, D_MODEL)
    grad_ln2_w, grad_ln2_b = tot[64:72].reshape(1, D_MODEL), tot[72:80].reshape(1, D_MODEL)
    grad_ret_norm = tot[80:84].reshape(1, 512)
    grad_gate_b = tot[88:90].reshape(1, 256)
    grad_gla_norm = tot[96:100].reshape(1, 512)
    grad_gate_w = lax.dynamic_slice(tot[104:136].reshape(GATE_RANK, 256), (0, chip * gate_cols),
                                    (GATE_RANK, gate_cols))

    small_grads = [grad_b_ada, grad_ln1_w, grad_ln1_b, grad_ln2_w, grad_ln2_b, grad_ret_norm, grad_gate_b,
                   grad_gla_norm, grad_gate_w[None]]
    small_out = _adam_small(list(zip(
        [b_ada, ln1_w, ln1_b, ln2_w, ln2_b, ret_norm_w, gla_gate_b, gla_norm_w, gla_gate_w], small_grads,
        [m_b_ada, m_ln1_w, m_ln1_b, m_ln2_w, m_ln2_b, m_ret_norm_w, m_gla_gate_b, m_gla_norm_w, m_gla_gate_w],
        [v_b_ada, v_ln1_w, v_ln1_b, v_ln2_w, v_ln2_b, v_ret_norm_w, v_gla_gate_b, v_gla_norm_w, v_gla_gate_w])))
    sm_delta, sm_m, sm_v = [[o[k] for o in small_out] for k in range(3)]

    dmod_all = g2[:, 0:48].reshape(N_DEV, 6 * D_MODEL)
    dmod_blk = lax.dynamic_slice(dmod_all, (0, chip * ada_cols), (N_DEV, ada_cols))
    ada_out = _ada_bwd_adam(jnp.transpose(c_all), dmod_blk, w_ada[0], m_w_ada[0], v_w_ada[0])
    ada_g, ada_delta, ada_m, ada_v = [t[None] for t in ada_out]

    partial["w_in"] = _sum_chips(r_in, "sum_w_in")
    (swapped["w_in"],) = _sibling_swap([partial["w_in"]], "swap_w_in")
    big = {}
    for n, w, m, v in zip(["w_in", "w_out", "w_ff1", "w_ff2"], [w_in, w_out, w_ff1, w_ff2],
                          [m_w_in, m_w_out, m_w_ff1, m_w_ff2], [v_w_in, v_w_out, v_w_ff1, v_w_ff2]):
        mine, theirs = partial[n], swapped[n]
        if n == "w_in":
            out = _adam_pair(_col_major(w), mine, theirs, _col_major(m), _col_major(v), "adam_" + n)
            big[n] = [jnp.transpose(t.reshape(t.shape[0], 1, t.shape[1]), (1, 2, 0)) for t in out]
        else:
            big[n] = [t[None] for t in _adam_pair(w[0], mine, theirs, m[0], v[0], "adam_" + n)]

    def assemble(ada, smalls, k):
        b_ada_o, ln1w_o, ln1b_o, ln2w_o, ln2b_o, ret_o, gb_o, gln_o, gw_o = smalls
        return [ada, b_ada_o, big["w_in"][k], ret_o, gw_o, gb_o, gln_o, big["w_out"][k], ln1w_o, ln1b_o,
                big["w_ff1"][k], big["w_ff2"][k], ln2w_o, ln2b_o]

    grads = assemble(ada_g, small_grads, 0)
    deltas = assemble(ada_delta, sm_delta, 1)
    new_m = assemble(ada_m, sm_m, 2)
    new_v = assemble(ada_v, sm_v, 3)
    return (loss, grad_x[None], *grads, *deltas, *new_m, *new_v)
```

```python
import functools

import numpy as np
import jax
import jax.numpy as jnp
from jax import lax
from jax.experimental import pallas as pl
from jax.experimental.pallas import tpu as pltpu

D_MODEL = 1024
D_FF = 4096
CHUNK = 64
N_HEADS = 4
HEAD_W = 128
GLA_DK = 64
GATE_RANK = 16
GATE_TAU = 16.0
LN_EPS = 1e-5
ALPHA = 2.0 ** 0.25
ROPE_BASE = 10000.0
RET_SCALE = float(HEAD_W) ** -0.5
GLA_SCALE = float(GLA_DK) ** -0.5

ADAM_LR = 0.001
ADAM_B1 = 0.9
ADAM_B2 = 0.999
ADAM_EPS = 1e-08
ADAM_WD = 0.01
ADAM_STEP = 10

OFF_RQ, OFF_RK, OFF_RV, OFF_RG = 0, 512, 1024, 1536
OFF_GQ, OFF_GK, OFF_GV, OFF_GG, OFF_LR = 2048, 2560, 3072, 3584, 4096
N_PROJ = 4224

N_DEV = 8
N_CHIP = 4
MESH = pl.DeviceIdType.MESH
MXU_DTYPE = jnp.bfloat16
WIRE_DTYPE = jnp.bfloat16
VMEM_LIMIT = 60 * 1024 * 1024
TOKEN_TILE = 256
INPROJ_TOKEN_TILE = 512
CHUNKS_PER_STEP = 8
CHUNKS_IN_LOCKSTEP = 4
GRAD_TOKEN_TILE = 2048
ELEMENTWISE_COLS = 256
HIGHEST = lax.Precision.HIGHEST


def _mm(a, b):
    return jnp.dot(a.astype(MXU_DTYPE), b.astype(MXU_DTYPE), preferred_element_type=jnp.float32)


def _mm_nt(a, b):
    return lax.dot_general(a.astype(MXU_DTYPE), b.astype(MXU_DTYPE), (((1,), (1,)), ((), ())),
                           preferred_element_type=jnp.float32)


def _mm_tn(a, b):
    return lax.dot_general(a.astype(MXU_DTYPE), b.astype(MXU_DTYPE), (((0,), (0,)), ((), ())),
                           preferred_element_type=jnp.float32)


def _mm32(a, b):
    return jnp.dot(a, b, precision=HIGHEST, preferred_element_type=jnp.float32)


def _running_sum(mask, a):
    m = mask.astype(jnp.bfloat16)
    hi = a.astype(jnp.bfloat16)
    rest = a - hi.astype(jnp.float32)
    mid = rest.astype(jnp.bfloat16)
    lo = (rest - mid.astype(jnp.float32)).astype(jnp.bfloat16)
    dot = lambda t: jnp.dot(m, t, preferred_element_type=jnp.float32)
    return dot(hi) + dot(mid) + dot(lo)


def _rowmean(a):
    return jnp.mean(a, axis=-1, keepdims=True)


def _colsum(a):
    return jnp.sum(a, axis=0, keepdims=True)


def _ln(z):
    zc = z - _rowmean(z)
    rstd = lax.rsqrt(_rowmean(zc * zc) + LN_EPS)
    return zc * rstd, rstd


def _ln_bwd(dzh, zh, rstd):
    return rstd * (dzh - _rowmean(dzh) - zh * _rowmean(dzh * zh))


def _sigmoid(a):
    return 1.0 / (1.0 + jnp.exp(-a))


def _log_sigmoid(a):
    return jnp.minimum(a, 0.0) - jnp.log(1.0 + jnp.exp(-jnp.abs(a)))


def _swap_halves(a):
    return pltpu.roll(a, HEAD_W // 2, 1)


def _tri_masks():
    row = lax.broadcasted_iota(jnp.int32, (CHUNK, CHUNK), 0)
    col = lax.broadcasted_iota(jnp.int32, (CHUNK, CHUNK), 1)
    return row, col


def _const_spec(shape):
    zeros = (0,) * len(shape)
    return pl.BlockSpec(shape, lambda *_: zeros, pipeline_mode=pl.Buffered(1))


def _params(semantics):
    return pltpu.CompilerParams(dimension_semantics=semantics, vmem_limit_bytes=VMEM_LIMIT)


def _decay_tables():
    log_gamma = np.log(1.0 - 2.0 ** (-5.0 - np.arange(N_HEADS, dtype=np.float64)))
    idx = np.arange(CHUNK, dtype=np.float64)
    dist = np.abs(idx[:, None] - idx[None, :])
    intra = np.exp(log_gamma[:, None, None] * dist)
    kdec = np.exp(log_gamma[None, :] * (CHUNK - 1.0 - idx)[:, None])
    qdec = np.exp(log_gamma[None, :] * (idx + 1.0)[:, None])
    chunk_decay = np.exp(log_gamma * CHUNK)
    lanes = lambda t: np.repeat(t, HEAD_W, axis=1).astype(np.float32)
    return (jnp.asarray(intra.astype(np.float32)), jnp.asarray(lanes(qdec)), jnp.asarray(lanes(kdec)),
            [float(np.float32(v)) for v in chunk_decay])


def _rotary_tables(seq):
    half = HEAD_W // 2
    inv = 1.0 / (ROPE_BASE ** jnp.linspace(0.0, 1.0, half, dtype=jnp.float32))
    both = lambda t: jnp.concatenate([t, t], axis=-1)
    ang_a = jnp.arange(0, seq, CHUNK, dtype=jnp.float32)[:, None] * inv[None, :]
    rot_a = jnp.stack([both(jnp.cos(ang_a)), both(jnp.sin(ang_a))], axis=1)
    rot_a = jnp.pad(rot_a, ((0, 0), (0, 6), (0, 0)))
    ang_b = jnp.arange(CHUNK, dtype=jnp.float32)[:, None] * inv[None, :]
    cos_b, sin_b = both(jnp.cos(ang_b)), both(jnp.sin(ang_b))
    sign = jnp.concatenate([-jnp.ones((half,), jnp.float32), jnp.ones((half,), jnp.float32)])
    return rot_a, jnp.stack([cos_b, sin_b, cos_b * sign, sin_b * sign])


def _rotary_chunk(ra_ref, c, rb_ref):
    cos_a, sin_a = ra_ref[c, 0:1, :], ra_ref[c, 1:2, :]
    return cos_a * rb_ref[0] - sin_a * rb_ref[1], sin_a * rb_ref[2] + cos_a * rb_ref[3]


def _mesh_pos():
    return lax.axis_index("x"), lax.axis_index("y"), lax.axis_index("c")


def _flip(v, bit):
    return 1 - v if bit else v


def _gather_rows(v, name):
    rows = v.shape[0]

    def body(v_ref, out_ref, send_sems, recv_sems):
        _all_devices_exchange(v_ref, out_ref, send_sems, recv_sems)

    return pl.pallas_call(
        body, name=name,
        out_shape=jax.ShapeDtypeStruct((N_DEV, rows, 128), jnp.float32),
        in_specs=[pl.BlockSpec(memory_space=pltpu.VMEM)],
        out_specs=pl.BlockSpec(memory_space=pltpu.VMEM),
        scratch_shapes=_all_devices_sems(),
    )(v)


def _all_devices_sems():
    return [pltpu.SemaphoreType.DMA((N_DEV - 1,)), pltpu.SemaphoreType.DMA((N_DEV - 1,))]


def _all_devices_exchange(v_ref, out_ref, send_sems, recv_sems):
    x, y, c = _mesh_pos()
    me = 4 * x + 2 * y + c
    out_ref[me] = v_ref[...]
    sends, recvs = [], []
    for k in range(1, N_DEV):
        px, py, pc = _flip(x, (k >> 2) & 1), _flip(y, (k >> 1) & 1), _flip(c, k & 1)
        peer = 4 * px + 2 * py + pc
        sends.append(pltpu.make_async_remote_copy(
            src_ref=v_ref, dst_ref=out_ref.at[me], send_sem=send_sems.at[k - 1], recv_sem=recv_sems.at[k - 1],
            device_id=(px, py, pc), device_id_type=MESH))
        recvs.append(pltpu.make_async_remote_copy(
            src_ref=v_ref, dst_ref=out_ref.at[peer], send_sem=send_sems.at[k - 1], recv_sem=recv_sems.at[k - 1],
            device_id=(px, py, pc), device_id_type=MESH))
    for cp in sends:
        cp.start()
    for cp in recvs:
        cp.wait_recv()
    for cp in sends:
        cp.wait_send()


def _prologue(cond_rows, w_ada_blk, b_blk, w_in_t):
    cols = w_ada_blk.shape[1]
    groups = cols // 128
    c_rows = D_MODEL // 128

    def body(cond_ref, w_ref, b_ref, win_ref, cond_all_ref, mod_all_ref, stack_ref, mod_sc, *sems):
        gather = _ChipGather([win_ref], [stack_ref], sems[:5])
        gather.start()
        _all_devices_exchange(cond_ref, cond_all_ref, sems[5], sems[6])
        acc = jnp.broadcast_to(b_ref[...], (N_DEV, cols))
        for r in range(c_rows):
            cv = cond_all_ref[:, r, :]
            acc = acc + _mm32(cv * _sigmoid(cv), w_ref[r * 128:(r + 1) * 128, :])
        for k in range(groups):
            mod_sc[k] = acc[:, k * 128:(k + 1) * 128]
        _all_devices_exchange(mod_sc, mod_all_ref, sems[7], sems[8])
        gather.forward()
        gather.finish()

    vmem = pl.BlockSpec(memory_space=pltpu.VMEM)
    hbm = pl.BlockSpec(memory_space=pl.ANY)
    return pl.pallas_call(
        body, name="prologue",
        out_shape=(jax.ShapeDtypeStruct((N_DEV,) + cond_rows.shape, jnp.float32),
                   jax.ShapeDtypeStruct((N_DEV, groups, N_DEV, 128), jnp.float32))
        + _exchange_out_shapes([w_in_t], True),
        in_specs=[vmem, vmem, vmem, hbm],
        out_specs=(vmem, vmem, hbm),
        scratch_shapes=[pltpu.VMEM((groups, N_DEV, 128), jnp.float32)] + _gather_sems(1)
        + _all_devices_sems() + _all_devices_sems(),
        compiler_params=pltpu.CompilerParams(vmem_limit_bytes=VMEM_LIMIT),
    )(cond_rows, w_ada_blk, b_blk, w_in_t)


def _exchange_out_shapes(arrays, gather):
    return tuple(jax.ShapeDtypeStruct((N_CHIP,) + a.shape if gather else a.shape, a.dtype) for a in arrays)


def _scatter_sems(n):
    n_sem = n * (N_CHIP - 1)
    return [pltpu.SemaphoreType.DMA((n_sem,)), pltpu.SemaphoreType.DMA((n_sem,)), pltpu.SemaphoreType.DMA((n,))]


def _gather_sems(n):
    n_sem = n * (N_CHIP - 1)
    return [pltpu.SemaphoreType.DMA((n_sem,))] * 4 + [pltpu.SemaphoreType.DMA((n,))]


def _peer_chips(x, y):
    out = []
    for k in range(1, N_CHIP):
        px, py = _flip(x, (k >> 1) & 1), _flip(y, k & 1)
        out.append((px, py, 2 * px + py))
    return out


class _ChipScatter:
    def __init__(self, ins, outs, sems):
        send_sems, recv_sems, local_sems = sems
        x, y, c = _mesh_pos()
        chip = 2 * x + y
        self.local, self.sends, self.recvs = [], [], []
        for i in range(len(ins)):
            self.local.append(pltpu.make_async_copy(ins[i].at[chip], outs[i].at[chip], local_sems.at[i]))
            for k, (px, py, peer_chip) in enumerate(_peer_chips(x, y)):
                sem = i * (N_CHIP - 1) + k
                src = ins[i].at[peer_chip]
                self.sends.append(pltpu.make_async_remote_copy(
                    src_ref=src, dst_ref=outs[i].at[chip], send_sem=send_sems.at[sem], recv_sem=recv_sems.at[sem],
                    device_id=(px, py, c), device_id_type=MESH))
                self.recvs.append(pltpu.make_async_remote_copy(
                    src_ref=src, dst_ref=outs[i].at[peer_chip], send_sem=send_sems.at[sem], recv_sem=recv_sems.at[sem],
                    device_id=(px, py, c), device_id_type=MESH))

    def start(self):
        for cp in self.local + self.sends:
            cp.start()

    def wait(self):
        for cp in self.recvs:
            cp.wait_recv()
        for cp in self.sends:
            cp.wait_send()
        for cp in self.local:
            cp.wait()


class _ChipGather:
    def __init__(self, ins, outs, sems):
        ici_send, ici_recv, d2d_send, d2d_recv, local_sems = sems
        x, y, c = _mesh_pos()
        chip = 2 * x + y
        self.local, self.ici_sends, self.ici_recvs, self.d2d_sends, self.d2d_recvs = [], [], [], [], []
        for i in range(len(ins)):
            half = ins[i].shape[-1] // 2
            assert half % 128 == 0
            lead = (slice(None),) * (len(ins[i].shape) - 1)
            mine = lead + (pl.ds(pl.multiple_of(c * half, 128), half),)
            theirs = lead + (pl.ds(pl.multiple_of((1 - c) * half, 128), half),)
            self.local.append(pltpu.make_async_copy(ins[i], outs[i].at[chip], local_sems.at[i]))
            for k, (px, py, peer_chip) in enumerate(_peer_chips(x, y)):
                sem = i * (N_CHIP - 1) + k
                self.ici_sends.append(pltpu.make_async_remote_copy(
                    src_ref=ins[i].at[mine], dst_ref=outs[i].at[chip].at[mine],
                    send_sem=ici_send.at[sem], recv_sem=ici_recv.at[sem], device_id=(px, py, c), device_id_type=MESH))
                landed = outs[i].at[peer_chip].at[mine]
                self.ici_recvs.append(pltpu.make_async_remote_copy(
                    src_ref=ins[i].at[mine], dst_ref=landed,
                    send_sem=ici_send.at[sem], recv_sem=ici_recv.at[sem], device_id=(px, py, c), device_id_type=MESH))
                self.d2d_sends.append(pltpu.make_async_remote_copy(
                    src_ref=landed, dst_ref=landed,
                    send_sem=d2d_send.at[sem], recv_sem=d2d_recv.at[sem], device_id=(x, y, 1 - c), device_id_type=MESH))
                self.d2d_recvs.append(pltpu.make_async_remote_copy(
                    src_ref=landed, dst_ref=outs[i].at[peer_chip].at[theirs],
                    send_sem=d2d_send.at[sem], recv_sem=d2d_recv.at[sem], device_id=(x, y, 1 - c), device_id_type=MESH))

    def start(self):
        for cp in self.local + self.ici_sends:
            cp.start()

    def forward(self):
        for landed, onward in zip(self.ici_recvs, self.d2d_sends):
            landed.wait_recv()
            onward.start()

    def finish(self):
        for cp in self.d2d_recvs:
            cp.wait_recv()
        for cp in self.d2d_sends + self.ici_sends:
            cp.wait_send()
        for cp in self.local:
            cp.wait()


def _sibling_swap(arrays, name):
    n = len(arrays)

    def body(*refs):
        swap = _SiblingSwap(refs[:n], refs[n:2 * n], refs[2 * n:])
        swap.start()
        swap.wait()

    return pl.pallas_call(
        body, name=name,
        out_shape=tuple(jax.ShapeDtypeStruct(a.shape, a.dtype) for a in arrays),
        in_specs=[pl.BlockSpec(memory_space=pl.ANY)] * n,
        out_specs=tuple(pl.BlockSpec(memory_space=pl.ANY) for _ in arrays),
        scratch_shapes=_swap_sems(n),
    )(*arrays)


def _swap_sems(n):
    return [pltpu.SemaphoreType.DMA((n,)), pltpu.SemaphoreType.DMA((n,))]


class _SiblingSwap:
    def __init__(self, ins, outs, sems):
        send_sems, recv_sems = sems
        x, y, c = _mesh_pos()
        self.copies = [pltpu.make_async_remote_copy(
            src_ref=ins[i], dst_ref=outs[i], send_sem=send_sems.at[i], recv_sem=recv_sems.at[i],
            device_id=(x, y, 1 - c), device_id_type=MESH) for i in range(len(ins))]

    def start(self):
        for cp in self.copies:
            cp.start()

    def wait(self):
        for cp in self.copies:
            cp.wait_recv()
        for cp in self.copies:
            cp.wait_send()


def _adam(w, g, m, v):
    m2 = ADAM_B1 * m + (1.0 - ADAM_B1) * g
    v2 = ADAM_B2 * v + (1.0 - ADAM_B2) * (g * g)
    m_hat = m2 / (1.0 - ADAM_B1 ** ADAM_STEP)
    v_hat = v2 / (1.0 - ADAM_B2 ** ADAM_STEP)
    delta = -ADAM_LR * (m_hat / (jnp.sqrt(v_hat) + ADAM_EPS) + ADAM_WD * w)
    return delta, m2, v2


def _ada_bwd_adam(c_t, dmod_blk, w, m, v):
    rows, cols = w.shape
    tile = 512
    assert cols % tile == 0

    def body(c_ref, d_ref, w_ref, m_ref, v_ref, g_ref, dl_ref, m2_ref, v2_ref):
        sc = c_ref[...]
        sc = sc * _sigmoid(sc)
        dm = d_ref[...]
        g = sc[:, 0:1] * dm[0:1, :]
        for b in range(1, N_DEV):
            g = g + sc[:, b:b + 1] * dm[b:b + 1, :]
        delta, m2, v2 = _adam(w_ref[...], g, m_ref[...], v_ref[...])
        g_ref[...] = g
        dl_ref[...] = delta
        m2_ref[...] = m2
        v2_ref[...] = v2

    blk = pl.BlockSpec((rows, tile), lambda j: (0, j))
    out = jax.ShapeDtypeStruct((rows, cols), jnp.float32)
    return pl.pallas_call(
        body, name="ada_bwd_adam", grid=(cols // tile,),
        out_shape=(out, out, out, out),
        in_specs=[pl.BlockSpec((rows, N_DEV), lambda j: (0, 0)), pl.BlockSpec((N_DEV, tile), lambda j: (0, j)),
                  blk, blk, blk],
        out_specs=(blk, blk, blk, blk),
        compiler_params=_params(("arbitrary",)),
    )(c_t, dmod_blk, w, m, v)


def _inproj_fwd(x2, vecs, w_in_p, tm, riders):
    seq = x2.shape[0]
    n_tiles = seq // tm
    n_ride = len(riders)

    def body(*refs):
        x_ref, vec_ref, w_ref = refs[:3]
        ride_in, refs = refs[3:3 + n_ride], refs[3 + n_ride:]
        p_ref, u_ref = refs[:2]
        ride_out, sems = refs[2:2 + n_ride], refs[2 + n_ride:]
        gather = _ChipGather(ride_in, ride_out, sems)

        @pl.when(pl.program_id(0) == 0)
        def _():
            gather.start()

        xh, _ = _ln(x_ref[...])
        u = (xh * (1.0 + vec_ref[1:2, :]) + vec_ref[0:1, :]).astype(MXU_DTYPE)
        u_ref[...] = u
        p_ref[...] = _mm(u, w_ref[...])

        @pl.when(pl.program_id(0) == (3 * n_tiles) // 4)
        def _():
            gather.forward()

        @pl.when(pl.program_id(0) == n_tiles - 1)
        def _():
            gather.finish()

    hbm = pl.BlockSpec(memory_space=pl.ANY)
    return pl.pallas_call(
        body, name="inproj_fwd", grid=(n_tiles,),
        out_shape=(jax.ShapeDtypeStruct((seq, N_PROJ), jnp.float32), jax.ShapeDtypeStruct((seq, D_MODEL), MXU_DTYPE))
        + _exchange_out_shapes(riders, True),
        in_specs=[pl.BlockSpec((tm, D_MODEL), lambda i: (i, 0)), _const_spec(vecs.shape), _const_spec(w_in_p.shape)]
        + [hbm] * n_ride,
        out_specs=(pl.BlockSpec((tm, N_PROJ), lambda i: (i, 0)), pl.BlockSpec((tm, D_MODEL), lambda i: (i, 0)))
        + (hbm,) * n_ride,
        scratch_shapes=_gather_sems(n_ride),
        compiler_params=_params(("arbitrary",)),
    )(x2, vecs, w_in_p, *riders)


def _inproj_bwd(dproj, x2, dxa, vecs, w_in_pt, tm, riders):
    seq = x2.shape[0]
    n_tiles = seq // tm
    n_ride = len(riders)

    def body(*refs):
        dp_ref, x_ref, dxa_ref, vec_ref, w_ref = refs[:5]
        ride_in, refs = refs[5:5 + n_ride], refs[5 + n_ride:]
        gx_ref, sums_ref = refs[:2]
        ride_out, sems = refs[2:2 + n_ride], refs[2 + n_ride:]
        exchange = _ChipScatter(ride_in, ride_out, sems)

        @pl.when(pl.program_id(0) == 0)
        def _():
            exchange.start()
            sums_ref[...] = jnp.zeros_like(sums_ref)

        du = _mm(dp_ref[...], w_ref[...])
        xh, rstd = _ln(x_ref[...])
        sums_ref[0:1, :] += _colsum(du)
        sums_ref[1:2, :] += _colsum(du * xh)
        gx_ref[...] = dxa_ref[...] + _ln_bwd(du * (1.0 + vec_ref[1:2, :]), xh, rstd)

        @pl.when(pl.program_id(0) == n_tiles - 1)
        def _():
            exchange.wait()

    tile = pl.BlockSpec((tm, D_MODEL), lambda i: (i, 0))
    hbm = pl.BlockSpec(memory_space=pl.ANY)
    return pl.pallas_call(
        body, name="inproj_bwd", grid=(n_tiles,),
        out_shape=(jax.ShapeDtypeStruct((seq, D_MODEL), jnp.float32), jax.ShapeDtypeStruct((8, D_MODEL), jnp.float32))
        + _exchange_out_shapes(riders, False),
        in_specs=[pl.BlockSpec((tm, N_PROJ), lambda i: (i, 0)), tile, tile, _const_spec(vecs.shape),
                  _const_spec(w_in_pt.shape)] + [hbm] * n_ride,
        out_specs=(tile, pl.BlockSpec((8, D_MODEL), lambda i: (0, 0))) + (hbm,) * n_ride,
        scratch_shapes=_scatter_sems(n_ride),
        compiler_params=_params(("arbitrary",)),
    )(dproj, x2, dxa, vecs, w_in_pt, *riders)


def _head(h):
    return slice(h * HEAD_W, (h + 1) * HEAD_W)


def _cols(ref, off, h):
    return ref[:, off + h * HEAD_W:off + (h + 1) * HEAD_W]


HEADS = range(N_HEADS)


def _mixer_chunk_forward(p_ref, cc, ss, dm_ref, qdec_ref, kdec_ref, wg_ref, bg_ref, states):
    row, col = _tri_masks()
    lower = row >= col
    f = {}
    f["glr"] = p_ref[:, OFF_LR:OFF_LR + HEAD_W]
    f["logit"] = _mm(f["glr"], wg_ref[...]) + bg_ref[...]
    rq = [_cols(p_ref, OFF_RQ, h) for h in HEADS]
    rk = [_cols(p_ref, OFF_RK, h) for h in HEADS]
    f["rv"] = [_cols(p_ref, OFF_RV, h) for h in HEADS]
    f["qr"] = [(rq[h] * cc + _swap_halves(rq[h]) * ss) * RET_SCALE for h in HEADS]
    f["kr"] = [rk[h] * cc + _swap_halves(rk[h]) * ss for h in HEADS]
    s_raw = [_mm_nt(f["qr"][h], f["kr"][h]) for h in HEADS]
    yield
    la = _log_sigmoid(f["logit"]) * (1.0 / GATE_TAU)
    b = _running_sum(lower, la)
    f["qd"] = [f["qr"][h] * qdec_ref[:, _head(h)] for h in HEADS]
    f["kd"] = [f["kr"][h] * kdec_ref[:, _head(h)] for h in HEADS]
    f["scores"] = [s_raw[h] * dm_ref[h] for h in HEADS]
    yield
    b_last = b[CHUNK - 1:CHUNK, :]
    b_mid = b[CHUNK // 2 - 1:CHUNK // 2, :]
    f["e"], f["ei"] = jnp.exp(b - b_mid), jnp.exp(b_mid - b)
    f["eb"], f["ek"], f["ebl"] = jnp.exp(b), jnp.exp(b_last - b), jnp.exp(b_last)
    gq = [_cols(p_ref, OFF_GQ, h) * GLA_SCALE for h in HEADS]
    gk = [_cols(p_ref, OFF_GK, h) for h in HEADS]
    f["gv"] = [_cols(p_ref, OFF_GV, h) for h in HEADS]
    f["q_e"] = [gq[h] * f["e"][:, _head(h)] for h in HEADS]
    f["q_i"] = [gq[h] * f["ei"][:, _head(h)] for h in HEADS]
    f["k_e"] = [gk[h] * f["e"][:, _head(h)] for h in HEADS]
    f["k_i"] = [gk[h] * f["ei"][:, _head(h)] for h in HEADS]
    low = [_mm_nt(f["q_e"][h], f["k_i"][h]) for h in HEADS]
    up = [_mm_nt(f["q_i"][h], f["k_e"][h]) for h in HEADS]
    yield
    f["att"] = [jnp.where(lower, low[h], up[h]) for h in HEADS]
    f["qb"] = [gq[h] * f["eb"][:, _head(h)] for h in HEADS]
    f["kb"] = [gk[h] * f["ek"][:, _head(h)] for h in HEADS]
    ret_state, gla_state_t = states()
    f["o_ret"] = [_mm(f["scores"][h], f["rv"][h]) + _mm(f["qd"][h], ret_state[h]) for h in HEADS]
    f["o_gla"] = [_mm(f["att"][h], f["gv"][h]) + _mm_nt(f["qb"][h], gla_state_t[h]) for h in HEADS]
    return f


def _interleave(generators):
    live = list(generators)
    while live:
        for g in list(live):
            try:
                next(g)
            except StopIteration:
                live.remove(g)


def _mixer_fwd(proj, tables, wg_p, bg_p, ret_norm_w, gla_norm_w, riders):
    seq = proj.shape[0]
    n_chunks = seq // CHUNK
    per_step = min(n_chunks, CHUNKS_PER_STEP)
    n_steps = n_chunks // per_step
    n_ride = len(riders)
    rot_a, rot_b, dm_t, qdec_t, kdec_t, chunk_decay = tables

    def body(*refs):
        p_ref, ra_ref, rb_ref, dm_ref, qdec_ref, kdec_ref, wg_ref, bg_ref, wr_ref, wl_ref = refs[:10]
        ride_in, refs = refs[10:10 + n_ride], refs[10 + n_ride:]
        mix_ref, rsave_ref, ssave_ref = refs[:3]
        ride_out, refs = refs[3:3 + n_ride], refs[3 + n_ride:]
        r_sc, s_sc = refs[:2]
        gather = _ChipGather(ride_in, ride_out, refs[2:])

        @pl.when(pl.program_id(0) == 0)
        def _():
            gather.start()
            r_sc[...] = jnp.zeros_like(r_sc)
            s_sc[...] = jnp.zeros_like(s_sc)

        def one_chunk(c):
            p_c = p_ref.at[c * CHUNK:(c + 1) * CHUNK, :]
            mix_c = mix_ref.at[c * CHUNK:(c + 1) * CHUNK, :]
            before = {}

            def states():
                before["ret"] = [r_sc[h] for h in HEADS]
                before["gla"] = [s_sc[h] for h in HEADS]
                for h in HEADS:
                    rsave_ref[c, h] = before["ret"][h]
                    ssave_ref[c, h] = before["gla"][h]
                return before["ret"], before["gla"]

            cc, ss = _rotary_chunk(ra_ref, c, rb_ref)
            f = yield from _mixer_chunk_forward(p_c, cc, ss, dm_ref, qdec_ref, kdec_ref, wg_ref, bg_ref, states)
            for h in HEADS:
                r_sc[h] = chunk_decay[h] * before["ret"][h] + _mm_tn(f["kd"][h], f["rv"][h])
            for h in HEADS:
                s_sc[h] = before["gla"][h] * f["ebl"][:, _head(h)] + _mm_tn(f["gv"][h], f["kb"][h])
            yield
            for h in HEADS:
                on, _ = _ln(f["o_ret"][h])
                g = _cols(p_c, OFF_RG, h)
                mix_c[:, _head(h)] = (on * wr_ref[:, _head(h)] * (g * _sigmoid(g))).astype(mix_ref.dtype)
            for h in HEADS:
                o = f["o_gla"][h]
                on = o * lax.rsqrt(_rowmean(o * o) + LN_EPS)
                g = _cols(p_c, OFF_GG, h)
                mix_c[:, _head(N_HEADS + h)] = (on * wl_ref[:, _head(h)] * (g * _sigmoid(g))).astype(mix_ref.dtype)

        for c0 in range(0, per_step, CHUNKS_IN_LOCKSTEP):
            _interleave([one_chunk(c) for c in range(c0, min(per_step, c0 + CHUNKS_IN_LOCKSTEP))])

        @pl.when(pl.program_id(0) == (3 * n_steps) // 4)
        def _():
            gather.forward()

        @pl.when(pl.program_id(0) == n_steps - 1)
        def _():
            gather.finish()

    state_shape = (n_chunks, N_HEADS, HEAD_W, HEAD_W)
    state_blk = pl.BlockSpec((per_step, N_HEADS, HEAD_W, HEAD_W), lambda i: (i, 0, 0, 0))
    rot_blk = pl.BlockSpec((per_step, 8, HEAD_W), lambda i: (i, 0, 0))
    rows = per_step * CHUNK
    hbm = pl.BlockSpec(memory_space=pl.ANY)
    return pl.pallas_call(
        body, name="mixer_fwd", grid=(n_steps,),
        out_shape=(jax.ShapeDtypeStruct((seq, D_MODEL), MXU_DTYPE),
                   jax.ShapeDtypeStruct(state_shape, jnp.float32), jax.ShapeDtypeStruct(state_shape, jnp.float32))
        + _exchange_out_shapes(riders, True),
        in_specs=[pl.BlockSpec((rows, N_PROJ), lambda i: (i, 0)), rot_blk, _const_spec(rot_b.shape),
                  _const_spec(dm_t.shape), _const_spec(qdec_t.shape), _const_spec(kdec_t.shape),
                  _const_spec(wg_p.shape), _const_spec(bg_p.shape), _const_spec(ret_norm_w.shape),
                  _const_spec(gla_norm_w.shape)] + [hbm] * n_ride,
        out_specs=(pl.BlockSpec((rows, D_MODEL), lambda i: (i, 0)), state_blk, state_blk) + (hbm,) * n_ride,
        scratch_shapes=[pltpu.VMEM((N_HEADS, HEAD_W, HEAD_W), jnp.float32),
                        pltpu.VMEM((N_HEADS, HEAD_W, HEAD_W), jnp.float32)] + _gather_sems(n_ride),
        compiler_params=_params(("arbitrary",)),
    )(proj, rot_a, rot_b, dm_t, qdec_t, kdec_t, wg_p, bg_p, ret_norm_w, gla_norm_w, *riders)


def _mixer_bwd(proj, dmixed, rsave, ssave, tables, wg_p, bg_p, ret_norm_w, gla_norm_w, riders):
    seq = proj.shape[0]
    n_chunks = seq // CHUNK
    per_step = min(n_chunks, CHUNKS_PER_STEP)
    n_steps = n_chunks // per_step
    n_ride = len(riders)
    rot_a, rot_b, dm_t, qdec_t, kdec_t, chunk_decay = tables
    last = n_steps - 1

    def body(*refs):
        p_blk, dmx_blk = refs[:2]
        shared_in = refs[2:13]
        ride_in, refs = refs[13:13 + n_ride], refs[13 + n_ride:]
        dp_blk, dwr_ref, dwl_ref, dwg_ref, dbg_ref = refs[:5]
        ride_out, refs = refs[5:5 + n_ride], refs[5 + n_ride:]
        dr_sc, ds_sc = refs[:2]
        exchange = _ChipScatter(ride_in, ride_out, refs[2:])

        @pl.when(pl.program_id(0) == 0)
        def _():
            exchange.start()
            dr_sc[...] = jnp.zeros_like(dr_sc)
            ds_sc[...] = jnp.zeros_like(ds_sc)
            dwr_ref[...] = jnp.zeros_like(dwr_ref)
            dwl_ref[...] = jnp.zeros_like(dwl_ref)
            dwg_ref[...] = jnp.zeros_like(dwg_ref)
            dbg_ref[...] = jnp.zeros_like(dbg_ref)

        def chunk_stages(c):
            rows = slice(c * CHUNK, (c + 1) * CHUNK)
            return one_chunk(c, p_blk.at[rows, :], dmx_blk.at[rows, :], dp_blk.at[rows, :], *shared_in,
                             dwr_ref, dwl_ref, dwg_ref, dbg_ref, dr_sc, ds_sc)

        for c0 in range(per_step, 0, -CHUNKS_IN_LOCKSTEP):
            _interleave([chunk_stages(c) for c in reversed(range(max(0, c0 - CHUNKS_IN_LOCKSTEP), c0))])

        @pl.when(pl.program_id(0) == last)
        def _():
            exchange.wait()

    def one_chunk(c, p_ref, dmx_ref, dp_ref, rsave_ref, ssave_ref, ra_ref, rb_ref, dm_ref, qdec_ref, kdec_ref,
                  wg_ref, bg_ref, wr_ref, wl_ref, dwr_ref, dwl_ref, dwg_ref, dbg_ref, dr_sc, ds_sc):
        def put(off, h, val):
            dp_ref[:, off + h * HEAD_W:off + (h + 1) * HEAD_W] = val.astype(dp_ref.dtype)

        cc, ss = _rotary_chunk(ra_ref, c, rb_ref)
        row, col = _tri_masks()
        ret_state = [rsave_ref[c, h] for h in HEADS]
        gla_state_t = [ssave_ref[c, h] for h in HEADS]
        f = yield from _mixer_chunk_forward(p_ref, cc, ss, dm_ref, qdec_ref, kdec_ref, wg_ref, bg_ref,
                                            lambda: (ret_state, gla_state_t))
        yield

        do_ret, do_gla = [], []
        for h in HEADS:
            on, rstd = _ln(f["o_ret"][h])
            g = _cols(p_ref, OFF_RG, h)
            sg = _sigmoid(g)
            dy = dmx_ref[:, _head(h)].astype(jnp.float32)
            wr = wr_ref[:, _head(h)]
            dwr_ref[:, _head(h)] += _colsum(dy * on * (g * sg))
            put(OFF_RG, h, dy * on * wr * (sg * (1.0 + g * (1.0 - sg))))
            do_ret.append(_ln_bwd(dy * wr * (g * sg), on, rstd))
        for h in HEADS:
            o = f["o_gla"][h]
            rstd = lax.rsqrt(_rowmean(o * o) + LN_EPS)
            on = o * rstd
            g = _cols(p_ref, OFF_GG, h)
            sg = _sigmoid(g)
            dy = dmx_ref[:, _head(N_HEADS + h)].astype(jnp.float32)
            wl = wl_ref[:, _head(h)]
            dwl_ref[:, _head(h)] += _colsum(dy * on * (g * sg))
            put(OFF_GG, h, dy * on * wl * (sg * (1.0 + g * (1.0 - sg))))
            don = dy * wl * (g * sg)
            do_gla.append(rstd * (don - on * _rowmean(don * on)))

        yield

        d_ret_new = [dr_sc[h] for h in HEADS]
        d_gla_new = [ds_sc[h] for h in HEADS]
        ds_raw = [_mm_nt(do_ret[h], f["rv"][h]) * dm_ref[h] for h in HEADS]
        d_att = [_mm_nt(do_gla[h], f["gv"][h]) for h in HEADS]
        dq_state = [_mm_nt(do_ret[h], ret_state[h]) for h in HEADS]
        dk_state = [_mm_nt(f["rv"][h], d_ret_new[h]) for h in HEADS]
        dqb = [_mm(do_gla[h], gla_state_t[h]) for h in HEADS]
        dkb = [_mm(f["gv"][h], d_gla_new[h]) for h in HEADS]
        for h in HEADS:
            put(OFF_RV, h, _mm_tn(f["scores"][h], do_ret[h]) + _mm(f["kd"][h], d_ret_new[h]))
        for h in HEADS:
            put(OFF_GV, h, _mm_tn(f["att"][h], do_gla[h]) + _mm_nt(f["kb"][h], d_gla_new[h]))
        for h in HEADS:
            dr_sc[h] = chunk_decay[h] * d_ret_new[h] + _mm_tn(f["qd"][h], do_ret[h])
        for h in HEADS:
            ds_sc[h] = d_gla_new[h] * f["ebl"][:, _head(h)] + _mm_tn(do_gla[h], f["qb"][h])
        yield

        dqr = [_mm(ds_raw[h], f["kr"][h]) + dq_state[h] * qdec_ref[:, _head(h)] for h in HEADS]
        dkr = [_mm_tn(ds_raw[h], f["qr"][h]) + dk_state[h] * kdec_ref[:, _head(h)] for h in HEADS]
        d_low = [jnp.where(row >= col, d_att[h], 0.0) for h in HEADS]
        d_up = [jnp.where(row < col, d_att[h], 0.0) for h in HEADS]
        dq_e = [_mm(d_low[h], f["k_i"][h]) for h in HEADS]
        dk_i = [_mm_tn(d_low[h], f["q_e"][h]) for h in HEADS]
        dq_i = [_mm(d_up[h], f["k_e"][h]) for h in HEADS]
        dk_e = [_mm_tn(d_up[h], f["q_i"][h]) for h in HEADS]
        yield
        for h in HEADS:
            put(OFF_RQ, h, (dqr[h] * cc + _swap_halves(dqr[h] * ss)) * RET_SCALE)
            put(OFF_RK, h, dkr[h] * cc + _swap_halves(dkr[h] * ss))
        row_id = lax.broadcasted_iota(jnp.int32, (CHUNK, HEAD_W), 0)
        db_heads = []
        for h in HEADS:
            hs = _head(h)
            e, ei, eb, ek, ebl = f["e"][:, hs], f["ei"][:, hs], f["eb"][:, hs], f["ek"][:, hs], f["ebl"][:, hs]
            put(OFF_GQ, h, (dq_e[h] * e + dq_i[h] * ei + dqb[h] * eb) * GLA_SCALE)
            put(OFF_GK, h, dk_e[h] * e + dk_i[h] * ei + dkb[h] * ek)
            db = (dq_e[h] * f["q_e"][h] - dq_i[h] * f["q_i"][h] + dk_e[h] * f["k_e"][h] - dk_i[h] * f["k_i"][h]
                  + dqb[h] * f["qb"][h] - dkb[h] * f["kb"][h])
            db_last = _colsum(dkb[h] * f["kb"][h]) + ebl * _colsum(gla_state_t[h] * d_gla_new[h])
            db_heads.append(db + jnp.where(row_id == CHUNK - 1, db_last, 0.0))
        db = jnp.concatenate(db_heads, axis=1)
        d_la = _running_sum(col >= row, db)
        d_logit = d_la * (1.0 / GATE_TAU) * (1.0 - _sigmoid(f["logit"]))
        put(OFF_LR, 0, _mm_nt(d_logit, wg_ref[...]))
        dwg_ref[...] += _mm_tn(f["glr"], d_logit)
        dbg_ref[...] += _colsum(d_logit)

    state_blk = pl.BlockSpec((per_step, N_HEADS, HEAD_W, HEAD_W), lambda i: (last - i, 0, 0, 0))
    rot_blk = pl.BlockSpec((per_step, 8, HEAD_W), lambda i: (last - i, 0, 0))
    width = N_HEADS * HEAD_W
    vec_out = pl.BlockSpec((1, width), lambda i: (0, 0))
    hbm = pl.BlockSpec(memory_space=pl.ANY)
    rows_blk = per_step * CHUNK
    return pl.pallas_call(
        body, name="mixer_bwd", grid=(n_steps,),
        out_shape=(jax.ShapeDtypeStruct((seq, N_PROJ), MXU_DTYPE),
                   jax.ShapeDtypeStruct((1, width), jnp.float32), jax.ShapeDtypeStruct((1, width), jnp.float32),
                   jax.ShapeDtypeStruct((HEAD_W, width), jnp.float32), jax.ShapeDtypeStruct((1, width), jnp.float32))
        + _exchange_out_shapes(riders, False),
        in_specs=[pl.BlockSpec((rows_blk, N_PROJ), lambda i: (last - i, 0)),
                  pl.BlockSpec((rows_blk, D_MODEL), lambda i: (last - i, 0)), state_blk, state_blk, rot_blk,
                  _const_spec(rot_b.shape),
                  _const_spec(dm_t.shape), _const_spec(qdec_t.shape), _const_spec(kdec_t.shape),
                  _const_spec(wg_p.shape), _const_spec(bg_p.shape), _const_spec(ret_norm_w.shape),
                  _const_spec(gla_norm_w.shape)] + [hbm] * n_ride,
        out_specs=(pl.BlockSpec((rows_blk, N_PROJ), lambda i: (last - i, 0)), vec_out, vec_out,
                   pl.BlockSpec((HEAD_W, width), lambda i: (0, 0)), vec_out) + (hbm,) * n_ride,
        scratch_shapes=[pltpu.VMEM((N_HEADS, HEAD_W, HEAD_W), jnp.float32),
                        pltpu.VMEM((N_HEADS, HEAD_W, HEAD_W), jnp.float32)] + _scatter_sems(n_ride),
        compiler_params=_params(("arbitrary",)),
    )(proj, dmixed, rsave, ssave, rot_a, rot_b, dm_t, qdec_t, kdec_t, wg_p, bg_p, ret_norm_w, gla_norm_w, *riders)


V_GATE1, V_SCALE2, V_SHIFT2, V_GATE2, V_LN1W, V_LN1B, V_LN2W, V_LN2B = range(8)
S_GATE1, S_SCALE2, S_SHIFT2, S_GATE2, S_LN1W, S_LN1B, S_LN2W, S_LN2B, S_LOSS = range(9)


def _mlp_fwd_bwd(x2, mixed, target, vecs, w_out, w1_chunks, w2_chunks, tm):
    seq = x2.shape[0]
    n_fc, _, fc = w1_chunks.shape

    def body(x_ref, mx_ref, t_ref, vec_ref, wo_ref, w1_ref, w2_ref,
             dmx_ref, dxa_ref, a_ref, dh_ref, u2_ref, df_ref, dm_ref, sums_ref, relu_sc):
        @pl.when(pl.program_id(0) == 0)
        def _():
            sums_ref[...] = jnp.zeros_like(sums_ref)

        vec = lambda r: vec_ref[r:r + 1, :]

        def acc(r, val):
            sums_ref[r:r + 1, :] += _colsum(val)

        xx = x_ref[...]
        m = _mm(mx_ref[...], wo_ref[...])
        z1h, rstd1 = _ln(ALPHA * xx + vec(V_GATE1) * m)
        x1 = z1h * vec(V_LN1W) + vec(V_LN1B)
        x1h, rstd0 = _ln(x1)
        u2 = (x1h * (1.0 + vec(V_SCALE2)) + vec(V_SHIFT2)).astype(MXU_DTYPE)
        u2_ref[...] = u2
        f = jnp.zeros((tm, D_MODEL), jnp.float32)
        for j in range(n_fc):
            r = jnp.maximum(_mm(u2, w1_ref[j]), 0.0)
            relu_sc[:, j * fc:(j + 1) * fc] = r
            a = (r * r).astype(MXU_DTYPE)
            a_ref[:, j * fc:(j + 1) * fc] = a
            f = f + _mm(a, w2_ref[j])
        z2h, rstd2 = _ln(ALPHA * x1 + vec(V_GATE2) * f)
        err = z2h * vec(V_LN2W) + vec(V_LN2B) - t_ref[...]
        acc(S_LOSS, err * err)
        dy = err * (1.0 / D_MODEL)
        acc(S_LN2W, dy * z2h)
        acc(S_LN2B, dy)
        dz2 = _ln_bwd(dy * vec(V_LN2W), z2h, rstd2)
        acc(S_GATE2, dz2 * f)
        df = (vec(V_GATE2) * dz2).astype(MXU_DTYPE)
        df_ref[...] = df
        du2 = jnp.zeros((tm, D_MODEL), jnp.float32)
        for j in range(n_fc):
            dh = (_mm_nt(df, w2_ref[j]) * (2.0 * relu_sc[:, j * fc:(j + 1) * fc])).astype(MXU_DTYPE)
            dh_ref[:, j * fc:(j + 1) * fc] = dh
            du2 = du2 + _mm_nt(dh, w1_ref[j])
        acc(S_SCALE2, du2 * x1h)
        acc(S_SHIFT2, du2)
        dx1 = ALPHA * dz2 + _ln_bwd(du2 * (1.0 + vec(V_SCALE2)), x1h, rstd0)
        acc(S_LN1W, dx1 * z1h)
        acc(S_LN1B, dx1)
        dz1 = _ln_bwd(dx1 * vec(V_LN1W), z1h, rstd1)
        acc(S_GATE1, dz1 * m)
        dxa_ref[...] = ALPHA * dz1
        dm = (vec(V_GATE1) * dz1).astype(MXU_DTYPE)
        dm_ref[...] = dm
        dmx_ref[...] = _mm_nt(dm, wo_ref[...])

    tile = lambda width: pl.BlockSpec((tm, width), lambda i: (i, 0))
    f32 = lambda width: jax.ShapeDtypeStruct((seq, width), jnp.float32)
    b16 = lambda width: jax.ShapeDtypeStruct((seq, width), MXU_DTYPE)
    return pl.pallas_call(
        body, name="mlp_fwd_bwd", grid=(seq // tm,),
        out_shape=(f32(D_MODEL), f32(D_MODEL), b16(D_FF), b16(D_FF), b16(D_MODEL), b16(D_MODEL), b16(D_MODEL),
                   jax.ShapeDtypeStruct((16, D_MODEL), jnp.float32)),
        in_specs=[tile(D_MODEL), tile(D_MODEL), tile(D_MODEL), _const_spec(vecs.shape), _const_spec(w_out.shape),
                  _const_spec(w1_chunks.shape), _const_spec(w2_chunks.shape)],
        out_specs=(tile(D_MODEL), tile(D_MODEL), tile(D_FF), tile(D_FF), tile(D_MODEL), tile(D_MODEL),
                   tile(D_MODEL), pl.BlockSpec((16, D_MODEL), lambda i: (0, 0))),
        scratch_shapes=[pltpu.VMEM((tm, D_FF), jnp.float32)],
        compiler_params=_params(("arbitrary",)),
    )(x2, mixed, target, vecs, w_out, w1_chunks, w2_chunks)


def _grad_matmul(a, b, name, tn, blocks_are_rows, riders=()):
    seq, m_dim = a.shape
    n_dim = b.shape[1]
    tk = min(seq, GRAD_TOKEN_TILE)
    nk = seq // tk
    n_ride = len(riders)
    if blocks_are_rows:
        tm = m_dim // N_CHIP
        assert tn == n_dim
        grid = (N_CHIP, 1, nk)
        out_map = lambda i, j, k: (i, 0, 0)
    else:
        tm = m_dim
        assert tn * N_CHIP == n_dim
        grid = (1, N_CHIP, nk)
        out_map = lambda i, j, k: (j, 0, 0)

    def body(*refs):
        a_ref, b_ref = refs[:2]
        ride_in, refs = refs[2:2 + n_ride], refs[2 + n_ride:]
        o_ref = refs[0]
        ride_out, refs = refs[1:1 + n_ride], refs[1 + n_ride:]
        acc_sc = refs[0]
        exchange = _ChipScatter(ride_in, ride_out, refs[1:]) if n_ride else None
        block = pl.program_id(0) + pl.program_id(1)
        k = pl.program_id(2)

        if exchange is not None:
            @pl.when((block == 0) & (k == 0))
            def _():
                exchange.start()

        @pl.when(k == 0)
        def _():
            acc_sc[...] = jnp.zeros_like(acc_sc)

        acc_sc[...] += _mm_tn(a_ref[...], b_ref[...])

        @pl.when(k == nk - 1)
        def _():
            o_ref[0] = acc_sc[...].astype(o_ref.dtype)

        if exchange is not None:
            @pl.when((block == N_CHIP - 1) & (k == nk - 1))
            def _():
                exchange.wait()

    hbm = pl.BlockSpec(memory_space=pl.ANY)
    out = pl.pallas_call(
        body, name=name, grid=grid,
        out_shape=(jax.ShapeDtypeStruct((N_CHIP, tm, tn), WIRE_DTYPE),) + _exchange_out_shapes(riders, False),
        in_specs=[pl.BlockSpec((tk, tm), lambda i, j, k: (k, i)), pl.BlockSpec((tk, tn), lambda i, j, k: (k, j))]
        + [hbm] * n_ride,
        out_specs=(pl.BlockSpec((1, tm, tn), out_map),) + (hbm,) * n_ride,
        scratch_shapes=[pltpu.VMEM((tm, tn), jnp.float32)] + (_scatter_sems(n_ride) if n_ride else []),
        compiler_params=_params(("arbitrary", "arbitrary", "arbitrary")),
    )(a, b, *riders)
    return out if n_ride else out[0]


def _grad_matmul_full(a, b, name, tm, riders):
    seq, m_dim = a.shape
    n_dim = b.shape[1]
    tk = min(seq, GRAD_TOKEN_TILE)
    nk = seq // tk
    n_blocks = m_dim // tm
    n_ride = len(riders)
    assert m_dim % tm == 0

    def body(*refs):
        a_ref, b_ref = refs[:2]
        ride_in, refs = refs[2:2 + n_ride], refs[2 + n_ride:]
        o_ref = refs[0]
        ride_out, refs = refs[1:1 + n_ride], refs[1 + n_ride:]
        acc_sc = refs[0]
        swap = _SiblingSwap(ride_in, ride_out, refs[1:])
        i, k = pl.program_id(0), pl.program_id(1)

        @pl.when((i == 0) & (k == 0))
        def _():
            swap.start()

        @pl.when(k == 0)
        def _():
            acc_sc[...] = jnp.zeros_like(acc_sc)

        acc_sc[...] += _mm_tn(a_ref[...], b_ref[...])

        @pl.when(k == nk - 1)
        def _():
            o_ref[...] = acc_sc[...].astype(o_ref.dtype)

        @pl.when((i == n_blocks - 1) & (k == nk - 1))
        def _():
            swap.wait()

    hbm = pl.BlockSpec(memory_space=pl.ANY)
    return pl.pallas_call(
        body, name=name, grid=(n_blocks, nk),
        out_shape=(jax.ShapeDtypeStruct((m_dim, n_dim), WIRE_DTYPE),)
        + tuple(jax.ShapeDtypeStruct(r.shape, r.dtype) for r in riders),
        in_specs=[pl.BlockSpec((tk, tm), lambda i, k: (k, i)), pl.BlockSpec((tk, n_dim), lambda i, k: (k, 0))]
        + [hbm] * n_ride,
        out_specs=(pl.BlockSpec((tm, n_dim), lambda i, k: (i, 0)),) + (hbm,) * n_ride,
        scratch_shapes=[pltpu.VMEM((tm, n_dim), jnp.float32)] + _swap_sems(n_ride),
        compiler_params=_params(("arbitrary", "arbitrary")),
    )(a, b, *riders)


def _sum_chips(stack, name):
    _, rows, cols = stack.shape
    tc = min(cols, ELEMENTWISE_COLS)

    def body(s_ref, o_ref):
        total = s_ref[0].astype(jnp.float32)
        for j in range(1, N_CHIP):
            total = total + s_ref[j].astype(jnp.float32)
        o_ref[...] = total

    return pl.pallas_call(
        body, name=name, grid=(cols // tc,),
        out_shape=jax.ShapeDtypeStruct((rows, cols), jnp.float32),
        in_specs=[pl.BlockSpec((N_CHIP, rows, tc), lambda i: (0, 0, i))],
        out_specs=pl.BlockSpec((rows, tc), lambda i: (0, i)),
        compiler_params=_params(("arbitrary",)),
    )(stack)


def _adam_pair(w, g_mine, g_sibling, m, v, name):
    rows, cols = w.shape
    tc = min(cols, ELEMENTWISE_COLS)

    def total(ref):
        if len(ref.shape) == 2:
            return ref[...]
        acc = ref[0].astype(jnp.float32)
        for j in range(1, ref.shape[0]):
            acc = acc + ref[j].astype(jnp.float32)
        return acc

    def body(w_ref, ga_ref, gb_ref, m_ref, v_ref, g_ref, dl_ref, m2_ref, v2_ref):
        g = total(ga_ref) + total(gb_ref)
        delta, m2, v2 = _adam(w_ref[...], g, m_ref[...], v_ref[...])
        g_ref[...] = g
        dl_ref[...] = delta
        m2_ref[...] = m2
        v2_ref[...] = v2

    blk = pl.BlockSpec((rows, tc), lambda i: (0, i))
    g_blk = lambda a: blk if a.ndim == 2 else pl.BlockSpec((a.shape[0], rows, tc), lambda i: (0, 0, i))
    out = jax.ShapeDtypeStruct((rows, cols), jnp.float32)
    return pl.pallas_call(
        body, name=name, grid=(cols // tc,),
        out_shape=(out, out, out, out),
        in_specs=[blk, g_blk(g_mine), g_blk(g_sibling), blk, blk], out_specs=(blk,) * 4,
        compiler_params=_params(("arbitrary",)),
    )(w, g_mine, g_sibling, m, v)


def _sum_devices(gathered):
    _, rows, _ = gathered.shape

    def body(g_ref, o_ref):
        total = g_ref[0]
        for d in range(1, N_DEV):
            total = total + g_ref[d]
        o_ref[...] = total

    return pl.pallas_call(
        body, name="sum_devices",
        out_shape=jax.ShapeDtypeStruct((rows, 128), jnp.float32),
    )(gathered)


def _adam_small(params):
    n = len(params)

    def body(*refs):
        ins, outs = refs[:4 * n], refs[4 * n:]
        for i in range(n):
            w_ref, g_ref, m_ref, v_ref = ins[4 * i:4 * i + 4]
            delta, m2, v2 = _adam(w_ref[...], g_ref[...], m_ref[...], v_ref[...])
            outs[3 * i][...] = delta
            outs[3 * i + 1][...] = m2
            outs[3 * i + 2][...] = v2

    out_shape = tuple(jax.ShapeDtypeStruct(p[0].shape, jnp.float32) for p in params for _ in range(3))
    out = pl.pallas_call(body, name="adam_small", out_shape=out_shape)(*[t for p in params for t in p])
    return [out[3 * i:3 * i + 3] for i in range(n)]


def _pad_heads(w):
    lead = w.shape[:-1]
    w = w.reshape(lead + (N_HEADS, GLA_DK))
    w = jnp.pad(w, [(0, 0)] * len(lead) + [(0, 0), (0, HEAD_W - GLA_DK)])
    return w.reshape(lead + (N_HEADS * HEAD_W,))


def _unpad_heads(w):
    lead = w.shape[:-1]
    return w.reshape(lead + (N_HEADS, HEAD_W))[..., :GLA_DK].reshape(lead + (N_HEADS * GLA_DK,))


def _pad_head_rows(w):
    w = w.reshape(N_HEADS, GLA_DK, w.shape[-1])
    return jnp.pad(w, ((0, 0), (0, HEAD_W - GLA_DK), (0, 0))).reshape(N_HEADS * HEAD_W, w.shape[-1])


def _unpad_head_rows(w):
    return w.reshape(N_HEADS, HEAD_W, w.shape[-1])[:, :GLA_DK].reshape(N_HEADS * GLA_DK, w.shape[-1])


def _owner_rows(stack, lo, hi):
    per = stack.shape[1]
    pieces = []
    for j in range(stack.shape[0]):
        a, b = max(lo, j * per), min(hi, (j + 1) * per)
        if a < b:
            pieces.append(stack[j, a - j * per:b - j * per])
    return pieces


def _pad_w_in_rows(stack):
    rows = lambda lo, hi: jnp.concatenate(_owner_rows(stack, lo, hi), axis=0)
    return jnp.concatenate(
        _owner_rows(stack, 0, 2048) + [_pad_head_rows(rows(2048, 2304)), _pad_head_rows(rows(2304, 2560))]
        + _owner_rows(stack, 2560, 3584) + [jnp.pad(rows(3584, 3600), ((0, HEAD_W - GATE_RANK), (0, 0)))], axis=0)


def _unpad_w_in_stack(g, per):
    segments = [(0, g[:2048]), (2048, _unpad_head_rows(g[OFF_GQ:OFF_GQ + 512])),
                (2304, _unpad_head_rows(g[OFF_GK:OFF_GK + 512])), (2560, g[OFF_GV:OFF_LR]),
                (3584, g[OFF_LR:OFF_LR + GATE_RANK])]
    blocks = []
    for j in range(N_CHIP):
        lo, hi = j * per, (j + 1) * per
        pieces = []
        for start, rows in segments:
            a, b = max(lo, start), min(hi, start + rows.shape[0])
            if a < b:
                pieces.append(rows[a - start:b - start])
        blocks.append(jnp.concatenate(pieces, axis=0))
    return jnp.stack(blocks)


def _col_major(w):
    return jnp.transpose(w, (2, 0, 1)).reshape(w.shape[2], w.shape[1])


def _rows128(a):
    return a.reshape(-1, 128)


def _rows8(a):
    a = a.reshape(-1, 128)
    return jnp.pad(a, ((0, -a.shape[0] % 8), (0, 0)))


def kernel(x, c, w_ada, b_ada, w_in, ret_norm_w, gla_gate_w, gla_gate_b, gla_norm_w, w_out, ln1_w, ln1_b, w_ff1, w_ff2, ln2_w, ln2_b, loss_target, m_w_ada, m_b_ada, m_w_in, m_ret_norm_w, m_gla_gate_w, m_gla_gate_b, m_gla_norm_w, m_w_out, m_ln1_w, m_ln1_b, m_w_ff1, m_w_ff2, m_ln2_w, m_ln2_b, v_w_ada, v_b_ada, v_w_in, v_ret_norm_w, v_gla_gate_w, v_gla_gate_b, v_gla_norm_w, v_w_out, v_ln1_w, v_ln1_b, v_w_ff1, v_w_ff2, v_ln2_w, v_ln2_b):
    seq = x.shape[1]
    tm = min(seq, TOKEN_TILE)
    tm_in = min(seq, INPROJ_TOKEN_TILE)
    xi, yi, ci = _mesh_pos()
    dev = 4 * xi + 2 * yi + ci
    chip = 2 * xi + yi
    x2, target = x[0], loss_target[0]
    ada_cols = w_ada.shape[2]
    in_cols = w_in.shape[2]
    gate_cols = gla_gate_w.shape[2]

    b_blk = lax.dynamic_slice(b_ada, (0, chip * ada_cols), (1, ada_cols))
    g0, g1, w_in_stack = _prologue(jnp.concatenate([_rows128(c), _rows128(gla_gate_w[0])], axis=0), w_ada[0], b_blk,
                                   _col_major(w_in).astype(WIRE_DTYPE))
    c_all = g0[:, :8].reshape(N_DEV, D_MODEL)
    gate_w_full = jnp.concatenate([g0[2 * j, 8:16].reshape(GATE_RANK, gate_cols) for j in range(N_CHIP)], axis=1)
    wg_p = jnp.pad(_pad_heads(gate_w_full), ((0, HEAD_W - GATE_RANK), (0, 0)))
    bg_p = _pad_heads(gla_gate_b)
    mine = lax.dynamic_index_in_dim(g1, dev, axis=2, keepdims=False)
    mod = jnp.concatenate([mine[2 * j].reshape(1, ada_cols) for j in range(N_CHIP)], axis=1)
    shift1, scale1, gate1, shift2, scale2, gate2 = [mod[:, i * D_MODEL:(i + 1) * D_MODEL] for i in range(6)]
    w_in_pt = _pad_w_in_rows(w_in_stack).astype(MXU_DTYPE)
    w_in_p = jnp.transpose(w_in_pt)

    zeros_row = jnp.zeros((1, D_MODEL), jnp.float32)
    vecs1 = jnp.concatenate([shift1, scale1] + [zeros_row] * 6, axis=0)
    proj, u, w2_stack = _inproj_fwd(x2, vecs1, w_in_p, tm_in, [w_ff2[0].astype(WIRE_DTYPE)])
    rot_a, rot_b = _rotary_tables(seq)
    dm_t, qdec_t, kdec_t, chunk_decay = _decay_tables()
    tables = (rot_a, rot_b, dm_t, qdec_t, kdec_t, chunk_decay)
    mixed, rsave, ssave, w_out_stack, w1_stack = _mixer_fwd(
        proj, tables, wg_p, bg_p, ret_norm_w, gla_norm_w,
        [w_out[0].astype(WIRE_DTYPE), w_ff1[0].astype(WIRE_DTYPE)])
    w_out_full = w_out_stack.reshape(D_MODEL, D_MODEL).astype(MXU_DTYPE)
    w1_chunks = w1_stack.astype(MXU_DTYPE)
    w2_chunks = w2_stack.astype(MXU_DTYPE)

    vecs2 = jnp.concatenate([gate1, scale2, shift2, gate2, ln1_w, ln1_b, ln2_w, ln2_b], axis=0)
    dmixed, dxa, act, dh, u2, df, dm, sums2 = _mlp_fwd_bwd(x2, mixed, target, vecs2, w_out_full, w1_chunks,
                                                           w2_chunks, tm)

    g_out_stack = _grad_matmul(mixed, dm, "grad_w_out", D_MODEL, True)
    g_ff1_stack, r_out = _grad_matmul(u2, dh, "grad_w_ff1", D_FF // N_CHIP, False, [g_out_stack])
    g_ff2_stack = _grad_matmul(act, df, "grad_w_ff2", D_MODEL, True)
    dproj, d_ret_norm, d_gla_norm, d_wg_p, d_bg_p, r_ff1, r_ff2 = _mixer_bwd(
        proj, dmixed, rsave, ssave, tables, wg_p, bg_p, ret_norm_w, gla_norm_w, [g_ff1_stack, g_ff2_stack])
    early = ["w_out", "w_ff1", "w_ff2"]
    partial = dict(zip(early, [r_out, r_ff1, r_ff2]))
    g_in_t, *swapped_early = _grad_matmul_full(dproj, u, "grad_w_in", N_PROJ // 3, [partial[n] for n in early])
    swapped = dict(zip(early, swapped_early))
    g_in_stack = _unpad_w_in_stack(g_in_t, in_cols)
    grad_x, sums1, r_in = _inproj_bwd(dproj, x2, dxa, vecs1, w_in_pt, tm_in, [g_in_stack])

    dmod = jnp.concatenate([sums1[0:1], sums1[1:2], sums2[S_GATE1:S_GATE1 + 1], sums2[S_SHIFT2:S_SHIFT2 + 1],
                            sums2[S_SCALE2:S_SCALE2 + 1], sums2[S_GATE2:S_GATE2 + 1]], axis=1)
    d_gate_w_full = _unpad_heads(d_wg_p[:GATE_RANK])
    flat = lambda parts: jnp.concatenate([_rows8(p) for p in parts], axis=0)
    small = flat([dmod, sums2[S_LN1W:S_LN1W + 1], sums2[S_LN1B:S_LN1B + 1], sums2[S_LN2W:S_LN2W + 1],
                  sums2[S_LN2B:S_LN2B + 1], d_ret_norm, _unpad_heads(d_bg_p), d_gla_norm, d_gate_w_full,
                  sums2[S_LOSS:S_LOSS + 1]])
    g2 = _gather_rows(small, "gather_small")
    tot = _sum_devices(g2)
    loss = 0.5 / D_MODEL * jnp.sum(tot[136:144])
    grad_b_ada = tot[0:48].reshape(1, 6 * D_MODEL)
    grad_ln1_w, grad_ln1_b = tot[48:56].reshape(1, D_MODEL), tot[56:64].reshape(1, D_MODEL)
    grad_ln2_w, grad_ln2_b = tot[64:72].reshape(1, D_MODEL), tot[72:80].reshape(1, D_MODEL)
    grad_ret_norm = tot[80:84].reshape(1, 512)
    grad_gate_b = tot[88:90].reshape(1, 256)
    grad_gla_norm = tot[96:100].reshape(1, 512)
    grad_gate_w = lax.dynamic_slice(tot[104:136].reshape(GATE_RANK, 256), (0, chip * gate_cols),
                                    (GATE_RANK, gate_cols))

    small_grads = [grad_b_ada, grad_ln1_w, grad_ln1_b, grad_ln2_w, grad_ln2_b, grad_ret_norm, grad_gate_b,
                   grad_gla_norm, grad_gate_w[None]]
    small_out = _adam_small(list(zip(
        [b_ada, ln1_w, ln1_b, ln2_w, ln2_b, ret_norm_w, gla_gate_b, gla_norm_w, gla_gate_w], small_grads,
        [m_b_ada, m_ln1_w, m_ln1_b, m_ln2_w, m_ln2_b, m_ret_norm_w, m_gla_gate_b, m_gla_norm_w, m_gla_gate_w],
        [v_b_ada, v_ln1_w, v_ln1_b, v_ln2_w, v_ln2_b, v_ret_norm_w, v_gla_gate_b, v_gla_norm_w, v_gla_gate_w])))
    sm_delta, sm_m, sm_v = [[o[k] for o in small_out] for k in range(3)]

    dmod_all = g2[:, 0:48].reshape(N_DEV, 6 * D_MODEL)
    dmod_blk = lax.dynamic_slice(dmod_all, (0, chip * ada_cols), (N_DEV, ada_cols))
    ada_out = _ada_bwd_adam(jnp.transpose(c_all), dmod_blk, w_ada[0], m_w_ada[0], v_w_ada[0])
    ada_g, ada_delta, ada_m, ada_v = [t[None] for t in ada_out]

    partial["w_in"] = _sum_chips(r_in, "sum_w_in")
    (swapped["w_in"],) = _sibling_swap([partial["w_in"]], "swap_w_in")
    big = {}
    for n, w, m, v in zip(["w_in", "w_out", "w_ff1", "w_ff2"], [w_in, w_out, w_ff1, w_ff2],
                          [m_w_in, m_w_out, m_w_ff1, m_w_ff2], [v_w_in, v_w_out, v_w_ff1, v_w_ff2]):
        mine, theirs = partial[n], swapped[n]
        if n == "w_in":
            out = _adam_pair(_col_major(w), mine, theirs, _col_major(m), _col_major(v), "adam_" + n)
            big[n] = [jnp.transpose(t.reshape(t.shape[0], 1, t.shape[1]), (1, 2, 0)) for t in out]
        else:
            big[n] = [t[None] for t in _adam_pair(w[0], mine, theirs, m[0], v[0], "adam_" + n)]

    def assemble(ada, smalls, k):
        b_ada_o, ln1w_o, ln1b_o, ln2w_o, ln2b_o, ret_o, gb_o, gln_o, gw_o = smalls
        return [ada, b_ada_o, big["w_in"][k], ret_o, gw_o, gb_o, gln_o, big["w_out"][k], ln1w_o, ln1b_o,
                big["w_ff1"][k], big["w_ff2"][k], ln2w_o, ln2b_o]

    grads = assemble(ada_g, small_grads, 0)
    deltas = assemble(ada_delta, sm_delta, 1)
    new_m = assemble(ada_m, sm_m, 2)
    new_v = assemble(ada_v, sm_v, 3)
    return (loss, grad_x[None], *grads, *deltas, *new_m, *new_v)
```

```python
import functools

import numpy as np
import jax
import jax.numpy as jnp
from jax import lax
from jax.experimental import pallas as pl
from jax.experimental.pallas import tpu as pltpu

D_MODEL = 1024
D_FF = 4096
CHUNK = 64
N_HEADS = 4
HEAD_W = 128
GLA_DK = 64
GATE_RANK = 16
GATE_TAU = 16.0
LN_EPS = 1e-5
ALPHA = 2.0 ** 0.25
ROPE_BASE = 10000.0
RET_SCALE = float(HEAD_W) ** -0.5
GLA_SCALE = float(GLA_DK) ** -0.5

ADAM_LR = 0.001
ADAM_B1 = 0.9
ADAM_B2 = 0.999
ADAM_EPS = 1e-08
ADAM_WD = 0.01
ADAM_STEP = 10

OFF_RQ, OFF_RK, OFF_RV, OFF_RG = 0, 512, 1024, 1536
OFF_GQ, OFF_GK, OFF_GV, OFF_GG, OFF_LR = 2048, 2560, 3072, 3584, 4096
N_PROJ = 4224

N_DEV = 8
N_CHIP = 4
MESH = pl.DeviceIdType.MESH
MXU_DTYPE = jnp.bfloat16
WIRE_DTYPE = jnp.bfloat16
VMEM_LIMIT = 60 * 1024 * 1024
TOKEN_TILE = 256
INPROJ_TOKEN_TILE = 512
CHUNKS_PER_STEP = 8
CHUNKS_IN_LOCKSTEP = 4
GRAD_TOKEN_TILE = 2048
ELEMENTWISE_COLS = 256
HIGHEST = lax.Precision.HIGHEST


def _mm(a, b):
    return jnp.dot(a.astype(MXU_DTYPE), b.astype(MXU_DTYPE), preferred_element_type=jnp.float32)


def _mm_nt(a, b):
    return lax.dot_general(a.astype(MXU_DTYPE), b.astype(MXU_DTYPE), (((1,), (1,)), ((), ())),
                           preferred_element_type=jnp.float32)


def _mm_tn(a, b):
    return lax.dot_general(a.astype(MXU_DTYPE), b.astype(MXU_DTYPE), (((0,), (0,)), ((), ())),
                           preferred_element_type=jnp.float32)


def _mm32(a, b):
    return jnp.dot(a, b, precision=HIGHEST, preferred_element_type=jnp.float32)


def _running_sum(mask, a):
    m = mask.astype(jnp.bfloat16)
    hi = a.astype(jnp.bfloat16)
    rest = a - hi.astype(jnp.float32)
    mid = rest.astype(jnp.bfloat16)
    lo = (rest - mid.astype(jnp.float32)).astype(jnp.bfloat16)
    dot = lambda t: jnp.dot(m, t, preferred_element_type=jnp.float32)
    return dot(hi) + dot(mid) + dot(lo)


def _rowmean(a):
    return jnp.mean(a, axis=-1, keepdims=True)


def _colsum(a):
    return jnp.sum(a, axis=0, keepdims=True)


def _ln(z):
    zc = z - _rowmean(z)
    rstd = lax.rsqrt(_rowmean(zc * zc) + LN_EPS)
    return zc * rstd, rstd


def _ln_bwd(dzh, zh, rstd):
    return rstd * (dzh - _rowmean(dzh) - zh * _rowmean(dzh * zh))


def _sigmoid(a):
    return 1.0 / (1.0 + jnp.exp(-a))


def _log_sigmoid(a):
    return jnp.minimum(a, 0.0) - jnp.log(1.0 + jnp.exp(-jnp.abs(a)))


def _swap_halves(a):
    return pltpu.roll(a, HEAD_W // 2, 1)


def _tri_masks():
    row = lax.broadcasted_iota(jnp.int32, (CHUNK, CHUNK), 0)
    col = lax.broadcasted_iota(jnp.int32, (CHUNK, CHUNK), 1)
    return row, col


def _const_spec(shape):
    zeros = (0,) * len(shape)
    return pl.BlockSpec(shape, lambda *_: zeros, pipeline_mode=pl.Buffered(1))


def _params(semantics):
    return pltpu.CompilerParams(dimension_semantics=semantics, vmem_limit_bytes=VMEM_LIMIT)


def _decay_tables():
    log_gamma = np.log(1.0 - 2.0 ** (-5.0 - np.arange(N_HEADS, dtype=np.float64)))
    idx = np.arange(CHUNK, dtype=np.float64)
    dist = np.abs(idx[:, None] - idx[None, :])
    intra = np.exp(log_gamma[:, None, None] * dist)
    kdec = np.exp(log_gamma[None, :] * (CHUNK - 1.0 - idx)[:, None])
    qdec = np.exp(log_gamma[None, :] * (idx + 1.0)[:, None])
    chunk_decay = np.exp(log_gamma * CHUNK)
    lanes = lambda t: np.repeat(t, HEAD_W, axis=1).astype(np.float32)
    return (jnp.asarray(intra.astype(np.float32)), jnp.asarray(lanes(qdec)), jnp.asarray(lanes(kdec)),
            [float(np.float32(v)) for v in chunk_decay])


def _rotary_tables(seq):
    half = HEAD_W // 2
    inv = 1.0 / (ROPE_BASE ** jnp.linspace(0.0, 1.0, half, dtype=jnp.float32))
    both = lambda t: jnp.concatenate([t, t], axis=-1)
    ang_a = jnp.arange(0, seq, CHUNK, dtype=jnp.float32)[:, None] * inv[None, :]
    rot_a = jnp.stack([both(jnp.cos(ang_a)), both(jnp.sin(ang_a))], axis=1)
    rot_a = jnp.pad(rot_a, ((0, 0), (0, 6), (0, 0)))
    ang_b = jnp.arange(CHUNK, dtype=jnp.float32)[:, None] * inv[None, :]
    cos_b, sin_b = both(jnp.cos(ang_b)), both(jnp.sin(ang_b))
    sign = jnp.concatenate([-jnp.ones((half,), jnp.float32), jnp.ones((half,), jnp.float32)])
    return rot_a, jnp.stack([cos_b, sin_b, cos_b * sign, sin_b * sign])


def _rotary_chunk(ra_ref, c, rb_ref):
    cos_a, sin_a = ra_ref[c, 0:1, :], ra_ref[c, 1:2, :]
    return cos_a * rb_ref[0] - sin_a * rb_ref[1], sin_a * rb_ref[2] + cos_a * rb_ref[3]


def _mesh_pos():
    return lax.axis_index("x"), lax.axis_index("y"), lax.axis_index("c")


def _flip(v, bit):
    return 1 - v if bit else v


def _gather_rows(v, name):
    rows = v.shape[0]

    def body(v_ref, out_ref, send_sems, recv_sems):
        _all_devices_exchange(v_ref, out_ref, send_sems, recv_sems)

    return pl.pallas_call(
        body, name=name,
        out_shape=jax.ShapeDtypeStruct((N_DEV, rows, 128), jnp.float32),
        in_specs=[pl.BlockSpec(memory_space=pltpu.VMEM)],
        out_specs=pl.BlockSpec(memory_space=pltpu.VMEM),
        scratch_shapes=_all_devices_sems(),
    )(v)


def _all_devices_sems():
    return [pltpu.SemaphoreType.DMA((N_DEV - 1,)), pltpu.SemaphoreType.DMA((N_DEV - 1,))]


def _all_devices_exchange(v_ref, out_ref, send_sems, recv_sems):
    x, y, c = _mesh_pos()
    me = 4 * x + 2 * y + c
    out_ref[me] = v_ref[...]
    sends, recvs = [], []
    for k in range(1, N_DEV):
        px, py, pc = _flip(x, (k >> 2) & 1), _flip(y, (k >> 1) & 1), _flip(c, k & 1)
        peer = 4 * px + 2 * py + pc
        sends.append(pltpu.make_async_remote_copy(
            src_ref=v_ref, dst_ref=out_ref.at[me], send_sem=send_sems.at[k - 1], recv_sem=recv_sems.at[k - 1],
            device_id=(px, py, pc), device_id_type=MESH))
        recvs.append(pltpu.make_async_remote_copy(
            src_ref=v_ref, dst_ref=out_ref.at[peer], send_sem=send_sems.at[k - 1], recv_sem=recv_sems.at[k - 1],
            device_id=(px, py, pc), device_id_type=MESH))
    for cp in sends:
        cp.start()
    for cp in recvs:
        cp.wait_recv()
    for cp in sends:
        cp.wait_send()


def _prologue(cond_rows, w_ada_blk, b_blk, w_in_t):
    cols = w_ada_blk.shape[1]
    groups = cols // 128
    c_rows = D_MODEL // 128

    def body(cond_ref, w_ref, b_ref, win_ref, cond_all_ref, mod_all_ref, stack_ref, mod_sc, *sems):
        gather = _ChipGather([win_ref], [stack_ref], sems[:5])
        gather.start()
        _all_devices_exchange(cond_ref, cond_all_ref, sems[5], sems[6])
        acc = jnp.broadcast_to(b_ref[...], (N_DEV, cols))
        for r in range(c_rows):
            cv = cond_all_ref[:, r, :]
            acc = acc + _mm32(cv * _sigmoid(cv), w_ref[r * 128:(r + 1) * 128, :])
        for k in range(groups):
            mod_sc[k] = acc[:, k * 128:(k + 1) * 128]
        _all_devices_exchange(mod_sc, mod_all_ref, sems[7], sems[8])
        gather.forward()
        gather.finish()

    vmem = pl.BlockSpec(memory_space=pltpu.VMEM)
    hbm = pl.BlockSpec(memory_space=pl.ANY)
    return pl.pallas_call(
        body, name="prologue",
        out_shape=(jax.ShapeDtypeStruct((N_DEV,) + cond_rows.shape, jnp.float32),
                   jax.ShapeDtypeStruct((N_DEV, groups, N_DEV, 128), jnp.float32))
        + _exchange_out_shapes([w_in_t], True),
        in_specs=[vmem, vmem, vmem, hbm],
        out_specs=(vmem, vmem, hbm),
        scratch_shapes=[pltpu.VMEM((groups, N_DEV, 128), jnp.float32)] + _gather_sems(1)
        + _all_devices_sems() + _all_devices_sems(),
        compiler_params=pltpu.CompilerParams(vmem_limit_bytes=VMEM_LIMIT),
    )(cond_rows, w_ada_blk, b_blk, w_in_t)


def _exchange_out_shapes(arrays, gather):
    return tuple(jax.ShapeDtypeStruct((N_CHIP,) + a.shape if gather else a.shape, a.dtype) for a in arrays)


def _scatter_sems(n):
    n_sem = n * (N_CHIP - 1)
    return [pltpu.SemaphoreType.DMA((n_sem,)), pltpu.SemaphoreType.DMA((n_sem,)), pltpu.SemaphoreType.DMA((n,))]


def _gather_sems(n):
    n_sem = n * (N_CHIP - 1)
    return [pltpu.SemaphoreType.DMA((n_sem,))] * 4 + [pltpu.SemaphoreType.DMA((n,))]


def _peer_chips(x, y):
    out = []
    for k in range(1, N_CHIP):
        px, py = _flip(x, (k >> 1) & 1), _flip(y, k & 1)
        out.append((px, py, 2 * px + py))
    return out


class _ChipScatter:
    def __init__(self, ins, outs, sems):
        send_sems, recv_sems, local_sems = sems
        x, y, c = _mesh_pos()
        chip = 2 * x + y
        self.local, self.sends, self.recvs = [], [], []
        for i in range(len(ins)):
            self.local.append(pltpu.make_async_copy(ins[i].at[chip], outs[i].at[chip], local_sems.at[i]))
            for k, (px, py, peer_chip) in enumerate(_peer_chips(x, y)):
                sem = i * (N_CHIP - 1) + k
                src = ins[i].at[peer_chip]
                self.sends.append(pltpu.make_async_remote_copy(
                    src_ref=src, dst_ref=outs[i].at[chip], send_sem=send_sems.at[sem], recv_sem=recv_sems.at[sem],
                    device_id=(px, py, c), device_id_type=MESH))
                self.recvs.append(pltpu.make_async_remote_copy(
                    src_ref=src, dst_ref=outs[i].at[peer_chip], send_sem=send_sems.at[sem], recv_sem=recv_sems.at[sem],
                    device_id=(px, py, c), device_id_type=MESH))

    def start(self):
        for cp in self.local + self.sends:
            cp.start()

    def wait(self):
        for cp in self.recvs:
            cp.wait_recv()
        for cp in self.sends:
            cp.wait_send()
        for cp in self.local:
            cp.wait()


class _ChipGather:
    def __init__(self, ins, outs, sems):
        ici_send, ici_recv, d2d_send, d2d_recv, local_sems = sems
        x, y, c = _mesh_pos()
        chip = 2 * x + y
        self.local, self.ici_sends, self.ici_recvs, self.d2d_sends, self.d2d_recvs = [], [], [], [], []
        for i in range(len(ins)):
            half = ins[i].shape[-1] // 2
            assert half % 128 == 0
            lead = (slice(None),) * (len(ins[i].shape) - 1)
            mine = lead + (pl.ds(pl.multiple_of(c * half, 128), half),)
            theirs = lead + (pl.ds(pl.multiple_of((1 - c) * half, 128), half),)
            self.local.append(pltpu.make_async_copy(ins[i], outs[i].at[chip], local_sems.at[i]))
            for k, (px, py, peer_chip) in enumerate(_peer_chips(x, y)):
                sem = i * (N_CHIP - 1) + k
                self.ici_sends.append(pltpu.make_async_remote_copy(
                    src_ref=ins[i].at[mine], dst_ref=outs[i].at[chip].at[mine],
                    send_sem=ici_send.at[sem], recv_sem=ici_recv.at[sem], device_id=(px, py, c), device_id_type=MESH))
                landed = outs[i].at[peer_chip].at[mine]
                self.ici_recvs.append(pltpu.make_async_remote_copy(
                    src_ref=ins[i].at[mine], dst_ref=landed,
                    send_sem=ici_send.at[sem], recv_sem=ici_recv.at[sem], device_id=(px, py, c), device_id_type=MESH))
                self.d2d_sends.append(pltpu.make_async_remote_copy(
                    src_ref=landed, dst_ref=landed,
                    send_sem=d2d_send.at[sem], recv_sem=d2d_recv.at[sem], device_id=(x, y, 1 - c), device_id_type=MESH))
                self.d2d_recvs.append(pltpu.make_async_remote_copy(
                    src_ref=landed, dst_ref=outs[i].at[peer_chip].at[theirs],
                    send_sem=d2d_send.at[sem], recv_sem=d2d_recv.at[sem], device_id=(x, y, 1 - c), device_id_type=MESH))

    def start(self):
        for cp in self.local + self.ici_sends:
            cp.start()

    def forward(self):
        for landed, onward in zip(self.ici_recvs, self.d2d_sends):
            landed.wait_recv()
            onward.start()

    def finish(self):
        for cp in self.d2d_recvs:
            cp.wait_recv()
        for cp in self.d2d_sends + self.ici_sends:
            cp.wait_send()
        for cp in self.local:
            cp.wait()


def _sibling_swap(arrays, name):
    n = len(arrays)

    def body(*refs):
        swap = _SiblingSwap(refs[:n], refs[n:2 * n], refs[2 * n:])
        swap.start()
        swap.wait()

    return pl.pallas_call(
        body, name=name,
        out_shape=tuple(jax.ShapeDtypeStruct(a.shape, a.dtype) for a in arrays),
        in_specs=[pl.BlockSpec(memory_space=pl.ANY)] * n,
        out_specs=tuple(pl.BlockSpec(memory_space=pl.ANY) for _ in arrays),
        scratch_shapes=_swap_sems(n),
    )(*arrays)


def _swap_sems(n):
    return [pltpu.SemaphoreType.DMA((n,)), pltpu.SemaphoreType.DMA((n,))]


class _SiblingSwap:
    def __init__(self, ins, outs, sems):
        send_sems, recv_sems = sems
        x, y, c = _mesh_pos()
        self.copies = [pltpu.make_async_remote_copy(
            src_ref=ins[i], dst_ref=outs[i], send_sem=send_sems.at[i], recv_sem=recv_sems.at[i],
            device_id=(x, y, 1 - c), device_id_type=MESH) for i in range(len(ins))]

    def start(self):
        for cp in self.copies:
            cp.start()

    def wait(self):
        for cp in self.copies:
            cp.wait_recv()
        for cp in self.copies:
            cp.wait_send()


def _adam(w, g, m, v):
    m2 = ADAM_B1 * m + (1.0 - ADAM_B1) * g
    v2 = ADAM_B2 * v + (1.0 - ADAM_B2) * (g * g)
    m_hat = m2 / (1.0 - ADAM_B1 ** ADAM_STEP)
    v_hat = v2 / (1.0 - ADAM_B2 ** ADAM_STEP)
    delta = -ADAM_LR * (m_hat / (jnp.sqrt(v_hat) + ADAM_EPS) + ADAM_WD * w)
    return delta, m2, v2


def _ada_bwd_adam(c_t, dmod_blk, w, m, v):
    rows, cols = w.shape
    tile = 512
    assert cols % tile == 0

    def body(c_ref, d_ref, w_ref, m_ref, v_ref, g_ref, dl_ref, m2_ref, v2_ref):
        sc = c_ref[...]
        sc = sc * _sigmoid(sc)
        dm = d_ref[...]
        g = sc[:, 0:1] * dm[0:1, :]
        for b in range(1, N_DEV):
            g = g + sc[:, b:b + 1] * dm[b:b + 1, :]
        delta, m2, v2 = _adam(w_ref[...], g, m_ref[...], v_ref[...])
        g_ref[...] = g
        dl_ref[...] = delta
        m2_ref[...] = m2
        v2_ref[...] = v2

    blk = pl.BlockSpec((rows, tile), lambda j: (0, j))
    out = jax.ShapeDtypeStruct((rows, cols), jnp.float32)
    return pl.pallas_call(
        body, name="ada_bwd_adam", grid=(cols // tile,),
        out_shape=(out, out, out, out),
        in_specs=[pl.BlockSpec((rows, N_DEV), lambda j: (0, 0)), pl.BlockSpec((N_DEV, tile), lambda j: (0, j)),
                  blk, blk, blk],
        out_specs=(blk, blk, blk, blk),
        compiler_params=_params(("arbitrary",)),
    )(c_t, dmod_blk, w, m, v)


def _inproj_fwd(x2, vecs, w_in_p, tm, riders):
    seq = x2.shape[0]
    n_tiles = seq // tm
    n_ride = len(riders)

    def body(*refs):
        x_ref, vec_ref, w_ref = refs[:3]
        ride_in, refs = refs[3:3 + n_ride], refs[3 + n_ride:]
        p_ref, u_ref = refs[:2]
        ride_out, sems = refs[2:2 + n_ride], refs[2 + n_ride:]
        gather = _ChipGather(ride_in, ride_out, sems)

        @pl.when(pl.program_id(0) == 0)
        def _():
            gather.start()

        xh, _ = _ln(x_ref[...])
        u = (xh * (1.0 + vec_ref[1:2, :]) + vec_ref[0:1, :]).astype(MXU_DTYPE)
        u_ref[...] = u
        p_ref[...] = _mm(u, w_ref[...])

        @pl.when(pl.program_id(0) == (3 * n_tiles) // 4)
        def _():
            gather.forward()

        @pl.when(pl.program_id(0) == n_tiles - 1)
        def _():
            gather.finish()

    hbm = pl.BlockSpec(memory_space=pl.ANY)
    return pl.pallas_call(
        body, name="inproj_fwd", grid=(n_tiles,),
        out_shape=(jax.ShapeDtypeStruct((seq, N_PROJ), jnp.float32), jax.ShapeDtypeStruct((seq, D_MODEL), MXU_DTYPE))
        + _exchange_out_shapes(riders, True),
        in_specs=[pl.BlockSpec((tm, D_MODEL), lambda i: (i, 0)), _const_spec(vecs.shape), _const_spec(w_in_p.shape)]
        + [hbm] * n_ride,
        out_specs=(pl.BlockSpec((tm, N_PROJ), lambda i: (i, 0)), pl.BlockSpec((tm, D_MODEL), lambda i: (i, 0)))
        + (hbm,) * n_ride,
        scratch_shapes=_gather_sems(n_ride),
        compiler_params=_params(("arbitrary",)),
    )(x2, vecs, w_in_p, *riders)


def _inproj_bwd(dproj, x2, dxa, vecs, w_in_pt, tm, riders):
    seq = x2.shape[0]
    n_tiles = seq // tm
    n_ride = len(riders)

    def body(*refs):
        dp_ref, x_ref, dxa_ref, vec_ref, w_ref = refs[:5]
        ride_in, refs = refs[5:5 + n_ride], refs[5 + n_ride:]
        gx_ref, sums_ref = refs[:2]
        ride_out, sems = refs[2:2 + n_ride], refs[2 + n_ride:]
        exchange = _ChipScatter(ride_in, ride_out, sems)

        @pl.when(pl.program_id(0) == 0)
        def _():
            exchange.start()
            sums_ref[...] = jnp.zeros_like(sums_ref)

        du = _mm(dp_ref[...], w_ref[...])
        xh, rstd = _ln(x_ref[...])
        sums_ref[0:1, :] += _colsum(du)
        sums_ref[1:2, :] += _colsum(du * xh)
        gx_ref[...] = dxa_ref[...] + _ln_bwd(du * (1.0 + vec_ref[1:2, :]), xh, rstd)

        @pl.when(pl.program_id(0) == n_tiles - 1)
        def _():
            exchange.wait()

    tile = pl.BlockSpec((tm, D_MODEL), lambda i: (i, 0))
    hbm = pl.BlockSpec(memory_space=pl.ANY)
    return pl.pallas_call(
        body, name="inproj_bwd", grid=(n_tiles,),
        out_shape=(jax.ShapeDtypeStruct((seq, D_MODEL), jnp.float32), jax.ShapeDtypeStruct((8, D_MODEL), jnp.float32))
        + _exchange_out_shapes(riders, False),
        in_specs=[pl.BlockSpec((tm, N_PROJ), lambda i: (i, 0)), tile, tile, _const_spec(vecs.shape),
                  _const_spec(w_in_pt.shape)] + [hbm] * n_ride,
        out_specs=(tile, pl.BlockSpec((8, D_MODEL), lambda i: (0, 0))) + (hbm,) * n_ride,
        scratch_shapes=_scatter_sems(n_ride),
        compiler_params=_params(("arbitrary",)),
    )(dproj, x2, dxa, vecs, w_in_pt, *riders)


def _head(h):
    return slice(h * HEAD_W, (h + 1) * HEAD_W)


def _cols(ref, off, h):
    return ref[:, off + h * HEAD_W:off + (h + 1) * HEAD_W]


HEADS = range(N_HEADS)


def _mixer_chunk_forward(p_ref, cc, ss, dm_ref, qdec_ref, kdec_ref, wg_ref, bg_ref, states):
    row, col = _tri_masks()
    lower = row >= col
    f = {}
    f["glr"] = p_ref[:, OFF_LR:OFF_LR + HEAD_W]
    f["logit"] = _mm(f["glr"], wg_ref[...]) + bg_ref[...]
    rq = [_cols(p_ref, OFF_RQ, h) for h in HEADS]
    rk = [_cols(p_ref, OFF_RK, h) for h in HEADS]
    f["rv"] = [_cols(p_ref, OFF_RV, h) for h in HEADS]
    f["qr"] = [(rq[h] * cc + _swap_halves(rq[h]) * ss) * RET_SCALE for h in HEADS]
    f["kr"] = [rk[h] * cc + _swap_halves(rk[h]) * ss for h in HEADS]
    s_raw = [_mm_nt(f["qr"][h], f["kr"][h]) for h in HEADS]
    yield
    la = _log_sigmoid(f["logit"]) * (1.0 / GATE_TAU)
    b = _running_sum(lower, la)
    f["qd"] = [f["qr"][h] * qdec_ref[:, _head(h)] for h in HEADS]
    f["kd"] = [f["kr"][h] * kdec_ref[:, _head(h)] for h in HEADS]
    f["scores"] = [s_raw[h] * dm_ref[h] for h in HEADS]
    yield
    b_last = b[CHUNK - 1:CHUNK, :]
    b_mid = b[CHUNK // 2 - 1:CHUNK // 2, :]
    f["e"], f["ei"] = jnp.exp(b - b_mid), jnp.exp(b_mid - b)
    f["eb"], f["ek"], f["ebl"] = jnp.exp(b), jnp.exp(b_last - b), jnp.exp(b_last)
    gq = [_cols(p_ref, OFF_GQ, h) * GLA_SCALE for h in HEADS]
    gk = [_cols(p_ref, OFF_GK, h) for h in HEADS]
    f["gv"] = [_cols(p_ref, OFF_GV, h) for h in HEADS]
    f["q_e"] = [gq[h] * f["e"][:, _head(h)] for h in HEADS]
    f["q_i"] = [gq[h] * f["ei"][:, _head(h)] for h in HEADS]
    f["k_e"] = [gk[h] * f["e"][:, _head(h)] for h in HEADS]
    f["k_i"] = [gk[h] * f["ei"][:, _head(h)] for h in HEADS]
    low = [_mm_nt(f["q_e"][h], f["k_i"][h]) for h in HEADS]
    up = [_mm_nt(f["q_i"][h], f["k_e"][h]) for h in HEADS]
    yield
    f["att"] = [jnp.where(lower, low[h], up[h]) for h in HEADS]
    f["qb"] = [gq[h] * f["eb"][:, _head(h)] for h in HEADS]
    f["kb"] = [gk[h] * f["ek"][:, _head(h)] for h in HEADS]
    ret_state, gla_state_t = states()
    f["o_ret"] = [_mm(f["scores"][h], f["rv"][h]) + _mm(f["qd"][h], ret_state[h]) for h in HEADS]
    f["o_gla"] = [_mm(f["att"][h], f["gv"][h]) + _mm_nt(f["qb"][h], gla_state_t[h]) for h in HEADS]
    return f


def _interleave(generators):
    live = list(generators)
    while live:
        for g in list(live):
            try:
                next(g)
            except StopIteration:
                live.remove(g)


def _mixer_fwd(proj, tables, wg_p, bg_p, ret_norm_w, gla_norm_w, riders):
    seq = proj.shape[0]
    n_chunks = seq // CHUNK
    per_step = min(n_chunks, CHUNKS_PER_STEP)
    n_steps = n_chunks // per_step
    n_ride = len(riders)
    rot_a, rot_b, dm_t, qdec_t, kdec_t, chunk_decay = tables

    def body(*refs):
        p_ref, ra_ref, rb_ref, dm_ref, qdec_ref, kdec_ref, wg_ref, bg_ref, wr_ref, wl_ref = refs[:10]
        ride_in, refs = refs[10:10 + n_ride], refs[10 + n_ride:]
        mix_ref, rsave_ref, ssave_ref = refs[:3]
        ride_out, refs = refs[3:3 + n_ride], refs[3 + n_ride:]
        r_sc, s_sc = refs[:2]
        gather = _ChipGather(ride_in, ride_out, refs[2:])

        @pl.when(pl.program_id(0) == 0)
        def _():
            gather.start()
            r_sc[...] = jnp.zeros_like(r_sc)
            s_sc[...] = jnp.zeros_like(s_sc)

        def one_chunk(c):
            p_c = p_ref.at[c * CHUNK:(c + 1) * CHUNK, :]
            mix_c = mix_ref.at[c * CHUNK:(c + 1) * CHUNK, :]
            before = {}

            def states():
                before["ret"] = [r_sc[h] for h in HEADS]
                before["gla"] = [s_sc[h] for h in HEADS]
                for h in HEADS:
                    rsave_ref[c, h] = before["ret"][h].astype(rsave_ref.dtype)
                    ssave_ref[c, h] = before["gla"][h]
                return before["ret"], before["gla"]

            cc, ss = _rotary_chunk(ra_ref, c, rb_ref)
            f = yield from _mixer_chunk_forward(p_c, cc, ss, dm_ref, qdec_ref, kdec_ref, wg_ref, bg_ref, states)
            for h in HEADS:
                r_sc[h] = chunk_decay[h] * before["ret"][h] + _mm_tn(f["kd"][h], f["rv"][h])
            for h in HEADS:
                s_sc[h] = before["gla"][h] * f["ebl"][:, _head(h)] + _mm_tn(f["gv"][h], f["kb"][h])
            yield
            for h in HEADS:
                on, _ = _ln(f["o_ret"][h])
                g = _cols(p_c, OFF_RG, h)
                mix_c[:, _head(h)] = (on * wr_ref[:, _head(h)] * (g * _sigmoid(g))).astype(mix_ref.dtype)
            for h in HEADS:
                o = f["o_gla"][h]
                on = o * lax.rsqrt(_rowmean(o * o) + LN_EPS)
                g = _cols(p_c, OFF_GG, h)
                mix_c[:, _head(N_HEADS + h)] = (on * wl_ref[:, _head(h)] * (g * _sigmoid(g))).astype(mix_ref.dtype)

        for c0 in range(0, per_step, CHUNKS_IN_LOCKSTEP):
            _interleave([one_chunk(c) for c in range(c0, min(per_step, c0 + CHUNKS_IN_LOCKSTEP))])

        @pl.when(pl.program_id(0) == (3 * n_steps) // 4)
        def _():
            gather.forward()

        @pl.when(pl.program_id(0) == n_steps - 1)
        def _():
            gather.finish()

    state_shape = (n_chunks, N_HEADS, HEAD_W, HEAD_W)
    state_blk = pl.BlockSpec((per_step, N_HEADS, HEAD_W, HEAD_W), lambda i: (i, 0, 0, 0))
    rot_blk = pl.BlockSpec((per_step, 8, HEAD_W), lambda i: (i, 0, 0))
    rows = per_step * CHUNK
    hbm = pl.BlockSpec(memory_space=pl.ANY)
    return pl.pallas_call(
        body, name="mixer_fwd", grid=(n_steps,),
        out_shape=(jax.ShapeDtypeStruct((seq, D_MODEL), MXU_DTYPE),
                   jax.ShapeDtypeStruct(state_shape, MXU_DTYPE), jax.ShapeDtypeStruct(state_shape, jnp.float32))
        + _exchange_out_shapes(riders, True),
        in_specs=[pl.BlockSpec((rows, N_PROJ), lambda i: (i, 0)), rot_blk, _const_spec(rot_b.shape),
                  _const_spec(dm_t.shape), _const_spec(qdec_t.shape), _const_spec(kdec_t.shape),
                  _const_spec(wg_p.shape), _const_spec(bg_p.shape), _const_spec(ret_norm_w.shape),
                  _const_spec(gla_norm_w.shape)] + [hbm] * n_ride,
        out_specs=(pl.BlockSpec((rows, D_MODEL), lambda i: (i, 0)), state_blk, state_blk) + (hbm,) * n_ride,
        scratch_shapes=[pltpu.VMEM((N_HEADS, HEAD_W, HEAD_W), jnp.float32),
                        pltpu.VMEM((N_HEADS, HEAD_W, HEAD_W), jnp.float32)] + _gather_sems(n_ride),
        compiler_params=_params(("arbitrary",)),
    )(proj, rot_a, rot_b, dm_t, qdec_t, kdec_t, wg_p, bg_p, ret_norm_w, gla_norm_w, *riders)


def _mixer_bwd(proj, dmixed, rsave, ssave, tables, wg_p, bg_p, ret_norm_w, gla_norm_w, riders):
    seq = proj.shape[0]
    n_chunks = seq // CHUNK
    per_step = min(n_chunks, CHUNKS_PER_STEP)
    n_steps = n_chunks // per_step
    n_ride = len(riders)
    rot_a, rot_b, dm_t, qdec_t, kdec_t, chunk_decay = tables
    last = n_steps - 1

    def body(*refs):
        p_blk, dmx_blk = refs[:2]
        shared_in = refs[2:13]
        ride_in, refs = refs[13:13 + n_ride], refs[13 + n_ride:]
        dp_blk, dwr_ref, dwl_ref, dwg_ref, dbg_ref = refs[:5]
        ride_out, refs = refs[5:5 + n_ride], refs[5 + n_ride:]
        dr_sc, ds_sc = refs[:2]
        exchange = _ChipScatter(ride_in, ride_out, refs[2:])

        @pl.when(pl.program_id(0) == 0)
        def _():
            exchange.start()
            dr_sc[...] = jnp.zeros_like(dr_sc)
            ds_sc[...] = jnp.zeros_like(ds_sc)
            dwr_ref[...] = jnp.zeros_like(dwr_ref)
            dwl_ref[...] = jnp.zeros_like(dwl_ref)
            dwg_ref[...] = jnp.zeros_like(dwg_ref)
            dbg_ref[...] = jnp.zeros_like(dbg_ref)

        def chunk_stages(c):
            rows = slice(c * CHUNK, (c + 1) * CHUNK)
            return one_chunk(c, p_blk.at[rows, :], dmx_blk.at[rows, :], dp_blk.at[rows, :], *shared_in,
                             dwr_ref, dwl_ref, dwg_ref, dbg_ref, dr_sc, ds_sc)

        for c0 in range(per_step, 0, -CHUNKS_IN_LOCKSTEP):
            _interleave([chunk_stages(c) for c in reversed(range(max(0, c0 - CHUNKS_IN_LOCKSTEP), c0))])

        @pl.when(pl.program_id(0) == last)
        def _():
            exchange.wait()

    def one_chunk(c, p_ref, dmx_ref, dp_ref, rsave_ref, ssave_ref, ra_ref, rb_ref, dm_ref, qdec_ref, kdec_ref,
                  wg_ref, bg_ref, wr_ref, wl_ref, dwr_ref, dwl_ref, dwg_ref, dbg_ref, dr_sc, ds_sc):
        def put(off, h, val):
            dp_ref[:, off + h * HEAD_W:off + (h + 1) * HEAD_W] = val.astype(dp_ref.dtype)

        cc, ss = _rotary_chunk(ra_ref, c, rb_ref)
        row, col = _tri_masks()
        ret_state = [rsave_ref[c, h] for h in HEADS]
        gla_state_t = [ssave_ref[c, h] for h in HEADS]
        f = yield from _mixer_chunk_forward(p_ref, cc, ss, dm_ref, qdec_ref, kdec_ref, wg_ref, bg_ref,
                                            lambda: (ret_state, gla_state_t))
        yield

        do_ret, do_gla = [], []
        for h in HEADS:
            on, rstd = _ln(f["o_ret"][h])
            g = _cols(p_ref, OFF_RG, h)
            sg = _sigmoid(g)
            dy = dmx_ref[:, _head(h)].astype(jnp.float32)
            wr = wr_ref[:, _head(h)]
            dwr_ref[:, _head(h)] += _colsum(dy * on * (g * sg))
            put(OFF_RG, h, dy * on * wr * (sg * (1.0 + g * (1.0 - sg))))
            do_ret.append(_ln_bwd(dy * wr * (g * sg), on, rstd))
        for h in HEADS:
            o = f["o_gla"][h]
            rstd = lax.rsqrt(_rowmean(o * o) + LN_EPS)
            on = o * rstd
            g = _cols(p_ref, OFF_GG, h)
            sg = _sigmoid(g)
            dy = dmx_ref[:, _head(N_HEADS + h)].astype(jnp.float32)
            wl = wl_ref[:, _head(h)]
            dwl_ref[:, _head(h)] += _colsum(dy * on * (g * sg))
            put(OFF_GG, h, dy * on * wl * (sg * (1.0 + g * (1.0 - sg))))
            don = dy * wl * (g * sg)
            do_gla.append(rstd * (don - on * _rowmean(don * on)))

        yield

        d_ret_new = [dr_sc[h] for h in HEADS]
        d_gla_new = [ds_sc[h] for h in HEADS]
        ds_raw = [_mm_nt(do_ret[h], f["rv"][h]) * dm_ref[h] for h in HEADS]
        d_att = [_mm_nt(do_gla[h], f["gv"][h]) for h in HEADS]
        dq_state = [_mm_nt(do_ret[h], ret_state[h]) for h in HEADS]
        dk_state = [_mm_nt(f["rv"][h], d_ret_new[h]) for h in HEADS]
        dqb = [_mm(do_gla[h], gla_state_t[h]) for h in HEADS]
        dkb = [_mm(f["gv"][h], d_gla_new[h]) for h in HEADS]
        for h in HEADS:
            put(OFF_RV, h, _mm_tn(f["scores"][h], do_ret[h]) + _mm(f["kd"][h], d_ret_new[h]))
        for h in HEADS:
            put(OFF_GV, h, _mm_tn(f["att"][h], do_gla[h]) + _mm_nt(f["kb"][h], d_gla_new[h]))
        for h in HEADS:
            dr_sc[h] = chunk_decay[h] * d_ret_new[h] + _mm_tn(f["qd"][h], do_ret[h])
        for h in HEADS:
            ds_sc[h] = d_gla_new[h] * f["ebl"][:, _head(h)] + _mm_tn(do_gla[h], f["qb"][h])
        yield

        dqr = [_mm(ds_raw[h], f["kr"][h]) + dq_state[h] * qdec_ref[:, _head(h)] for h in HEADS]
        dkr = [_mm_tn(ds_raw[h], f["qr"][h]) + dk_state[h] * kdec_ref[:, _head(h)] for h in HEADS]
        d_low = [jnp.where(row >= col, d_att[h], 0.0) for h in HEADS]
        d_up = [jnp.where(row < col, d_att[h], 0.0) for h in HEADS]
        dq_e = [_mm(d_low[h], f["k_i"][h]) for h in HEADS]
        dk_i = [_mm_tn(d_low[h], f["q_e"][h]) for h in HEADS]
        dq_i = [_mm(d_up[h], f["k_e"][h]) for h in HEADS]
        dk_e = [_mm_tn(d_up[h], f["q_i"][h]) for h in HEADS]
        yield
        for h in HEADS:
            put(OFF_RQ, h, (dqr[h] * cc + _swap_halves(dqr[h] * ss)) * RET_SCALE)
            put(OFF_RK, h, dkr[h] * cc + _swap_halves(dkr[h] * ss))
        row_id = lax.broadcasted_iota(jnp.int32, (CHUNK, HEAD_W), 0)
        db_heads = []
        for h in HEADS:
            hs = _head(h)
            e, ei, eb, ek, ebl = f["e"][:, hs], f["ei"][:, hs], f["eb"][:, hs], f["ek"][:, hs], f["ebl"][:, hs]
            put(OFF_GQ, h, (dq_e[h] * e + dq_i[h] * ei + dqb[h] * eb) * GLA_SCALE)
            put(OFF_GK, h, dk_e[h] * e + dk_i[h] * ei + dkb[h] * ek)
            db = (dq_e[h] * f["q_e"][h] - dq_i[h] * f["q_i"][h] + dk_e[h] * f["k_e"][h] - dk_i[h] * f["k_i"][h]
                  + dqb[h] * f["qb"][h] - dkb[h] * f["kb"][h])
            db_last = _colsum(dkb[h] * f["kb"][h]) + ebl * _colsum(gla_state_t[h] * d_gla_new[h])
            db_heads.append(db + jnp.where(row_id == CHUNK - 1, db_last, 0.0))
        db = jnp.concatenate(db_heads, axis=1)
        d_la = _running_sum(col >= row, db)
        d_logit = d_la * (1.0 / GATE_TAU) * (1.0 - _sigmoid(f["logit"]))
        put(OFF_LR, 0, _mm_nt(d_logit, wg_ref[...]))
        dwg_ref[...] += _mm_tn(f["glr"], d_logit)
        dbg_ref[...] += _colsum(d_logit)

    state_blk = pl.BlockSpec((per_step, N_HEADS, HEAD_W, HEAD_W), lambda i: (last - i, 0, 0, 0))
    rot_blk = pl.BlockSpec((per_step, 8, HEAD_W), lambda i: (last - i, 0, 0))
    width = N_HEADS * HEAD_W
    vec_out = pl.BlockSpec((1, width), lambda i: (0, 0))
    hbm = pl.BlockSpec(memory_space=pl.ANY)
    rows_blk = per_step * CHUNK
    return pl.pallas_call(
        body, name="mixer_bwd", grid=(n_steps,),
        out_shape=(jax.ShapeDtypeStruct((seq, N_PROJ), MXU_DTYPE),
                   jax.ShapeDtypeStruct((1, width), jnp.float32), jax.ShapeDtypeStruct((1, width), jnp.float32),
                   jax.ShapeDtypeStruct((HEAD_W, width), jnp.float32), jax.ShapeDtypeStruct((1, width), jnp.float32))
        + _exchange_out_shapes(riders, False),
        in_specs=[pl.BlockSpec((rows_blk, N_PROJ), lambda i: (last - i, 0)),
                  pl.BlockSpec((rows_blk, D_MODEL), lambda i: (last - i, 0)), state_blk, state_blk, rot_blk,
                  _const_spec(rot_b.shape),
                  _const_spec(dm_t.shape), _const_spec(qdec_t.shape), _const_spec(kdec_t.shape),
                  _const_spec(wg_p.shape), _const_spec(bg_p.shape), _const_spec(ret_norm_w.shape),
                  _const_spec(gla_norm_w.shape)] + [hbm] * n_ride,
        out_specs=(pl.BlockSpec((rows_blk, N_PROJ), lambda i: (last - i, 0)), vec_out, vec_out,
                   pl.BlockSpec((HEAD_W, width), lambda i: (0, 0)), vec_out) + (hbm,) * n_ride,
        scratch_shapes=[pltpu.VMEM((N_HEADS, HEAD_W, HEAD_W), jnp.float32),
                        pltpu.VMEM((N_HEADS, HEAD_W, HEAD_W), jnp.float32)] + _scatter_sems(n_ride),
        compiler_params=_params(("arbitrary",)),
    )(proj, dmixed, rsave, ssave, rot_a, rot_b, dm_t, qdec_t, kdec_t, wg_p, bg_p, ret_norm_w, gla_norm_w, *riders)


V_GATE1, V_SCALE2, V_SHIFT2, V_GATE2, V_LN1W, V_LN1B, V_LN2W, V_LN2B = range(8)
S_GATE1, S_SCALE2, S_SHIFT2, S_GATE2, S_LN1W, S_LN1B, S_LN2W, S_LN2B, S_LOSS = range(9)


def _mlp_fwd_bwd(x2, mixed, target, vecs, w_out, w1_chunks, w2_chunks, tm):
    seq = x2.shape[0]
    n_fc, _, fc = w1_chunks.shape

    def body(x_ref, mx_ref, t_ref, vec_ref, wo_ref, w1_ref, w2_ref,
             dmx_ref, dxa_ref, a_ref, dh_ref, u2_ref, df_ref, dm_ref, sums_ref, relu_sc):
        @pl.when(pl.program_id(0) == 0)
        def _():
            sums_ref[...] = jnp.zeros_like(sums_ref)

        vec = lambda r: vec_ref[r:r + 1, :]

        def acc(r, val):
            sums_ref[r:r + 1, :] += _colsum(val)

        xx = x_ref[...]
        m = _mm(mx_ref[...], wo_ref[...])
        z1h, rstd1 = _ln(ALPHA * xx + vec(V_GATE1) * m)
        x1 = z1h * vec(V_LN1W) + vec(V_LN1B)
        x1h, rstd0 = _ln(x1)
        u2 = (x1h * (1.0 + vec(V_SCALE2)) + vec(V_SHIFT2)).astype(MXU_DTYPE)
        u2_ref[...] = u2
        f = jnp.zeros((tm, D_MODEL), jnp.float32)
        for j in range(n_fc):
            r = jnp.maximum(_mm(u2, w1_ref[j]), 0.0)
            relu_sc[:, j * fc:(j + 1) * fc] = r
            a = (r * r).astype(MXU_DTYPE)
            a_ref[:, j * fc:(j + 1) * fc] = a
            f = f + _mm(a, w2_ref[j])
        z2h, rstd2 = _ln(ALPHA * x1 + vec(V_GATE2) * f)
        err = z2h * vec(V_LN2W) + vec(V_LN2B) - t_ref[...]
        acc(S_LOSS, err * err)
        dy = err * (1.0 / D_MODEL)
        acc(S_LN2W, dy * z2h)
        acc(S_LN2B, dy)
        dz2 = _ln_bwd(dy * vec(V_LN2W), z2h, rstd2)
        acc(S_GATE2, dz2 * f)
        df = (vec(V_GATE2) * dz2).astype(MXU_DTYPE)
        df_ref[...] = df
        du2 = jnp.zeros((tm, D_MODEL), jnp.float32)
        for j in range(n_fc):
            dh = (_mm_nt(df, w2_ref[j]) * (2.0 * relu_sc[:, j * fc:(j + 1) * fc])).astype(MXU_DTYPE)
            dh_ref[:, j * fc:(j + 1) * fc] = dh
            du2 = du2 + _mm_nt(dh, w1_ref[j])
        acc(S_SCALE2, du2 * x1h)
        acc(S_SHIFT2, du2)
        dx1 = ALPHA * dz2 + _ln_bwd(du2 * (1.0 + vec(V_SCALE2)), x1h, rstd0)
        acc(S_LN1W, dx1 * z1h)
        acc(S_LN1B, dx1)
        dz1 = _ln_bwd(dx1 * vec(V_LN1W), z1h, rstd1)
        acc(S_GATE1, dz1 * m)
        dxa_ref[...] = ALPHA * dz1
        dm = (vec(V_GATE1) * dz1).astype(MXU_DTYPE)
        dm_ref[...] = dm
        dmx_ref[...] = _mm_nt(dm, wo_ref[...])

    tile = lambda width: pl.BlockSpec((tm, width), lambda i: (i, 0))
    f32 = lambda width: jax.ShapeDtypeStruct((seq, width), jnp.float32)
    b16 = lambda width: jax.ShapeDtypeStruct((seq, width), MXU_DTYPE)
    return pl.pallas_call(
        body, name="mlp_fwd_bwd", grid=(seq // tm,),
        out_shape=(f32(D_MODEL), f32(D_MODEL), b16(D_FF), b16(D_FF), b16(D_MODEL), b16(D_MODEL), b16(D_MODEL),
                   jax.ShapeDtypeStruct((16, D_MODEL), jnp.float32)),
        in_specs=[tile(D_MODEL), tile(D_MODEL), tile(D_MODEL), _const_spec(vecs.shape), _const_spec(w_out.shape),
                  _const_spec(w1_chunks.shape), _const_spec(w2_chunks.shape)],
        out_specs=(tile(D_MODEL), tile(D_MODEL), tile(D_FF), tile(D_FF), tile(D_MODEL), tile(D_MODEL),
                   tile(D_MODEL), pl.BlockSpec((16, D_MODEL), lambda i: (0, 0))),
        scratch_shapes=[pltpu.VMEM((tm, D_FF), jnp.float32)],
        compiler_params=_params(("arbitrary",)),
    )(x2, mixed, target, vecs, w_out, w1_chunks, w2_chunks)


def _grad_matmul(a, b, name, tn, blocks_are_rows, riders=()):
    seq, m_dim = a.shape
    n_dim = b.shape[1]
    tk = min(seq, GRAD_TOKEN_TILE)
    nk = seq // tk
    n_ride = len(riders)
    if blocks_are_rows:
        tm = m_dim // N_CHIP
        assert tn == n_dim
        grid = (N_CHIP, 1, nk)
        out_map = lambda i, j, k: (i, 0, 0)
    else:
        tm = m_dim
        assert tn * N_CHIP == n_dim
        grid = (1, N_CHIP, nk)
        out_map = lambda i, j, k: (j, 0, 0)

    def body(*refs):
        a_ref, b_ref = refs[:2]
        ride_in, refs = refs[2:2 + n_ride], refs[2 + n_ride:]
        o_ref = refs[0]
        ride_out, refs = refs[1:1 + n_ride], refs[1 + n_ride:]
        acc_sc = refs[0]
        exchange = _ChipScatter(ride_in, ride_out, refs[1:]) if n_ride else None
        block = pl.program_id(0) + pl.program_id(1)
        k = pl.program_id(2)

        if exchange is not None:
            @pl.when((block == 0) & (k == 0))
            def _():
                exchange.start()

        @pl.when(k == 0)
        def _():
            acc_sc[...] = jnp.zeros_like(acc_sc)

        acc_sc[...] += _mm_tn(a_ref[...], b_ref[...])

        @pl.when(k == nk - 1)
        def _():
            o_ref[0] = acc_sc[...].astype(o_ref.dtype)

        if exchange is not None:
            @pl.when((block == N_CHIP - 1) & (k == nk - 1))
            def _():
                exchange.wait()

    hbm = pl.BlockSpec(memory_space=pl.ANY)
    out = pl.pallas_call(
        body, name=name, grid=grid,
        out_shape=(jax.ShapeDtypeStruct((N_CHIP, tm, tn), WIRE_DTYPE),) + _exchange_out_shapes(riders, False),
        in_specs=[pl.BlockSpec((tk, tm), lambda i, j, k: (k, i)), pl.BlockSpec((tk, tn), lambda i, j, k: (k, j))]
        + [hbm] * n_ride,
        out_specs=(pl.BlockSpec((1, tm, tn), out_map),) + (hbm,) * n_ride,
        scratch_shapes=[pltpu.VMEM((tm, tn), jnp.float32)] + (_scatter_sems(n_ride) if n_ride else []),
        compiler_params=_params(("arbitrary", "arbitrary", "arbitrary")),
    )(a, b, *riders)
    return out if n_ride else out[0]


def _grad_matmul_full(a, b, name, tm, riders):
    seq, m_dim = a.shape
    n_dim = b.shape[1]
    tk = min(seq, GRAD_TOKEN_TILE)
    nk = seq // tk
    n_blocks = m_dim // tm
    n_ride = len(riders)
    assert m_dim % tm == 0

    def body(*refs):
        a_ref, b_ref = refs[:2]
        ride_in, refs = refs[2:2 + n_ride], refs[2 + n_ride:]
        o_ref = refs[0]
        ride_out, refs = refs[1:1 + n_ride], refs[1 + n_ride:]
        acc_sc = refs[0]
        swap = _SiblingSwap(ride_in, ride_out, refs[1:])
        i, k = pl.program_id(0), pl.program_id(1)

        @pl.when((i == 0) & (k == 0))
        def _():
            swap.start()

        @pl.when(k == 0)
        def _():
            acc_sc[...] = jnp.zeros_like(acc_sc)

        acc_sc[...] += _mm_tn(a_ref[...], b_ref[...])

        @pl.when(k == nk - 1)
        def _():
            o_ref[...] = acc_sc[...].astype(o_ref.dtype)

        @pl.when((i == n_blocks - 1) & (k == nk - 1))
        def _():
            swap.wait()

    hbm = pl.BlockSpec(memory_space=pl.ANY)
    return pl.pallas_call(
        body, name=name, grid=(n_blocks, nk),
        out_shape=(jax.ShapeDtypeStruct((m_dim, n_dim), WIRE_DTYPE),)
        + tuple(jax.ShapeDtypeStruct(r.shape, r.dtype) for r in riders),
        in_specs=[pl.BlockSpec((tk, tm), lambda i, k: (k, i)), pl.BlockSpec((tk, n_dim), lambda i, k: (k, 0))]
        + [hbm] * n_ride,
        out_specs=(pl.BlockSpec((tm, n_dim), lambda i, k: (i, 0)),) + (hbm,) * n_ride,
        scratch_shapes=[pltpu.VMEM((tm, n_dim), jnp.float32)] + _swap_sems(n_ride),
        compiler_params=_params(("arbitrary", "arbitrary")),
    )(a, b, *riders)


def _sum_chips(stack, name):
    _, rows, cols = stack.shape
    tc = min(cols, ELEMENTWISE_COLS)

    def body(s_ref, o_ref):
        total = s_ref[0].astype(jnp.float32)
        for j in range(1, N_CHIP):
            total = total + s_ref[j].astype(jnp.float32)
        o_ref[...] = total

    return pl.pallas_call(
        body, name=name, grid=(cols // tc,),
        out_shape=jax.ShapeDtypeStruct((rows, cols), jnp.float32),
        in_specs=[pl.BlockSpec((N_CHIP, rows, tc), lambda i: (0, 0, i))],
        out_specs=pl.BlockSpec((rows, tc), lambda i: (0, i)),
        compiler_params=_params(("arbitrary",)),
    )(stack)


def _adam_pair(w, g_mine, g_sibling, m, v, name):
    rows, cols = w.shape
    tc = min(cols, ELEMENTWISE_COLS)

    def total(ref):
        if len(ref.shape) == 2:
            return ref[...]
        acc = ref[0].astype(jnp.float32)
        for j in range(1, ref.shape[0]):
            acc = acc + ref[j].astype(jnp.float32)
        return acc

    def body(w_ref, ga_ref, gb_ref, m_ref, v_ref, g_ref, dl_ref, m2_ref, v2_ref):
        g = total(ga_ref) + total(gb_ref)
        delta, m2, v2 = _adam(w_ref[...], g, m_ref[...], v_ref[...])
        g_ref[...] = g
        dl_ref[...] = delta
        m2_ref[...] = m2
        v2_ref[...] = v2

    blk = pl.BlockSpec((rows, tc), lambda i: (0, i))
    g_blk = lambda a: blk if a.ndim == 2 else pl.BlockSpec((a.shape[0], rows, tc), lambda i: (0, 0, i))
    out = jax.ShapeDtypeStruct((rows, cols), jnp.float32)
    return pl.pallas_call(
        body, name=name, grid=(cols // tc,),
        out_shape=(out, out, out, out),
        in_specs=[blk, g_blk(g_mine), g_blk(g_sibling), blk, blk], out_specs=(blk,) * 4,
        compiler_params=_params(("arbitrary",)),
    )(w, g_mine, g_sibling, m, v)


def _sum_devices(gathered):
    _, rows, _ = gathered.shape

    def body(g_ref, o_ref):
        total = g_ref[0]
        for d in range(1, N_DEV):
            total = total + g_ref[d]
        o_ref[...] = total

    return pl.pallas_call(
        body, name="sum_devices",
        out_shape=jax.ShapeDtypeStruct((rows, 128), jnp.float32),
    )(gathered)


def _adam_small(params):
    n = len(params)

    def body(*refs):
        ins, outs = refs[:4 * n], refs[4 * n:]
        for i in range(n):
            w_ref, g_ref, m_ref, v_ref = ins[4 * i:4 * i + 4]
            delta, m2, v2 = _adam(w_ref[...], g_ref[...], m_ref[...], v_ref[...])
            outs[3 * i][...] = delta
            outs[3 * i + 1][...] = m2
            outs[3 * i + 2][...] = v2

    out_shape = tuple(jax.ShapeDtypeStruct(p[0].shape, jnp.float32) for p in params for _ in range(3))
    out = pl.pallas_call(body, name="adam_small", out_shape=out_shape)(*[t for p in params for t in p])
    return [out[3 * i:3 * i + 3] for i in range(n)]


def _pad_heads(w):
    lead = w.shape[:-1]
    w = w.reshape(lead + (N_HEADS, GLA_DK))
    w = jnp.pad(w, [(0, 0)] * len(lead) + [(0, 0), (0, HEAD_W - GLA_DK)])
    return w.reshape(lead + (N_HEADS * HEAD_W,))


def _unpad_heads(w):
    lead = w.shape[:-1]
    return w.reshape(lead + (N_HEADS, HEAD_W))[..., :GLA_DK].reshape(lead + (N_HEADS * GLA_DK,))


def _pad_head_rows(w):
    w = w.reshape(N_HEADS, GLA_DK, w.shape[-1])
    return jnp.pad(w, ((0, 0), (0, HEAD_W - GLA_DK), (0, 0))).reshape(N_HEADS * HEAD_W, w.shape[-1])


def _unpad_head_rows(w):
    return w.reshape(N_HEADS, HEAD_W, w.shape[-1])[:, :GLA_DK].reshape(N_HEADS * GLA_DK, w.shape[-1])


def _pad_w_in_rows(stack):
    w = stack.reshape(-1, stack.shape[-1])
    return jnp.concatenate([
        w[:2048], _pad_head_rows(w[2048:2304]), _pad_head_rows(w[2304:2560]), w[2560:3584],
        jnp.pad(w[3584:3600], ((0, HEAD_W - GATE_RANK), (0, 0)))], axis=0)


def _unpad_w_in_stack(g, per):
    segments = [(0, g[:2048]), (2048, _unpad_head_rows(g[OFF_GQ:OFF_GQ + 512])),
                (2304, _unpad_head_rows(g[OFF_GK:OFF_GK + 512])), (2560, g[OFF_GV:OFF_LR]),
                (3584, g[OFF_LR:OFF_LR + GATE_RANK])]
    blocks = []
    for j in range(N_CHIP):
        lo, hi = j * per, (j + 1) * per
        pieces = []
        for start, rows in segments:
            a, b = max(lo, start), min(hi, start + rows.shape[0])
            if a < b:
                pieces.append(rows[a - start:b - start])
        blocks.append(jnp.concatenate(pieces, axis=0))
    return jnp.stack(blocks)


def _col_major(w):
    return jnp.transpose(w, (2, 0, 1)).reshape(w.shape[2], w.shape[1])


def _rows128(a):
    return a.reshape(-1, 128)


def _rows8(a):
    a = a.reshape(-1, 128)
    return jnp.pad(a, ((0, -a.shape[0] % 8), (0, 0)))


def kernel(x, c, w_ada, b_ada, w_in, ret_norm_w, gla_gate_w, gla_gate_b, gla_norm_w, w_out, ln1_w, ln1_b, w_ff1, w_ff2, ln2_w, ln2_b, loss_target, m_w_ada, m_b_ada, m_w_in, m_ret_norm_w, m_gla_gate_w, m_gla_gate_b, m_gla_norm_w, m_w_out, m_ln1_w, m_ln1_b, m_w_ff1, m_w_ff2, m_ln2_w, m_ln2_b, v_w_ada, v_b_ada, v_w_in, v_ret_norm_w, v_gla_gate_w, v_gla_gate_b, v_gla_norm_w, v_w_out, v_ln1_w, v_ln1_b, v_w_ff1, v_w_ff2, v_ln2_w, v_ln2_b):
    seq = x.shape[1]
    tm = min(seq, TOKEN_TILE)
    tm_in = min(seq, INPROJ_TOKEN_TILE)
    xi, yi, ci = _mesh_pos()
    dev = 4 * xi + 2 * yi + ci
    chip = 2 * xi + yi
    x2, target = x[0], loss_target[0]
    ada_cols = w_ada.shape[2]
    in_cols = w_in.shape[2]
    gate_cols = gla_gate_w.shape[2]

    b_blk = lax.dynamic_slice(b_ada, (0, chip * ada_cols), (1, ada_cols))
    g0, g1, w_in_stack = _prologue(jnp.concatenate([_rows128(c), _rows128(gla_gate_w[0])], axis=0), w_ada[0], b_blk,
                                   _col_major(w_in).astype(WIRE_DTYPE))
    c_all = g0[:, :8].reshape(N_DEV, D_MODEL)
    gate_w_full = jnp.concatenate([g0[2 * j, 8:16].reshape(GATE_RANK, gate_cols) for j in range(N_CHIP)], axis=1)
    wg_p = jnp.pad(_pad_heads(gate_w_full), ((0, HEAD_W - GATE_RANK), (0, 0)))
    bg_p = _pad_heads(gla_gate_b)
    mine = lax.dynamic_index_in_dim(g1, dev, axis=2, keepdims=False)
    mod = jnp.concatenate([mine[2 * j].reshape(1, ada_cols) for j in range(N_CHIP)], axis=1)
    shift1, scale1, gate1, shift2, scale2, gate2 = [mod[:, i * D_MODEL:(i + 1) * D_MODEL] for i in range(6)]
    w_in_pt = _pad_w_in_rows(w_in_stack).astype(MXU_DTYPE)
    w_in_p = jnp.transpose(w_in_pt)

    zeros_row = jnp.zeros((1, D_MODEL), jnp.float32)
    vecs1 = jnp.concatenate([shift1, scale1] + [zeros_row] * 6, axis=0)
    proj, u, w2_stack = _inproj_fwd(x2, vecs1, w_in_p, tm_in, [w_ff2[0].astype(WIRE_DTYPE)])
    rot_a, rot_b = _rotary_tables(seq)
    dm_t, qdec_t, kdec_t, chunk_decay = _decay_tables()
    tables = (rot_a, rot_b, dm_t, qdec_t, kdec_t, chunk_decay)
    mixed, rsave, ssave, w_out_stack, w1_stack = _mixer_fwd(
        proj, tables, wg_p, bg_p, ret_norm_w, gla_norm_w,
        [w_out[0].astype(WIRE_DTYPE), w_ff1[0].astype(WIRE_DTYPE)])
    w_out_full = w_out_stack.reshape(D_MODEL, D_MODEL).astype(MXU_DTYPE)
    w1_chunks = w1_stack.astype(MXU_DTYPE)
    w2_chunks = w2_stack.astype(MXU_DTYPE)

    vecs2 = jnp.concatenate([gate1, scale2, shift2, gate2, ln1_w, ln1_b, ln2_w, ln2_b], axis=0)
    dmixed, dxa, act, dh, u2, df, dm, sums2 = _mlp_fwd_bwd(x2, mixed, target, vecs2, w_out_full, w1_chunks,
                                                           w2_chunks, tm)

    g_out_stack = _grad_matmul(mixed, dm, "grad_w_out", D_MODEL, True)
    g_ff1_stack, r_out = _grad_matmul(u2, dh, "grad_w_ff1", D_FF // N_CHIP, False, [g_out_stack])
    g_ff2_stack = _grad_matmul(act, df, "grad_w_ff2", D_MODEL, True)
    dproj, d_ret_norm, d_gla_norm, d_wg_p, d_bg_p, r_ff1, r_ff2 = _mixer_bwd(
        proj, dmixed, rsave, ssave, tables, wg_p, bg_p, ret_norm_w, gla_norm_w, [g_ff1_stack, g_ff2_stack])
    early = ["w_out", "w_ff1", "w_ff2"]
    partial = dict(zip(early, [r_out, r_ff1, r_ff2]))
    g_in_t, *swapped_early = _grad_matmul_full(dproj, u, "grad_w_in", N_PROJ // 3, [partial[n] for n in early])
    swapped = dict(zip(early, swapped_early))
    g_in_stack = _unpad_w_in_stack(g_in_t, in_cols)
    grad_x, sums1, r_in = _inproj_bwd(dproj, x2, dxa, vecs1, w_in_pt, tm_in, [g_in_stack])

    dmod = jnp.concatenate([sums1[0:1], sums1[1:2], sums2[S_GATE1:S_GATE1 + 1], sums2[S_SHIFT2:S_SHIFT2 + 1],
                            sums2[S_SCALE2:S_SCALE2 + 1], sums2[S_GATE2:S_GATE2 + 1]], axis=1)
    d_gate_w_full = _unpad_heads(d_wg_p[:GATE_RANK])
    flat = lambda parts: jnp.concatenate([_rows8(p) for p in parts], axis=0)
    small = flat([dmod, sums2[S_LN1W:S_LN1W + 1], sums2[S_LN1B:S_LN1B + 1], sums2[S_LN2W:S_LN2W + 1],
                  sums2[S_LN2B:S_LN2B + 1], d_ret_norm, _unpad_heads(d_bg_p), d_gla_norm, d_gate_w_full,
                  sums2[S_LOSS:S_LOSS + 1]])
    g2 = _gather_rows(small, "gather_small")
    tot = _sum_devices(g2)
    loss = 0.5 / D_MODEL * jnp.sum(tot[136:144])
    grad_b_ada = tot[0:48].reshape(1, 6 * D_MODEL)
    grad_ln1_w, grad_ln1_b = tot[48:56].reshape(1, D_MODEL), tot[56:64].reshape(1, D_MODEL)
    grad_ln2_w, grad_ln2_b = tot[64:72].reshape(1, D_MODEL), tot[72:80].reshape(1, D_MODEL)
    grad_ret_norm = tot[80:84].reshape(1, 512)
    grad_gate_b = tot[88:90].reshape(1, 256)
    grad_gla_norm = tot[96:100].reshape(1, 512)
    grad_gate_w = lax.dynamic_slice(tot[104:136].reshape(GATE_RANK, 256), (0, chip * gate_cols),
                                    (GATE_RANK, gate_cols))

    small_grads = [grad_b_ada, grad_ln1_w, grad_ln1_b, grad_ln2_w, grad_ln2_b, grad_ret_norm, grad_gate_b,
                   grad_gla_norm, grad_gate_w[None]]
    small_out = _adam_small(list(zip(
        [b_ada, ln1_w, ln1_b, ln2_w, ln2_b, ret_norm_w, gla_gate_b, gla_norm_w, gla_gate_w], small_grads,
        [m_b_ada, m_ln1_w, m_ln1_b, m_ln2_w, m_ln2_b, m_ret_norm_w, m_gla_gate_b, m_gla_norm_w, m_gla_gate_w],
        [v_b_ada, v_ln1_w, v_ln1_b, v_ln2_w, v_ln2_b, v_ret_norm_w, v_gla_gate_b, v_gla_norm_w, v_gla_gate_w])))
    sm_delta, sm_m, sm_v = [[o[k] for o in small_out] for k in range(3)]

    dmod_all = g2[:, 0:48].reshape(N_DEV, 6 * D_MODEL)
    dmod_blk = lax.dynamic_slice(dmod_all, (0, chip * ada_cols), (N_DEV, ada_cols))
    ada_out = _ada_bwd_adam(jnp.transpose(c_all), dmod_blk, w_ada[0], m_w_ada[0], v_w_ada[0])
    ada_g, ada_delta, ada_m, ada_v = [t[None] for t in ada_out]

    partial["w_in"] = _sum_chips(r_in, "sum_w_in")
    (swapped["w_in"],) = _sibling_swap([partial["w_in"]], "swap_w_in")
    big = {}
    for n, w, m, v in zip(["w_in", "w_out", "w_ff1", "w_ff2"], [w_in, w_out, w_ff1, w_ff2],
                          [m_w_in, m_w_out, m_w_ff1, m_w_ff2], [v_w_in, v_w_out, v_w_ff1, v_w_ff2]):
        mine, theirs = partial[n], swapped[n]
        if n == "w_in":
            out = _adam_pair(_col_major(w), mine, theirs, _col_major(m), _col_major(v), "adam_" + n)
            big[n] = [jnp.transpose(t.reshape(t.shape[0], 1, t.shape[1]), (1, 2, 0)) for t in out]
        else:
            big[n] = [t[None] for t in _adam_pair(w[0], mine, theirs, m[0], v[0], "adam_" + n)]

    def assemble(ada, smalls, k):
        b_ada_o, ln1w_o, ln1b_o, ln2w_o, ln2b_o, ret_o, gb_o, gln_o, gw_o = smalls
        return [ada, b_ada_o, big["w_in"][k], ret_o, gw_o, gb_o, gln_o, big["w_out"][k], ln1w_o, ln1b_o,
                big["w_ff1"][k], big["w_ff2"][k], ln2w_o, ln2b_o]

    grads = assemble(ada_g, small_grads, 0)
    deltas = assemble(ada_delta, sm_delta, 1)
    new_m = assemble(ada_m, sm_m, 2)
    new_v = assemble(ada_v, sm_v, 3)
    return (loss, grad_x[None], *grads, *deltas, *new_m, *new_v)
```

```python
import numpy as np
import jax
import jax.numpy as jnp
from jax import lax
from jax.experimental import pallas as pl
from jax.experimental.pallas import tpu as pltpu

D_MODEL = 1024
D_FF = 4096
CHUNK = 64
N_HEADS = 4
HEAD_W = 128
GLA_DK = 64
GATE_RANK = 16
GATE_TAU = 16.0
LN_EPS = 1e-5
ALPHA = 2.0 ** 0.25
ROPE_BASE = 10000.0
RET_SCALE = float(HEAD_W) ** -0.5
GLA_SCALE = float(GLA_DK) ** -0.5

ADAM_LR = 0.001
ADAM_B1 = 0.9
ADAM_B2 = 0.999
ADAM_EPS = 1e-08
ADAM_WD = 0.01
ADAM_STEP = 10

OFF_RQ, OFF_RK, OFF_RV, OFF_RG = 0, 512, 1024, 1536
OFF_GQ, OFF_GK, OFF_GV, OFF_GG, OFF_LR = 2048, 2560, 3072, 3584, 4096
N_PROJ = 4224

N_DEV = 8
N_CHIP = 4
MESH = pl.DeviceIdType.MESH
MXU_DTYPE = jnp.bfloat16
WIRE_DTYPE = jnp.bfloat16
VMEM_LIMIT = 60 * 1024 * 1024
TOKEN_TILE = 256
INPROJ_TOKEN_TILE = 512
CHUNKS_PER_STEP = 8
CHUNKS_IN_LOCKSTEP = 4
GRAD_TOKEN_TILE = 2048
ELEMENTWISE_COLS = 256
HIGHEST = lax.Precision.HIGHEST


def _mm(a, b):
    return jnp.dot(a.astype(MXU_DTYPE), b.astype(MXU_DTYPE), preferred_element_type=jnp.float32)


def _mm_nt(a, b):
    return lax.dot_general(a.astype(MXU_DTYPE), b.astype(MXU_DTYPE), (((1,), (1,)), ((), ())),
                           preferred_element_type=jnp.float32)


def _mm_tn(a, b):
    return lax.dot_general(a.astype(MXU_DTYPE), b.astype(MXU_DTYPE), (((0,), (0,)), ((), ())),
                           preferred_element_type=jnp.float32)


def _mm32(a, b):
    return jnp.dot(a, b, precision=HIGHEST, preferred_element_type=jnp.float32)


def _running_sum(mask, a):
    m = mask.astype(jnp.bfloat16)
    hi = a.astype(jnp.bfloat16)
    rest = a - hi.astype(jnp.float32)
    mid = rest.astype(jnp.bfloat16)
    lo = (rest - mid.astype(jnp.float32)).astype(jnp.bfloat16)
    dot = lambda t: jnp.dot(m, t, preferred_element_type=jnp.float32)
    return dot(hi) + dot(mid) + dot(lo)


def _rowmean(a):
    return jnp.mean(a, axis=-1, keepdims=True)


def _colsum(a):
    return jnp.sum(a, axis=0, keepdims=True)


def _ln(z):
    zc = z - _rowmean(z)
    rstd = lax.rsqrt(_rowmean(zc * zc) + LN_EPS)
    return zc * rstd, rstd


def _ln_bwd(dzh, zh, rstd):
    return rstd * (dzh - _rowmean(dzh) - zh * _rowmean(dzh * zh))


def _sigmoid(a):
    return 1.0 / (1.0 + jnp.exp(-a))


def _log_sigmoid(a):
    return jnp.minimum(a, 0.0) - jnp.log(1.0 + jnp.exp(-jnp.abs(a)))


def _swap_halves(a):
    return pltpu.roll(a, HEAD_W // 2, 1)


def _tri_masks():
    row = lax.broadcasted_iota(jnp.int32, (CHUNK, CHUNK), 0)
    col = lax.broadcasted_iota(jnp.int32, (CHUNK, CHUNK), 1)
    return row, col


def _const_spec(shape):
    zeros = (0,) * len(shape)
    return pl.BlockSpec(shape, lambda *_: zeros, pipeline_mode=pl.Buffered(1))


def _params(semantics):
    return pltpu.CompilerParams(dimension_semantics=semantics, vmem_limit_bytes=VMEM_LIMIT)


def _decay_tables():
    log_gamma = np.log(1.0 - 2.0 ** (-5.0 - np.arange(N_HEADS, dtype=np.float64)))
    idx = np.arange(CHUNK, dtype=np.float64)
    dist = np.abs(idx[:, None] - idx[None, :])
    intra = np.exp(log_gamma[:, None, None] * dist)
    kdec = np.exp(log_gamma[None, :] * (CHUNK - 1.0 - idx)[:, None])
    qdec = np.exp(log_gamma[None, :] * (idx + 1.0)[:, None])
    chunk_decay = np.exp(log_gamma * CHUNK)
    lanes = lambda t: np.repeat(t, HEAD_W, axis=1).astype(np.float32)
    return (jnp.asarray(intra.astype(np.float32)), jnp.asarray(lanes(qdec)), jnp.asarray(lanes(kdec)),
            [float(np.float32(v)) for v in chunk_decay])


def _rotary_tables(seq):
    half = HEAD_W // 2
    inv = 1.0 / (ROPE_BASE ** jnp.linspace(0.0, 1.0, half, dtype=jnp.float32))
    both = lambda t: jnp.concatenate([t, t], axis=-1)
    ang_a = jnp.arange(0, seq, CHUNK, dtype=jnp.float32)[:, None] * inv[None, :]
    rot_a = jnp.stack([both(jnp.cos(ang_a)), both(jnp.sin(ang_a))], axis=1)
    rot_a = jnp.pad(rot_a, ((0, 0), (0, 6), (0, 0)))
    ang_b = jnp.arange(CHUNK, dtype=jnp.float32)[:, None] * inv[None, :]
    cos_b, sin_b = both(jnp.cos(ang_b)), both(jnp.sin(ang_b))
    sign = jnp.concatenate([-jnp.ones((half,), jnp.float32), jnp.ones((half,), jnp.float32)])
    return rot_a, jnp.stack([cos_b, sin_b, cos_b * sign, sin_b * sign])


def _rotary_chunk(ra_ref, c, rb_ref):
    cos_a, sin_a = ra_ref[c, 0:1, :], ra_ref[c, 1:2, :]
    return cos_a * rb_ref[0] - sin_a * rb_ref[1], sin_a * rb_ref[2] + cos_a * rb_ref[3]


def _mesh_pos():
    return lax.axis_index("x"), lax.axis_index("y"), lax.axis_index("c")


def _flip(v, bit):
    return 1 - v if bit else v


def _gather_rows(v, name, swaps):
    rows = v.shape[0]
    n = len(swaps)

    def body(*refs):
        v_ref, out_ref = refs[0], refs[1 + n]
        swap = _SiblingSwap(refs[1:1 + n], refs[2 + n:2 + 2 * n], refs[4 + 2 * n:])
        swap.start()
        _all_devices_exchange(v_ref, out_ref, refs[2 + 2 * n], refs[3 + 2 * n])
        swap.wait()

    hbm = pl.BlockSpec(memory_space=pl.ANY)
    vmem = pl.BlockSpec(memory_space=pltpu.VMEM)
    return pl.pallas_call(
        body, name=name,
        out_shape=(jax.ShapeDtypeStruct((N_DEV, rows, 128), jnp.float32),)
        + tuple(jax.ShapeDtypeStruct(a.shape, a.dtype) for a in swaps),
        in_specs=[vmem] + [hbm] * n,
        out_specs=(vmem,) + (hbm,) * n,
        scratch_shapes=_all_devices_sems() + _swap_sems(n),
    )(v, *swaps)


def _all_devices_sems():
    return [pltpu.SemaphoreType.DMA((N_DEV - 1,)), pltpu.SemaphoreType.DMA((N_DEV - 1,))]


def _all_devices_exchange(v_ref, out_ref, send_sems, recv_sems):
    x, y, c = _mesh_pos()
    me = 4 * x + 2 * y + c
    out_ref[me] = v_ref[...]
    sends, recvs = [], []
    for k in range(1, N_DEV):
        px, py, pc = _flip(x, (k >> 2) & 1), _flip(y, (k >> 1) & 1), _flip(c, k & 1)
        peer = 4 * px + 2 * py + pc
        sends.append(pltpu.make_async_remote_copy(
            src_ref=v_ref, dst_ref=out_ref.at[me], send_sem=send_sems.at[k - 1], recv_sem=recv_sems.at[k - 1],
            device_id=(px, py, pc), device_id_type=MESH))
        recvs.append(pltpu.make_async_remote_copy(
            src_ref=v_ref, dst_ref=out_ref.at[peer], send_sem=send_sems.at[k - 1], recv_sem=recv_sems.at[k - 1],
            device_id=(px, py, pc), device_id_type=MESH))
    for cp in sends:
        cp.start()
    for cp in recvs:
        cp.wait_recv()
    for cp in sends:
        cp.wait_send()


def _prologue(cond_rows, w_ada_blk, b_blk, w_in_t):
    cols = w_ada_blk.shape[1]
    groups = cols // 128
    c_rows = D_MODEL // 128

    def body(cond_ref, w_ref, b_ref, win_ref, cond_all_ref, mod_all_ref, stack_ref, mod_sc, *sems):
        gather = _ChipGather([win_ref], [stack_ref], sems[:5])
        gather.start()
        _all_devices_exchange(cond_ref, cond_all_ref, sems[5], sems[6])
        acc = jnp.broadcast_to(b_ref[...], (N_DEV, cols))
        for r in range(c_rows):
            cv = cond_all_ref[:, r, :]
            acc = acc + _mm32(cv * _sigmoid(cv), w_ref[r * 128:(r + 1) * 128, :])
        for k in range(groups):
            mod_sc[k] = acc[:, k * 128:(k + 1) * 128]
        _all_devices_exchange(mod_sc, mod_all_ref, sems[7], sems[8])
        gather.forward()
        gather.finish()

    vmem = pl.BlockSpec(memory_space=pltpu.VMEM)
    hbm = pl.BlockSpec(memory_space=pl.ANY)
    return pl.pallas_call(
        body, name="prologue",
        out_shape=(jax.ShapeDtypeStruct((N_DEV,) + cond_rows.shape, jnp.float32),
                   jax.ShapeDtypeStruct((N_DEV, groups, N_DEV, 128), jnp.float32))
        + _exchange_out_shapes([w_in_t], True),
        in_specs=[vmem, vmem, vmem, hbm],
        out_specs=(vmem, vmem, hbm),
        scratch_shapes=[pltpu.VMEM((groups, N_DEV, 128), jnp.float32)] + _gather_sems(1)
        + _all_devices_sems() + _all_devices_sems(),
        compiler_params=pltpu.CompilerParams(vmem_limit_bytes=VMEM_LIMIT),
    )(cond_rows, w_ada_blk, b_blk, w_in_t)


def _exchange_out_shapes(arrays, gather):
    return tuple(jax.ShapeDtypeStruct((N_CHIP,) + a.shape if gather else a.shape, a.dtype) for a in arrays)


def _scatter_sems(n):
    n_sem = n * (N_CHIP - 1)
    return [pltpu.SemaphoreType.DMA((n_sem,)), pltpu.SemaphoreType.DMA((n_sem,)), pltpu.SemaphoreType.DMA((n,))]


def _gather_sems(n):
    n_sem = n * (N_CHIP - 1)
    return [pltpu.SemaphoreType.DMA((n_sem,))] * 4 + [pltpu.SemaphoreType.DMA((n,))]


def _peer_chips(x, y):
    out = []
    for k in range(1, N_CHIP):
        px, py = _flip(x, (k >> 1) & 1), _flip(y, k & 1)
        out.append((px, py, 2 * px + py))
    return out


class _ChipScatter:
    def __init__(self, ins, outs, sems):
        send_sems, recv_sems, local_sems = sems
        x, y, c = _mesh_pos()
        chip = 2 * x + y
        self.local, self.sends, self.recvs = [], [], []
        for i in range(len(ins)):
            self.local.append(pltpu.make_async_copy(ins[i].at[chip], outs[i].at[chip], local_sems.at[i]))
            for k, (px, py, peer_chip) in enumerate(_peer_chips(x, y)):
                sem = i * (N_CHIP - 1) + k
                src = ins[i].at[peer_chip]
                self.sends.append(pltpu.make_async_remote_copy(
                    src_ref=src, dst_ref=outs[i].at[chip], send_sem=send_sems.at[sem], recv_sem=recv_sems.at[sem],
                    device_id=(px, py, c), device_id_type=MESH))
                self.recvs.append(pltpu.make_async_remote_copy(
                    src_ref=src, dst_ref=outs[i].at[peer_chip], send_sem=send_sems.at[sem], recv_sem=recv_sems.at[sem],
                    device_id=(px, py, c), device_id_type=MESH))

    def start(self):
        for cp in self.local + self.sends:
            cp.start()

    def wait(self):
        for cp in self.recvs:
            cp.wait_recv()
        for cp in self.sends:
            cp.wait_send()
        for cp in self.local:
            cp.wait()


class _ChipGather:
    def __init__(self, ins, outs, sems):
        ici_send, ici_recv, d2d_send, d2d_recv, local_sems = sems
        x, y, c = _mesh_pos()
        chip = 2 * x + y
        self.local, self.ici_sends, self.ici_recvs, self.d2d_sends, self.d2d_recvs = [], [], [], [], []
        for i in range(len(ins)):
            half = ins[i].shape[-1] // 2
            assert half % 128 == 0
            lead = (slice(None),) * (len(ins[i].shape) - 1)
            mine = lead + (pl.ds(pl.multiple_of(c * half, 128), half),)
            theirs = lead + (pl.ds(pl.multiple_of((1 - c) * half, 128), half),)
            self.local.append(pltpu.make_async_copy(ins[i], outs[i].at[chip], local_sems.at[i]))
            for k, (px, py, peer_chip) in enumerate(_peer_chips(x, y)):
                sem = i * (N_CHIP - 1) + k
                self.ici_sends.append(pltpu.make_async_remote_copy(
                    src_ref=ins[i].at[mine], dst_ref=outs[i].at[chip].at[mine],
                    send_sem=ici_send.at[sem], recv_sem=ici_recv.at[sem], device_id=(px, py, c), device_id_type=MESH))
                landed = outs[i].at[peer_chip].at[mine]
                self.ici_recvs.append(pltpu.make_async_remote_copy(
                    src_ref=ins[i].at[mine], dst_ref=landed,
                    send_sem=ici_send.at[sem], recv_sem=ici_recv.at[sem], device_id=(px, py, c), device_id_type=MESH))
                self.d2d_sends.append(pltpu.make_async_remote_copy(
                    src_ref=landed, dst_ref=landed,
                    send_sem=d2d_send.at[sem], recv_sem=d2d_recv.at[sem], device_id=(x, y, 1 - c), device_id_type=MESH))
                self.d2d_recvs.append(pltpu.make_async_remote_copy(
                    src_ref=landed, dst_ref=outs[i].at[peer_chip].at[theirs],
                    send_sem=d2d_send.at[sem], recv_sem=d2d_recv.at[sem], device_id=(x, y, 1 - c), device_id_type=MESH))

    def start(self):
        for cp in self.local + self.ici_sends:
            cp.start()

    def forward(self):
        for landed, onward in zip(self.ici_recvs, self.d2d_sends):
            landed.wait_recv()
            onward.start()

    def finish(self):
        for cp in self.d2d_recvs:
            cp.wait_recv()
        for cp in self.d2d_sends + self.ici_sends:
            cp.wait_send()
        for cp in self.local:
            cp.wait()


def _swap_sems(n):
    return [pltpu.SemaphoreType.DMA((n,)), pltpu.SemaphoreType.DMA((n,))]


class _SiblingSwap:
    def __init__(self, ins, outs, sems):
        send_sems, recv_sems = sems
        x, y, c = _mesh_pos()
        self.copies = [pltpu.make_async_remote_copy(
            src_ref=ins[i], dst_ref=outs[i], send_sem=send_sems.at[i], recv_sem=recv_sems.at[i],
            device_id=(x, y, 1 - c), device_id_type=MESH) for i in range(len(ins))]

    def start(self):
        for cp in self.copies:
            cp.start()

    def wait(self):
        for cp in self.copies:
            cp.wait_recv()
        for cp in self.copies:
            cp.wait_send()


def _adam(w, g, m, v):
    m2 = ADAM_B1 * m + (1.0 - ADAM_B1) * g
    v2 = ADAM_B2 * v + (1.0 - ADAM_B2) * (g * g)
    m_hat = m2 / (1.0 - ADAM_B1 ** ADAM_STEP)
    v_hat = v2 / (1.0 - ADAM_B2 ** ADAM_STEP)
    delta = -ADAM_LR * (m_hat / (jnp.sqrt(v_hat) + ADAM_EPS) + ADAM_WD * w)
    return delta, m2, v2


def _ada_bwd_adam(c_t, dmod_blk, w, m, v):
    rows, cols = w.shape
    tile = 512
    assert cols % tile == 0

    def body(c_ref, d_ref, w_ref, m_ref, v_ref, g_ref, dl_ref, m2_ref, v2_ref):
        sc = c_ref[...]
        sc = sc * _sigmoid(sc)
        dm = d_ref[...]
        g = sc[:, 0:1] * dm[0:1, :]
        for b in range(1, N_DEV):
            g = g + sc[:, b:b + 1] * dm[b:b + 1, :]
        delta, m2, v2 = _adam(w_ref[...], g, m_ref[...], v_ref[...])
        g_ref[...] = g
        dl_ref[...] = delta
        m2_ref[...] = m2
        v2_ref[...] = v2

    blk = pl.BlockSpec((rows, tile), lambda j: (0, j))
    out = jax.ShapeDtypeStruct((rows, cols), jnp.float32)
    return pl.pallas_call(
        body, name="ada_bwd_adam", grid=(cols // tile,),
        out_shape=(out, out, out, out),
        in_specs=[pl.BlockSpec((rows, N_DEV), lambda j: (0, 0)), pl.BlockSpec((N_DEV, tile), lambda j: (0, j)),
                  blk, blk, blk],
        out_specs=(blk, blk, blk, blk),
        compiler_params=_params(("arbitrary",)),
    )(c_t, dmod_blk, w, m, v)


def _inproj_fwd(x2, vecs, w_in_p, tm, riders):
    seq = x2.shape[0]
    n_tiles = seq // tm
    n_ride = len(riders)

    def body(*refs):
        x_ref, vec_ref, w_ref = refs[:3]
        ride_in, refs = refs[3:3 + n_ride], refs[3 + n_ride:]
        p_ref, u_ref = refs[:2]
        ride_out, sems = refs[2:2 + n_ride], refs[2 + n_ride:]
        gather = _ChipGather(ride_in, ride_out, sems)

        @pl.when(pl.program_id(0) == 0)
        def _():
            gather.start()

        xh, _ = _ln(x_ref[...])
        u = (xh * (1.0 + vec_ref[1:2, :]) + vec_ref[0:1, :]).astype(MXU_DTYPE)
        u_ref[...] = u
        p_ref[...] = _mm(u, w_ref[...])

        @pl.when(pl.program_id(0) == (3 * n_tiles) // 4)
        def _():
            gather.forward()

        @pl.when(pl.program_id(0) == n_tiles - 1)
        def _():
            gather.finish()

    hbm = pl.BlockSpec(memory_space=pl.ANY)
    return pl.pallas_call(
        body, name="inproj_fwd", grid=(n_tiles,),
        out_shape=(jax.ShapeDtypeStruct((seq, N_PROJ), jnp.float32), jax.ShapeDtypeStruct((seq, D_MODEL), MXU_DTYPE))
        + _exchange_out_shapes(riders, True),
        in_specs=[pl.BlockSpec((tm, D_MODEL), lambda i: (i, 0)), _const_spec(vecs.shape), _const_spec(w_in_p.shape)]
        + [hbm] * n_ride,
        out_specs=(pl.BlockSpec((tm, N_PROJ), lambda i: (i, 0)), pl.BlockSpec((tm, D_MODEL), lambda i: (i, 0)))
        + (hbm,) * n_ride,
        scratch_shapes=_gather_sems(n_ride),
        compiler_params=_params(("arbitrary",)),
    )(x2, vecs, w_in_p, *riders)


def _inproj_bwd(dproj, x2, dxa, vecs, w_in_pt, tm, riders):
    seq = x2.shape[0]
    n_tiles = seq // tm
    n_ride = len(riders)

    def body(*refs):
        dp_ref, x_ref, dxa_ref, vec_ref, w_ref = refs[:5]
        ride_in, refs = refs[5:5 + n_ride], refs[5 + n_ride:]
        gx_ref, sums_ref = refs[:2]
        ride_out, sems = refs[2:2 + n_ride], refs[2 + n_ride:]
        exchange = _ChipScatter(ride_in, ride_out, sems)

        @pl.when(pl.program_id(0) == 0)
        def _():
            exchange.start()
            sums_ref[...] = jnp.zeros_like(sums_ref)

        du = _mm(dp_ref[...], w_ref[...])
        xh, rstd = _ln(x_ref[...])
        sums_ref[0:1, :] += _colsum(du)
        sums_ref[1:2, :] += _colsum(du * xh)
        gx_ref[...] = dxa_ref[...] + _ln_bwd(du * (1.0 + vec_ref[1:2, :]), xh, rstd)

        @pl.when(pl.program_id(0) == n_tiles - 1)
        def _():
            exchange.wait()

    tile = pl.BlockSpec((tm, D_MODEL), lambda i: (i, 0))
    hbm = pl.BlockSpec(memory_space=pl.ANY)
    return pl.pallas_call(
        body, name="inproj_bwd", grid=(n_tiles,),
        out_shape=(jax.ShapeDtypeStruct((seq, D_MODEL), jnp.float32), jax.ShapeDtypeStruct((8, D_MODEL), jnp.float32))
        + _exchange_out_shapes(riders, False),
        in_specs=[pl.BlockSpec((tm, N_PROJ), lambda i: (i, 0)), tile, tile, _const_spec(vecs.shape),
                  _const_spec(w_in_pt.shape)] + [hbm] * n_ride,
        out_specs=(tile, pl.BlockSpec((8, D_MODEL), lambda i: (0, 0))) + (hbm,) * n_ride,
        scratch_shapes=_scatter_sems(n_ride),
        compiler_params=_params(("arbitrary",)),
    )(dproj, x2, dxa, vecs, w_in_pt, *riders)


def _head(h):
    return slice(h * HEAD_W, (h + 1) * HEAD_W)


def _cols(ref, off, h):
    return ref[:, off + h * HEAD_W:off + (h + 1) * HEAD_W]


HEADS = range(N_HEADS)


def _mixer_chunk_forward(p_ref, cc, ss, dm_ref, qdec_ref, kdec_ref, wg_ref, bg_ref, states):
    row, col = _tri_masks()
    lower = row >= col
    f = {}
    f["glr"] = p_ref[:, OFF_LR:OFF_LR + HEAD_W]
    f["logit"] = _mm(f["glr"], wg_ref[...]) + bg_ref[...]
    rq = [_cols(p_ref, OFF_RQ, h) for h in HEADS]
    rk = [_cols(p_ref, OFF_RK, h) for h in HEADS]
    f["rv"] = [_cols(p_ref, OFF_RV, h) for h in HEADS]
    f["qr"] = [(rq[h] * cc + _swap_halves(rq[h]) * ss) * RET_SCALE for h in HEADS]
    f["kr"] = [rk[h] * cc + _swap_halves(rk[h]) * ss for h in HEADS]
    s_raw = [_mm_nt(f["qr"][h], f["kr"][h]) for h in HEADS]
    yield
    la = _log_sigmoid(f["logit"]) * (1.0 / GATE_TAU)
    b = _running_sum(lower, la)
    f["qd"] = [f["qr"][h] * qdec_ref[:, _head(h)] for h in HEADS]
    f["kd"] = [f["kr"][h] * kdec_ref[:, _head(h)] for h in HEADS]
    f["scores"] = [s_raw[h] * dm_ref[h] for h in HEADS]
    yield
    b_last = b[CHUNK - 1:CHUNK, :]
    b_mid = b[CHUNK // 2 - 1:CHUNK // 2, :]
    f["e"], f["ei"] = jnp.exp(b - b_mid), jnp.exp(b_mid - b)
    f["eb"], f["ek"], f["ebl"] = jnp.exp(b), jnp.exp(b_last - b), jnp.exp(b_last)
    gq = [_cols(p_ref, OFF_GQ, h) * GLA_SCALE for h in HEADS]
    gk = [_cols(p_ref, OFF_GK, h) for h in HEADS]
    f["gv"] = [_cols(p_ref, OFF_GV, h) for h in HEADS]
    f["q_e"] = [gq[h] * f["e"][:, _head(h)] for h in HEADS]
    f["q_i"] = [gq[h] * f["ei"][:, _head(h)] for h in HEADS]
    f["k_e"] = [gk[h] * f["e"][:, _head(h)] for h in HEADS]
    f["k_i"] = [gk[h] * f["ei"][:, _head(h)] for h in HEADS]
    low = [_mm_nt(f["q_e"][h], f["k_i"][h]) for h in HEADS]
    up = [_mm_nt(f["q_i"][h], f["k_e"][h]) for h in HEADS]
    yield
    f["att"] = [jnp.where(lower, low[h], up[h]) for h in HEADS]
    f["qb"] = [gq[h] * f["eb"][:, _head(h)] for h in HEADS]
    f["kb"] = [gk[h] * f["ek"][:, _head(h)] for h in HEADS]
    ret_state, gla_state_t = states()
    f["o_ret"] = [_mm(f["scores"][h], f["rv"][h]) + _mm(f["qd"][h], ret_state[h]) for h in HEADS]
    f["o_gla"] = [_mm(f["att"][h], f["gv"][h]) + _mm_nt(f["qb"][h], gla_state_t[h]) for h in HEADS]
    return f


def _interleave(generators):
    live = list(generators)
    while live:
        for g in list(live):
            try:
                next(g)
            except StopIteration:
                live.remove(g)


def _mixer_fwd(proj, tables, wg_p, bg_p, ret_norm_w, gla_norm_w, riders):
    seq = proj.shape[0]
    n_chunks = seq // CHUNK
    per_step = min(n_chunks, CHUNKS_PER_STEP)
    n_steps = n_chunks // per_step
    n_ride = len(riders)
    rot_a, rot_b, dm_t, qdec_t, kdec_t, chunk_decay = tables

    def body(*refs):
        p_ref, ra_ref, rb_ref, dm_ref, qdec_ref, kdec_ref, wg_ref, bg_ref, wr_ref, wl_ref = refs[:10]
        ride_in, refs = refs[10:10 + n_ride], refs[10 + n_ride:]
        mix_ref, rsave_ref, ssave_ref = refs[:3]
        ride_out, refs = refs[3:3 + n_ride], refs[3 + n_ride:]
        r_sc, s_sc = refs[:2]
        gather = _ChipGather(ride_in, ride_out, refs[2:])

        @pl.when(pl.program_id(0) == 0)
        def _():
            gather.start()
            r_sc[...] = jnp.zeros_like(r_sc)
            s_sc[...] = jnp.zeros_like(s_sc)

        def one_chunk(c):
            p_c = p_ref.at[c * CHUNK:(c + 1) * CHUNK, :]
            mix_c = mix_ref.at[c * CHUNK:(c + 1) * CHUNK, :]
            before = {}

            def states():
                before["ret"] = [r_sc[h] for h in HEADS]
                before["gla"] = [s_sc[h] for h in HEADS]
                for h in HEADS:
                    rsave_ref[c, h] = before["ret"][h].astype(rsave_ref.dtype)
                    ssave_ref[c, h] = before["gla"][h]
                return before["ret"], before["gla"]

            cc, ss = _rotary_chunk(ra_ref, c, rb_ref)
            f = yield from _mixer_chunk_forward(p_c, cc, ss, dm_ref, qdec_ref, kdec_ref, wg_ref, bg_ref, states)
            for h in HEADS:
                r_sc[h] = chunk_decay[h] * before["ret"][h] + _mm_tn(f["kd"][h], f["rv"][h])
            for h in HEADS:
                s_sc[h] = before["gla"][h] * f["ebl"][:, _head(h)] + _mm_tn(f["gv"][h], f["kb"][h])
            yield
            for h in HEADS:
                on, _ = _ln(f["o_ret"][h])
                g = _cols(p_c, OFF_RG, h)
                mix_c[:, _head(h)] = (on * wr_ref[:, _head(h)] * (g * _sigmoid(g))).astype(mix_ref.dtype)
            for h in HEADS:
                o = f["o_gla"][h]
                on = o * lax.rsqrt(_rowmean(o * o) + LN_EPS)
                g = _cols(p_c, OFF_GG, h)
                mix_c[:, _head(N_HEADS + h)] = (on * wl_ref[:, _head(h)] * (g * _sigmoid(g))).astype(mix_ref.dtype)

        for c0 in range(0, per_step, CHUNKS_IN_LOCKSTEP):
            _interleave([one_chunk(c) for c in range(c0, min(per_step, c0 + CHUNKS_IN_LOCKSTEP))])

        @pl.when(pl.program_id(0) == (3 * n_steps) // 4)
        def _():
            gather.forward()

        @pl.when(pl.program_id(0) == n_steps - 1)
        def _():
            gather.finish()

    state_shape = (n_chunks, N_HEADS, HEAD_W, HEAD_W)
    state_blk = pl.BlockSpec((per_step, N_HEADS, HEAD_W, HEAD_W), lambda i: (i, 0, 0, 0))
    rot_blk = pl.BlockSpec((per_step, 8, HEAD_W), lambda i: (i, 0, 0))
    rows = per_step * CHUNK
    hbm = pl.BlockSpec(memory_space=pl.ANY)
    return pl.pallas_call(
        body, name="mixer_fwd", grid=(n_steps,),
        out_shape=(jax.ShapeDtypeStruct((seq, D_MODEL), MXU_DTYPE),
                   jax.ShapeDtypeStruct(state_shape, MXU_DTYPE), jax.ShapeDtypeStruct(state_shape, jnp.float32))
        + _exchange_out_shapes(riders, True),
        in_specs=[pl.BlockSpec((rows, N_PROJ), lambda i: (i, 0)), rot_blk, _const_spec(rot_b.shape),
                  _const_spec(dm_t.shape), _const_spec(qdec_t.shape), _const_spec(kdec_t.shape),
                  _const_spec(wg_p.shape), _const_spec(bg_p.shape), _const_spec(ret_norm_w.shape),
                  _const_spec(gla_norm_w.shape)] + [hbm] * n_ride,
        out_specs=(pl.BlockSpec((rows, D_MODEL), lambda i: (i, 0)), state_blk, state_blk) + (hbm,) * n_ride,
        scratch_shapes=[pltpu.VMEM((N_HEADS, HEAD_W, HEAD_W), jnp.float32),
                        pltpu.VMEM((N_HEADS, HEAD_W, HEAD_W), jnp.float32)] + _gather_sems(n_ride),
        compiler_params=_params(("arbitrary",)),
    )(proj, rot_a, rot_b, dm_t, qdec_t, kdec_t, wg_p, bg_p, ret_norm_w, gla_norm_w, *riders)


def _mixer_bwd(proj, dmixed, rsave, ssave, tables, wg_p, bg_p, ret_norm_w, gla_norm_w, riders):
    seq = proj.shape[0]
    n_chunks = seq // CHUNK
    per_step = min(n_chunks, CHUNKS_PER_STEP)
    n_steps = n_chunks // per_step
    n_ride = len(riders)
    rot_a, rot_b, dm_t, qdec_t, kdec_t, chunk_decay = tables
    last = n_steps - 1

    def body(*refs):
        p_blk, dmx_blk = refs[:2]
        shared_in = refs[2:13]
        ride_in, refs = refs[13:13 + n_ride], refs[13 + n_ride:]
        dp_blk, dwr_ref, dwl_ref, dwg_ref, dbg_ref = refs[:5]
        ride_out, refs = refs[5:5 + n_ride], refs[5 + n_ride:]
        dr_sc, ds_sc = refs[:2]
        exchange = _ChipScatter(ride_in, ride_out, refs[2:])

        @pl.when(pl.program_id(0) == 0)
        def _():
            exchange.start()
            dr_sc[...] = jnp.zeros_like(dr_sc)
            ds_sc[...] = jnp.zeros_like(ds_sc)
            dwr_ref[...] = jnp.zeros_like(dwr_ref)
            dwl_ref[...] = jnp.zeros_like(dwl_ref)
            dwg_ref[...] = jnp.zeros_like(dwg_ref)
            dbg_ref[...] = jnp.zeros_like(dbg_ref)

        def chunk_stages(c):
            rows = slice(c * CHUNK, (c + 1) * CHUNK)
            return one_chunk(c, p_blk.at[rows, :], dmx_blk.at[rows, :], dp_blk.at[rows, :], *shared_in,
                             dwr_ref, dwl_ref, dwg_ref, dbg_ref, dr_sc, ds_sc)

        for c0 in range(per_step, 0, -CHUNKS_IN_LOCKSTEP):
            _interleave([chunk_stages(c) for c in reversed(range(max(0, c0 - CHUNKS_IN_LOCKSTEP), c0))])

        @pl.when(pl.program_id(0) == last)
        def _():
            exchange.wait()

    def one_chunk(c, p_ref, dmx_ref, dp_ref, rsave_ref, ssave_ref, ra_ref, rb_ref, dm_ref, qdec_ref, kdec_ref,
                  wg_ref, bg_ref, wr_ref, wl_ref, dwr_ref, dwl_ref, dwg_ref, dbg_ref, dr_sc, ds_sc):
        def put(off, h, val):
            dp_ref[:, off + h * HEAD_W:off + (h + 1) * HEAD_W] = val.astype(dp_ref.dtype)

        cc, ss = _rotary_chunk(ra_ref, c, rb_ref)
        row, col = _tri_masks()
        ret_state = [rsave_ref[c, h] for h in HEADS]
        gla_state_t = [ssave_ref[c, h] for h in HEADS]
        f = yield from _mixer_chunk_forward(p_ref, cc, ss, dm_ref, qdec_ref, kdec_ref, wg_ref, bg_ref,
                                            lambda: (ret_state, gla_state_t))
        yield

        do_ret, do_gla = [], []
        for h in HEADS:
            on, rstd = _ln(f["o_ret"][h])
            g = _cols(p_ref, OFF_RG, h)
            sg = _sigmoid(g)
            dy = dmx_ref[:, _head(h)].astype(jnp.float32)
            wr = wr_ref[:, _head(h)]
            dwr_ref[:, _head(h)] += _colsum(dy * on * (g * sg))
            put(OFF_RG, h, dy * on * wr * (sg * (1.0 + g * (1.0 - sg))))
            do_ret.append(_ln_bwd(dy * wr * (g * sg), on, rstd))
        for h in HEADS:
            o = f["o_gla"][h]
            rstd = lax.rsqrt(_rowmean(o * o) + LN_EPS)
            on = o * rstd
            g = _cols(p_ref, OFF_GG, h)
            sg = _sigmoid(g)
            dy = dmx_ref[:, _head(N_HEADS + h)].astype(jnp.float32)
            wl = wl_ref[:, _head(h)]
            dwl_ref[:, _head(h)] += _colsum(dy * on * (g * sg))
            put(OFF_GG, h, dy * on * wl * (sg * (1.0 + g * (1.0 - sg))))
            don = dy * wl * (g * sg)
            do_gla.append(rstd * (don - on * _rowmean(don * on)))

        yield

        d_ret_new = [dr_sc[h] for h in HEADS]
        d_gla_new = [ds_sc[h] for h in HEADS]
        ds_raw = [_mm_nt(do_ret[h], f["rv"][h]) * dm_ref[h] for h in HEADS]
        d_att = [_mm_nt(do_gla[h], f["gv"][h]) for h in HEADS]
        dq_state = [_mm_nt(do_ret[h], ret_state[h]) for h in HEADS]
        dk_state = [_mm_nt(f["rv"][h], d_ret_new[h]) for h in HEADS]
        dqb = [_mm(do_gla[h], gla_state_t[h]) for h in HEADS]
        dkb = [_mm(f["gv"][h], d_gla_new[h]) for h in HEADS]
        for h in HEADS:
            put(OFF_RV, h, _mm_tn(f["scores"][h], do_ret[h]) + _mm(f["kd"][h], d_ret_new[h]))
        for h in HEADS:
            put(OFF_GV, h, _mm_tn(f["att"][h], do_gla[h]) + _mm_nt(f["kb"][h], d_gla_new[h]))
        for h in HEADS:
            dr_sc[h] = chunk_decay[h] * d_ret_new[h] + _mm_tn(f["qd"][h], do_ret[h])
        for h in HEADS:
            ds_sc[h] = d_gla_new[h] * f["ebl"][:, _head(h)] + _mm_tn(do_gla[h], f["qb"][h])
        yield

        dqr = [_mm(ds_raw[h], f["kr"][h]) + dq_state[h] * qdec_ref[:, _head(h)] for h in HEADS]
        dkr = [_mm_tn(ds_raw[h], f["qr"][h]) + dk_state[h] * kdec_ref[:, _head(h)] for h in HEADS]
        d_low = [jnp.where(row >= col, d_att[h], 0.0) for h in HEADS]
        d_up = [jnp.where(row < col, d_att[h], 0.0) for h in HEADS]
        dq_e = [_mm(d_low[h], f["k_i"][h]) for h in HEADS]
        dk_i = [_mm_tn(d_low[h], f["q_e"][h]) for h in HEADS]
        dq_i = [_mm(d_up[h], f["k_e"][h]) for h in HEADS]
        dk_e = [_mm_tn(d_up[h], f["q_i"][h]) for h in HEADS]
        yield
        for h in HEADS:
            put(OFF_RQ, h, (dqr[h] * cc + _swap_halves(dqr[h] * ss)) * RET_SCALE)
            put(OFF_RK, h, dkr[h] * cc + _swap_halves(dkr[h] * ss))
        row_id = lax.broadcasted_iota(jnp.int32, (CHUNK, HEAD_W), 0)
        db_heads = []
        for h in HEADS:
            hs = _head(h)
            e, ei, eb, ek, ebl = f["e"][:, hs], f["ei"][:, hs], f["eb"][:, hs], f["ek"][:, hs], f["ebl"][:, hs]
            put(OFF_GQ, h, (dq_e[h] * e + dq_i[h] * ei + dqb[h] * eb) * GLA_SCALE)
            put(OFF_GK, h, dk_e[h] * e + dk_i[h] * ei + dkb[h] * ek)
            db = (dq_e[h] * f["q_e"][h] - dq_i[h] * f["q_i"][h] + dk_e[h] * f["k_e"][h] - dk_i[h] * f["k_i"][h]
                  + dqb[h] * f["qb"][h] - dkb[h] * f["kb"][h])
            db_last = _colsum(dkb[h] * f["kb"][h]) + ebl * _colsum(gla_state_t[h] * d_gla_new[h])
            db_heads.append(db + jnp.where(row_id == CHUNK - 1, db_last, 0.0))
        db = jnp.concatenate(db_heads, axis=1)
        d_la = _running_sum(col >= row, db)
        d_logit = d_la * (1.0 / GATE_TAU) * (1.0 - _sigmoid(f["logit"]))
        put(OFF_LR, 0, _mm_nt(d_logit, wg_ref[...]))
        dwg_ref[...] += _mm_tn(f["glr"], d_logit)
        dbg_ref[...] += _colsum(d_logit)

    state_blk = pl.BlockSpec((per_step, N_HEADS, HEAD_W, HEAD_W), lambda i: (last - i, 0, 0, 0))
    rot_blk = pl.BlockSpec((per_step, 8, HEAD_W), lambda i: (last - i, 0, 0))
    width = N_HEADS * HEAD_W
    vec_out = pl.BlockSpec((1, width), lambda i: (0, 0))
    hbm = pl.BlockSpec(memory_space=pl.ANY)
    rows_blk = per_step * CHUNK
    return pl.pallas_call(
        body, name="mixer_bwd", grid=(n_steps,),
        out_shape=(jax.ShapeDtypeStruct((seq, N_PROJ), MXU_DTYPE),
                   jax.ShapeDtypeStruct((1, width), jnp.float32), jax.ShapeDtypeStruct((1, width), jnp.float32),
                   jax.ShapeDtypeStruct((HEAD_W, width), jnp.float32), jax.ShapeDtypeStruct((1, width), jnp.float32))
        + _exchange_out_shapes(riders, False),
        in_specs=[pl.BlockSpec((rows_blk, N_PROJ), lambda i: (last - i, 0)),
                  pl.BlockSpec((rows_blk, D_MODEL), lambda i: (last - i, 0)), state_blk, state_blk, rot_blk,
                  _const_spec(rot_b.shape),
                  _const_spec(dm_t.shape), _const_spec(qdec_t.shape), _const_spec(kdec_t.shape),
                  _const_spec(wg_p.shape), _const_spec(bg_p.shape), _const_spec(ret_norm_w.shape),
                  _const_spec(gla_norm_w.shape)] + [hbm] * n_ride,
        out_specs=(pl.BlockSpec((rows_blk, N_PROJ), lambda i: (last - i, 0)), vec_out, vec_out,
                   pl.BlockSpec((HEAD_W, width), lambda i: (0, 0)), vec_out) + (hbm,) * n_ride,
        scratch_shapes=[pltpu.VMEM((N_HEADS, HEAD_W, HEAD_W), jnp.float32),
                        pltpu.VMEM((N_HEADS, HEAD_W, HEAD_W), jnp.float32)] + _scatter_sems(n_ride),
        compiler_params=_params(("arbitrary",)),
    )(proj, dmixed, rsave, ssave, rot_a, rot_b, dm_t, qdec_t, kdec_t, wg_p, bg_p, ret_norm_w, gla_norm_w, *riders)


V_GATE1, V_SCALE2, V_SHIFT2, V_GATE2, V_LN1W, V_LN1B, V_LN2W, V_LN2B = range(8)
S_GATE1, S_SCALE2, S_SHIFT2, S_GATE2, S_LN1W, S_LN1B, S_LN2W, S_LN2B, S_LOSS = range(9)


def _mlp_fwd_bwd(x2, mixed, target, vecs, w_out, w1_chunks, w2_chunks, tm):
    seq = x2.shape[0]
    n_fc, _, fc = w1_chunks.shape

    def body(x_ref, mx_ref, t_ref, vec_ref, wo_ref, w1_ref, w2_ref,
             dmx_ref, dxa_ref, a_ref, dh_ref, u2_ref, df_ref, dm_ref, sums_ref, relu_sc):
        @pl.when(pl.program_id(0) == 0)
        def _():
            sums_ref[...] = jnp.zeros_like(sums_ref)

        vec = lambda r: vec_ref[r:r + 1, :]

        def acc(r, val):
            sums_ref[r:r + 1, :] += _colsum(val)

        xx = x_ref[...]
        m = _mm(mx_ref[...], wo_ref[...])
        z1h, rstd1 = _ln(ALPHA * xx + vec(V_GATE1) * m)
        x1 = z1h * vec(V_LN1W) + vec(V_LN1B)
        x1h, rstd0 = _ln(x1)
        u2 = (x1h * (1.0 + vec(V_SCALE2)) + vec(V_SHIFT2)).astype(MXU_DTYPE)
        u2_ref[...] = u2
        f = jnp.zeros((tm, D_MODEL), jnp.float32)
        for j in range(n_fc):
            r = jnp.maximum(_mm(u2, w1_ref[j]), 0.0)
            relu_sc[:, j * fc:(j + 1) * fc] = r
            a = (r * r).astype(MXU_DTYPE)
            a_ref[:, j * fc:(j + 1) * fc] = a
            f = f + _mm(a, w2_ref[j])
        z2h, rstd2 = _ln(ALPHA * x1 + vec(V_GATE2) * f)
        err = z2h * vec(V_LN2W) + vec(V_LN2B) - t_ref[...]
        acc(S_LOSS, err * err)
        dy = err * (1.0 / D_MODEL)
        acc(S_LN2W, dy * z2h)
        acc(S_LN2B, dy)
        dz2 = _ln_bwd(dy * vec(V_LN2W), z2h, rstd2)
        acc(S_GATE2, dz2 * f)
        df = (vec(V_GATE2) * dz2).astype(MXU_DTYPE)
        df_ref[...] = df
        du2 = jnp.zeros((tm, D_MODEL), jnp.float32)
        for j in range(n_fc):
            dh = (_mm_nt(df, w2_ref[j]) * (2.0 * relu_sc[:, j * fc:(j + 1) * fc])).astype(MXU_DTYPE)
            dh_ref[:, j * fc:(j + 1) * fc] = dh
            du2 = du2 + _mm_nt(dh, w1_ref[j])
        acc(S_SCALE2, du2 * x1h)
        acc(S_SHIFT2, du2)
        dx1 = ALPHA * dz2 + _ln_bwd(du2 * (1.0 + vec(V_SCALE2)), x1h, rstd0)
        acc(S_LN1W, dx1 * z1h)
        acc(S_LN1B, dx1)
        dz1 = _ln_bwd(dx1 * vec(V_LN1W), z1h, rstd1)
        acc(S_GATE1, dz1 * m)
        dxa_ref[...] = ALPHA * dz1
        dm = (vec(V_GATE1) * dz1).astype(MXU_DTYPE)
        dm_ref[...] = dm
        dmx_ref[...] = _mm_nt(dm, wo_ref[...])

    tile = lambda width: pl.BlockSpec((tm, width), lambda i: (i, 0))
    f32 = lambda width: jax.ShapeDtypeStruct((seq, width), jnp.float32)
    b16 = lambda width: jax.ShapeDtypeStruct((seq, width), MXU_DTYPE)
    return pl.pallas_call(
        body, name="mlp_fwd_bwd", grid=(seq // tm,),
        out_shape=(f32(D_MODEL), f32(D_MODEL), b16(D_FF), b16(D_FF), b16(D_MODEL), b16(D_MODEL), b16(D_MODEL),
                   jax.ShapeDtypeStruct((16, D_MODEL), jnp.float32)),
        in_specs=[tile(D_MODEL), tile(D_MODEL), tile(D_MODEL), _const_spec(vecs.shape), _const_spec(w_out.shape),
                  _const_spec(w1_chunks.shape), _const_spec(w2_chunks.shape)],
        out_specs=(tile(D_MODEL), tile(D_MODEL), tile(D_FF), tile(D_FF), tile(D_MODEL), tile(D_MODEL),
                   tile(D_MODEL), pl.BlockSpec((16, D_MODEL), lambda i: (0, 0))),
        scratch_shapes=[pltpu.VMEM((tm, D_FF), jnp.float32)],
        compiler_params=_params(("arbitrary",)),
    )(x2, mixed, target, vecs, w_out, w1_chunks, w2_chunks)


def _grad_matmul(a, b, name, tn, blocks_are_rows, riders=()):
    seq, m_dim = a.shape
    n_dim = b.shape[1]
    tk = min(seq, GRAD_TOKEN_TILE)
    nk = seq // tk
    n_ride = len(riders)
    if blocks_are_rows:
        tm = m_dim // N_CHIP
        assert tn == n_dim
        grid = (N_CHIP, 1, nk)
        out_map = lambda i, j, k: (i, 0, 0)
    else:
        tm = m_dim
        assert tn * N_CHIP == n_dim
        grid = (1, N_CHIP, nk)
        out_map = lambda i, j, k: (j, 0, 0)

    def body(*refs):
        a_ref, b_ref = refs[:2]
        ride_in, refs = refs[2:2 + n_ride], refs[2 + n_ride:]
        o_ref = refs[0]
        ride_out, refs = refs[1:1 + n_ride], refs[1 + n_ride:]
        acc_sc = refs[0]
        exchange = _ChipScatter(ride_in, ride_out, refs[1:]) if n_ride else None
        block = pl.program_id(0) + pl.program_id(1)
        k = pl.program_id(2)

        if exchange is not None:
            @pl.when((block == 0) & (k == 0))
            def _():
                exchange.start()

        @pl.when(k == 0)
        def _():
            acc_sc[...] = jnp.zeros_like(acc_sc)

        acc_sc[...] += _mm_tn(a_ref[...], b_ref[...])

        @pl.when(k == nk - 1)
        def _():
            o_ref[0] = acc_sc[...].astype(o_ref.dtype)

        if exchange is not None:
            @pl.when((block == N_CHIP - 1) & (k == nk - 1))
            def _():
                exchange.wait()

    hbm = pl.BlockSpec(memory_space=pl.ANY)
    out = pl.pallas_call(
        body, name=name, grid=grid,
        out_shape=(jax.ShapeDtypeStruct((N_CHIP, tm, tn), WIRE_DTYPE),) + _exchange_out_shapes(riders, False),
        in_specs=[pl.BlockSpec((tk, tm), lambda i, j, k: (k, i)), pl.BlockSpec((tk, tn), lambda i, j, k: (k, j))]
        + [hbm] * n_ride,
        out_specs=(pl.BlockSpec((1, tm, tn), out_map),) + (hbm,) * n_ride,
        scratch_shapes=[pltpu.VMEM((tm, tn), jnp.float32)] + (_scatter_sems(n_ride) if n_ride else []),
        compiler_params=_params(("arbitrary", "arbitrary", "arbitrary")),
    )(a, b, *riders)
    return out if n_ride else out[0]


def _grad_matmul_full(a, b, name, tm, riders):
    seq, m_dim = a.shape
    n_dim = b.shape[1]
    tk = min(seq, GRAD_TOKEN_TILE)
    nk = seq // tk
    n_blocks = m_dim // tm
    n_ride = len(riders)
    assert m_dim % tm == 0

    def body(*refs):
        a_ref, b_ref = refs[:2]
        ride_in, refs = refs[2:2 + n_ride], refs[2 + n_ride:]
        o_ref = refs[0]
        ride_out, refs = refs[1:1 + n_ride], refs[1 + n_ride:]
        acc_sc = refs[0]
        swap = _SiblingSwap(ride_in, ride_out, refs[1:])
        i, k = pl.program_id(0), pl.program_id(1)

        @pl.when((i == 0) & (k == 0))
        def _():
            swap.start()

        @pl.when(k == 0)
        def _():
            acc_sc[...] = jnp.zeros_like(acc_sc)

        acc_sc[...] += _mm_tn(a_ref[...], b_ref[...])

        @pl.when(k == nk - 1)
        def _():
            o_ref[...] = acc_sc[...].astype(o_ref.dtype)

        @pl.when((i == n_blocks - 1) & (k == nk - 1))
        def _():
            swap.wait()

    hbm = pl.BlockSpec(memory_space=pl.ANY)
    return pl.pallas_call(
        body, name=name, grid=(n_blocks, nk),
        out_shape=(jax.ShapeDtypeStruct((m_dim, n_dim), WIRE_DTYPE),)
        + tuple(jax.ShapeDtypeStruct(r.shape, r.dtype) for r in riders),
        in_specs=[pl.BlockSpec((tk, tm), lambda i, k: (k, i)), pl.BlockSpec((tk, n_dim), lambda i, k: (k, 0))]
        + [hbm] * n_ride,
        out_specs=(pl.BlockSpec((tm, n_dim), lambda i, k: (i, 0)),) + (hbm,) * n_ride,
        scratch_shapes=[pltpu.VMEM((tm, n_dim), jnp.float32)] + _swap_sems(n_ride),
        compiler_params=_params(("arbitrary", "arbitrary")),
    )(a, b, *riders)


def _sum_chips(stack, name):
    _, rows, cols = stack.shape
    tc = min(cols, ELEMENTWISE_COLS)

    def body(s_ref, o_ref):
        total = s_ref[0].astype(jnp.float32)
        for j in range(1, N_CHIP):
            total = total + s_ref[j].astype(jnp.float32)
        o_ref[...] = total

    return pl.pallas_call(
        body, name=name, grid=(cols // tc,),
        out_shape=jax.ShapeDtypeStruct((rows, cols), jnp.float32),
        in_specs=[pl.BlockSpec((N_CHIP, rows, tc), lambda i: (0, 0, i))],
        out_specs=pl.BlockSpec((rows, tc), lambda i: (0, i)),
        compiler_params=_params(("arbitrary",)),
    )(stack)


def _adam_pair(w, g_mine, g_sibling, m, v, name):
    rows, cols = w.shape
    tc = min(cols, ELEMENTWISE_COLS)

    def total(ref):
        if len(ref.shape) == 2:
            return ref[...]
        acc = ref[0].astype(jnp.float32)
        for j in range(1, ref.shape[0]):
            acc = acc + ref[j].astype(jnp.float32)
        return acc

    def body(w_ref, ga_ref, gb_ref, m_ref, v_ref, g_ref, dl_ref, m2_ref, v2_ref):
        g = total(ga_ref) + total(gb_ref)
        delta, m2, v2 = _adam(w_ref[...], g, m_ref[...], v_ref[...])
        g_ref[...] = g
        dl_ref[...] = delta
        m2_ref[...] = m2
        v2_ref[...] = v2

    blk = pl.BlockSpec((rows, tc), lambda i: (0, i))
    g_blk = lambda a: blk if a.ndim == 2 else pl.BlockSpec((a.shape[0], rows, tc), lambda i: (0, 0, i))
    out = jax.ShapeDtypeStruct((rows, cols), jnp.float32)
    return pl.pallas_call(
        body, name=name, grid=(cols // tc,),
        out_shape=(out, out, out, out),
        in_specs=[blk, g_blk(g_mine), g_blk(g_sibling), blk, blk], out_specs=(blk,) * 4,
        compiler_params=_params(("arbitrary",)),
    )(w, g_mine, g_sibling, m, v)


def _sum_devices(gathered):
    _, rows, _ = gathered.shape

    def body(g_ref, o_ref):
        total = g_ref[0]
        for d in range(1, N_DEV):
            total = total + g_ref[d]
        o_ref[...] = total

    return pl.pallas_call(
        body, name="sum_devices",
        out_shape=jax.ShapeDtypeStruct((rows, 128), jnp.float32),
    )(gathered)


def _adam_small(params):
    n = len(params)

    def body(*refs):
        ins, outs = refs[:4 * n], refs[4 * n:]
        for i in range(n):
            w_ref, g_ref, m_ref, v_ref = ins[4 * i:4 * i + 4]
            delta, m2, v2 = _adam(w_ref[...], g_ref[...], m_ref[...], v_ref[...])
            outs[3 * i][...] = delta
            outs[3 * i + 1][...] = m2
            outs[3 * i + 2][...] = v2

    out_shape = tuple(jax.ShapeDtypeStruct(p[0].shape, jnp.float32) for p in params for _ in range(3))
    out = pl.pallas_call(body, name="adam_small", out_shape=out_shape)(*[t for p in params for t in p])
    return [out[3 * i:3 * i + 3] for i in range(n)]


def _pad_heads(w):
    lead = w.shape[:-1]
    w = w.reshape(lead + (N_HEADS, GLA_DK))
    w = jnp.pad(w, [(0, 0)] * len(lead) + [(0, 0), (0, HEAD_W - GLA_DK)])
    return w.reshape(lead + (N_HEADS * HEAD_W,))


def _unpad_heads(w):
    lead = w.shape[:-1]
    return w.reshape(lead + (N_HEADS, HEAD_W))[..., :GLA_DK].reshape(lead + (N_HEADS * GLA_DK,))


def _pad_head_rows(w):
    w = w.reshape(N_HEADS, GLA_DK, w.shape[-1])
    return jnp.pad(w, ((0, 0), (0, HEAD_W - GLA_DK), (0, 0))).reshape(N_HEADS * HEAD_W, w.shape[-1])


def _unpad_head_rows(w):
    return w.reshape(N_HEADS, HEAD_W, w.shape[-1])[:, :GLA_DK].reshape(N_HEADS * GLA_DK, w.shape[-1])


def _pad_w_in_rows(stack):
    w = stack.reshape(-1, stack.shape[-1])
    return jnp.concatenate([
        w[:2048], _pad_head_rows(w[2048:2304]), _pad_head_rows(w[2304:2560]), w[2560:3584],
        jnp.pad(w[3584:3600], ((0, HEAD_W - GATE_RANK), (0, 0)))], axis=0)


def _unpad_w_in_stack(g, per):
    segments = [(0, g[:2048]), (2048, _unpad_head_rows(g[OFF_GQ:OFF_GQ + 512])),
                (2304, _unpad_head_rows(g[OFF_GK:OFF_GK + 512])), (2560, g[OFF_GV:OFF_LR]),
                (3584, g[OFF_LR:OFF_LR + GATE_RANK])]
    blocks = []
    for j in range(N_CHIP):
        lo, hi = j * per, (j + 1) * per
        pieces = []
        for start, rows in segments:
            a, b = max(lo, start), min(hi, start + rows.shape[0])
            if a < b:
                pieces.append(rows[a - start:b - start])
        blocks.append(jnp.concatenate(pieces, axis=0))
    return jnp.stack(blocks)


def _col_major(w):
    return jnp.transpose(w, (2, 0, 1)).reshape(w.shape[2], w.shape[1])


def _rows128(a):
    return a.reshape(-1, 128)


def _rows8(a):
    a = a.reshape(-1, 128)
    return jnp.pad(a, ((0, -a.shape[0] % 8), (0, 0)))


def kernel(x, c, w_ada, b_ada, w_in, ret_norm_w, gla_gate_w, gla_gate_b, gla_norm_w, w_out, ln1_w, ln1_b, w_ff1, w_ff2, ln2_w, ln2_b, loss_target, m_w_ada, m_b_ada, m_w_in, m_ret_norm_w, m_gla_gate_w, m_gla_gate_b, m_gla_norm_w, m_w_out, m_ln1_w, m_ln1_b, m_w_ff1, m_w_ff2, m_ln2_w, m_ln2_b, v_w_ada, v_b_ada, v_w_in, v_ret_norm_w, v_gla_gate_w, v_gla_gate_b, v_gla_norm_w, v_w_out, v_ln1_w, v_ln1_b, v_w_ff1, v_w_ff2, v_ln2_w, v_ln2_b):
    seq = x.shape[1]
    tm = min(seq, TOKEN_TILE)
    tm_in = min(seq, INPROJ_TOKEN_TILE)
    xi, yi, ci = _mesh_pos()
    dev = 4 * xi + 2 * yi + ci
    chip = 2 * xi + yi
    x2, target = x[0], loss_target[0]
    ada_cols = w_ada.shape[2]
    in_cols = w_in.shape[2]
    gate_cols = gla_gate_w.shape[2]

    b_blk = lax.dynamic_slice(b_ada, (0, chip * ada_cols), (1, ada_cols))
    g0, g1, w_in_stack = _prologue(jnp.concatenate([_rows128(c), _rows128(gla_gate_w[0])], axis=0), w_ada[0], b_blk,
                                   _col_major(w_in).astype(WIRE_DTYPE))
    c_all = g0[:, :8].reshape(N_DEV, D_MODEL)
    gate_w_full = jnp.concatenate([g0[2 * j, 8:16].reshape(GATE_RANK, gate_cols) for j in range(N_CHIP)], axis=1)
    wg_p = jnp.pad(_pad_heads(gate_w_full), ((0, HEAD_W - GATE_RANK), (0, 0)))
    bg_p = _pad_heads(gla_gate_b)
    mine = lax.dynamic_index_in_dim(g1, dev, axis=2, keepdims=False)
    mod = jnp.concatenate([mine[2 * j].reshape(1, ada_cols) for j in range(N_CHIP)], axis=1)
    shift1, scale1, gate1, shift2, scale2, gate2 = [mod[:, i * D_MODEL:(i + 1) * D_MODEL] for i in range(6)]
    w_in_pt = _pad_w_in_rows(w_in_stack).astype(MXU_DTYPE)
    w_in_p = jnp.transpose(w_in_pt)

    zeros_row = jnp.zeros((1, D_MODEL), jnp.float32)
    vecs1 = jnp.concatenate([shift1, scale1] + [zeros_row] * 6, axis=0)
    proj, u, w2_stack = _inproj_fwd(x2, vecs1, w_in_p, tm_in, [w_ff2[0].astype(WIRE_DTYPE)])
    rot_a, rot_b = _rotary_tables(seq)
    dm_t, qdec_t, kdec_t, chunk_decay = _decay_tables()
    tables = (rot_a, rot_b, dm_t, qdec_t, kdec_t, chunk_decay)
    mixed, rsave, ssave, w_out_stack, w1_stack = _mixer_fwd(
        proj, tables, wg_p, bg_p, ret_norm_w, gla_norm_w,
        [w_out[0].astype(WIRE_DTYPE), w_ff1[0].astype(WIRE_DTYPE)])
    w_out_full = w_out_stack.reshape(D_MODEL, D_MODEL).astype(MXU_DTYPE)
    w1_chunks = w1_stack.astype(MXU_DTYPE)
    w2_chunks = w2_stack.astype(MXU_DTYPE)

    vecs2 = jnp.concatenate([gate1, scale2, shift2, gate2, ln1_w, ln1_b, ln2_w, ln2_b], axis=0)
    dmixed, dxa, act, dh, u2, df, dm, sums2 = _mlp_fwd_bwd(x2, mixed, target, vecs2, w_out_full, w1_chunks,
                                                           w2_chunks, tm)

    g_out_stack = _grad_matmul(mixed, dm, "grad_w_out", D_MODEL, True)
    g_ff1_stack, r_out = _grad_matmul(u2, dh, "grad_w_ff1", D_FF // N_CHIP, False, [g_out_stack])
    g_ff2_stack = _grad_matmul(act, df, "grad_w_ff2", D_MODEL, True)
    dproj, d_ret_norm, d_gla_norm, d_wg_p, d_bg_p, r_ff1, r_ff2 = _mixer_bwd(
        proj, dmixed, rsave, ssave, tables, wg_p, bg_p, ret_norm_w, gla_norm_w, [g_ff1_stack, g_ff2_stack])
    early = ["w_out", "w_ff1", "w_ff2"]
    partial = dict(zip(early, [r_out, r_ff1, r_ff2]))
    g_in_t, *swapped_early = _grad_matmul_full(dproj, u, "grad_w_in", N_PROJ // 3, [partial[n] for n in early])
    swapped = dict(zip(early, swapped_early))
    g_in_stack = _unpad_w_in_stack(g_in_t, in_cols)
    grad_x, sums1, r_in = _inproj_bwd(dproj, x2, dxa, vecs1, w_in_pt, tm_in, [g_in_stack])

    dmod = jnp.concatenate([sums1[0:1], sums1[1:2], sums2[S_GATE1:S_GATE1 + 1], sums2[S_SHIFT2:S_SHIFT2 + 1],
                            sums2[S_SCALE2:S_SCALE2 + 1], sums2[S_GATE2:S_GATE2 + 1]], axis=1)
    d_gate_w_full = _unpad_heads(d_wg_p[:GATE_RANK])
    flat = lambda parts: jnp.concatenate([_rows8(p) for p in parts], axis=0)
    small = flat([dmod, sums2[S_LN1W:S_LN1W + 1], sums2[S_LN1B:S_LN1B + 1], sums2[S_LN2W:S_LN2W + 1],
                  sums2[S_LN2B:S_LN2B + 1], d_ret_norm, _unpad_heads(d_bg_p), d_gla_norm, d_gate_w_full,
                  sums2[S_LOSS:S_LOSS + 1]])
    partial["w_in"] = _sum_chips(r_in, "sum_w_in")
    g2, swapped["w_in"] = _gather_rows(small, "gather_small", [partial["w_in"]])
    tot = _sum_devices(g2)
    loss = 0.5 / D_MODEL * jnp.sum(tot[136:144])
    grad_b_ada = tot[0:48].reshape(1, 6 * D_MODEL)
    grad_ln1_w, grad_ln1_b = tot[48:56].reshape(1, D_MODEL), tot[56:64].reshape(1, D_MODEL)
    grad_ln2_w, grad_ln2_b = tot[64:72].reshape(1, D_MODEL), tot[72:80].reshape(1, D_MODEL)
    grad_ret_norm = tot[80:84].reshape(1, 512)
    grad_gate_b = tot[88:90].reshape(1, 256)
    grad_gla_norm = tot[96:100].reshape(1, 512)
    grad_gate_w = lax.dynamic_slice(tot[104:136].reshape(GATE_RANK, 256), (0, chip * gate_cols),
                                    (GATE_RANK, gate_cols))

    small_grads = [grad_b_ada, grad_ln1_w, grad_ln1_b, grad_ln2_w, grad_ln2_b, grad_ret_norm, grad_gate_b,
                   grad_gla_norm, grad_gate_w[None]]
    small_out = _adam_small(list(zip(
        [b_ada, ln1_w, ln1_b, ln2_w, ln2_b, ret_norm_w, gla_gate_b, gla_norm_w, gla_gate_w], small_grads,
        [m_b_ada, m_ln1_w, m_ln1_b, m_ln2_w, m_ln2_b, m_ret_norm_w, m_gla_gate_b, m_gla_norm_w, m_gla_gate_w],
        [v_b_ada, v_ln1_w, v_ln1_b, v_ln2_w, v_ln2_b, v_ret_norm_w, v_gla_gate_b, v_gla_norm_w, v_gla_gate_w])))
    sm_delta, sm_m, sm_v = [[o[k] for o in small_out] for k in range(3)]

    dmod_all = g2[:, 0:48].reshape(N_DEV, 6 * D_MODEL)
    dmod_blk = lax.dynamic_slice(dmod_all, (0, chip * ada_cols), (N_DEV, ada_cols))
    ada_out = _ada_bwd_adam(jnp.transpose(c_all), dmod_blk, w_ada[0], m_w_ada[0], v_w_ada[0])
    ada_g, ada_delta, ada_m, ada_v = [t[None] for t in ada_out]

    big = {}
    for n, w, m, v in zip(["w_in", "w_out", "w_ff1", "w_ff2"], [w_in, w_out, w_ff1, w_ff2],
                          [m_w_in, m_w_out, m_w_ff1, m_w_ff2], [v_w_in, v_w_out, v_w_ff1, v_w_ff2]):
        mine, theirs = partial[n], swapped[n]
        if n == "w_in":
            out = _adam_pair(_col_major(w), mine, theirs, _col_major(m), _col_major(v), "adam_" + n)
            big[n] = [jnp.transpose(t.reshape(t.shape[0], 1, t.shape[1]), (1, 2, 0)) for t in out]
        else:
            big[n] = [t[None] for t in _adam_pair(w[0], mine, theirs, m[0], v[0], "adam_" + n)]

    def assemble(ada, smalls, k):
        b_ada_o, ln1w_o, ln1b_o, ln2w_o, ln2b_o, ret_o, gb_o, gln_o, gw_o = smalls
        return [ada, b_ada_o, big["w_in"][k], ret_o, gw_o, gb_o, gln_o, big["w_out"][k], ln1w_o, ln1b_o,
                big["w_ff1"][k], big["w_ff2"][k], ln2w_o, ln2b_o]

    grads = assemble(ada_g, small_grads, 0)
    deltas = assemble(ada_delta, sm_delta, 1)
    new_m = assemble(ada_m, sm_m, 2)
    new_v = assemble(ada_v, sm_v, 3)
    return (loss, grad_x[None], *grads, *deltas, *new_m, *new_v)
```

```python
import numpy as np
import jax
import jax.numpy as jnp
from jax import lax
from jax.experimental import pallas as pl
from jax.experimental.pallas import tpu as pltpu

D_MODEL = 1024
D_FF = 4096
CHUNK = 64
N_HEADS = 4
HEAD_W = 128
GLA_DK = 64
GATE_RANK = 16
GATE_TAU = 16.0
LN_EPS = 1e-5
ALPHA = 2.0 ** 0.25
ROPE_BASE = 10000.0
RET_SCALE = float(HEAD_W) ** -0.5
GLA_SCALE = float(GLA_DK) ** -0.5

ADAM_LR = 0.001
ADAM_B1 = 0.9
ADAM_B2 = 0.999
ADAM_EPS = 1e-08
ADAM_WD = 0.01
ADAM_STEP = 10

OFF_RQ, OFF_RK, OFF_RV, OFF_RG = 0, 512, 1024, 1536
OFF_GQ, OFF_GK, OFF_GV, OFF_GG, OFF_LR = 2048, 2560, 3072, 3584, 4096
N_PROJ = 4224

N_DEV = 8
N_CHIP = 4
MESH = pl.DeviceIdType.MESH
MXU_DTYPE = jnp.bfloat16
WIRE_DTYPE = jnp.bfloat16
VMEM_LIMIT = 60 * 1024 * 1024
TOKEN_TILE = 256
INPROJ_TOKEN_TILE = 512
CHUNKS_PER_STEP = 8
CHUNKS_IN_LOCKSTEP = 4
GRAD_TOKEN_TILE = 2048
ELEMENTWISE_COLS = 256
HIGHEST = lax.Precision.HIGHEST


def _mm(a, b):
    return jnp.dot(a.astype(MXU_DTYPE), b.astype(MXU_DTYPE), preferred_element_type=jnp.float32)


def _mm_nt(a, b):
    return lax.dot_general(a.astype(MXU_DTYPE), b.astype(MXU_DTYPE), (((1,), (1,)), ((), ())),
                           preferred_element_type=jnp.float32)


def _mm_tn(a, b):
    return lax.dot_general(a.astype(MXU_DTYPE), b.astype(MXU_DTYPE), (((0,), (0,)), ((), ())),
                           preferred_element_type=jnp.float32)


def _mm32(a, b):
    return jnp.dot(a, b, precision=HIGHEST, preferred_element_type=jnp.float32)


def _running_sum(mask, a):
    m = mask.astype(jnp.bfloat16)
    hi = a.astype(jnp.bfloat16)
    rest = a - hi.astype(jnp.float32)
    mid = rest.astype(jnp.bfloat16)
    lo = (rest - mid.astype(jnp.float32)).astype(jnp.bfloat16)
    dot = lambda t: jnp.dot(m, t, preferred_element_type=jnp.float32)
    return dot(hi) + dot(mid) + dot(lo)


def _rowmean(a):
    return jnp.mean(a, axis=-1, keepdims=True)


def _colsum(a):
    return jnp.sum(a, axis=0, keepdims=True)


def _ln(z):
    zc = z - _rowmean(z)
    rstd = lax.rsqrt(_rowmean(zc * zc) + LN_EPS)
    return zc * rstd, rstd


def _ln_bwd(dzh, zh, rstd):
    return rstd * (dzh - _rowmean(dzh) - zh * _rowmean(dzh * zh))


def _sigmoid(a):
    return 1.0 / (1.0 + jnp.exp(-a))


def _log_sigmoid(a):
    return jnp.minimum(a, 0.0) - jnp.log(1.0 + jnp.exp(-jnp.abs(a)))


def _swap_halves(a):
    return pltpu.roll(a, HEAD_W // 2, 1)


def _tri_masks():
    row = lax.broadcasted_iota(jnp.int32, (CHUNK, CHUNK), 0)
    col = lax.broadcasted_iota(jnp.int32, (CHUNK, CHUNK), 1)
    return row, col


def _const_spec(shape):
    zeros = (0,) * len(shape)
    return pl.BlockSpec(shape, lambda *_: zeros, pipeline_mode=pl.Buffered(1))


def _params(semantics):
    return pltpu.CompilerParams(dimension_semantics=semantics, vmem_limit_bytes=VMEM_LIMIT)


def _decay_tables():
    log_gamma = np.log(1.0 - 2.0 ** (-5.0 - np.arange(N_HEADS, dtype=np.float64)))
    idx = np.arange(CHUNK, dtype=np.float64)
    dist = np.abs(idx[:, None] - idx[None, :])
    intra = np.exp(log_gamma[:, None, None] * dist)
    kdec = np.exp(log_gamma[None, :] * (CHUNK - 1.0 - idx)[:, None])
    qdec = np.exp(log_gamma[None, :] * (idx + 1.0)[:, None])
    chunk_decay = np.exp(log_gamma * CHUNK)
    lanes = lambda t: np.repeat(t, HEAD_W, axis=1).astype(np.float32)
    return (jnp.asarray(intra.astype(np.float32)), jnp.asarray(lanes(qdec)), jnp.asarray(lanes(kdec)),
            [float(np.float32(v)) for v in chunk_decay])


def _rotary_tables(seq):
    half = HEAD_W // 2
    inv = 1.0 / (ROPE_BASE ** jnp.linspace(0.0, 1.0, half, dtype=jnp.float32))
    both = lambda t: jnp.concatenate([t, t], axis=-1)
    ang_a = jnp.arange(0, seq, CHUNK, dtype=jnp.float32)[:, None] * inv[None, :]
    rot_a = jnp.stack([both(jnp.cos(ang_a)), both(jnp.sin(ang_a))], axis=1)
    rot_a = jnp.pad(rot_a, ((0, 0), (0, 6), (0, 0)))
    ang_b = jnp.arange(CHUNK, dtype=jnp.float32)[:, None] * inv[None, :]
    cos_b, sin_b = both(jnp.cos(ang_b)), both(jnp.sin(ang_b))
    sign = jnp.concatenate([-jnp.ones((half,), jnp.float32), jnp.ones((half,), jnp.float32)])
    return rot_a, jnp.stack([cos_b, sin_b, cos_b * sign, sin_b * sign])


def _rotary_chunk(ra_ref, c, rb_ref):
    cos_a, sin_a = ra_ref[c, 0:1, :], ra_ref[c, 1:2, :]
    return cos_a * rb_ref[0] - sin_a * rb_ref[1], sin_a * rb_ref[2] + cos_a * rb_ref[3]


def _mesh_pos():
    return lax.axis_index("x"), lax.axis_index("y"), lax.axis_index("c")


def _flip(v, bit):
    return 1 - v if bit else v


def _gather_rows(v, name, swaps):
    rows = v.shape[0]
    n = len(swaps)

    def body(*refs):
        v_ref, out_ref = refs[0], refs[1 + n]
        swap = _SiblingSwap(refs[1:1 + n], refs[2 + n:2 + 2 * n], refs[4 + 2 * n:])
        swap.start()
        _all_devices_exchange(v_ref, out_ref, refs[2 + 2 * n], refs[3 + 2 * n])
        swap.wait()

    hbm = pl.BlockSpec(memory_space=pl.ANY)
    vmem = pl.BlockSpec(memory_space=pltpu.VMEM)
    return pl.pallas_call(
        body, name=name,
        out_shape=(jax.ShapeDtypeStruct((N_DEV, rows, 128), jnp.float32),)
        + tuple(jax.ShapeDtypeStruct(a.shape, a.dtype) for a in swaps),
        in_specs=[vmem] + [hbm] * n,
        out_specs=(vmem,) + (hbm,) * n,
        scratch_shapes=_all_devices_sems() + _swap_sems(n),
    )(v, *swaps)


def _all_devices_sems():
    return [pltpu.SemaphoreType.DMA((N_DEV - 1,)), pltpu.SemaphoreType.DMA((N_DEV - 1,))]


def _all_devices_exchange(v_ref, out_ref, send_sems, recv_sems):
    x, y, c = _mesh_pos()
    me = 4 * x + 2 * y + c
    out_ref[me] = v_ref[...]
    sends, recvs = [], []
    for k in range(1, N_DEV):
        px, py, pc = _flip(x, (k >> 2) & 1), _flip(y, (k >> 1) & 1), _flip(c, k & 1)
        peer = 4 * px + 2 * py + pc
        sends.append(pltpu.make_async_remote_copy(
            src_ref=v_ref, dst_ref=out_ref.at[me], send_sem=send_sems.at[k - 1], recv_sem=recv_sems.at[k - 1],
            device_id=(px, py, pc), device_id_type=MESH))
        recvs.append(pltpu.make_async_remote_copy(
            src_ref=v_ref, dst_ref=out_ref.at[peer], send_sem=send_sems.at[k - 1], recv_sem=recv_sems.at[k - 1],
            device_id=(px, py, pc), device_id_type=MESH))
    for cp in sends:
        cp.start()
    for cp in recvs:
        cp.wait_recv()
    for cp in sends:
        cp.wait_send()


def _prologue(cond_rows, w_ada_blk, b_blk, w_in_t):
    cols = w_ada_blk.shape[1]
    groups = cols // 128
    c_rows = D_MODEL // 128

    def body(cond_ref, w_ref, b_ref, win_ref, cond_all_ref, mod_all_ref, stack_ref, mod_sc, *sems):
        gather = _ChipGather([win_ref], [stack_ref], sems[:5])
        gather.start()
        _all_devices_exchange(cond_ref, cond_all_ref, sems[5], sems[6])
        acc = jnp.broadcast_to(b_ref[...], (N_DEV, cols))
        for r in range(c_rows):
            cv = cond_all_ref[:, r, :]
            acc = acc + _mm32(cv * _sigmoid(cv), w_ref[r * 128:(r + 1) * 128, :])
        for k in range(groups):
            mod_sc[k] = acc[:, k * 128:(k + 1) * 128]
        _all_devices_exchange(mod_sc, mod_all_ref, sems[7], sems[8])
        gather.forward()
        gather.finish()

    vmem = pl.BlockSpec(memory_space=pltpu.VMEM)
    hbm = pl.BlockSpec(memory_space=pl.ANY)
    return pl.pallas_call(
        body, name="prologue",
        out_shape=(jax.ShapeDtypeStruct((N_DEV,) + cond_rows.shape, jnp.float32),
                   jax.ShapeDtypeStruct((N_DEV, groups, N_DEV, 128), jnp.float32))
        + _exchange_out_shapes([w_in_t], True),
        in_specs=[vmem, vmem, vmem, hbm],
        out_specs=(vmem, vmem, hbm),
        scratch_shapes=[pltpu.VMEM((groups, N_DEV, 128), jnp.float32)] + _gather_sems(1)
        + _all_devices_sems() + _all_devices_sems(),
        compiler_params=pltpu.CompilerParams(vmem_limit_bytes=VMEM_LIMIT),
    )(cond_rows, w_ada_blk, b_blk, w_in_t)


def _exchange_out_shapes(arrays, gather):
    return tuple(jax.ShapeDtypeStruct((N_CHIP,) + a.shape if gather else a.shape, a.dtype) for a in arrays)


def _scatter_sems(n):
    n_sem = n * (N_CHIP - 1)
    return [pltpu.SemaphoreType.DMA((n_sem,)), pltpu.SemaphoreType.DMA((n_sem,)), pltpu.SemaphoreType.DMA((n,))]


def _gather_sems(n):
    n_sem = n * (N_CHIP - 1)
    return [pltpu.SemaphoreType.DMA((n_sem,))] * 4 + [pltpu.SemaphoreType.DMA((n,))]


def _peer_chips(x, y):
    out = []
    for k in range(1, N_CHIP):
        px, py = _flip(x, (k >> 1) & 1), _flip(y, k & 1)
        out.append((px, py, 2 * px + py))
    return out


class _ChipScatter:
    def __init__(self, ins, outs, sems):
        send_sems, recv_sems, local_sems = sems
        x, y, c = _mesh_pos()
        chip = 2 * x + y
        self.local, self.sends, self.recvs = [], [], []
        for i in range(len(ins)):
            self.local.append(pltpu.make_async_copy(ins[i].at[chip], outs[i].at[chip], local_sems.at[i]))
            for k, (px, py, peer_chip) in enumerate(_peer_chips(x, y)):
                sem = i * (N_CHIP - 1) + k
                src = ins[i].at[peer_chip]
                self.sends.append(pltpu.make_async_remote_copy(
                    src_ref=src, dst_ref=outs[i].at[chip], send_sem=send_sems.at[sem], recv_sem=recv_sems.at[sem],
                    device_id=(px, py, c), device_id_type=MESH))
                self.recvs.append(pltpu.make_async_remote_copy(
                    src_ref=src, dst_ref=outs[i].at[peer_chip], send_sem=send_sems.at[sem], recv_sem=recv_sems.at[sem],
                    device_id=(px, py, c), device_id_type=MESH))

    def start(self):
        for cp in self.local + self.sends:
            cp.start()

    def wait(self):
        for cp in self.recvs:
            cp.wait_recv()
        for cp in self.sends:
            cp.wait_send()
        for cp in self.local:
            cp.wait()


class _ChipGather:
    def __init__(self, ins, outs, sems):
        ici_send, ici_recv, d2d_send, d2d_recv, local_sems = sems
        x, y, c = _mesh_pos()
        chip = 2 * x + y
        self.local, self.ici_sends, self.ici_recvs, self.d2d_sends, self.d2d_recvs = [], [], [], [], []
        for i in range(len(ins)):
            half = ins[i].shape[-1] // 2
            assert half % 128 == 0
            lead = (slice(None),) * (len(ins[i].shape) - 1)
            mine = lead + (pl.ds(pl.multiple_of(c * half, 128), half),)
            theirs = lead + (pl.ds(pl.multiple_of((1 - c) * half, 128), half),)
            self.local.append(pltpu.make_async_copy(ins[i], outs[i].at[chip], local_sems.at[i]))
            for k, (px, py, peer_chip) in enumerate(_peer_chips(x, y)):
                sem = i * (N_CHIP - 1) + k
                self.ici_sends.append(pltpu.make_async_remote_copy(
                    src_ref=ins[i].at[mine], dst_ref=outs[i].at[chip].at[mine],
                    send_sem=ici_send.at[sem], recv_sem=ici_recv.at[sem], device_id=(px, py, c), device_id_type=MESH))
                landed = outs[i].at[peer_chip].at[mine]
                self.ici_recvs.append(pltpu.make_async_remote_copy(
                    src_ref=ins[i].at[mine], dst_ref=landed,
                    send_sem=ici_send.at[sem], recv_sem=ici_recv.at[sem], device_id=(px, py, c), device_id_type=MESH))
                self.d2d_sends.append(pltpu.make_async_remote_copy(
                    src_ref=landed, dst_ref=landed,
                    send_sem=d2d_send.at[sem], recv_sem=d2d_recv.at[sem], device_id=(x, y, 1 - c), device_id_type=MESH))
                self.d2d_recvs.append(pltpu.make_async_remote_copy(
                    src_ref=landed, dst_ref=outs[i].at[peer_chip].at[theirs],
                    send_sem=d2d_send.at[sem], recv_sem=d2d_recv.at[sem], device_id=(x, y, 1 - c), device_id_type=MESH))

    def start(self):
        for cp in self.local + self.ici_sends:
            cp.start()

    def forward(self):
        for landed, onward in zip(self.ici_recvs, self.d2d_sends):
            landed.wait_recv()
            onward.start()

    def finish(self):
        for cp in self.d2d_recvs:
            cp.wait_recv()
        for cp in self.d2d_sends + self.ici_sends:
            cp.wait_send()
        for cp in self.local:
            cp.wait()


def _swap_sems(n):
    return [pltpu.SemaphoreType.DMA((n,)), pltpu.SemaphoreType.DMA((n,))]


class _SiblingSwap:
    def __init__(self, ins, outs, sems):
        send_sems, recv_sems = sems
        x, y, c = _mesh_pos()
        self.copies = [pltpu.make_async_remote_copy(
            src_ref=ins[i], dst_ref=outs[i], send_sem=send_sems.at[i], recv_sem=recv_sems.at[i],
            device_id=(x, y, 1 - c), device_id_type=MESH) for i in range(len(ins))]

    def start(self):
        for cp in self.copies:
            cp.start()

    def wait(self):
        for cp in self.copies:
            cp.wait_recv()
        for cp in self.copies:
            cp.wait_send()


def _adam(w, g, m, v):
    m2 = ADAM_B1 * m + (1.0 - ADAM_B1) * g
    v2 = ADAM_B2 * v + (1.0 - ADAM_B2) * (g * g)
    m_hat = m2 / (1.0 - ADAM_B1 ** ADAM_STEP)
    v_hat = v2 / (1.0 - ADAM_B2 ** ADAM_STEP)
    delta = -ADAM_LR * (m_hat / (jnp.sqrt(v_hat) + ADAM_EPS) + ADAM_WD * w)
    return delta, m2, v2


def _ada_bwd_adam(c_t, dmod_blk, w, m, v):
    rows, cols = w.shape
    tile = 512
    assert cols % tile == 0

    def body(c_ref, d_ref, w_ref, m_ref, v_ref, g_ref, dl_ref, m2_ref, v2_ref):
        sc = c_ref[...]
        sc = sc * _sigmoid(sc)
        dm = d_ref[...]
        g = sc[:, 0:1] * dm[0:1, :]
        for b in range(1, N_DEV):
            g = g + sc[:, b:b + 1] * dm[b:b + 1, :]
        delta, m2, v2 = _adam(w_ref[...], g, m_ref[...], v_ref[...])
        g_ref[...] = g
        dl_ref[...] = delta
        m2_ref[...] = m2
        v2_ref[...] = v2

    blk = pl.BlockSpec((rows, tile), lambda j: (0, j))
    out = jax.ShapeDtypeStruct((rows, cols), jnp.float32)
    return pl.pallas_call(
        body, name="ada_bwd_adam", grid=(cols // tile,),
        out_shape=(out, out, out, out),
        in_specs=[pl.BlockSpec((rows, N_DEV), lambda j: (0, 0)), pl.BlockSpec((N_DEV, tile), lambda j: (0, j)),
                  blk, blk, blk],
        out_specs=(blk, blk, blk, blk),
        compiler_params=_params(("arbitrary",)),
    )(c_t, dmod_blk, w, m, v)


def _inproj_fwd(x2, vecs, w_in_p, tm, riders):
    seq = x2.shape[0]
    n_tiles = seq // tm
    n_ride = len(riders)

    def body(*refs):
        x_ref, vec_ref, w_ref = refs[:3]
        ride_in, refs = refs[3:3 + n_ride], refs[3 + n_ride:]
        p_ref, u_ref = refs[:2]
        ride_out, sems = refs[2:2 + n_ride], refs[2 + n_ride:]
        gather = _ChipGather(ride_in, ride_out, sems)

        @pl.when(pl.program_id(0) == 0)
        def _():
            gather.start()

        xh, _ = _ln(x_ref[...])
        u = (xh * (1.0 + vec_ref[1:2, :]) + vec_ref[0:1, :]).astype(MXU_DTYPE)
        u_ref[...] = u
        p_ref[...] = _mm(u, w_ref[...])

        @pl.when(pl.program_id(0) == (3 * n_tiles) // 4)
        def _():
            gather.forward()

        @pl.when(pl.program_id(0) == n_tiles - 1)
        def _():
            gather.finish()

    hbm = pl.BlockSpec(memory_space=pl.ANY)
    return pl.pallas_call(
        body, name="inproj_fwd", grid=(n_tiles,),
        out_shape=(jax.ShapeDtypeStruct((seq, N_PROJ), jnp.float32), jax.ShapeDtypeStruct((seq, D_MODEL), MXU_DTYPE))
        + _exchange_out_shapes(riders, True),
        in_specs=[pl.BlockSpec((tm, D_MODEL), lambda i: (i, 0)), _const_spec(vecs.shape), _const_spec(w_in_p.shape)]
        + [hbm] * n_ride,
        out_specs=(pl.BlockSpec((tm, N_PROJ), lambda i: (i, 0)), pl.BlockSpec((tm, D_MODEL), lambda i: (i, 0)))
        + (hbm,) * n_ride,
        scratch_shapes=_gather_sems(n_ride),
        compiler_params=_params(("arbitrary",)),
    )(x2, vecs, w_in_p, *riders)


def _inproj_bwd(dproj, x2, dxa, vecs, w_in_pt, tm, riders):
    seq = x2.shape[0]
    n_tiles = seq // tm
    n_ride = len(riders)

    def body(*refs):
        dp_ref, x_ref, dxa_ref, vec_ref, w_ref = refs[:5]
        ride_in, refs = refs[5:5 + n_ride], refs[5 + n_ride:]
        gx_ref, sums_ref = refs[:2]
        ride_out, sems = refs[2:2 + n_ride], refs[2 + n_ride:]
        exchange = _ChipScatter(ride_in, ride_out, sems)

        @pl.when(pl.program_id(0) == 0)
        def _():
            exchange.start()
            sums_ref[...] = jnp.zeros_like(sums_ref)

        du = _mm(dp_ref[...], w_ref[...])
        xh, rstd = _ln(x_ref[...])
        sums_ref[0:1, :] += _colsum(du)
        sums_ref[1:2, :] += _colsum(du * xh)
        gx_ref[...] = dxa_ref[...] + _ln_bwd(du * (1.0 + vec_ref[1:2, :]), xh, rstd)

        @pl.when(pl.program_id(0) == n_tiles - 1)
        def _():
            exchange.wait()

    tile = pl.BlockSpec((tm, D_MODEL), lambda i: (i, 0))
    hbm = pl.BlockSpec(memory_space=pl.ANY)
    return pl.pallas_call(
        body, name="inproj_bwd", grid=(n_tiles,),
        out_shape=(jax.ShapeDtypeStruct((seq, D_MODEL), jnp.float32), jax.ShapeDtypeStruct((8, D_MODEL), jnp.float32))
        + _exchange_out_shapes(riders, False),
        in_specs=[pl.BlockSpec((tm, N_PROJ), lambda i: (i, 0)), tile, tile, _const_spec(vecs.shape),
                  _const_spec(w_in_pt.shape)] + [hbm] * n_ride,
        out_specs=(tile, pl.BlockSpec((8, D_MODEL), lambda i: (0, 0))) + (hbm,) * n_ride,
        scratch_shapes=_scatter_sems(n_ride),
        compiler_params=_params(("arbitrary",)),
    )(dproj, x2, dxa, vecs, w_in_pt, *riders)


def _head(h):
    return slice(h * HEAD_W, (h + 1) * HEAD_W)


def _cols(ref, off, h):
    return ref[:, off + h * HEAD_W:off + (h + 1) * HEAD_W]


HEADS = range(N_HEADS)


def _mixer_chunk_forward(p_ref, cc, ss, dm_ref, qdec_ref, kdec_ref, wg_ref, bg_ref, states):
    row, col = _tri_masks()
    lower = row >= col
    f = {}
    f["glr"] = p_ref[:, OFF_LR:OFF_LR + HEAD_W]
    f["logit"] = _mm(f["glr"], wg_ref[...]) + bg_ref[...]
    rq = [_cols(p_ref, OFF_RQ, h) for h in HEADS]
    rk = [_cols(p_ref, OFF_RK, h) for h in HEADS]
    f["rv"] = [_cols(p_ref, OFF_RV, h) for h in HEADS]
    f["qr"] = [(rq[h] * cc + _swap_halves(rq[h]) * ss) * RET_SCALE for h in HEADS]
    f["kr"] = [rk[h] * cc + _swap_halves(rk[h]) * ss for h in HEADS]
    s_raw = [_mm_nt(f["qr"][h], f["kr"][h]) for h in HEADS]
    yield
    la = _log_sigmoid(f["logit"]) * (1.0 / GATE_TAU)
    b = _running_sum(lower, la)
    f["qd"] = [f["qr"][h] * qdec_ref[:, _head(h)] for h in HEADS]
    f["kd"] = [f["kr"][h] * kdec_ref[:, _head(h)] for h in HEADS]
    f["scores"] = [s_raw[h] * dm_ref[h] for h in HEADS]
    yield
    b_last = b[CHUNK - 1:CHUNK, :]
    b_mid = b[CHUNK // 2 - 1:CHUNK // 2, :]
    f["e"], f["ei"] = jnp.exp(b - b_mid), jnp.exp(b_mid - b)
    f["eb"], f["ek"], f["ebl"] = jnp.exp(b), jnp.exp(b_last - b), jnp.exp(b_last)
    gq = [_cols(p_ref, OFF_GQ, h) * GLA_SCALE for h in HEADS]
    gk = [_cols(p_ref, OFF_GK, h) for h in HEADS]
    f["gv"] = [_cols(p_ref, OFF_GV, h) for h in HEADS]
    f["q_e"] = [gq[h] * f["e"][:, _head(h)] for h in HEADS]
    f["q_i"] = [gq[h] * f["ei"][:, _head(h)] for h in HEADS]
    f["k_e"] = [gk[h] * f["e"][:, _head(h)] for h in HEADS]
    f["k_i"] = [gk[h] * f["ei"][:, _head(h)] for h in HEADS]
    low = [_mm_nt(f["q_e"][h], f["k_i"][h]) for h in HEADS]
    up = [_mm_nt(f["q_i"][h], f["k_e"][h]) for h in HEADS]
    yield
    f["att"] = [jnp.where(lower, low[h], up[h]) for h in HEADS]
    f["qb"] = [gq[h] * f["eb"][:, _head(h)] for h in HEADS]
    f["kb"] = [gk[h] * f["ek"][:, _head(h)] for h in HEADS]
    ret_state, gla_state_t = states()
    f["o_ret"] = [_mm(f["scores"][h], f["rv"][h]) + _mm(f["qd"][h], ret_state[h]) for h in HEADS]
    f["o_gla"] = [_mm(f["att"][h], f["gv"][h]) + _mm_nt(f["qb"][h], gla_state_t[h]) for h in HEADS]
    return f


def _interleave(generators):
    live = list(generators)
    while live:
        for g in list(live):
            try:
                next(g)
            except StopIteration:
                live.remove(g)


def _mixer_fwd(proj, tables, wg_p, bg_p, ret_norm_w, gla_norm_w, riders):
    seq = proj.shape[0]
    n_chunks = seq // CHUNK
    per_step = min(n_chunks, CHUNKS_PER_STEP)
    n_steps = n_chunks // per_step
    n_ride = len(riders)
    rot_a, rot_b, dm_t, qdec_t, kdec_t, chunk_decay = tables

    def body(*refs):
        p_ref, ra_ref, rb_ref, dm_ref, qdec_ref, kdec_ref, wg_ref, bg_ref, wr_ref, wl_ref = refs[:10]
        ride_in, refs = refs[10:10 + n_ride], refs[10 + n_ride:]
        mix_ref, rsave_ref, ssave_ref = refs[:3]
        ride_out, refs = refs[3:3 + n_ride], refs[3 + n_ride:]
        r_sc, s_sc = refs[:2]
        gather = _ChipGather(ride_in, ride_out, refs[2:])

        @pl.when(pl.program_id(0) == 0)
        def _():
            gather.start()
            r_sc[...] = jnp.zeros_like(r_sc)
            s_sc[...] = jnp.zeros_like(s_sc)

        def one_chunk(c):
            p_c = p_ref.at[c * CHUNK:(c + 1) * CHUNK, :]
            mix_c = mix_ref.at[c * CHUNK:(c + 1) * CHUNK, :]
            before = {}

            def states():
                before["ret"] = [r_sc[h] for h in HEADS]
                before["gla"] = [s_sc[h] for h in HEADS]
                for h in HEADS:
                    rsave_ref[c, h] = before["ret"][h].astype(rsave_ref.dtype)
                    ssave_ref[c, h] = before["gla"][h]
                return before["ret"], before["gla"]

            cc, ss = _rotary_chunk(ra_ref, c, rb_ref)
            f = yield from _mixer_chunk_forward(p_c, cc, ss, dm_ref, qdec_ref, kdec_ref, wg_ref, bg_ref, states)
            for h in HEADS:
                r_sc[h] = chunk_decay[h] * before["ret"][h] + _mm_tn(f["kd"][h], f["rv"][h])
            for h in HEADS:
                s_sc[h] = before["gla"][h] * f["ebl"][:, _head(h)] + _mm_tn(f["gv"][h], f["kb"][h])
            yield
            for h in HEADS:
                on, _ = _ln(f["o_ret"][h])
                g = _cols(p_c, OFF_RG, h)
                mix_c[:, _head(h)] = (on * wr_ref[:, _head(h)] * (g * _sigmoid(g))).astype(mix_ref.dtype)
            for h in HEADS:
                o = f["o_gla"][h]
                on = o * lax.rsqrt(_rowmean(o * o) + LN_EPS)
                g = _cols(p_c, OFF_GG, h)
                mix_c[:, _head(N_HEADS + h)] = (on * wl_ref[:, _head(h)] * (g * _sigmoid(g))).astype(mix_ref.dtype)

        for c0 in range(0, per_step, CHUNKS_IN_LOCKSTEP):
            _interleave([one_chunk(c) for c in range(c0, min(per_step, c0 + CHUNKS_IN_LOCKSTEP))])

        @pl.when(pl.program_id(0) == (3 * n_steps) // 4)
        def _():
            gather.forward()

        @pl.when(pl.program_id(0) == n_steps - 1)
        def _():
            gather.finish()

    state_shape = (n_chunks, N_HEADS, HEAD_W, HEAD_W)
    state_blk = pl.BlockSpec((per_step, N_HEADS, HEAD_W, HEAD_W), lambda i: (i, 0, 0, 0))
    rot_blk = pl.BlockSpec((per_step, 8, HEAD_W), lambda i: (i, 0, 0))
    rows = per_step * CHUNK
    hbm = pl.BlockSpec(memory_space=pl.ANY)
    return pl.pallas_call(
        body, name="mixer_fwd", grid=(n_steps,),
        out_shape=(jax.ShapeDtypeStruct((seq, D_MODEL), MXU_DTYPE),
                   jax.ShapeDtypeStruct(state_shape, MXU_DTYPE), jax.ShapeDtypeStruct(state_shape, jnp.float32))
        + _exchange_out_shapes(riders, True),
        in_specs=[pl.BlockSpec((rows, N_PROJ), lambda i: (i, 0)), rot_blk, _const_spec(rot_b.shape),
                  _const_spec(dm_t.shape), _const_spec(qdec_t.shape), _const_spec(kdec_t.shape),
                  _const_spec(wg_p.shape), _const_spec(bg_p.shape), _const_spec(ret_norm_w.shape),
                  _const_spec(gla_norm_w.shape)] + [hbm] * n_ride,
        out_specs=(pl.BlockSpec((rows, D_MODEL), lambda i: (i, 0)), state_blk, state_blk) + (hbm,) * n_ride,
        scratch_shapes=[pltpu.VMEM((N_HEADS, HEAD_W, HEAD_W), jnp.float32),
                        pltpu.VMEM((N_HEADS, HEAD_W, HEAD_W), jnp.float32)] + _gather_sems(n_ride),
        compiler_params=_params(("arbitrary",)),
    )(proj, rot_a, rot_b, dm_t, qdec_t, kdec_t, wg_p, bg_p, ret_norm_w, gla_norm_w, *riders)


def _mixer_bwd(proj, dmixed, rsave, ssave, tables, wg_p, bg_p, ret_norm_w, gla_norm_w, riders):
    seq = proj.shape[0]
    n_chunks = seq // CHUNK
    per_step = min(n_chunks, CHUNKS_PER_STEP)
    n_steps = n_chunks // per_step
    n_ride = len(riders)
    rot_a, rot_b, dm_t, qdec_t, kdec_t, chunk_decay = tables
    last = n_steps - 1

    def body(*refs):
        p_blk, dmx_blk = refs[:2]
        shared_in = refs[2:13]
        ride_in, refs = refs[13:13 + n_ride], refs[13 + n_ride:]
        dp_blk, dwr_ref, dwl_ref, dwg_ref, dbg_ref = refs[:5]
        ride_out, refs = refs[5:5 + n_ride], refs[5 + n_ride:]
        dr_sc, ds_sc = refs[:2]
        exchange = _ChipScatter(ride_in, ride_out, refs[2:])

        @pl.when(pl.program_id(0) == 0)
        def _():
            exchange.start()
            dr_sc[...] = jnp.zeros_like(dr_sc)
            ds_sc[...] = jnp.zeros_like(ds_sc)
            dwr_ref[...] = jnp.zeros_like(dwr_ref)
            dwl_ref[...] = jnp.zeros_like(dwl_ref)
            dwg_ref[...] = jnp.zeros_like(dwg_ref)
            dbg_ref[...] = jnp.zeros_like(dbg_ref)

        def chunk_stages(c):
            rows = slice(c * CHUNK, (c + 1) * CHUNK)
            return one_chunk(c, p_blk.at[rows, :], dmx_blk.at[rows, :], dp_blk.at[rows, :], *shared_in,
                             dwr_ref, dwl_ref, dwg_ref, dbg_ref, dr_sc, ds_sc)

        for c0 in range(per_step, 0, -CHUNKS_IN_LOCKSTEP):
            _interleave([chunk_stages(c) for c in reversed(range(max(0, c0 - CHUNKS_IN_LOCKSTEP), c0))])

        @pl.when(pl.program_id(0) == last)
        def _():
            exchange.wait()

    def one_chunk(c, p_ref, dmx_ref, dp_ref, rsave_ref, ssave_ref, ra_ref, rb_ref, dm_ref, qdec_ref, kdec_ref,
                  wg_ref, bg_ref, wr_ref, wl_ref, dwr_ref, dwl_ref, dwg_ref, dbg_ref, dr_sc, ds_sc):
        def put(off, h, val):
            dp_ref[:, off + h * HEAD_W:off + (h + 1) * HEAD_W] = val.astype(dp_ref.dtype)

        cc, ss = _rotary_chunk(ra_ref, c, rb_ref)
        row, col = _tri_masks()
        ret_state = [rsave_ref[c, h] for h in HEADS]
        gla_state_t = [ssave_ref[c, h] for h in HEADS]
        f = yield from _mixer_chunk_forward(p_ref, cc, ss, dm_ref, qdec_ref, kdec_ref, wg_ref, bg_ref,
                                            lambda: (ret_state, gla_state_t))
        yield

        do_ret, do_gla = [], []
        for h in HEADS:
            on, rstd = _ln(f["o_ret"][h])
            g = _cols(p_ref, OFF_RG, h)
            sg = _sigmoid(g)
            dy = dmx_ref[:, _head(h)].astype(jnp.float32)
            wr = wr_ref[:, _head(h)]
            dwr_ref[:, _head(h)] += _colsum(dy * on * (g * sg))
            put(OFF_RG, h, dy * on * wr * (sg * (1.0 + g * (1.0 - sg))))
            do_ret.append(_ln_bwd(dy * wr * (g * sg), on, rstd))
        for h in HEADS:
            o = f["o_gla"][h]
            rstd = lax.rsqrt(_rowmean(o * o) + LN_EPS)
            on = o * rstd
            g = _cols(p_ref, OFF_GG, h)
            sg = _sigmoid(g)
            dy = dmx_ref[:, _head(N_HEADS + h)].astype(jnp.float32)
            wl = wl_ref[:, _head(h)]
            dwl_ref[:, _head(h)] += _colsum(dy * on * (g * sg))
            put(OFF_GG, h, dy * on * wl * (sg * (1.0 + g * (1.0 - sg))))
            don = dy * wl * (g * sg)
            do_gla.append(rstd * (don - on * _rowmean(don * on)))

        yield

        d_ret_new = [dr_sc[h] for h in HEADS]
        d_gla_new = [ds_sc[h] for h in HEADS]
        ds_raw = [_mm_nt(do_ret[h], f["rv"][h]) * dm_ref[h] for h in HEADS]
        d_att = [_mm_nt(do_gla[h], f["gv"][h]) for h in HEADS]
        dq_state = [_mm_nt(do_ret[h], ret_state[h]) for h in HEADS]
        dk_state = [_mm_nt(f["rv"][h], d_ret_new[h]) for h in HEADS]
        dqb = [_mm(do_gla[h], gla_state_t[h]) for h in HEADS]
        dkb = [_mm(f["gv"][h], d_gla_new[h]) for h in HEADS]
        for h in HEADS:
            put(OFF_RV, h, _mm_tn(f["scores"][h], do_ret[h]) + _mm(f["kd"][h], d_ret_new[h]))
        for h in HEADS:
            put(OFF_GV, h, _mm_tn(f["att"][h], do_gla[h]) + _mm_nt(f["kb"][h], d_gla_new[h]))
        for h in HEADS:
            dr_sc[h] = chunk_decay[h] * d_ret_new[h] + _mm_tn(f["qd"][h], do_ret[h])
        for h in HEADS:
            ds_sc[h] = d_gla_new[h] * f["ebl"][:, _head(h)] + _mm_tn(do_gla[h], f["qb"][h])
        yield

        dqr = [_mm(ds_raw[h], f["kr"][h]) + dq_state[h] * qdec_ref[:, _head(h)] for h in HEADS]
        dkr = [_mm_tn(ds_raw[h], f["qr"][h]) + dk_state[h] * kdec_ref[:, _head(h)] for h in HEADS]
        d_low = [jnp.where(row >= col, d_att[h], 0.0) for h in HEADS]
        d_up = [jnp.where(row < col, d_att[h], 0.0) for h in HEADS]
        dq_e = [_mm(d_low[h], f["k_i"][h]) for h in HEADS]
        dk_i = [_mm_tn(d_low[h], f["q_e"][h]) for h in HEADS]
        dq_i = [_mm(d_up[h], f["k_e"][h]) for h in HEADS]
        dk_e = [_mm_tn(d_up[h], f["q_i"][h]) for h in HEADS]
        yield
        for h in HEADS:
            put(OFF_RQ, h, (dqr[h] * cc + _swap_halves(dqr[h] * ss)) * RET_SCALE)
            put(OFF_RK, h, dkr[h] * cc + _swap_halves(dkr[h] * ss))
        row_id = lax.broadcasted_iota(jnp.int32, (CHUNK, HEAD_W), 0)
        db_heads = []
        for h in HEADS:
            hs = _head(h)
            e, ei, eb, ek, ebl = f["e"][:, hs], f["ei"][:, hs], f["eb"][:, hs], f["ek"][:, hs], f["ebl"][:, hs]
            put(OFF_GQ, h, (dq_e[h] * e + dq_i[h] * ei + dqb[h] * eb) * GLA_SCALE)
            put(OFF_GK, h, dk_e[h] * e + dk_i[h] * ei + dkb[h] * ek)
            db = (dq_e[h] * f["q_e"][h] - dq_i[h] * f["q_i"][h] + dk_e[h] * f["k_e"][h] - dk_i[h] * f["k_i"][h]
                  + dqb[h] * f["qb"][h] - dkb[h] * f["kb"][h])
            db_last = _colsum(dkb[h] * f["kb"][h]) + ebl * _colsum(gla_state_t[h] * d_gla_new[h])
            db_heads.append(db + jnp.where(row_id == CHUNK - 1, db_last, 0.0))
        db = jnp.concatenate(db_heads, axis=1)
        d_la = _running_sum(col >= row, db)
        d_logit = d_la * (1.0 / GATE_TAU) * (1.0 - _sigmoid(f["logit"]))
        put(OFF_LR, 0, _mm_nt(d_logit, wg_ref[...]))
        dwg_ref[...] += _mm_tn(f["glr"], d_logit)
        dbg_ref[...] += _colsum(d_logit)

    state_blk = pl.BlockSpec((per_step, N_HEADS, HEAD_W, HEAD_W), lambda i: (last - i, 0, 0, 0))
    rot_blk = pl.BlockSpec((per_step, 8, HEAD_W), lambda i: (last - i, 0, 0))
    width = N_HEADS * HEAD_W
    vec_out = pl.BlockSpec((1, width), lambda i: (0, 0))
    hbm = pl.BlockSpec(memory_space=pl.ANY)
    rows_blk = per_step * CHUNK
    return pl.pallas_call(
        body, name="mixer_bwd", grid=(n_steps,),
        out_shape=(jax.ShapeDtypeStruct((seq, N_PROJ), MXU_DTYPE),
                   jax.ShapeDtypeStruct((1, width), jnp.float32), jax.ShapeDtypeStruct((1, width), jnp.float32),
                   jax.ShapeDtypeStruct((HEAD_W, width), jnp.float32), jax.ShapeDtypeStruct((1, width), jnp.float32))
        + _exchange_out_shapes(riders, False),
        in_specs=[pl.BlockSpec((rows_blk, N_PROJ), lambda i: (last - i, 0)),
                  pl.BlockSpec((rows_blk, D_MODEL), lambda i: (last - i, 0)), state_blk, state_blk, rot_blk,
                  _const_spec(rot_b.shape),
                  _const_spec(dm_t.shape), _const_spec(qdec_t.shape), _const_spec(kdec_t.shape),
                  _const_spec(wg_p.shape), _const_spec(bg_p.shape), _const_spec(ret_norm_w.shape),
                  _const_spec(gla_norm_w.shape)] + [hbm] * n_ride,
        out_specs=(pl.BlockSpec((rows_blk, N_PROJ), lambda i: (last - i, 0)), vec_out, vec_out,
                   pl.BlockSpec((HEAD_W, width), lambda i: (0, 0)), vec_out) + (hbm,) * n_ride,
        scratch_shapes=[pltpu.VMEM((N_HEADS, HEAD_W, HEAD_W), jnp.float32),
                        pltpu.VMEM((N_HEADS, HEAD_W, HEAD_W), jnp.float32)] + _scatter_sems(n_ride),
        compiler_params=_params(("arbitrary",)),
    )(proj, dmixed, rsave, ssave, rot_a, rot_b, dm_t, qdec_t, kdec_t, wg_p, bg_p, ret_norm_w, gla_norm_w, *riders)


V_GATE1, V_SCALE2, V_SHIFT2, V_GATE2, V_LN1W, V_LN1B, V_LN2W, V_LN2B = range(8)
S_GATE1, S_SCALE2, S_SHIFT2, S_GATE2, S_LN1W, S_LN1B, S_LN2W, S_LN2B, S_LOSS = range(9)


def _mlp_fwd_bwd(x2, mixed, target, vecs, w_out, w1_chunks, w2_chunks, tm):
    seq = x2.shape[0]
    n_fc, _, fc = w1_chunks.shape

    def body(x_ref, mx_ref, t_ref, vec_ref, wo_ref, w1_ref, w2_ref,
             dmx_ref, dxa_ref, a_ref, dh_ref, u2_ref, df_ref, dm_ref, sums_ref, relu_sc):
        @pl.when(pl.program_id(0) == 0)
        def _():
            sums_ref[...] = jnp.zeros_like(sums_ref)

        vec = lambda r: vec_ref[r:r + 1, :]

        def acc(r, val):
            sums_ref[r:r + 1, :] += _colsum(val)

        xx = x_ref[...]
        m = _mm(mx_ref[...], wo_ref[...])
        z1h, rstd1 = _ln(ALPHA * xx + vec(V_GATE1) * m)
        x1 = z1h * vec(V_LN1W) + vec(V_LN1B)
        x1h, rstd0 = _ln(x1)
        u2 = (x1h * (1.0 + vec(V_SCALE2)) + vec(V_SHIFT2)).astype(MXU_DTYPE)
        u2_ref[...] = u2
        f = jnp.zeros((tm, D_MODEL), jnp.float32)
        for j in range(n_fc):
            r = jnp.maximum(_mm(u2, w1_ref[j]), 0.0)
            relu_sc[:, j * fc:(j + 1) * fc] = r
            a = (r * r).astype(MXU_DTYPE)
            a_ref[:, j * fc:(j + 1) * fc] = a
            f = f + _mm(a, w2_ref[j])
        z2h, rstd2 = _ln(ALPHA * x1 + vec(V_GATE2) * f)
        err = z2h * vec(V_LN2W) + vec(V_LN2B) - t_ref[...]
        acc(S_LOSS, err * err)
        dy = err * (1.0 / D_MODEL)
        acc(S_LN2W, dy * z2h)
        acc(S_LN2B, dy)
        dz2 = _ln_bwd(dy * vec(V_LN2W), z2h, rstd2)
        acc(S_GATE2, dz2 * f)
        df = (vec(V_GATE2) * dz2).astype(MXU_DTYPE)
        df_ref[...] = df
        du2 = jnp.zeros((tm, D_MODEL), jnp.float32)
        for j in range(n_fc):
            dh = (_mm_nt(df, w2_ref[j]) * (2.0 * relu_sc[:, j * fc:(j + 1) * fc])).astype(MXU_DTYPE)
            dh_ref[:, j * fc:(j + 1) * fc] = dh
            du2 = du2 + _mm_nt(dh, w1_ref[j])
        acc(S_SCALE2, du2 * x1h)
        acc(S_SHIFT2, du2)
        dx1 = ALPHA * dz2 + _ln_bwd(du2 * (1.0 + vec(V_SCALE2)), x1h, rstd0)
        acc(S_LN1W, dx1 * z1h)
        acc(S_LN1B, dx1)
        dz1 = _ln_bwd(dx1 * vec(V_LN1W), z1h, rstd1)
        acc(S_GATE1, dz1 * m)
        dxa_ref[...] = ALPHA * dz1
        dm = (vec(V_GATE1) * dz1).astype(MXU_DTYPE)
        dm_ref[...] = dm
        dmx_ref[...] = _mm_nt(dm, wo_ref[...])

    tile = lambda width: pl.BlockSpec((tm, width), lambda i: (i, 0))
    f32 = lambda width: jax.ShapeDtypeStruct((seq, width), jnp.float32)
    b16 = lambda width: jax.ShapeDtypeStruct((seq, width), MXU_DTYPE)
    return pl.pallas_call(
        body, name="mlp_fwd_bwd", grid=(seq // tm,),
        out_shape=(f32(D_MODEL), f32(D_MODEL), b16(D_FF), b16(D_FF), b16(D_MODEL), b16(D_MODEL), b16(D_MODEL),
                   jax.ShapeDtypeStruct((16, D_MODEL), jnp.float32)),
        in_specs=[tile(D_MODEL), tile(D_MODEL), tile(D_MODEL), _const_spec(vecs.shape), _const_spec(w_out.shape),
                  _const_spec(w1_chunks.shape), _const_spec(w2_chunks.shape)],
        out_specs=(tile(D_MODEL), tile(D_MODEL), tile(D_FF), tile(D_FF), tile(D_MODEL), tile(D_MODEL),
                   tile(D_MODEL), pl.BlockSpec((16, D_MODEL), lambda i: (0, 0))),
        scratch_shapes=[pltpu.VMEM((tm, D_FF), jnp.float32)],
        compiler_params=_params(("arbitrary",)),
    )(x2, mixed, target, vecs, w_out, w1_chunks, w2_chunks)


def _grad_matmul(a, b, name, tn, blocks_are_rows, riders=()):
    seq, m_dim = a.shape
    n_dim = b.shape[1]
    tk = min(seq, GRAD_TOKEN_TILE)
    nk = seq // tk
    n_ride = len(riders)
    if blocks_are_rows:
        tm = m_dim // N_CHIP
        assert tn == n_dim
        grid = (N_CHIP, 1, nk)
        out_map = lambda i, j, k: (i, 0, 0)
    else:
        tm = m_dim
        assert tn * N_CHIP == n_dim
        grid = (1, N_CHIP, nk)
        out_map = lambda i, j, k: (j, 0, 0)

    def body(*refs):
        a_ref, b_ref = refs[:2]
        ride_in, refs = refs[2:2 + n_ride], refs[2 + n_ride:]
        o_ref = refs[0]
        ride_out, refs = refs[1:1 + n_ride], refs[1 + n_ride:]
        acc_sc = refs[0]
        exchange = _ChipScatter(ride_in, ride_out, refs[1:]) if n_ride else None
        block = pl.program_id(0) + pl.program_id(1)
        k = pl.program_id(2)

        if exchange is not None:
            @pl.when((block == 0) & (k == 0))
            def _():
                exchange.start()

        @pl.when(k == 0)
        def _():
            acc_sc[...] = jnp.zeros_like(acc_sc)

        acc_sc[...] += _mm_tn(a_ref[...], b_ref[...])

        @pl.when(k == nk - 1)
        def _():
            o_ref[0] = acc_sc[...].astype(o_ref.dtype)

        if exchange is not None:
            @pl.when((block == N_CHIP - 1) & (k == nk - 1))
            def _():
                exchange.wait()

    hbm = pl.BlockSpec(memory_space=pl.ANY)
    out = pl.pallas_call(
        body, name=name, grid=grid,
        out_shape=(jax.ShapeDtypeStruct((N_CHIP, tm, tn), WIRE_DTYPE),) + _exchange_out_shapes(riders, False),
        in_specs=[pl.BlockSpec((tk, tm), lambda i, j, k: (k, i)), pl.BlockSpec((tk, tn), lambda i, j, k: (k, j))]
        + [hbm] * n_ride,
        out_specs=(pl.BlockSpec((1, tm, tn), out_map),) + (hbm,) * n_ride,
        scratch_shapes=[pltpu.VMEM((tm, tn), jnp.float32)] + (_scatter_sems(n_ride) if n_ride else []),
        compiler_params=_params(("arbitrary", "arbitrary", "arbitrary")),
    )(a, b, *riders)
    return out if n_ride else out[0]


def _grad_matmul_full(a, b, name, tm, riders):
    seq, m_dim = a.shape
    n_dim = b.shape[1]
    tk = min(seq, GRAD_TOKEN_TILE)
    nk = seq // tk
    n_blocks = m_dim // tm
    n_ride = len(riders)
    assert m_dim % tm == 0

    def body(*refs):
        a_ref, b_ref = refs[:2]
        ride_in, refs = refs[2:2 + n_ride], refs[2 + n_ride:]
        o_ref = refs[0]
        ride_out, refs = refs[1:1 + n_ride], refs[1 + n_ride:]
        acc_sc = refs[0]
        swap = _SiblingSwap(ride_in, ride_out, refs[1:])
        i, k = pl.program_id(0), pl.program_id(1)

        @pl.when((i == 0) & (k == 0))
        def _():
            swap.start()

        @pl.when(k == 0)
        def _():
            acc_sc[...] = jnp.zeros_like(acc_sc)

        acc_sc[...] += _mm_tn(a_ref[...], b_ref[...])

        @pl.when(k == nk - 1)
        def _():
            o_ref[...] = acc_sc[...].astype(o_ref.dtype)

        @pl.when((i == n_blocks - 1) & (k == nk - 1))
        def _():
            swap.wait()

    hbm = pl.BlockSpec(memory_space=pl.ANY)
    return pl.pallas_call(
        body, name=name, grid=(n_blocks, nk),
        out_shape=(jax.ShapeDtypeStruct((m_dim, n_dim), WIRE_DTYPE),)
        + tuple(jax.ShapeDtypeStruct(r.shape, r.dtype) for r in riders),
        in_specs=[pl.BlockSpec((tk, tm), lambda i, k: (k, i)), pl.BlockSpec((tk, n_dim), lambda i, k: (k, 0))]
        + [hbm] * n_ride,
        out_specs=(pl.BlockSpec((tm, n_dim), lambda i, k: (i, 0)),) + (hbm,) * n_ride,
        scratch_shapes=[pltpu.VMEM((tm, n_dim), jnp.float32)] + _swap_sems(n_ride),
        compiler_params=_params(("arbitrary", "arbitrary")),
    )(a, b, *riders)


def _adam_pair(w, g_mine, g_sibling, m, v, name):
    rows, cols = w.shape
    tc = min(cols, ELEMENTWISE_COLS)

    def total(ref):
        if len(ref.shape) == 2:
            return ref[...]
        acc = ref[0].astype(jnp.float32)
        for j in range(1, ref.shape[0]):
            acc = acc + ref[j].astype(jnp.float32)
        return acc

    def body(w_ref, ga_ref, gb_ref, m_ref, v_ref, g_ref, dl_ref, m2_ref, v2_ref):
        g = total(ga_ref) + total(gb_ref)
        delta, m2, v2 = _adam(w_ref[...], g, m_ref[...], v_ref[...])
        g_ref[...] = g
        dl_ref[...] = delta
        m2_ref[...] = m2
        v2_ref[...] = v2

    blk = pl.BlockSpec((rows, tc), lambda i: (0, i))
    g_blk = lambda a: blk if a.ndim == 2 else pl.BlockSpec((a.shape[0], rows, tc), lambda i: (0, 0, i))
    out = jax.ShapeDtypeStruct((rows, cols), jnp.float32)
    return pl.pallas_call(
        body, name=name, grid=(cols // tc,),
        out_shape=(out, out, out, out),
        in_specs=[blk, g_blk(g_mine), g_blk(g_sibling), blk, blk], out_specs=(blk,) * 4,
        compiler_params=_params(("arbitrary",)),
    )(w, g_mine, g_sibling, m, v)


def _sum_devices(gathered):
    _, rows, _ = gathered.shape

    def body(g_ref, o_ref):
        total = g_ref[0]
        for d in range(1, N_DEV):
            total = total + g_ref[d]
        o_ref[...] = total

    return pl.pallas_call(
        body, name="sum_devices",
        out_shape=jax.ShapeDtypeStruct((rows, 128), jnp.float32),
    )(gathered)


def _adam_small(params):
    n = len(params)

    def body(*refs):
        ins, outs = refs[:4 * n], refs[4 * n:]
        for i in range(n):
            w_ref, g_ref, m_ref, v_ref = ins[4 * i:4 * i + 4]
            delta, m2, v2 = _adam(w_ref[...], g_ref[...], m_ref[...], v_ref[...])
            outs[3 * i][...] = delta
            outs[3 * i + 1][...] = m2
            outs[3 * i + 2][...] = v2

    out_shape = tuple(jax.ShapeDtypeStruct(p[0].shape, jnp.float32) for p in params for _ in range(3))
    out = pl.pallas_call(body, name="adam_small", out_shape=out_shape)(*[t for p in params for t in p])
    return [out[3 * i:3 * i + 3] for i in range(n)]


def _pad_heads(w):
    lead = w.shape[:-1]
    w = w.reshape(lead + (N_HEADS, GLA_DK))
    w = jnp.pad(w, [(0, 0)] * len(lead) + [(0, 0), (0, HEAD_W - GLA_DK)])
    return w.reshape(lead + (N_HEADS * HEAD_W,))


def _unpad_heads(w):
    lead = w.shape[:-1]
    return w.reshape(lead + (N_HEADS, HEAD_W))[..., :GLA_DK].reshape(lead + (N_HEADS * GLA_DK,))


def _pad_head_rows(w):
    w = w.reshape(N_HEADS, GLA_DK, w.shape[-1])
    return jnp.pad(w, ((0, 0), (0, HEAD_W - GLA_DK), (0, 0))).reshape(N_HEADS * HEAD_W, w.shape[-1])


def _unpad_head_rows(w):
    return w.reshape(N_HEADS, HEAD_W, w.shape[-1])[:, :GLA_DK].reshape(N_HEADS * GLA_DK, w.shape[-1])


def _pad_w_in_rows(stack):
    w = stack.reshape(-1, stack.shape[-1])
    return jnp.concatenate([
        w[:2048], _pad_head_rows(w[2048:2304]), _pad_head_rows(w[2304:2560]), w[2560:3584],
        jnp.pad(w[3584:3600], ((0, HEAD_W - GATE_RANK), (0, 0)))], axis=0)


def _unpad_w_in_stack(g, per):
    segments = [(0, g[:2048]), (2048, _unpad_head_rows(g[OFF_GQ:OFF_GQ + 512])),
                (2304, _unpad_head_rows(g[OFF_GK:OFF_GK + 512])), (2560, g[OFF_GV:OFF_LR]),
                (3584, g[OFF_LR:OFF_LR + GATE_RANK])]
    blocks = []
    for j in range(N_CHIP):
        lo, hi = j * per, (j + 1) * per
        pieces = []
        for start, rows in segments:
            a, b = max(lo, start), min(hi, start + rows.shape[0])
            if a < b:
                pieces.append(rows[a - start:b - start])
        blocks.append(jnp.concatenate(pieces, axis=0))
    return jnp.stack(blocks)


def _col_major(w):
    return jnp.transpose(w, (2, 0, 1)).reshape(w.shape[2], w.shape[1])


def _rows128(a):
    return a.reshape(-1, 128)


def _rows8(a):
    a = a.reshape(-1, 128)
    return jnp.pad(a, ((0, -a.shape[0] % 8), (0, 0)))


def kernel(x, c, w_ada, b_ada, w_in, ret_norm_w, gla_gate_w, gla_gate_b, gla_norm_w, w_out, ln1_w, ln1_b, w_ff1, w_ff2, ln2_w, ln2_b, loss_target, m_w_ada, m_b_ada, m_w_in, m_ret_norm_w, m_gla_gate_w, m_gla_gate_b, m_gla_norm_w, m_w_out, m_ln1_w, m_ln1_b, m_w_ff1, m_w_ff2, m_ln2_w, m_ln2_b, v_w_ada, v_b_ada, v_w_in, v_ret_norm_w, v_gla_gate_w, v_gla_gate_b, v_gla_norm_w, v_w_out, v_ln1_w, v_ln1_b, v_w_ff1, v_w_ff2, v_ln2_w, v_ln2_b):
    seq = x.shape[1]
    tm = min(seq, TOKEN_TILE)
    tm_in = min(seq, INPROJ_TOKEN_TILE)
    xi, yi, ci = _mesh_pos()
    dev = 4 * xi + 2 * yi + ci
    chip = 2 * xi + yi
    x2, target = x[0], loss_target[0]
    ada_cols = w_ada.shape[2]
    in_cols = w_in.shape[2]
    gate_cols = gla_gate_w.shape[2]

    b_blk = lax.dynamic_slice(b_ada, (0, chip * ada_cols), (1, ada_cols))
    g0, g1, w_in_stack = _prologue(jnp.concatenate([_rows128(c), _rows128(gla_gate_w[0])], axis=0), w_ada[0], b_blk,
                                   _col_major(w_in.astype(WIRE_DTYPE)))
    c_all = g0[:, :8].reshape(N_DEV, D_MODEL)
    gate_w_full = jnp.concatenate([g0[2 * j, 8:16].reshape(GATE_RANK, gate_cols) for j in range(N_CHIP)], axis=1)
    wg_p = jnp.pad(_pad_heads(gate_w_full), ((0, HEAD_W - GATE_RANK), (0, 0)))
    bg_p = _pad_heads(gla_gate_b)
    mine = lax.dynamic_index_in_dim(g1, dev, axis=2, keepdims=False)
    mod = jnp.concatenate([mine[2 * j].reshape(1, ada_cols) for j in range(N_CHIP)], axis=1)
    shift1, scale1, gate1, shift2, scale2, gate2 = [mod[:, i * D_MODEL:(i + 1) * D_MODEL] for i in range(6)]
    w_in_pt = _pad_w_in_rows(w_in_stack).astype(MXU_DTYPE)
    w_in_p = jnp.transpose(w_in_pt)

    zeros_row = jnp.zeros((1, D_MODEL), jnp.float32)
    vecs1 = jnp.concatenate([shift1, scale1] + [zeros_row] * 6, axis=0)
    proj, u, w2_stack = _inproj_fwd(x2, vecs1, w_in_p, tm_in, [w_ff2[0].astype(WIRE_DTYPE)])
    rot_a, rot_b = _rotary_tables(seq)
    dm_t, qdec_t, kdec_t, chunk_decay = _decay_tables()
    tables = (rot_a, rot_b, dm_t, qdec_t, kdec_t, chunk_decay)
    mixed, rsave, ssave, w_out_stack, w1_stack = _mixer_fwd(
        proj, tables, wg_p, bg_p, ret_norm_w, gla_norm_w,
        [w_out[0].astype(WIRE_DTYPE), w_ff1[0].astype(WIRE_DTYPE)])
    w_out_full = w_out_stack.reshape(D_MODEL, D_MODEL).astype(MXU_DTYPE)
    w1_chunks = w1_stack.astype(MXU_DTYPE)
    w2_chunks = w2_stack.astype(MXU_DTYPE)

    vecs2 = jnp.concatenate([gate1, scale2, shift2, gate2, ln1_w, ln1_b, ln2_w, ln2_b], axis=0)
    dmixed, dxa, act, dh, u2, df, dm, sums2 = _mlp_fwd_bwd(x2, mixed, target, vecs2, w_out_full, w1_chunks,
                                                           w2_chunks, tm)

    g_out_stack = _grad_matmul(mixed, dm, "grad_w_out", D_MODEL, True)
    g_ff1_stack, r_out = _grad_matmul(u2, dh, "grad_w_ff1", D_FF // N_CHIP, False, [g_out_stack])
    g_ff2_stack = _grad_matmul(act, df, "grad_w_ff2", D_MODEL, True)
    dproj, d_ret_norm, d_gla_norm, d_wg_p, d_bg_p, r_ff1, r_ff2 = _mixer_bwd(
        proj, dmixed, rsave, ssave, tables, wg_p, bg_p, ret_norm_w, gla_norm_w, [g_ff1_stack, g_ff2_stack])
    early = ["w_out", "w_ff1", "w_ff2"]
    partial = dict(zip(early, [r_out, r_ff1, r_ff2]))
    g_in_t, *swapped_early = _grad_matmul_full(dproj, u, "grad_w_in", N_PROJ // 3, [partial[n] for n in early])
    swapped = dict(zip(early, swapped_early))
    g_in_stack = _unpad_w_in_stack(g_in_t, in_cols)
    grad_x, sums1, r_in = _inproj_bwd(dproj, x2, dxa, vecs1, w_in_pt, tm_in, [g_in_stack])

    dmod = jnp.concatenate([sums1[0:1], sums1[1:2], sums2[S_GATE1:S_GATE1 + 1], sums2[S_SHIFT2:S_SHIFT2 + 1],
                            sums2[S_SCALE2:S_SCALE2 + 1], sums2[S_GATE2:S_GATE2 + 1]], axis=1)
    d_gate_w_full = _unpad_heads(d_wg_p[:GATE_RANK])
    flat = lambda parts: jnp.concatenate([_rows8(p) for p in parts], axis=0)
    small = flat([dmod, sums2[S_LN1W:S_LN1W + 1], sums2[S_LN1B:S_LN1B + 1], sums2[S_LN2W:S_LN2W + 1],
                  sums2[S_LN2B:S_LN2B + 1], d_ret_norm, _unpad_heads(d_bg_p), d_gla_norm, d_gate_w_full,
                  sums2[S_LOSS:S_LOSS + 1]])
    partial["w_in"] = r_in
    g2, swapped["w_in"] = _gather_rows(small, "gather_small", [r_in])
    tot = _sum_devices(g2)
    loss = 0.5 / D_MODEL * jnp.sum(tot[136:144])
    grad_b_ada = tot[0:48].reshape(1, 6 * D_MODEL)
    grad_ln1_w, grad_ln1_b = tot[48:56].reshape(1, D_MODEL), tot[56:64].reshape(1, D_MODEL)
    grad_ln2_w, grad_ln2_b = tot[64:72].reshape(1, D_MODEL), tot[72:80].reshape(1, D_MODEL)
    grad_ret_norm = tot[80:84].reshape(1, 512)
    grad_gate_b = tot[88:90].reshape(1, 256)
    grad_gla_norm = tot[96:100].reshape(1, 512)
    grad_gate_w = lax.dynamic_slice(tot[104:136].reshape(GATE_RANK, 256), (0, chip * gate_cols),
                                    (GATE_RANK, gate_cols))

    small_grads = [grad_b_ada, grad_ln1_w, grad_ln1_b, grad_ln2_w, grad_ln2_b, grad_ret_norm, grad_gate_b,
                   grad_gla_norm, grad_gate_w[None]]
    small_out = _adam_small(list(zip(
        [b_ada, ln1_w, ln1_b, ln2_w, ln2_b, ret_norm_w, gla_gate_b, gla_norm_w, gla_gate_w], small_grads,
        [m_b_ada, m_ln1_w, m_ln1_b, m_ln2_w, m_ln2_b, m_ret_norm_w, m_gla_gate_b, m_gla_norm_w, m_gla_gate_w],
        [v_b_ada, v_ln1_w, v_ln1_b, v_ln2_w, v_ln2_b, v_ret_norm_w, v_gla_gate_b, v_gla_norm_w, v_gla_gate_w])))
    sm_delta, sm_m, sm_v = [[o[k] for o in small_out] for k in range(3)]

    dmod_all = g2[:, 0:48].reshape(N_DEV, 6 * D_MODEL)
    dmod_blk = lax.dynamic_slice(dmod_all, (0, chip * ada_cols), (N_DEV, ada_cols))
    ada_out = _ada_bwd_adam(jnp.transpose(c_all), dmod_blk, w_ada[0], m_w_ada[0], v_w_ada[0])
    ada_g, ada_delta, ada_m, ada_v = [t[None] for t in ada_out]

    big = {}
    for n, w, m, v in zip(["w_in", "w_out", "w_ff1", "w_ff2"], [w_in, w_out, w_ff1, w_ff2],
                          [m_w_in, m_w_out, m_w_ff1, m_w_ff2], [v_w_in, v_w_out, v_w_ff1, v_w_ff2]):
        mine, theirs = partial[n], swapped[n]
        if n == "w_in":
            out = _adam_pair(_col_major(w), mine, theirs, _col_major(m), _col_major(v), "adam_" + n)
            big[n] = [jnp.transpose(t.reshape(t.shape[0], 1, t.shape[1]), (1, 2, 0)) for t in out]
        else:
            big[n] = [t[None] for t in _adam_pair(w[0], mine, theirs, m[0], v[0], "adam_" + n)]

    def assemble(ada, smalls, k):
        b_ada_o, ln1w_o, ln1b_o, ln2w_o, ln2b_o, ret_o, gb_o, gln_o, gw_o = smalls
        return [ada, b_ada_o, big["w_in"][k], ret_o, gw_o, gb_o, gln_o, big["w_out"][k], ln1w_o, ln1b_o,
                big["w_ff1"][k], big["w_ff2"][k], ln2w_o, ln2b_o]

    grads = assemble(ada_g, small_grads, 0)
    deltas = assemble(ada_delta, sm_delta, 1)
    new_m = assemble(ada_m, sm_m, 2)
    new_v = assemble(ada_v, sm_v, 3)
    return (loss, grad_x[None], *grads, *deltas, *new_m, *new_v)
```

```python
import numpy as np
import jax
import jax.numpy as jnp
from jax import lax
from jax.experimental import pallas as pl
from jax.experimental.pallas import tpu as pltpu

D_MODEL = 1024
D_FF = 4096
CHUNK = 64
N_HEADS = 4
HEAD_W = 128
GLA_DK = 64
GATE_RANK = 16
GATE_TAU = 16.0
LN_EPS = 1e-5
ALPHA = 2.0 ** 0.25
ROPE_BASE = 10000.0
RET_SCALE = float(HEAD_W) ** -0.5
GLA_SCALE = float(GLA_DK) ** -0.5

ADAM_LR = 0.001
ADAM_B1 = 0.9
ADAM_B2 = 0.999
ADAM_EPS = 1e-08
ADAM_WD = 0.01
ADAM_STEP = 10

OFF_RQ, OFF_RK, OFF_RV, OFF_RG = 0, 512, 1024, 1536
OFF_GQ, OFF_GK, OFF_GV, OFF_GG, OFF_LR = 2048, 2560, 3072, 3584, 4096
N_PROJ = 4224

N_DEV = 8
N_CHIP = 4
MESH = pl.DeviceIdType.MESH
MXU_DTYPE = jnp.bfloat16
WIRE_DTYPE = jnp.bfloat16
VMEM_LIMIT = 60 * 1024 * 1024
TOKEN_TILE = 256
INPROJ_TOKEN_TILE = 512
CHUNKS_PER_STEP = 8
CHUNKS_IN_LOCKSTEP = 4
GRAD_ROWS_PER_STEP = 1024
GRAD_TOKEN_TILE = 2048
ELEMENTWISE_COLS = 256
HIGHEST = lax.Precision.HIGHEST


def _mm(a, b):
    return jnp.dot(a.astype(MXU_DTYPE), b.astype(MXU_DTYPE), preferred_element_type=jnp.float32)


def _mm_nt(a, b):
    return lax.dot_general(a.astype(MXU_DTYPE), b.astype(MXU_DTYPE), (((1,), (1,)), ((), ())),
                           preferred_element_type=jnp.float32)


def _mm_tn(a, b):
    return lax.dot_general(a.astype(MXU_DTYPE), b.astype(MXU_DTYPE), (((0,), (0,)), ((), ())),
                           preferred_element_type=jnp.float32)


def _mm32(a, b):
    return jnp.dot(a, b, precision=HIGHEST, preferred_element_type=jnp.float32)


def _running_sum(mask, a):
    m = mask.astype(jnp.bfloat16)
    hi = a.astype(jnp.bfloat16)
    rest = a - hi.astype(jnp.float32)
    mid = rest.astype(jnp.bfloat16)
    lo = (rest - mid.astype(jnp.float32)).astype(jnp.bfloat16)
    dot = lambda t: jnp.dot(m, t, preferred_element_type=jnp.float32)
    return dot(hi) + dot(mid) + dot(lo)


def _rowmean(a):
    return jnp.mean(a, axis=-1, keepdims=True)


def _colsum(a):
    return jnp.sum(a, axis=0, keepdims=True)


def _ln(z):
    zc = z - _rowmean(z)
    rstd = lax.rsqrt(_rowmean(zc * zc) + LN_EPS)
    return zc * rstd, rstd


def _ln_bwd(dzh, zh, rstd):
    return rstd * (dzh - _rowmean(dzh) - zh * _rowmean(dzh * zh))


def _sigmoid(a):
    return 1.0 / (1.0 + jnp.exp(-a))


def _log_sigmoid(a):
    return jnp.minimum(a, 0.0) - jnp.log(1.0 + jnp.exp(-jnp.abs(a)))


def _swap_halves(a):
    return pltpu.roll(a, HEAD_W // 2, 1)


def _tri_masks():
    row = lax.broadcasted_iota(jnp.int32, (CHUNK, CHUNK), 0)
    col = lax.broadcasted_iota(jnp.int32, (CHUNK, CHUNK), 1)
    return row, col


def _const_spec(shape):
    zeros = (0,) * len(shape)
    return pl.BlockSpec(shape, lambda *_: zeros, pipeline_mode=pl.Buffered(1))


def _params(semantics):
    return pltpu.CompilerParams(dimension_semantics=semantics, vmem_limit_bytes=VMEM_LIMIT)


def _decay_tables():
    log_gamma = np.log(1.0 - 2.0 ** (-5.0 - np.arange(N_HEADS, dtype=np.float64)))
    idx = np.arange(CHUNK, dtype=np.float64)
    dist = np.abs(idx[:, None] - idx[None, :])
    intra = np.exp(log_gamma[:, None, None] * dist)
    kdec = np.exp(log_gamma[None, :] * (CHUNK - 1.0 - idx)[:, None])
    qdec = np.exp(log_gamma[None, :] * (idx + 1.0)[:, None])
    chunk_decay = np.exp(log_gamma * CHUNK)
    lanes = lambda t: np.repeat(t, HEAD_W, axis=1).astype(np.float32)
    return (jnp.asarray(intra.astype(np.float32)), jnp.asarray(lanes(qdec)), jnp.asarray(lanes(kdec)),
            [float(np.float32(v)) for v in chunk_decay])


def _rotary_tables(seq):
    half = HEAD_W // 2
    inv = 1.0 / (ROPE_BASE ** jnp.linspace(0.0, 1.0, half, dtype=jnp.float32))
    both = lambda t: jnp.concatenate([t, t], axis=-1)
    ang_a = jnp.arange(0, seq, CHUNK, dtype=jnp.float32)[:, None] * inv[None, :]
    rot_a = jnp.stack([both(jnp.cos(ang_a)), both(jnp.sin(ang_a))], axis=1)
    rot_a = jnp.pad(rot_a, ((0, 0), (0, 6), (0, 0)))
    ang_b = jnp.arange(CHUNK, dtype=jnp.float32)[:, None] * inv[None, :]
    cos_b, sin_b = both(jnp.cos(ang_b)), both(jnp.sin(ang_b))
    sign = jnp.concatenate([-jnp.ones((half,), jnp.float32), jnp.ones((half,), jnp.float32)])
    return rot_a, jnp.stack([cos_b, sin_b, cos_b * sign, sin_b * sign])


def _rotary_chunk(ra_ref, c, rb_ref):
    cos_a, sin_a = ra_ref[c, 0:1, :], ra_ref[c, 1:2, :]
    return cos_a * rb_ref[0] - sin_a * rb_ref[1], sin_a * rb_ref[2] + cos_a * rb_ref[3]


def _mesh_pos():
    return lax.axis_index("x"), lax.axis_index("y"), lax.axis_index("c")


def _flip(v, bit):
    return 1 - v if bit else v


def _gather_rows(v, name, swaps):
    rows = v.shape[0]
    n = len(swaps)

    def body(*refs):
        v_ref, out_ref = refs[0], refs[1 + n]
        swap = _SiblingSwap(refs[1:1 + n], refs[2 + n:2 + 2 * n], refs[4 + 2 * n:])
        swap.start()
        _all_devices_exchange(v_ref, out_ref, refs[2 + 2 * n], refs[3 + 2 * n])
        swap.wait()

    hbm = pl.BlockSpec(memory_space=pl.ANY)
    vmem = pl.BlockSpec(memory_space=pltpu.VMEM)
    return pl.pallas_call(
        body, name=name,
        out_shape=(jax.ShapeDtypeStruct((N_DEV, rows, 128), jnp.float32),)
        + tuple(jax.ShapeDtypeStruct(a.shape, a.dtype) for a in swaps),
        in_specs=[vmem] + [hbm] * n,
        out_specs=(vmem,) + (hbm,) * n,
        scratch_shapes=_all_devices_sems() + _swap_sems(n),
    )(v, *swaps)


def _all_devices_sems():
    return [pltpu.SemaphoreType.DMA((N_DEV - 1,)), pltpu.SemaphoreType.DMA((N_DEV - 1,))]


def _all_devices_exchange(v_ref, out_ref, send_sems, recv_sems):
    x, y, c = _mesh_pos()
    me = 4 * x + 2 * y + c
    out_ref[me] = v_ref[...]
    sends, recvs = [], []
    for k in range(1, N_DEV):
        px, py, pc = _flip(x, (k >> 2) & 1), _flip(y, (k >> 1) & 1), _flip(c, k & 1)
        peer = 4 * px + 2 * py + pc
        sends.append(pltpu.make_async_remote_copy(
            src_ref=v_ref, dst_ref=out_ref.at[me], send_sem=send_sems.at[k - 1], recv_sem=recv_sems.at[k - 1],
            device_id=(px, py, pc), device_id_type=MESH))
        recvs.append(pltpu.make_async_remote_copy(
            src_ref=v_ref, dst_ref=out_ref.at[peer], send_sem=send_sems.at[k - 1], recv_sem=recv_sems.at[k - 1],
            device_id=(px, py, pc), device_id_type=MESH))
    for cp in sends:
        cp.start()
    for cp in recvs:
        cp.wait_recv()
    for cp in sends:
        cp.wait_send()


def _prologue(cond_rows, w_ada_blk, b_blk, w_in_t):
    cols = w_ada_blk.shape[1]
    groups = cols // 128
    c_rows = D_MODEL // 128

    def body(cond_ref, w_ref, b_ref, win_ref, cond_all_ref, mod_all_ref, stack_ref, mod_sc, *sems):
        gather = _ChipGather([win_ref], [stack_ref], sems[:5])
        gather.start()
        _all_devices_exchange(cond_ref, cond_all_ref, sems[5], sems[6])
        acc = jnp.broadcast_to(b_ref[...], (N_DEV, cols))
        for r in range(c_rows):
            cv = cond_all_ref[:, r, :]
            acc = acc + _mm32(cv * _sigmoid(cv), w_ref[r * 128:(r + 1) * 128, :])
        for k in range(groups):
            mod_sc[k] = acc[:, k * 128:(k + 1) * 128]
        _all_devices_exchange(mod_sc, mod_all_ref, sems[7], sems[8])
        gather.forward()
        gather.finish()

    vmem = pl.BlockSpec(memory_space=pltpu.VMEM)
    hbm = pl.BlockSpec(memory_space=pl.ANY)
    return pl.pallas_call(
        body, name="prologue",
        out_shape=(jax.ShapeDtypeStruct((N_DEV,) + cond_rows.shape, jnp.float32),
                   jax.ShapeDtypeStruct((N_DEV, groups, N_DEV, 128), jnp.float32))
        + _exchange_out_shapes([w_in_t], True),
        in_specs=[vmem, vmem, vmem, hbm],
        out_specs=(vmem, vmem, hbm),
        scratch_shapes=[pltpu.VMEM((groups, N_DEV, 128), jnp.float32)] + _gather_sems(1)
        + _all_devices_sems() + _all_devices_sems(),
        compiler_params=pltpu.CompilerParams(vmem_limit_bytes=VMEM_LIMIT),
    )(cond_rows, w_ada_blk, b_blk, w_in_t)


def _exchange_out_shapes(arrays, gather):
    return tuple(jax.ShapeDtypeStruct((N_CHIP,) + a.shape if gather else a.shape, a.dtype) for a in arrays)


def _scatter_sems(n):
    n_sem = n * (N_CHIP - 1)
    return [pltpu.SemaphoreType.DMA((n_sem,)), pltpu.SemaphoreType.DMA((n_sem,)), pltpu.SemaphoreType.DMA((n,))]


def _gather_sems(n):
    n_sem = n * (N_CHIP - 1)
    return [pltpu.SemaphoreType.DMA((n_sem,))] * 4 + [pltpu.SemaphoreType.DMA((n,))]


def _peer_chips(x, y):
    out = []
    for k in range(1, N_CHIP):
        px, py = _flip(x, (k >> 1) & 1), _flip(y, k & 1)
        out.append((px, py, 2 * px + py))
    return out


class _ChipScatter:
    def __init__(self, ins, outs, sems):
        send_sems, recv_sems, local_sems = sems
        x, y, c = _mesh_pos()
        chip = 2 * x + y
        self.local, self.sends, self.recvs = [], [], []
        for i in range(len(ins)):
            self.local.append(pltpu.make_async_copy(ins[i].at[chip], outs[i].at[chip], local_sems.at[i]))
            for k, (px, py, peer_chip) in enumerate(_peer_chips(x, y)):
                sem = i * (N_CHIP - 1) + k
                src = ins[i].at[peer_chip]
                self.sends.append(pltpu.make_async_remote_copy(
                    src_ref=src, dst_ref=outs[i].at[chip], send_sem=send_sems.at[sem], recv_sem=recv_sems.at[sem],
                    device_id=(px, py, c), device_id_type=MESH))
                self.recvs.append(pltpu.make_async_remote_copy(
                    src_ref=src, dst_ref=outs[i].at[peer_chip], send_sem=send_sems.at[sem], recv_sem=recv_sems.at[sem],
                    device_id=(px, py, c), device_id_type=MESH))

    def start(self):
        for cp in self.local + self.sends:
            cp.start()

    def wait(self):
        for cp in self.recvs:
            cp.wait_recv()
        for cp in self.sends:
            cp.wait_send()
        for cp in self.local:
            cp.wait()


class _ChipGather:
    def __init__(self, ins, outs, sems):
        ici_send, ici_recv, d2d_send, d2d_recv, local_sems = sems
        x, y, c = _mesh_pos()
        chip = 2 * x + y
        self.local, self.ici_sends, self.ici_recvs, self.d2d_sends, self.d2d_recvs = [], [], [], [], []
        for i in range(len(ins)):
            half = ins[i].shape[-1] // 2
            assert half % 128 == 0
            lead = (slice(None),) * (len(ins[i].shape) - 1)
            mine = lead + (pl.ds(pl.multiple_of(c * half, 128), half),)
            theirs = lead + (pl.ds(pl.multiple_of((1 - c) * half, 128), half),)
            self.local.append(pltpu.make_async_copy(ins[i], outs[i].at[chip], local_sems.at[i]))
            for k, (px, py, peer_chip) in enumerate(_peer_chips(x, y)):
                sem = i * (N_CHIP - 1) + k
                self.ici_sends.append(pltpu.make_async_remote_copy(
                    src_ref=ins[i].at[mine], dst_ref=outs[i].at[chip].at[mine],
                    send_sem=ici_send.at[sem], recv_sem=ici_recv.at[sem], device_id=(px, py, c), device_id_type=MESH))
                landed = outs[i].at[peer_chip].at[mine]
                self.ici_recvs.append(pltpu.make_async_remote_copy(
                    src_ref=ins[i].at[mine], dst_ref=landed,
                    send_sem=ici_send.at[sem], recv_sem=ici_recv.at[sem], device_id=(px, py, c), device_id_type=MESH))
                self.d2d_sends.append(pltpu.make_async_remote_copy(
                    src_ref=landed, dst_ref=landed,
                    send_sem=d2d_send.at[sem], recv_sem=d2d_recv.at[sem], device_id=(x, y, 1 - c), device_id_type=MESH))
                self.d2d_recvs.append(pltpu.make_async_remote_copy(
                    src_ref=landed, dst_ref=outs[i].at[peer_chip].at[theirs],
                    send_sem=d2d_send.at[sem], recv_sem=d2d_recv.at[sem], device_id=(x, y, 1 - c), device_id_type=MESH))

    def start(self):
        for cp in self.local + self.ici_sends:
            cp.start()

    def forward(self):
        for landed, onward in zip(self.ici_recvs, self.d2d_sends):
            landed.wait_recv()
            onward.start()

    def finish(self):
        for cp in self.d2d_recvs:
            cp.wait_recv()
        for cp in self.d2d_sends + self.ici_sends:
            cp.wait_send()
        for cp in self.local:
            cp.wait()


def _swap_sems(n):
    return [pltpu.SemaphoreType.DMA((n,)), pltpu.SemaphoreType.DMA((n,))]


class _SiblingSwap:
    def __init__(self, ins, outs, sems):
        send_sems, recv_sems = sems
        x, y, c = _mesh_pos()
        self.copies = [pltpu.make_async_remote_copy(
            src_ref=ins[i], dst_ref=outs[i], send_sem=send_sems.at[i], recv_sem=recv_sems.at[i],
            device_id=(x, y, 1 - c), device_id_type=MESH) for i in range(len(ins))]

    def start(self):
        for cp in self.copies:
            cp.start()

    def wait(self):
        for cp in self.copies:
            cp.wait_recv()
        for cp in self.copies:
            cp.wait_send()


def _adam(w, g, m, v):
    m2 = ADAM_B1 * m + (1.0 - ADAM_B1) * g
    v2 = ADAM_B2 * v + (1.0 - ADAM_B2) * (g * g)
    m_hat = m2 / (1.0 - ADAM_B1 ** ADAM_STEP)
    v_hat = v2 / (1.0 - ADAM_B2 ** ADAM_STEP)
    delta = -ADAM_LR * (m_hat / (jnp.sqrt(v_hat) + ADAM_EPS) + ADAM_WD * w)
    return delta, m2, v2


def _ada_bwd_adam(c_t, dmod_blk, w, m, v):
    rows, cols = w.shape
    tile = 512
    assert cols % tile == 0

    def body(c_ref, d_ref, w_ref, m_ref, v_ref, g_ref, dl_ref, m2_ref, v2_ref):
        sc = c_ref[...]
        sc = sc * _sigmoid(sc)
        dm = d_ref[...]
        g = sc[:, 0:1] * dm[0:1, :]
        for b in range(1, N_DEV):
            g = g + sc[:, b:b + 1] * dm[b:b + 1, :]
        delta, m2, v2 = _adam(w_ref[...], g, m_ref[...], v_ref[...])
        g_ref[...] = g
        dl_ref[...] = delta
        m2_ref[...] = m2
        v2_ref[...] = v2

    blk = pl.BlockSpec((rows, tile), lambda j: (0, j))
    out = jax.ShapeDtypeStruct((rows, cols), jnp.float32)
    return pl.pallas_call(
        body, name="ada_bwd_adam", grid=(cols // tile,),
        out_shape=(out, out, out, out),
        in_specs=[pl.BlockSpec((rows, N_DEV), lambda j: (0, 0)), pl.BlockSpec((N_DEV, tile), lambda j: (0, j)),
                  blk, blk, blk],
        out_specs=(blk, blk, blk, blk),
        compiler_params=_params(("arbitrary",)),
    )(c_t, dmod_blk, w, m, v)


def _inproj_fwd(x2, vecs, w_in_p, tm, riders):
    seq = x2.shape[0]
    n_tiles = seq // tm
    n_ride = len(riders)

    def body(*refs):
        x_ref, vec_ref, w_ref = refs[:3]
        ride_in, refs = refs[3:3 + n_ride], refs[3 + n_ride:]
        p_ref, u_ref = refs[:2]
        ride_out, sems = refs[2:2 + n_ride], refs[2 + n_ride:]
        gather = _ChipGather(ride_in, ride_out, sems)

        @pl.when(pl.program_id(0) == 0)
        def _():
            gather.start()

        xh, _ = _ln(x_ref[...])
        u = (xh * (1.0 + vec_ref[1:2, :]) + vec_ref[0:1, :]).astype(MXU_DTYPE)
        u_ref[...] = u
        p_ref[...] = _mm(u, w_ref[...])

        @pl.when(pl.program_id(0) == (3 * n_tiles) // 4)
        def _():
            gather.forward()

        @pl.when(pl.program_id(0) == n_tiles - 1)
        def _():
            gather.finish()

    hbm = pl.BlockSpec(memory_space=pl.ANY)
    return pl.pallas_call(
        body, name="inproj_fwd", grid=(n_tiles,),
        out_shape=(jax.ShapeDtypeStruct((seq, N_PROJ), jnp.float32), jax.ShapeDtypeStruct((seq, D_MODEL), MXU_DTYPE))
        + _exchange_out_shapes(riders, True),
        in_specs=[pl.BlockSpec((tm, D_MODEL), lambda i: (i, 0)), _const_spec(vecs.shape), _const_spec(w_in_p.shape)]
        + [hbm] * n_ride,
        out_specs=(pl.BlockSpec((tm, N_PROJ), lambda i: (i, 0)), pl.BlockSpec((tm, D_MODEL), lambda i: (i, 0)))
        + (hbm,) * n_ride,
        scratch_shapes=_gather_sems(n_ride),
        compiler_params=_params(("arbitrary",)),
    )(x2, vecs, w_in_p, *riders)


def _inproj_bwd(dproj, x2, dxa, vecs, w_in_pt, tm, riders):
    seq = x2.shape[0]
    n_tiles = seq // tm
    n_ride = len(riders)

    def body(*refs):
        dp_ref, x_ref, dxa_ref, vec_ref, w_ref = refs[:5]
        ride_in, refs = refs[5:5 + n_ride], refs[5 + n_ride:]
        gx_ref, sums_ref = refs[:2]
        ride_out, sems = refs[2:2 + n_ride], refs[2 + n_ride:]
        exchange = _ChipScatter(ride_in, ride_out, sems)

        @pl.when(pl.program_id(0) == 0)
        def _():
            exchange.start()
            sums_ref[...] = jnp.zeros_like(sums_ref)

        du = _mm(dp_ref[...], w_ref[...])
        xh, rstd = _ln(x_ref[...])
        sums_ref[0:1, :] += _colsum(du)
        sums_ref[1:2, :] += _colsum(du * xh)
        gx_ref[...] = dxa_ref[...] + _ln_bwd(du * (1.0 + vec_ref[1:2, :]), xh, rstd)

        @pl.when(pl.program_id(0) == n_tiles - 1)
        def _():
            exchange.wait()

    tile = pl.BlockSpec((tm, D_MODEL), lambda i: (i, 0))
    hbm = pl.BlockSpec(memory_space=pl.ANY)
    return pl.pallas_call(
        body, name="inproj_bwd", grid=(n_tiles,),
        out_shape=(jax.ShapeDtypeStruct((seq, D_MODEL), jnp.float32), jax.ShapeDtypeStruct((8, D_MODEL), jnp.float32))
        + _exchange_out_shapes(riders, False),
        in_specs=[pl.BlockSpec((tm, N_PROJ), lambda i: (i, 0)), tile, tile, _const_spec(vecs.shape),
                  _const_spec(w_in_pt.shape)] + [hbm] * n_ride,
        out_specs=(tile, pl.BlockSpec((8, D_MODEL), lambda i: (0, 0))) + (hbm,) * n_ride,
        scratch_shapes=_scatter_sems(n_ride),
        compiler_params=_params(("arbitrary",)),
    )(dproj, x2, dxa, vecs, w_in_pt, *riders)


def _head(h):
    return slice(h * HEAD_W, (h + 1) * HEAD_W)


def _cols(ref, off, h):
    return ref[:, off + h * HEAD_W:off + (h + 1) * HEAD_W]


HEADS = range(N_HEADS)


def _mixer_chunk_forward(p_ref, cc, ss, dm_ref, qdec_ref, kdec_ref, wg_ref, bg_ref, states):
    row, col = _tri_masks()
    lower = row >= col
    f = {}
    f["glr"] = p_ref[:, OFF_LR:OFF_LR + HEAD_W]
    f["logit"] = _mm(f["glr"], wg_ref[...]) + bg_ref[...]
    rq = [_cols(p_ref, OFF_RQ, h) for h in HEADS]
    rk = [_cols(p_ref, OFF_RK, h) for h in HEADS]
    f["rv"] = [_cols(p_ref, OFF_RV, h) for h in HEADS]
    f["qr"] = [(rq[h] * cc + _swap_halves(rq[h]) * ss) * RET_SCALE for h in HEADS]
    f["kr"] = [rk[h] * cc + _swap_halves(rk[h]) * ss for h in HEADS]
    s_raw = [_mm_nt(f["qr"][h], f["kr"][h]) for h in HEADS]
    yield
    la = _log_sigmoid(f["logit"]) * (1.0 / GATE_TAU)
    b = _running_sum(lower, la)
    f["qd"] = [f["qr"][h] * qdec_ref[:, _head(h)] for h in HEADS]
    f["kd"] = [f["kr"][h] * kdec_ref[:, _head(h)] for h in HEADS]
    f["scores"] = [s_raw[h] * dm_ref[h] for h in HEADS]
    yield
    b_last = b[CHUNK - 1:CHUNK, :]
    b_mid = b[CHUNK // 2 - 1:CHUNK // 2, :]
    f["e"], f["ei"] = jnp.exp(b - b_mid), jnp.exp(b_mid - b)
    f["eb"], f["ek"], f["ebl"] = jnp.exp(b), jnp.exp(b_last - b), jnp.exp(b_last)
    gq = [_cols(p_ref, OFF_GQ, h) * GLA_SCALE for h in HEADS]
    gk = [_cols(p_ref, OFF_GK, h) for h in HEADS]
    f["gv"] = [_cols(p_ref, OFF_GV, h) for h in HEADS]
    f["q_e"] = [gq[h] * f["e"][:, _head(h)] for h in HEADS]
    f["q_i"] = [gq[h] * f["ei"][:, _head(h)] for h in HEADS]
    f["k_e"] = [gk[h] * f["e"][:, _head(h)] for h in HEADS]
    f["k_i"] = [gk[h] * f["ei"][:, _head(h)] for h in HEADS]
    low = [_mm_nt(f["q_e"][h], f["k_i"][h]) for h in HEADS]
    up = [_mm_nt(f["q_i"][h], f["k_e"][h]) for h in HEADS]
    yield
    f["att"] = [jnp.where(lower, low[h], up[h]) for h in HEADS]
    f["qb"] = [gq[h] * f["eb"][:, _head(h)] for h in HEADS]
    f["kb"] = [gk[h] * f["ek"][:, _head(h)] for h in HEADS]
    ret_state, gla_state_t = states()
    f["o_ret"] = [_mm(f["scores"][h], f["rv"][h]) + _mm(f["qd"][h], ret_state[h]) for h in HEADS]
    f["o_gla"] = [_mm(f["att"][h], f["gv"][h]) + _mm_nt(f["qb"][h], gla_state_t[h]) for h in HEADS]
    return f


def _interleave(generators):
    live = list(generators)
    while live:
        for g in list(live):
            try:
                next(g)
            except StopIteration:
                live.remove(g)


def _mixer_fwd(proj, tables, wg_p, bg_p, ret_norm_w, gla_norm_w, riders):
    seq = proj.shape[0]
    n_chunks = seq // CHUNK
    per_step = min(n_chunks, CHUNKS_PER_STEP)
    n_steps = n_chunks // per_step
    n_ride = len(riders)
    rot_a, rot_b, dm_t, qdec_t, kdec_t, chunk_decay = tables

    def body(*refs):
        p_ref, ra_ref, rb_ref, dm_ref, qdec_ref, kdec_ref, wg_ref, bg_ref, wr_ref, wl_ref = refs[:10]
        ride_in, refs = refs[10:10 + n_ride], refs[10 + n_ride:]
        mix_ref, rsave_ref, ssave_ref = refs[:3]
        ride_out, refs = refs[3:3 + n_ride], refs[3 + n_ride:]
        r_sc, s_sc = refs[:2]
        gather = _ChipGather(ride_in, ride_out, refs[2:])

        @pl.when(pl.program_id(0) == 0)
        def _():
            gather.start()
            r_sc[...] = jnp.zeros_like(r_sc)
            s_sc[...] = jnp.zeros_like(s_sc)

        def one_chunk(c):
            p_c = p_ref.at[c * CHUNK:(c + 1) * CHUNK, :]
            mix_c = mix_ref.at[c * CHUNK:(c + 1) * CHUNK, :]
            before = {}

            def states():
                before["ret"] = [r_sc[h] for h in HEADS]
                before["gla"] = [s_sc[h] for h in HEADS]
                for h in HEADS:
                    rsave_ref[c, h] = before["ret"][h].astype(rsave_ref.dtype)
                    ssave_ref[c, h] = before["gla"][h]
                return before["ret"], before["gla"]

            cc, ss = _rotary_chunk(ra_ref, c, rb_ref)
            f = yield from _mixer_chunk_forward(p_c, cc, ss, dm_ref, qdec_ref, kdec_ref, wg_ref, bg_ref, states)
            for h in HEADS:
                r_sc[h] = chunk_decay[h] * before["ret"][h] + _mm_tn(f["kd"][h], f["rv"][h])
            for h in HEADS:
                s_sc[h] = before["gla"][h] * f["ebl"][:, _head(h)] + _mm_tn(f["gv"][h], f["kb"][h])
            yield
            for h in HEADS:
                on, _ = _ln(f["o_ret"][h])
                g = _cols(p_c, OFF_RG, h)
                mix_c[:, _head(h)] = (on * wr_ref[:, _head(h)] * (g * _sigmoid(g))).astype(mix_ref.dtype)
            for h in HEADS:
                o = f["o_gla"][h]
                on = o * lax.rsqrt(_rowmean(o * o) + LN_EPS)
                g = _cols(p_c, OFF_GG, h)
                mix_c[:, _head(N_HEADS + h)] = (on * wl_ref[:, _head(h)] * (g * _sigmoid(g))).astype(mix_ref.dtype)

        for c0 in range(0, per_step, CHUNKS_IN_LOCKSTEP):
            _interleave([one_chunk(c) for c in range(c0, min(per_step, c0 + CHUNKS_IN_LOCKSTEP))])

        @pl.when(pl.program_id(0) == (3 * n_steps) // 4)
        def _():
            gather.forward()

        @pl.when(pl.program_id(0) == n_steps - 1)
        def _():
            gather.finish()

    state_shape = (n_chunks, N_HEADS, HEAD_W, HEAD_W)
    state_blk = pl.BlockSpec((per_step, N_HEADS, HEAD_W, HEAD_W), lambda i: (i, 0, 0, 0))
    rot_blk = pl.BlockSpec((per_step, 8, HEAD_W), lambda i: (i, 0, 0))
    rows = per_step * CHUNK
    hbm = pl.BlockSpec(memory_space=pl.ANY)
    return pl.pallas_call(
        body, name="mixer_fwd", grid=(n_steps,),
        out_shape=(jax.ShapeDtypeStruct((seq, D_MODEL), MXU_DTYPE),
                   jax.ShapeDtypeStruct(state_shape, MXU_DTYPE), jax.ShapeDtypeStruct(state_shape, jnp.float32))
        + _exchange_out_shapes(riders, True),
        in_specs=[pl.BlockSpec((rows, N_PROJ), lambda i: (i, 0)), rot_blk, _const_spec(rot_b.shape),
                  _const_spec(dm_t.shape), _const_spec(qdec_t.shape), _const_spec(kdec_t.shape),
                  _const_spec(wg_p.shape), _const_spec(bg_p.shape), _const_spec(ret_norm_w.shape),
                  _const_spec(gla_norm_w.shape)] + [hbm] * n_ride,
        out_specs=(pl.BlockSpec((rows, D_MODEL), lambda i: (i, 0)), state_blk, state_blk) + (hbm,) * n_ride,
        scratch_shapes=[pltpu.VMEM((N_HEADS, HEAD_W, HEAD_W), jnp.float32),
                        pltpu.VMEM((N_HEADS, HEAD_W, HEAD_W), jnp.float32)] + _gather_sems(n_ride),
        compiler_params=_params(("arbitrary",)),
    )(proj, rot_a, rot_b, dm_t, qdec_t, kdec_t, wg_p, bg_p, ret_norm_w, gla_norm_w, *riders)


def _mixer_bwd(proj, dmixed, rsave, ssave, tables, wg_p, bg_p, ret_norm_w, gla_norm_w, riders):
    seq = proj.shape[0]
    n_chunks = seq // CHUNK
    per_step = min(n_chunks, CHUNKS_PER_STEP)
    n_steps = n_chunks // per_step
    n_ride = len(riders)
    rot_a, rot_b, dm_t, qdec_t, kdec_t, chunk_decay = tables
    last = n_steps - 1

    def body(*refs):
        p_blk, dmx_blk = refs[:2]
        shared_in = refs[2:13]
        ride_in, refs = refs[13:13 + n_ride], refs[13 + n_ride:]
        dp_blk, dwr_ref, dwl_ref, dwg_ref, dbg_ref = refs[:5]
        ride_out, refs = refs[5:5 + n_ride], refs[5 + n_ride:]
        dr_sc, ds_sc = refs[:2]
        exchange = _ChipScatter(ride_in, ride_out, refs[2:])

        @pl.when(pl.program_id(0) == 0)
        def _():
            exchange.start()
            dr_sc[...] = jnp.zeros_like(dr_sc)
            ds_sc[...] = jnp.zeros_like(ds_sc)
            dwr_ref[...] = jnp.zeros_like(dwr_ref)
            dwl_ref[...] = jnp.zeros_like(dwl_ref)
            dwg_ref[...] = jnp.zeros_like(dwg_ref)
            dbg_ref[...] = jnp.zeros_like(dbg_ref)

        def chunk_stages(c):
            rows = slice(c * CHUNK, (c + 1) * CHUNK)
            return one_chunk(c, p_blk.at[rows, :], dmx_blk.at[rows, :], dp_blk.at[rows, :], *shared_in,
                             dwr_ref, dwl_ref, dwg_ref, dbg_ref, dr_sc, ds_sc)

        for c0 in range(per_step, 0, -CHUNKS_IN_LOCKSTEP):
            _interleave([chunk_stages(c) for c in reversed(range(max(0, c0 - CHUNKS_IN_LOCKSTEP), c0))])

        @pl.when(pl.program_id(0) == last)
        def _():
            exchange.wait()

    def one_chunk(c, p_ref, dmx_ref, dp_ref, rsave_ref, ssave_ref, ra_ref, rb_ref, dm_ref, qdec_ref, kdec_ref,
                  wg_ref, bg_ref, wr_ref, wl_ref, dwr_ref, dwl_ref, dwg_ref, dbg_ref, dr_sc, ds_sc):
        def put(off, h, val):
            dp_ref[:, off + h * HEAD_W:off + (h + 1) * HEAD_W] = val.astype(dp_ref.dtype)

        cc, ss = _rotary_chunk(ra_ref, c, rb_ref)
        row, col = _tri_masks()
        ret_state = [rsave_ref[c, h] for h in HEADS]
        gla_state_t = [ssave_ref[c, h] for h in HEADS]
        f = yield from _mixer_chunk_forward(p_ref, cc, ss, dm_ref, qdec_ref, kdec_ref, wg_ref, bg_ref,
                                            lambda: (ret_state, gla_state_t))
        yield

        do_ret, do_gla = [], []
        for h in HEADS:
            on, rstd = _ln(f["o_ret"][h])
            g = _cols(p_ref, OFF_RG, h)
            sg = _sigmoid(g)
            dy = dmx_ref[:, _head(h)].astype(jnp.float32)
            wr = wr_ref[:, _head(h)]
            dwr_ref[:, _head(h)] += _colsum(dy * on * (g * sg))
            put(OFF_RG, h, dy * on * wr * (sg * (1.0 + g * (1.0 - sg))))
            do_ret.append(_ln_bwd(dy * wr * (g * sg), on, rstd))
        for h in HEADS:
            o = f["o_gla"][h]
            rstd = lax.rsqrt(_rowmean(o * o) + LN_EPS)
            on = o * rstd
            g = _cols(p_ref, OFF_GG, h)
            sg = _sigmoid(g)
            dy = dmx_ref[:, _head(N_HEADS + h)].astype(jnp.float32)
            wl = wl_ref[:, _head(h)]
            dwl_ref[:, _head(h)] += _colsum(dy * on * (g * sg))
            put(OFF_GG, h, dy * on * wl * (sg * (1.0 + g * (1.0 - sg))))
            don = dy * wl * (g * sg)
            do_gla.append(rstd * (don - on * _rowmean(don * on)))

        yield

        d_ret_new = [dr_sc[h] for h in HEADS]
        d_gla_new = [ds_sc[h] for h in HEADS]
        ds_raw = [_mm_nt(do_ret[h], f["rv"][h]) * dm_ref[h] for h in HEADS]
        d_att = [_mm_nt(do_gla[h], f["gv"][h]) for h in HEADS]
        dq_state = [_mm_nt(do_ret[h], ret_state[h]) for h in HEADS]
        dk_state = [_mm_nt(f["rv"][h], d_ret_new[h]) for h in HEADS]
        dqb = [_mm(do_gla[h], gla_state_t[h]) for h in HEADS]
        dkb = [_mm(f["gv"][h], d_gla_new[h]) for h in HEADS]
        for h in HEADS:
            put(OFF_RV, h, _mm_tn(f["scores"][h], do_ret[h]) + _mm(f["kd"][h], d_ret_new[h]))
        for h in HEADS:
            put(OFF_GV, h, _mm_tn(f["att"][h], do_gla[h]) + _mm_nt(f["kb"][h], d_gla_new[h]))
        for h in HEADS:
            dr_sc[h] = chunk_decay[h] * d_ret_new[h] + _mm_tn(f["qd"][h], do_ret[h])
        for h in HEADS:
            ds_sc[h] = d_gla_new[h] * f["ebl"][:, _head(h)] + _mm_tn(do_gla[h], f["qb"][h])
        yield

        dqr = [_mm(ds_raw[h], f["kr"][h]) + dq_state[h] * qdec_ref[:, _head(h)] for h in HEADS]
        dkr = [_mm_tn(ds_raw[h], f["qr"][h]) + dk_state[h] * kdec_ref[:, _head(h)] for h in HEADS]
        d_low = [jnp.where(row >= col, d_att[h], 0.0) for h in HEADS]
        d_up = [jnp.where(row < col, d_att[h], 0.0) for h in HEADS]
        dq_e = [_mm(d_low[h], f["k_i"][h]) for h in HEADS]
        dk_i = [_mm_tn(d_low[h], f["q_e"][h]) for h in HEADS]
        dq_i = [_mm(d_up[h], f["k_e"][h]) for h in HEADS]
        dk_e = [_mm_tn(d_up[h], f["q_i"][h]) for h in HEADS]
        yield
        for h in HEADS:
            put(OFF_RQ, h, (dqr[h] * cc + _swap_halves(dqr[h] * ss)) * RET_SCALE)
            put(OFF_RK, h, dkr[h] * cc + _swap_halves(dkr[h] * ss))
        row_id = lax.broadcasted_iota(jnp.int32, (CHUNK, HEAD_W), 0)
        db_heads = []
        for h in HEADS:
            hs = _head(h)
            e, ei, eb, ek, ebl = f["e"][:, hs], f["ei"][:, hs], f["eb"][:, hs], f["ek"][:, hs], f["ebl"][:, hs]
            put(OFF_GQ, h, (dq_e[h] * e + dq_i[h] * ei + dqb[h] * eb) * GLA_SCALE)
            put(OFF_GK, h, dk_e[h] * e + dk_i[h] * ei + dkb[h] * ek)
            db = (dq_e[h] * f["q_e"][h] - dq_i[h] * f["q_i"][h] + dk_e[h] * f["k_e"][h] - dk_i[h] * f["k_i"][h]
                  + dqb[h] * f["qb"][h] - dkb[h] * f["kb"][h])
            db_last = _colsum(dkb[h] * f["kb"][h]) + ebl * _colsum(gla_state_t[h] * d_gla_new[h])
            db_heads.append(db + jnp.where(row_id == CHUNK - 1, db_last, 0.0))
        db = jnp.concatenate(db_heads, axis=1)
        d_la = _running_sum(col >= row, db)
        d_logit = d_la * (1.0 / GATE_TAU) * (1.0 - _sigmoid(f["logit"]))
        put(OFF_LR, 0, _mm_nt(d_logit, wg_ref[...]))
        dwg_ref[...] += _mm_tn(f["glr"], d_logit)
        dbg_ref[...] += _colsum(d_logit)

    state_blk = pl.BlockSpec((per_step, N_HEADS, HEAD_W, HEAD_W), lambda i: (last - i, 0, 0, 0))
    rot_blk = pl.BlockSpec((per_step, 8, HEAD_W), lambda i: (last - i, 0, 0))
    width = N_HEADS * HEAD_W
    vec_out = pl.BlockSpec((1, width), lambda i: (0, 0))
    hbm = pl.BlockSpec(memory_space=pl.ANY)
    rows_blk = per_step * CHUNK
    return pl.pallas_call(
        body, name="mixer_bwd", grid=(n_steps,),
        out_shape=(jax.ShapeDtypeStruct((seq, N_PROJ), MXU_DTYPE),
                   jax.ShapeDtypeStruct((1, width), jnp.float32), jax.ShapeDtypeStruct((1, width), jnp.float32),
                   jax.ShapeDtypeStruct((HEAD_W, width), jnp.float32), jax.ShapeDtypeStruct((1, width), jnp.float32))
        + _exchange_out_shapes(riders, False),
        in_specs=[pl.BlockSpec((rows_blk, N_PROJ), lambda i: (last - i, 0)),
                  pl.BlockSpec((rows_blk, D_MODEL), lambda i: (last - i, 0)), state_blk, state_blk, rot_blk,
                  _const_spec(rot_b.shape),
                  _const_spec(dm_t.shape), _const_spec(qdec_t.shape), _const_spec(kdec_t.shape),
                  _const_spec(wg_p.shape), _const_spec(bg_p.shape), _const_spec(ret_norm_w.shape),
                  _const_spec(gla_norm_w.shape)] + [hbm] * n_ride,
        out_specs=(pl.BlockSpec((rows_blk, N_PROJ), lambda i: (last - i, 0)), vec_out, vec_out,
                   pl.BlockSpec((HEAD_W, width), lambda i: (0, 0)), vec_out) + (hbm,) * n_ride,
        scratch_shapes=[pltpu.VMEM((N_HEADS, HEAD_W, HEAD_W), jnp.float32),
                        pltpu.VMEM((N_HEADS, HEAD_W, HEAD_W), jnp.float32)] + _scatter_sems(n_ride),
        compiler_params=_params(("arbitrary",)),
    )(proj, dmixed, rsave, ssave, rot_a, rot_b, dm_t, qdec_t, kdec_t, wg_p, bg_p, ret_norm_w, gla_norm_w, *riders)


V_GATE1, V_SCALE2, V_SHIFT2, V_GATE2, V_LN1W, V_LN1B, V_LN2W, V_LN2B = range(8)
S_GATE1, S_SCALE2, S_SHIFT2, S_GATE2, S_LN1W, S_LN1B, S_LN2W, S_LN2B, S_LOSS = range(9)


def _mlp_fwd_bwd(x2, mixed, target, vecs, w_out, w1_chunks, w2_chunks, tm):
    seq = x2.shape[0]
    n_fc, _, fc = w1_chunks.shape

    def body(x_ref, mx_ref, t_ref, vec_ref, wo_ref, w1_ref, w2_ref,
             dmx_ref, dxa_ref, a_ref, dh_ref, u2_ref, df_ref, dm_ref, sums_ref, relu_sc):
        @pl.when(pl.program_id(0) == 0)
        def _():
            sums_ref[...] = jnp.zeros_like(sums_ref)

        vec = lambda r: vec_ref[r:r + 1, :]

        def acc(r, val):
            sums_ref[r:r + 1, :] += _colsum(val)

        xx = x_ref[...]
        m = _mm(mx_ref[...], wo_ref[...])
        z1h, rstd1 = _ln(ALPHA * xx + vec(V_GATE1) * m)
        x1 = z1h * vec(V_LN1W) + vec(V_LN1B)
        x1h, rstd0 = _ln(x1)
        u2 = (x1h * (1.0 + vec(V_SCALE2)) + vec(V_SHIFT2)).astype(MXU_DTYPE)
        u2_ref[...] = u2
        f = jnp.zeros((tm, D_MODEL), jnp.float32)
        for j in range(n_fc):
            r = jnp.maximum(_mm(u2, w1_ref[j]), 0.0)
            relu_sc[:, j * fc:(j + 1) * fc] = r
            a = (r * r).astype(MXU_DTYPE)
            a_ref[:, j * fc:(j + 1) * fc] = a
            f = f + _mm(a, w2_ref[j])
        z2h, rstd2 = _ln(ALPHA * x1 + vec(V_GATE2) * f)
        err = z2h * vec(V_LN2W) + vec(V_LN2B) - t_ref[...]
        acc(S_LOSS, err * err)
        dy = err * (1.0 / D_MODEL)
        acc(S_LN2W, dy * z2h)
        acc(S_LN2B, dy)
        dz2 = _ln_bwd(dy * vec(V_LN2W), z2h, rstd2)
        acc(S_GATE2, dz2 * f)
        df = (vec(V_GATE2) * dz2).astype(MXU_DTYPE)
        df_ref[...] = df
        du2 = jnp.zeros((tm, D_MODEL), jnp.float32)
        for j in range(n_fc):
            dh = (_mm_nt(df, w2_ref[j]) * (2.0 * relu_sc[:, j * fc:(j + 1) * fc])).astype(MXU_DTYPE)
            dh_ref[:, j * fc:(j + 1) * fc] = dh
            du2 = du2 + _mm_nt(dh, w1_ref[j])
        acc(S_SCALE2, du2 * x1h)
        acc(S_SHIFT2, du2)
        dx1 = ALPHA * dz2 + _ln_bwd(du2 * (1.0 + vec(V_SCALE2)), x1h, rstd0)
        acc(S_LN1W, dx1 * z1h)
        acc(S_LN1B, dx1)
        dz1 = _ln_bwd(dx1 * vec(V_LN1W), z1h, rstd1)
        acc(S_GATE1, dz1 * m)
        dxa_ref[...] = ALPHA * dz1
        dm = (vec(V_GATE1) * dz1).astype(MXU_DTYPE)
        dm_ref[...] = dm
        dmx_ref[...] = _mm_nt(dm, wo_ref[...])

    tile = lambda width: pl.BlockSpec((tm, width), lambda i: (i, 0))
    f32 = lambda width: jax.ShapeDtypeStruct((seq, width), jnp.float32)
    b16 = lambda width: jax.ShapeDtypeStruct((seq, width), MXU_DTYPE)
    return pl.pallas_call(
        body, name="mlp_fwd_bwd", grid=(seq // tm,),
        out_shape=(f32(D_MODEL), f32(D_MODEL), b16(D_FF), b16(D_FF), b16(D_MODEL), b16(D_MODEL), b16(D_MODEL),
                   jax.ShapeDtypeStruct((16, D_MODEL), jnp.float32)),
        in_specs=[tile(D_MODEL), tile(D_MODEL), tile(D_MODEL), _const_spec(vecs.shape), _const_spec(w_out.shape),
                  _const_spec(w1_chunks.shape), _const_spec(w2_chunks.shape)],
        out_specs=(tile(D_MODEL), tile(D_MODEL), tile(D_FF), tile(D_FF), tile(D_MODEL), tile(D_MODEL),
                   tile(D_MODEL), pl.BlockSpec((16, D_MODEL), lambda i: (0, 0))),
        scratch_shapes=[pltpu.VMEM((tm, D_FF), jnp.float32)],
        compiler_params=_params(("arbitrary",)),
    )(x2, mixed, target, vecs, w_out, w1_chunks, w2_chunks)


def _grad_matmul(a, b, name, tn, blocks_are_rows, riders=()):
    seq, m_dim = a.shape
    n_dim = b.shape[1]
    tk = min(seq, GRAD_TOKEN_TILE)
    nk = seq // tk
    n_ride = len(riders)
    if blocks_are_rows:
        tm = m_dim // N_CHIP
        assert tn == n_dim
        per_step = N_CHIP if m_dim <= GRAD_ROWS_PER_STEP else 1
        grid = (N_CHIP // per_step, 1, nk)
        out_map = lambda i, j, k: (i, 0, 0)
    else:
        tm = m_dim
        assert tn * N_CHIP == n_dim
        per_step = 1
        grid = (1, N_CHIP, nk)
        out_map = lambda i, j, k: (j, 0, 0)
    n_blocks = grid[0] * grid[1]
    rows = per_step * tm

    def body(*refs):
        a_ref, b_ref = refs[:2]
        ride_in, refs = refs[2:2 + n_ride], refs[2 + n_ride:]
        o_ref = refs[0]
        ride_out, refs = refs[1:1 + n_ride], refs[1 + n_ride:]
        acc_sc = refs[0]
        exchange = _ChipScatter(ride_in, ride_out, refs[1:]) if n_ride else None
        block = pl.program_id(0) + pl.program_id(1)
        k = pl.program_id(2)

        if exchange is not None:
            @pl.when((block == 0) & (k == 0))
            def _():
                exchange.start()

        @pl.when(k == 0)
        def _():
            acc_sc[...] = jnp.zeros_like(acc_sc)

        acc_sc[...] += _mm_tn(a_ref[...], b_ref[...])

        @pl.when(k == nk - 1)
        def _():
            for p in range(per_step):
                o_ref[p] = acc_sc[p * tm:(p + 1) * tm, :].astype(o_ref.dtype)

        if exchange is not None:
            @pl.when((block == n_blocks - 1) & (k == nk - 1))
            def _():
                exchange.wait()

    hbm = pl.BlockSpec(memory_space=pl.ANY)
    out = pl.pallas_call(
        body, name=name, grid=grid,
        out_shape=(jax.ShapeDtypeStruct((N_CHIP, tm, tn), WIRE_DTYPE),) + _exchange_out_shapes(riders, False),
        in_specs=[pl.BlockSpec((tk, rows), lambda i, j, k: (k, i)), pl.BlockSpec((tk, tn), lambda i, j, k: (k, j))]
        + [hbm] * n_ride,
        out_specs=(pl.BlockSpec((per_step, tm, tn), out_map),) + (hbm,) * n_ride,
        scratch_shapes=[pltpu.VMEM((rows, tn), jnp.float32)] + (_scatter_sems(n_ride) if n_ride else []),
        compiler_params=_params(("arbitrary", "arbitrary", "arbitrary")),
    )(a, b, *riders)
    return out if n_ride else out[0]


def _grad_matmul_full(a, b, name, tm, riders):
    seq, m_dim = a.shape
    n_dim = b.shape[1]
    tk = min(seq, GRAD_TOKEN_TILE)
    nk = seq // tk
    n_blocks = m_dim // tm
    n_ride = len(riders)
    assert m_dim % tm == 0

    def body(*refs):
        a_ref, b_ref = refs[:2]
        ride_in, refs = refs[2:2 + n_ride], refs[2 + n_ride:]
        o_ref = refs[0]
        ride_out, refs = refs[1:1 + n_ride], refs[1 + n_ride:]
        acc_sc = refs[0]
        swap = _SiblingSwap(ride_in, ride_out, refs[1:])
        i, k = pl.program_id(0), pl.program_id(1)

        @pl.when((i == 0) & (k == 0))
        def _():
            swap.start()

        @pl.when(k == 0)
        def _():
            acc_sc[...] = jnp.zeros_like(acc_sc)

        acc_sc[...] += _mm_tn(a_ref[...], b_ref[...])

        @pl.when(k == nk - 1)
        def _():
            o_ref[...] = acc_sc[...].astype(o_ref.dtype)

        @pl.when((i == n_blocks - 1) & (k == nk - 1))
        def _():
            swap.wait()

    hbm = pl.BlockSpec(memory_space=pl.ANY)
    return pl.pallas_call(
        body, name=name, grid=(n_blocks, nk),
        out_shape=(jax.ShapeDtypeStruct((m_dim, n_dim), WIRE_DTYPE),)
        + tuple(jax.ShapeDtypeStruct(r.shape, r.dtype) for r in riders),
        in_specs=[pl.BlockSpec((tk, tm), lambda i, k: (k, i)), pl.BlockSpec((tk, n_dim), lambda i, k: (k, 0))]
        + [hbm] * n_ride,
        out_specs=(pl.BlockSpec((tm, n_dim), lambda i, k: (i, 0)),) + (hbm,) * n_ride,
        scratch_shapes=[pltpu.VMEM((tm, n_dim), jnp.float32)] + _swap_sems(n_ride),
        compiler_params=_params(("arbitrary", "arbitrary")),
    )(a, b, *riders)


def _adam_pair(w, g_mine, g_sibling, m, v, name):
    rows, cols = w.shape
    tc = min(cols, ELEMENTWISE_COLS)

    def total(ref):
        if len(ref.shape) == 2:
            return ref[...]
        acc = ref[0].astype(jnp.float32)
        for j in range(1, ref.shape[0]):
            acc = acc + ref[j].astype(jnp.float32)
        return acc

    def body(w_ref, ga_ref, gb_ref, m_ref, v_ref, g_ref, dl_ref, m2_ref, v2_ref):
        g = total(ga_ref) + total(gb_ref)
        delta, m2, v2 = _adam(w_ref[...], g, m_ref[...], v_ref[...])
        g_ref[...] = g
        dl_ref[...] = delta
        m2_ref[...] = m2
        v2_ref[...] = v2

    blk = pl.BlockSpec((rows, tc), lambda i: (0, i))
    g_blk = lambda a: blk if a.ndim == 2 else pl.BlockSpec((a.shape[0], rows, tc), lambda i: (0, 0, i))
    out = jax.ShapeDtypeStruct((rows, cols), jnp.float32)
    return pl.pallas_call(
        body, name=name, grid=(cols // tc,),
        out_shape=(out, out, out, out),
        in_specs=[blk, g_blk(g_mine), g_blk(g_sibling), blk, blk], out_specs=(blk,) * 4,
        compiler_params=_params(("arbitrary",)),
    )(w, g_mine, g_sibling, m, v)


def _sum_devices(gathered):
    _, rows, _ = gathered.shape

    def body(g_ref, o_ref):
        total = g_ref[0]
        for d in range(1, N_DEV):
            total = total + g_ref[d]
        o_ref[...] = total

    return pl.pallas_call(
        body, name="sum_devices",
        out_shape=jax.ShapeDtypeStruct((rows, 128), jnp.float32),
    )(gathered)


def _adam_small(params):
    n = len(params)

    def body(*refs):
        ins, outs = refs[:4 * n], refs[4 * n:]
        for i in range(n):
            w_ref, g_ref, m_ref, v_ref = ins[4 * i:4 * i + 4]
            delta, m2, v2 = _adam(w_ref[...], g_ref[...], m_ref[...], v_ref[...])
            outs[3 * i][...] = delta
            outs[3 * i + 1][...] = m2
            outs[3 * i + 2][...] = v2

    out_shape = tuple(jax.ShapeDtypeStruct(p[0].shape, jnp.float32) for p in params for _ in range(3))
    out = pl.pallas_call(body, name="adam_small", out_shape=out_shape)(*[t for p in params for t in p])
    return [out[3 * i:3 * i + 3] for i in range(n)]


def _pad_heads(w):
    lead = w.shape[:-1]
    w = w.reshape(lead + (N_HEADS, GLA_DK))
    w = jnp.pad(w, [(0, 0)] * len(lead) + [(0, 0), (0, HEAD_W - GLA_DK)])
    return w.reshape(lead + (N_HEADS * HEAD_W,))


def _unpad_heads(w):
    lead = w.shape[:-1]
    return w.reshape(lead + (N_HEADS, HEAD_W))[..., :GLA_DK].reshape(lead + (N_HEADS * GLA_DK,))


def _pad_head_rows(w):
    w = w.reshape(N_HEADS, GLA_DK, w.shape[-1])
    return jnp.pad(w, ((0, 0), (0, HEAD_W - GLA_DK), (0, 0))).reshape(N_HEADS * HEAD_W, w.shape[-1])


def _unpad_head_rows(w):
    return w.reshape(N_HEADS, HEAD_W, w.shape[-1])[:, :GLA_DK].reshape(N_HEADS * GLA_DK, w.shape[-1])


def _pad_w_in_rows(stack):
    w = stack.reshape(-1, stack.shape[-1])
    return jnp.concatenate([
        w[:2048], _pad_head_rows(w[2048:2304]), _pad_head_rows(w[2304:2560]), w[2560:3584],
        jnp.pad(w[3584:3600], ((0, HEAD_W - GATE_RANK), (0, 0)))], axis=0)


def _unpad_w_in_stack(g, per):
    segments = [(0, g[:2048]), (2048, _unpad_head_rows(g[OFF_GQ:OFF_GQ + 512])),
                (2304, _unpad_head_rows(g[OFF_GK:OFF_GK + 512])), (2560, g[OFF_GV:OFF_LR]),
                (3584, g[OFF_LR:OFF_LR + GATE_RANK])]
    blocks = []
    for j in range(N_CHIP):
        lo, hi = j * per, (j + 1) * per
        pieces = []
        for start, rows in segments:
            a, b = max(lo, start), min(hi, start + rows.shape[0])
            if a < b:
                pieces.append(rows[a - start:b - start])
        blocks.append(jnp.concatenate(pieces, axis=0))
    return jnp.stack(blocks)


def _col_major(w):
    return jnp.transpose(w, (2, 0, 1)).reshape(w.shape[2], w.shape[1])


def _rows128(a):
    return a.reshape(-1, 128)


def _rows8(a):
    a = a.reshape(-1, 128)
    return jnp.pad(a, ((0, -a.shape[0] % 8), (0, 0)))


def kernel(x, c, w_ada, b_ada, w_in, ret_norm_w, gla_gate_w, gla_gate_b, gla_norm_w, w_out, ln1_w, ln1_b, w_ff1, w_ff2, ln2_w, ln2_b, loss_target, m_w_ada, m_b_ada, m_w_in, m_ret_norm_w, m_gla_gate_w, m_gla_gate_b, m_gla_norm_w, m_w_out, m_ln1_w, m_ln1_b, m_w_ff1, m_w_ff2, m_ln2_w, m_ln2_b, v_w_ada, v_b_ada, v_w_in, v_ret_norm_w, v_gla_gate_w, v_gla_gate_b, v_gla_norm_w, v_w_out, v_ln1_w, v_ln1_b, v_w_ff1, v_w_ff2, v_ln2_w, v_ln2_b):
    seq = x.shape[1]
    tm = min(seq, TOKEN_TILE)
    tm_in = min(seq, INPROJ_TOKEN_TILE)
    xi, yi, ci = _mesh_pos()
    dev = 4 * xi + 2 * yi + ci
    chip = 2 * xi + yi
    x2, target = x[0], loss_target[0]
    ada_cols = w_ada.shape[2]
    in_cols = w_in.shape[2]
    gate_cols = gla_gate_w.shape[2]

    b_blk = lax.dynamic_slice(b_ada, (0, chip * ada_cols), (1, ada_cols))
    g0, g1, w_in_stack = _prologue(jnp.concatenate([_rows128(c), _rows128(gla_gate_w[0])], axis=0), w_ada[0], b_blk,
                                   _col_major(w_in.astype(WIRE_DTYPE)))
    c_all = g0[:, :8].reshape(N_DEV, D_MODEL)
    gate_w_full = jnp.concatenate([g0[2 * j, 8:16].reshape(GATE_RANK, gate_cols) for j in range(N_CHIP)], axis=1)
    wg_p = jnp.pad(_pad_heads(gate_w_full), ((0, HEAD_W - GATE_RANK), (0, 0)))
    bg_p = _pad_heads(gla_gate_b)
    mine = lax.dynamic_index_in_dim(g1, dev, axis=2, keepdims=False)
    mod = jnp.concatenate([mine[2 * j].reshape(1, ada_cols) for j in range(N_CHIP)], axis=1)
    shift1, scale1, gate1, shift2, scale2, gate2 = [mod[:, i * D_MODEL:(i + 1) * D_MODEL] for i in range(6)]
    w_in_pt = _pad_w_in_rows(w_in_stack).astype(MXU_DTYPE)
    w_in_p = jnp.transpose(w_in_pt)

    zeros_row = jnp.zeros((1, D_MODEL), jnp.float32)
    vecs1 = jnp.concatenate([shift1, scale1] + [zeros_row] * 6, axis=0)
    proj, u, w2_stack = _inproj_fwd(x2, vecs1, w_in_p, tm_in, [w_ff2[0].astype(WIRE_DTYPE)])
    rot_a, rot_b = _rotary_tables(seq)
    dm_t, qdec_t, kdec_t, chunk_decay = _decay_tables()
    tables = (rot_a, rot_b, dm_t, qdec_t, kdec_t, chunk_decay)
    mixed, rsave, ssave, w_out_stack, w1_stack = _mixer_fwd(
        proj, tables, wg_p, bg_p, ret_norm_w, gla_norm_w,
        [w_out[0].astype(WIRE_DTYPE), w_ff1[0].astype(WIRE_DTYPE)])
    w_out_full = w_out_stack.reshape(D_MODEL, D_MODEL).astype(MXU_DTYPE)
    w1_chunks = w1_stack.astype(MXU_DTYPE)
    w2_chunks = w2_stack.astype(MXU_DTYPE)

    vecs2 = jnp.concatenate([gate1, scale2, shift2, gate2, ln1_w, ln1_b, ln2_w, ln2_b], axis=0)
    dmixed, dxa, act, dh, u2, df, dm, sums2 = _mlp_fwd_bwd(x2, mixed, target, vecs2, w_out_full, w1_chunks,
                                                           w2_chunks, tm)

    g_out_stack = _grad_matmul(mixed, dm, "grad_w_out", D_MODEL, True)
    g_ff1_stack, r_out = _grad_matmul(u2, dh, "grad_w_ff1", D_FF // N_CHIP, False, [g_out_stack])
    g_ff2_stack = _grad_matmul(act, df, "grad_w_ff2", D_MODEL, True)
    dproj, d_ret_norm, d_gla_norm, d_wg_p, d_bg_p, r_ff1, r_ff2 = _mixer_bwd(
        proj, dmixed, rsave, ssave, tables, wg_p, bg_p, ret_norm_w, gla_norm_w, [g_ff1_stack, g_ff2_stack])
    early = ["w_out", "w_ff1", "w_ff2"]
    partial = dict(zip(early, [r_out, r_ff1, r_ff2]))
    g_in_t, *swapped_early = _grad_matmul_full(dproj, u, "grad_w_in", N_PROJ // 3, [partial[n] for n in early])
    swapped = dict(zip(early, swapped_early))
    g_in_stack = _unpad_w_in_stack(g_in_t, in_cols)
    grad_x, sums1, r_in = _inproj_bwd(dproj, x2, dxa, vecs1, w_in_pt, tm_in, [g_in_stack])

    dmod = jnp.concatenate([sums1[0:1], sums1[1:2], sums2[S_GATE1:S_GATE1 + 1], sums2[S_SHIFT2:S_SHIFT2 + 1],
                            sums2[S_SCALE2:S_SCALE2 + 1], sums2[S_GATE2:S_GATE2 + 1]], axis=1)
    d_gate_w_full = _unpad_heads(d_wg_p[:GATE_RANK])
    flat = lambda parts: jnp.concatenate([_rows8(p) for p in parts], axis=0)
    small = flat([dmod, sums2[S_LN1W:S_LN1W + 1], sums2[S_LN1B:S_LN1B + 1], sums2[S_LN2W:S_LN2W + 1],
                  sums2[S_LN2B:S_LN2B + 1], d_ret_norm, _unpad_heads(d_bg_p), d_gla_norm, d_gate_w_full,
                  sums2[S_LOSS:S_LOSS + 1]])
    partial["w_in"] = r_in
    g2, swapped["w_in"] = _gather_rows(small, "gather_small", [r_in])
    tot = _sum_devices(g2)
    loss = 0.5 / D_MODEL * jnp.sum(tot[136:144])
    grad_b_ada = tot[0:48].reshape(1, 6 * D_MODEL)
    grad_ln1_w, grad_ln1_b = tot[48:56].reshape(1, D_MODEL), tot[56:64].reshape(1, D_MODEL)
    grad_ln2_w, grad_ln2_b = tot[64:72].reshape(1, D_MODEL), tot[72:80].reshape(1, D_MODEL)
    grad_ret_norm = tot[80:84].reshape(1, 512)
    grad_gate_b = tot[88:90].reshape(1, 256)
    grad_gla_norm = tot[96:100].reshape(1, 512)
    grad_gate_w = lax.dynamic_slice(tot[104:136].reshape(GATE_RANK, 256), (0, chip * gate_cols),
                                    (GATE_RANK, gate_cols))

    small_grads = [grad_b_ada, grad_ln1_w, grad_ln1_b, grad_ln2_w, grad_ln2_b, grad_ret_norm, grad_gate_b,
                   grad_gla_norm, grad_gate_w[None]]
    small_out = _adam_small(list(zip(
        [b_ada, ln1_w, ln1_b, ln2_w, ln2_b, ret_norm_w, gla_gate_b, gla_norm_w, gla_gate_w], small_grads,
        [m_b_ada, m_ln1_w, m_ln1_b, m_ln2_w, m_ln2_b, m_ret_norm_w, m_gla_gate_b, m_gla_norm_w, m_gla_gate_w],
        [v_b_ada, v_ln1_w, v_ln1_b, v_ln2_w, v_ln2_b, v_ret_norm_w, v_gla_gate_b, v_gla_norm_w, v_gla_gate_w])))
    sm_delta, sm_m, sm_v = [[o[k] for o in small_out] for k in range(3)]

    dmod_all = g2[:, 0:48].reshape(N_DEV, 6 * D_MODEL)
    dmod_blk = lax.dynamic_slice(dmod_all, (0, chip * ada_cols), (N_DEV, ada_cols))
    ada_out = _ada_bwd_adam(jnp.transpose(c_all), dmod_blk, w_ada[0], m_w_ada[0], v_w_ada[0])
    ada_g, ada_delta, ada_m, ada_v = [t[None] for t in ada_out]

    big = {}
    for n, w, m, v in zip(["w_in", "w_out", "w_ff1", "w_ff2"], [w_in, w_out, w_ff1, w_ff2],
                          [m_w_in, m_w_out, m_w_ff1, m_w_ff2], [v_w_in, v_w_out, v_w_ff1, v_w_ff2]):
        mine, theirs = partial[n], swapped[n]
        if n == "w_in":
            out = _adam_pair(_col_major(w), mine, theirs, _col_major(m), _col_major(v), "adam_" + n)
            big[n] = [jnp.transpose(t.reshape(t.shape[0], 1, t.shape[1]), (1, 2, 0)) for t in out]
        else:
            big[n] = [t[None] for t in _adam_pair(w[0], mine, theirs, m[0], v[0], "adam_" + n)]

    def assemble(ada, smalls, k):
        b_ada_o, ln1w_o, ln1b_o, ln2w_o, ln2b_o, ret_o, gb_o, gln_o, gw_o = smalls
        return [ada, b_ada_o, big["w_in"][k], ret_o, gw_o, gb_o, gln_o, big["w_out"][k], ln1w_o, ln1b_o,
                big["w_ff1"][k], big["w_ff2"][k], ln2w_o, ln2b_o]

    grads = assemble(ada_g, small_grads, 0)
    deltas = assemble(ada_delta, sm_delta, 1)
    new_m = assemble(ada_m, sm_m, 2)
    new_v = assemble(ada_v, sm_v, 3)
    return (loss, grad_x[None], *grads, *deltas, *new_m, *new_v)
```

```python
import numpy as np
import jax
import jax.numpy as jnp
from jax import lax
from jax.experimental import pallas as pl
from jax.experimental.pallas import tpu as pltpu

D_MODEL = 1024
D_FF = 4096
CHUNK = 64
N_HEADS = 4
HEAD_W = 128
GLA_DK = 64
GATE_RANK = 16
GATE_TAU = 16.0
LN_EPS = 1e-5
ALPHA = 2.0 ** 0.25
ROPE_BASE = 10000.0
RET_SCALE = float(HEAD_W) ** -0.5
GLA_SCALE = float(GLA_DK) ** -0.5

ADAM_LR = 0.001
ADAM_B1 = 0.9
ADAM_B2 = 0.999
ADAM_EPS = 1e-08
ADAM_WD = 0.01
ADAM_STEP = 10

OFF_RQ, OFF_RK, OFF_RV, OFF_RG = 0, 512, 1024, 1536
OFF_GQ, OFF_GK, OFF_GV, OFF_GG, OFF_LR = 2048, 2560, 3072, 3584, 4096
N_PROJ = 4224

N_DEV = 8
N_CHIP = 4
MESH = pl.DeviceIdType.MESH
MXU_DTYPE = jnp.bfloat16
WIRE_DTYPE = jnp.bfloat16
VMEM_LIMIT = 60 * 1024 * 1024
TOKEN_TILE = 256
INPROJ_TOKEN_TILE = 512
CHUNKS_PER_STEP = 8
CHUNKS_IN_LOCKSTEP = 4
GRAD_ROWS_PER_STEP = 1024
GRAD_TOKEN_TILE = 2048
ELEMENTWISE_COLS = 256
HIGHEST = lax.Precision.HIGHEST


def _mm(a, b):
    return jnp.dot(a.astype(MXU_DTYPE), b.astype(MXU_DTYPE), preferred_element_type=jnp.float32)


def _mm_nt(a, b):
    return lax.dot_general(a.astype(MXU_DTYPE), b.astype(MXU_DTYPE), (((1,), (1,)), ((), ())),
                           preferred_element_type=jnp.float32)


def _mm_tn(a, b):
    return lax.dot_general(a.astype(MXU_DTYPE), b.astype(MXU_DTYPE), (((0,), (0,)), ((), ())),
                           preferred_element_type=jnp.float32)


def _mm32(a, b):
    return jnp.dot(a, b, precision=HIGHEST, preferred_element_type=jnp.float32)


def _running_sum(mask, a):
    m = mask.astype(jnp.bfloat16)
    hi = a.astype(jnp.bfloat16)
    rest = a - hi.astype(jnp.float32)
    mid = rest.astype(jnp.bfloat16)
    lo = (rest - mid.astype(jnp.float32)).astype(jnp.bfloat16)
    dot = lambda t: jnp.dot(m, t, preferred_element_type=jnp.float32)
    return dot(hi) + dot(mid) + dot(lo)


def _rowmean(a):
    return jnp.mean(a, axis=-1, keepdims=True)


def _colsum(a):
    return jnp.sum(a, axis=0, keepdims=True)


def _ln(z):
    zc = z - _rowmean(z)
    rstd = lax.rsqrt(_rowmean(zc * zc) + LN_EPS)
    return zc * rstd, rstd


def _ln_bwd(dzh, zh, rstd):
    return rstd * (dzh - _rowmean(dzh) - zh * _rowmean(dzh * zh))


def _sigmoid(a):
    return 1.0 / (1.0 + jnp.exp(-a))


def _log_sigmoid(a):
    return jnp.minimum(a, 0.0) - jnp.log(1.0 + jnp.exp(-jnp.abs(a)))


def _swap_halves(a):
    return pltpu.roll(a, HEAD_W // 2, 1)


def _tri_masks():
    row = lax.broadcasted_iota(jnp.int32, (CHUNK, CHUNK), 0)
    col = lax.broadcasted_iota(jnp.int32, (CHUNK, CHUNK), 1)
    return row, col


def _const_spec(shape):
    zeros = (0,) * len(shape)
    return pl.BlockSpec(shape, lambda *_: zeros, pipeline_mode=pl.Buffered(1))


def _params(semantics):
    return pltpu.CompilerParams(dimension_semantics=semantics, vmem_limit_bytes=VMEM_LIMIT)


def _decay_tables():
    log_gamma = np.log(1.0 - 2.0 ** (-5.0 - np.arange(N_HEADS, dtype=np.float64)))
    idx = np.arange(CHUNK, dtype=np.float64)
    dist = np.abs(idx[:, None] - idx[None, :])
    intra = np.exp(log_gamma[:, None, None] * dist)
    kdec = np.exp(log_gamma[None, :] * (CHUNK - 1.0 - idx)[:, None])
    qdec = np.exp(log_gamma[None, :] * (idx + 1.0)[:, None])
    chunk_decay = np.exp(log_gamma * CHUNK)
    lanes = lambda t: np.repeat(t, HEAD_W, axis=1).astype(np.float32)
    return (jnp.asarray(intra.astype(np.float32)), jnp.asarray(lanes(qdec)), jnp.asarray(lanes(kdec)),
            [float(np.float32(v)) for v in chunk_decay])


def _rotary_tables(seq):
    half = HEAD_W // 2
    inv = 1.0 / (ROPE_BASE ** jnp.linspace(0.0, 1.0, half, dtype=jnp.float32))
    both = lambda t: jnp.concatenate([t, t], axis=-1)
    ang_a = jnp.arange(0, seq, CHUNK, dtype=jnp.float32)[:, None] * inv[None, :]
    rot_a = jnp.stack([both(jnp.cos(ang_a)), both(jnp.sin(ang_a))], axis=1)
    rot_a = jnp.pad(rot_a, ((0, 0), (0, 6), (0, 0)))
    ang_b = jnp.arange(CHUNK, dtype=jnp.float32)[:, None] * inv[None, :]
    cos_b, sin_b = both(jnp.cos(ang_b)), both(jnp.sin(ang_b))
    sign = jnp.concatenate([-jnp.ones((half,), jnp.float32), jnp.ones((half,), jnp.float32)])
    return rot_a, jnp.stack([cos_b, sin_b, cos_b * sign, sin_b * sign])


def _rotary_chunk(ra_ref, c, rb_ref):
    cos_a, sin_a = ra_ref[c, 0:1, :], ra_ref[c, 1:2, :]
    return cos_a * rb_ref[0] - sin_a * rb_ref[1], sin_a * rb_ref[2] + cos_a * rb_ref[3]


def _mesh_pos():
    return lax.axis_index("x"), lax.axis_index("y"), lax.axis_index("c")


def _flip(v, bit):
    return 1 - v if bit else v


def _gather_rows(sources, pieces, rows, name, swaps):
    n_src, n = len(sources), len(swaps)

    def body(*refs):
        src_refs, refs = refs[:n_src], refs[n_src:]
        out_ref = refs[n]
        v_sc = refs[1 + 2 * n]
        swap = _SiblingSwap(refs[:n], refs[1 + n:1 + 2 * n], refs[4 + 2 * n:])
        swap.start()
        v_sc[...] = jnp.zeros_like(v_sc)
        for s, row, first in pieces:
            for k in range(src_refs[s].shape[1] // 128):
                v_sc[first + k:first + k + 1, :] = src_refs[s][row:row + 1, k * 128:(k + 1) * 128]
        _all_devices_exchange(v_sc, out_ref, refs[2 + 2 * n], refs[3 + 2 * n])
        swap.wait()

    hbm = pl.BlockSpec(memory_space=pl.ANY)
    vmem = pl.BlockSpec(memory_space=pltpu.VMEM)
    return pl.pallas_call(
        body, name=name,
        out_shape=(jax.ShapeDtypeStruct((N_DEV, rows, 128), jnp.float32),)
        + tuple(jax.ShapeDtypeStruct(a.shape, a.dtype) for a in swaps),
        in_specs=[vmem] * n_src + [hbm] * n,
        out_specs=(vmem,) + (hbm,) * n,
        scratch_shapes=[pltpu.VMEM((rows, 128), jnp.float32)] + _all_devices_sems() + _swap_sems(n),
    )(*sources, *swaps)


def _all_devices_sems():
    return [pltpu.SemaphoreType.DMA((N_DEV - 1,)), pltpu.SemaphoreType.DMA((N_DEV - 1,))]


def _all_devices_exchange(v_ref, out_ref, send_sems, recv_sems):
    x, y, c = _mesh_pos()
    me = 4 * x + 2 * y + c
    out_ref[me] = v_ref[...]
    sends, recvs = [], []
    for k in range(1, N_DEV):
        px, py, pc = _flip(x, (k >> 2) & 1), _flip(y, (k >> 1) & 1), _flip(c, k & 1)
        peer = 4 * px + 2 * py + pc
        sends.append(pltpu.make_async_remote_copy(
            src_ref=v_ref, dst_ref=out_ref.at[me], send_sem=send_sems.at[k - 1], recv_sem=recv_sems.at[k - 1],
            device_id=(px, py, pc), device_id_type=MESH))
        recvs.append(pltpu.make_async_remote_copy(
            src_ref=v_ref, dst_ref=out_ref.at[peer], send_sem=send_sems.at[k - 1], recv_sem=recv_sems.at[k - 1],
            device_id=(px, py, pc), device_id_type=MESH))
    for cp in sends:
        cp.start()
    for cp in recvs:
        cp.wait_recv()
    for cp in sends:
        cp.wait_send()


def _prologue(cond_rows, w_ada_blk, b_blk, w_in_t):
    cols = w_ada_blk.shape[1]
    groups = cols // 128
    c_rows = D_MODEL // 128

    def body(cond_ref, w_ref, b_ref, win_ref, cond_all_ref, mod_all_ref, stack_ref, mod_sc, *sems):
        gather = _ChipGather([win_ref], [stack_ref], sems[:5])
        gather.start()
        _all_devices_exchange(cond_ref, cond_all_ref, sems[5], sems[6])
        acc = jnp.broadcast_to(b_ref[...], (N_DEV, cols))
        for r in range(c_rows):
            cv = cond_all_ref[:, r, :]
            acc = acc + _mm32(cv * _sigmoid(cv), w_ref[r * 128:(r + 1) * 128, :])
        for k in range(groups):
            mod_sc[k] = acc[:, k * 128:(k + 1) * 128]
        _all_devices_exchange(mod_sc, mod_all_ref, sems[7], sems[8])
        gather.forward()
        gather.finish()

    vmem = pl.BlockSpec(memory_space=pltpu.VMEM)
    hbm = pl.BlockSpec(memory_space=pl.ANY)
    return pl.pallas_call(
        body, name="prologue",
        out_shape=(jax.ShapeDtypeStruct((N_DEV,) + cond_rows.shape, jnp.float32),
                   jax.ShapeDtypeStruct((N_DEV, groups, N_DEV, 128), jnp.float32))
        + _exchange_out_shapes([w_in_t], True),
        in_specs=[vmem, vmem, vmem, hbm],
        out_specs=(vmem, vmem, hbm),
        scratch_shapes=[pltpu.VMEM((groups, N_DEV, 128), jnp.float32)] + _gather_sems(1)
        + _all_devices_sems() + _all_devices_sems(),
        compiler_params=pltpu.CompilerParams(vmem_limit_bytes=VMEM_LIMIT),
    )(cond_rows, w_ada_blk, b_blk, w_in_t)


def _exchange_out_shapes(arrays, gather):
    return tuple(jax.ShapeDtypeStruct((N_CHIP,) + a.shape if gather else a.shape, a.dtype) for a in arrays)


def _scatter_sems(n):
    n_sem = n * (N_CHIP - 1)
    return [pltpu.SemaphoreType.DMA((n_sem,)), pltpu.SemaphoreType.DMA((n_sem,)), pltpu.SemaphoreType.DMA((n,))]


def _gather_sems(n):
    n_sem = n * (N_CHIP - 1)
    return [pltpu.SemaphoreType.DMA((n_sem,))] * 4 + [pltpu.SemaphoreType.DMA((n,))]


def _peer_chips(x, y):
    out = []
    for k in range(1, N_CHIP):
        px, py = _flip(x, (k >> 1) & 1), _flip(y, k & 1)
        out.append((px, py, 2 * px + py))
    return out


class _ChipScatter:
    def __init__(self, ins, outs, sems):
        send_sems, recv_sems, local_sems = sems
        x, y, c = _mesh_pos()
        chip = 2 * x + y
        self.local, self.sends, self.recvs = [], [], []
        for i in range(len(ins)):
            self.local.append(pltpu.make_async_copy(ins[i].at[chip], outs[i].at[chip], local_sems.at[i]))
            for k, (px, py, peer_chip) in enumerate(_peer_chips(x, y)):
                sem = i * (N_CHIP - 1) + k
                src = ins[i].at[peer_chip]
                self.sends.append(pltpu.make_async_remote_copy(
                    src_ref=src, dst_ref=outs[i].at[chip], send_sem=send_sems.at[sem], recv_sem=recv_sems.at[sem],
                    device_id=(px, py, c), device_id_type=MESH))
                self.recvs.append(pltpu.make_async_remote_copy(
                    src_ref=src, dst_ref=outs[i].at[peer_chip], send_sem=send_sems.at[sem], recv_sem=recv_sems.at[sem],
                    device_id=(px, py, c), device_id_type=MESH))

    def start(self):
        for cp in self.local + self.sends:
            cp.start()

    def wait(self):
        for cp in self.recvs:
            cp.wait_recv()
        for cp in self.sends:
            cp.wait_send()
        for cp in self.local:
            cp.wait()


class _ChipGather:
    def __init__(self, ins, outs, sems):
        ici_send, ici_recv, d2d_send, d2d_recv, local_sems = sems
        x, y, c = _mesh_pos()
        chip = 2 * x + y
        self.local, self.ici_sends, self.ici_recvs, self.d2d_sends, self.d2d_recvs = [], [], [], [], []
        for i in range(len(ins)):
            half = ins[i].shape[-1] // 2
            assert half % 128 == 0
            lead = (slice(None),) * (len(ins[i].shape) - 1)
            mine = lead + (pl.ds(pl.multiple_of(c * half, 128), half),)
            theirs = lead + (pl.ds(pl.multiple_of((1 - c) * half, 128), half),)
            self.local.append(pltpu.make_async_copy(ins[i], outs[i].at[chip], local_sems.at[i]))
            for k, (px, py, peer_chip) in enumerate(_peer_chips(x, y)):
                sem = i * (N_CHIP - 1) + k
                self.ici_sends.append(pltpu.make_async_remote_copy(
                    src_ref=ins[i].at[mine], dst_ref=outs[i].at[chip].at[mine],
                    send_sem=ici_send.at[sem], recv_sem=ici_recv.at[sem], device_id=(px, py, c), device_id_type=MESH))
                landed = outs[i].at[peer_chip].at[mine]
                self.ici_recvs.append(pltpu.make_async_remote_copy(
                    src_ref=ins[i].at[mine], dst_ref=landed,
                    send_sem=ici_send.at[sem], recv_sem=ici_recv.at[sem], device_id=(px, py, c), device_id_type=MESH))
                self.d2d_sends.append(pltpu.make_async_remote_copy(
                    src_ref=landed, dst_ref=landed,
                    send_sem=d2d_send.at[sem], recv_sem=d2d_recv.at[sem], device_id=(x, y, 1 - c), device_id_type=MESH))
                self.d2d_recvs.append(pltpu.make_async_remote_copy(
                    src_ref=landed, dst_ref=outs[i].at[peer_chip].at[theirs],
                    send_sem=d2d_send.at[sem], recv_sem=d2d_recv.at[sem], device_id=(x, y, 1 - c), device_id_type=MESH))

    def start(self):
        for cp in self.local + self.ici_sends:
            cp.start()

    def forward(self):
        for landed, onward in zip(self.ici_recvs, self.d2d_sends):
            landed.wait_recv()
            onward.start()

    def finish(self):
        for cp in self.d2d_recvs:
            cp.wait_recv()
        for cp in self.d2d_sends + self.ici_sends:
            cp.wait_send()
        for cp in self.local:
            cp.wait()


def _swap_sems(n):
    return [pltpu.SemaphoreType.DMA((n,)), pltpu.SemaphoreType.DMA((n,))]


class _SiblingSwap:
    def __init__(self, ins, outs, sems):
        send_sems, recv_sems = sems
        x, y, c = _mesh_pos()
        self.copies = [pltpu.make_async_remote_copy(
            src_ref=ins[i], dst_ref=outs[i], send_sem=send_sems.at[i], recv_sem=recv_sems.at[i],
            device_id=(x, y, 1 - c), device_id_type=MESH) for i in range(len(ins))]

    def start(self):
        for cp in self.copies:
            cp.start()

    def wait(self):
        for cp in self.copies:
            cp.wait_recv()
        for cp in self.copies:
            cp.wait_send()


def _adam(w, g, m, v):
    m2 = ADAM_B1 * m + (1.0 - ADAM_B1) * g
    v2 = ADAM_B2 * v + (1.0 - ADAM_B2) * (g * g)
    m_hat = m2 / (1.0 - ADAM_B1 ** ADAM_STEP)
    v_hat = v2 / (1.0 - ADAM_B2 ** ADAM_STEP)
    delta = -ADAM_LR * (m_hat / (jnp.sqrt(v_hat) + ADAM_EPS) + ADAM_WD * w)
    return delta, m2, v2


def _ada_bwd_adam(c_t, dmod_blk, w, m, v):
    rows, cols = w.shape
    tile = 512
    assert cols % tile == 0

    def body(c_ref, d_ref, w_ref, m_ref, v_ref, g_ref, dl_ref, m2_ref, v2_ref):
        sc = c_ref[...]
        sc = sc * _sigmoid(sc)
        dm = d_ref[...]
        g = sc[:, 0:1] * dm[0:1, :]
        for b in range(1, N_DEV):
            g = g + sc[:, b:b + 1] * dm[b:b + 1, :]
        delta, m2, v2 = _adam(w_ref[...], g, m_ref[...], v_ref[...])
        g_ref[...] = g
        dl_ref[...] = delta
        m2_ref[...] = m2
        v2_ref[...] = v2

    blk = pl.BlockSpec((rows, tile), lambda j: (0, j))
    out = jax.ShapeDtypeStruct((rows, cols), jnp.float32)
    return pl.pallas_call(
        body, name="ada_bwd_adam", grid=(cols // tile,),
        out_shape=(out, out, out, out),
        in_specs=[pl.BlockSpec((rows, N_DEV), lambda j: (0, 0)), pl.BlockSpec((N_DEV, tile), lambda j: (0, j)),
                  blk, blk, blk],
        out_specs=(blk, blk, blk, blk),
        compiler_params=_params(("arbitrary",)),
    )(c_t, dmod_blk, w, m, v)


def _inproj_fwd(x2, vecs, w_in_p, tm, riders):
    seq = x2.shape[0]
    n_tiles = seq // tm
    n_ride = len(riders)

    def body(*refs):
        x_ref, vec_ref, w_ref = refs[:3]
        ride_in, refs = refs[3:3 + n_ride], refs[3 + n_ride:]
        p_ref, u_ref = refs[:2]
        ride_out, sems = refs[2:2 + n_ride], refs[2 + n_ride:]
        gather = _ChipGather(ride_in, ride_out, sems)

        @pl.when(pl.program_id(0) == 0)
        def _():
            gather.start()

        xh, _ = _ln(x_ref[...])
        u = (xh * (1.0 + vec_ref[1:2, :]) + vec_ref[0:1, :]).astype(MXU_DTYPE)
        u_ref[...] = u
        p_ref[...] = _mm(u, w_ref[...])

        @pl.when(pl.program_id(0) == (3 * n_tiles) // 4)
        def _():
            gather.forward()

        @pl.when(pl.program_id(0) == n_tiles - 1)
        def _():
            gather.finish()

    hbm = pl.BlockSpec(memory_space=pl.ANY)
    return pl.pallas_call(
        body, name="inproj_fwd", grid=(n_tiles,),
        out_shape=(jax.ShapeDtypeStruct((seq, N_PROJ), jnp.float32), jax.ShapeDtypeStruct((seq, D_MODEL), MXU_DTYPE))
        + _exchange_out_shapes(riders, True),
        in_specs=[pl.BlockSpec((tm, D_MODEL), lambda i: (i, 0)), _const_spec(vecs.shape), _const_spec(w_in_p.shape)]
        + [hbm] * n_ride,
        out_specs=(pl.BlockSpec((tm, N_PROJ), lambda i: (i, 0)), pl.BlockSpec((tm, D_MODEL), lambda i: (i, 0)))
        + (hbm,) * n_ride,
        scratch_shapes=_gather_sems(n_ride),
        compiler_params=_params(("arbitrary",)),
    )(x2, vecs, w_in_p, *riders)


def _inproj_bwd(dproj, x2, dxa, vecs, w_in_pt, tm, riders):
    seq = x2.shape[0]
    n_tiles = seq // tm
    n_ride = len(riders)

    def body(*refs):
        dp_ref, x_ref, dxa_ref, vec_ref, w_ref = refs[:5]
        ride_in, refs = refs[5:5 + n_ride], refs[5 + n_ride:]
        gx_ref, sums_ref = refs[:2]
        ride_out, sems = refs[2:2 + n_ride], refs[2 + n_ride:]
        exchange = _ChipScatter(ride_in, ride_out, sems)

        @pl.when(pl.program_id(0) == 0)
        def _():
            exchange.start()
            sums_ref[...] = jnp.zeros_like(sums_ref)

        du = _mm(dp_ref[...], w_ref[...])
        xh, rstd = _ln(x_ref[...])
        sums_ref[0:1, :] += _colsum(du)
        sums_ref[1:2, :] += _colsum(du * xh)
        gx_ref[...] = dxa_ref[...] + _ln_bwd(du * (1.0 + vec_ref[1:2, :]), xh, rstd)

        @pl.when(pl.program_id(0) == n_tiles - 1)
        def _():
            exchange.wait()

    tile = pl.BlockSpec((tm, D_MODEL), lambda i: (i, 0))
    hbm = pl.BlockSpec(memory_space=pl.ANY)
    return pl.pallas_call(
        body, name="inproj_bwd", grid=(n_tiles,),
        out_shape=(jax.ShapeDtypeStruct((seq, D_MODEL), jnp.float32), jax.ShapeDtypeStruct((8, D_MODEL), jnp.float32))
        + _exchange_out_shapes(riders, False),
        in_specs=[pl.BlockSpec((tm, N_PROJ), lambda i: (i, 0)), tile, tile, _const_spec(vecs.shape),
                  _const_spec(w_in_pt.shape)] + [hbm] * n_ride,
        out_specs=(tile, pl.BlockSpec((8, D_MODEL), lambda i: (0, 0))) + (hbm,) * n_ride,
        scratch_shapes=_scatter_sems(n_ride),
        compiler_params=_params(("arbitrary",)),
    )(dproj, x2, dxa, vecs, w_in_pt, *riders)


def _head(h):
    return slice(h * HEAD_W, (h + 1) * HEAD_W)


def _cols(ref, off, h):
    return ref[:, off + h * HEAD_W:off + (h + 1) * HEAD_W]


HEADS = range(N_HEADS)


def _mixer_chunk_forward(p_ref, cc, ss, dm_ref, qdec_ref, kdec_ref, wg_ref, bg_ref, states):
    row, col = _tri_masks()
    lower = row >= col
    f = {}
    f["glr"] = p_ref[:, OFF_LR:OFF_LR + HEAD_W]
    f["logit"] = _mm(f["glr"], wg_ref[...]) + bg_ref[...]
    rq = [_cols(p_ref, OFF_RQ, h) for h in HEADS]
    rk = [_cols(p_ref, OFF_RK, h) for h in HEADS]
    f["rv"] = [_cols(p_ref, OFF_RV, h) for h in HEADS]
    f["qr"] = [(rq[h] * cc + _swap_halves(rq[h]) * ss) * RET_SCALE for h in HEADS]
    f["kr"] = [rk[h] * cc + _swap_halves(rk[h]) * ss for h in HEADS]
    s_raw = [_mm_nt(f["qr"][h], f["kr"][h]) for h in HEADS]
    yield
    la = _log_sigmoid(f["logit"]) * (1.0 / GATE_TAU)
    b = _running_sum(lower, la)
    f["qd"] = [f["qr"][h] * qdec_ref[:, _head(h)] for h in HEADS]
    f["kd"] = [f["kr"][h] * kdec_ref[:, _head(h)] for h in HEADS]
    f["scores"] = [s_raw[h] * dm_ref[h] for h in HEADS]
    yield
    b_last = b[CHUNK - 1:CHUNK, :]
    b_mid = b[CHUNK // 2 - 1:CHUNK // 2, :]
    f["e"], f["ei"] = jnp.exp(b - b_mid), jnp.exp(b_mid - b)
    f["eb"], f["ek"], f["ebl"] = jnp.exp(b), jnp.exp(b_last - b), jnp.exp(b_last)
    gq = [_cols(p_ref, OFF_GQ, h) * GLA_SCALE for h in HEADS]
    gk = [_cols(p_ref, OFF_GK, h) for h in HEADS]
    f["gv"] = [_cols(p_ref, OFF_GV, h) for h in HEADS]
    f["q_e"] = [gq[h] * f["e"][:, _head(h)] for h in HEADS]
    f["q_i"] = [gq[h] * f["ei"][:, _head(h)] for h in HEADS]
    f["k_e"] = [gk[h] * f["e"][:, _head(h)] for h in HEADS]
    f["k_i"] = [gk[h] * f["ei"][:, _head(h)] for h in HEADS]
    low = [_mm_nt(f["q_e"][h], f["k_i"][h]) for h in HEADS]
    up = [_mm_nt(f["q_i"][h], f["k_e"][h]) for h in HEADS]
    yield
    f["att"] = [jnp.where(lower, low[h], up[h]) for h in HEADS]
    f["qb"] = [gq[h] * f["eb"][:, _head(h)] for h in HEADS]
    f["kb"] = [gk[h] * f["ek"][:, _head(h)] for h in HEADS]
    ret_state, gla_state_t = states()
    f["o_ret"] = [_mm(f["scores"][h], f["rv"][h]) + _mm(f["qd"][h], ret_state[h]) for h in HEADS]
    f["o_gla"] = [_mm(f["att"][h], f["gv"][h]) + _mm_nt(f["qb"][h], gla_state_t[h]) for h in HEADS]
    return f


def _interleave(generators):
    live = list(generators)
    while live:
        for g in list(live):
            try:
                next(g)
            except StopIteration:
                live.remove(g)


def _mixer_fwd(proj, tables, wg_p, bg_p, ret_norm_w, gla_norm_w, riders):
    seq = proj.shape[0]
    n_chunks = seq // CHUNK
    per_step = min(n_chunks, CHUNKS_PER_STEP)
    n_steps = n_chunks // per_step
    n_ride = len(riders)
    rot_a, rot_b, dm_t, qdec_t, kdec_t, chunk_decay = tables

    def body(*refs):
        p_ref, ra_ref, rb_ref, dm_ref, qdec_ref, kdec_ref, wg_ref, bg_ref, wr_ref, wl_ref = refs[:10]
        ride_in, refs = refs[10:10 + n_ride], refs[10 + n_ride:]
        mix_ref, rsave_ref, ssave_ref = refs[:3]
        ride_out, refs = refs[3:3 + n_ride], refs[3 + n_ride:]
        r_sc, s_sc = refs[:2]
        gather = _ChipGather(ride_in, ride_out, refs[2:])

        @pl.when(pl.program_id(0) == 0)
        def _():
            gather.start()
            r_sc[...] = jnp.zeros_like(r_sc)
            s_sc[...] = jnp.zeros_like(s_sc)

        def one_chunk(c):
            p_c = p_ref.at[c * CHUNK:(c + 1) * CHUNK, :]
            mix_c = mix_ref.at[c * CHUNK:(c + 1) * CHUNK, :]
            before = {}

            def states():
                before["ret"] = [r_sc[h] for h in HEADS]
                before["gla"] = [s_sc[h] for h in HEADS]
                for h in HEADS:
                    rsave_ref[c, h] = before["ret"][h].astype(rsave_ref.dtype)
                    ssave_ref[c, h] = before["gla"][h]
                return before["ret"], before["gla"]

            cc, ss = _rotary_chunk(ra_ref, c, rb_ref)
            f = yield from _mixer_chunk_forward(p_c, cc, ss, dm_ref, qdec_ref, kdec_ref, wg_ref, bg_ref, states)
            for h in HEADS:
                r_sc[h] = chunk_decay[h] * before["ret"][h] + _mm_tn(f["kd"][h], f["rv"][h])
            for h in HEADS:
                s_sc[h] = before["gla"][h] * f["ebl"][:, _head(h)] + _mm_tn(f["gv"][h], f["kb"][h])
            yield
            for h in HEADS:
                on, _ = _ln(f["o_ret"][h])
                g = _cols(p_c, OFF_RG, h)
                mix_c[:, _head(h)] = (on * wr_ref[:, _head(h)] * (g * _sigmoid(g))).astype(mix_ref.dtype)
            for h in HEADS:
                o = f["o_gla"][h]
                on = o * lax.rsqrt(_rowmean(o * o) + LN_EPS)
                g = _cols(p_c, OFF_GG, h)
                mix_c[:, _head(N_HEADS + h)] = (on * wl_ref[:, _head(h)] * (g * _sigmoid(g))).astype(mix_ref.dtype)

        for c0 in range(0, per_step, CHUNKS_IN_LOCKSTEP):
            _interleave([one_chunk(c) for c in range(c0, min(per_step, c0 + CHUNKS_IN_LOCKSTEP))])

        @pl.when(pl.program_id(0) == (3 * n_steps) // 4)
        def _():
            gather.forward()

        @pl.when(pl.program_id(0) == n_steps - 1)
        def _():
            gather.finish()

    state_shape = (n_chunks, N_HEADS, HEAD_W, HEAD_W)
    state_blk = pl.BlockSpec((per_step, N_HEADS, HEAD_W, HEAD_W), lambda i: (i, 0, 0, 0))
    rot_blk = pl.BlockSpec((per_step, 8, HEAD_W), lambda i: (i, 0, 0))
    rows = per_step * CHUNK
    hbm = pl.BlockSpec(memory_space=pl.ANY)
    return pl.pallas_call(
        body, name="mixer_fwd", grid=(n_steps,),
        out_shape=(jax.ShapeDtypeStruct((seq, D_MODEL), MXU_DTYPE),
                   jax.ShapeDtypeStruct(state_shape, MXU_DTYPE), jax.ShapeDtypeStruct(state_shape, jnp.float32))
        + _exchange_out_shapes(riders, True),
        in_specs=[pl.BlockSpec((rows, N_PROJ), lambda i: (i, 0)), rot_blk, _const_spec(rot_b.shape),
                  _const_spec(dm_t.shape), _const_spec(qdec_t.shape), _const_spec(kdec_t.shape),
                  _const_spec(wg_p.shape), _const_spec(bg_p.shape), _const_spec(ret_norm_w.shape),
                  _const_spec(gla_norm_w.shape)] + [hbm] * n_ride,
        out_specs=(pl.BlockSpec((rows, D_MODEL), lambda i: (i, 0)), state_blk, state_blk) + (hbm,) * n_ride,
        scratch_shapes=[pltpu.VMEM((N_HEADS, HEAD_W, HEAD_W), jnp.float32),
                        pltpu.VMEM((N_HEADS, HEAD_W, HEAD_W), jnp.float32)] + _gather_sems(n_ride),
        compiler_params=_params(("arbitrary",)),
    )(proj, rot_a, rot_b, dm_t, qdec_t, kdec_t, wg_p, bg_p, ret_norm_w, gla_norm_w, *riders)


def _mixer_bwd(proj, dmixed, rsave, ssave, tables, wg_p, bg_p, ret_norm_w, gla_norm_w, riders):
    seq = proj.shape[0]
    n_chunks = seq // CHUNK
    per_step = min(n_chunks, CHUNKS_PER_STEP)
    n_steps = n_chunks // per_step
    n_ride = len(riders)
    rot_a, rot_b, dm_t, qdec_t, kdec_t, chunk_decay = tables
    last = n_steps - 1

    def body(*refs):
        p_blk, dmx_blk = refs[:2]
        shared_in = refs[2:13]
        ride_in, refs = refs[13:13 + n_ride], refs[13 + n_ride:]
        dp_blk, dwr_ref, dwl_ref, dwg_ref, dbg_ref = refs[:5]
        ride_out, refs = refs[5:5 + n_ride], refs[5 + n_ride:]
        dr_sc, ds_sc = refs[:2]
        exchange = _ChipScatter(ride_in, ride_out, refs[2:])

        @pl.when(pl.program_id(0) == 0)
        def _():
            exchange.start()
            dr_sc[...] = jnp.zeros_like(dr_sc)
            ds_sc[...] = jnp.zeros_like(ds_sc)
            dwr_ref[...] = jnp.zeros_like(dwr_ref)
            dwl_ref[...] = jnp.zeros_like(dwl_ref)
            dwg_ref[...] = jnp.zeros_like(dwg_ref)
            dbg_ref[...] = jnp.zeros_like(dbg_ref)

        def chunk_stages(c):
            rows = slice(c * CHUNK, (c + 1) * CHUNK)
            return one_chunk(c, p_blk.at[rows, :], dmx_blk.at[rows, :], dp_blk.at[rows, :], *shared_in,
                             dwr_ref, dwl_ref, dwg_ref, dbg_ref, dr_sc, ds_sc)

        for c0 in range(per_step, 0, -CHUNKS_IN_LOCKSTEP):
            _interleave([chunk_stages(c) for c in reversed(range(max(0, c0 - CHUNKS_IN_LOCKSTEP), c0))])

        @pl.when(pl.program_id(0) == last)
        def _():
            exchange.wait()

    def one_chunk(c, p_ref, dmx_ref, dp_ref, rsave_ref, ssave_ref, ra_ref, rb_ref, dm_ref, qdec_ref, kdec_ref,
                  wg_ref, bg_ref, wr_ref, wl_ref, dwr_ref, dwl_ref, dwg_ref, dbg_ref, dr_sc, ds_sc):
        def put(off, h, val):
            dp_ref[:, off + h * HEAD_W:off + (h + 1) * HEAD_W] = val.astype(dp_ref.dtype)

        cc, ss = _rotary_chunk(ra_ref, c, rb_ref)
        row, col = _tri_masks()
        ret_state = [rsave_ref[c, h] for h in HEADS]
        gla_state_t = [ssave_ref[c, h] for h in HEADS]
        f = yield from _mixer_chunk_forward(p_ref, cc, ss, dm_ref, qdec_ref, kdec_ref, wg_ref, bg_ref,
                                            lambda: (ret_state, gla_state_t))
        yield

        do_ret, do_gla = [], []
        for h in HEADS:
            on, rstd = _ln(f["o_ret"][h])
            g = _cols(p_ref, OFF_RG, h)
            sg = _sigmoid(g)
            dy = dmx_ref[:, _head(h)].astype(jnp.float32)
            wr = wr_ref[:, _head(h)]
            dwr_ref[:, _head(h)] += _colsum(dy * on * (g * sg))
            put(OFF_RG, h, dy * on * wr * (sg * (1.0 + g * (1.0 - sg))))
            do_ret.append(_ln_bwd(dy * wr * (g * sg), on, rstd))
        for h in HEADS:
            o = f["o_gla"][h]
            rstd = lax.rsqrt(_rowmean(o * o) + LN_EPS)
            on = o * rstd
            g = _cols(p_ref, OFF_GG, h)
            sg = _sigmoid(g)
            dy = dmx_ref[:, _head(N_HEADS + h)].astype(jnp.float32)
            wl = wl_ref[:, _head(h)]
            dwl_ref[:, _head(h)] += _colsum(dy * on * (g * sg))
            put(OFF_GG, h, dy * on * wl * (sg * (1.0 + g * (1.0 - sg))))
            don = dy * wl * (g * sg)
            do_gla.append(rstd * (don - on * _rowmean(don * on)))

        yield

        d_ret_new = [dr_sc[h] for h in HEADS]
        d_gla_new = [ds_sc[h] for h in HEADS]
        ds_raw = [_mm_nt(do_ret[h], f["rv"][h]) * dm_ref[h] for h in HEADS]
        d_att = [_mm_nt(do_gla[h], f["gv"][h]) for h in HEADS]
        dq_state = [_mm_nt(do_ret[h], ret_state[h]) for h in HEADS]
        dk_state = [_mm_nt(f["rv"][h], d_ret_new[h]) for h in HEADS]
        dqb = [_mm(do_gla[h], gla_state_t[h]) for h in HEADS]
        dkb = [_mm(f["gv"][h], d_gla_new[h]) for h in HEADS]
        for h in HEADS:
            put(OFF_RV, h, _mm_tn(f["scores"][h], do_ret[h]) + _mm(f["kd"][h], d_ret_new[h]))
        for h in HEADS:
            put(OFF_GV, h, _mm_tn(f["att"][h], do_gla[h]) + _mm_nt(f["kb"][h], d_gla_new[h]))
        for h in HEADS:
            dr_sc[h] = chunk_decay[h] * d_ret_new[h] + _mm_tn(f["qd"][h], do_ret[h])
        for h in HEADS:
            ds_sc[h] = d_gla_new[h] * f["ebl"][:, _head(h)] + _mm_tn(do_gla[h], f["qb"][h])
        yield

        dqr = [_mm(ds_raw[h], f["kr"][h]) + dq_state[h] * qdec_ref[:, _head(h)] for h in HEADS]
        dkr = [_mm_tn(ds_raw[h], f["qr"][h]) + dk_state[h] * kdec_ref[:, _head(h)] for h in HEADS]
        d_low = [jnp.where(row >= col, d_att[h], 0.0) for h in HEADS]
        d_up = [jnp.where(row < col, d_att[h], 0.0) for h in HEADS]
        dq_e = [_mm(d_low[h], f["k_i"][h]) for h in HEADS]
        dk_i = [_mm_tn(d_low[h], f["q_e"][h]) for h in HEADS]
        dq_i = [_mm(d_up[h], f["k_e"][h]) for h in HEADS]
        dk_e = [_mm_tn(d_up[h], f["q_i"][h]) for h in HEADS]
        yield
        for h in HEADS:
            put(OFF_RQ, h, (dqr[h] * cc + _swap_halves(dqr[h] * ss)) * RET_SCALE)
            put(OFF_RK, h, dkr[h] * cc + _swap_halves(dkr[h] * ss))
        row_id = lax.broadcasted_iota(jnp.int32, (CHUNK, HEAD_W), 0)
        db_heads = []
        for h in HEADS:
            hs = _head(h)
            e, ei, eb, ek, ebl = f["e"][:, hs], f["ei"][:, hs], f["eb"][:, hs], f["ek"][:, hs], f["ebl"][:, hs]
            put(OFF_GQ, h, (dq_e[h] * e + dq_i[h] * ei + dqb[h] * eb) * GLA_SCALE)
            put(OFF_GK, h, dk_e[h] * e + dk_i[h] * ei + dkb[h] * ek)
            db = (dq_e[h] * f["q_e"][h] - dq_i[h] * f["q_i"][h] + dk_e[h] * f["k_e"][h] - dk_i[h] * f["k_i"][h]
                  + dqb[h] * f["qb"][h] - dkb[h] * f["kb"][h])
            db_last = _colsum(dkb[h] * f["kb"][h]) + ebl * _colsum(gla_state_t[h] * d_gla_new[h])
            db_heads.append(db + jnp.where(row_id == CHUNK - 1, db_last, 0.0))
        db = jnp.concatenate(db_heads, axis=1)
        d_la = _running_sum(col >= row, db)
        d_logit = d_la * (1.0 / GATE_TAU) * (1.0 - _sigmoid(f["logit"]))
        put(OFF_LR, 0, _mm_nt(d_logit, wg_ref[...]))
        dwg_ref[...] += _mm_tn(f["glr"], d_logit)
        dbg_ref[...] += _colsum(d_logit)

    state_blk = pl.BlockSpec((per_step, N_HEADS, HEAD_W, HEAD_W), lambda i: (last - i, 0, 0, 0))
    rot_blk = pl.BlockSpec((per_step, 8, HEAD_W), lambda i: (last - i, 0, 0))
    width = N_HEADS * HEAD_W
    vec_out = pl.BlockSpec((1, width), lambda i: (0, 0))
    hbm = pl.BlockSpec(memory_space=pl.ANY)
    rows_blk = per_step * CHUNK
    return pl.pallas_call(
        body, name="mixer_bwd", grid=(n_steps,),
        out_shape=(jax.ShapeDtypeStruct((seq, N_PROJ), MXU_DTYPE),
                   jax.ShapeDtypeStruct((1, width), jnp.float32), jax.ShapeDtypeStruct((1, width), jnp.float32),
                   jax.ShapeDtypeStruct((HEAD_W, width), jnp.float32), jax.ShapeDtypeStruct((1, width), jnp.float32))
        + _exchange_out_shapes(riders, False),
        in_specs=[pl.BlockSpec((rows_blk, N_PROJ), lambda i: (last - i, 0)),
                  pl.BlockSpec((rows_blk, D_MODEL), lambda i: (last - i, 0)), state_blk, state_blk, rot_blk,
                  _const_spec(rot_b.shape),
                  _const_spec(dm_t.shape), _const_spec(qdec_t.shape), _const_spec(kdec_t.shape),
                  _const_spec(wg_p.shape), _const_spec(bg_p.shape), _const_spec(ret_norm_w.shape),
                  _const_spec(gla_norm_w.shape)] + [hbm] * n_ride,
        out_specs=(pl.BlockSpec((rows_blk, N_PROJ), lambda i: (last - i, 0)), vec_out, vec_out,
                   pl.BlockSpec((HEAD_W, width), lambda i: (0, 0)), vec_out) + (hbm,) * n_ride,
        scratch_shapes=[pltpu.VMEM((N_HEADS, HEAD_W, HEAD_W), jnp.float32),
                        pltpu.VMEM((N_HEADS, HEAD_W, HEAD_W), jnp.float32)] + _scatter_sems(n_ride),
        compiler_params=_params(("arbitrary",)),
    )(proj, dmixed, rsave, ssave, rot_a, rot_b, dm_t, qdec_t, kdec_t, wg_p, bg_p, ret_norm_w, gla_norm_w, *riders)


V_GATE1, V_SCALE2, V_SHIFT2, V_GATE2, V_LN1W, V_LN1B, V_LN2W, V_LN2B = range(8)
S_GATE1, S_SCALE2, S_SHIFT2, S_GATE2, S_LN1W, S_LN1B, S_LN2W, S_LN2B, S_LOSS = range(9)


def _mlp_fwd_bwd(x2, mixed, target, vecs, w_out, w1_chunks, w2_chunks, tm):
    seq = x2.shape[0]
    n_fc, _, fc = w1_chunks.shape

    def body(x_ref, mx_ref, t_ref, vec_ref, wo_ref, w1_ref, w2_ref,
             dmx_ref, dxa_ref, a_ref, dh_ref, u2_ref, df_ref, dm_ref, sums_ref, relu_sc):
        @pl.when(pl.program_id(0) == 0)
        def _():
            sums_ref[...] = jnp.zeros_like(sums_ref)

        vec = lambda r: vec_ref[r:r + 1, :]

        def acc(r, val):
            sums_ref[r:r + 1, :] += _colsum(val)

        xx = x_ref[...]
        m = _mm(mx_ref[...], wo_ref[...])
        z1h, rstd1 = _ln(ALPHA * xx + vec(V_GATE1) * m)
        x1 = z1h * vec(V_LN1W) + vec(V_LN1B)
        x1h, rstd0 = _ln(x1)
        u2 = (x1h * (1.0 + vec(V_SCALE2)) + vec(V_SHIFT2)).astype(MXU_DTYPE)
        u2_ref[...] = u2
        f = jnp.zeros((tm, D_MODEL), jnp.float32)
        for j in range(n_fc):
            r = jnp.maximum(_mm(u2, w1_ref[j]), 0.0)
            relu_sc[:, j * fc:(j + 1) * fc] = r
            a = (r * r).astype(MXU_DTYPE)
            a_ref[:, j * fc:(j + 1) * fc] = a
            f = f + _mm(a, w2_ref[j])
        z2h, rstd2 = _ln(ALPHA * x1 + vec(V_GATE2) * f)
        err = z2h * vec(V_LN2W) + vec(V_LN2B) - t_ref[...]
        acc(S_LOSS, err * err)
        dy = err * (1.0 / D_MODEL)
        acc(S_LN2W, dy * z2h)
        acc(S_LN2B, dy)
        dz2 = _ln_bwd(dy * vec(V_LN2W), z2h, rstd2)
        acc(S_GATE2, dz2 * f)
        df = (vec(V_GATE2) * dz2).astype(MXU_DTYPE)
        df_ref[...] = df
        du2 = jnp.zeros((tm, D_MODEL), jnp.float32)
        for j in range(n_fc):
            dh = (_mm_nt(df, w2_ref[j]) * (2.0 * relu_sc[:, j * fc:(j + 1) * fc])).astype(MXU_DTYPE)
            dh_ref[:, j * fc:(j + 1) * fc] = dh
            du2 = du2 + _mm_nt(dh, w1_ref[j])
        acc(S_SCALE2, du2 * x1h)
        acc(S_SHIFT2, du2)
        dx1 = ALPHA * dz2 + _ln_bwd(du2 * (1.0 + vec(V_SCALE2)), x1h, rstd0)
        acc(S_LN1W, dx1 * z1h)
        acc(S_LN1B, dx1)
        dz1 = _ln_bwd(dx1 * vec(V_LN1W), z1h, rstd1)
        acc(S_GATE1, dz1 * m)
        dxa_ref[...] = ALPHA * dz1
        dm = (vec(V_GATE1) * dz1).astype(MXU_DTYPE)
        dm_ref[...] = dm
        dmx_ref[...] = _mm_nt(dm, wo_ref[...])

    tile = lambda width: pl.BlockSpec((tm, width), lambda i: (i, 0))
    f32 = lambda width: jax.ShapeDtypeStruct((seq, width), jnp.float32)
    b16 = lambda width: jax.ShapeDtypeStruct((seq, width), MXU_DTYPE)
    return pl.pallas_call(
        body, name="mlp_fwd_bwd", grid=(seq // tm,),
        out_shape=(f32(D_MODEL), f32(D_MODEL), b16(D_FF), b16(D_FF), b16(D_MODEL), b16(D_MODEL), b16(D_MODEL),
                   jax.ShapeDtypeStruct((16, D_MODEL), jnp.float32)),
        in_specs=[tile(D_MODEL), tile(D_MODEL), tile(D_MODEL), _const_spec(vecs.shape), _const_spec(w_out.shape),
                  _const_spec(w1_chunks.shape), _const_spec(w2_chunks.shape)],
        out_specs=(tile(D_MODEL), tile(D_MODEL), tile(D_FF), tile(D_FF), tile(D_MODEL), tile(D_MODEL),
                   tile(D_MODEL), pl.BlockSpec((16, D_MODEL), lambda i: (0, 0))),
        scratch_shapes=[pltpu.VMEM((tm, D_FF), jnp.float32)],
        compiler_params=_params(("arbitrary",)),
    )(x2, mixed, target, vecs, w_out, w1_chunks, w2_chunks)


def _grad_matmul(a, b, name, tn, blocks_are_rows, riders=()):
    seq, m_dim = a.shape
    n_dim = b.shape[1]
    tk = min(seq, GRAD_TOKEN_TILE)
    nk = seq // tk
    n_ride = len(riders)
    if blocks_are_rows:
        tm = m_dim // N_CHIP
        assert tn == n_dim
        per_step = N_CHIP if m_dim <= GRAD_ROWS_PER_STEP else 1
        grid = (N_CHIP // per_step, 1, nk)
        out_map = lambda i, j, k: (i, 0, 0)
    else:
        tm = m_dim
        assert tn * N_CHIP == n_dim
        per_step = 1
        grid = (1, N_CHIP, nk)
        out_map = lambda i, j, k: (j, 0, 0)
    n_blocks = grid[0] * grid[1]
    rows = per_step * tm

    def body(*refs):
        a_ref, b_ref = refs[:2]
        ride_in, refs = refs[2:2 + n_ride], refs[2 + n_ride:]
        o_ref = refs[0]
        ride_out, refs = refs[1:1 + n_ride], refs[1 + n_ride:]
        acc_sc = refs[0]
        exchange = _ChipScatter(ride_in, ride_out, refs[1:]) if n_ride else None
        block = pl.program_id(0) + pl.program_id(1)
        k = pl.program_id(2)

        if exchange is not None:
            @pl.when((block == 0) & (k == 0))
            def _():
                exchange.start()

        @pl.when(k == 0)
        def _():
            acc_sc[...] = jnp.zeros_like(acc_sc)

        acc_sc[...] += _mm_tn(a_ref[...], b_ref[...])

        @pl.when(k == nk - 1)
        def _():
            for p in range(per_step):
                o_ref[p] = acc_sc[p * tm:(p + 1) * tm, :].astype(o_ref.dtype)

        if exchange is not None:
            @pl.when((block == n_blocks - 1) & (k == nk - 1))
            def _():
                exchange.wait()

    hbm = pl.BlockSpec(memory_space=pl.ANY)
    out = pl.pallas_call(
        body, name=name, grid=grid,
        out_shape=(jax.ShapeDtypeStruct((N_CHIP, tm, tn), WIRE_DTYPE),) + _exchange_out_shapes(riders, False),
        in_specs=[pl.BlockSpec((tk, rows), lambda i, j, k: (k, i)), pl.BlockSpec((tk, tn), lambda i, j, k: (k, j))]
        + [hbm] * n_ride,
        out_specs=(pl.BlockSpec((per_step, tm, tn), out_map),) + (hbm,) * n_ride,
        scratch_shapes=[pltpu.VMEM((rows, tn), jnp.float32)] + (_scatter_sems(n_ride) if n_ride else []),
        compiler_params=_params(("arbitrary", "arbitrary", "arbitrary")),
    )(a, b, *riders)
    return out if n_ride else out[0]


def _grad_matmul_full(a, b, name, tm, riders):
    seq, m_dim = a.shape
    n_dim = b.shape[1]
    tk = min(seq, GRAD_TOKEN_TILE)
    nk = seq // tk
    n_blocks = m_dim // tm
    n_ride = len(riders)
    assert m_dim % tm == 0

    def body(*refs):
        a_ref, b_ref = refs[:2]
        ride_in, refs = refs[2:2 + n_ride], refs[2 + n_ride:]
        o_ref = refs[0]
        ride_out, refs = refs[1:1 + n_ride], refs[1 + n_ride:]
        acc_sc = refs[0]
        swap = _SiblingSwap(ride_in, ride_out, refs[1:])
        i, k = pl.program_id(0), pl.program_id(1)

        @pl.when((i == 0) & (k == 0))
        def _():
            swap.start()

        @pl.when(k == 0)
        def _():
            acc_sc[...] = jnp.zeros_like(acc_sc)

        acc_sc[...] += _mm_tn(a_ref[...], b_ref[...])

        @pl.when(k == nk - 1)
        def _():
            o_ref[...] = acc_sc[...].astype(o_ref.dtype)

        @pl.when((i == n_blocks - 1) & (k == nk - 1))
        def _():
            swap.wait()

    hbm = pl.BlockSpec(memory_space=pl.ANY)
    return pl.pallas_call(
        body, name=name, grid=(n_blocks, nk),
        out_shape=(jax.ShapeDtypeStruct((m_dim, n_dim), WIRE_DTYPE),)
        + tuple(jax.ShapeDtypeStruct(r.shape, r.dtype) for r in riders),
        in_specs=[pl.BlockSpec((tk, tm), lambda i, k: (k, i)), pl.BlockSpec((tk, n_dim), lambda i, k: (k, 0))]
        + [hbm] * n_ride,
        out_specs=(pl.BlockSpec((tm, n_dim), lambda i, k: (i, 0)),) + (hbm,) * n_ride,
        scratch_shapes=[pltpu.VMEM((tm, n_dim), jnp.float32)] + _swap_sems(n_ride),
        compiler_params=_params(("arbitrary", "arbitrary")),
    )(a, b, *riders)


def _adam_pair(w, g_mine, g_sibling, m, v, name):
    rows, cols = w.shape
    tc = min(cols, ELEMENTWISE_COLS)

    def total(ref):
        if len(ref.shape) == 2:
            return ref[...]
        acc = ref[0].astype(jnp.float32)
        for j in range(1, ref.shape[0]):
            acc = acc + ref[j].astype(jnp.float32)
        return acc

    def body(w_ref, ga_ref, gb_ref, m_ref, v_ref, g_ref, dl_ref, m2_ref, v2_ref):
        g = total(ga_ref) + total(gb_ref)
        delta, m2, v2 = _adam(w_ref[...], g, m_ref[...], v_ref[...])
        g_ref[...] = g
        dl_ref[...] = delta
        m2_ref[...] = m2
        v2_ref[...] = v2

    blk = pl.BlockSpec((rows, tc), lambda i: (0, i))
    g_blk = lambda a: blk if a.ndim == 2 else pl.BlockSpec((a.shape[0], rows, tc), lambda i: (0, 0, i))
    out = jax.ShapeDtypeStruct((rows, cols), jnp.float32)
    return pl.pallas_call(
        body, name=name, grid=(cols // tc,),
        out_shape=(out, out, out, out),
        in_specs=[blk, g_blk(g_mine), g_blk(g_sibling), blk, blk], out_specs=(blk,) * 4,
        compiler_params=_params(("arbitrary",)),
    )(w, g_mine, g_sibling, m, v)


def _sum_devices(gathered, layout):
    def body(g_ref, *o_refs):
        total = g_ref[0]
        for d in range(1, N_DEV):
            total = total + g_ref[d]
        for (first, (rows_out, cols_out)), o_ref in zip(layout, o_refs):
            per_row = cols_out // 128
            for r in range(rows_out):
                for k in range(per_row):
                    src = first + r * per_row + k
                    o_ref[r:r + 1, k * 128:(k + 1) * 128] = total[src:src + 1, :]
        tail = total[total.shape[0] - 8:, :]
        o_refs[-1][...] = jnp.full((1, 128), jnp.sum(tail), jnp.float32)

    out_shape = tuple(jax.ShapeDtypeStruct(shape, jnp.float32) for _, shape in layout)
    return pl.pallas_call(
        body, name="sum_devices",
        out_shape=out_shape + (jax.ShapeDtypeStruct((1, 128), jnp.float32),),
    )(gathered)


def _adam_small(params):
    n = len(params)

    def body(*refs):
        ins, outs = refs[:4 * n], refs[4 * n:]
        for i in range(n):
            w_ref, g_ref, m_ref, v_ref = ins[4 * i:4 * i + 4]
            delta, m2, v2 = _adam(w_ref[...], g_ref[...], m_ref[...], v_ref[...])
            outs[3 * i][...] = delta
            outs[3 * i + 1][...] = m2
            outs[3 * i + 2][...] = v2

    out_shape = tuple(jax.ShapeDtypeStruct(p[0].shape, jnp.float32) for p in params for _ in range(3))
    out = pl.pallas_call(body, name="adam_small", out_shape=out_shape)(*[t for p in params for t in p])
    return [out[3 * i:3 * i + 3] for i in range(n)]


def _pad_heads(w):
    lead = w.shape[:-1]
    w = w.reshape(lead + (N_HEADS, GLA_DK))
    w = jnp.pad(w, [(0, 0)] * len(lead) + [(0, 0), (0, HEAD_W - GLA_DK)])
    return w.reshape(lead + (N_HEADS * HEAD_W,))


def _unpad_heads(w):
    lead = w.shape[:-1]
    return w.reshape(lead + (N_HEADS, HEAD_W))[..., :GLA_DK].reshape(lead + (N_HEADS * GLA_DK,))


def _pad_head_rows(w):
    w = w.reshape(N_HEADS, GLA_DK, w.shape[-1])
    return jnp.pad(w, ((0, 0), (0, HEAD_W - GLA_DK), (0, 0))).reshape(N_HEADS * HEAD_W, w.shape[-1])


def _unpad_head_rows(w):
    return w.reshape(N_HEADS, HEAD_W, w.shape[-1])[:, :GLA_DK].reshape(N_HEADS * GLA_DK, w.shape[-1])


def _pad_w_in_rows(stack):
    w = stack.reshape(-1, stack.shape[-1])
    return jnp.concatenate([
        w[:2048], _pad_head_rows(w[2048:2304]), _pad_head_rows(w[2304:2560]), w[2560:3584],
        jnp.pad(w[3584:3600], ((0, HEAD_W - GATE_RANK), (0, 0)))], axis=0)


def _unpad_w_in_stack(g, per):
    segments = [(0, g[:2048]), (2048, _unpad_head_rows(g[OFF_GQ:OFF_GQ + 512])),
                (2304, _unpad_head_rows(g[OFF_GK:OFF_GK + 512])), (2560, g[OFF_GV:OFF_LR]),
                (3584, g[OFF_LR:OFF_LR + GATE_RANK])]
    blocks = []
    for j in range(N_CHIP):
        lo, hi = j * per, (j + 1) * per
        pieces = []
        for start, rows in segments:
            a, b = max(lo, start), min(hi, start + rows.shape[0])
            if a < b:
                pieces.append(rows[a - start:b - start])
        blocks.append(jnp.concatenate(pieces, axis=0))
    return jnp.stack(blocks)


def _col_major(w):
    return jnp.transpose(w, (2, 0, 1)).reshape(w.shape[2], w.shape[1])


def _rows128(a):
    return a.reshape(-1, 128)


def kernel(x, c, w_ada, b_ada, w_in, ret_norm_w, gla_gate_w, gla_gate_b, gla_norm_w, w_out, ln1_w, ln1_b, w_ff1, w_ff2, ln2_w, ln2_b, loss_target, m_w_ada, m_b_ada, m_w_in, m_ret_norm_w, m_gla_gate_w, m_gla_gate_b, m_gla_norm_w, m_w_out, m_ln1_w, m_ln1_b, m_w_ff1, m_w_ff2, m_ln2_w, m_ln2_b, v_w_ada, v_b_ada, v_w_in, v_ret_norm_w, v_gla_gate_w, v_gla_gate_b, v_gla_norm_w, v_w_out, v_ln1_w, v_ln1_b, v_w_ff1, v_w_ff2, v_ln2_w, v_ln2_b):
    seq = x.shape[1]
    tm = min(seq, TOKEN_TILE)
    tm_in = min(seq, INPROJ_TOKEN_TILE)
    xi, yi, ci = _mesh_pos()
    dev = 4 * xi + 2 * yi + ci
    chip = 2 * xi + yi
    x2, target = x[0], loss_target[0]
    ada_cols = w_ada.shape[2]
    in_cols = w_in.shape[2]
    gate_cols = gla_gate_w.shape[2]

    b_blk = lax.dynamic_slice(b_ada, (0, chip * ada_cols), (1, ada_cols))
    g0, g1, w_in_stack = _prologue(jnp.concatenate([_rows128(c), _rows128(gla_gate_w[0])], axis=0), w_ada[0], b_blk,
                                   _col_major(w_in.astype(WIRE_DTYPE)))
    c_all = g0[:, :8].reshape(N_DEV, D_MODEL)
    gate_w_full = jnp.concatenate([g0[2 * j, 8:16].reshape(GATE_RANK, gate_cols) for j in range(N_CHIP)], axis=1)
    wg_p = jnp.pad(_pad_heads(gate_w_full), ((0, HEAD_W - GATE_RANK), (0, 0)))
    bg_p = _pad_heads(gla_gate_b)
    mine = lax.dynamic_index_in_dim(g1, dev, axis=2, keepdims=False)
    mod = jnp.concatenate([mine[2 * j].reshape(1, ada_cols) for j in range(N_CHIP)], axis=1)
    shift1, scale1, gate1, shift2, scale2, gate2 = [mod[:, i * D_MODEL:(i + 1) * D_MODEL] for i in range(6)]
    w_in_pt = _pad_w_in_rows(w_in_stack).astype(MXU_DTYPE)
    w_in_p = jnp.transpose(w_in_pt)

    zeros_row = jnp.zeros((1, D_MODEL), jnp.float32)
    vecs1 = jnp.concatenate([shift1, scale1] + [zeros_row] * 6, axis=0)
    proj, u, w2_stack = _inproj_fwd(x2, vecs1, w_in_p, tm_in, [w_ff2[0].astype(WIRE_DTYPE)])
    rot_a, rot_b = _rotary_tables(seq)
    dm_t, qdec_t, kdec_t, chunk_decay = _decay_tables()
    tables = (rot_a, rot_b, dm_t, qdec_t, kdec_t, chunk_decay)
    mixed, rsave, ssave, w_out_stack, w1_stack = _mixer_fwd(
        proj, tables, wg_p, bg_p, ret_norm_w, gla_norm_w,
        [w_out[0].astype(WIRE_DTYPE), w_ff1[0].astype(WIRE_DTYPE)])
    w_out_full = w_out_stack.reshape(D_MODEL, D_MODEL).astype(MXU_DTYPE)
    w1_chunks = w1_stack.astype(MXU_DTYPE)
    w2_chunks = w2_stack.astype(MXU_DTYPE)

    vecs2 = jnp.concatenate([gate1, scale2, shift2, gate2, ln1_w, ln1_b, ln2_w, ln2_b], axis=0)
    dmixed, dxa, act, dh, u2, df, dm, sums2 = _mlp_fwd_bwd(x2, mixed, target, vecs2, w_out_full, w1_chunks,
                                                           w2_chunks, tm)

    g_out_stack = _grad_matmul(mixed, dm, "grad_w_out", D_MODEL, True)
    g_ff1_stack, r_out = _grad_matmul(u2, dh, "grad_w_ff1", D_FF // N_CHIP, False, [g_out_stack])
    g_ff2_stack = _grad_matmul(act, df, "grad_w_ff2", D_MODEL, True)
    dproj, d_ret_norm, d_gla_norm, d_wg_p, d_bg_p, r_ff1, r_ff2 = _mixer_bwd(
        proj, dmixed, rsave, ssave, tables, wg_p, bg_p, ret_norm_w, gla_norm_w, [g_ff1_stack, g_ff2_stack])
    early = ["w_out", "w_ff1", "w_ff2"]
    partial = dict(zip(early, [r_out, r_ff1, r_ff2]))
    g_in_t, *swapped_early = _grad_matmul_full(dproj, u, "grad_w_in", N_PROJ // 3, [partial[n] for n in early])
    swapped = dict(zip(early, swapped_early))
    g_in_stack = _unpad_w_in_stack(g_in_t, in_cols)
    grad_x, sums1, r_in = _inproj_bwd(dproj, x2, dxa, vecs1, w_in_pt, tm_in, [g_in_stack])

    sources = [sums1, sums2, d_ret_norm, _unpad_heads(d_bg_p), d_gla_norm, _unpad_heads(d_wg_p[:GATE_RANK])]
    pieces = [(0, 0, 0), (0, 1, 8), (1, S_GATE1, 16), (1, S_SHIFT2, 24), (1, S_SCALE2, 32), (1, S_GATE2, 40),
              (1, S_LN1W, 48), (1, S_LN1B, 56), (1, S_LN2W, 64), (1, S_LN2B, 72), (2, 0, 80), (3, 0, 88), (4, 0, 96)]
    pieces += [(5, r, 104 + 2 * r) for r in range(GATE_RANK)] + [(1, S_LOSS, 136)]
    partial["w_in"] = r_in
    g2, swapped["w_in"] = _gather_rows(sources, pieces, 144, "gather_small", [r_in])
    (grad_b_ada, grad_ln1_w, grad_ln1_b, grad_ln2_w, grad_ln2_b, grad_ret_norm, grad_gate_b, grad_gla_norm,
     grad_gate_w_full, loss_sum) = _sum_devices(g2, [
         (0, (1, 6 * D_MODEL)), (48, (1, D_MODEL)), (56, (1, D_MODEL)), (64, (1, D_MODEL)), (72, (1, D_MODEL)),
         (80, (1, 512)), (88, (1, 256)), (96, (1, 512)), (104, (GATE_RANK, 256))])
    loss = 0.5 / D_MODEL * loss_sum[0, 0]
    grad_gate_w = lax.dynamic_slice(grad_gate_w_full, (0, chip * gate_cols), (GATE_RANK, gate_cols))

    small_grads = [grad_b_ada, grad_ln1_w, grad_ln1_b, grad_ln2_w, grad_ln2_b, grad_ret_norm, grad_gate_b,
                   grad_gla_norm, grad_gate_w[None]]
    small_out = _adam_small(list(zip(
        [b_ada, ln1_w, ln1_b, ln2_w, ln2_b, ret_norm_w, gla_gate_b, gla_norm_w, gla_gate_w], small_grads,
        [m_b_ada, m_ln1_w, m_ln1_b, m_ln2_w, m_ln2_b, m_ret_norm_w, m_gla_gate_b, m_gla_norm_w, m_gla_gate_w],
        [v_b_ada, v_ln1_w, v_ln1_b, v_ln2_w, v_ln2_b, v_ret_norm_w, v_gla_gate_b, v_gla_norm_w, v_gla_gate_w])))
    sm_delta, sm_m, sm_v = [[o[k] for o in small_out] for k in range(3)]

    dmod_all = g2[:, 0:48].reshape(N_DEV, 6 * D_MODEL)
    dmod_blk = lax.dynamic_slice(dmod_all, (0, chip * ada_cols), (N_DEV, ada_cols))
    ada_out = _ada_bwd_adam(jnp.transpose(c_all), dmod_blk, w_ada[0], m_w_ada[0], v_w_ada[0])
    ada_g, ada_delta, ada_m, ada_v = [t[None] for t in ada_out]

    big = {}
    for n, w, m, v in zip(["w_in", "w_out", "w_ff1", "w_ff2"], [w_in, w_out, w_ff1, w_ff2],
                          [m_w_in, m_w_out, m_w_ff1, m_w_ff2], [v_w_in, v_w_out, v_w_ff1, v_w_ff2]):
        mine, theirs = partial[n], swapped[n]
        if n == "w_in":
            out = _adam_pair(_col_major(w), mine, theirs, _col_major(m), _col_major(v), "adam_" + n)
            big[n] = [jnp.transpose(t.reshape(t.shape[0], 1, t.shape[1]), (1, 2, 0)) for t in out]
        else:
            big[n] = [t[None] for t in _adam_pair(w[0], mine, theirs, m[0], v[0], "adam_" + n)]

    def assemble(ada, smalls, k):
        b_ada_o, ln1w_o, ln1b_o, ln2w_o, ln2b_o, ret_o, gb_o, gln_o, gw_o = smalls
        return [ada, b_ada_o, big["w_in"][k], ret_o, gw_o, gb_o, gln_o, big["w_out"][k], ln1w_o, ln1b_o,
                big["w_ff1"][k], big["w_ff2"][k], ln2w_o, ln2b_o]

    grads = assemble(ada_g, small_grads, 0)
    deltas = assemble(ada_delta, sm_delta, 1)
    new_m = assemble(ada_m, sm_m, 2)
    new_v = assemble(ada_v, sm_v, 3)
    return (loss, grad_x[None], *grads, *deltas, *new_m, *new_v)
```

```python
import numpy as np
import jax
import jax.numpy as jnp
from jax import lax
from jax.experimental import pallas as pl
from jax.experimental.pallas import tpu as pltpu

D_MODEL = 1024
D_FF = 4096
CHUNK = 64
N_HEADS = 4
HEAD_W = 128
GLA_DK = 64
GATE_RANK = 16
GATE_TAU = 16.0
LN_EPS = 1e-5
ALPHA = 2.0 ** 0.25
ROPE_BASE = 10000.0
RET_SCALE = float(HEAD_W) ** -0.5
GLA_SCALE = float(GLA_DK) ** -0.5

ADAM_LR = 0.001
ADAM_B1 = 0.9
ADAM_B2 = 0.999
ADAM_EPS = 1e-08
ADAM_WD = 0.01
ADAM_STEP = 10

OFF_RQ, OFF_RK, OFF_RV, OFF_RG = 0, 512, 1024, 1536
OFF_GQ, OFF_GK, OFF_GV, OFF_GG, OFF_LR = 2048, 2560, 3072, 3584, 4096
N_PROJ = 4224

N_DEV = 8
N_CHIP = 4
MESH = pl.DeviceIdType.MESH
MXU_DTYPE = jnp.bfloat16
WIRE_DTYPE = jnp.bfloat16
VMEM_LIMIT = 60 * 1024 * 1024
TOKEN_TILE = 256
INPROJ_TOKEN_TILE = 512
CHUNKS_PER_STEP = 8
CHUNKS_IN_LOCKSTEP = 4
GRAD_ROWS_PER_STEP = 1024
GRAD_TOKEN_TILE = 2048
ELEMENTWISE_COLS = 256
HIGHEST = lax.Precision.HIGHEST


def _mm(a, b):
    return jnp.dot(a.astype(MXU_DTYPE), b.astype(MXU_DTYPE), preferred_element_type=jnp.float32)


def _mm_nt(a, b):
    return lax.dot_general(a.astype(MXU_DTYPE), b.astype(MXU_DTYPE), (((1,), (1,)), ((), ())),
                           preferred_element_type=jnp.float32)


def _mm_tn(a, b):
    return lax.dot_general(a.astype(MXU_DTYPE), b.astype(MXU_DTYPE), (((0,), (0,)), ((), ())),
                           preferred_element_type=jnp.float32)


def _mm32(a, b):
    return jnp.dot(a, b, precision=HIGHEST, preferred_element_type=jnp.float32)


def _running_sum(mask, a):
    m = mask.astype(jnp.bfloat16)
    hi = a.astype(jnp.bfloat16)
    rest = a - hi.astype(jnp.float32)
    mid = rest.astype(jnp.bfloat16)
    lo = (rest - mid.astype(jnp.float32)).astype(jnp.bfloat16)
    dot = lambda t: jnp.dot(m, t, preferred_element_type=jnp.float32)
    return dot(hi) + dot(mid) + dot(lo)


def _rowmean(a):
    return jnp.mean(a, axis=-1, keepdims=True)


def _colsum(a):
    return jnp.sum(a, axis=0, keepdims=True)


def _ln(z):
    zc = z - _rowmean(z)
    rstd = lax.rsqrt(_rowmean(zc * zc) + LN_EPS)
    return zc * rstd, rstd


def _ln_bwd(dzh, zh, rstd):
    return rstd * (dzh - _rowmean(dzh) - zh * _rowmean(dzh * zh))


def _sigmoid(a):
    return 1.0 / (1.0 + jnp.exp(-a))


def _log_sigmoid(a):
    return jnp.minimum(a, 0.0) - jnp.log(1.0 + jnp.exp(-jnp.abs(a)))


def _swap_halves(a):
    return pltpu.roll(a, HEAD_W // 2, 1)


def _tri_masks():
    row = lax.broadcasted_iota(jnp.int32, (CHUNK, CHUNK), 0)
    col = lax.broadcasted_iota(jnp.int32, (CHUNK, CHUNK), 1)
    return row, col


def _const_spec(shape):
    zeros = (0,) * len(shape)
    return pl.BlockSpec(shape, lambda *_: zeros, pipeline_mode=pl.Buffered(1))


def _params(semantics):
    return pltpu.CompilerParams(dimension_semantics=semantics, vmem_limit_bytes=VMEM_LIMIT)


def _decay_tables():
    log_gamma = np.log(1.0 - 2.0 ** (-5.0 - np.arange(N_HEADS, dtype=np.float64)))
    idx = np.arange(CHUNK, dtype=np.float64)
    dist = np.abs(idx[:, None] - idx[None, :])
    intra = np.exp(log_gamma[:, None, None] * dist)
    kdec = np.exp(log_gamma[None, :] * (CHUNK - 1.0 - idx)[:, None])
    qdec = np.exp(log_gamma[None, :] * (idx + 1.0)[:, None])
    chunk_decay = np.exp(log_gamma * CHUNK)
    lanes = lambda t: np.repeat(t, HEAD_W, axis=1).astype(np.float32)
    return (jnp.asarray(intra.astype(np.float32)), jnp.asarray(lanes(qdec)), jnp.asarray(lanes(kdec)),
            [float(np.float32(v)) for v in chunk_decay])


def _rotary_tables(seq):
    half = HEAD_W // 2
    inv = 1.0 / (ROPE_BASE ** jnp.linspace(0.0, 1.0, half, dtype=jnp.float32))
    both = lambda t: jnp.concatenate([t, t], axis=-1)
    ang_a = jnp.arange(0, seq, CHUNK, dtype=jnp.float32)[:, None] * inv[None, :]
    rot_a = jnp.stack([both(jnp.cos(ang_a)), both(jnp.sin(ang_a))], axis=1)
    rot_a = jnp.pad(rot_a, ((0, 0), (0, 6), (0, 0)))
    ang_b = jnp.arange(CHUNK, dtype=jnp.float32)[:, None] * inv[None, :]
    cos_b, sin_b = both(jnp.cos(ang_b)), both(jnp.sin(ang_b))
    sign = jnp.concatenate([-jnp.ones((half,), jnp.float32), jnp.ones((half,), jnp.float32)])
    return rot_a, jnp.stack([cos_b, sin_b, cos_b * sign, sin_b * sign])


def _rotary_chunk(ra_ref, c, rb_ref):
    cos_a, sin_a = ra_ref[c, 0:1, :], ra_ref[c, 1:2, :]
    return cos_a * rb_ref[0] - sin_a * rb_ref[1], sin_a * rb_ref[2] + cos_a * rb_ref[3]


def _mesh_pos():
    return lax.axis_index("x"), lax.axis_index("y"), lax.axis_index("c")


def _flip(v, bit):
    return 1 - v if bit else v


def _gather_rows(sources, pieces, rows, name, swaps):
    n_src, n = len(sources), len(swaps)

    def body(*refs):
        src_refs, refs = refs[:n_src], refs[n_src:]
        out_ref = refs[n]
        v_sc = refs[1 + 2 * n]
        swap = _SiblingSwap(refs[:n], refs[1 + n:1 + 2 * n], refs[4 + 2 * n:])
        swap.start()
        v_sc[...] = jnp.zeros_like(v_sc)
        for s, row, first in pieces:
            for k in range(src_refs[s].shape[1] // 128):
                v_sc[first + k:first + k + 1, :] = src_refs[s][row:row + 1, k * 128:(k + 1) * 128]
        _all_devices_exchange(v_sc, out_ref, refs[2 + 2 * n], refs[3 + 2 * n])
        swap.wait()

    hbm = pl.BlockSpec(memory_space=pl.ANY)
    vmem = pl.BlockSpec(memory_space=pltpu.VMEM)
    return pl.pallas_call(
        body, name=name,
        out_shape=(jax.ShapeDtypeStruct((N_DEV, rows, 128), jnp.float32),)
        + tuple(jax.ShapeDtypeStruct(a.shape, a.dtype) for a in swaps),
        in_specs=[vmem] * n_src + [hbm] * n,
        out_specs=(vmem,) + (hbm,) * n,
        scratch_shapes=[pltpu.VMEM((rows, 128), jnp.float32)] + _all_devices_sems() + _swap_sems(n),
    )(*sources, *swaps)


def _all_devices_sems():
    return [pltpu.SemaphoreType.DMA((N_DEV - 1,)), pltpu.SemaphoreType.DMA((N_DEV - 1,))]


def _all_devices_exchange(v_ref, out_ref, send_sems, recv_sems):
    x, y, c = _mesh_pos()
    me = 4 * x + 2 * y + c
    out_ref[me] = v_ref[...]
    sends, recvs = [], []
    for k in range(1, N_DEV):
        px, py, pc = _flip(x, (k >> 2) & 1), _flip(y, (k >> 1) & 1), _flip(c, k & 1)
        peer = 4 * px + 2 * py + pc
        sends.append(pltpu.make_async_remote_copy(
            src_ref=v_ref, dst_ref=out_ref.at[me], send_sem=send_sems.at[k - 1], recv_sem=recv_sems.at[k - 1],
            device_id=(px, py, pc), device_id_type=MESH))
        recvs.append(pltpu.make_async_remote_copy(
            src_ref=v_ref, dst_ref=out_ref.at[peer], send_sem=send_sems.at[k - 1], recv_sem=recv_sems.at[k - 1],
            device_id=(px, py, pc), device_id_type=MESH))
    for cp in sends:
        cp.start()
    for cp in recvs:
        cp.wait_recv()
    for cp in sends:
        cp.wait_send()


def _prologue(cond_rows, w_ada_blk, b_blk, w_in_t):
    cols = w_ada_blk.shape[1]
    groups = cols // 128
    c_rows = D_MODEL // 128

    def body(cond_ref, w_ref, b_ref, win_ref, cond_all_ref, mod_all_ref, mod_ref, stack_ref, mod_sc, *sems):
        gather = _ChipGather([win_ref], [stack_ref], sems[:5])
        gather.start()
        _all_devices_exchange(cond_ref, cond_all_ref, sems[5], sems[6])
        acc = jnp.broadcast_to(b_ref[...], (N_DEV, cols))
        for r in range(c_rows):
            cv = cond_all_ref[:, r, :]
            acc = acc + _mm32(cv * _sigmoid(cv), w_ref[r * 128:(r + 1) * 128, :])
        for k in range(groups):
            mod_sc[k] = acc[:, k * 128:(k + 1) * 128]
        _all_devices_exchange(mod_sc, mod_all_ref, sems[7], sems[8])
        x, y, c = _mesh_pos()
        me = 4 * x + 2 * y + c
        for j in range(N_CHIP):
            for k in range(groups):
                lane = j * cols + k * 128
                mod_ref[:, lane:lane + 128] = mod_all_ref[2 * j, k, pl.ds(me, 1), :]
        gather.forward()
        gather.finish()

    vmem = pl.BlockSpec(memory_space=pltpu.VMEM)
    hbm = pl.BlockSpec(memory_space=pl.ANY)
    return pl.pallas_call(
        body, name="prologue",
        out_shape=(jax.ShapeDtypeStruct((N_DEV,) + cond_rows.shape, jnp.float32),
                   jax.ShapeDtypeStruct((N_DEV, groups, N_DEV, 128), jnp.float32),
                   jax.ShapeDtypeStruct((1, N_CHIP * cols), jnp.float32))
        + _exchange_out_shapes([w_in_t], True),
        in_specs=[vmem, vmem, vmem, hbm],
        out_specs=(vmem, vmem, vmem, hbm),
        scratch_shapes=[pltpu.VMEM((groups, N_DEV, 128), jnp.float32)] + _gather_sems(1)
        + _all_devices_sems() + _all_devices_sems(),
        compiler_params=pltpu.CompilerParams(vmem_limit_bytes=VMEM_LIMIT),
    )(cond_rows, w_ada_blk, b_blk, w_in_t)


def _exchange_out_shapes(arrays, gather):
    return tuple(jax.ShapeDtypeStruct((N_CHIP,) + a.shape if gather else a.shape, a.dtype) for a in arrays)


def _scatter_sems(n):
    n_sem = n * (N_CHIP - 1)
    return [pltpu.SemaphoreType.DMA((n_sem,)), pltpu.SemaphoreType.DMA((n_sem,)), pltpu.SemaphoreType.DMA((n,))]


def _gather_sems(n):
    n_sem = n * (N_CHIP - 1)
    return [pltpu.SemaphoreType.DMA((n_sem,))] * 4 + [pltpu.SemaphoreType.DMA((n,))]


def _peer_chips(x, y):
    out = []
    for k in range(1, N_CHIP):
        px, py = _flip(x, (k >> 1) & 1), _flip(y, k & 1)
        out.append((px, py, 2 * px + py))
    return out


class _ChipScatter:
    def __init__(self, ins, outs, sems):
        send_sems, recv_sems, local_sems = sems
        x, y, c = _mesh_pos()
        chip = 2 * x + y
        self.local, self.sends, self.recvs = [], [], []
        for i in range(len(ins)):
            self.local.append(pltpu.make_async_copy(ins[i].at[chip], outs[i].at[chip], local_sems.at[i]))
            for k, (px, py, peer_chip) in enumerate(_peer_chips(x, y)):
                sem = i * (N_CHIP - 1) + k
                src = ins[i].at[peer_chip]
                self.sends.append(pltpu.make_async_remote_copy(
                    src_ref=src, dst_ref=outs[i].at[chip], send_sem=send_sems.at[sem], recv_sem=recv_sems.at[sem],
                    device_id=(px, py, c), device_id_type=MESH))
                self.recvs.append(pltpu.make_async_remote_copy(
                    src_ref=src, dst_ref=outs[i].at[peer_chip], send_sem=send_sems.at[sem], recv_sem=recv_sems.at[sem],
                    device_id=(px, py, c), device_id_type=MESH))

    def start(self):
        for cp in self.local + self.sends:
            cp.start()

    def wait(self):
        for cp in self.recvs:
            cp.wait_recv()
        for cp in self.sends:
            cp.wait_send()
        for cp in self.local:
            cp.wait()


class _ChipGather:
    def __init__(self, ins, outs, sems):
        ici_send, ici_recv, d2d_send, d2d_recv, local_sems = sems
        x, y, c = _mesh_pos()
        chip = 2 * x + y
        self.local, self.ici_sends, self.ici_recvs, self.d2d_sends, self.d2d_recvs = [], [], [], [], []
        for i in range(len(ins)):
            half = ins[i].shape[-1] // 2
            assert half % 128 == 0
            lead = (slice(None),) * (len(ins[i].shape) - 1)
            mine = lead + (pl.ds(pl.multiple_of(c * half, 128), half),)
            theirs = lead + (pl.ds(pl.multiple_of((1 - c) * half, 128), half),)
            self.local.append(pltpu.make_async_copy(ins[i], outs[i].at[chip], local_sems.at[i]))
            for k, (px, py, peer_chip) in enumerate(_peer_chips(x, y)):
                sem = i * (N_CHIP - 1) + k
                self.ici_sends.append(pltpu.make_async_remote_copy(
                    src_ref=ins[i].at[mine], dst_ref=outs[i].at[chip].at[mine],
                    send_sem=ici_send.at[sem], recv_sem=ici_recv.at[sem], device_id=(px, py, c), device_id_type=MESH))
                landed = outs[i].at[peer_chip].at[mine]
                self.ici_recvs.append(pltpu.make_async_remote_copy(
                    src_ref=ins[i].at[mine], dst_ref=landed,
                    send_sem=ici_send.at[sem], recv_sem=ici_recv.at[sem], device_id=(px, py, c), device_id_type=MESH))
                self.d2d_sends.append(pltpu.make_async_remote_copy(
                    src_ref=landed, dst_ref=landed,
                    send_sem=d2d_send.at[sem], recv_sem=d2d_recv.at[sem], device_id=(x, y, 1 - c), device_id_type=MESH))
                self.d2d_recvs.append(pltpu.make_async_remote_copy(
                    src_ref=landed, dst_ref=outs[i].at[peer_chip].at[theirs],
                    send_sem=d2d_send.at[sem], recv_sem=d2d_recv.at[sem], device_id=(x, y, 1 - c), device_id_type=MESH))

    def start(self):
        for cp in self.local + self.ici_sends:
            cp.start()

    def forward(self):
        for landed, onward in zip(self.ici_recvs, self.d2d_sends):
            landed.wait_recv()
            onward.start()

    def finish(self):
        for cp in self.d2d_recvs:
            cp.wait_recv()
        for cp in self.d2d_sends + self.ici_sends:
            cp.wait_send()
        for cp in self.local:
            cp.wait()


def _swap_sems(n):
    return [pltpu.SemaphoreType.DMA((n,)), pltpu.SemaphoreType.DMA((n,))]


class _SiblingSwap:
    def __init__(self, ins, outs, sems):
        send_sems, recv_sems = sems
        x, y, c = _mesh_pos()
        self.copies = [pltpu.make_async_remote_copy(
            src_ref=ins[i], dst_ref=outs[i], send_sem=send_sems.at[i], recv_sem=recv_sems.at[i],
            device_id=(x, y, 1 - c), device_id_type=MESH) for i in range(len(ins))]

    def start(self):
        for cp in self.copies:
            cp.start()

    def wait(self):
        for cp in self.copies:
            cp.wait_recv()
        for cp in self.copies:
            cp.wait_send()


def _adam(w, g, m, v):
    m2 = ADAM_B1 * m + (1.0 - ADAM_B1) * g
    v2 = ADAM_B2 * v + (1.0 - ADAM_B2) * (g * g)
    m_hat = m2 / (1.0 - ADAM_B1 ** ADAM_STEP)
    v_hat = v2 / (1.0 - ADAM_B2 ** ADAM_STEP)
    delta = -ADAM_LR * (m_hat / (jnp.sqrt(v_hat) + ADAM_EPS) + ADAM_WD * w)
    return delta, m2, v2


def _ada_bwd_adam(c_t, dmod_blk, w, m, v):
    rows, cols = w.shape
    tile = 512
    assert cols % tile == 0

    def body(c_ref, d_ref, w_ref, m_ref, v_ref, g_ref, dl_ref, m2_ref, v2_ref):
        sc = c_ref[...]
        sc = sc * _sigmoid(sc)
        dm = d_ref[...]
        g = sc[:, 0:1] * dm[0:1, :]
        for b in range(1, N_DEV):
            g = g + sc[:, b:b + 1] * dm[b:b + 1, :]
        delta, m2, v2 = _adam(w_ref[...], g, m_ref[...], v_ref[...])
        g_ref[...] = g
        dl_ref[...] = delta
        m2_ref[...] = m2
        v2_ref[...] = v2

    blk = pl.BlockSpec((rows, tile), lambda j: (0, j))
    out = jax.ShapeDtypeStruct((rows, cols), jnp.float32)
    return pl.pallas_call(
        body, name="ada_bwd_adam", grid=(cols // tile,),
        out_shape=(out, out, out, out),
        in_specs=[pl.BlockSpec((rows, N_DEV), lambda j: (0, 0)), pl.BlockSpec((N_DEV, tile), lambda j: (0, j)),
                  blk, blk, blk],
        out_specs=(blk, blk, blk, blk),
        compiler_params=_params(("arbitrary",)),
    )(c_t, dmod_blk, w, m, v)


MOD_SHIFT1, MOD_SCALE1, MOD_GATE1, MOD_SHIFT2, MOD_SCALE2, MOD_GATE2 = range(6)


def _mod(mod_ref, segment):
    return mod_ref[:, segment * D_MODEL:(segment + 1) * D_MODEL]


def _inproj_fwd(x2, vecs, w_in_p, tm, riders):
    seq = x2.shape[0]
    n_tiles = seq // tm
    n_ride = len(riders)

    def body(*refs):
        x_ref, vec_ref, w_ref = refs[:3]
        ride_in, refs = refs[3:3 + n_ride], refs[3 + n_ride:]
        p_ref, u_ref = refs[:2]
        ride_out, sems = refs[2:2 + n_ride], refs[2 + n_ride:]
        gather = _ChipGather(ride_in, ride_out, sems)

        @pl.when(pl.program_id(0) == 0)
        def _():
            gather.start()

        xh, _ = _ln(x_ref[...])
        u = (xh * (1.0 + _mod(vec_ref, MOD_SCALE1)) + _mod(vec_ref, MOD_SHIFT1)).astype(MXU_DTYPE)
        u_ref[...] = u
        p_ref[...] = _mm(u, w_ref[...])

        @pl.when(pl.program_id(0) == (3 * n_tiles) // 4)
        def _():
            gather.forward()

        @pl.when(pl.program_id(0) == n_tiles - 1)
        def _():
            gather.finish()

    hbm = pl.BlockSpec(memory_space=pl.ANY)
    return pl.pallas_call(
        body, name="inproj_fwd", grid=(n_tiles,),
        out_shape=(jax.ShapeDtypeStruct((seq, N_PROJ), jnp.float32), jax.ShapeDtypeStruct((seq, D_MODEL), MXU_DTYPE))
        + _exchange_out_shapes(riders, True),
        in_specs=[pl.BlockSpec((tm, D_MODEL), lambda i: (i, 0)), _const_spec(vecs.shape), _const_spec(w_in_p.shape)]
        + [hbm] * n_ride,
        out_specs=(pl.BlockSpec((tm, N_PROJ), lambda i: (i, 0)), pl.BlockSpec((tm, D_MODEL), lambda i: (i, 0)))
        + (hbm,) * n_ride,
        scratch_shapes=_gather_sems(n_ride),
        compiler_params=_params(("arbitrary",)),
    )(x2, vecs, w_in_p, *riders)


def _inproj_bwd(dproj, x2, dxa, vecs, w_in_pt, tm, riders):
    seq = x2.shape[0]
    n_tiles = seq // tm
    n_ride = len(riders)

    def body(*refs):
        dp_ref, x_ref, dxa_ref, vec_ref, w_ref = refs[:5]
        ride_in, refs = refs[5:5 + n_ride], refs[5 + n_ride:]
        gx_ref, sums_ref = refs[:2]
        ride_out, sems = refs[2:2 + n_ride], refs[2 + n_ride:]
        exchange = _ChipScatter(ride_in, ride_out, sems)

        @pl.when(pl.program_id(0) == 0)
        def _():
            exchange.start()
            sums_ref[...] = jnp.zeros_like(sums_ref)

        du = _mm(dp_ref[...], w_ref[...])
        xh, rstd = _ln(x_ref[...])
        sums_ref[0:1, :] += _colsum(du)
        sums_ref[1:2, :] += _colsum(du * xh)
        gx_ref[...] = dxa_ref[...] + _ln_bwd(du * (1.0 + _mod(vec_ref, MOD_SCALE1)), xh, rstd)

        @pl.when(pl.program_id(0) == n_tiles - 1)
        def _():
            exchange.wait()

    tile = pl.BlockSpec((tm, D_MODEL), lambda i: (i, 0))
    hbm = pl.BlockSpec(memory_space=pl.ANY)
    return pl.pallas_call(
        body, name="inproj_bwd", grid=(n_tiles,),
        out_shape=(jax.ShapeDtypeStruct((seq, D_MODEL), jnp.float32), jax.ShapeDtypeStruct((8, D_MODEL), jnp.float32))
        + _exchange_out_shapes(riders, False),
        in_specs=[pl.BlockSpec((tm, N_PROJ), lambda i: (i, 0)), tile, tile, _const_spec(vecs.shape),
                  _const_spec(w_in_pt.shape)] + [hbm] * n_ride,
        out_specs=(tile, pl.BlockSpec((8, D_MODEL), lambda i: (0, 0))) + (hbm,) * n_ride,
        scratch_shapes=_scatter_sems(n_ride),
        compiler_params=_params(("arbitrary",)),
    )(dproj, x2, dxa, vecs, w_in_pt, *riders)


def _head(h):
    return slice(h * HEAD_W, (h + 1) * HEAD_W)


def _cols(ref, off, h):
    return ref[:, off + h * HEAD_W:off + (h + 1) * HEAD_W]


HEADS = range(N_HEADS)


def _mixer_chunk_forward(p_ref, cc, ss, dm_ref, qdec_ref, kdec_ref, wg_ref, bg_ref, states):
    row, col = _tri_masks()
    lower = row >= col
    f = {}
    f["glr"] = p_ref[:, OFF_LR:OFF_LR + HEAD_W]
    f["logit"] = _mm(f["glr"], wg_ref[...]) + bg_ref[...]
    rq = [_cols(p_ref, OFF_RQ, h) for h in HEADS]
    rk = [_cols(p_ref, OFF_RK, h) for h in HEADS]
    f["rv"] = [_cols(p_ref, OFF_RV, h) for h in HEADS]
    f["qr"] = [(rq[h] * cc + _swap_halves(rq[h]) * ss) * RET_SCALE for h in HEADS]
    f["kr"] = [rk[h] * cc + _swap_halves(rk[h]) * ss for h in HEADS]
    s_raw = [_mm_nt(f["qr"][h], f["kr"][h]) for h in HEADS]
    yield
    la = _log_sigmoid(f["logit"]) * (1.0 / GATE_TAU)
    b = _running_sum(lower, la)
    f["qd"] = [f["qr"][h] * qdec_ref[:, _head(h)] for h in HEADS]
    f["kd"] = [f["kr"][h] * kdec_ref[:, _head(h)] for h in HEADS]
    f["scores"] = [s_raw[h] * dm_ref[h] for h in HEADS]
    yield
    b_last = b[CHUNK - 1:CHUNK, :]
    b_mid = b[CHUNK // 2 - 1:CHUNK // 2, :]
    f["e"], f["ei"] = jnp.exp(b - b_mid), jnp.exp(b_mid - b)
    f["eb"], f["ek"], f["ebl"] = jnp.exp(b), jnp.exp(b_last - b), jnp.exp(b_last)
    gq = [_cols(p_ref, OFF_GQ, h) * GLA_SCALE for h in HEADS]
    gk = [_cols(p_ref, OFF_GK, h) for h in HEADS]
    f["gv"] = [_cols(p_ref, OFF_GV, h) for h in HEADS]
    f["q_e"] = [gq[h] * f["e"][:, _head(h)] for h in HEADS]
    f["q_i"] = [gq[h] * f["ei"][:, _head(h)] for h in HEADS]
    f["k_e"] = [gk[h] * f["e"][:, _head(h)] for h in HEADS]
    f["k_i"] = [gk[h] * f["ei"][:, _head(h)] for h in HEADS]
    low = [_mm_nt(f["q_e"][h], f["k_i"][h]) for h in HEADS]
    up = [_mm_nt(f["q_i"][h], f["k_e"][h]) for h in HEADS]
    yield
    f["att"] = [jnp.where(lower, low[h], up[h]) for h in HEADS]
    f["qb"] = [gq[h] * f["eb"][:, _head(h)] for h in HEADS]
    f["kb"] = [gk[h] * f["ek"][:, _head(h)] for h in HEADS]
    ret_state, gla_state_t = states()
    f["o_ret"] = [_mm(f["scores"][h], f["rv"][h]) + _mm(f["qd"][h], ret_state[h]) for h in HEADS]
    f["o_gla"] = [_mm(f["att"][h], f["gv"][h]) + _mm_nt(f["qb"][h], gla_state_t[h]) for h in HEADS]
    return f


def _interleave(generators):
    live = list(generators)
    while live:
        for g in list(live):
            try:
                next(g)
            except StopIteration:
                live.remove(g)


def _mixer_fwd(proj, tables, wg_p, bg_p, ret_norm_w, gla_norm_w, riders):
    seq = proj.shape[0]
    n_chunks = seq // CHUNK
    per_step = min(n_chunks, CHUNKS_PER_STEP)
    n_steps = n_chunks // per_step
    n_ride = len(riders)
    rot_a, rot_b, dm_t, qdec_t, kdec_t, chunk_decay = tables

    def body(*refs):
        p_ref, ra_ref, rb_ref, dm_ref, qdec_ref, kdec_ref, wg_ref, bg_ref, wr_ref, wl_ref = refs[:10]
        ride_in, refs = refs[10:10 + n_ride], refs[10 + n_ride:]
        mix_ref, rsave_ref, ssave_ref = refs[:3]
        ride_out, refs = refs[3:3 + n_ride], refs[3 + n_ride:]
        r_sc, s_sc = refs[:2]
        gather = _ChipGather(ride_in, ride_out, refs[2:])

        @pl.when(pl.program_id(0) == 0)
        def _():
            gather.start()
            r_sc[...] = jnp.zeros_like(r_sc)
            s_sc[...] = jnp.zeros_like(s_sc)

        def one_chunk(c):
            p_c = p_ref.at[c * CHUNK:(c + 1) * CHUNK, :]
            mix_c = mix_ref.at[c * CHUNK:(c + 1) * CHUNK, :]
            before = {}

            def states():
                before["ret"] = [r_sc[h] for h in HEADS]
                before["gla"] = [s_sc[h] for h in HEADS]
                for h in HEADS:
                    rsave_ref[c, h] = before["ret"][h].astype(rsave_ref.dtype)
                    ssave_ref[c, h] = before["gla"][h]
                return before["ret"], before["gla"]

            cc, ss = _rotary_chunk(ra_ref, c, rb_ref)
            f = yield from _mixer_chunk_forward(p_c, cc, ss, dm_ref, qdec_ref, kdec_ref, wg_ref, bg_ref, states)
            for h in HEADS:
                r_sc[h] = chunk_decay[h] * before["ret"][h] + _mm_tn(f["kd"][h], f["rv"][h])
            for h in HEADS:
                s_sc[h] = before["gla"][h] * f["ebl"][:, _head(h)] + _mm_tn(f["gv"][h], f["kb"][h])
            yield
            for h in HEADS:
                on, _ = _ln(f["o_ret"][h])
                g = _cols(p_c, OFF_RG, h)
                mix_c[:, _head(h)] = (on * wr_ref[:, _head(h)] * (g * _sigmoid(g))).astype(mix_ref.dtype)
            for h in HEADS:
                o = f["o_gla"][h]
                on = o * lax.rsqrt(_rowmean(o * o) + LN_EPS)
                g = _cols(p_c, OFF_GG, h)
                mix_c[:, _head(N_HEADS + h)] = (on * wl_ref[:, _head(h)] * (g * _sigmoid(g))).astype(mix_ref.dtype)

        for c0 in range(0, per_step, CHUNKS_IN_LOCKSTEP):
            _interleave([one_chunk(c) for c in range(c0, min(per_step, c0 + CHUNKS_IN_LOCKSTEP))])

        @pl.when(pl.program_id(0) == (3 * n_steps) // 4)
        def _():
            gather.forward()

        @pl.when(pl.program_id(0) == n_steps - 1)
        def _():
            gather.finish()

    state_shape = (n_chunks, N_HEADS, HEAD_W, HEAD_W)
    state_blk = pl.BlockSpec((per_step, N_HEADS, HEAD_W, HEAD_W), lambda i: (i, 0, 0, 0))
    rot_blk = pl.BlockSpec((per_step, 8, HEAD_W), lambda i: (i, 0, 0))
    rows = per_step * CHUNK
    hbm = pl.BlockSpec(memory_space=pl.ANY)
    return pl.pallas_call(
        body, name="mixer_fwd", grid=(n_steps,),
        out_shape=(jax.ShapeDtypeStruct((seq, D_MODEL), MXU_DTYPE),
                   jax.ShapeDtypeStruct(state_shape, MXU_DTYPE), jax.ShapeDtypeStruct(state_shape, jnp.float32))
        + _exchange_out_shapes(riders, True),
        in_specs=[pl.BlockSpec((rows, N_PROJ), lambda i: (i, 0)), rot_blk, _const_spec(rot_b.shape),
                  _const_spec(dm_t.shape), _const_spec(qdec_t.shape), _const_spec(kdec_t.shape),
                  _const_spec(wg_p.shape), _const_spec(bg_p.shape), _const_spec(ret_norm_w.shape),
                  _const_spec(gla_norm_w.shape)] + [hbm] * n_ride,
        out_specs=(pl.BlockSpec((rows, D_MODEL), lambda i: (i, 0)), state_blk, state_blk) + (hbm,) * n_ride,
        scratch_shapes=[pltpu.VMEM((N_HEADS, HEAD_W, HEAD_W), jnp.float32),
                        pltpu.VMEM((N_HEADS, HEAD_W, HEAD_W), jnp.float32)] + _gather_sems(n_ride),
        compiler_params=_params(("arbitrary",)),
    )(proj, rot_a, rot_b, dm_t, qdec_t, kdec_t, wg_p, bg_p, ret_norm_w, gla_norm_w, *riders)


def _mixer_bwd(proj, dmixed, rsave, ssave, tables, wg_p, bg_p, ret_norm_w, gla_norm_w, riders):
    seq = proj.shape[0]
    n_chunks = seq // CHUNK
    per_step = min(n_chunks, CHUNKS_PER_STEP)
    n_steps = n_chunks // per_step
    n_ride = len(riders)
    rot_a, rot_b, dm_t, qdec_t, kdec_t, chunk_decay = tables
    last = n_steps - 1

    def body(*refs):
        p_blk, dmx_blk = refs[:2]
        shared_in = refs[2:13]
        ride_in, refs = refs[13:13 + n_ride], refs[13 + n_ride:]
        dp_blk, dwr_ref, dwl_ref, dwg_ref, dbg_ref = refs[:5]
        ride_out, refs = refs[5:5 + n_ride], refs[5 + n_ride:]
        dr_sc, ds_sc = refs[:2]
        exchange = _ChipScatter(ride_in, ride_out, refs[2:])

        @pl.when(pl.program_id(0) == 0)
        def _():
            exchange.start()
            dr_sc[...] = jnp.zeros_like(dr_sc)
            ds_sc[...] = jnp.zeros_like(ds_sc)
            dwr_ref[...] = jnp.zeros_like(dwr_ref)
            dwl_ref[...] = jnp.zeros_like(dwl_ref)
            dwg_ref[...] = jnp.zeros_like(dwg_ref)
            dbg_ref[...] = jnp.zeros_like(dbg_ref)

        def chunk_stages(c):
            rows = slice(c * CHUNK, (c + 1) * CHUNK)
            return one_chunk(c, p_blk.at[rows, :], dmx_blk.at[rows, :], dp_blk.at[rows, :], *shared_in,
                             dwr_ref, dwl_ref, dwg_ref, dbg_ref, dr_sc, ds_sc)

        for c0 in range(per_step, 0, -CHUNKS_IN_LOCKSTEP):
            _interleave([chunk_stages(c) for c in reversed(range(max(0, c0 - CHUNKS_IN_LOCKSTEP), c0))])

        @pl.when(pl.program_id(0) == last)
        def _():
            exchange.wait()

    def one_chunk(c, p_ref, dmx_ref, dp_ref, rsave_ref, ssave_ref, ra_ref, rb_ref, dm_ref, qdec_ref, kdec_ref,
                  wg_ref, bg_ref, wr_ref, wl_ref, dwr_ref, dwl_ref, dwg_ref, dbg_ref, dr_sc, ds_sc):
        def put(off, h, val):
            dp_ref[:, off + h * HEAD_W:off + (h + 1) * HEAD_W] = val.astype(dp_ref.dtype)

        cc, ss = _rotary_chunk(ra_ref, c, rb_ref)
        row, col = _tri_masks()
        ret_state = [rsave_ref[c, h] for h in HEADS]
        gla_state_t = [ssave_ref[c, h] for h in HEADS]
        f = yield from _mixer_chunk_forward(p_ref, cc, ss, dm_ref, qdec_ref, kdec_ref, wg_ref, bg_ref,
                                            lambda: (ret_state, gla_state_t))
        yield

        do_ret, do_gla = [], []
        for h in HEADS:
            on, rstd = _ln(f["o_ret"][h])
            g = _cols(p_ref, OFF_RG, h)
            sg = _sigmoid(g)
            dy = dmx_ref[:, _head(h)].astype(jnp.float32)
            wr = wr_ref[:, _head(h)]
            dwr_ref[:, _head(h)] += _colsum(dy * on * (g * sg))
            put(OFF_RG, h, dy * on * wr * (sg * (1.0 + g * (1.0 - sg))))
            do_ret.append(_ln_bwd(dy * wr * (g * sg), on, rstd))
        for h in HEADS:
            o = f["o_gla"][h]
            rstd = lax.rsqrt(_rowmean(o * o) + LN_EPS)
            on = o * rstd
            g = _cols(p_ref, OFF_GG, h)
            sg = _sigmoid(g)
            dy = dmx_ref[:, _head(N_HEADS + h)].astype(jnp.float32)
            wl = wl_ref[:, _head(h)]
            dwl_ref[:, _head(h)] += _colsum(dy * on * (g * sg))
            put(OFF_GG, h, dy * on * wl * (sg * (1.0 + g * (1.0 - sg))))
            don = dy * wl * (g * sg)
            do_gla.append(rstd * (don - on * _rowmean(don * on)))

        yield

        d_ret_new = [dr_sc[h] for h in HEADS]
        d_gla_new = [ds_sc[h] for h in HEADS]
        ds_raw = [_mm_nt(do_ret[h], f["rv"][h]) * dm_ref[h] for h in HEADS]
        d_att = [_mm_nt(do_gla[h], f["gv"][h]) for h in HEADS]
        dq_state = [_mm_nt(do_ret[h], ret_state[h]) for h in HEADS]
        dk_state = [_mm_nt(f["rv"][h], d_ret_new[h]) for h in HEADS]
        dqb = [_mm(do_gla[h], gla_state_t[h]) for h in HEADS]
        dkb = [_mm(f["gv"][h], d_gla_new[h]) for h in HEADS]
        for h in HEADS:
            put(OFF_RV, h, _mm_tn(f["scores"][h], do_ret[h]) + _mm(f["kd"][h], d_ret_new[h]))
        for h in HEADS:
            put(OFF_GV, h, _mm_tn(f["att"][h], do_gla[h]) + _mm_nt(f["kb"][h], d_gla_new[h]))
        for h in HEADS:
            dr_sc[h] = chunk_decay[h] * d_ret_new[h] + _mm_tn(f["qd"][h], do_ret[h])
        for h in HEADS:
            ds_sc[h] = d_gla_new[h] * f["ebl"][:, _head(h)] + _mm_tn(do_gla[h], f["qb"][h])
        yield

        dqr = [_mm(ds_raw[h], f["kr"][h]) + dq_state[h] * qdec_ref[:, _head(h)] for h in HEADS]
        dkr = [_mm_tn(ds_raw[h], f["qr"][h]) + dk_state[h] * kdec_ref[:, _head(h)] for h in HEADS]
        d_low = [jnp.where(row >= col, d_att[h], 0.0) for h in HEADS]
        d_up = [jnp.where(row < col, d_att[h], 0.0) for h in HEADS]
        dq_e = [_mm(d_low[h], f["k_i"][h]) for h in HEADS]
        dk_i = [_mm_tn(d_low[h], f["q_e"][h]) for h in HEADS]
        dq_i = [_mm(d_up[h], f["k_e"][h]) for h in HEADS]
        dk_e = [_mm_tn(d_up[h], f["q_i"][h]) for h in HEADS]
        yield
        for h in HEADS:
            put(OFF_RQ, h, (dqr[h] * cc + _swap_halves(dqr[h] * ss)) * RET_SCALE)
            put(OFF_RK, h, dkr[h] * cc + _swap_halves(dkr[h] * ss))
        row_id = lax.broadcasted_iota(jnp.int32, (CHUNK, HEAD_W), 0)
        db_heads = []
        for h in HEADS:
            hs = _head(h)
            e, ei, eb, ek, ebl = f["e"][:, hs], f["ei"][:, hs], f["eb"][:, hs], f["ek"][:, hs], f["ebl"][:, hs]
            put(OFF_GQ, h, (dq_e[h] * e + dq_i[h] * ei + dqb[h] * eb) * GLA_SCALE)
            put(OFF_GK, h, dk_e[h] * e + dk_i[h] * ei + dkb[h] * ek)
            db = (dq_e[h] * f["q_e"][h] - dq_i[h] * f["q_i"][h] + dk_e[h] * f["k_e"][h] - dk_i[h] * f["k_i"][h]
                  + dqb[h] * f["qb"][h] - dkb[h] * f["kb"][h])
            db_last = _colsum(dkb[h] * f["kb"][h]) + ebl * _colsum(gla_state_t[h] * d_gla_new[h])
            db_heads.append(db + jnp.where(row_id == CHUNK - 1, db_last, 0.0))
        db = jnp.concatenate(db_heads, axis=1)
        d_la = _running_sum(col >= row, db)
        d_logit = d_la * (1.0 / GATE_TAU) * (1.0 - _sigmoid(f["logit"]))
        put(OFF_LR, 0, _mm_nt(d_logit, wg_ref[...]))
        dwg_ref[...] += _mm_tn(f["glr"], d_logit)
        dbg_ref[...] += _colsum(d_logit)

    state_blk = pl.BlockSpec((per_step, N_HEADS, HEAD_W, HEAD_W), lambda i: (last - i, 0, 0, 0))
    rot_blk = pl.BlockSpec((per_step, 8, HEAD_W), lambda i: (last - i, 0, 0))
    width = N_HEADS * HEAD_W
    vec_out = pl.BlockSpec((1, width), lambda i: (0, 0))
    hbm = pl.BlockSpec(memory_space=pl.ANY)
    rows_blk = per_step * CHUNK
    return pl.pallas_call(
        body, name="mixer_bwd", grid=(n_steps,),
        out_shape=(jax.ShapeDtypeStruct((seq, N_PROJ), MXU_DTYPE),
                   jax.ShapeDtypeStruct((1, width), jnp.float32), jax.ShapeDtypeStruct((1, width), jnp.float32),
                   jax.ShapeDtypeStruct((HEAD_W, width), jnp.float32), jax.ShapeDtypeStruct((1, width), jnp.float32))
        + _exchange_out_shapes(riders, False),
        in_specs=[pl.BlockSpec((rows_blk, N_PROJ), lambda i: (last - i, 0)),
                  pl.BlockSpec((rows_blk, D_MODEL), lambda i: (last - i, 0)), state_blk, state_blk, rot_blk,
                  _const_spec(rot_b.shape),
                  _const_spec(dm_t.shape), _const_spec(qdec_t.shape), _const_spec(kdec_t.shape),
                  _const_spec(wg_p.shape), _const_spec(bg_p.shape), _const_spec(ret_norm_w.shape),
                  _const_spec(gla_norm_w.shape)] + [hbm] * n_ride,
        out_specs=(pl.BlockSpec((rows_blk, N_PROJ), lambda i: (last - i, 0)), vec_out, vec_out,
                   pl.BlockSpec((HEAD_W, width), lambda i: (0, 0)), vec_out) + (hbm,) * n_ride,
        scratch_shapes=[pltpu.VMEM((N_HEADS, HEAD_W, HEAD_W), jnp.float32),
                        pltpu.VMEM((N_HEADS, HEAD_W, HEAD_W), jnp.float32)] + _scatter_sems(n_ride),
        compiler_params=_params(("arbitrary",)),
    )(proj, dmixed, rsave, ssave, rot_a, rot_b, dm_t, qdec_t, kdec_t, wg_p, bg_p, ret_norm_w, gla_norm_w, *riders)


V_GATE1, V_SCALE2, V_SHIFT2, V_GATE2, V_LN1W, V_LN1B, V_LN2W, V_LN2B = range(8)
S_GATE1, S_SCALE2, S_SHIFT2, S_GATE2, S_LN1W, S_LN1B, S_LN2W, S_LN2B, S_LOSS = range(9)


def _mlp_fwd_bwd(x2, mixed, target, mod, ln_rows, w_out, w1_chunks, w2_chunks, tm):
    seq = x2.shape[0]
    n_fc, _, fc = w1_chunks.shape
    segment_of = {V_GATE1: MOD_GATE1, V_SCALE2: MOD_SCALE2, V_SHIFT2: MOD_SHIFT2, V_GATE2: MOD_GATE2}

    def body(x_ref, mx_ref, t_ref, mod_ref, ln_ref, wo_ref, w1_ref, w2_ref,
             dmx_ref, dxa_ref, a_ref, dh_ref, u2_ref, df_ref, dm_ref, sums_ref, relu_sc):
        @pl.when(pl.program_id(0) == 0)
        def _():
            sums_ref[...] = jnp.zeros_like(sums_ref)

        def vec(r):
            if r in segment_of:
                return _mod(mod_ref, segment_of[r])
            return ln_ref[r - V_LN1W:r - V_LN1W + 1, :]

        def acc(r, val):
            sums_ref[r:r + 1, :] += _colsum(val)

        xx = x_ref[...]
        m = _mm(mx_ref[...], wo_ref[...])
        z1h, rstd1 = _ln(ALPHA * xx + vec(V_GATE1) * m)
        x1 = z1h * vec(V_LN1W) + vec(V_LN1B)
        x1h, rstd0 = _ln(x1)
        u2 = (x1h * (1.0 + vec(V_SCALE2)) + vec(V_SHIFT2)).astype(MXU_DTYPE)
        u2_ref[...] = u2
        f = jnp.zeros((tm, D_MODEL), jnp.float32)
        for j in range(n_fc):
            r = jnp.maximum(_mm(u2, w1_ref[j]), 0.0)
            relu_sc[:, j * fc:(j + 1) * fc] = r
            a = (r * r).astype(MXU_DTYPE)
            a_ref[:, j * fc:(j + 1) * fc] = a
            f = f + _mm(a, w2_ref[j])
        z2h, rstd2 = _ln(ALPHA * x1 + vec(V_GATE2) * f)
        err = z2h * vec(V_LN2W) + vec(V_LN2B) - t_ref[...]
        acc(S_LOSS, err * err)
        dy = err * (1.0 / D_MODEL)
        acc(S_LN2W, dy * z2h)
        acc(S_LN2B, dy)
        dz2 = _ln_bwd(dy * vec(V_LN2W), z2h, rstd2)
        acc(S_GATE2, dz2 * f)
        df = (vec(V_GATE2) * dz2).astype(MXU_DTYPE)
        df_ref[...] = df
        du2 = jnp.zeros((tm, D_MODEL), jnp.float32)
        for j in range(n_fc):
            dh = (_mm_nt(df, w2_ref[j]) * (2.0 * relu_sc[:, j * fc:(j + 1) * fc])).astype(MXU_DTYPE)
            dh_ref[:, j * fc:(j + 1) * fc] = dh
            du2 = du2 + _mm_nt(dh, w1_ref[j])
        acc(S_SCALE2, du2 * x1h)
        acc(S_SHIFT2, du2)
        dx1 = ALPHA * dz2 + _ln_bwd(du2 * (1.0 + vec(V_SCALE2)), x1h, rstd0)
        acc(S_LN1W, dx1 * z1h)
        acc(S_LN1B, dx1)
        dz1 = _ln_bwd(dx1 * vec(V_LN1W), z1h, rstd1)
        acc(S_GATE1, dz1 * m)
        dxa_ref[...] = ALPHA * dz1
        dm = (vec(V_GATE1) * dz1).astype(MXU_DTYPE)
        dm_ref[...] = dm
        dmx_ref[...] = _mm_nt(dm, wo_ref[...])

    tile = lambda width: pl.BlockSpec((tm, width), lambda i: (i, 0))
    f32 = lambda width: jax.ShapeDtypeStruct((seq, width), jnp.float32)
    b16 = lambda width: jax.ShapeDtypeStruct((seq, width), MXU_DTYPE)
    return pl.pallas_call(
        body, name="mlp_fwd_bwd", grid=(seq // tm,),
        out_shape=(f32(D_MODEL), f32(D_MODEL), b16(D_FF), b16(D_FF), b16(D_MODEL), b16(D_MODEL), b16(D_MODEL),
                   jax.ShapeDtypeStruct((16, D_MODEL), jnp.float32)),
        in_specs=[tile(D_MODEL), tile(D_MODEL), tile(D_MODEL), _const_spec(mod.shape), _const_spec(ln_rows.shape),
                  _const_spec(w_out.shape), _const_spec(w1_chunks.shape), _const_spec(w2_chunks.shape)],
        out_specs=(tile(D_MODEL), tile(D_MODEL), tile(D_FF), tile(D_FF), tile(D_MODEL), tile(D_MODEL),
                   tile(D_MODEL), pl.BlockSpec((16, D_MODEL), lambda i: (0, 0))),
        scratch_shapes=[pltpu.VMEM((tm, D_FF), jnp.float32)],
        compiler_params=_params(("arbitrary",)),
    )(x2, mixed, target, mod, ln_rows, w_out, w1_chunks, w2_chunks)


def _grad_matmul(a, b, name, tn, blocks_are_rows, riders=()):
    seq, m_dim = a.shape
    n_dim = b.shape[1]
    tk = min(seq, GRAD_TOKEN_TILE)
    nk = seq // tk
    n_ride = len(riders)
    if blocks_are_rows:
        tm = m_dim // N_CHIP
        assert tn == n_dim
        per_step = N_CHIP if m_dim <= GRAD_ROWS_PER_STEP else 1
        grid = (N_CHIP // per_step, 1, nk)
        out_map = lambda i, j, k: (i, 0, 0)
    else:
        tm = m_dim
        assert tn * N_CHIP == n_dim
        per_step = 1
        grid = (1, N_CHIP, nk)
        out_map = lambda i, j, k: (j, 0, 0)
    n_blocks = grid[0] * grid[1]
    rows = per_step * tm

    def body(*refs):
        a_ref, b_ref = refs[:2]
        ride_in, refs = refs[2:2 + n_ride], refs[2 + n_ride:]
        o_ref = refs[0]
        ride_out, refs = refs[1:1 + n_ride], refs[1 + n_ride:]
        acc_sc = refs[0]
        exchange = _ChipScatter(ride_in, ride_out, refs[1:]) if n_ride else None
        block = pl.program_id(0) + pl.program_id(1)
        k = pl.program_id(2)

        if exchange is not None:
            @pl.when((block == 0) & (k == 0))
            def _():
                exchange.start()

        @pl.when(k == 0)
        def _():
            acc_sc[...] = jnp.zeros_like(acc_sc)

        acc_sc[...] += _mm_tn(a_ref[...], b_ref[...])

        @pl.when(k == nk - 1)
        def _():
            for p in range(per_step):
                o_ref[p] = acc_sc[p * tm:(p + 1) * tm, :].astype(o_ref.dtype)

        if exchange is not None:
            @pl.when((block == n_blocks - 1) & (k == nk - 1))
            def _():
                exchange.wait()

    hbm = pl.BlockSpec(memory_space=pl.ANY)
    out = pl.pallas_call(
        body, name=name, grid=grid,
        out_shape=(jax.ShapeDtypeStruct((N_CHIP, tm, tn), WIRE_DTYPE),) + _exchange_out_shapes(riders, False),
        in_specs=[pl.BlockSpec((tk, rows), lambda i, j, k: (k, i)), pl.BlockSpec((tk, tn), lambda i, j, k: (k, j))]
        + [hbm] * n_ride,
        out_specs=(pl.BlockSpec((per_step, tm, tn), out_map),) + (hbm,) * n_ride,
        scratch_shapes=[pltpu.VMEM((rows, tn), jnp.float32)] + (_scatter_sems(n_ride) if n_ride else []),
        compiler_params=_params(("arbitrary", "arbitrary", "arbitrary")),
    )(a, b, *riders)
    return out if n_ride else out[0]


def _grad_matmul_full(a, b, name, tm, riders):
    seq, m_dim = a.shape
    n_dim = b.shape[1]
    tk = min(seq, GRAD_TOKEN_TILE)
    nk = seq // tk
    n_blocks = m_dim // tm
    n_ride = len(riders)
    assert m_dim % tm == 0

    def body(*refs):
        a_ref, b_ref = refs[:2]
        ride_in, refs = refs[2:2 + n_ride], refs[2 + n_ride:]
        o_ref = refs[0]
        ride_out, refs = refs[1:1 + n_ride], refs[1 + n_ride:]
        acc_sc = refs[0]
        swap = _SiblingSwap(ride_in, ride_out, refs[1:])
        i, k = pl.program_id(0), pl.program_id(1)

        @pl.when((i == 0) & (k == 0))
        def _():
            swap.start()

        @pl.when(k == 0)
        def _():
            acc_sc[...] = jnp.zeros_like(acc_sc)

        acc_sc[...] += _mm_tn(a_ref[...], b_ref[...])

        @pl.when(k == nk - 1)
        def _():
            o_ref[...] = acc_sc[...].astype(o_ref.dtype)

        @pl.when((i == n_blocks - 1) & (k == nk - 1))
        def _():
            swap.wait()

    hbm = pl.BlockSpec(memory_space=pl.ANY)
    return pl.pallas_call(
        body, name=name, grid=(n_blocks, nk),
        out_shape=(jax.ShapeDtypeStruct((m_dim, n_dim), WIRE_DTYPE),)
        + tuple(jax.ShapeDtypeStruct(r.shape, r.dtype) for r in riders),
        in_specs=[pl.BlockSpec((tk, tm), lambda i, k: (k, i)), pl.BlockSpec((tk, n_dim), lambda i, k: (k, 0))]
        + [hbm] * n_ride,
        out_specs=(pl.BlockSpec((tm, n_dim), lambda i, k: (i, 0)),) + (hbm,) * n_ride,
        scratch_shapes=[pltpu.VMEM((tm, n_dim), jnp.float32)] + _swap_sems(n_ride),
        compiler_params=_params(("arbitrary", "arbitrary")),
    )(a, b, *riders)


def _adam_pair(w, g_mine, g_sibling, m, v, name):
    rows, cols = w.shape
    tc = min(cols, ELEMENTWISE_COLS)

    def total(ref):
        if len(ref.shape) == 2:
            return ref[...]
        acc = ref[0].astype(jnp.float32)
        for j in range(1, ref.shape[0]):
            acc = acc + ref[j].astype(jnp.float32)
        return acc

    def body(w_ref, ga_ref, gb_ref, m_ref, v_ref, g_ref, dl_ref, m2_ref, v2_ref):
        g = total(ga_ref) + total(gb_ref)
        delta, m2, v2 = _adam(w_ref[...], g, m_ref[...], v_ref[...])
        g_ref[...] = g
        dl_ref[...] = delta
        m2_ref[...] = m2
        v2_ref[...] = v2

    blk = pl.BlockSpec((rows, tc), lambda i: (0, i))
    g_blk = lambda a: blk if a.ndim == 2 else pl.BlockSpec((a.shape[0], rows, tc), lambda i: (0, 0, i))
    out = jax.ShapeDtypeStruct((rows, cols), jnp.float32)
    return pl.pallas_call(
        body, name=name, grid=(cols // tc,),
        out_shape=(out, out, out, out),
        in_specs=[blk, g_blk(g_mine), g_blk(g_sibling), blk, blk], out_specs=(blk,) * 4,
        compiler_params=_params(("arbitrary",)),
    )(w, g_mine, g_sibling, m, v)


def _sum_devices(gathered, layout):
    def body(g_ref, *o_refs):
        total = g_ref[0]
        for d in range(1, N_DEV):
            total = total + g_ref[d]
        for (first, (rows_out, cols_out)), o_ref in zip(layout, o_refs):
            per_row = cols_out // 128
            for r in range(rows_out):
                for k in range(per_row):
                    src = first + r * per_row + k
                    o_ref[r:r + 1, k * 128:(k + 1) * 128] = total[src:src + 1, :]
        tail = total[total.shape[0] - 8:, :]
        o_refs[-1][...] = jnp.full((1, 128), jnp.sum(tail), jnp.float32)

    out_shape = tuple(jax.ShapeDtypeStruct(shape, jnp.float32) for _, shape in layout)
    return pl.pallas_call(
        body, name="sum_devices",
        out_shape=out_shape + (jax.ShapeDtypeStruct((1, 128), jnp.float32),),
    )(gathered)


def _adam_small(params):
    n = len(params)

    def body(*refs):
        ins, outs = refs[:4 * n], refs[4 * n:]
        for i in range(n):
            w_ref, g_ref, m_ref, v_ref = ins[4 * i:4 * i + 4]
            delta, m2, v2 = _adam(w_ref[...], g_ref[...], m_ref[...], v_ref[...])
            outs[3 * i][...] = delta
            outs[3 * i + 1][...] = m2
            outs[3 * i + 2][...] = v2

    out_shape = tuple(jax.ShapeDtypeStruct(p[0].shape, jnp.float32) for p in params for _ in range(3))
    out = pl.pallas_call(body, name="adam_small", out_shape=out_shape)(*[t for p in params for t in p])
    return [out[3 * i:3 * i + 3] for i in range(n)]


def _pad_heads(w):
    lead = w.shape[:-1]
    w = w.reshape(lead + (N_HEADS, GLA_DK))
    w = jnp.pad(w, [(0, 0)] * len(lead) + [(0, 0), (0, HEAD_W - GLA_DK)])
    return w.reshape(lead + (N_HEADS * HEAD_W,))


def _unpad_heads(w):
    lead = w.shape[:-1]
    return w.reshape(lead + (N_HEADS, HEAD_W))[..., :GLA_DK].reshape(lead + (N_HEADS * GLA_DK,))


def _pad_head_rows(w):
    w = w.reshape(N_HEADS, GLA_DK, w.shape[-1])
    return jnp.pad(w, ((0, 0), (0, HEAD_W - GLA_DK), (0, 0))).reshape(N_HEADS * HEAD_W, w.shape[-1])


def _unpad_head_rows(w):
    return w.reshape(N_HEADS, HEAD_W, w.shape[-1])[:, :GLA_DK].reshape(N_HEADS * GLA_DK, w.shape[-1])


def _pad_w_in_rows(stack):
    w = stack.reshape(-1, stack.shape[-1])
    return jnp.concatenate([
        w[:2048], _pad_head_rows(w[2048:2304]), _pad_head_rows(w[2304:2560]), w[2560:3584],
        jnp.pad(w[3584:3600], ((0, HEAD_W - GATE_RANK), (0, 0)))], axis=0)


def _unpad_w_in_stack(g, per):
    segments = [(0, g[:2048]), (2048, _unpad_head_rows(g[OFF_GQ:OFF_GQ + 512])),
                (2304, _unpad_head_rows(g[OFF_GK:OFF_GK + 512])), (2560, g[OFF_GV:OFF_LR]),
                (3584, g[OFF_LR:OFF_LR + GATE_RANK])]
    blocks = []
    for j in range(N_CHIP):
        lo, hi = j * per, (j + 1) * per
        pieces = []
        for start, rows in segments:
            a, b = max(lo, start), min(hi, start + rows.shape[0])
            if a < b:
                pieces.append(rows[a - start:b - start])
        blocks.append(jnp.concatenate(pieces, axis=0))
    return jnp.stack(blocks)


def _col_major(w):
    return jnp.transpose(w, (2, 0, 1)).reshape(w.shape[2], w.shape[1])


def _rows128(a):
    return a.reshape(-1, 128)


def kernel(x, c, w_ada, b_ada, w_in, ret_norm_w, gla_gate_w, gla_gate_b, gla_norm_w, w_out, ln1_w, ln1_b, w_ff1, w_ff2, ln2_w, ln2_b, loss_target, m_w_ada, m_b_ada, m_w_in, m_ret_norm_w, m_gla_gate_w, m_gla_gate_b, m_gla_norm_w, m_w_out, m_ln1_w, m_ln1_b, m_w_ff1, m_w_ff2, m_ln2_w, m_ln2_b, v_w_ada, v_b_ada, v_w_in, v_ret_norm_w, v_gla_gate_w, v_gla_gate_b, v_gla_norm_w, v_w_out, v_ln1_w, v_ln1_b, v_w_ff1, v_w_ff2, v_ln2_w, v_ln2_b):
    seq = x.shape[1]
    tm = min(seq, TOKEN_TILE)
    tm_in = min(seq, INPROJ_TOKEN_TILE)
    xi, yi, ci = _mesh_pos()
    dev = 4 * xi + 2 * yi + ci
    chip = 2 * xi + yi
    x2, target = x[0], loss_target[0]
    ada_cols = w_ada.shape[2]
    in_cols = w_in.shape[2]
    gate_cols = gla_gate_w.shape[2]

    b_blk = lax.dynamic_slice(b_ada, (0, chip * ada_cols), (1, ada_cols))
    g0, _, mod, w_in_stack = _prologue(jnp.concatenate([_rows128(c), _rows128(gla_gate_w[0])], axis=0), w_ada[0],
                                       b_blk, _col_major(w_in.astype(WIRE_DTYPE)))
    c_all = g0[:, :8].reshape(N_DEV, D_MODEL)
    gate_w_full = jnp.concatenate([g0[2 * j, 8:16].reshape(GATE_RANK, gate_cols) for j in range(N_CHIP)], axis=1)
    wg_p = jnp.pad(_pad_heads(gate_w_full), ((0, HEAD_W - GATE_RANK), (0, 0)))
    bg_p = _pad_heads(gla_gate_b)
    w_in_pt = _pad_w_in_rows(w_in_stack).astype(MXU_DTYPE)
    w_in_p = jnp.transpose(w_in_pt)

    proj, u, w2_stack = _inproj_fwd(x2, mod, w_in_p, tm_in, [w_ff2[0].astype(WIRE_DTYPE)])
    rot_a, rot_b = _rotary_tables(seq)
    dm_t, qdec_t, kdec_t, chunk_decay = _decay_tables()
    tables = (rot_a, rot_b, dm_t, qdec_t, kdec_t, chunk_decay)
    mixed, rsave, ssave, w_out_stack, w1_stack = _mixer_fwd(
        proj, tables, wg_p, bg_p, ret_norm_w, gla_norm_w,
        [w_out[0].astype(WIRE_DTYPE), w_ff1[0].astype(WIRE_DTYPE)])
    w_out_full = w_out_stack.reshape(D_MODEL, D_MODEL).astype(MXU_DTYPE)
    w1_chunks = w1_stack.astype(MXU_DTYPE)
    w2_chunks = w2_stack.astype(MXU_DTYPE)

    ln_rows = jnp.concatenate([ln1_w, ln1_b, ln2_w, ln2_b], axis=0)
    dmixed, dxa, act, dh, u2, df, dm, sums2 = _mlp_fwd_bwd(x2, mixed, target, mod, ln_rows, w_out_full, w1_chunks,
                                                           w2_chunks, tm)

    g_out_stack = _grad_matmul(mixed, dm, "grad_w_out", D_MODEL, True)
    g_ff1_stack, r_out = _grad_matmul(u2, dh, "grad_w_ff1", D_FF // N_CHIP, False, [g_out_stack])
    g_ff2_stack = _grad_matmul(act, df, "grad_w_ff2", D_MODEL, True)
    dproj, d_ret_norm, d_gla_norm, d_wg_p, d_bg_p, r_ff1, r_ff2 = _mixer_bwd(
        proj, dmixed, rsave, ssave, tables, wg_p, bg_p, ret_norm_w, gla_norm_w, [g_ff1_stack, g_ff2_stack])
    early = ["w_out", "w_ff1", "w_ff2"]
    partial = dict(zip(early, [r_out, r_ff1, r_ff2]))
    g_in_t, *swapped_early = _grad_matmul_full(dproj, u, "grad_w_in", N_PROJ // 3, [partial[n] for n in early])
    swapped = dict(zip(early, swapped_early))
    g_in_stack = _unpad_w_in_stack(g_in_t, in_cols)
    grad_x, sums1, r_in = _inproj_bwd(dproj, x2, dxa, mod, w_in_pt, tm_in, [g_in_stack])

    sources = [sums1, sums2, d_ret_norm, _unpad_heads(d_bg_p), d_gla_norm, _unpad_heads(d_wg_p[:GATE_RANK])]
    pieces = [(0, 0, 0), (0, 1, 8), (1, S_GATE1, 16), (1, S_SHIFT2, 24), (1, S_SCALE2, 32), (1, S_GATE2, 40),
              (1, S_LN1W, 48), (1, S_LN1B, 56), (1, S_LN2W, 64), (1, S_LN2B, 72), (2, 0, 80), (3, 0, 88), (4, 0, 96)]
    pieces += [(5, r, 104 + 2 * r) for r in range(GATE_RANK)] + [(1, S_LOSS, 136)]
    partial["w_in"] = r_in
    g2, swapped["w_in"] = _gather_rows(sources, pieces, 144, "gather_small", [r_in])
    (grad_b_ada, grad_ln1_w, grad_ln1_b, grad_ln2_w, grad_ln2_b, grad_ret_norm, grad_gate_b, grad_gla_norm,
     grad_gate_w_full, loss_sum) = _sum_devices(g2, [
         (0, (1, 6 * D_MODEL)), (48, (1, D_MODEL)), (56, (1, D_MODEL)), (64, (1, D_MODEL)), (72, (1, D_MODEL)),
         (80, (1, 512)), (88, (1, 256)), (96, (1, 512)), (104, (GATE_RANK, 256))])
    loss = 0.5 / D_MODEL * loss_sum[0, 0]
    grad_gate_w = lax.dynamic_slice(grad_gate_w_full, (0, chip * gate_cols), (GATE_RANK, gate_cols))

    small_grads = [grad_b_ada, grad_ln1_w, grad_ln1_b, grad_ln2_w, grad_ln2_b, grad_ret_norm, grad_gate_b,
                   grad_gla_norm, grad_gate_w[None]]
    small_out = _adam_small(list(zip(
        [b_ada, ln1_w, ln1_b, ln2_w, ln2_b, ret_norm_w, gla_gate_b, gla_norm_w, gla_gate_w], small_grads,
        [m_b_ada, m_ln1_w, m_ln1_b, m_ln2_w, m_ln2_b, m_ret_norm_w, m_gla_gate_b, m_gla_norm_w, m_gla_gate_w],
        [v_b_ada, v_ln1_w, v_ln1_b, v_ln2_w, v_ln2_b, v_ret_norm_w, v_gla_gate_b, v_gla_norm_w, v_gla_gate_w])))
    sm_delta, sm_m, sm_v = [[o[k] for o in small_out] for k in range(3)]

    dmod_all = g2[:, 0:48].reshape(N_DEV, 6 * D_MODEL)
    dmod_blk = lax.dynamic_slice(dmod_all, (0, chip * ada_cols), (N_DEV, ada_cols))
    ada_out = _ada_bwd_adam(jnp.transpose(c_all), dmod_blk, w_ada[0], m_w_ada[0], v_w_ada[0])
    ada_g, ada_delta, ada_m, ada_v = [t[None] for t in ada_out]

    big = {}
    for n, w, m, v in zip(["w_in", "w_out", "w_ff1", "w_ff2"], [w_in, w_out, w_ff1, w_ff2],
                          [m_w_in, m_w_out, m_w_ff1, m_w_ff2], [v_w_in, v_w_out, v_w_ff1, v_w_ff2]):
        mine, theirs = partial[n], swapped[n]
        if n == "w_in":
            out = _adam_pair(_col_major(w), mine, theirs, _col_major(m), _col_major(v), "adam_" + n)
            big[n] = [jnp.transpose(t.reshape(t.shape[0], 1, t.shape[1]), (1, 2, 0)) for t in out]
        else:
            big[n] = [t[None] for t in _adam_pair(w[0], mine, theirs, m[0], v[0], "adam_" + n)]

    def assemble(ada, smalls, k):
        b_ada_o, ln1w_o, ln1b_o, ln2w_o, ln2b_o, ret_o, gb_o, gln_o, gw_o = smalls
        return [ada, b_ada_o, big["w_in"][k], ret_o, gw_o, gb_o, gln_o, big["w_out"][k], ln1w_o, ln1b_o,
                big["w_ff1"][k], big["w_ff2"][k], ln2w_o, ln2b_o]

    grads = assemble(ada_g, small_grads, 0)
    deltas = assemble(ada_delta, sm_delta, 1)
    new_m = assemble(ada_m, sm_m, 2)
    new_v = assemble(ada_v, sm_v, 3)
    return (loss, grad_x[None], *grads, *deltas, *new_m, *new_v)
```

```python
import numpy as np
import jax
import jax.numpy as jnp
from jax import lax
from jax.experimental import pallas as pl
from jax.experimental.pallas import tpu as pltpu

D_MODEL = 1024
D_FF = 4096
CHUNK = 64
N_HEADS = 4
HEAD_W = 128
GLA_DK = 64
GATE_RANK = 16
GATE_TAU = 16.0
LN_EPS = 1e-5
ALPHA = 2.0 ** 0.25
ROPE_BASE = 10000.0
RET_SCALE = float(HEAD_W) ** -0.5
GLA_SCALE = float(GLA_DK) ** -0.5

ADAM_LR = 0.001
ADAM_B1 = 0.9
ADAM_B2 = 0.999
ADAM_EPS = 1e-08
ADAM_WD = 0.01
ADAM_STEP = 10

OFF_RQ, OFF_RK, OFF_RV, OFF_RG = 0, 512, 1024, 1536
OFF_GQ, OFF_GK, OFF_GV, OFF_GG, OFF_LR = 2048, 2560, 3072, 3584, 4096
N_PROJ = 4224

N_DEV = 8
N_CHIP = 4
MESH = pl.DeviceIdType.MESH
MXU_DTYPE = jnp.bfloat16
WIRE_DTYPE = jnp.bfloat16
VMEM_LIMIT = 60 * 1024 * 1024
TOKEN_TILE = 256
INPROJ_TOKEN_TILE = 512
CHUNKS_PER_STEP = 8
CHUNKS_IN_LOCKSTEP = 4
GRAD_ROWS_PER_STEP = 1024
GRAD_TOKEN_TILE = 2048
ELEMENTWISE_COLS = 512
HIGHEST = lax.Precision.HIGHEST


def _mm(a, b):
    return jnp.dot(a.astype(MXU_DTYPE), b.astype(MXU_DTYPE), preferred_element_type=jnp.float32)


def _mm_nt(a, b):
    return lax.dot_general(a.astype(MXU_DTYPE), b.astype(MXU_DTYPE), (((1,), (1,)), ((), ())),
                           preferred_element_type=jnp.float32)


def _mm_tn(a, b):
    return lax.dot_general(a.astype(MXU_DTYPE), b.astype(MXU_DTYPE), (((0,), (0,)), ((), ())),
                           preferred_element_type=jnp.float32)


def _mm32(a, b):
    return jnp.dot(a, b, precision=HIGHEST, preferred_element_type=jnp.float32)


def _running_sum(mask, a):
    m = mask.astype(jnp.bfloat16)
    hi = a.astype(jnp.bfloat16)
    rest = a - hi.astype(jnp.float32)
    mid = rest.astype(jnp.bfloat16)
    lo = (rest - mid.astype(jnp.float32)).astype(jnp.bfloat16)
    dot = lambda t: jnp.dot(m, t, preferred_element_type=jnp.float32)
    return dot(hi) + dot(mid) + dot(lo)


def _rowmean(a):
    return jnp.mean(a, axis=-1, keepdims=True)


def _colsum(a):
    return jnp.sum(a, axis=0, keepdims=True)


def _ln(z):
    zc = z - _rowmean(z)
    rstd = lax.rsqrt(_rowmean(zc * zc) + LN_EPS)
    return zc * rstd, rstd


def _ln_bwd(dzh, zh, rstd):
    return rstd * (dzh - _rowmean(dzh) - zh * _rowmean(dzh * zh))


def _sigmoid(a):
    return 1.0 / (1.0 + jnp.exp(-a))


def _log_sigmoid(a):
    return jnp.minimum(a, 0.0) - jnp.log(1.0 + jnp.exp(-jnp.abs(a)))


def _swap_halves(a):
    return pltpu.roll(a, HEAD_W // 2, 1)


def _tri_masks():
    row = lax.broadcasted_iota(jnp.int32, (CHUNK, CHUNK), 0)
    col = lax.broadcasted_iota(jnp.int32, (CHUNK, CHUNK), 1)
    return row, col


def _const_spec(shape):
    zeros = (0,) * len(shape)
    return pl.BlockSpec(shape, lambda *_: zeros, pipeline_mode=pl.Buffered(1))


def _params(semantics):
    return pltpu.CompilerParams(dimension_semantics=semantics, vmem_limit_bytes=VMEM_LIMIT)


def _decay_tables():
    log_gamma = np.log(1.0 - 2.0 ** (-5.0 - np.arange(N_HEADS, dtype=np.float64)))
    idx = np.arange(CHUNK, dtype=np.float64)
    dist = np.abs(idx[:, None] - idx[None, :])
    intra = np.exp(log_gamma[:, None, None] * dist)
    kdec = np.exp(log_gamma[None, :] * (CHUNK - 1.0 - idx)[:, None])
    qdec = np.exp(log_gamma[None, :] * (idx + 1.0)[:, None])
    chunk_decay = np.exp(log_gamma * CHUNK)
    lanes = lambda t: np.repeat(t, HEAD_W, axis=1).astype(np.float32)
    return (jnp.asarray(intra.astype(np.float32)), jnp.asarray(lanes(qdec)), jnp.asarray(lanes(kdec)),
            [float(np.float32(v)) for v in chunk_decay])


def _rotary_tables(seq):
    half = HEAD_W // 2
    inv = 1.0 / (ROPE_BASE ** jnp.linspace(0.0, 1.0, half, dtype=jnp.float32))
    both = lambda t: jnp.concatenate([t, t], axis=-1)
    ang_a = jnp.arange(0, seq, CHUNK, dtype=jnp.float32)[:, None] * inv[None, :]
    rot_a = jnp.stack([both(jnp.cos(ang_a)), both(jnp.sin(ang_a))], axis=1)
    rot_a = jnp.pad(rot_a, ((0, 0), (0, 6), (0, 0)))
    ang_b = jnp.arange(CHUNK, dtype=jnp.float32)[:, None] * inv[None, :]
    cos_b, sin_b = both(jnp.cos(ang_b)), both(jnp.sin(ang_b))
    sign = jnp.concatenate([-jnp.ones((half,), jnp.float32), jnp.ones((half,), jnp.float32)])
    return rot_a, jnp.stack([cos_b, sin_b, cos_b * sign, sin_b * sign])


def _rotary_chunk(ra_ref, c, rb_ref):
    cos_a, sin_a = ra_ref[c, 0:1, :], ra_ref[c, 1:2, :]
    return cos_a * rb_ref[0] - sin_a * rb_ref[1], sin_a * rb_ref[2] + cos_a * rb_ref[3]


def _mesh_pos():
    return lax.axis_index("x"), lax.axis_index("y"), lax.axis_index("c")


def _flip(v, bit):
    return 1 - v if bit else v


def _gather_rows(sources, pieces, rows, name, swaps):
    n_src, n = len(sources), len(swaps)

    def body(*refs):
        src_refs, refs = refs[:n_src], refs[n_src:]
        out_ref = refs[n]
        v_sc = refs[1 + 2 * n]
        swap = _SiblingSwap(refs[:n], refs[1 + n:1 + 2 * n], refs[4 + 2 * n:])
        swap.start()
        v_sc[...] = jnp.zeros_like(v_sc)
        for s, row, first in pieces:
            for k in range(src_refs[s].shape[1] // 128):
                v_sc[first + k:first + k + 1, :] = src_refs[s][row:row + 1, k * 128:(k + 1) * 128]
        _all_devices_exchange(v_sc, out_ref, refs[2 + 2 * n], refs[3 + 2 * n])
        swap.wait()

    hbm = pl.BlockSpec(memory_space=pl.ANY)
    vmem = pl.BlockSpec(memory_space=pltpu.VMEM)
    return pl.pallas_call(
        body, name=name,
        out_shape=(jax.ShapeDtypeStruct((N_DEV, rows, 128), jnp.float32),)
        + tuple(jax.ShapeDtypeStruct(a.shape, a.dtype) for a in swaps),
        in_specs=[vmem] * n_src + [hbm] * n,
        out_specs=(vmem,) + (hbm,) * n,
        scratch_shapes=[pltpu.VMEM((rows, 128), jnp.float32)] + _all_devices_sems() + _swap_sems(n),
    )(*sources, *swaps)


def _all_devices_sems():
    return [pltpu.SemaphoreType.DMA((N_DEV - 1,)), pltpu.SemaphoreType.DMA((N_DEV - 1,))]


def _all_devices_exchange(v_ref, out_ref, send_sems, recv_sems):
    x, y, c = _mesh_pos()
    me = 4 * x + 2 * y + c
    out_ref[me] = v_ref[...]
    sends, recvs = [], []
    for k in range(1, N_DEV):
        px, py, pc = _flip(x, (k >> 2) & 1), _flip(y, (k >> 1) & 1), _flip(c, k & 1)
        peer = 4 * px + 2 * py + pc
        sends.append(pltpu.make_async_remote_copy(
            src_ref=v_ref, dst_ref=out_ref.at[me], send_sem=send_sems.at[k - 1], recv_sem=recv_sems.at[k - 1],
            device_id=(px, py, pc), device_id_type=MESH))
        recvs.append(pltpu.make_async_remote_copy(
            src_ref=v_ref, dst_ref=out_ref.at[peer], send_sem=send_sems.at[k - 1], recv_sem=recv_sems.at[k - 1],
            device_id=(px, py, pc), device_id_type=MESH))
    for cp in sends:
        cp.start()
    for cp in recvs:
        cp.wait_recv()
    for cp in sends:
        cp.wait_send()


def _prologue(cond_rows, w_ada_blk, b_blk, w_in_t):
    cols = w_ada_blk.shape[1]
    groups = cols // 128
    c_rows = D_MODEL // 128

    def body(cond_ref, w_ref, b_ref, win_ref, cond_all_ref, mod_all_ref, mod_ref, stack_ref, mod_sc, *sems):
        gather = _ChipGather([win_ref], [stack_ref], sems[:5])
        gather.start()
        _all_devices_exchange(cond_ref, cond_all_ref, sems[5], sems[6])
        acc = jnp.broadcast_to(b_ref[...], (N_DEV, cols))
        for r in range(c_rows):
            cv = cond_all_ref[:, r, :]
            acc = acc + _mm32(cv * _sigmoid(cv), w_ref[r * 128:(r + 1) * 128, :])
        for k in range(groups):
            mod_sc[k] = acc[:, k * 128:(k + 1) * 128]
        _all_devices_exchange(mod_sc, mod_all_ref, sems[7], sems[8])
        x, y, c = _mesh_pos()
        me = 4 * x + 2 * y + c
        for j in range(N_CHIP):
            for k in range(groups):
                lane = j * cols + k * 128
                mod_ref[:, lane:lane + 128] = mod_all_ref[2 * j, k, pl.ds(me, 1), :]
        gather.forward()
        gather.finish()

    vmem = pl.BlockSpec(memory_space=pltpu.VMEM)
    hbm = pl.BlockSpec(memory_space=pl.ANY)
    return pl.pallas_call(
        body, name="prologue",
        out_shape=(jax.ShapeDtypeStruct((N_DEV,) + cond_rows.shape, jnp.float32),
                   jax.ShapeDtypeStruct((N_DEV, groups, N_DEV, 128), jnp.float32),
                   jax.ShapeDtypeStruct((1, N_CHIP * cols), jnp.float32))
        + _exchange_out_shapes([w_in_t], True),
        in_specs=[vmem, vmem, vmem, hbm],
        out_specs=(vmem, vmem, vmem, hbm),
        scratch_shapes=[pltpu.VMEM((groups, N_DEV, 128), jnp.float32)] + _gather_sems(1)
        + _all_devices_sems() + _all_devices_sems(),
        compiler_params=pltpu.CompilerParams(vmem_limit_bytes=VMEM_LIMIT),
    )(cond_rows, w_ada_blk, b_blk, w_in_t)


def _exchange_out_shapes(arrays, gather):
    return tuple(jax.ShapeDtypeStruct((N_CHIP,) + a.shape if gather else a.shape, a.dtype) for a in arrays)


def _scatter_sems(n):
    n_sem = n * (N_CHIP - 1)
    return [pltpu.SemaphoreType.DMA((n_sem,)), pltpu.SemaphoreType.DMA((n_sem,)), pltpu.SemaphoreType.DMA((n,))]


def _gather_sems(n):
    n_sem = n * (N_CHIP - 1)
    return [pltpu.SemaphoreType.DMA((n_sem,))] * 4 + [pltpu.SemaphoreType.DMA((n,))]


def _peer_chips(x, y):
    out = []
    for k in range(1, N_CHIP):
        px, py = _flip(x, (k >> 1) & 1), _flip(y, k & 1)
        out.append((px, py, 2 * px + py))
    return out


class _ChipScatter:
    def __init__(self, ins, outs, sems):
        send_sems, recv_sems, local_sems = sems
        x, y, c = _mesh_pos()
        chip = 2 * x + y
        self.local, self.sends, self.recvs = [], [], []
        for i in range(len(ins)):
            self.local.append(pltpu.make_async_copy(ins[i].at[chip], outs[i].at[chip], local_sems.at[i]))
            for k, (px, py, peer_chip) in enumerate(_peer_chips(x, y)):
                sem = i * (N_CHIP - 1) + k
                src = ins[i].at[peer_chip]
                self.sends.append(pltpu.make_async_remote_copy(
                    src_ref=src, dst_ref=outs[i].at[chip], send_sem=send_sems.at[sem], recv_sem=recv_sems.at[sem],
                    device_id=(px, py, c), device_id_type=MESH))
                self.recvs.append(pltpu.make_async_remote_copy(
                    src_ref=src, dst_ref=outs[i].at[peer_chip], send_sem=send_sems.at[sem], recv_sem=recv_sems.at[sem],
                    device_id=(px, py, c), device_id_type=MESH))

    def start(self):
        for cp in self.local + self.sends:
            cp.start()

    def wait(self):
        for cp in self.recvs:
            cp.wait_recv()
        for cp in self.sends:
            cp.wait_send()
        for cp in self.local:
            cp.wait()


class _ChipGather:
    def __init__(self, ins, outs, sems):
        ici_send, ici_recv, d2d_send, d2d_recv, local_sems = sems
        x, y, c = _mesh_pos()
        chip = 2 * x + y
        self.local, self.ici_sends, self.ici_recvs, self.d2d_sends, self.d2d_recvs = [], [], [], [], []
        for i in range(len(ins)):
            half = ins[i].shape[-1] // 2
            assert half % 128 == 0
            lead = (slice(None),) * (len(ins[i].shape) - 1)
            mine = lead + (pl.ds(pl.multiple_of(c * half, 128), half),)
            theirs = lead + (pl.ds(pl.multiple_of((1 - c) * half, 128), half),)
            self.local.append(pltpu.make_async_copy(ins[i], outs[i].at[chip], local_sems.at[i]))
            for k, (px, py, peer_chip) in enumerate(_peer_chips(x, y)):
                sem = i * (N_CHIP - 1) + k
                self.ici_sends.append(pltpu.make_async_remote_copy(
                    src_ref=ins[i].at[mine], dst_ref=outs[i].at[chip].at[mine],
                    send_sem=ici_send.at[sem], recv_sem=ici_recv.at[sem], device_id=(px, py, c), device_id_type=MESH))
                landed = outs[i].at[peer_chip].at[mine]
                self.ici_recvs.append(pltpu.make_async_remote_copy(
                    src_ref=ins[i].at[mine], dst_ref=landed,
                    send_sem=ici_send.at[sem], recv_sem=ici_recv.at[sem], device_id=(px, py, c), device_id_type=MESH))
                self.d2d_sends.append(pltpu.make_async_remote_copy(
                    src_ref=landed, dst_ref=landed,
                    send_sem=d2d_send.at[sem], recv_sem=d2d_recv.at[sem], device_id=(x, y, 1 - c), device_id_type=MESH))
                self.d2d_recvs.append(pltpu.make_async_remote_copy(
                    src_ref=landed, dst_ref=outs[i].at[peer_chip].at[theirs],
                    send_sem=d2d_send.at[sem], recv_sem=d2d_recv.at[sem], device_id=(x, y, 1 - c), device_id_type=MESH))

    def start(self):
        for cp in self.local + self.ici_sends:
            cp.start()

    def forward(self):
        for landed, onward in zip(self.ici_recvs, self.d2d_sends):
            landed.wait_recv()
            onward.start()

    def finish(self):
        for cp in self.d2d_recvs:
            cp.wait_recv()
        for cp in self.d2d_sends + self.ici_sends:
            cp.wait_send()
        for cp in self.local:
            cp.wait()


def _swap_sems(n):
    return [pltpu.SemaphoreType.DMA((n,)), pltpu.SemaphoreType.DMA((n,))]


class _SiblingSwap:
    def __init__(self, ins, outs, sems):
        send_sems, recv_sems = sems
        x, y, c = _mesh_pos()
        self.copies = [pltpu.make_async_remote_copy(
            src_ref=ins[i], dst_ref=outs[i], send_sem=send_sems.at[i], recv_sem=recv_sems.at[i],
            device_id=(x, y, 1 - c), device_id_type=MESH) for i in range(len(ins))]

    def start(self):
        for cp in self.copies:
            cp.start()

    def wait(self):
        for cp in self.copies:
            cp.wait_recv()
        for cp in self.copies:
            cp.wait_send()


def _adam(w, g, m, v):
    m2 = ADAM_B1 * m + (1.0 - ADAM_B1) * g
    v2 = ADAM_B2 * v + (1.0 - ADAM_B2) * (g * g)
    m_hat = m2 / (1.0 - ADAM_B1 ** ADAM_STEP)
    v_hat = v2 / (1.0 - ADAM_B2 ** ADAM_STEP)
    delta = -ADAM_LR * (m_hat / (jnp.sqrt(v_hat) + ADAM_EPS) + ADAM_WD * w)
    return delta, m2, v2


def _ada_bwd_adam(c_t, dmod_blk, w, m, v):
    rows, cols = w.shape
    tile = 512
    assert cols % tile == 0

    def body(c_ref, d_ref, w_ref, m_ref, v_ref, g_ref, dl_ref, m2_ref, v2_ref):
        sc = c_ref[...]
        sc = sc * _sigmoid(sc)
        dm = d_ref[...]
        g = sc[:, 0:1] * dm[0:1, :]
        for b in range(1, N_DEV):
            g = g + sc[:, b:b + 1] * dm[b:b + 1, :]
        delta, m2, v2 = _adam(w_ref[...], g, m_ref[...], v_ref[...])
        g_ref[...] = g
        dl_ref[...] = delta
        m2_ref[...] = m2
        v2_ref[...] = v2

    blk = pl.BlockSpec((rows, tile), lambda j: (0, j))
    out = jax.ShapeDtypeStruct((rows, cols), jnp.float32)
    return pl.pallas_call(
        body, name="ada_bwd_adam", grid=(cols // tile,),
        out_shape=(out, out, out, out),
        in_specs=[pl.BlockSpec((rows, N_DEV), lambda j: (0, 0)), pl.BlockSpec((N_DEV, tile), lambda j: (0, j)),
                  blk, blk, blk],
        out_specs=(blk, blk, blk, blk),
        compiler_params=_params(("arbitrary",)),
    )(c_t, dmod_blk, w, m, v)


MOD_SHIFT1, MOD_SCALE1, MOD_GATE1, MOD_SHIFT2, MOD_SCALE2, MOD_GATE2 = range(6)


def _mod(mod_ref, segment):
    return mod_ref[:, segment * D_MODEL:(segment + 1) * D_MODEL]


def _inproj_fwd(x2, vecs, w_in_p, tm, riders):
    seq = x2.shape[0]
    n_tiles = seq // tm
    n_ride = len(riders)

    def body(*refs):
        x_ref, vec_ref, w_ref = refs[:3]
        ride_in, refs = refs[3:3 + n_ride], refs[3 + n_ride:]
        p_ref, u_ref = refs[:2]
        ride_out, sems = refs[2:2 + n_ride], refs[2 + n_ride:]
        gather = _ChipGather(ride_in, ride_out, sems)

        @pl.when(pl.program_id(0) == 0)
        def _():
            gather.start()

        xh, _ = _ln(x_ref[...])
        u = (xh * (1.0 + _mod(vec_ref, MOD_SCALE1)) + _mod(vec_ref, MOD_SHIFT1)).astype(MXU_DTYPE)
        u_ref[...] = u
        p_ref[...] = _mm(u, w_ref[...])

        @pl.when(pl.program_id(0) == (3 * n_tiles) // 4)
        def _():
            gather.forward()

        @pl.when(pl.program_id(0) == n_tiles - 1)
        def _():
            gather.finish()

    hbm = pl.BlockSpec(memory_space=pl.ANY)
    return pl.pallas_call(
        body, name="inproj_fwd", grid=(n_tiles,),
        out_shape=(jax.ShapeDtypeStruct((seq, N_PROJ), jnp.float32), jax.ShapeDtypeStruct((seq, D_MODEL), MXU_DTYPE))
        + _exchange_out_shapes(riders, True),
        in_specs=[pl.BlockSpec((tm, D_MODEL), lambda i: (i, 0)), _const_spec(vecs.shape), _const_spec(w_in_p.shape)]
        + [hbm] * n_ride,
        out_specs=(pl.BlockSpec((tm, N_PROJ), lambda i: (i, 0)), pl.BlockSpec((tm, D_MODEL), lambda i: (i, 0)))
        + (hbm,) * n_ride,
        scratch_shapes=_gather_sems(n_ride),
        compiler_params=_params(("arbitrary",)),
    )(x2, vecs, w_in_p, *riders)


def _inproj_bwd(dproj, x2, dxa, vecs, w_in_pt, tm, riders):
    seq = x2.shape[0]
    n_tiles = seq // tm
    n_ride = len(riders)

    def body(*refs):
        dp_ref, x_ref, dxa_ref, vec_ref, w_ref = refs[:5]
        ride_in, refs = refs[5:5 + n_ride], refs[5 + n_ride:]
        gx_ref, sums_ref = refs[:2]
        ride_out, sems = refs[2:2 + n_ride], refs[2 + n_ride:]
        exchange = _ChipScatter(ride_in, ride_out, sems)

        @pl.when(pl.program_id(0) == 0)
        def _():
            exchange.start()
            sums_ref[...] = jnp.zeros_like(sums_ref)

        du = _mm(dp_ref[...], w_ref[...])
        xh, rstd = _ln(x_ref[...])
        sums_ref[0:1, :] += _colsum(du)
        sums_ref[1:2, :] += _colsum(du * xh)
        gx_ref[...] = dxa_ref[...] + _ln_bwd(du * (1.0 + _mod(vec_ref, MOD_SCALE1)), xh, rstd)

        @pl.when(pl.program_id(0) == n_tiles - 1)
        def _():
            exchange.wait()

    tile = pl.BlockSpec((tm, D_MODEL), lambda i: (i, 0))
    hbm = pl.BlockSpec(memory_space=pl.ANY)
    return pl.pallas_call(
        body, name="inproj_bwd", grid=(n_tiles,),
        out_shape=(jax.ShapeDtypeStruct((seq, D_MODEL), jnp.float32), jax.ShapeDtypeStruct((8, D_MODEL), jnp.float32))
        + _exchange_out_shapes(riders, False),
        in_specs=[pl.BlockSpec((tm, N_PROJ), lambda i: (i, 0)), tile, tile, _const_spec(vecs.shape),
                  _const_spec(w_in_pt.shape)] + [hbm] * n_ride,
        out_specs=(tile, pl.BlockSpec((8, D_MODEL), lambda i: (0, 0))) + (hbm,) * n_ride,
        scratch_shapes=_scatter_sems(n_ride),
        compiler_params=_params(("arbitrary",)),
    )(dproj, x2, dxa, vecs, w_in_pt, *riders)


def _head(h):
    return slice(h * HEAD_W, (h + 1) * HEAD_W)


def _cols(ref, off, h):
    return ref[:, off + h * HEAD_W:off + (h + 1) * HEAD_W]


HEADS = range(N_HEADS)


def _mixer_chunk_forward(p_ref, cc, ss, dm_ref, qdec_ref, kdec_ref, wg_ref, bg_ref, states):
    row, col = _tri_masks()
    lower = row >= col
    f = {}
    f["glr"] = p_ref[:, OFF_LR:OFF_LR + HEAD_W]
    f["logit"] = _mm(f["glr"], wg_ref[...]) + bg_ref[...]
    rq = [_cols(p_ref, OFF_RQ, h) for h in HEADS]
    rk = [_cols(p_ref, OFF_RK, h) for h in HEADS]
    f["rv"] = [_cols(p_ref, OFF_RV, h) for h in HEADS]
    f["qr"] = [(rq[h] * cc + _swap_halves(rq[h]) * ss) * RET_SCALE for h in HEADS]
    f["kr"] = [rk[h] * cc + _swap_halves(rk[h]) * ss for h in HEADS]
    s_raw = [_mm_nt(f["qr"][h], f["kr"][h]) for h in HEADS]
    yield
    la = _log_sigmoid(f["logit"]) * (1.0 / GATE_TAU)
    b = _running_sum(lower, la)
    f["qd"] = [f["qr"][h] * qdec_ref[:, _head(h)] for h in HEADS]
    f["kd"] = [f["kr"][h] * kdec_ref[:, _head(h)] for h in HEADS]
    f["scores"] = [s_raw[h] * dm_ref[h] for h in HEADS]
    yield
    b_last = b[CHUNK - 1:CHUNK, :]
    b_mid = b[CHUNK // 2 - 1:CHUNK // 2, :]
    f["e"], f["ei"] = jnp.exp(b - b_mid), jnp.exp(b_mid - b)
    f["eb"], f["ek"], f["ebl"] = jnp.exp(b), jnp.exp(b_last - b), jnp.exp(b_last)
    gq = [_cols(p_ref, OFF_GQ, h) * GLA_SCALE for h in HEADS]
    gk = [_cols(p_ref, OFF_GK, h) for h in HEADS]
    f["gv"] = [_cols(p_ref, OFF_GV, h) for h in HEADS]
    f["q_e"] = [gq[h] * f["e"][:, _head(h)] for h in HEADS]
    f["q_i"] = [gq[h] * f["ei"][:, _head(h)] for h in HEADS]
    f["k_e"] = [gk[h] * f["e"][:, _head(h)] for h in HEADS]
    f["k_i"] = [gk[h] * f["ei"][:, _head(h)] for h in HEADS]
    low = [_mm_nt(f["q_e"][h], f["k_i"][h]) for h in HEADS]
    up = [_mm_nt(f["q_i"][h], f["k_e"][h]) for h in HEADS]
    yield
    f["att"] = [jnp.where(lower, low[h], up[h]) for h in HEADS]
    f["qb"] = [gq[h] * f["eb"][:, _head(h)] for h in HEADS]
    f["kb"] = [gk[h] * f["ek"][:, _head(h)] for h in HEADS]
    ret_state, gla_state_t = states()
    f["o_ret"] = [_mm(f["scores"][h], f["rv"][h]) + _mm(f["qd"][h], ret_state[h]) for h in HEADS]
    f["o_gla"] = [_mm(f["att"][h], f["gv"][h]) + _mm_nt(f["qb"][h], gla_state_t[h]) for h in HEADS]
    return f


def _interleave(generators):
    live = list(generators)
    while live:
        for g in list(live):
            try:
                next(g)
            except StopIteration:
                live.remove(g)


def _mixer_fwd(proj, tables, wg_p, bg_p, ret_norm_w, gla_norm_w, riders):
    seq = proj.shape[0]
    n_chunks = seq // CHUNK
    per_step = min(n_chunks, CHUNKS_PER_STEP)
    n_steps = n_chunks // per_step
    n_ride = len(riders)
    rot_a, rot_b, dm_t, qdec_t, kdec_t, chunk_decay = tables

    def body(*refs):
        p_ref, ra_ref, rb_ref, dm_ref, qdec_ref, kdec_ref, wg_ref, bg_ref, wr_ref, wl_ref = refs[:10]
        ride_in, refs = refs[10:10 + n_ride], refs[10 + n_ride:]
        mix_ref, rsave_ref, ssave_ref = refs[:3]
        ride_out, refs = refs[3:3 + n_ride], refs[3 + n_ride:]
        r_sc, s_sc = refs[:2]
        gather = _ChipGather(ride_in, ride_out, refs[2:])

        @pl.when(pl.program_id(0) == 0)
        def _():
            gather.start()
            r_sc[...] = jnp.zeros_like(r_sc)
            s_sc[...] = jnp.zeros_like(s_sc)

        def one_chunk(c):
            p_c = p_ref.at[c * CHUNK:(c + 1) * CHUNK, :]
            mix_c = mix_ref.at[c * CHUNK:(c + 1) * CHUNK, :]
            before = {}

            def states():
                before["ret"] = [r_sc[h] for h in HEADS]
                before["gla"] = [s_sc[h] for h in HEADS]
                for h in HEADS:
                    rsave_ref[c, h] = before["ret"][h].astype(rsave_ref.dtype)
                    ssave_ref[c, h] = before["gla"][h]
                return before["ret"], before["gla"]

            cc, ss = _rotary_chunk(ra_ref, c, rb_ref)
            f = yield from _mixer_chunk_forward(p_c, cc, ss, dm_ref, qdec_ref, kdec_ref, wg_ref, bg_ref, states)
            for h in HEADS:
                r_sc[h] = chunk_decay[h] * before["ret"][h] + _mm_tn(f["kd"][h], f["rv"][h])
            for h in HEADS:
                s_sc[h] = before["gla"][h] * f["ebl"][:, _head(h)] + _mm_tn(f["gv"][h], f["kb"][h])
            yield
            for h in HEADS:
                on, _ = _ln(f["o_ret"][h])
                g = _cols(p_c, OFF_RG, h)
                mix_c[:, _head(h)] = (on * wr_ref[:, _head(h)] * (g * _sigmoid(g))).astype(mix_ref.dtype)
            for h in HEADS:
                o = f["o_gla"][h]
                on = o * lax.rsqrt(_rowmean(o * o) + LN_EPS)
                g = _cols(p_c, OFF_GG, h)
                mix_c[:, _head(N_HEADS + h)] = (on * wl_ref[:, _head(h)] * (g * _sigmoid(g))).astype(mix_ref.dtype)

        for c0 in range(0, per_step, CHUNKS_IN_LOCKSTEP):
            _interleave([one_chunk(c) for c in range(c0, min(per_step, c0 + CHUNKS_IN_LOCKSTEP))])

        @pl.when(pl.program_id(0) == (3 * n_steps) // 4)
        def _():
            gather.forward()

        @pl.when(pl.program_id(0) == n_steps - 1)
        def _():
            gather.finish()

    state_shape = (n_chunks, N_HEADS, HEAD_W, HEAD_W)
    state_blk = pl.BlockSpec((per_step, N_HEADS, HEAD_W, HEAD_W), lambda i: (i, 0, 0, 0))
    rot_blk = pl.BlockSpec((per_step, 8, HEAD_W), lambda i: (i, 0, 0))
    rows = per_step * CHUNK
    hbm = pl.BlockSpec(memory_space=pl.ANY)
    return pl.pallas_call(
        body, name="mixer_fwd", grid=(n_steps,),
        out_shape=(jax.ShapeDtypeStruct((seq, D_MODEL), MXU_DTYPE),
                   jax.ShapeDtypeStruct(state_shape, MXU_DTYPE), jax.ShapeDtypeStruct(state_shape, jnp.float32))
        + _exchange_out_shapes(riders, True),
        in_specs=[pl.BlockSpec((rows, N_PROJ), lambda i: (i, 0)), rot_blk, _const_spec(rot_b.shape),
                  _const_spec(dm_t.shape), _const_spec(qdec_t.shape), _const_spec(kdec_t.shape),
                  _const_spec(wg_p.shape), _const_spec(bg_p.shape), _const_spec(ret_norm_w.shape),
                  _const_spec(gla_norm_w.shape)] + [hbm] * n_ride,
        out_specs=(pl.BlockSpec((rows, D_MODEL), lambda i: (i, 0)), state_blk, state_blk) + (hbm,) * n_ride,
        scratch_shapes=[pltpu.VMEM((N_HEADS, HEAD_W, HEAD_W), jnp.float32),
                        pltpu.VMEM((N_HEADS, HEAD_W, HEAD_W), jnp.float32)] + _gather_sems(n_ride),
        compiler_params=_params(("arbitrary",)),
    )(proj, rot_a, rot_b, dm_t, qdec_t, kdec_t, wg_p, bg_p, ret_norm_w, gla_norm_w, *riders)


def _mixer_bwd(proj, dmixed, rsave, ssave, tables, wg_p, bg_p, ret_norm_w, gla_norm_w, riders):
    seq = proj.shape[0]
    n_chunks = seq // CHUNK
    per_step = min(n_chunks, CHUNKS_PER_STEP)
    n_steps = n_chunks // per_step
    n_ride = len(riders)
    rot_a, rot_b, dm_t, qdec_t, kdec_t, chunk_decay = tables
    last = n_steps - 1

    def body(*refs):
        p_blk, dmx_blk = refs[:2]
        shared_in = refs[2:13]
        ride_in, refs = refs[13:13 + n_ride], refs[13 + n_ride:]
        dp_blk, dwr_ref, dwl_ref, dwg_ref, dbg_ref = refs[:5]
        ride_out, refs = refs[5:5 + n_ride], refs[5 + n_ride:]
        dr_sc, ds_sc = refs[:2]
        exchange = _ChipScatter(ride_in, ride_out, refs[2:])

        @pl.when(pl.program_id(0) == 0)
        def _():
            exchange.start()
            dr_sc[...] = jnp.zeros_like(dr_sc)
            ds_sc[...] = jnp.zeros_like(ds_sc)
            dwr_ref[...] = jnp.zeros_like(dwr_ref)
            dwl_ref[...] = jnp.zeros_like(dwl_ref)
            dwg_ref[...] = jnp.zeros_like(dwg_ref)
            dbg_ref[...] = jnp.zeros_like(dbg_ref)

        def chunk_stages(c):
            rows = slice(c * CHUNK, (c + 1) * CHUNK)
            return one_chunk(c, p_blk.at[rows, :], dmx_blk.at[rows, :], dp_blk.at[rows, :], *shared_in,
                             dwr_ref, dwl_ref, dwg_ref, dbg_ref, dr_sc, ds_sc)

        for c0 in range(per_step, 0, -CHUNKS_IN_LOCKSTEP):
            _interleave([chunk_stages(c) for c in reversed(range(max(0, c0 - CHUNKS_IN_LOCKSTEP), c0))])

        @pl.when(pl.program_id(0) == last)
        def _():
            exchange.wait()

    def one_chunk(c, p_ref, dmx_ref, dp_ref, rsave_ref, ssave_ref, ra_ref, rb_ref, dm_ref, qdec_ref, kdec_ref,
                  wg_ref, bg_ref, wr_ref, wl_ref, dwr_ref, dwl_ref, dwg_ref, dbg_ref, dr_sc, ds_sc):
        def put(off, h, val):
            dp_ref[:, off + h * HEAD_W:off + (h + 1) * HEAD_W] = val.astype(dp_ref.dtype)

        cc, ss = _rotary_chunk(ra_ref, c, rb_ref)
        row, col = _tri_masks()
        ret_state = [rsave_ref[c, h] for h in HEADS]
        gla_state_t = [ssave_ref[c, h] for h in HEADS]
        f = yield from _mixer_chunk_forward(p_ref, cc, ss, dm_ref, qdec_ref, kdec_ref, wg_ref, bg_ref,
                                            lambda: (ret_state, gla_state_t))
        yield

        do_ret, do_gla = [], []
        for h in HEADS:
            on, rstd = _ln(f["o_ret"][h])
            g = _cols(p_ref, OFF_RG, h)
            sg = _sigmoid(g)
            dy = dmx_ref[:, _head(h)].astype(jnp.float32)
            wr = wr_ref[:, _head(h)]
            dwr_ref[:, _head(h)] += _colsum(dy * on * (g * sg))
            put(OFF_RG, h, dy * on * wr * (sg * (1.0 + g * (1.0 - sg))))
            do_ret.append(_ln_bwd(dy * wr * (g * sg), on, rstd))
        for h in HEADS:
            o = f["o_gla"][h]
            rstd = lax.rsqrt(_rowmean(o * o) + LN_EPS)
            on = o * rstd
            g = _cols(p_ref, OFF_GG, h)
            sg = _sigmoid(g)
            dy = dmx_ref[:, _head(N_HEADS + h)].astype(jnp.float32)
            wl = wl_ref[:, _head(h)]
            dwl_ref[:, _head(h)] += _colsum(dy * on * (g * sg))
            put(OFF_GG, h, dy * on * wl * (sg * (1.0 + g * (1.0 - sg))))
            don = dy * wl * (g * sg)
            do_gla.append(rstd * (don - on * _rowmean(don * on)))

        yield

        d_ret_new = [dr_sc[h] for h in HEADS]
        d_gla_new = [ds_sc[h] for h in HEADS]
        ds_raw = [_mm_nt(do_ret[h], f["rv"][h]) * dm_ref[h] for h in HEADS]
        d_att = [_mm_nt(do_gla[h], f["gv"][h]) for h in HEADS]
        dq_state = [_mm_nt(do_ret[h], ret_state[h]) for h in HEADS]
        dk_state = [_mm_nt(f["rv"][h], d_ret_new[h]) for h in HEADS]
        dqb = [_mm(do_gla[h], gla_state_t[h]) for h in HEADS]
        dkb = [_mm(f["gv"][h], d_gla_new[h]) for h in HEADS]
        for h in HEADS:
            put(OFF_RV, h, _mm_tn(f["scores"][h], do_ret[h]) + _mm(f["kd"][h], d_ret_new[h]))
        for h in HEADS:
            put(OFF_GV, h, _mm_tn(f["att"][h], do_gla[h]) + _mm_nt(f["kb"][h], d_gla_new[h]))
        for h in HEADS:
            dr_sc[h] = chunk_decay[h] * d_ret_new[h] + _mm_tn(f["qd"][h], do_ret[h])
        for h in HEADS:
            ds_sc[h] = d_gla_new[h] * f["ebl"][:, _head(h)] + _mm_tn(do_gla[h], f["qb"][h])
        yield

        dqr = [_mm(ds_raw[h], f["kr"][h]) + dq_state[h] * qdec_ref[:, _head(h)] for h in HEADS]
        dkr = [_mm_tn(ds_raw[h], f["qr"][h]) + dk_state[h] * kdec_ref[:, _head(h)] for h in HEADS]
        d_low = [jnp.where(row >= col, d_att[h], 0.0) for h in HEADS]
        d_up = [jnp.where(row < col, d_att[h], 0.0) for h in HEADS]
        dq_e = [_mm(d_low[h], f["k_i"][h]) for h in HEADS]
        dk_i = [_mm_tn(d_low[h], f["q_e"][h]) for h in HEADS]
        dq_i = [_mm(d_up[h], f["k_e"][h]) for h in HEADS]
        dk_e = [_mm_tn(d_up[h], f["q_i"][h]) for h in HEADS]
        yield
        for h in HEADS:
            put(OFF_RQ, h, (dqr[h] * cc + _swap_halves(dqr[h] * ss)) * RET_SCALE)
            put(OFF_RK, h, dkr[h] * cc + _swap_halves(dkr[h] * ss))
        row_id = lax.broadcasted_iota(jnp.int32, (CHUNK, HEAD_W), 0)
        db_heads = []
        for h in HEADS:
            hs = _head(h)
            e, ei, eb, ek, ebl = f["e"][:, hs], f["ei"][:, hs], f["eb"][:, hs], f["ek"][:, hs], f["ebl"][:, hs]
            put(OFF_GQ, h, (dq_e[h] * e + dq_i[h] * ei + dqb[h] * eb) * GLA_SCALE)
            put(OFF_GK, h, dk_e[h] * e + dk_i[h] * ei + dkb[h] * ek)
            db = (dq_e[h] * f["q_e"][h] - dq_i[h] * f["q_i"][h] + dk_e[h] * f["k_e"][h] - dk_i[h] * f["k_i"][h]
                  + dqb[h] * f["qb"][h] - dkb[h] * f["kb"][h])
            db_last = _colsum(dkb[h] * f["kb"][h]) + ebl * _colsum(gla_state_t[h] * d_gla_new[h])
            db_heads.append(db + jnp.where(row_id == CHUNK - 1, db_last, 0.0))
        db = jnp.concatenate(db_heads, axis=1)
        d_la = _running_sum(col >= row, db)
        d_logit = d_la * (1.0 / GATE_TAU) * (1.0 - _sigmoid(f["logit"]))
        put(OFF_LR, 0, _mm_nt(d_logit, wg_ref[...]))
        dwg_ref[...] += _mm_tn(f["glr"], d_logit)
        dbg_ref[...] += _colsum(d_logit)

    state_blk = pl.BlockSpec((per_step, N_HEADS, HEAD_W, HEAD_W), lambda i: (last - i, 0, 0, 0))
    rot_blk = pl.BlockSpec((per_step, 8, HEAD_W), lambda i: (last - i, 0, 0))
    width = N_HEADS * HEAD_W
    vec_out = pl.BlockSpec((1, width), lambda i: (0, 0))
    hbm = pl.BlockSpec(memory_space=pl.ANY)
    rows_blk = per_step * CHUNK
    return pl.pallas_call(
        body, name="mixer_bwd", grid=(n_steps,),
        out_shape=(jax.ShapeDtypeStruct((seq, N_PROJ), MXU_DTYPE),
                   jax.ShapeDtypeStruct((1, width), jnp.float32), jax.ShapeDtypeStruct((1, width), jnp.float32),
                   jax.ShapeDtypeStruct((HEAD_W, width), jnp.float32), jax.ShapeDtypeStruct((1, width), jnp.float32))
        + _exchange_out_shapes(riders, False),
        in_specs=[pl.BlockSpec((rows_blk, N_PROJ), lambda i: (last - i, 0)),
                  pl.BlockSpec((rows_blk, D_MODEL), lambda i: (last - i, 0)), state_blk, state_blk, rot_blk,
                  _const_spec(rot_b.shape),
                  _const_spec(dm_t.shape), _const_spec(qdec_t.shape), _const_spec(kdec_t.shape),
                  _const_spec(wg_p.shape), _const_spec(bg_p.shape), _const_spec(ret_norm_w.shape),
                  _const_spec(gla_norm_w.shape)] + [hbm] * n_ride,
        out_specs=(pl.BlockSpec((rows_blk, N_PROJ), lambda i: (last - i, 0)), vec_out, vec_out,
                   pl.BlockSpec((HEAD_W, width), lambda i: (0, 0)), vec_out) + (hbm,) * n_ride,
        scratch_shapes=[pltpu.VMEM((N_HEADS, HEAD_W, HEAD_W), jnp.float32),
                        pltpu.VMEM((N_HEADS, HEAD_W, HEAD_W), jnp.float32)] + _scatter_sems(n_ride),
        compiler_params=_params(("arbitrary",)),
    )(proj, dmixed, rsave, ssave, rot_a, rot_b, dm_t, qdec_t, kdec_t, wg_p, bg_p, ret_norm_w, gla_norm_w, *riders)


V_GATE1, V_SCALE2, V_SHIFT2, V_GATE2, V_LN1W, V_LN1B, V_LN2W, V_LN2B = range(8)
S_GATE1, S_SCALE2, S_SHIFT2, S_GATE2, S_LN1W, S_LN1B, S_LN2W, S_LN2B, S_LOSS = range(9)


def _mlp_fwd_bwd(x2, mixed, target, mod, ln_rows, w_out, w1_chunks, w2_chunks, tm):
    seq = x2.shape[0]
    n_fc, _, fc = w1_chunks.shape
    segment_of = {V_GATE1: MOD_GATE1, V_SCALE2: MOD_SCALE2, V_SHIFT2: MOD_SHIFT2, V_GATE2: MOD_GATE2}

    def body(x_ref, mx_ref, t_ref, mod_ref, ln_ref, wo_ref, w1_ref, w2_ref,
             dmx_ref, dxa_ref, a_ref, dh_ref, u2_ref, df_ref, dm_ref, sums_ref, relu_sc):
        @pl.when(pl.program_id(0) == 0)
        def _():
            sums_ref[...] = jnp.zeros_like(sums_ref)

        def vec(r):
            if r in segment_of:
                return _mod(mod_ref, segment_of[r])
            return ln_ref[r - V_LN1W:r - V_LN1W + 1, :]

        def acc(r, val):
            sums_ref[r:r + 1, :] += _colsum(val)

        xx = x_ref[...]
        m = _mm(mx_ref[...], wo_ref[...])
        z1h, rstd1 = _ln(ALPHA * xx + vec(V_GATE1) * m)
        x1 = z1h * vec(V_LN1W) + vec(V_LN1B)
        x1h, rstd0 = _ln(x1)
        u2 = (x1h * (1.0 + vec(V_SCALE2)) + vec(V_SHIFT2)).astype(MXU_DTYPE)
        u2_ref[...] = u2
        f = jnp.zeros((tm, D_MODEL), jnp.float32)
        for j in range(n_fc):
            r = jnp.maximum(_mm(u2, w1_ref[j]), 0.0)
            relu_sc[:, j * fc:(j + 1) * fc] = r
            a = (r * r).astype(MXU_DTYPE)
            a_ref[:, j * fc:(j + 1) * fc] = a
            f = f + _mm(a, w2_ref[j])
        z2h, rstd2 = _ln(ALPHA * x1 + vec(V_GATE2) * f)
        err = z2h * vec(V_LN2W) + vec(V_LN2B) - t_ref[...]
        acc(S_LOSS, err * err)
        dy = err * (1.0 / D_MODEL)
        acc(S_LN2W, dy * z2h)
        acc(S_LN2B, dy)
        dz2 = _ln_bwd(dy * vec(V_LN2W), z2h, rstd2)
        acc(S_GATE2, dz2 * f)
        df = (vec(V_GATE2) * dz2).astype(MXU_DTYPE)
        df_ref[...] = df
        du2 = jnp.zeros((tm, D_MODEL), jnp.float32)
        for j in range(n_fc):
            dh = (_mm_nt(df, w2_ref[j]) * (2.0 * relu_sc[:, j * fc:(j + 1) * fc])).astype(MXU_DTYPE)
            dh_ref[:, j * fc:(j + 1) * fc] = dh
            du2 = du2 + _mm_nt(dh, w1_ref[j])
        acc(S_SCALE2, du2 * x1h)
        acc(S_SHIFT2, du2)
        dx1 = ALPHA * dz2 + _ln_bwd(du2 * (1.0 + vec(V_SCALE2)), x1h, rstd0)
        acc(S_LN1W, dx1 * z1h)
        acc(S_LN1B, dx1)
        dz1 = _ln_bwd(dx1 * vec(V_LN1W), z1h, rstd1)
        acc(S_GATE1, dz1 * m)
        dxa_ref[...] = ALPHA * dz1
        dm = (vec(V_GATE1) * dz1).astype(MXU_DTYPE)
        dm_ref[...] = dm
        dmx_ref[...] = _mm_nt(dm, wo_ref[...])

    tile = lambda width: pl.BlockSpec((tm, width), lambda i: (i, 0))
    f32 = lambda width: jax.ShapeDtypeStruct((seq, width), jnp.float32)
    b16 = lambda width: jax.ShapeDtypeStruct((seq, width), MXU_DTYPE)
    return pl.pallas_call(
        body, name="mlp_fwd_bwd", grid=(seq // tm,),
        out_shape=(f32(D_MODEL), f32(D_MODEL), b16(D_FF), b16(D_FF), b16(D_MODEL), b16(D_MODEL), b16(D_MODEL),
                   jax.ShapeDtypeStruct((16, D_MODEL), jnp.float32)),
        in_specs=[tile(D_MODEL), tile(D_MODEL), tile(D_MODEL), _const_spec(mod.shape), _const_spec(ln_rows.shape),
                  _const_spec(w_out.shape), _const_spec(w1_chunks.shape), _const_spec(w2_chunks.shape)],
        out_specs=(tile(D_MODEL), tile(D_MODEL), tile(D_FF), tile(D_FF), tile(D_MODEL), tile(D_MODEL),
                   tile(D_MODEL), pl.BlockSpec((16, D_MODEL), lambda i: (0, 0))),
        scratch_shapes=[pltpu.VMEM((tm, D_FF), jnp.float32)],
        compiler_params=_params(("arbitrary",)),
    )(x2, mixed, target, mod, ln_rows, w_out, w1_chunks, w2_chunks)


def _grad_matmul(a, b, name, tn, blocks_are_rows, riders=()):
    seq, m_dim = a.shape
    n_dim = b.shape[1]
    tk = min(seq, GRAD_TOKEN_TILE)
    nk = seq // tk
    n_ride = len(riders)
    if blocks_are_rows:
        tm = m_dim // N_CHIP
        assert tn == n_dim
        per_step = N_CHIP if m_dim <= GRAD_ROWS_PER_STEP else 1
        grid = (N_CHIP // per_step, 1, nk)
        out_map = lambda i, j, k: (i, 0, 0)
    else:
        tm = m_dim
        assert tn * N_CHIP == n_dim
        per_step = 1
        grid = (1, N_CHIP, nk)
        out_map = lambda i, j, k: (j, 0, 0)
    n_blocks = grid[0] * grid[1]
    rows = per_step * tm

    def body(*refs):
        a_ref, b_ref = refs[:2]
        ride_in, refs = refs[2:2 + n_ride], refs[2 + n_ride:]
        o_ref = refs[0]
        ride_out, refs = refs[1:1 + n_ride], refs[1 + n_ride:]
        acc_sc = refs[0]
        exchange = _ChipScatter(ride_in, ride_out, refs[1:]) if n_ride else None
        block = pl.program_id(0) + pl.program_id(1)
        k = pl.program_id(2)

        if exchange is not None:
            @pl.when((block == 0) & (k == 0))
            def _():
                exchange.start()

        @pl.when(k == 0)
        def _():
            acc_sc[...] = jnp.zeros_like(acc_sc)

        acc_sc[...] += _mm_tn(a_ref[...], b_ref[...])

        @pl.when(k == nk - 1)
        def _():
            for p in range(per_step):
                o_ref[p] = acc_sc[p * tm:(p + 1) * tm, :].astype(o_ref.dtype)

        if exchange is not None:
            @pl.when((block == n_blocks - 1) & (k == nk - 1))
            def _():
                exchange.wait()

    hbm = pl.BlockSpec(memory_space=pl.ANY)
    out = pl.pallas_call(
        body, name=name, grid=grid,
        out_shape=(jax.ShapeDtypeStruct((N_CHIP, tm, tn), WIRE_DTYPE),) + _exchange_out_shapes(riders, False),
        in_specs=[pl.BlockSpec((tk, rows), lambda i, j, k: (k, i)), pl.BlockSpec((tk, tn), lambda i, j, k: (k, j))]
        + [hbm] * n_ride,
        out_specs=(pl.BlockSpec((per_step, tm, tn), out_map),) + (hbm,) * n_ride,
        scratch_shapes=[pltpu.VMEM((rows, tn), jnp.float32)] + (_scatter_sems(n_ride) if n_ride else []),
        compiler_params=_params(("arbitrary", "arbitrary", "arbitrary")),
    )(a, b, *riders)
    return out if n_ride else out[0]


def _grad_matmul_full(a, b, name, tm, riders):
    seq, m_dim = a.shape
    n_dim = b.shape[1]
    tk = min(seq, GRAD_TOKEN_TILE)
    nk = seq // tk
    n_blocks = m_dim // tm
    n_ride = len(riders)
    assert m_dim % tm == 0

    def body(*refs):
        a_ref, b_ref = refs[:2]
        ride_in, refs = refs[2:2 + n_ride], refs[2 + n_ride:]
        o_ref = refs[0]
        ride_out, refs = refs[1:1 + n_ride], refs[1 + n_ride:]
        acc_sc = refs[0]
        swap = _SiblingSwap(ride_in, ride_out, refs[1:])
        i, k = pl.program_id(0), pl.program_id(1)

        @pl.when((i == 0) & (k == 0))
        def _():
            swap.start()

        @pl.when(k == 0)
        def _():
            acc_sc[...] = jnp.zeros_like(acc_sc)

        acc_sc[...] += _mm_tn(a_ref[...], b_ref[...])

        @pl.when(k == nk - 1)
        def _():
            o_ref[...] = acc_sc[...].astype(o_ref.dtype)

        @pl.when((i == n_blocks - 1) & (k == nk - 1))
        def _():
            swap.wait()

    hbm = pl.BlockSpec(memory_space=pl.ANY)
    return pl.pallas_call(
        body, name=name, grid=(n_blocks, nk),
        out_shape=(jax.ShapeDtypeStruct((m_dim, n_dim), WIRE_DTYPE),)
        + tuple(jax.ShapeDtypeStruct(r.shape, r.dtype) for r in riders),
        in_specs=[pl.BlockSpec((tk, tm), lambda i, k: (k, i)), pl.BlockSpec((tk, n_dim), lambda i, k: (k, 0))]
        + [hbm] * n_ride,
        out_specs=(pl.BlockSpec((tm, n_dim), lambda i, k: (i, 0)),) + (hbm,) * n_ride,
        scratch_shapes=[pltpu.VMEM((tm, n_dim), jnp.float32)] + _swap_sems(n_ride),
        compiler_params=_params(("arbitrary", "arbitrary")),
    )(a, b, *riders)


def _adam_pair(w, g_mine, g_sibling, m, v, name):
    rows, cols = w.shape
    tc = min(cols, ELEMENTWISE_COLS)

    def total(ref):
        if len(ref.shape) == 2:
            return ref[...]
        acc = ref[0].astype(jnp.float32)
        for j in range(1, ref.shape[0]):
            acc = acc + ref[j].astype(jnp.float32)
        return acc

    def body(w_ref, ga_ref, gb_ref, m_ref, v_ref, g_ref, dl_ref, m2_ref, v2_ref):
        g = total(ga_ref) + total(gb_ref)
        delta, m2, v2 = _adam(w_ref[...], g, m_ref[...], v_ref[...])
        g_ref[...] = g
        dl_ref[...] = delta
        m2_ref[...] = m2
        v2_ref[...] = v2

    blk = pl.BlockSpec((rows, tc), lambda i: (0, i))
    g_blk = lambda a: blk if a.ndim == 2 else pl.BlockSpec((a.shape[0], rows, tc), lambda i: (0, 0, i))
    out = jax.ShapeDtypeStruct((rows, cols), jnp.float32)
    return pl.pallas_call(
        body, name=name, grid=(cols // tc,),
        out_shape=(out, out, out, out),
        in_specs=[blk, g_blk(g_mine), g_blk(g_sibling), blk, blk], out_specs=(blk,) * 4,
        compiler_params=_params(("arbitrary",)),
    )(w, g_mine, g_sibling, m, v)


def _sum_devices(gathered, layout):
    def body(g_ref, *o_refs):
        total = g_ref[0]
        for d in range(1, N_DEV):
            total = total + g_ref[d]
        for (first, (rows_out, cols_out)), o_ref in zip(layout, o_refs):
            per_row = cols_out // 128
            for r in range(rows_out):
                for k in range(per_row):
                    src = first + r * per_row + k
                    o_ref[r:r + 1, k * 128:(k + 1) * 128] = total[src:src + 1, :]
        tail = total[total.shape[0] - 8:, :]
        o_refs[-1][...] = jnp.full((1, 128), jnp.sum(tail), jnp.float32)

    out_shape = tuple(jax.ShapeDtypeStruct(shape, jnp.float32) for _, shape in layout)
    return pl.pallas_call(
        body, name="sum_devices",
        out_shape=out_shape + (jax.ShapeDtypeStruct((1, 128), jnp.float32),),
    )(gathered)


def _adam_small(params):
    n = len(params)

    def body(*refs):
        ins, outs = refs[:4 * n], refs[4 * n:]
        for i in range(n):
            w_ref, g_ref, m_ref, v_ref = ins[4 * i:4 * i + 4]
            delta, m2, v2 = _adam(w_ref[...], g_ref[...], m_ref[...], v_ref[...])
            outs[3 * i][...] = delta
            outs[3 * i + 1][...] = m2
            outs[3 * i + 2][...] = v2

    out_shape = tuple(jax.ShapeDtypeStruct(p[0].shape, jnp.float32) for p in params for _ in range(3))
    out = pl.pallas_call(body, name="adam_small", out_shape=out_shape)(*[t for p in params for t in p])
    return [out[3 * i:3 * i + 3] for i in range(n)]


def _pad_heads(w):
    lead = w.shape[:-1]
    w = w.reshape(lead + (N_HEADS, GLA_DK))
    w = jnp.pad(w, [(0, 0)] * len(lead) + [(0, 0), (0, HEAD_W - GLA_DK)])
    return w.reshape(lead + (N_HEADS * HEAD_W,))


def _unpad_heads(w):
    lead = w.shape[:-1]
    return w.reshape(lead + (N_HEADS, HEAD_W))[..., :GLA_DK].reshape(lead + (N_HEADS * GLA_DK,))


def _pad_head_rows(w):
    w = w.reshape(N_HEADS, GLA_DK, w.shape[-1])
    return jnp.pad(w, ((0, 0), (0, HEAD_W - GLA_DK), (0, 0))).reshape(N_HEADS * HEAD_W, w.shape[-1])


def _unpad_head_rows(w):
    return w.reshape(N_HEADS, HEAD_W, w.shape[-1])[:, :GLA_DK].reshape(N_HEADS * GLA_DK, w.shape[-1])


def _pad_w_in_rows(stack):
    w = stack.reshape(-1, stack.shape[-1])
    return jnp.concatenate([
        w[:2048], _pad_head_rows(w[2048:2304]), _pad_head_rows(w[2304:2560]), w[2560:3584],
        jnp.pad(w[3584:3600], ((0, HEAD_W - GATE_RANK), (0, 0)))], axis=0)


def _unpad_w_in_stack(g, per):
    segments = [(0, g[:2048]), (2048, _unpad_head_rows(g[OFF_GQ:OFF_GQ + 512])),
                (2304, _unpad_head_rows(g[OFF_GK:OFF_GK + 512])), (2560, g[OFF_GV:OFF_LR]),
                (3584, g[OFF_LR:OFF_LR + GATE_RANK])]
    blocks = []
    for j in range(N_CHIP):
        lo, hi = j * per, (j + 1) * per
        pieces = []
        for start, rows in segments:
            a, b = max(lo, start), min(hi, start + rows.shape[0])
            if a < b:
                pieces.append(rows[a - start:b - start])
        blocks.append(jnp.concatenate(pieces, axis=0))
    return jnp.stack(blocks)


def _col_major(w):
    return jnp.transpose(w, (2, 0, 1)).reshape(w.shape[2], w.shape[1])


def _rows128(a):
    return a.reshape(-1, 128)


def kernel(x, c, w_ada, b_ada, w_in, ret_norm_w, gla_gate_w, gla_gate_b, gla_norm_w, w_out, ln1_w, ln1_b, w_ff1, w_ff2, ln2_w, ln2_b, loss_target, m_w_ada, m_b_ada, m_w_in, m_ret_norm_w, m_gla_gate_w, m_gla_gate_b, m_gla_norm_w, m_w_out, m_ln1_w, m_ln1_b, m_w_ff1, m_w_ff2, m_ln2_w, m_ln2_b, v_w_ada, v_b_ada, v_w_in, v_ret_norm_w, v_gla_gate_w, v_gla_gate_b, v_gla_norm_w, v_w_out, v_ln1_w, v_ln1_b, v_w_ff1, v_w_ff2, v_ln2_w, v_ln2_b):
    seq = x.shape[1]
    tm = min(seq, TOKEN_TILE)
    tm_in = min(seq, INPROJ_TOKEN_TILE)
    xi, yi, _ = _mesh_pos()
    chip = 2 * xi + yi
    x2, target = x[0], loss_target[0]
    ada_cols = w_ada.shape[2]
    in_cols = w_in.shape[2]
    gate_cols = gla_gate_w.shape[2]

    b_blk = lax.dynamic_slice(b_ada, (0, chip * ada_cols), (1, ada_cols))
    g0, _, mod, w_in_stack = _prologue(jnp.concatenate([_rows128(c), _rows128(gla_gate_w[0])], axis=0), w_ada[0],
                                       b_blk, _col_major(w_in.astype(WIRE_DTYPE)))
    c_all = g0[:, :8].reshape(N_DEV, D_MODEL)
    gate_w_full = jnp.concatenate([g0[2 * j, 8:16].reshape(GATE_RANK, gate_cols) for j in range(N_CHIP)], axis=1)
    wg_p = jnp.pad(_pad_heads(gate_w_full), ((0, HEAD_W - GATE_RANK), (0, 0)))
    bg_p = _pad_heads(gla_gate_b)
    w_in_pt = _pad_w_in_rows(w_in_stack).astype(MXU_DTYPE)
    w_in_p = jnp.transpose(w_in_pt)

    proj, u, w2_stack = _inproj_fwd(x2, mod, w_in_p, tm_in, [w_ff2[0].astype(WIRE_DTYPE)])
    rot_a, rot_b = _rotary_tables(seq)
    dm_t, qdec_t, kdec_t, chunk_decay = _decay_tables()
    tables = (rot_a, rot_b, dm_t, qdec_t, kdec_t, chunk_decay)
    mixed, rsave, ssave, w_out_stack, w1_stack = _mixer_fwd(
        proj, tables, wg_p, bg_p, ret_norm_w, gla_norm_w,
        [w_out[0].astype(WIRE_DTYPE), w_ff1[0].astype(WIRE_DTYPE)])
    w_out_full = w_out_stack.reshape(D_MODEL, D_MODEL).astype(MXU_DTYPE)
    w1_chunks = w1_stack.astype(MXU_DTYPE)
    w2_chunks = w2_stack.astype(MXU_DTYPE)

    ln_rows = jnp.concatenate([ln1_w, ln1_b, ln2_w, ln2_b], axis=0)
    dmixed, dxa, act, dh, u2, df, dm, sums2 = _mlp_fwd_bwd(x2, mixed, target, mod, ln_rows, w_out_full, w1_chunks,
                                                           w2_chunks, tm)

    g_out_stack = _grad_matmul(mixed, dm, "grad_w_out", D_MODEL, True)
    g_ff1_stack, r_out = _grad_matmul(u2, dh, "grad_w_ff1", D_FF // N_CHIP, False, [g_out_stack])
    g_ff2_stack = _grad_matmul(act, df, "grad_w_ff2", D_MODEL, True)
    dproj, d_ret_norm, d_gla_norm, d_wg_p, d_bg_p, r_ff1, r_ff2 = _mixer_bwd(
        proj, dmixed, rsave, ssave, tables, wg_p, bg_p, ret_norm_w, gla_norm_w, [g_ff1_stack, g_ff2_stack])
    early = ["w_out", "w_ff1", "w_ff2"]
    partial = dict(zip(early, [r_out, r_ff1, r_ff2]))
    g_in_t, *swapped_early = _grad_matmul_full(dproj, u, "grad_w_in", N_PROJ // 3, [partial[n] for n in early])
    swapped = dict(zip(early, swapped_early))
    g_in_stack = _unpad_w_in_stack(g_in_t, in_cols)
    grad_x, sums1, r_in = _inproj_bwd(dproj, x2, dxa, mod, w_in_pt, tm_in, [g_in_stack])

    sources = [sums1, sums2, d_ret_norm, _unpad_heads(d_bg_p), d_gla_norm, _unpad_heads(d_wg_p[:GATE_RANK])]
    pieces = [(0, 0, 0), (0, 1, 8), (1, S_GATE1, 16), (1, S_SHIFT2, 24), (1, S_SCALE2, 32), (1, S_GATE2, 40),
              (1, S_LN1W, 48), (1, S_LN1B, 56), (1, S_LN2W, 64), (1, S_LN2B, 72), (2, 0, 80), (3, 0, 88), (4, 0, 96)]
    pieces += [(5, r, 104 + 2 * r) for r in range(GATE_RANK)] + [(1, S_LOSS, 136)]
    partial["w_in"] = r_in
    g2, swapped["w_in"] = _gather_rows(sources, pieces, 144, "gather_small", [r_in])
    (grad_b_ada, grad_ln1_w, grad_ln1_b, grad_ln2_w, grad_ln2_b, grad_ret_norm, grad_gate_b, grad_gla_norm,
     grad_gate_w_full, loss_sum) = _sum_devices(g2, [
         (0, (1, 6 * D_MODEL)), (48, (1, D_MODEL)), (56, (1, D_MODEL)), (64, (1, D_MODEL)), (72, (1, D_MODEL)),
         (80, (1, 512)), (88, (1, 256)), (96, (1, 512)), (104, (GATE_RANK, 256))])
    loss = 0.5 / D_MODEL * loss_sum[0, 0]
    grad_gate_w = lax.dynamic_slice(grad_gate_w_full, (0, chip * gate_cols), (GATE_RANK, gate_cols))

    small_grads = [grad_b_ada, grad_ln1_w, grad_ln1_b, grad_ln2_w, grad_ln2_b, grad_ret_norm, grad_gate_b,
                   grad_gla_norm, grad_gate_w[None]]
    small_out = _adam_small(list(zip(
        [b_ada, ln1_w, ln1_b, ln2_w, ln2_b, ret_norm_w, gla_gate_b, gla_norm_w, gla_gate_w], small_grads,
        [m_b_ada, m_ln1_w, m_ln1_b, m_ln2_w, m_ln2_b, m_ret_norm_w, m_gla_gate_b, m_gla_norm_w, m_gla_gate_w],
        [v_b_ada, v_ln1_w, v_ln1_b, v_ln2_w, v_ln2_b, v_ret_norm_w, v_gla_gate_b, v_gla_norm_w, v_gla_gate_w])))
    sm_delta, sm_m, sm_v = [[o[k] for o in small_out] for k in range(3)]

    dmod_all = g2[:, 0:48].reshape(N_DEV, 6 * D_MODEL)
    dmod_blk = lax.dynamic_slice(dmod_all, (0, chip * ada_cols), (N_DEV, ada_cols))
    ada_out = _ada_bwd_adam(jnp.transpose(c_all), dmod_blk, w_ada[0], m_w_ada[0], v_w_ada[0])
    ada_g, ada_delta, ada_m, ada_v = [t[None] for t in ada_out]

    big = {}
    for n, w, m, v in zip(["w_in", "w_out", "w_ff1", "w_ff2"], [w_in, w_out, w_ff1, w_ff2],
                          [m_w_in, m_w_out, m_w_ff1, m_w_ff2], [v_w_in, v_w_out, v_w_ff1, v_w_ff2]):
        mine, theirs = partial[n], swapped[n]
        if n == "w_in":
            out = _adam_pair(_col_major(w), mine, theirs, _col_major(m), _col_major(v), "adam_" + n)
            big[n] = [jnp.transpose(t.reshape(t.shape[0], 1, t.shape[1]), (1, 2, 0)) for t in out]
        else:
            big[n] = [t[None] for t in _adam_pair(w[0], mine, theirs, m[0], v[0], "adam_" + n)]

    def assemble(ada, smalls, k):
        b_ada_o, ln1w_o, ln1b_o, ln2w_o, ln2b_o, ret_o, gb_o, gln_o, gw_o = smalls
        return [ada, b_ada_o, big["w_in"][k], ret_o, gw_o, gb_o, gln_o, big["w_out"][k], ln1w_o, ln1b_o,
                big["w_ff1"][k], big["w_ff2"][k], ln2w_o, ln2b_o]

    grads = assemble(ada_g, small_grads, 0)
    deltas = assemble(ada_delta, sm_delta, 1)
    new_m = assemble(ada_m, sm_m, 2)
    new_v = assemble(ada_v, sm_v, 3)
    return (loss, grad_x[None], *grads, *deltas, *new_m, *new_v)
```

```python
import numpy as np
import jax
import jax.numpy as jnp
from jax import lax
from jax.experimental import pallas as pl
from jax.experimental.pallas import tpu as pltpu

D_MODEL = 1024
D_FF = 4096
CHUNK = 64
N_HEADS = 4
HEAD_W = 128
GLA_DK = 64
GATE_RANK = 16
GATE_TAU = 16.0
LN_EPS = 1e-5
ALPHA = 2.0 ** 0.25
ROPE_BASE = 10000.0
RET_SCALE = float(HEAD_W) ** -0.5
GLA_SCALE = float(GLA_DK) ** -0.5

ADAM_LR = 0.001
ADAM_B1 = 0.9
ADAM_B2 = 0.999
ADAM_EPS = 1e-08
ADAM_WD = 0.01
ADAM_STEP = 10

OFF_RQ, OFF_RK, OFF_RV, OFF_RG = 0, 512, 1024, 1536
OFF_GQ, OFF_GK, OFF_GV, OFF_GG, OFF_LR = 2048, 2560, 3072, 3584, 4096
N_PROJ = 4224

N_DEV = 8
N_CHIP = 4
MESH = pl.DeviceIdType.MESH
MXU_DTYPE = jnp.bfloat16
WIRE_DTYPE = jnp.bfloat16
VMEM_LIMIT = 60 * 1024 * 1024
TOKEN_TILE = 256
INPROJ_TOKEN_TILE = 512
CHUNKS_PER_STEP = 8
CHUNKS_IN_LOCKSTEP = 4
GRAD_ROWS_PER_STEP = 1024
GRAD_TOKEN_TILE = 2048
ELEMENTWISE_COLS = 512
HIGHEST = lax.Precision.HIGHEST


def _mm(a, b):
    return jnp.dot(a.astype(MXU_DTYPE), b.astype(MXU_DTYPE), preferred_element_type=jnp.float32)


def _mm_nt(a, b):
    return lax.dot_general(a.astype(MXU_DTYPE), b.astype(MXU_DTYPE), (((1,), (1,)), ((), ())),
                           preferred_element_type=jnp.float32)


def _mm_tn(a, b):
    return lax.dot_general(a.astype(MXU_DTYPE), b.astype(MXU_DTYPE), (((0,), (0,)), ((), ())),
                           preferred_element_type=jnp.float32)


def _mm32(a, b):
    return jnp.dot(a, b, precision=HIGHEST, preferred_element_type=jnp.float32)


def _running_sum(mask, a):
    m = mask.astype(jnp.bfloat16)
    hi = a.astype(jnp.bfloat16)
    rest = a - hi.astype(jnp.float32)
    mid = rest.astype(jnp.bfloat16)
    lo = (rest - mid.astype(jnp.float32)).astype(jnp.bfloat16)
    dot = lambda t: jnp.dot(m, t, preferred_element_type=jnp.float32)
    return dot(hi) + dot(mid) + dot(lo)


def _rowmean(a):
    return jnp.mean(a, axis=-1, keepdims=True)


def _colsum(a):
    return jnp.sum(a, axis=0, keepdims=True)


def _ln(z):
    zc = z - _rowmean(z)
    rstd = lax.rsqrt(_rowmean(zc * zc) + LN_EPS)
    return zc * rstd, rstd


def _ln_bwd(dzh, zh, rstd):
    return rstd * (dzh - _rowmean(dzh) - zh * _rowmean(dzh * zh))


def _sigmoid(a):
    return 1.0 / (1.0 + jnp.exp(-a))


def _log_sigmoid(a):
    return jnp.minimum(a, 0.0) - jnp.log(1.0 + jnp.exp(-jnp.abs(a)))


def _swap_halves(a):
    return pltpu.roll(a, HEAD_W // 2, 1)


def _tri_masks():
    row = lax.broadcasted_iota(jnp.int32, (CHUNK, CHUNK), 0)
    col = lax.broadcasted_iota(jnp.int32, (CHUNK, CHUNK), 1)
    return row, col


def _const_spec(shape):
    zeros = (0,) * len(shape)
    return pl.BlockSpec(shape, lambda *_: zeros, pipeline_mode=pl.Buffered(1))


def _params(semantics):
    return pltpu.CompilerParams(dimension_semantics=semantics, vmem_limit_bytes=VMEM_LIMIT)


def _decay_tables():
    log_gamma = np.log(1.0 - 2.0 ** (-5.0 - np.arange(N_HEADS, dtype=np.float64)))
    idx = np.arange(CHUNK, dtype=np.float64)
    dist = np.abs(idx[:, None] - idx[None, :])
    intra = np.exp(log_gamma[:, None, None] * dist)
    kdec = np.exp(log_gamma[None, :] * (CHUNK - 1.0 - idx)[:, None])
    qdec = np.exp(log_gamma[None, :] * (idx + 1.0)[:, None])
    chunk_decay = np.exp(log_gamma * CHUNK)
    lanes = lambda t: np.repeat(t, HEAD_W, axis=1).astype(np.float32)
    return (jnp.asarray(intra.astype(np.float32)), jnp.asarray(lanes(qdec)), jnp.asarray(lanes(kdec)),
            [float(np.float32(v)) for v in chunk_decay])


def _rotary_tables(seq):
    half = HEAD_W // 2
    inv = 1.0 / (ROPE_BASE ** jnp.linspace(0.0, 1.0, half, dtype=jnp.float32))
    both = lambda t: jnp.concatenate([t, t], axis=-1)
    ang_a = jnp.arange(0, seq, CHUNK, dtype=jnp.float32)[:, None] * inv[None, :]
    rot_a = jnp.stack([both(jnp.cos(ang_a)), both(jnp.sin(ang_a))], axis=1)
    rot_a = jnp.pad(rot_a, ((0, 0), (0, 6), (0, 0)))
    ang_b = jnp.arange(CHUNK, dtype=jnp.float32)[:, None] * inv[None, :]
    cos_b, sin_b = both(jnp.cos(ang_b)), both(jnp.sin(ang_b))
    sign = jnp.concatenate([-jnp.ones((half,), jnp.float32), jnp.ones((half,), jnp.float32)])
    return rot_a, jnp.stack([cos_b, sin_b, cos_b * sign, sin_b * sign])


def _rotary_chunk(ra_ref, c, rb_ref):
    cos_a, sin_a = ra_ref[c, 0:1, :], ra_ref[c, 1:2, :]
    return cos_a * rb_ref[0] - sin_a * rb_ref[1], sin_a * rb_ref[2] + cos_a * rb_ref[3]


def _mesh_pos():
    return lax.axis_index("x"), lax.axis_index("y"), lax.axis_index("c")


def _flip(v, bit):
    return 1 - v if bit else v


def _gather_rows(sources, pieces, rows, name, swaps):
    n_src, n = len(sources), len(swaps)

    def body(*refs):
        src_refs, refs = refs[:n_src], refs[n_src:]
        out_ref = refs[n]
        v_sc = refs[1 + 2 * n]
        swap = _SiblingSwap(refs[:n], refs[1 + n:1 + 2 * n], refs[4 + 2 * n:])
        swap.start()
        v_sc[...] = jnp.zeros_like(v_sc)
        for s, row, first in pieces:
            for k in range(src_refs[s].shape[1] // 128):
                v_sc[first + k:first + k + 1, :] = src_refs[s][row:row + 1, k * 128:(k + 1) * 128]
        _all_devices_exchange(v_sc, out_ref, refs[2 + 2 * n], refs[3 + 2 * n])
        swap.wait()

    hbm = pl.BlockSpec(memory_space=pl.ANY)
    vmem = pl.BlockSpec(memory_space=pltpu.VMEM)
    return pl.pallas_call(
        body, name=name,
        out_shape=(jax.ShapeDtypeStruct((N_DEV, rows, 128), jnp.float32),)
        + tuple(jax.ShapeDtypeStruct(a.shape, a.dtype) for a in swaps),
        in_specs=[vmem] * n_src + [hbm] * n,
        out_specs=(vmem,) + (hbm,) * n,
        scratch_shapes=[pltpu.VMEM((rows, 128), jnp.float32)] + _all_devices_sems() + _swap_sems(n),
    )(*sources, *swaps)


def _all_devices_sems():
    return [pltpu.SemaphoreType.DMA((N_DEV - 1,)), pltpu.SemaphoreType.DMA((N_DEV - 1,))]


def _all_devices_exchange(v_ref, out_ref, send_sems, recv_sems):
    x, y, c = _mesh_pos()
    me = 4 * x + 2 * y + c
    out_ref[me] = v_ref[...]
    sends, recvs = [], []
    for k in range(1, N_DEV):
        px, py, pc = _flip(x, (k >> 2) & 1), _flip(y, (k >> 1) & 1), _flip(c, k & 1)
        peer = 4 * px + 2 * py + pc
        sends.append(pltpu.make_async_remote_copy(
            src_ref=v_ref, dst_ref=out_ref.at[me], send_sem=send_sems.at[k - 1], recv_sem=recv_sems.at[k - 1],
            device_id=(px, py, pc), device_id_type=MESH))
        recvs.append(pltpu.make_async_remote_copy(
            src_ref=v_ref, dst_ref=out_ref.at[peer], send_sem=send_sems.at[k - 1], recv_sem=recv_sems.at[k - 1],
            device_id=(px, py, pc), device_id_type=MESH))
    for cp in sends:
        cp.start()
    for cp in recvs:
        cp.wait_recv()
    for cp in sends:
        cp.wait_send()


def _prologue(cond_rows, w_ada_blk, b_blk, w_in_t):
    cols = w_ada_blk.shape[1]
    groups = cols // 128
    c_rows = D_MODEL // 128

    def body(cond_ref, w_ref, b_ref, win_ref, cond_all_ref, mod_all_ref, mod_ref, stack_ref, mod_sc, *sems):
        gather = _ChipGather([win_ref], [stack_ref], sems[:5])
        gather.start()
        _all_devices_exchange(cond_ref, cond_all_ref, sems[5], sems[6])
        acc = jnp.broadcast_to(b_ref[...], (N_DEV, cols))
        for r in range(c_rows):
            cv = cond_all_ref[:, r, :]
            acc = acc + _mm32(cv * _sigmoid(cv), w_ref[r * 128:(r + 1) * 128, :])
        for k in range(groups):
            mod_sc[k] = acc[:, k * 128:(k + 1) * 128]
        _all_devices_exchange(mod_sc, mod_all_ref, sems[7], sems[8])
        x, y, c = _mesh_pos()
        me = 4 * x + 2 * y + c
        for j in range(N_CHIP):
            for k in range(groups):
                lane = j * cols + k * 128
                mod_ref[:, lane:lane + 128] = mod_all_ref[2 * j, k, pl.ds(me, 1), :]
        gather.forward()
        gather.finish()

    vmem = pl.BlockSpec(memory_space=pltpu.VMEM)
    hbm = pl.BlockSpec(memory_space=pl.ANY)
    return pl.pallas_call(
        body, name="prologue",
        out_shape=(jax.ShapeDtypeStruct((N_DEV,) + cond_rows.shape, jnp.float32),
                   jax.ShapeDtypeStruct((N_DEV, groups, N_DEV, 128), jnp.float32),
                   jax.ShapeDtypeStruct((1, N_CHIP * cols), jnp.float32))
        + _exchange_out_shapes([w_in_t], True),
        in_specs=[vmem, vmem, vmem, hbm],
        out_specs=(vmem, vmem, vmem, hbm),
        scratch_shapes=[pltpu.VMEM((groups, N_DEV, 128), jnp.float32)] + _gather_sems(1)
        + _all_devices_sems() + _all_devices_sems(),
        compiler_params=pltpu.CompilerParams(vmem_limit_bytes=VMEM_LIMIT),
    )(cond_rows, w_ada_blk, b_blk, w_in_t)


def _exchange_out_shapes(arrays, gather):
    return tuple(jax.ShapeDtypeStruct((N_CHIP,) + a.shape if gather else a.shape, a.dtype) for a in arrays)


def _scatter_sems(n):
    n_sem = n * (N_CHIP - 1)
    return [pltpu.SemaphoreType.DMA((n_sem,)), pltpu.SemaphoreType.DMA((n_sem,)), pltpu.SemaphoreType.DMA((n,))]


def _gather_sems(n):
    n_sem = n * (N_CHIP - 1)
    return [pltpu.SemaphoreType.DMA((n_sem,))] * 4 + [pltpu.SemaphoreType.DMA((n,))]


def _peer_chips(x, y):
    out = []
    for k in range(1, N_CHIP):
        px, py = _flip(x, (k >> 1) & 1), _flip(y, k & 1)
        out.append((px, py, 2 * px + py))
    return out


class _ChipScatter:
    def __init__(self, ins, outs, sems):
        send_sems, recv_sems, local_sems = sems
        x, y, c = _mesh_pos()
        chip = 2 * x + y
        self.local, self.sends, self.recvs = [], [], []
        for i in range(len(ins)):
            self.local.append(pltpu.make_async_copy(ins[i].at[chip], outs[i].at[chip], local_sems.at[i]))
            for k, (px, py, peer_chip) in enumerate(_peer_chips(x, y)):
                sem = i * (N_CHIP - 1) + k
                src = ins[i].at[peer_chip]
                self.sends.append(pltpu.make_async_remote_copy(
                    src_ref=src, dst_ref=outs[i].at[chip], send_sem=send_sems.at[sem], recv_sem=recv_sems.at[sem],
                    device_id=(px, py, c), device_id_type=MESH))
                self.recvs.append(pltpu.make_async_remote_copy(
                    src_ref=src, dst_ref=outs[i].at[peer_chip], send_sem=send_sems.at[sem], recv_sem=recv_sems.at[sem],
                    device_id=(px, py, c), device_id_type=MESH))

    def start(self):
        for cp in self.local + self.sends:
            cp.start()

    def wait(self):
        for cp in self.recvs:
            cp.wait_recv()
        for cp in self.sends:
            cp.wait_send()
        for cp in self.local:
            cp.wait()


class _ChipGather:
    def __init__(self, ins, outs, sems):
        ici_send, ici_recv, d2d_send, d2d_recv, local_sems = sems
        x, y, c = _mesh_pos()
        chip = 2 * x + y
        self.local, self.ici_sends, self.ici_recvs, self.d2d_sends, self.d2d_recvs = [], [], [], [], []
        for i in range(len(ins)):
            half = ins[i].shape[-1] // 2
            assert half % 128 == 0
            lead = (slice(None),) * (len(ins[i].shape) - 1)
            mine = lead + (pl.ds(pl.multiple_of(c * half, 128), half),)
            theirs = lead + (pl.ds(pl.multiple_of((1 - c) * half, 128), half),)
            self.local.append(pltpu.make_async_copy(ins[i], outs[i].at[chip], local_sems.at[i]))
            for k, (px, py, peer_chip) in enumerate(_peer_chips(x, y)):
                sem = i * (N_CHIP - 1) + k
                self.ici_sends.append(pltpu.make_async_remote_copy(
                    src_ref=ins[i].at[mine], dst_ref=outs[i].at[chip].at[mine],
                    send_sem=ici_send.at[sem], recv_sem=ici_recv.at[sem], device_id=(px, py, c), device_id_type=MESH))
                landed = outs[i].at[peer_chip].at[mine]
                self.ici_recvs.append(pltpu.make_async_remote_copy(
                    src_ref=ins[i].at[mine], dst_ref=landed,
                    send_sem=ici_send.at[sem], recv_sem=ici_recv.at[sem], device_id=(px, py, c), device_id_type=MESH))
                self.d2d_sends.append(pltpu.make_async_remote_copy(
                    src_ref=landed, dst_ref=landed,
                    send_sem=d2d_send.at[sem], recv_sem=d2d_recv.at[sem], device_id=(x, y, 1 - c), device_id_type=MESH))
                self.d2d_recvs.append(pltpu.make_async_remote_copy(
                    src_ref=landed, dst_ref=outs[i].at[peer_chip].at[theirs],
                    send_sem=d2d_send.at[sem], recv_sem=d2d_recv.at[sem], device_id=(x, y, 1 - c), device_id_type=MESH))

    def start(self):
        for cp in self.local + self.ici_sends:
            cp.start()

    def forward(self):
        for landed, onward in zip(self.ici_recvs, self.d2d_sends):
            landed.wait_recv()
            onward.start()

    def finish(self):
        for cp in self.d2d_recvs:
            cp.wait_recv()
        for cp in self.d2d_sends + self.ici_sends:
            cp.wait_send()
        for cp in self.local:
            cp.wait()


def _swap_sems(n):
    return [pltpu.SemaphoreType.DMA((n,)), pltpu.SemaphoreType.DMA((n,))]


class _SiblingSwap:
    def __init__(self, ins, outs, sems):
        send_sems, recv_sems = sems
        x, y, c = _mesh_pos()
        self.copies = [pltpu.make_async_remote_copy(
            src_ref=ins[i], dst_ref=outs[i], send_sem=send_sems.at[i], recv_sem=recv_sems.at[i],
            device_id=(x, y, 1 - c), device_id_type=MESH) for i in range(len(ins))]

    def start(self):
        for cp in self.copies:
            cp.start()

    def wait(self):
        for cp in self.copies:
            cp.wait_recv()
        for cp in self.copies:
            cp.wait_send()


def _adam(w, g, m, v):
    m2 = ADAM_B1 * m + (1.0 - ADAM_B1) * g
    v2 = ADAM_B2 * v + (1.0 - ADAM_B2) * (g * g)
    m_hat = m2 / (1.0 - ADAM_B1 ** ADAM_STEP)
    v_hat = v2 / (1.0 - ADAM_B2 ** ADAM_STEP)
    delta = -ADAM_LR * (m_hat / (jnp.sqrt(v_hat) + ADAM_EPS) + ADAM_WD * w)
    return delta, m2, v2


def _ada_bwd_adam(c_t, dmod_blk, w, m, v):
    rows, cols = w.shape
    tile = 512
    assert cols % tile == 0

    def body(c_ref, d_ref, w_ref, m_ref, v_ref, g_ref, dl_ref, m2_ref, v2_ref):
        sc = c_ref[...]
        sc = sc * _sigmoid(sc)
        dm = d_ref[...]
        g = sc[:, 0:1] * dm[0:1, :]
        for b in range(1, N_DEV):
            g = g + sc[:, b:b + 1] * dm[b:b + 1, :]
        delta, m2, v2 = _adam(w_ref[...], g, m_ref[...], v_ref[...])
        g_ref[...] = g
        dl_ref[...] = delta
        m2_ref[...] = m2
        v2_ref[...] = v2

    blk = pl.BlockSpec((rows, tile), lambda j: (0, j))
    out = jax.ShapeDtypeStruct((rows, cols), jnp.float32)
    return pl.pallas_call(
        body, name="ada_bwd_adam", grid=(cols // tile,),
        out_shape=(out, out, out, out),
        in_specs=[pl.BlockSpec((rows, N_DEV), lambda j: (0, 0)), pl.BlockSpec((N_DEV, tile), lambda j: (0, j)),
                  blk, blk, blk],
        out_specs=(blk, blk, blk, blk),
        compiler_params=_params(("arbitrary",)),
    )(c_t, dmod_blk, w, m, v)


MOD_SHIFT1, MOD_SCALE1, MOD_GATE1, MOD_SHIFT2, MOD_SCALE2, MOD_GATE2 = range(6)


def _mod(mod_ref, segment):
    return mod_ref[:, segment * D_MODEL:(segment + 1) * D_MODEL]


def _inproj_fwd(x2, vecs, w_in_p, tm, riders):
    seq = x2.shape[0]
    n_tiles = seq // tm
    n_ride = len(riders)

    def body(*refs):
        x_ref, vec_ref, w_ref = refs[:3]
        ride_in, refs = refs[3:3 + n_ride], refs[3 + n_ride:]
        p_ref, u_ref = refs[:2]
        ride_out, sems = refs[2:2 + n_ride], refs[2 + n_ride:]
        gather = _ChipGather(ride_in, ride_out, sems)

        @pl.when(pl.program_id(0) == 0)
        def _():
            gather.start()

        xh, _ = _ln(x_ref[...])
        u = (xh * (1.0 + _mod(vec_ref, MOD_SCALE1)) + _mod(vec_ref, MOD_SHIFT1)).astype(MXU_DTYPE)
        u_ref[...] = u
        p_ref[...] = _mm(u, w_ref[...])

        @pl.when(pl.program_id(0) == (3 * n_tiles) // 4)
        def _():
            gather.forward()

        @pl.when(pl.program_id(0) == n_tiles - 1)
        def _():
            gather.finish()

    hbm = pl.BlockSpec(memory_space=pl.ANY)
    return pl.pallas_call(
        body, name="inproj_fwd", grid=(n_tiles,),
        out_shape=(jax.ShapeDtypeStruct((seq, N_PROJ), jnp.float32), jax.ShapeDtypeStruct((seq, D_MODEL), MXU_DTYPE))
        + _exchange_out_shapes(riders, True),
        in_specs=[pl.BlockSpec((tm, D_MODEL), lambda i: (i, 0)), _const_spec(vecs.shape), _const_spec(w_in_p.shape)]
        + [hbm] * n_ride,
        out_specs=(pl.BlockSpec((tm, N_PROJ), lambda i: (i, 0)), pl.BlockSpec((tm, D_MODEL), lambda i: (i, 0)))
        + (hbm,) * n_ride,
        scratch_shapes=_gather_sems(n_ride),
        compiler_params=_params(("arbitrary",)),
    )(x2, vecs, w_in_p, *riders)


def _inproj_bwd(dproj, x2, dxa, vecs, w_in_pt, tm, riders):
    seq = x2.shape[0]
    n_tiles = seq // tm
    n_ride = len(riders)

    def body(*refs):
        dp_ref, x_ref, dxa_ref, vec_ref, w_ref = refs[:5]
        ride_in, refs = refs[5:5 + n_ride], refs[5 + n_ride:]
        gx_ref, sums_ref = refs[:2]
        ride_out, sems = refs[2:2 + n_ride], refs[2 + n_ride:]
        exchange = _ChipScatter(ride_in, ride_out, sems)

        @pl.when(pl.program_id(0) == 0)
        def _():
            exchange.start()
            sums_ref[...] = jnp.zeros_like(sums_ref)

        du = _mm(dp_ref[...], w_ref[...])
        xh, rstd = _ln(x_ref[...])
        sums_ref[0:1, :] += _colsum(du)
        sums_ref[1:2, :] += _colsum(du * xh)
        gx_ref[...] = dxa_ref[...] + _ln_bwd(du * (1.0 + _mod(vec_ref, MOD_SCALE1)), xh, rstd)

        @pl.when(pl.program_id(0) == n_tiles - 1)
        def _():
            exchange.wait()

    tile = pl.BlockSpec((tm, D_MODEL), lambda i: (i, 0))
    hbm = pl.BlockSpec(memory_space=pl.ANY)
    return pl.pallas_call(
        body, name="inproj_bwd", grid=(n_tiles,),
        out_shape=(jax.ShapeDtypeStruct((seq, D_MODEL), jnp.float32), jax.ShapeDtypeStruct((8, D_MODEL), jnp.float32))
        + _exchange_out_shapes(riders, False),
        in_specs=[pl.BlockSpec((tm, N_PROJ), lambda i: (i, 0)), tile, tile, _const_spec(vecs.shape),
                  _const_spec(w_in_pt.shape)] + [hbm] * n_ride,
        out_specs=(tile, pl.BlockSpec((8, D_MODEL), lambda i: (0, 0))) + (hbm,) * n_ride,
        scratch_shapes=_scatter_sems(n_ride),
        compiler_params=_params(("arbitrary",)),
    )(dproj, x2, dxa, vecs, w_in_pt, *riders)


def _head(h):
    return slice(h * HEAD_W, (h + 1) * HEAD_W)


def _cols(ref, off, h):
    return ref[:, off + h * HEAD_W:off + (h + 1) * HEAD_W]


HEADS = range(N_HEADS)


def _mixer_chunk_forward(p_ref, cc, ss, dm_ref, qdec_ref, kdec_ref, wg_ref, bg_ref, states):
    row, col = _tri_masks()
    lower = row >= col
    f = {}
    f["glr"] = p_ref[:, OFF_LR:OFF_LR + HEAD_W]
    f["logit"] = _mm(f["glr"], wg_ref[...]) + bg_ref[...]
    rq = [_cols(p_ref, OFF_RQ, h) for h in HEADS]
    rk = [_cols(p_ref, OFF_RK, h) for h in HEADS]
    f["rv"] = [_cols(p_ref, OFF_RV, h) for h in HEADS]
    f["qr"] = [(rq[h] * cc + _swap_halves(rq[h]) * ss) * RET_SCALE for h in HEADS]
    f["kr"] = [rk[h] * cc + _swap_halves(rk[h]) * ss for h in HEADS]
    s_raw = [_mm_nt(f["qr"][h], f["kr"][h]) for h in HEADS]
    yield
    la = _log_sigmoid(f["logit"]) * (1.0 / GATE_TAU)
    b = _running_sum(lower, la)
    f["qd"] = [f["qr"][h] * qdec_ref[:, _head(h)] for h in HEADS]
    f["kd"] = [f["kr"][h] * kdec_ref[:, _head(h)] for h in HEADS]
    f["scores"] = [s_raw[h] * dm_ref[h] for h in HEADS]
    yield
    b_last = b[CHUNK - 1:CHUNK, :]
    b_mid = b[CHUNK // 2 - 1:CHUNK // 2, :]
    f["e"], f["ei"] = jnp.exp(b - b_mid), jnp.exp(b_mid - b)
    f["eb"], f["ek"], f["ebl"] = jnp.exp(b), jnp.exp(b_last - b), jnp.exp(b_last)
    gq = [_cols(p_ref, OFF_GQ, h) * GLA_SCALE for h in HEADS]
    gk = [_cols(p_ref, OFF_GK, h) for h in HEADS]
    f["gv"] = [_cols(p_ref, OFF_GV, h) for h in HEADS]
    f["q_e"] = [gq[h] * f["e"][:, _head(h)] for h in HEADS]
    f["q_i"] = [gq[h] * f["ei"][:, _head(h)] for h in HEADS]
    f["k_e"] = [gk[h] * f["e"][:, _head(h)] for h in HEADS]
    f["k_i"] = [gk[h] * f["ei"][:, _head(h)] for h in HEADS]
    low = [_mm_nt(f["q_e"][h], f["k_i"][h]) for h in HEADS]
    up = [_mm_nt(f["q_i"][h], f["k_e"][h]) for h in HEADS]
    yield
    f["att"] = [jnp.where(lower, low[h], up[h]) for h in HEADS]
    f["qb"] = [gq[h] * f["eb"][:, _head(h)] for h in HEADS]
    f["kb"] = [gk[h] * f["ek"][:, _head(h)] for h in HEADS]
    ret_state, gla_state_t = states()
    f["o_ret"] = [_mm(f["scores"][h], f["rv"][h]) + _mm(f["qd"][h], ret_state[h]) for h in HEADS]
    f["o_gla"] = [_mm(f["att"][h], f["gv"][h]) + _mm_nt(f["qb"][h], gla_state_t[h]) for h in HEADS]
    return f


def _interleave(generators):
    live = list(generators)
    while live:
        for g in list(live):
            try:
                next(g)
            except StopIteration:
                live.remove(g)


def _mixer_fwd(proj, tables, wg_p, bg_p, ret_norm_w, gla_norm_w, riders):
    seq = proj.shape[0]
    n_chunks = seq // CHUNK
    per_step = min(n_chunks, CHUNKS_PER_STEP)
    n_steps = n_chunks // per_step
    n_ride = len(riders)
    rot_a, rot_b, dm_t, qdec_t, kdec_t, chunk_decay = tables

    def body(*refs):
        p_ref, ra_ref, rb_ref, dm_ref, qdec_ref, kdec_ref, wg_ref, bg_ref, wr_ref, wl_ref = refs[:10]
        ride_in, refs = refs[10:10 + n_ride], refs[10 + n_ride:]
        mix_ref, rsave_ref, ssave_ref = refs[:3]
        ride_out, refs = refs[3:3 + n_ride], refs[3 + n_ride:]
        r_sc, s_sc = refs[:2]
        gather = _ChipGather(ride_in, ride_out, refs[2:])

        @pl.when(pl.program_id(0) == 0)
        def _():
            gather.start()
            r_sc[...] = jnp.zeros_like(r_sc)
            s_sc[...] = jnp.zeros_like(s_sc)

        def one_chunk(c):
            p_c = p_ref.at[c * CHUNK:(c + 1) * CHUNK, :]
            mix_c = mix_ref.at[c * CHUNK:(c + 1) * CHUNK, :]
            before = {}

            def states():
                before["ret"] = [r_sc[h] for h in HEADS]
                before["gla"] = [s_sc[h] for h in HEADS]
                for h in HEADS:
                    rsave_ref[c, h] = before["ret"][h].astype(rsave_ref.dtype)
                    ssave_ref[c, h] = before["gla"][h]
                return before["ret"], before["gla"]

            cc, ss = _rotary_chunk(ra_ref, c, rb_ref)
            f = yield from _mixer_chunk_forward(p_c, cc, ss, dm_ref, qdec_ref, kdec_ref, wg_ref, bg_ref, states)
            for h in HEADS:
                r_sc[h] = chunk_decay[h] * before["ret"][h] + _mm_tn(f["kd"][h], f["rv"][h])
            for h in HEADS:
                s_sc[h] = before["gla"][h] * f["ebl"][:, _head(h)] + _mm_tn(f["gv"][h], f["kb"][h])
            yield
            for h in HEADS:
                on, _ = _ln(f["o_ret"][h])
                g = _cols(p_c, OFF_RG, h)
                mix_c[:, _head(h)] = (on * wr_ref[:, _head(h)] * (g * _sigmoid(g))).astype(mix_ref.dtype)
            for h in HEADS:
                o = f["o_gla"][h]
                on = o * lax.rsqrt(_rowmean(o * o) + LN_EPS)
                g = _cols(p_c, OFF_GG, h)
                mix_c[:, _head(N_HEADS + h)] = (on * wl_ref[:, _head(h)] * (g * _sigmoid(g))).astype(mix_ref.dtype)

        for c0 in range(0, per_step, CHUNKS_IN_LOCKSTEP):
            _interleave([one_chunk(c) for c in range(c0, min(per_step, c0 + CHUNKS_IN_LOCKSTEP))])

        @pl.when(pl.program_id(0) == (7 * n_steps) // 8)
        def _():
            gather.forward()

        @pl.when(pl.program_id(0) == n_steps - 1)
        def _():
            gather.finish()

    state_shape = (n_chunks, N_HEADS, HEAD_W, HEAD_W)
    state_blk = pl.BlockSpec((per_step, N_HEADS, HEAD_W, HEAD_W), lambda i: (i, 0, 0, 0))
    rot_blk = pl.BlockSpec((per_step, 8, HEAD_W), lambda i: (i, 0, 0))
    rows = per_step * CHUNK
    hbm = pl.BlockSpec(memory_space=pl.ANY)
    return pl.pallas_call(
        body, name="mixer_fwd", grid=(n_steps,),
        out_shape=(jax.ShapeDtypeStruct((seq, D_MODEL), MXU_DTYPE),
                   jax.ShapeDtypeStruct(state_shape, MXU_DTYPE), jax.ShapeDtypeStruct(state_shape, jnp.float32))
        + _exchange_out_shapes(riders, True),
        in_specs=[pl.BlockSpec((rows, N_PROJ), lambda i: (i, 0)), rot_blk, _const_spec(rot_b.shape),
                  _const_spec(dm_t.shape), _const_spec(qdec_t.shape), _const_spec(kdec_t.shape),
                  _const_spec(wg_p.shape), _const_spec(bg_p.shape), _const_spec(ret_norm_w.shape),
                  _const_spec(gla_norm_w.shape)] + [hbm] * n_ride,
        out_specs=(pl.BlockSpec((rows, D_MODEL), lambda i: (i, 0)), state_blk, state_blk) + (hbm,) * n_ride,
        scratch_shapes=[pltpu.VMEM((N_HEADS, HEAD_W, HEAD_W), jnp.float32),
                        pltpu.VMEM((N_HEADS, HEAD_W, HEAD_W), jnp.float32)] + _gather_sems(n_ride),
        compiler_params=_params(("arbitrary",)),
    )(proj, rot_a, rot_b, dm_t, qdec_t, kdec_t, wg_p, bg_p, ret_norm_w, gla_norm_w, *riders)


def _mixer_bwd(proj, dmixed, rsave, ssave, tables, wg_p, bg_p, ret_norm_w, gla_norm_w, riders):
    seq = proj.shape[0]
    n_chunks = seq // CHUNK
    per_step = min(n_chunks, CHUNKS_PER_STEP)
    n_steps = n_chunks // per_step
    n_ride = len(riders)
    rot_a, rot_b, dm_t, qdec_t, kdec_t, chunk_decay = tables
    last = n_steps - 1

    def body(*refs):
        p_blk, dmx_blk = refs[:2]
        shared_in = refs[2:13]
        ride_in, refs = refs[13:13 + n_ride], refs[13 + n_ride:]
        dp_blk, dwr_ref, dwl_ref, dwg_ref, dbg_ref = refs[:5]
        ride_out, refs = refs[5:5 + n_ride], refs[5 + n_ride:]
        dr_sc, ds_sc = refs[:2]
        exchange = _ChipScatter(ride_in, ride_out, refs[2:])

        @pl.when(pl.program_id(0) == 0)
        def _():
            exchange.start()
            dr_sc[...] = jnp.zeros_like(dr_sc)
            ds_sc[...] = jnp.zeros_like(ds_sc)
            dwr_ref[...] = jnp.zeros_like(dwr_ref)
            dwl_ref[...] = jnp.zeros_like(dwl_ref)
            dwg_ref[...] = jnp.zeros_like(dwg_ref)
            dbg_ref[...] = jnp.zeros_like(dbg_ref)

        def chunk_stages(c):
            rows = slice(c * CHUNK, (c + 1) * CHUNK)
            return one_chunk(c, p_blk.at[rows, :], dmx_blk.at[rows, :], dp_blk.at[rows, :], *shared_in,
                             dwr_ref, dwl_ref, dwg_ref, dbg_ref, dr_sc, ds_sc)

        for c0 in range(per_step, 0, -CHUNKS_IN_LOCKSTEP):
            _interleave([chunk_stages(c) for c in reversed(range(max(0, c0 - CHUNKS_IN_LOCKSTEP), c0))])

        @pl.when(pl.program_id(0) == last)
        def _():
            exchange.wait()

    def one_chunk(c, p_ref, dmx_ref, dp_ref, rsave_ref, ssave_ref, ra_ref, rb_ref, dm_ref, qdec_ref, kdec_ref,
                  wg_ref, bg_ref, wr_ref, wl_ref, dwr_ref, dwl_ref, dwg_ref, dbg_ref, dr_sc, ds_sc):
        def put(off, h, val):
            dp_ref[:, off + h * HEAD_W:off + (h + 1) * HEAD_W] = val.astype(dp_ref.dtype)

        cc, ss = _rotary_chunk(ra_ref, c, rb_ref)
        row, col = _tri_masks()
        ret_state = [rsave_ref[c, h] for h in HEADS]
        gla_state_t = [ssave_ref[c, h] for h in HEADS]
        f = yield from _mixer_chunk_forward(p_ref, cc, ss, dm_ref, qdec_ref, kdec_ref, wg_ref, bg_ref,
                                            lambda: (ret_state, gla_state_t))
        yield

        do_ret, do_gla = [], []
        for h in HEADS:
            on, rstd = _ln(f["o_ret"][h])
            g = _cols(p_ref, OFF_RG, h)
            sg = _sigmoid(g)
            dy = dmx_ref[:, _head(h)].astype(jnp.float32)
            wr = wr_ref[:, _head(h)]
            dwr_ref[:, _head(h)] += _colsum(dy * on * (g * sg))
            put(OFF_RG, h, dy * on * wr * (sg * (1.0 + g * (1.0 - sg))))
            do_ret.append(_ln_bwd(dy * wr * (g * sg), on, rstd))
        for h in HEADS:
            o = f["o_gla"][h]
            rstd = lax.rsqrt(_rowmean(o * o) + LN_EPS)
            on = o * rstd
            g = _cols(p_ref, OFF_GG, h)
            sg = _sigmoid(g)
            dy = dmx_ref[:, _head(N_HEADS + h)].astype(jnp.float32)
            wl = wl_ref[:, _head(h)]
            dwl_ref[:, _head(h)] += _colsum(dy * on * (g * sg))
            put(OFF_GG, h, dy * on * wl * (sg * (1.0 + g * (1.0 - sg))))
            don = dy * wl * (g * sg)
            do_gla.append(rstd * (don - on * _rowmean(don * on)))

        yield

        d_ret_new = [dr_sc[h] for h in HEADS]
        d_gla_new = [ds_sc[h] for h in HEADS]
        ds_raw = [_mm_nt(do_ret[h], f["rv"][h]) * dm_ref[h] for h in HEADS]
        d_att = [_mm_nt(do_gla[h], f["gv"][h]) for h in HEADS]
        dq_state = [_mm_nt(do_ret[h], ret_state[h]) for h in HEADS]
        dk_state = [_mm_nt(f["rv"][h], d_ret_new[h]) for h in HEADS]
        dqb = [_mm(do_gla[h], gla_state_t[h]) for h in HEADS]
        dkb = [_mm(f["gv"][h], d_gla_new[h]) for h in HEADS]
        for h in HEADS:
            put(OFF_RV, h, _mm_tn(f["scores"][h], do_ret[h]) + _mm(f["kd"][h], d_ret_new[h]))
        for h in HEADS:
            put(OFF_GV, h, _mm_tn(f["att"][h], do_gla[h]) + _mm_nt(f["kb"][h], d_gla_new[h]))
        for h in HEADS:
            dr_sc[h] = chunk_decay[h] * d_ret_new[h] + _mm_tn(f["qd"][h], do_ret[h])
        for h in HEADS:
            ds_sc[h] = d_gla_new[h] * f["ebl"][:, _head(h)] + _mm_tn(do_gla[h], f["qb"][h])
        yield

        dqr = [_mm(ds_raw[h], f["kr"][h]) + dq_state[h] * qdec_ref[:, _head(h)] for h in HEADS]
        dkr = [_mm_tn(ds_raw[h], f["qr"][h]) + dk_state[h] * kdec_ref[:, _head(h)] for h in HEADS]
        d_low = [jnp.where(row >= col, d_att[h], 0.0) for h in HEADS]
        d_up = [jnp.where(row < col, d_att[h], 0.0) for h in HEADS]
        dq_e = [_mm(d_low[h], f["k_i"][h]) for h in HEADS]
        dk_i = [_mm_tn(d_low[h], f["q_e"][h]) for h in HEADS]
        dq_i = [_mm(d_up[h], f["k_e"][h]) for h in HEADS]
        dk_e = [_mm_tn(d_up[h], f["q_i"][h]) for h in HEADS]
        yield
        for h in HEADS:
            put(OFF_RQ, h, (dqr[h] * cc + _swap_halves(dqr[h] * ss)) * RET_SCALE)
            put(OFF_RK, h, dkr[h] * cc + _swap_halves(dkr[h] * ss))
        row_id = lax.broadcasted_iota(jnp.int32, (CHUNK, HEAD_W), 0)
        db_heads = []
        for h in HEADS:
            hs = _head(h)
            e, ei, eb, ek, ebl = f["e"][:, hs], f["ei"][:, hs], f["eb"][:, hs], f["ek"][:, hs], f["ebl"][:, hs]
            put(OFF_GQ, h, (dq_e[h] * e + dq_i[h] * ei + dqb[h] * eb) * GLA_SCALE)
            put(OFF_GK, h, dk_e[h] * e + dk_i[h] * ei + dkb[h] * ek)
            db = (dq_e[h] * f["q_e"][h] - dq_i[h] * f["q_i"][h] + dk_e[h] * f["k_e"][h] - dk_i[h] * f["k_i"][h]
                  + dqb[h] * f["qb"][h] - dkb[h] * f["kb"][h])
            db_last = _colsum(dkb[h] * f["kb"][h]) + ebl * _colsum(gla_state_t[h] * d_gla_new[h])
            db_heads.append(db + jnp.where(row_id == CHUNK - 1, db_last, 0.0))
        db = jnp.concatenate(db_heads, axis=1)
        d_la = _running_sum(col >= row, db)
        d_logit = d_la * (1.0 / GATE_TAU) * (1.0 - _sigmoid(f["logit"]))
        put(OFF_LR, 0, _mm_nt(d_logit, wg_ref[...]))
        dwg_ref[...] += _mm_tn(f["glr"], d_logit)
        dbg_ref[...] += _colsum(d_logit)

    state_blk = pl.BlockSpec((per_step, N_HEADS, HEAD_W, HEAD_W), lambda i: (last - i, 0, 0, 0))
    rot_blk = pl.BlockSpec((per_step, 8, HEAD_W), lambda i: (last - i, 0, 0))
    width = N_HEADS * HEAD_W
    vec_out = pl.BlockSpec((1, width), lambda i: (0, 0))
    hbm = pl.BlockSpec(memory_space=pl.ANY)
    rows_blk = per_step * CHUNK
    return pl.pallas_call(
        body, name="mixer_bwd", grid=(n_steps,),
        out_shape=(jax.ShapeDtypeStruct((seq, N_PROJ), MXU_DTYPE),
                   jax.ShapeDtypeStruct((1, width), jnp.float32), jax.ShapeDtypeStruct((1, width), jnp.float32),
                   jax.ShapeDtypeStruct((HEAD_W, width), jnp.float32), jax.ShapeDtypeStruct((1, width), jnp.float32))
        + _exchange_out_shapes(riders, False),
        in_specs=[pl.BlockSpec((rows_blk, N_PROJ), lambda i: (last - i, 0)),
                  pl.BlockSpec((rows_blk, D_MODEL), lambda i: (last - i, 0)), state_blk, state_blk, rot_blk,
                  _const_spec(rot_b.shape),
                  _const_spec(dm_t.shape), _const_spec(qdec_t.shape), _const_spec(kdec_t.shape),
                  _const_spec(wg_p.shape), _const_spec(bg_p.shape), _const_spec(ret_norm_w.shape),
                  _const_spec(gla_norm_w.shape)] + [hbm] * n_ride,
        out_specs=(pl.BlockSpec((rows_blk, N_PROJ), lambda i: (last - i, 0)), vec_out, vec_out,
                   pl.BlockSpec((HEAD_W, width), lambda i: (0, 0)), vec_out) + (hbm,) * n_ride,
        scratch_shapes=[pltpu.VMEM((N_HEADS, HEAD_W, HEAD_W), jnp.float32),
                        pltpu.VMEM((N_HEADS, HEAD_W, HEAD_W), jnp.float32)] + _scatter_sems(n_ride),
        compiler_params=_params(("arbitrary",)),
    )(proj, dmixed, rsave, ssave, rot_a, rot_b, dm_t, qdec_t, kdec_t, wg_p, bg_p, ret_norm_w, gla_norm_w, *riders)


V_GATE1, V_SCALE2, V_SHIFT2, V_GATE2, V_LN1W, V_LN1B, V_LN2W, V_LN2B = range(8)
S_GATE1, S_SCALE2, S_SHIFT2, S_GATE2, S_LN1W, S_LN1B, S_LN2W, S_LN2B, S_LOSS = range(9)


def _mlp_fwd_bwd(x2, mixed, target, mod, ln_rows, w_out, w1_chunks, w2_chunks, tm):
    seq = x2.shape[0]
    n_fc, _, fc = w1_chunks.shape
    segment_of = {V_GATE1: MOD_GATE1, V_SCALE2: MOD_SCALE2, V_SHIFT2: MOD_SHIFT2, V_GATE2: MOD_GATE2}

    def body(x_ref, mx_ref, t_ref, mod_ref, ln_ref, wo_ref, w1_ref, w2_ref,
             dmx_ref, dxa_ref, a_ref, dh_ref, u2_ref, df_ref, dm_ref, sums_ref, relu_sc):
        @pl.when(pl.program_id(0) == 0)
        def _():
            sums_ref[...] = jnp.zeros_like(sums_ref)

        def vec(r):
            if r in segment_of:
                return _mod(mod_ref, segment_of[r])
            return ln_ref[r - V_LN1W:r - V_LN1W + 1, :]

        def acc(r, val):
            sums_ref[r:r + 1, :] += _colsum(val)

        xx = x_ref[...]
        m = _mm(mx_ref[...], wo_ref[...])
        z1h, rstd1 = _ln(ALPHA * xx + vec(V_GATE1) * m)
        x1 = z1h * vec(V_LN1W) + vec(V_LN1B)
        x1h, rstd0 = _ln(x1)
        u2 = (x1h * (1.0 + vec(V_SCALE2)) + vec(V_SHIFT2)).astype(MXU_DTYPE)
        u2_ref[...] = u2
        f = jnp.zeros((tm, D_MODEL), jnp.float32)
        for j in range(n_fc):
            r = jnp.maximum(_mm(u2, w1_ref[j]), 0.0)
            relu_sc[:, j * fc:(j + 1) * fc] = r
            a = (r * r).astype(MXU_DTYPE)
            a_ref[:, j * fc:(j + 1) * fc] = a
            f = f + _mm(a, w2_ref[j])
        z2h, rstd2 = _ln(ALPHA * x1 + vec(V_GATE2) * f)
        err = z2h * vec(V_LN2W) + vec(V_LN2B) - t_ref[...]
        acc(S_LOSS, err * err)
        dy = err * (1.0 / D_MODEL)
        acc(S_LN2W, dy * z2h)
        acc(S_LN2B, dy)
        dz2 = _ln_bwd(dy * vec(V_LN2W), z2h, rstd2)
        acc(S_GATE2, dz2 * f)
        df = (vec(V_GATE2) * dz2).astype(MXU_DTYPE)
        df_ref[...] = df
        du2 = jnp.zeros((tm, D_MODEL), jnp.float32)
        for j in range(n_fc):
            dh = (_mm_nt(df, w2_ref[j]) * (2.0 * relu_sc[:, j * fc:(j + 1) * fc])).astype(MXU_DTYPE)
            dh_ref[:, j * fc:(j + 1) * fc] = dh
            du2 = du2 + _mm_nt(dh, w1_ref[j])
        acc(S_SCALE2, du2 * x1h)
        acc(S_SHIFT2, du2)
        dx1 = ALPHA * dz2 + _ln_bwd(du2 * (1.0 + vec(V_SCALE2)), x1h, rstd0)
        acc(S_LN1W, dx1 * z1h)
        acc(S_LN1B, dx1)
        dz1 = _ln_bwd(dx1 * vec(V_LN1W), z1h, rstd1)
        acc(S_GATE1, dz1 * m)
        dxa_ref[...] = ALPHA * dz1
        dm = (vec(V_GATE1) * dz1).astype(MXU_DTYPE)
        dm_ref[...] = dm
        dmx_ref[...] = _mm_nt(dm, wo_ref[...])

    tile = lambda width: pl.BlockSpec((tm, width), lambda i: (i, 0))
    f32 = lambda width: jax.ShapeDtypeStruct((seq, width), jnp.float32)
    b16 = lambda width: jax.ShapeDtypeStruct((seq, width), MXU_DTYPE)
    return pl.pallas_call(
        body, name="mlp_fwd_bwd", grid=(seq // tm,),
        out_shape=(f32(D_MODEL), f32(D_MODEL), b16(D_FF), b16(D_FF), b16(D_MODEL), b16(D_MODEL), b16(D_MODEL),
                   jax.ShapeDtypeStruct((16, D_MODEL), jnp.float32)),
        in_specs=[tile(D_MODEL), tile(D_MODEL), tile(D_MODEL), _const_spec(mod.shape), _const_spec(ln_rows.shape),
                  _const_spec(w_out.shape), _const_spec(w1_chunks.shape), _const_spec(w2_chunks.shape)],
        out_specs=(tile(D_MODEL), tile(D_MODEL), tile(D_FF), tile(D_FF), tile(D_MODEL), tile(D_MODEL),
                   tile(D_MODEL), pl.BlockSpec((16, D_MODEL), lambda i: (0, 0))),
        scratch_shapes=[pltpu.VMEM((tm, D_FF), jnp.float32)],
        compiler_params=_params(("arbitrary",)),
    )(x2, mixed, target, mod, ln_rows, w_out, w1_chunks, w2_chunks)


def _grad_matmul(a, b, name, tn, blocks_are_rows, riders=()):
    seq, m_dim = a.shape
    n_dim = b.shape[1]
    tk = min(seq, GRAD_TOKEN_TILE)
    nk = seq // tk
    n_ride = len(riders)
    if blocks_are_rows:
        tm = m_dim // N_CHIP
        assert tn == n_dim
        per_step = N_CHIP if m_dim <= GRAD_ROWS_PER_STEP else 1
        grid = (N_CHIP // per_step, 1, nk)
        out_map = lambda i, j, k: (i, 0, 0)
    else:
        tm = m_dim
        assert tn * N_CHIP == n_dim
        per_step = 1
        grid = (1, N_CHIP, nk)
        out_map = lambda i, j, k: (j, 0, 0)
    n_blocks = grid[0] * grid[1]
    rows = per_step * tm

    def body(*refs):
        a_ref, b_ref = refs[:2]
        ride_in, refs = refs[2:2 + n_ride], refs[2 + n_ride:]
        o_ref = refs[0]
        ride_out, refs = refs[1:1 + n_ride], refs[1 + n_ride:]
        acc_sc = refs[0]
        exchange = _ChipScatter(ride_in, ride_out, refs[1:]) if n_ride else None
        block = pl.program_id(0) + pl.program_id(1)
        k = pl.program_id(2)

        if exchange is not None:
            @pl.when((block == 0) & (k == 0))
            def _():
                exchange.start()

        @pl.when(k == 0)
        def _():
            acc_sc[...] = jnp.zeros_like(acc_sc)

        acc_sc[...] += _mm_tn(a_ref[...], b_ref[...])

        @pl.when(k == nk - 1)
        def _():
            for p in range(per_step):
                o_ref[p] = acc_sc[p * tm:(p + 1) * tm, :].astype(o_ref.dtype)

        if exchange is not None:
            @pl.when((block == n_blocks - 1) & (k == nk - 1))
            def _():
                exchange.wait()

    hbm = pl.BlockSpec(memory_space=pl.ANY)
    out = pl.pallas_call(
        body, name=name, grid=grid,
        out_shape=(jax.ShapeDtypeStruct((N_CHIP, tm, tn), WIRE_DTYPE),) + _exchange_out_shapes(riders, False),
        in_specs=[pl.BlockSpec((tk, rows), lambda i, j, k: (k, i)), pl.BlockSpec((tk, tn), lambda i, j, k: (k, j))]
        + [hbm] * n_ride,
        out_specs=(pl.BlockSpec((per_step, tm, tn), out_map),) + (hbm,) * n_ride,
        scratch_shapes=[pltpu.VMEM((rows, tn), jnp.float32)] + (_scatter_sems(n_ride) if n_ride else []),
        compiler_params=_params(("arbitrary", "arbitrary", "arbitrary")),
    )(a, b, *riders)
    return out if n_ride else out[0]


def _grad_matmul_full(a, b, name, tm, riders):
    seq, m_dim = a.shape
    n_dim = b.shape[1]
    tk = min(seq, GRAD_TOKEN_TILE)
    nk = seq // tk
    n_blocks = m_dim // tm
    n_ride = len(riders)
    assert m_dim % tm == 0

    def body(*refs):
        a_ref, b_ref = refs[:2]
        ride_in, refs = refs[2:2 + n_ride], refs[2 + n_ride:]
        o_ref = refs[0]
        ride_out, refs = refs[1:1 + n_ride], refs[1 + n_ride:]
        acc_sc = refs[0]
        swap = _SiblingSwap(ride_in, ride_out, refs[1:])
        i, k = pl.program_id(0), pl.program_id(1)

        @pl.when((i == 0) & (k == 0))
        def _():
            swap.start()

        @pl.when(k == 0)
        def _():
            acc_sc[...] = jnp.zeros_like(acc_sc)

        acc_sc[...] += _mm_tn(a_ref[...], b_ref[...])

        @pl.when(k == nk - 1)
        def _():
            o_ref[...] = acc_sc[...].astype(o_ref.dtype)

        @pl.when((i == n_blocks - 1) & (k == nk - 1))
        def _():
            swap.wait()

    hbm = pl.BlockSpec(memory_space=pl.ANY)
    return pl.pallas_call(
        body, name=name, grid=(n_blocks, nk),
        out_shape=(jax.ShapeDtypeStruct((m_dim, n_dim), WIRE_DTYPE),)
        + tuple(jax.ShapeDtypeStruct(r.shape, r.dtype) for r in riders),
        in_specs=[pl.BlockSpec((tk, tm), lambda i, k: (k, i)), pl.BlockSpec((tk, n_dim), lambda i, k: (k, 0))]
        + [hbm] * n_ride,
        out_specs=(pl.BlockSpec((tm, n_dim), lambda i, k: (i, 0)),) + (hbm,) * n_ride,
        scratch_shapes=[pltpu.VMEM((tm, n_dim), jnp.float32)] + _swap_sems(n_ride),
        compiler_params=_params(("arbitrary", "arbitrary")),
    )(a, b, *riders)


def _adam_pair(w, g_mine, g_sibling, m, v, name):
    rows, cols = w.shape
    tc = min(cols, ELEMENTWISE_COLS)

    def total(ref):
        if len(ref.shape) == 2:
            return ref[...]
        acc = ref[0].astype(jnp.float32)
        for j in range(1, ref.shape[0]):
            acc = acc + ref[j].astype(jnp.float32)
        return acc

    def body(w_ref, ga_ref, gb_ref, m_ref, v_ref, g_ref, dl_ref, m2_ref, v2_ref):
        g = total(ga_ref) + total(gb_ref)
        delta, m2, v2 = _adam(w_ref[...], g, m_ref[...], v_ref[...])
        g_ref[...] = g
        dl_ref[...] = delta
        m2_ref[...] = m2
        v2_ref[...] = v2

    blk = pl.BlockSpec((rows, tc), lambda i: (0, i))
    g_blk = lambda a: blk if a.ndim == 2 else pl.BlockSpec((a.shape[0], rows, tc), lambda i: (0, 0, i))
    out = jax.ShapeDtypeStruct((rows, cols), jnp.float32)
    return pl.pallas_call(
        body, name=name, grid=(cols // tc,),
        out_shape=(out, out, out, out),
        in_specs=[blk, g_blk(g_mine), g_blk(g_sibling), blk, blk], out_specs=(blk,) * 4,
        compiler_params=_params(("arbitrary",)),
    )(w, g_mine, g_sibling, m, v)


def _sum_devices(gathered, layout):
    def body(g_ref, *o_refs):
        total = g_ref[0]
        for d in range(1, N_DEV):
            total = total + g_ref[d]
        for (first, (rows_out, cols_out)), o_ref in zip(layout, o_refs):
            per_row = cols_out // 128
            for r in range(rows_out):
                for k in range(per_row):
                    src = first + r * per_row + k
                    o_ref[r:r + 1, k * 128:(k + 1) * 128] = total[src:src + 1, :]
        tail = total[total.shape[0] - 8:, :]
        o_refs[-1][...] = jnp.full((1, 128), jnp.sum(tail), jnp.float32)

    out_shape = tuple(jax.ShapeDtypeStruct(shape, jnp.float32) for _, shape in layout)
    return pl.pallas_call(
        body, name="sum_devices",
        out_shape=out_shape + (jax.ShapeDtypeStruct((1, 128), jnp.float32),),
    )(gathered)


def _adam_small(params):
    n = len(params)

    def body(*refs):
        ins, outs = refs[:4 * n], refs[4 * n:]
        for i in range(n):
            w_ref, g_ref, m_ref, v_ref = ins[4 * i:4 * i + 4]
            delta, m2, v2 = _adam(w_ref[...], g_ref[...], m_ref[...], v_ref[...])
            outs[3 * i][...] = delta
            outs[3 * i + 1][...] = m2
            outs[3 * i + 2][...] = v2

    out_shape = tuple(jax.ShapeDtypeStruct(p[0].shape, jnp.float32) for p in params for _ in range(3))
    out = pl.pallas_call(body, name="adam_small", out_shape=out_shape)(*[t for p in params for t in p])
    return [out[3 * i:3 * i + 3] for i in range(n)]


def _pad_heads(w):
    lead = w.shape[:-1]
    w = w.reshape(lead + (N_HEADS, GLA_DK))
    w = jnp.pad(w, [(0, 0)] * len(lead) + [(0, 0), (0, HEAD_W - GLA_DK)])
    return w.reshape(lead + (N_HEADS * HEAD_W,))


def _unpad_heads(w):
    lead = w.shape[:-1]
    return w.reshape(lead + (N_HEADS, HEAD_W))[..., :GLA_DK].reshape(lead + (N_HEADS * GLA_DK,))


def _pad_head_rows(w):
    w = w.reshape(N_HEADS, GLA_DK, w.shape[-1])
    return jnp.pad(w, ((0, 0), (0, HEAD_W - GLA_DK), (0, 0))).reshape(N_HEADS * HEAD_W, w.shape[-1])


def _unpad_head_rows(w):
    return w.reshape(N_HEADS, HEAD_W, w.shape[-1])[:, :GLA_DK].reshape(N_HEADS * GLA_DK, w.shape[-1])


def _pad_w_in_rows(stack):
    w = stack.reshape(-1, stack.shape[-1])
    return jnp.concatenate([
        w[:2048], _pad_head_rows(w[2048:2304]), _pad_head_rows(w[2304:2560]), w[2560:3584],
        jnp.pad(w[3584:3600], ((0, HEAD_W - GATE_RANK), (0, 0)))], axis=0)


def _unpad_w_in_stack(g, per):
    segments = [(0, g[:2048]), (2048, _unpad_head_rows(g[OFF_GQ:OFF_GQ + 512])),
                (2304, _unpad_head_rows(g[OFF_GK:OFF_GK + 512])), (2560, g[OFF_GV:OFF_LR]),
                (3584, g[OFF_LR:OFF_LR + GATE_RANK])]
    blocks = []
    for j in range(N_CHIP):
        lo, hi = j * per, (j + 1) * per
        pieces = []
        for start, rows in segments:
            a, b = max(lo, start), min(hi, start + rows.shape[0])
            if a < b:
                pieces.append(rows[a - start:b - start])
        blocks.append(jnp.concatenate(pieces, axis=0))
    return jnp.stack(blocks)


def _col_major(w):
    return jnp.transpose(w, (2, 0, 1)).reshape(w.shape[2], w.shape[1])


def _rows128(a):
    return a.reshape(-1, 128)


def kernel(x, c, w_ada, b_ada, w_in, ret_norm_w, gla_gate_w, gla_gate_b, gla_norm_w, w_out, ln1_w, ln1_b, w_ff1, w_ff2, ln2_w, ln2_b, loss_target, m_w_ada, m_b_ada, m_w_in, m_ret_norm_w, m_gla_gate_w, m_gla_gate_b, m_gla_norm_w, m_w_out, m_ln1_w, m_ln1_b, m_w_ff1, m_w_ff2, m_ln2_w, m_ln2_b, v_w_ada, v_b_ada, v_w_in, v_ret_norm_w, v_gla_gate_w, v_gla_gate_b, v_gla_norm_w, v_w_out, v_ln1_w, v_ln1_b, v_w_ff1, v_w_ff2, v_ln2_w, v_ln2_b):
    seq = x.shape[1]
    tm = min(seq, TOKEN_TILE)
    tm_in = min(seq, INPROJ_TOKEN_TILE)
    xi, yi, _ = _mesh_pos()
    chip = 2 * xi + yi
    x2, target = x[0], loss_target[0]
    ada_cols = w_ada.shape[2]
    in_cols = w_in.shape[2]
    gate_cols = gla_gate_w.shape[2]

    b_blk = lax.dynamic_slice(b_ada, (0, chip * ada_cols), (1, ada_cols))
    g0, _, mod, w_in_stack = _prologue(jnp.concatenate([_rows128(c), _rows128(gla_gate_w[0])], axis=0), w_ada[0],
                                       b_blk, _col_major(w_in.astype(WIRE_DTYPE)))
    c_all = g0[:, :8].reshape(N_DEV, D_MODEL)
    gate_w_full = jnp.concatenate([g0[2 * j, 8:16].reshape(GATE_RANK, gate_cols) for j in range(N_CHIP)], axis=1)
    wg_p = jnp.pad(_pad_heads(gate_w_full), ((0, HEAD_W - GATE_RANK), (0, 0)))
    bg_p = _pad_heads(gla_gate_b)
    w_in_pt = _pad_w_in_rows(w_in_stack).astype(MXU_DTYPE)
    w_in_p = jnp.transpose(w_in_pt)

    proj, u, w2_stack = _inproj_fwd(x2, mod, w_in_p, tm_in, [w_ff2[0].astype(WIRE_DTYPE)])
    rot_a, rot_b = _rotary_tables(seq)
    dm_t, qdec_t, kdec_t, chunk_decay = _decay_tables()
    tables = (rot_a, rot_b, dm_t, qdec_t, kdec_t, chunk_decay)
    mixed, rsave, ssave, w_out_stack, w1_stack = _mixer_fwd(
        proj, tables, wg_p, bg_p, ret_norm_w, gla_norm_w,
        [w_out[0].astype(WIRE_DTYPE), w_ff1[0].astype(WIRE_DTYPE)])
    w_out_full = w_out_stack.reshape(D_MODEL, D_MODEL).astype(MXU_DTYPE)
    w1_chunks = w1_stack.astype(MXU_DTYPE)
    w2_chunks = w2_stack.astype(MXU_DTYPE)

    ln_rows = jnp.concatenate([ln1_w, ln1_b, ln2_w, ln2_b], axis=0)
    dmixed, dxa, act, dh, u2, df, dm, sums2 = _mlp_fwd_bwd(x2, mixed, target, mod, ln_rows, w_out_full, w1_chunks,
                                                           w2_chunks, tm)

    g_out_stack = _grad_matmul(mixed, dm, "grad_w_out", D_MODEL, True)
    g_ff1_stack, r_out = _grad_matmul(u2, dh, "grad_w_ff1", D_FF // N_CHIP, False, [g_out_stack])
    g_ff2_stack = _grad_matmul(act, df, "grad_w_ff2", D_MODEL, True)
    dproj, d_ret_norm, d_gla_norm, d_wg_p, d_bg_p, r_ff1, r_ff2 = _mixer_bwd(
        proj, dmixed, rsave, ssave, tables, wg_p, bg_p, ret_norm_w, gla_norm_w, [g_ff1_stack, g_ff2_stack])
    early = ["w_out", "w_ff1", "w_ff2"]
    partial = dict(zip(early, [r_out, r_ff1, r_ff2]))
    g_in_t, *swapped_early = _grad_matmul_full(dproj, u, "grad_w_in", N_PROJ // 3, [partial[n] for n in early])
    swapped = dict(zip(early, swapped_early))
    g_in_stack = _unpad_w_in_stack(g_in_t, in_cols)
    grad_x, sums1, r_in = _inproj_bwd(dproj, x2, dxa, mod, w_in_pt, tm_in, [g_in_stack])

    sources = [sums1, sums2, d_ret_norm, _unpad_heads(d_bg_p), d_gla_norm, _unpad_heads(d_wg_p[:GATE_RANK])]
    pieces = [(0, 0, 0), (0, 1, 8), (1, S_GATE1, 16), (1, S_SHIFT2, 24), (1, S_SCALE2, 32), (1, S_GATE2, 40),
              (1, S_LN1W, 48), (1, S_LN1B, 56), (1, S_LN2W, 64), (1, S_LN2B, 72), (2, 0, 80), (3, 0, 88), (4, 0, 96)]
    pieces += [(5, r, 104 + 2 * r) for r in range(GATE_RANK)] + [(1, S_LOSS, 136)]
    partial["w_in"] = r_in
    g2, swapped["w_in"] = _gather_rows(sources, pieces, 144, "gather_small", [r_in])
    (grad_b_ada, grad_ln1_w, grad_ln1_b, grad_ln2_w, grad_ln2_b, grad_ret_norm, grad_gate_b, grad_gla_norm,
     grad_gate_w_full, loss_sum) = _sum_devices(g2, [
         (0, (1, 6 * D_MODEL)), (48, (1, D_MODEL)), (56, (1, D_MODEL)), (64, (1, D_MODEL)), (72, (1, D_MODEL)),
         (80, (1, 512)), (88, (1, 256)), (96, (1, 512)), (104, (GATE_RANK, 256))])
    loss = 0.5 / D_MODEL * loss_sum[0, 0]
    grad_gate_w = lax.dynamic_slice(grad_gate_w_full, (0, chip * gate_cols), (GATE_RANK, gate_cols))

    small_grads = [grad_b_ada, grad_ln1_w, grad_ln1_b, grad_ln2_w, grad_ln2_b, grad_ret_norm, grad_gate_b,
                   grad_gla_norm, grad_gate_w[None]]
    small_out = _adam_small(list(zip(
        [b_ada, ln1_w, ln1_b, ln2_w, ln2_b, ret_norm_w, gla_gate_b, gla_norm_w, gla_gate_w], small_grads,
        [m_b_ada, m_ln1_w, m_ln1_b, m_ln2_w, m_ln2_b, m_ret_norm_w, m_gla_gate_b, m_gla_norm_w, m_gla_gate_w],
        [v_b_ada, v_ln1_w, v_ln1_b, v_ln2_w, v_ln2_b, v_ret_norm_w, v_gla_gate_b, v_gla_norm_w, v_gla_gate_w])))
    sm_delta, sm_m, sm_v = [[o[k] for o in small_out] for k in range(3)]

    dmod_all = g2[:, 0:48].reshape(N_DEV, 6 * D_MODEL)
    dmod_blk = lax.dynamic_slice(dmod_all, (0, chip * ada_cols), (N_DEV, ada_cols))
    ada_out = _ada_bwd_adam(jnp.transpose(c_all), dmod_blk, w_ada[0], m_w_ada[0], v_w_ada[0])
    ada_g, ada_delta, ada_m, ada_v = [t[None] for t in ada_out]

    big = {}
    for n, w, m, v in zip(["w_in", "w_out", "w_ff1", "w_ff2"], [w_in, w_out, w_ff1, w_ff2],
                          [m_w_in, m_w_out, m_w_ff1, m_w_ff2], [v_w_in, v_w_out, v_w_ff1, v_w_ff2]):
        mine, theirs = partial[n], swapped[n]
        if n == "w_in":
            out = _adam_pair(_col_major(w), mine, theirs, _col_major(m), _col_major(v), "adam_" + n)
            big[n] = [jnp.transpose(t.reshape(t.shape[0], 1, t.shape[1]), (1, 2, 0)) for t in out]
        else:
            big[n] = [t[None] for t in _adam_pair(w[0], mine, theirs, m[0], v[0], "adam_" + n)]

    def assemble(ada, smalls, k):
        b_ada_o, ln1w_o, ln1b_o, ln2w_o, ln2b_o, ret_o, gb_o, gln_o, gw_o = smalls
        return [ada, b_ada_o, big["w_in"][k], ret_o, gw_o, gb_o, gln_o, big["w_out"][k], ln1w_o, ln1b_o,
                big["w_ff1"][k], big["w_ff2"][k], ln2w_o, ln2b_o]

    grads = assemble(ada_g, small_grads, 0)
    deltas = assemble(ada_delta, sm_delta, 1)
    new_m = assemble(ada_m, sm_m, 2)
    new_v = assemble(ada_v, sm_v, 3)
    return (loss, grad_x[None], *grads, *deltas, *new_m, *new_v)
```

```python
import numpy as np
import jax
import jax.numpy as jnp
from jax import lax
from jax.experimental import pallas as pl
from jax.experimental.pallas import tpu as pltpu

D_MODEL = 1024
D_FF = 4096
CHUNK = 64
N_HEADS = 4
HEAD_W = 128
GLA_DK = 64
GATE_RANK = 16
GATE_TAU = 16.0
LN_EPS = 1e-5
ALPHA = 2.0 ** 0.25
ROPE_BASE = 10000.0
RET_SCALE = float(HEAD_W) ** -0.5
GLA_SCALE = float(GLA_DK) ** -0.5

ADAM_LR = 0.001
ADAM_B1 = 0.9
ADAM_B2 = 0.999
ADAM_EPS = 1e-08
ADAM_WD = 0.01
ADAM_STEP = 10

OFF_RQ, OFF_RK, OFF_RV, OFF_RG = 0, 512, 1024, 1536
OFF_GQ, OFF_GK, OFF_GV, OFF_GG, OFF_LR = 2048, 2560, 3072, 3584, 4096
N_PROJ = 4224

N_DEV = 8
N_CHIP = 4
MESH = pl.DeviceIdType.MESH
MXU_DTYPE = jnp.bfloat16
WIRE_DTYPE = jnp.bfloat16
VMEM_LIMIT = 60 * 1024 * 1024
TOKEN_TILE = 256
INPROJ_TOKEN_TILE = 512
CHUNKS_PER_STEP = 8
CHUNKS_IN_LOCKSTEP = 4
GRAD_ROWS_PER_STEP = 1024
GRAD_TOKEN_TILE = 2048
ELEMENTWISE_COLS = 512
HIGHEST = lax.Precision.HIGHEST


def _mm(a, b):
    return jnp.dot(a.astype(MXU_DTYPE), b.astype(MXU_DTYPE), preferred_element_type=jnp.float32)


def _mm_nt(a, b):
    return lax.dot_general(a.astype(MXU_DTYPE), b.astype(MXU_DTYPE), (((1,), (1,)), ((), ())),
                           preferred_element_type=jnp.float32)


def _mm_tn(a, b):
    return lax.dot_general(a.astype(MXU_DTYPE), b.astype(MXU_DTYPE), (((0,), (0,)), ((), ())),
                           preferred_element_type=jnp.float32)


def _mm32(a, b):
    return jnp.dot(a, b, precision=HIGHEST, preferred_element_type=jnp.float32)


def _running_sum(mask, a):
    m = mask.astype(jnp.bfloat16)
    hi = a.astype(jnp.bfloat16)
    rest = a - hi.astype(jnp.float32)
    mid = rest.astype(jnp.bfloat16)
    lo = (rest - mid.astype(jnp.float32)).astype(jnp.bfloat16)
    dot = lambda t: jnp.dot(m, t, preferred_element_type=jnp.float32)
    return dot(hi) + dot(mid) + dot(lo)


def _rowmean(a):
    return jnp.mean(a, axis=-1, keepdims=True)


def _colsum(a):
    return jnp.sum(a, axis=0, keepdims=True)


def _ln(z):
    zc = z - _rowmean(z)
    rstd = lax.rsqrt(_rowmean(zc * zc) + LN_EPS)
    return zc * rstd, rstd


def _ln_bwd(dzh, zh, rstd):
    return rstd * (dzh - _rowmean(dzh) - zh * _rowmean(dzh * zh))


def _sigmoid(a):
    return 1.0 / (1.0 + jnp.exp(-a))


def _log_sigmoid(a):
    return jnp.minimum(a, 0.0) - jnp.log(1.0 + jnp.exp(-jnp.abs(a)))


def _swap_halves(a):
    return pltpu.roll(a, HEAD_W // 2, 1)


def _tri_masks():
    row = lax.broadcasted_iota(jnp.int32, (CHUNK, CHUNK), 0)
    col = lax.broadcasted_iota(jnp.int32, (CHUNK, CHUNK), 1)
    return row, col


def _const_spec(shape):
    zeros = (0,) * len(shape)
    return pl.BlockSpec(shape, lambda *_: zeros, pipeline_mode=pl.Buffered(1))


def _params(semantics):
    return pltpu.CompilerParams(dimension_semantics=semantics, vmem_limit_bytes=VMEM_LIMIT)


def _decay_tables():
    log_gamma = np.log(1.0 - 2.0 ** (-5.0 - np.arange(N_HEADS, dtype=np.float64)))
    idx = np.arange(CHUNK, dtype=np.float64)
    dist = np.abs(idx[:, None] - idx[None, :])
    intra = np.exp(log_gamma[:, None, None] * dist)
    kdec = np.exp(log_gamma[None, :] * (CHUNK - 1.0 - idx)[:, None])
    qdec = np.exp(log_gamma[None, :] * (idx + 1.0)[:, None])
    chunk_decay = np.exp(log_gamma * CHUNK)
    lanes = lambda t: np.repeat(t, HEAD_W, axis=1).astype(np.float32)
    return (jnp.asarray(intra.astype(np.float32)), jnp.asarray(lanes(qdec)), jnp.asarray(lanes(kdec)),
            [float(np.float32(v)) for v in chunk_decay])


def _rotary_tables(seq):
    half = HEAD_W // 2
    inv = 1.0 / (ROPE_BASE ** jnp.linspace(0.0, 1.0, half, dtype=jnp.float32))
    both = lambda t: jnp.concatenate([t, t], axis=-1)
    ang_a = jnp.arange(0, seq, CHUNK, dtype=jnp.float32)[:, None] * inv[None, :]
    rot_a = jnp.stack([both(jnp.cos(ang_a)), both(jnp.sin(ang_a))], axis=1)
    rot_a = jnp.pad(rot_a, ((0, 0), (0, 6), (0, 0)))
    ang_b = jnp.arange(CHUNK, dtype=jnp.float32)[:, None] * inv[None, :]
    cos_b, sin_b = both(jnp.cos(ang_b)), both(jnp.sin(ang_b))
    sign = jnp.concatenate([-jnp.ones((half,), jnp.float32), jnp.ones((half,), jnp.float32)])
    return rot_a, jnp.stack([cos_b, sin_b, cos_b * sign, sin_b * sign])


def _rotary_chunk(ra_ref, c, rb_ref):
    cos_a, sin_a = ra_ref[c, 0:1, :], ra_ref[c, 1:2, :]
    return cos_a * rb_ref[0] - sin_a * rb_ref[1], sin_a * rb_ref[2] + cos_a * rb_ref[3]


def _mesh_pos():
    return lax.axis_index("x"), lax.axis_index("y"), lax.axis_index("c")


def _flip(v, bit):
    return 1 - v if bit else v


def _gather_rows(sources, pieces, rows, name, swaps):
    n_src, n = len(sources), len(swaps)

    def body(*refs):
        src_refs, refs = refs[:n_src], refs[n_src:]
        out_ref = refs[n]
        v_sc = refs[1 + 2 * n]
        swap = _SiblingSwap(refs[:n], refs[1 + n:1 + 2 * n], refs[4 + 2 * n:])
        swap.start()
        v_sc[...] = jnp.zeros_like(v_sc)
        for s, row, first in pieces:
            for k in range(src_refs[s].shape[1] // 128):
                v_sc[first + k:first + k + 1, :] = src_refs[s][row:row + 1, k * 128:(k + 1) * 128]
        _all_devices_exchange(v_sc, out_ref, refs[2 + 2 * n], refs[3 + 2 * n])
        swap.wait()

    hbm = pl.BlockSpec(memory_space=pl.ANY)
    vmem = pl.BlockSpec(memory_space=pltpu.VMEM)
    return pl.pallas_call(
        body, name=name,
        out_shape=(jax.ShapeDtypeStruct((N_DEV, rows, 128), jnp.float32),)
        + tuple(jax.ShapeDtypeStruct(a.shape, a.dtype) for a in swaps),
        in_specs=[vmem] * n_src + [hbm] * n,
        out_specs=(vmem,) + (hbm,) * n,
        scratch_shapes=[pltpu.VMEM((rows, 128), jnp.float32)] + _all_devices_sems() + _swap_sems(n),
    )(*sources, *swaps)


def _all_devices_sems():
    return [pltpu.SemaphoreType.DMA((N_DEV - 1,)), pltpu.SemaphoreType.DMA((N_DEV - 1,))]


def _all_devices_exchange(v_ref, out_ref, send_sems, recv_sems):
    x, y, c = _mesh_pos()
    me = 4 * x + 2 * y + c
    out_ref[me] = v_ref[...]
    sends, recvs = [], []
    for k in range(1, N_DEV):
        px, py, pc = _flip(x, (k >> 2) & 1), _flip(y, (k >> 1) & 1), _flip(c, k & 1)
        peer = 4 * px + 2 * py + pc
        sends.append(pltpu.make_async_remote_copy(
            src_ref=v_ref, dst_ref=out_ref.at[me], send_sem=send_sems.at[k - 1], recv_sem=recv_sems.at[k - 1],
            device_id=(px, py, pc), device_id_type=MESH))
        recvs.append(pltpu.make_async_remote_copy(
            src_ref=v_ref, dst_ref=out_ref.at[peer], send_sem=send_sems.at[k - 1], recv_sem=recv_sems.at[k - 1],
            device_id=(px, py, pc), device_id_type=MESH))
    for cp in sends:
        cp.start()
    for cp in recvs:
        cp.wait_recv()
    for cp in sends:
        cp.wait_send()


def _prologue(cond_rows, w_ada_blk, b_blk, w_in_t):
    cols = w_ada_blk.shape[1]
    groups = cols // 128
    c_rows = D_MODEL // 128

    def body(cond_ref, w_ref, b_ref, win_ref, cond_all_ref, mod_all_ref, mod_ref, stack_ref, mod_sc, *sems):
        gather = _ChipGather([win_ref], [stack_ref], sems[:5])
        gather.start()
        _all_devices_exchange(cond_ref, cond_all_ref, sems[5], sems[6])
        acc = jnp.broadcast_to(b_ref[...], (N_DEV, cols))
        for r in range(c_rows):
            cv = cond_all_ref[:, r, :]
            acc = acc + _mm32(cv * _sigmoid(cv), w_ref[r * 128:(r + 1) * 128, :])
        for k in range(groups):
            mod_sc[k] = acc[:, k * 128:(k + 1) * 128]
        _all_devices_exchange(mod_sc, mod_all_ref, sems[7], sems[8])
        x, y, c = _mesh_pos()
        me = 4 * x + 2 * y + c
        for j in range(N_CHIP):
            for k in range(groups):
                lane = j * cols + k * 128
                mod_ref[:, lane:lane + 128] = mod_all_ref[2 * j, k, pl.ds(me, 1), :]
        gather.forward()
        gather.finish()

    vmem = pl.BlockSpec(memory_space=pltpu.VMEM)
    hbm = pl.BlockSpec(memory_space=pl.ANY)
    return pl.pallas_call(
        body, name="prologue",
        out_shape=(jax.ShapeDtypeStruct((N_DEV,) + cond_rows.shape, jnp.float32),
                   jax.ShapeDtypeStruct((N_DEV, groups, N_DEV, 128), jnp.float32),
                   jax.ShapeDtypeStruct((1, N_CHIP * cols), jnp.float32))
        + _exchange_out_shapes([w_in_t], True),
        in_specs=[vmem, vmem, vmem, hbm],
        out_specs=(vmem, vmem, vmem, hbm),
        scratch_shapes=[pltpu.VMEM((groups, N_DEV, 128), jnp.float32)] + _gather_sems(1)
        + _all_devices_sems() + _all_devices_sems(),
        compiler_params=pltpu.CompilerParams(vmem_limit_bytes=VMEM_LIMIT),
    )(cond_rows, w_ada_blk, b_blk, w_in_t)


def _exchange_out_shapes(arrays, gather):
    return tuple(jax.ShapeDtypeStruct((N_CHIP,) + a.shape if gather else a.shape, a.dtype) for a in arrays)


def _scatter_sems(n):
    n_sem = n * (N_CHIP - 1)
    return [pltpu.SemaphoreType.DMA((n_sem,)), pltpu.SemaphoreType.DMA((n_sem,)), pltpu.SemaphoreType.DMA((n,))]


def _gather_sems(n):
    n_sem = n * (N_CHIP - 1)
    return [pltpu.SemaphoreType.DMA((n_sem,))] * 4 + [pltpu.SemaphoreType.DMA((n,))]


def _peer_chips(x, y):
    out = []
    for k in range(1, N_CHIP):
        px, py = _flip(x, (k >> 1) & 1), _flip(y, k & 1)
        out.append((px, py, 2 * px + py))
    return out


class _ChipScatter:
    def __init__(self, ins, outs, sems):
        send_sems, recv_sems, local_sems = sems
        x, y, c = _mesh_pos()
        chip = 2 * x + y
        self.local, self.sends, self.recvs = [], [], []
        for i in range(len(ins)):
            self.local.append(pltpu.make_async_copy(ins[i].at[chip], outs[i].at[chip], local_sems.at[i]))
            for k, (px, py, peer_chip) in enumerate(_peer_chips(x, y)):
                sem = i * (N_CHIP - 1) + k
                src = ins[i].at[peer_chip]
                self.sends.append(pltpu.make_async_remote_copy(
                    src_ref=src, dst_ref=outs[i].at[chip], send_sem=send_sems.at[sem], recv_sem=recv_sems.at[sem],
                    device_id=(px, py, c), device_id_type=MESH))
                self.recvs.append(pltpu.make_async_remote_copy(
                    src_ref=src, dst_ref=outs[i].at[peer_chip], send_sem=send_sems.at[sem], recv_sem=recv_sems.at[sem],
                    device_id=(px, py, c), device_id_type=MESH))

    def start(self):
        for cp in self.local + self.sends:
            cp.start()

    def wait(self):
        for cp in self.recvs:
            cp.wait_recv()
        for cp in self.sends:
            cp.wait_send()
        for cp in self.local:
            cp.wait()


class _ChipGather:
    def __init__(self, ins, outs, sems):
        ici_send, ici_recv, d2d_send, d2d_recv, local_sems = sems
        x, y, c = _mesh_pos()
        chip = 2 * x + y
        self.local, self.ici_sends, self.ici_recvs, self.d2d_sends, self.d2d_recvs = [], [], [], [], []
        for i in range(len(ins)):
            half = ins[i].shape[-1] // 2
            assert half % 128 == 0
            lead = (slice(None),) * (len(ins[i].shape) - 1)
            mine = lead + (pl.ds(pl.multiple_of(c * half, 128), half),)
            theirs = lead + (pl.ds(pl.multiple_of((1 - c) * half, 128), half),)
            self.local.append(pltpu.make_async_copy(ins[i], outs[i].at[chip], local_sems.at[i]))
            for k, (px, py, peer_chip) in enumerate(_peer_chips(x, y)):
                sem = i * (N_CHIP - 1) + k
                self.ici_sends.append(pltpu.make_async_remote_copy(
                    src_ref=ins[i].at[mine], dst_ref=outs[i].at[chip].at[mine],
                    send_sem=ici_send.at[sem], recv_sem=ici_recv.at[sem], device_id=(px, py, c), device_id_type=MESH))
                landed = outs[i].at[peer_chip].at[mine]
                self.ici_recvs.append(pltpu.make_async_remote_copy(
                    src_ref=ins[i].at[mine], dst_ref=landed,
                    send_sem=ici_send.at[sem], recv_sem=ici_recv.at[sem], device_id=(px, py, c), device_id_type=MESH))
                self.d2d_sends.append(pltpu.make_async_remote_copy(
                    src_ref=landed, dst_ref=landed,
                    send_sem=d2d_send.at[sem], recv_sem=d2d_recv.at[sem], device_id=(x, y, 1 - c), device_id_type=MESH))
                self.d2d_recvs.append(pltpu.make_async_remote_copy(
                    src_ref=landed, dst_ref=outs[i].at[peer_chip].at[theirs],
                    send_sem=d2d_send.at[sem], recv_sem=d2d_recv.at[sem], device_id=(x, y, 1 - c), device_id_type=MESH))

    def start(self):
        for cp in self.local + self.ici_sends:
            cp.start()

    def forward(self):
        for landed, onward in zip(self.ici_recvs, self.d2d_sends):
            landed.wait_recv()
            onward.start()

    def finish(self):
        for cp in self.d2d_recvs:
            cp.wait_recv()
        for cp in self.d2d_sends + self.ici_sends:
            cp.wait_send()
        for cp in self.local:
            cp.wait()


def _swap_sems(n):
    return [pltpu.SemaphoreType.DMA((n,)), pltpu.SemaphoreType.DMA((n,))]


class _SiblingSwap:
    def __init__(self, ins, outs, sems):
        send_sems, recv_sems = sems
        x, y, c = _mesh_pos()
        self.copies = [pltpu.make_async_remote_copy(
            src_ref=ins[i], dst_ref=outs[i], send_sem=send_sems.at[i], recv_sem=recv_sems.at[i],
            device_id=(x, y, 1 - c), device_id_type=MESH) for i in range(len(ins))]

    def start(self):
        for cp in self.copies:
            cp.start()

    def wait(self):
        for cp in self.copies:
            cp.wait_recv()
        for cp in self.copies:
            cp.wait_send()


def _adam(w, g, m, v):
    m2 = ADAM_B1 * m + (1.0 - ADAM_B1) * g
    v2 = ADAM_B2 * v + (1.0 - ADAM_B2) * (g * g)
    m_hat = m2 / (1.0 - ADAM_B1 ** ADAM_STEP)
    v_hat = v2 / (1.0 - ADAM_B2 ** ADAM_STEP)
    delta = -ADAM_LR * (m_hat / (jnp.sqrt(v_hat) + ADAM_EPS) + ADAM_WD * w)
    return delta, m2, v2


def _ada_bwd_adam(c_t, dmod_blk, w, m, v):
    rows, cols = w.shape
    tile = 512
    assert cols % tile == 0

    def body(c_ref, d_ref, w_ref, m_ref, v_ref, g_ref, dl_ref, m2_ref, v2_ref):
        sc = c_ref[...]
        sc = sc * _sigmoid(sc)
        dm = d_ref[...]
        g = sc[:, 0:1] * dm[0:1, :]
        for b in range(1, N_DEV):
            g = g + sc[:, b:b + 1] * dm[b:b + 1, :]
        delta, m2, v2 = _adam(w_ref[...], g, m_ref[...], v_ref[...])
        g_ref[...] = g
        dl_ref[...] = delta
        m2_ref[...] = m2
        v2_ref[...] = v2

    blk = pl.BlockSpec((rows, tile), lambda j: (0, j))
    out = jax.ShapeDtypeStruct((rows, cols), jnp.float32)
    return pl.pallas_call(
        body, name="ada_bwd_adam", grid=(cols // tile,),
        out_shape=(out, out, out, out),
        in_specs=[pl.BlockSpec((rows, N_DEV), lambda j: (0, 0)), pl.BlockSpec((N_DEV, tile), lambda j: (0, j)),
                  blk, blk, blk],
        out_specs=(blk, blk, blk, blk),
        compiler_params=_params(("arbitrary",)),
    )(c_t, dmod_blk, w, m, v)


MOD_SHIFT1, MOD_SCALE1, MOD_GATE1, MOD_SHIFT2, MOD_SCALE2, MOD_GATE2 = range(6)


def _mod(mod_ref, segment):
    return mod_ref[:, segment * D_MODEL:(segment + 1) * D_MODEL]


def _inproj_fwd(x2, vecs, w_in_p, tm, riders):
    seq = x2.shape[0]
    n_tiles = seq // tm
    n_ride = len(riders)

    def body(*refs):
        x_ref, vec_ref, w_ref = refs[:3]
        ride_in, refs = refs[3:3 + n_ride], refs[3 + n_ride:]
        p_ref, u_ref = refs[:2]
        ride_out, sems = refs[2:2 + n_ride], refs[2 + n_ride:]
        gather = _ChipGather(ride_in, ride_out, sems)

        @pl.when(pl.program_id(0) == 0)
        def _():
            gather.start()

        xh, _ = _ln(x_ref[...])
        u = (xh * (1.0 + _mod(vec_ref, MOD_SCALE1)) + _mod(vec_ref, MOD_SHIFT1)).astype(MXU_DTYPE)
        u_ref[...] = u
        p_ref[...] = _mm(u, w_ref[...]).astype(p_ref.dtype)

        @pl.when(pl.program_id(0) == (3 * n_tiles) // 4)
        def _():
            gather.forward()

        @pl.when(pl.program_id(0) == n_tiles - 1)
        def _():
            gather.finish()

    hbm = pl.BlockSpec(memory_space=pl.ANY)
    return pl.pallas_call(
        body, name="inproj_fwd", grid=(n_tiles,),
        out_shape=(jax.ShapeDtypeStruct((seq, N_PROJ), MXU_DTYPE), jax.ShapeDtypeStruct((seq, D_MODEL), MXU_DTYPE))
        + _exchange_out_shapes(riders, True),
        in_specs=[pl.BlockSpec((tm, D_MODEL), lambda i: (i, 0)), _const_spec(vecs.shape), _const_spec(w_in_p.shape)]
        + [hbm] * n_ride,
        out_specs=(pl.BlockSpec((tm, N_PROJ), lambda i: (i, 0)), pl.BlockSpec((tm, D_MODEL), lambda i: (i, 0)))
        + (hbm,) * n_ride,
        scratch_shapes=_gather_sems(n_ride),
        compiler_params=_params(("arbitrary",)),
    )(x2, vecs, w_in_p, *riders)


def _inproj_bwd(dproj, x2, dxa, vecs, w_in_pt, tm, riders):
    seq = x2.shape[0]
    n_tiles = seq // tm
    n_ride = len(riders)

    def body(*refs):
        dp_ref, x_ref, dxa_ref, vec_ref, w_ref = refs[:5]
        ride_in, refs = refs[5:5 + n_ride], refs[5 + n_ride:]
        gx_ref, sums_ref = refs[:2]
        ride_out, sems = refs[2:2 + n_ride], refs[2 + n_ride:]
        exchange = _ChipScatter(ride_in, ride_out, sems)

        @pl.when(pl.program_id(0) == 0)
        def _():
            exchange.start()
            sums_ref[...] = jnp.zeros_like(sums_ref)

        du = _mm(dp_ref[...], w_ref[...])
        xh, rstd = _ln(x_ref[...])
        sums_ref[0:1, :] += _colsum(du)
        sums_ref[1:2, :] += _colsum(du * xh)
        gx_ref[...] = dxa_ref[...] + _ln_bwd(du * (1.0 + _mod(vec_ref, MOD_SCALE1)), xh, rstd)

        @pl.when(pl.program_id(0) == n_tiles - 1)
        def _():
            exchange.wait()

    tile = pl.BlockSpec((tm, D_MODEL), lambda i: (i, 0))
    hbm = pl.BlockSpec(memory_space=pl.ANY)
    return pl.pallas_call(
        body, name="inproj_bwd", grid=(n_tiles,),
        out_shape=(jax.ShapeDtypeStruct((seq, D_MODEL), jnp.float32), jax.ShapeDtypeStruct((8, D_MODEL), jnp.float32))
        + _exchange_out_shapes(riders, False),
        in_specs=[pl.BlockSpec((tm, N_PROJ), lambda i: (i, 0)), tile, tile, _const_spec(vecs.shape),
                  _const_spec(w_in_pt.shape)] + [hbm] * n_ride,
        out_specs=(tile, pl.BlockSpec((8, D_MODEL), lambda i: (0, 0))) + (hbm,) * n_ride,
        scratch_shapes=_scatter_sems(n_ride),
        compiler_params=_params(("arbitrary",)),
    )(dproj, x2, dxa, vecs, w_in_pt, *riders)


def _head(h):
    return slice(h * HEAD_W, (h + 1) * HEAD_W)


def _cols(ref, off, h):
    return ref[:, off + h * HEAD_W:off + (h + 1) * HEAD_W].astype(jnp.float32)


HEADS = range(N_HEADS)


def _mixer_chunk_forward(p_ref, cc, ss, dm_ref, qdec_ref, kdec_ref, wg_ref, bg_ref, states):
    row, col = _tri_masks()
    lower = row >= col
    f = {}
    f["glr"] = p_ref[:, OFF_LR:OFF_LR + HEAD_W]
    f["logit"] = _mm(f["glr"], wg_ref[...]) + bg_ref[...]
    rq = [_cols(p_ref, OFF_RQ, h) for h in HEADS]
    rk = [_cols(p_ref, OFF_RK, h) for h in HEADS]
    f["rv"] = [_cols(p_ref, OFF_RV, h) for h in HEADS]
    f["qr"] = [(rq[h] * cc + _swap_halves(rq[h]) * ss) * RET_SCALE for h in HEADS]
    f["kr"] = [rk[h] * cc + _swap_halves(rk[h]) * ss for h in HEADS]
    s_raw = [_mm_nt(f["qr"][h], f["kr"][h]) for h in HEADS]
    yield
    la = _log_sigmoid(f["logit"]) * (1.0 / GATE_TAU)
    b = _running_sum(lower, la)
    f["qd"] = [f["qr"][h] * qdec_ref[:, _head(h)] for h in HEADS]
    f["kd"] = [f["kr"][h] * kdec_ref[:, _head(h)] for h in HEADS]
    f["scores"] = [s_raw[h] * dm_ref[h] for h in HEADS]
    yield
    b_last = b[CHUNK - 1:CHUNK, :]
    b_mid = b[CHUNK // 2 - 1:CHUNK // 2, :]
    f["e"], f["ei"] = jnp.exp(b - b_mid), jnp.exp(b_mid - b)
    f["eb"], f["ek"], f["ebl"] = jnp.exp(b), jnp.exp(b_last - b), jnp.exp(b_last)
    gq = [_cols(p_ref, OFF_GQ, h) * GLA_SCALE for h in HEADS]
    gk = [_cols(p_ref, OFF_GK, h) for h in HEADS]
    f["gv"] = [_cols(p_ref, OFF_GV, h) for h in HEADS]
    f["q_e"] = [gq[h] * f["e"][:, _head(h)] for h in HEADS]
    f["q_i"] = [gq[h] * f["ei"][:, _head(h)] for h in HEADS]
    f["k_e"] = [gk[h] * f["e"][:, _head(h)] for h in HEADS]
    f["k_i"] = [gk[h] * f["ei"][:, _head(h)] for h in HEADS]
    low = [_mm_nt(f["q_e"][h], f["k_i"][h]) for h in HEADS]
    up = [_mm_nt(f["q_i"][h], f["k_e"][h]) for h in HEADS]
    yield
    f["att"] = [jnp.where(lower, low[h], up[h]) for h in HEADS]
    f["qb"] = [gq[h] * f["eb"][:, _head(h)] for h in HEADS]
    f["kb"] = [gk[h] * f["ek"][:, _head(h)] for h in HEADS]
    ret_state, gla_state_t = states()
    f["o_ret"] = [_mm(f["scores"][h], f["rv"][h]) + _mm(f["qd"][h], ret_state[h]) for h in HEADS]
    f["o_gla"] = [_mm(f["att"][h], f["gv"][h]) + _mm_nt(f["qb"][h], gla_state_t[h]) for h in HEADS]
    return f


def _interleave(generators):
    live = list(generators)
    while live:
        for g in list(live):
            try:
                next(g)
            except StopIteration:
                live.remove(g)


def _mixer_fwd(proj, tables, wg_p, bg_p, ret_norm_w, gla_norm_w, riders):
    seq = proj.shape[0]
    n_chunks = seq // CHUNK
    per_step = min(n_chunks, CHUNKS_PER_STEP)
    n_steps = n_chunks // per_step
    n_ride = len(riders)
    rot_a, rot_b, dm_t, qdec_t, kdec_t, chunk_decay = tables

    def body(*refs):
        p_ref, ra_ref, rb_ref, dm_ref, qdec_ref, kdec_ref, wg_ref, bg_ref, wr_ref, wl_ref = refs[:10]
        ride_in, refs = refs[10:10 + n_ride], refs[10 + n_ride:]
        mix_ref, rsave_ref, ssave_ref = refs[:3]
        ride_out, refs = refs[3:3 + n_ride], refs[3 + n_ride:]
        r_sc, s_sc = refs[:2]
        gather = _ChipGather(ride_in, ride_out, refs[2:])

        @pl.when(pl.program_id(0) == 0)
        def _():
            gather.start()
            r_sc[...] = jnp.zeros_like(r_sc)
            s_sc[...] = jnp.zeros_like(s_sc)

        def one_chunk(c):
            p_c = p_ref.at[c * CHUNK:(c + 1) * CHUNK, :]
            mix_c = mix_ref.at[c * CHUNK:(c + 1) * CHUNK, :]
            before = {}

            def states():
                before["ret"] = [r_sc[h] for h in HEADS]
                before["gla"] = [s_sc[h] for h in HEADS]
                for h in HEADS:
                    rsave_ref[c, h] = before["ret"][h].astype(rsave_ref.dtype)
                    ssave_ref[c, h] = before["gla"][h]
                return before["ret"], before["gla"]

            cc, ss = _rotary_chunk(ra_ref, c, rb_ref)
            f = yield from _mixer_chunk_forward(p_c, cc, ss, dm_ref, qdec_ref, kdec_ref, wg_ref, bg_ref, states)
            for h in HEADS:
                r_sc[h] = chunk_decay[h] * before["ret"][h] + _mm_tn(f["kd"][h], f["rv"][h])
            for h in HEADS:
                s_sc[h] = before["gla"][h] * f["ebl"][:, _head(h)] + _mm_tn(f["gv"][h], f["kb"][h])
            yield
            for h in HEADS:
                on, _ = _ln(f["o_ret"][h])
                g = _cols(p_c, OFF_RG, h)
                mix_c[:, _head(h)] = (on * wr_ref[:, _head(h)] * (g * _sigmoid(g))).astype(mix_ref.dtype)
            for h in HEADS:
                o = f["o_gla"][h]
                on = o * lax.rsqrt(_rowmean(o * o) + LN_EPS)
                g = _cols(p_c, OFF_GG, h)
                mix_c[:, _head(N_HEADS + h)] = (on * wl_ref[:, _head(h)] * (g * _sigmoid(g))).astype(mix_ref.dtype)

        for c0 in range(0, per_step, CHUNKS_IN_LOCKSTEP):
            _interleave([one_chunk(c) for c in range(c0, min(per_step, c0 + CHUNKS_IN_LOCKSTEP))])

        @pl.when(pl.program_id(0) == (3 * n_steps) // 4)
        def _():
            gather.forward()

        @pl.when(pl.program_id(0) == n_steps - 1)
        def _():
            gather.finish()

    state_shape = (n_chunks, N_HEADS, HEAD_W, HEAD_W)
    state_blk = pl.BlockSpec((per_step, N_HEADS, HEAD_W, HEAD_W), lambda i: (i, 0, 0, 0))
    rot_blk = pl.BlockSpec((per_step, 8, HEAD_W), lambda i: (i, 0, 0))
    rows = per_step * CHUNK
    hbm = pl.BlockSpec(memory_space=pl.ANY)
    return pl.pallas_call(
        body, name="mixer_fwd", grid=(n_steps,),
        out_shape=(jax.ShapeDtypeStruct((seq, D_MODEL), MXU_DTYPE),
                   jax.ShapeDtypeStruct(state_shape, MXU_DTYPE), jax.ShapeDtypeStruct(state_shape, jnp.float32))
        + _exchange_out_shapes(riders, True),
        in_specs=[pl.BlockSpec((rows, N_PROJ), lambda i: (i, 0)), rot_blk, _const_spec(rot_b.shape),
                  _const_spec(dm_t.shape), _const_spec(qdec_t.shape), _const_spec(kdec_t.shape),
                  _const_spec(wg_p.shape), _const_spec(bg_p.shape), _const_spec(ret_norm_w.shape),
                  _const_spec(gla_norm_w.shape)] + [hbm] * n_ride,
        out_specs=(pl.BlockSpec((rows, D_MODEL), lambda i: (i, 0)), state_blk, state_blk) + (hbm,) * n_ride,
        scratch_shapes=[pltpu.VMEM((N_HEADS, HEAD_W, HEAD_W), jnp.float32),
                        pltpu.VMEM((N_HEADS, HEAD_W, HEAD_W), jnp.float32)] + _gather_sems(n_ride),
        compiler_params=_params(("arbitrary",)),
    )(proj, rot_a, rot_b, dm_t, qdec_t, kdec_t, wg_p, bg_p, ret_norm_w, gla_norm_w, *riders)


def _mixer_bwd(proj, dmixed, rsave, ssave, tables, wg_p, bg_p, ret_norm_w, gla_norm_w, riders):
    seq = proj.shape[0]
    n_chunks = seq // CHUNK
    per_step = min(n_chunks, CHUNKS_PER_STEP)
    n_steps = n_chunks // per_step
    n_ride = len(riders)
    rot_a, rot_b, dm_t, qdec_t, kdec_t, chunk_decay = tables
    last = n_steps - 1

    def body(*refs):
        p_blk, dmx_blk = refs[:2]
        shared_in = refs[2:13]
        ride_in, refs = refs[13:13 + n_ride], refs[13 + n_ride:]
        dp_blk, dwr_ref, dwl_ref, dwg_ref, dbg_ref = refs[:5]
        ride_out, refs = refs[5:5 + n_ride], refs[5 + n_ride:]
        dr_sc, ds_sc = refs[:2]
        exchange = _ChipScatter(ride_in, ride_out, refs[2:])

        @pl.when(pl.program_id(0) == 0)
        def _():
            exchange.start()
            dr_sc[...] = jnp.zeros_like(dr_sc)
            ds_sc[...] = jnp.zeros_like(ds_sc)
            dwr_ref[...] = jnp.zeros_like(dwr_ref)
            dwl_ref[...] = jnp.zeros_like(dwl_ref)
            dwg_ref[...] = jnp.zeros_like(dwg_ref)
            dbg_ref[...] = jnp.zeros_like(dbg_ref)

        def chunk_stages(c):
            rows = slice(c * CHUNK, (c + 1) * CHUNK)
            return one_chunk(c, p_blk.at[rows, :], dmx_blk.at[rows, :], dp_blk.at[rows, :], *shared_in,
                             dwr_ref, dwl_ref, dwg_ref, dbg_ref, dr_sc, ds_sc)

        for c0 in range(per_step, 0, -CHUNKS_IN_LOCKSTEP):
            _interleave([chunk_stages(c) for c in reversed(range(max(0, c0 - CHUNKS_IN_LOCKSTEP), c0))])

        @pl.when(pl.program_id(0) == last)
        def _():
            exchange.wait()

    def one_chunk(c, p_ref, dmx_ref, dp_ref, rsave_ref, ssave_ref, ra_ref, rb_ref, dm_ref, qdec_ref, kdec_ref,
                  wg_ref, bg_ref, wr_ref, wl_ref, dwr_ref, dwl_ref, dwg_ref, dbg_ref, dr_sc, ds_sc):
        def put(off, h, val):
            dp_ref[:, off + h * HEAD_W:off + (h + 1) * HEAD_W] = val.astype(dp_ref.dtype)

        cc, ss = _rotary_chunk(ra_ref, c, rb_ref)
        row, col = _tri_masks()
        ret_state = [rsave_ref[c, h] for h in HEADS]
        gla_state_t = [ssave_ref[c, h] for h in HEADS]
        f = yield from _mixer_chunk_forward(p_ref, cc, ss, dm_ref, qdec_ref, kdec_ref, wg_ref, bg_ref,
                                            lambda: (ret_state, gla_state_t))
        yield

        do_ret, do_gla = [], []
        for h in HEADS:
            on, rstd = _ln(f["o_ret"][h])
            g = _cols(p_ref, OFF_RG, h)
            sg = _sigmoid(g)
            dy = dmx_ref[:, _head(h)].astype(jnp.float32)
            wr = wr_ref[:, _head(h)]
            dwr_ref[:, _head(h)] += _colsum(dy * on * (g * sg))
            put(OFF_RG, h, dy * on * wr * (sg * (1.0 + g * (1.0 - sg))))
            do_ret.append(_ln_bwd(dy * wr * (g * sg), on, rstd))
        for h in HEADS:
            o = f["o_gla"][h]
            rstd = lax.rsqrt(_rowmean(o * o) + LN_EPS)
            on = o * rstd
            g = _cols(p_ref, OFF_GG, h)
            sg = _sigmoid(g)
            dy = dmx_ref[:, _head(N_HEADS + h)].astype(jnp.float32)
            wl = wl_ref[:, _head(h)]
            dwl_ref[:, _head(h)] += _colsum(dy * on * (g * sg))
            put(OFF_GG, h, dy * on * wl * (sg * (1.0 + g * (1.0 - sg))))
            don = dy * wl * (g * sg)
            do_gla.append(rstd * (don - on * _rowmean(don * on)))

        yield

        d_ret_new = [dr_sc[h] for h in HEADS]
        d_gla_new = [ds_sc[h] for h in HEADS]
        ds_raw = [_mm_nt(do_ret[h], f["rv"][h]) * dm_ref[h] for h in HEADS]
        d_att = [_mm_nt(do_gla[h], f["gv"][h]) for h in HEADS]
        dq_state = [_mm_nt(do_ret[h], ret_state[h]) for h in HEADS]
        dk_state = [_mm_nt(f["rv"][h], d_ret_new[h]) for h in HEADS]
        dqb = [_mm(do_gla[h], gla_state_t[h]) for h in HEADS]
        dkb = [_mm(f["gv"][h], d_gla_new[h]) for h in HEADS]
        for h in HEADS:
            put(OFF_RV, h, _mm_tn(f["scores"][h], do_ret[h]) + _mm(f["kd"][h], d_ret_new[h]))
        for h in HEADS:
            put(OFF_GV, h, _mm_tn(f["att"][h], do_gla[h]) + _mm_nt(f["kb"][h], d_gla_new[h]))
        for h in HEADS:
            dr_sc[h] = chunk_decay[h] * d_ret_new[h] + _mm_tn(f["qd"][h], do_ret[h])
        for h in HEADS:
            ds_sc[h] = d_gla_new[h] * f["ebl"][:, _head(h)] + _mm_tn(do_gla[h], f["qb"][h])
        yield

        dqr = [_mm(ds_raw[h], f["kr"][h]) + dq_state[h] * qdec_ref[:, _head(h)] for h in HEADS]
        dkr = [_mm_tn(ds_raw[h], f["qr"][h]) + dk_state[h] * kdec_ref[:, _head(h)] for h in HEADS]
        d_low = [jnp.where(row >= col, d_att[h], 0.0) for h in HEADS]
        d_up = [jnp.where(row < col, d_att[h], 0.0) for h in HEADS]
        dq_e = [_mm(d_low[h], f["k_i"][h]) for h in HEADS]
        dk_i = [_mm_tn(d_low[h], f["q_e"][h]) for h in HEADS]
        dq_i = [_mm(d_up[h], f["k_e"][h]) for h in HEADS]
        dk_e = [_mm_tn(d_up[h], f["q_i"][h]) for h in HEADS]
        yield
        for h in HEADS:
            put(OFF_RQ, h, (dqr[h] * cc + _swap_halves(dqr[h] * ss)) * RET_SCALE)
            put(OFF_RK, h, dkr[h] * cc + _swap_halves(dkr[h] * ss))
        row_id = lax.broadcasted_iota(jnp.int32, (CHUNK, HEAD_W), 0)
        db_heads = []
        for h in HEADS:
            hs = _head(h)
            e, ei, eb, ek, ebl = f["e"][:, hs], f["ei"][:, hs], f["eb"][:, hs], f["ek"][:, hs], f["ebl"][:, hs]
            put(OFF_GQ, h, (dq_e[h] * e + dq_i[h] * ei + dqb[h] * eb) * GLA_SCALE)
            put(OFF_GK, h, dk_e[h] * e + dk_i[h] * ei + dkb[h] * ek)
            db = (dq_e[h] * f["q_e"][h] - dq_i[h] * f["q_i"][h] + dk_e[h] * f["k_e"][h] - dk_i[h] * f["k_i"][h]
                  + dqb[h] * f["qb"][h] - dkb[h] * f["kb"][h])
            db_last = _colsum(dkb[h] * f["kb"][h]) + ebl * _colsum(gla_state_t[h] * d_gla_new[h])
            db_heads.append(db + jnp.where(row_id == CHUNK - 1, db_last, 0.0))
        db = jnp.concatenate(db_heads, axis=1)
        d_la = _running_sum(col >= row, db)
        d_logit = d_la * (1.0 / GATE_TAU) * (1.0 - _sigmoid(f["logit"]))
        put(OFF_LR, 0, _mm_nt(d_logit, wg_ref[...]))
        dwg_ref[...] += _mm_tn(f["glr"], d_logit)
        dbg_ref[...] += _colsum(d_logit)

    state_blk = pl.BlockSpec((per_step, N_HEADS, HEAD_W, HEAD_W), lambda i: (last - i, 0, 0, 0))
    rot_blk = pl.BlockSpec((per_step, 8, HEAD_W), lambda i: (last - i, 0, 0))
    width = N_HEADS * HEAD_W
    vec_out = pl.BlockSpec((1, width), lambda i: (0, 0))
    hbm = pl.BlockSpec(memory_space=pl.ANY)
    rows_blk = per_step * CHUNK
    return pl.pallas_call(
        body, name="mixer_bwd", grid=(n_steps,),
        out_shape=(jax.ShapeDtypeStruct((seq, N_PROJ), MXU_DTYPE),
                   jax.ShapeDtypeStruct((1, width), jnp.float32), jax.ShapeDtypeStruct((1, width), jnp.float32),
                   jax.ShapeDtypeStruct((HEAD_W, width), jnp.float32), jax.ShapeDtypeStruct((1, width), jnp.float32))
        + _exchange_out_shapes(riders, False),
        in_specs=[pl.BlockSpec((rows_blk, N_PROJ), lambda i: (last - i, 0)),
                  pl.BlockSpec((rows_blk, D_MODEL), lambda i: (last - i, 0)), state_blk, state_blk, rot_blk,
                  _const_spec(rot_b.shape),
                  _const_spec(dm_t.shape), _const_spec(qdec_t.shape), _const_spec(kdec_t.shape),
                  _const_spec(wg_p.shape), _const_spec(bg_p.shape), _const_spec(ret_norm_w.shape),
                  _const_spec(gla_norm_w.shape)] + [hbm] * n_ride,
        out_specs=(pl.BlockSpec((rows_blk, N_PROJ), lambda i: (last - i, 0)), vec_out, vec_out,
                   pl.BlockSpec((HEAD_W, width), lambda i: (0, 0)), vec_out) + (hbm,) * n_ride,
        scratch_shapes=[pltpu.VMEM((N_HEADS, HEAD_W, HEAD_W), jnp.float32),
                        pltpu.VMEM((N_HEADS, HEAD_W, HEAD_W), jnp.float32)] + _scatter_sems(n_ride),
        compiler_params=_params(("arbitrary",)),
    )(proj, dmixed, rsave, ssave, rot_a, rot_b, dm_t, qdec_t, kdec_t, wg_p, bg_p, ret_norm_w, gla_norm_w, *riders)


V_GATE1, V_SCALE2, V_SHIFT2, V_GATE2, V_LN1W, V_LN1B, V_LN2W, V_LN2B = range(8)
S_GATE1, S_SCALE2, S_SHIFT2, S_GATE2, S_LN1W, S_LN1B, S_LN2W, S_LN2B, S_LOSS = range(9)


def _mlp_fwd_bwd(x2, mixed, target, mod, ln_rows, w_out, w1_chunks, w2_chunks, tm):
    seq = x2.shape[0]
    n_fc, _, fc = w1_chunks.shape
    segment_of = {V_GATE1: MOD_GATE1, V_SCALE2: MOD_SCALE2, V_SHIFT2: MOD_SHIFT2, V_GATE2: MOD_GATE2}

    def body(x_ref, mx_ref, t_ref, mod_ref, ln_ref, wo_ref, w1_ref, w2_ref,
             dmx_ref, dxa_ref, a_ref, dh_ref, u2_ref, df_ref, dm_ref, sums_ref, relu_sc):
        @pl.when(pl.program_id(0) == 0)
        def _():
            sums_ref[...] = jnp.zeros_like(sums_ref)

        def vec(r):
            if r in segment_of:
                return _mod(mod_ref, segment_of[r])
            return ln_ref[r - V_LN1W:r - V_LN1W + 1, :]

        def acc(r, val):
            sums_ref[r:r + 1, :] += _colsum(val)

        xx = x_ref[...]
        m = _mm(mx_ref[...], wo_ref[...])
        z1h, rstd1 = _ln(ALPHA * xx + vec(V_GATE1) * m)
        x1 = z1h * vec(V_LN1W) + vec(V_LN1B)
        x1h, rstd0 = _ln(x1)
        u2 = (x1h * (1.0 + vec(V_SCALE2)) + vec(V_SHIFT2)).astype(MXU_DTYPE)
        u2_ref[...] = u2
        f = jnp.zeros((tm, D_MODEL), jnp.float32)
        for j in range(n_fc):
            r = jnp.maximum(_mm(u2, w1_ref[j]), 0.0)
            relu_sc[:, j * fc:(j + 1) * fc] = r
            a = (r * r).astype(MXU_DTYPE)
            a_ref[:, j * fc:(j + 1) * fc] = a
            f = f + _mm(a, w2_ref[j])
        z2h, rstd2 = _ln(ALPHA * x1 + vec(V_GATE2) * f)
        err = z2h * vec(V_LN2W) + vec(V_LN2B) - t_ref[...]
        acc(S_LOSS, err * err)
        dy = err * (1.0 / D_MODEL)
        acc(S_LN2W, dy * z2h)
        acc(S_LN2B, dy)
        dz2 = _ln_bwd(dy * vec(V_LN2W), z2h, rstd2)
        acc(S_GATE2, dz2 * f)
        df = (vec(V_GATE2) * dz2).astype(MXU_DTYPE)
        df_ref[...] = df
        du2 = jnp.zeros((tm, D_MODEL), jnp.float32)
        for j in range(n_fc):
            dh = (_mm_nt(df, w2_ref[j]) * (2.0 * relu_sc[:, j * fc:(j + 1) * fc])).astype(MXU_DTYPE)
            dh_ref[:, j * fc:(j + 1) * fc] = dh
            du2 = du2 + _mm_nt(dh, w1_ref[j])
        acc(S_SCALE2, du2 * x1h)
        acc(S_SHIFT2, du2)
        dx1 = ALPHA * dz2 + _ln_bwd(du2 * (1.0 + vec(V_SCALE2)), x1h, rstd0)
        acc(S_LN1W, dx1 * z1h)
        acc(S_LN1B, dx1)
        dz1 = _ln_bwd(dx1 * vec(V_LN1W), z1h, rstd1)
        acc(S_GATE1, dz1 * m)
        dxa_ref[...] = ALPHA * dz1
        dm = (vec(V_GATE1) * dz1).astype(MXU_DTYPE)
        dm_ref[...] = dm
        dmx_ref[...] = _mm_nt(dm, wo_ref[...])

    tile = lambda width: pl.BlockSpec((tm, width), lambda i: (i, 0))
    f32 = lambda width: jax.ShapeDtypeStruct((seq, width), jnp.float32)
    b16 = lambda width: jax.ShapeDtypeStruct((seq, width), MXU_DTYPE)
    return pl.pallas_call(
        body, name="mlp_fwd_bwd", grid=(seq // tm,),
        out_shape=(f32(D_MODEL), f32(D_MODEL), b16(D_FF), b16(D_FF), b16(D_MODEL), b16(D_MODEL), b16(D_MODEL),
                   jax.ShapeDtypeStruct((16, D_MODEL), jnp.float32)),
        in_specs=[tile(D_MODEL), tile(D_MODEL), tile(D_MODEL), _const_spec(mod.shape), _const_spec(ln_rows.shape),
                  _const_spec(w_out.shape), _const_spec(w1_chunks.shape), _const_spec(w2_chunks.shape)],
        out_specs=(tile(D_MODEL), tile(D_MODEL), tile(D_FF), tile(D_FF), tile(D_MODEL), tile(D_MODEL),
                   tile(D_MODEL), pl.BlockSpec((16, D_MODEL), lambda i: (0, 0))),
        scratch_shapes=[pltpu.VMEM((tm, D_FF), jnp.float32)],
        compiler_params=_params(("arbitrary",)),
    )(x2, mixed, target, mod, ln_rows, w_out, w1_chunks, w2_chunks)


def _grad_matmul(a, b, name, tn, blocks_are_rows, riders=()):
    seq, m_dim = a.shape
    n_dim = b.shape[1]
    tk = min(seq, GRAD_TOKEN_TILE)
    nk = seq // tk
    n_ride = len(riders)
    if blocks_are_rows:
        tm = m_dim // N_CHIP
        assert tn == n_dim
        per_step = N_CHIP if m_dim <= GRAD_ROWS_PER_STEP else 1
        grid = (N_CHIP // per_step, 1, nk)
        out_map = lambda i, j, k: (i, 0, 0)
    else:
        tm = m_dim
        assert tn * N_CHIP == n_dim
        per_step = 1
        grid = (1, N_CHIP, nk)
        out_map = lambda i, j, k: (j, 0, 0)
    n_blocks = grid[0] * grid[1]
    rows = per_step * tm

    def body(*refs):
        a_ref, b_ref = refs[:2]
        ride_in, refs = refs[2:2 + n_ride], refs[2 + n_ride:]
        o_ref = refs[0]
        ride_out, refs = refs[1:1 + n_ride], refs[1 + n_ride:]
        acc_sc = refs[0]
        exchange = _ChipScatter(ride_in, ride_out, refs[1:]) if n_ride else None
        block = pl.program_id(0) + pl.program_id(1)
        k = pl.program_id(2)

        if exchange is not None:
            @pl.when((block == 0) & (k == 0))
            def _():
                exchange.start()

        @pl.when(k == 0)
        def _():
            acc_sc[...] = jnp.zeros_like(acc_sc)

        acc_sc[...] += _mm_tn(a_ref[...], b_ref[...])

        @pl.when(k == nk - 1)
        def _():
            for p in range(per_step):
                o_ref[p] = acc_sc[p * tm:(p + 1) * tm, :].astype(o_ref.dtype)

        if exchange is not None:
            @pl.when((block == n_blocks - 1) & (k == nk - 1))
            def _():
                exchange.wait()

    hbm = pl.BlockSpec(memory_space=pl.ANY)
    out = pl.pallas_call(
        body, name=name, grid=grid,
        out_shape=(jax.ShapeDtypeStruct((N_CHIP, tm, tn), WIRE_DTYPE),) + _exchange_out_shapes(riders, False),
        in_specs=[pl.BlockSpec((tk, rows), lambda i, j, k: (k, i)), pl.BlockSpec((tk, tn), lambda i, j, k: (k, j))]
        + [hbm] * n_ride,
        out_specs=(pl.BlockSpec((per_step, tm, tn), out_map),) + (hbm,) * n_ride,
        scratch_shapes=[pltpu.VMEM((rows, tn), jnp.float32)] + (_scatter_sems(n_ride) if n_ride else []),
        compiler_params=_params(("arbitrary", "arbitrary", "arbitrary")),
    )(a, b, *riders)
    return out if n_ride else out[0]


def _grad_matmul_full(a, b, name, tm, riders):
    seq, m_dim = a.shape
    n_dim = b.shape[1]
    tk = min(seq, GRAD_TOKEN_TILE)
    nk = seq // tk
    n_blocks = m_dim // tm
    n_ride = len(riders)
    assert m_dim % tm == 0

    def body(*refs):
        a_ref, b_ref = refs[:2]
        ride_in, refs = refs[2:2 + n_ride], refs[2 + n_ride:]
        o_ref = refs[0]
        ride_out, refs = refs[1:1 + n_ride], refs[1 + n_ride:]
        acc_sc = refs[0]
        swap = _SiblingSwap(ride_in, ride_out, refs[1:])
        i, k = pl.program_id(0), pl.program_id(1)

        @pl.when((i == 0) & (k == 0))
        def _():
            swap.start()

        @pl.when(k == 0)
        def _():
            acc_sc[...] = jnp.zeros_like(acc_sc)

        acc_sc[...] += _mm_tn(a_ref[...], b_ref[...])

        @pl.when(k == nk - 1)
        def _():
            o_ref[...] = acc_sc[...].astype(o_ref.dtype)

        @pl.when((i == n_blocks - 1) & (k == nk - 1))
        def _():
            swap.wait()

    hbm = pl.BlockSpec(memory_space=pl.ANY)
    return pl.pallas_call(
        body, name=name, grid=(n_blocks, nk),
        out_shape=(jax.ShapeDtypeStruct((m_dim, n_dim), WIRE_DTYPE),)
        + tuple(jax.ShapeDtypeStruct(r.shape, r.dtype) for r in riders),
        in_specs=[pl.BlockSpec((tk, tm), lambda i, k: (k, i)), pl.BlockSpec((tk, n_dim), lambda i, k: (k, 0))]
        + [hbm] * n_ride,
        out_specs=(pl.BlockSpec((tm, n_dim), lambda i, k: (i, 0)),) + (hbm,) * n_ride,
        scratch_shapes=[pltpu.VMEM((tm, n_dim), jnp.float32)] + _swap_sems(n_ride),
        compiler_params=_params(("arbitrary", "arbitrary")),
    )(a, b, *riders)


def _adam_pair(w, g_mine, g_sibling, m, v, name):
    rows, cols = w.shape
    tc = min(cols, ELEMENTWISE_COLS)

    def total(ref):
        if len(ref.shape) == 2:
            return ref[...]
        acc = ref[0].astype(jnp.float32)
        for j in range(1, ref.shape[0]):
            acc = acc + ref[j].astype(jnp.float32)
        return acc

    def body(w_ref, ga_ref, gb_ref, m_ref, v_ref, g_ref, dl_ref, m2_ref, v2_ref):
        g = total(ga_ref) + total(gb_ref)
        delta, m2, v2 = _adam(w_ref[...], g, m_ref[...], v_ref[...])
        g_ref[...] = g
        dl_ref[...] = delta
        m2_ref[...] = m2
        v2_ref[...] = v2

    blk = pl.BlockSpec((rows, tc), lambda i: (0, i))
    g_blk = lambda a: blk if a.ndim == 2 else pl.BlockSpec((a.shape[0], rows, tc), lambda i: (0, 0, i))
    out = jax.ShapeDtypeStruct((rows, cols), jnp.float32)
    return pl.pallas_call(
        body, name=name, grid=(cols // tc,),
        out_shape=(out, out, out, out),
        in_specs=[blk, g_blk(g_mine), g_blk(g_sibling), blk, blk], out_specs=(blk,) * 4,
        compiler_params=_params(("arbitrary",)),
    )(w, g_mine, g_sibling, m, v)


def _sum_devices(gathered, layout):
    def body(g_ref, *o_refs):
        total = g_ref[0]
        for d in range(1, N_DEV):
            total = total + g_ref[d]
        for (first, (rows_out, cols_out)), o_ref in zip(layout, o_refs):
            per_row = cols_out // 128
            for r in range(rows_out):
                for k in range(per_row):
                    src = first + r * per_row + k
                    o_ref[r:r + 1, k * 128:(k + 1) * 128] = total[src:src + 1, :]
        tail = total[total.shape[0] - 8:, :]
        o_refs[-1][...] = jnp.full((1, 128), jnp.sum(tail), jnp.float32)

    out_shape = tuple(jax.ShapeDtypeStruct(shape, jnp.float32) for _, shape in layout)
    return pl.pallas_call(
        body, name="sum_devices",
        out_shape=out_shape + (jax.ShapeDtypeStruct((1, 128), jnp.float32),),
    )(gathered)


def _adam_small(params):
    n = len(params)

    def body(*refs):
        ins, outs = refs[:4 * n], refs[4 * n:]
        for i in range(n):
            w_ref, g_ref, m_ref, v_ref = ins[4 * i:4 * i + 4]
            delta, m2, v2 = _adam(w_ref[...], g_ref[...], m_ref[...], v_ref[...])
            outs[3 * i][...] = delta
            outs[3 * i + 1][...] = m2
            outs[3 * i + 2][...] = v2

    out_shape = tuple(jax.ShapeDtypeStruct(p[0].shape, jnp.float32) for p in params for _ in range(3))
    out = pl.pallas_call(body, name="adam_small", out_shape=out_shape)(*[t for p in params for t in p])
    return [out[3 * i:3 * i + 3] for i in range(n)]


def _pad_heads(w):
    lead = w.shape[:-1]
    w = w.reshape(lead + (N_HEADS, GLA_DK))
    w = jnp.pad(w, [(0, 0)] * len(lead) + [(0, 0), (0, HEAD_W - GLA_DK)])
    return w.reshape(lead + (N_HEADS * HEAD_W,))


def _unpad_heads(w):
    lead = w.shape[:-1]
    return w.reshape(lead + (N_HEADS, HEAD_W))[..., :GLA_DK].reshape(lead + (N_HEADS * GLA_DK,))


def _pad_head_rows(w):
    w = w.reshape(N_HEADS, GLA_DK, w.shape[-1])
    return jnp.pad(w, ((0, 0), (0, HEAD_W - GLA_DK), (0, 0))).reshape(N_HEADS * HEAD_W, w.shape[-1])


def _unpad_head_rows(w):
    return w.reshape(N_HEADS, HEAD_W, w.shape[-1])[:, :GLA_DK].reshape(N_HEADS * GLA_DK, w.shape[-1])


def _pad_w_in_rows(stack):
    w = stack.reshape(-1, stack.shape[-1])
    return jnp.concatenate([
        w[:2048], _pad_head_rows(w[2048:2304]), _pad_head_rows(w[2304:2560]), w[2560:3584],
        jnp.pad(w[3584:3600], ((0, HEAD_W - GATE_RANK), (0, 0)))], axis=0)


def _unpad_w_in_stack(g, per):
    segments = [(0, g[:2048]), (2048, _unpad_head_rows(g[OFF_GQ:OFF_GQ + 512])),
                (2304, _unpad_head_rows(g[OFF_GK:OFF_GK + 512])), (2560, g[OFF_GV:OFF_LR]),
                (3584, g[OFF_LR:OFF_LR + GATE_RANK])]
    blocks = []
    for j in range(N_CHIP):
        lo, hi = j * per, (j + 1) * per
        pieces = []
        for start, rows in segments:
            a, b = max(lo, start), min(hi, start + rows.shape[0])
            if a < b:
                pieces.append(rows[a - start:b - start])
        blocks.append(jnp.concatenate(pieces, axis=0))
    return jnp.stack(blocks)


def _col_major(w):
    return jnp.transpose(w, (2, 0, 1)).reshape(w.shape[2], w.shape[1])


def _rows128(a):
    return a.reshape(-1, 128)


def kernel(x, c, w_ada, b_ada, w_in, ret_norm_w, gla_gate_w, gla_gate_b, gla_norm_w, w_out, ln1_w, ln1_b, w_ff1, w_ff2, ln2_w, ln2_b, loss_target, m_w_ada, m_b_ada, m_w_in, m_ret_norm_w, m_gla_gate_w, m_gla_gate_b, m_gla_norm_w, m_w_out, m_ln1_w, m_ln1_b, m_w_ff1, m_w_ff2, m_ln2_w, m_ln2_b, v_w_ada, v_b_ada, v_w_in, v_ret_norm_w, v_gla_gate_w, v_gla_gate_b, v_gla_norm_w, v_w_out, v_ln1_w, v_ln1_b, v_w_ff1, v_w_ff2, v_ln2_w, v_ln2_b):
    seq = x.shape[1]
    tm = min(seq, TOKEN_TILE)
    tm_in = min(seq, INPROJ_TOKEN_TILE)
    xi, yi, _ = _mesh_pos()
    chip = 2 * xi + yi
    x2, target = x[0], loss_target[0]
    ada_cols = w_ada.shape[2]
    in_cols = w_in.shape[2]
    gate_cols = gla_gate_w.shape[2]

    b_blk = lax.dynamic_slice(b_ada, (0, chip * ada_cols), (1, ada_cols))
    g0, _, mod, w_in_stack = _prologue(jnp.concatenate([_rows128(c), _rows128(gla_gate_w[0])], axis=0), w_ada[0],
                                       b_blk, _col_major(w_in.astype(WIRE_DTYPE)))
    c_all = g0[:, :8].reshape(N_DEV, D_MODEL)
    gate_w_full = jnp.concatenate([g0[2 * j, 8:16].reshape(GATE_RANK, gate_cols) for j in range(N_CHIP)], axis=1)
    wg_p = jnp.pad(_pad_heads(gate_w_full), ((0, HEAD_W - GATE_RANK), (0, 0)))
    bg_p = _pad_heads(gla_gate_b)
    w_in_pt = _pad_w_in_rows(w_in_stack).astype(MXU_DTYPE)
    w_in_p = jnp.transpose(w_in_pt)

    proj, u, w2_stack = _inproj_fwd(x2, mod, w_in_p, tm_in, [w_ff2[0].astype(WIRE_DTYPE)])
    rot_a, rot_b = _rotary_tables(seq)
    dm_t, qdec_t, kdec_t, chunk_decay = _decay_tables()
    tables = (rot_a, rot_b, dm_t, qdec_t, kdec_t, chunk_decay)
    mixed, rsave, ssave, w_out_stack, w1_stack = _mixer_fwd(
        proj, tables, wg_p, bg_p, ret_norm_w, gla_norm_w,
        [w_out[0].astype(WIRE_DTYPE), w_ff1[0].astype(WIRE_DTYPE)])
    w_out_full = w_out_stack.reshape(D_MODEL, D_MODEL).astype(MXU_DTYPE)
    w1_chunks = w1_stack.astype(MXU_DTYPE)
    w2_chunks = w2_stack.astype(MXU_DTYPE)

    ln_rows = jnp.concatenate([ln1_w, ln1_b, ln2_w, ln2_b], axis=0)
    dmixed, dxa, act, dh, u2, df, dm, sums2 = _mlp_fwd_bwd(x2, mixed, target, mod, ln_rows, w_out_full, w1_chunks,
                                                           w2_chunks, tm)

    g_out_stack = _grad_matmul(mixed, dm, "grad_w_out", D_MODEL, True)
    g_ff1_stack, r_out = _grad_matmul(u2, dh, "grad_w_ff1", D_FF // N_CHIP, False, [g_out_stack])
    g_ff2_stack = _grad_matmul(act, df, "grad_w_ff2", D_MODEL, True)
    dproj, d_ret_norm, d_gla_norm, d_wg_p, d_bg_p, r_ff1, r_ff2 = _mixer_bwd(
        proj, dmixed, rsave, ssave, tables, wg_p, bg_p, ret_norm_w, gla_norm_w, [g_ff1_stack, g_ff2_stack])
    early = ["w_out", "w_ff1", "w_ff2"]
    partial = dict(zip(early, [r_out, r_ff1, r_ff2]))
    g_in_t, *swapped_early = _grad_matmul_full(dproj, u, "grad_w_in", N_PROJ // 3, [partial[n] for n in early])
    swapped = dict(zip(early, swapped_early))
    g_in_stack = _unpad_w_in_stack(g_in_t, in_cols)
    grad_x, sums1, r_in = _inproj_bwd(dproj, x2, dxa, mod, w_in_pt, tm_in, [g_in_stack])

    sources = [sums1, sums2, d_ret_norm, _unpad_heads(d_bg_p), d_gla_norm, _unpad_heads(d_wg_p[:GATE_RANK])]
    pieces = [(0, 0, 0), (0, 1, 8), (1, S_GATE1, 16), (1, S_SHIFT2, 24), (1, S_SCALE2, 32), (1, S_GATE2, 40),
              (1, S_LN1W, 48), (1, S_LN1B, 56), (1, S_LN2W, 64), (1, S_LN2B, 72), (2, 0, 80), (3, 0, 88), (4, 0, 96)]
    pieces += [(5, r, 104 + 2 * r) for r in range(GATE_RANK)] + [(1, S_LOSS, 136)]
    partial["w_in"] = r_in
    g2, swapped["w_in"] = _gather_rows(sources, pieces, 144, "gather_small", [r_in])
    (grad_b_ada, grad_ln1_w, grad_ln1_b, grad_ln2_w, grad_ln2_b, grad_ret_norm, grad_gate_b, grad_gla_norm,
     grad_gate_w_full, loss_sum) = _sum_devices(g2, [
         (0, (1, 6 * D_MODEL)), (48, (1, D_MODEL)), (56, (1, D_MODEL)), (64, (1, D_MODEL)), (72, (1, D_MODEL)),
         (80, (1, 512)), (88, (1, 256)), (96, (1, 512)), (104, (GATE_RANK, 256))])
    loss = 0.5 / D_MODEL * loss_sum[0, 0]
    grad_gate_w = lax.dynamic_slice(grad_gate_w_full, (0, chip * gate_cols), (GATE_RANK, gate_cols))

    small_grads = [grad_b_ada, grad_ln1_w, grad_ln1_b, grad_ln2_w, grad_ln2_b, grad_ret_norm, grad_gate_b,
                   grad_gla_norm, grad_gate_w[None]]
    small_out = _adam_small(list(zip(
        [b_ada, ln1_w, ln1_b, ln2_w, ln2_b, ret_norm_w, gla_gate_b, gla_norm_w, gla_gate_w], small_grads,
        [m_b_ada, m_ln1_w, m_ln1_b, m_ln2_w, m_ln2_b, m_ret_norm_w, m_gla_gate_b, m_gla_norm_w, m_gla_gate_w],
        [v_b_ada, v_ln1_w, v_ln1_b, v_ln2_w, v_ln2_b, v_ret_norm_w, v_gla_gate_b, v_gla_norm_w, v_gla_gate_w])))
    sm_delta, sm_m, sm_v = [[o[k] for o in small_out] for k in range(3)]

    dmod_all = g2[:, 0:48].reshape(N_DEV, 6 * D_MODEL)
    dmod_blk = lax.dynamic_slice(dmod_all, (0, chip * ada_cols), (N_DEV, ada_cols))
    ada_out = _ada_bwd_adam(jnp.transpose(c_all), dmod_blk, w_ada[0], m_w_ada[0], v_w_ada[0])
    ada_g, ada_delta, ada_m, ada_v = [t[None] for t in ada_out]

    big = {}
    for n, w, m, v in zip(["w_in", "w_out", "w_ff1", "w_ff2"], [w_in, w_out, w_ff1, w_ff2],
                          [m_w_in, m_w_out, m_w_ff1, m_w_ff2], [v_w_in, v_w_out, v_w_ff1, v_w_ff2]):
        mine, theirs = partial[n], swapped[n]
        if n == "w_in":
            out = _adam_pair(_col_major(w), mine, theirs, _col_major(m), _col_major(v), "adam_" + n)
            big[n] = [jnp.transpose(t.reshape(t.shape[0], 1, t.shape[1]), (1, 2, 0)) for t in out]
        else:
            big[n] = [t[None] for t in _adam_pair(w[0], mine, theirs, m[0], v[0], "adam_" + n)]

    def assemble(ada, smalls, k):
        b_ada_o, ln1w_o, ln1b_o, ln2w_o, ln2b_o, ret_o, gb_o, gln_o, gw_o = smalls
        return [ada, b_ada_o, big["w_in"][k], ret_o, gw_o, gb_o, gln_o, big["w_out"][k], ln1w_o, ln1b_o,
                big["w_ff1"][k], big["w_ff2"][k], ln2w_o, ln2b_o]

    grads = assemble(ada_g, small_grads, 0)
    deltas = assemble(ada_delta, sm_delta, 1)
    new_m = assemble(ada_m, sm_m, 2)
    new_v = assemble(ada_v, sm_v, 3)
    return (loss, grad_x[None], *grads, *deltas, *new_m, *new_v)
```

```python
import numpy as np
import jax
import jax.numpy as jnp
from jax import lax
from jax.experimental import pallas as pl
from jax.experimental.pallas import tpu as pltpu

D_MODEL = 1024
D_FF = 4096
CHUNK = 64
N_HEADS = 4
HEAD_W = 128
GLA_DK = 64
GATE_RANK = 16
GATE_TAU = 16.0
LN_EPS = 1e-5
ALPHA = 2.0 ** 0.25
ROPE_BASE = 10000.0
RET_SCALE = float(HEAD_W) ** -0.5
GLA_SCALE = float(GLA_DK) ** -0.5

ADAM_LR = 0.001
ADAM_B1 = 0.9
ADAM_B2 = 0.999
ADAM_EPS = 1e-08
ADAM_WD = 0.01
ADAM_STEP = 10

OFF_RQ, OFF_RK, OFF_RV, OFF_RG = 0, 512, 1024, 1536
OFF_GQ, OFF_GK, OFF_GV, OFF_GG, OFF_LR = 2048, 2560, 3072, 3584, 4096
N_PROJ = 4224

N_DEV = 8
N_CHIP = 4
MESH = pl.DeviceIdType.MESH
MXU_DTYPE = jnp.bfloat16
WIRE_DTYPE = jnp.bfloat16
VMEM_LIMIT = 60 * 1024 * 1024
TOKEN_TILE = 256
INPROJ_TOKEN_TILE = 512
CHUNKS_PER_STEP = 8
CHUNKS_IN_LOCKSTEP = 4
GRAD_ROWS_PER_STEP = 1024
GRAD_TOKEN_TILE = 2048
ELEMENTWISE_COLS = 512
HIGHEST = lax.Precision.HIGHEST


def _mm(a, b):
    return jnp.dot(a.astype(MXU_DTYPE), b.astype(MXU_DTYPE), preferred_element_type=jnp.float32)


def _mm_nt(a, b):
    return lax.dot_general(a.astype(MXU_DTYPE), b.astype(MXU_DTYPE), (((1,), (1,)), ((), ())),
                           preferred_element_type=jnp.float32)


def _mm_tn(a, b):
    return lax.dot_general(a.astype(MXU_DTYPE), b.astype(MXU_DTYPE), (((0,), (0,)), ((), ())),
                           preferred_element_type=jnp.float32)


def _mm32(a, b):
    return jnp.dot(a, b, precision=HIGHEST, preferred_element_type=jnp.float32)


def _running_sum(mask, a):
    m = mask.astype(jnp.bfloat16)
    hi = a.astype(jnp.bfloat16)
    rest = a - hi.astype(jnp.float32)
    mid = rest.astype(jnp.bfloat16)
    lo = (rest - mid.astype(jnp.float32)).astype(jnp.bfloat16)
    dot = lambda t: jnp.dot(m, t, preferred_element_type=jnp.float32)
    return dot(hi) + dot(mid) + dot(lo)


def _rowmean(a):
    return jnp.mean(a, axis=-1, keepdims=True)


def _colsum(a):
    return jnp.sum(a, axis=0, keepdims=True)


def _ln(z):
    zc = z - _rowmean(z)
    rstd = lax.rsqrt(_rowmean(zc * zc) + LN_EPS)
    return zc * rstd, rstd


def _ln_bwd(dzh, zh, rstd):
    return rstd * (dzh - _rowmean(dzh) - zh * _rowmean(dzh * zh))


def _sigmoid(a):
    return 1.0 / (1.0 + jnp.exp(-a))


def _log_sigmoid(a):
    return jnp.minimum(a, 0.0) - jnp.log(1.0 + jnp.exp(-jnp.abs(a)))


def _swap_halves(a):
    return pltpu.roll(a, HEAD_W // 2, 1)


def _tri_masks():
    row = lax.broadcasted_iota(jnp.int32, (CHUNK, CHUNK), 0)
    col = lax.broadcasted_iota(jnp.int32, (CHUNK, CHUNK), 1)
    return row, col


def _const_spec(shape):
    zeros = (0,) * len(shape)
    return pl.BlockSpec(shape, lambda *_: zeros, pipeline_mode=pl.Buffered(1))


def _params(semantics):
    return pltpu.CompilerParams(dimension_semantics=semantics, vmem_limit_bytes=VMEM_LIMIT)


def _decay_tables():
    log_gamma = np.log(1.0 - 2.0 ** (-5.0 - np.arange(N_HEADS, dtype=np.float64)))
    idx = np.arange(CHUNK, dtype=np.float64)
    dist = np.abs(idx[:, None] - idx[None, :])
    intra = np.exp(log_gamma[:, None, None] * dist)
    kdec = np.exp(log_gamma[None, :] * (CHUNK - 1.0 - idx)[:, None])
    qdec = np.exp(log_gamma[None, :] * (idx + 1.0)[:, None])
    chunk_decay = np.exp(log_gamma * CHUNK)
    lanes = lambda t: np.repeat(t, HEAD_W, axis=1).astype(np.float32)
    return (jnp.asarray(intra.astype(np.float32)), jnp.asarray(lanes(qdec)), jnp.asarray(lanes(kdec)),
            [float(np.float32(v)) for v in chunk_decay])


def _rotary_tables(seq):
    half = HEAD_W // 2
    inv = (np.float32(1.0) / np.float32(ROPE_BASE) ** np.linspace(0.0, 1.0, half, dtype=np.float32)).astype(np.float32)
    both = lambda t: np.concatenate([t, t], axis=-1)
    ang_a = np.arange(0, seq, CHUNK, dtype=np.float32)[:, None] * inv[None, :]
    rot_a = np.stack([both(np.cos(ang_a)), both(np.sin(ang_a))], axis=1)
    rot_a = np.pad(rot_a, ((0, 0), (0, 6), (0, 0))).astype(np.float32)
    ang_b = np.arange(CHUNK, dtype=np.float32)[:, None] * inv[None, :]
    cos_b, sin_b = both(np.cos(ang_b)), both(np.sin(ang_b))
    sign = np.concatenate([-np.ones((half,), np.float32), np.ones((half,), np.float32)])
    rot_b = np.stack([cos_b, sin_b, cos_b * sign, sin_b * sign]).astype(np.float32)
    return jnp.asarray(rot_a), jnp.asarray(rot_b)


def _rotary_chunk(ra_ref, c, rb_ref):
    cos_a, sin_a = ra_ref[c, 0:1, :], ra_ref[c, 1:2, :]
    return cos_a * rb_ref[0] - sin_a * rb_ref[1], sin_a * rb_ref[2] + cos_a * rb_ref[3]


def _mesh_pos():
    return lax.axis_index("x"), lax.axis_index("y"), lax.axis_index("c")


def _flip(v, bit):
    return 1 - v if bit else v


def _gather_rows(sources, pieces, rows, name, swaps):
    n_src, n = len(sources), len(swaps)

    def body(*refs):
        src_refs, refs = refs[:n_src], refs[n_src:]
        out_ref = refs[n]
        v_sc = refs[1 + 2 * n]
        swap = _SiblingSwap(refs[:n], refs[1 + n:1 + 2 * n], refs[4 + 2 * n:])
        swap.start()
        v_sc[...] = jnp.zeros_like(v_sc)
        for s, row, first in pieces:
            for k in range(src_refs[s].shape[1] // 128):
                v_sc[first + k:first + k + 1, :] = src_refs[s][row:row + 1, k * 128:(k + 1) * 128]
        _all_devices_exchange(v_sc, out_ref, refs[2 + 2 * n], refs[3 + 2 * n])
        swap.wait()

    hbm = pl.BlockSpec(memory_space=pl.ANY)
    vmem = pl.BlockSpec(memory_space=pltpu.VMEM)
    return pl.pallas_call(
        body, name=name,
        out_shape=(jax.ShapeDtypeStruct((N_DEV, rows, 128), jnp.float32),)
        + tuple(jax.ShapeDtypeStruct(a.shape, a.dtype) for a in swaps),
        in_specs=[vmem] * n_src + [hbm] * n,
        out_specs=(vmem,) + (hbm,) * n,
        scratch_shapes=[pltpu.VMEM((rows, 128), jnp.float32)] + _all_devices_sems() + _swap_sems(n),
    )(*sources, *swaps)


def _all_devices_sems():
    return [pltpu.SemaphoreType.DMA((N_DEV - 1,)), pltpu.SemaphoreType.DMA((N_DEV - 1,))]


def _all_devices_exchange(v_ref, out_ref, send_sems, recv_sems):
    x, y, c = _mesh_pos()
    me = 4 * x + 2 * y + c
    out_ref[me] = v_ref[...]
    sends, recvs = [], []
    for k in range(1, N_DEV):
        px, py, pc = _flip(x, (k >> 2) & 1), _flip(y, (k >> 1) & 1), _flip(c, k & 1)
        peer = 4 * px + 2 * py + pc
        sends.append(pltpu.make_async_remote_copy(
            src_ref=v_ref, dst_ref=out_ref.at[me], send_sem=send_sems.at[k - 1], recv_sem=recv_sems.at[k - 1],
            device_id=(px, py, pc), device_id_type=MESH))
        recvs.append(pltpu.make_async_remote_copy(
            src_ref=v_ref, dst_ref=out_ref.at[peer], send_sem=send_sems.at[k - 1], recv_sem=recv_sems.at[k - 1],
            device_id=(px, py, pc), device_id_type=MESH))
    for cp in sends:
        cp.start()
    for cp in recvs:
        cp.wait_recv()
    for cp in sends:
        cp.wait_send()


def _prologue(cond_rows, w_ada_blk, b_blk, w_in_t):
    cols = w_ada_blk.shape[1]
    groups = cols // 128
    c_rows = D_MODEL // 128

    def body(cond_ref, w_ref, b_ref, win_ref, cond_all_ref, mod_all_ref, mod_ref, stack_ref, mod_sc, *sems):
        gather = _ChipGather([win_ref], [stack_ref], sems[:5])
        gather.start()
        _all_devices_exchange(cond_ref, cond_all_ref, sems[5], sems[6])
        acc = jnp.broadcast_to(b_ref[...], (N_DEV, cols))
        for r in range(c_rows):
            cv = cond_all_ref[:, r, :]
            acc = acc + _mm32(cv * _sigmoid(cv), w_ref[r * 128:(r + 1) * 128, :])
        for k in range(groups):
            mod_sc[k] = acc[:, k * 128:(k + 1) * 128]
        _all_devices_exchange(mod_sc, mod_all_ref, sems[7], sems[8])
        x, y, c = _mesh_pos()
        me = 4 * x + 2 * y + c
        for j in range(N_CHIP):
            for k in range(groups):
                lane = j * cols + k * 128
                mod_ref[:, lane:lane + 128] = mod_all_ref[2 * j, k, pl.ds(me, 1), :]
        gather.forward()
        gather.finish()

    vmem = pl.BlockSpec(memory_space=pltpu.VMEM)
    hbm = pl.BlockSpec(memory_space=pl.ANY)
    return pl.pallas_call(
        body, name="prologue",
        out_shape=(jax.ShapeDtypeStruct((N_DEV,) + cond_rows.shape, jnp.float32),
                   jax.ShapeDtypeStruct((N_DEV, groups, N_DEV, 128), jnp.float32),
                   jax.ShapeDtypeStruct((1, N_CHIP * cols), jnp.float32))
        + _exchange_out_shapes([w_in_t], True),
        in_specs=[vmem, vmem, vmem, hbm],
        out_specs=(vmem, vmem, vmem, hbm),
        scratch_shapes=[pltpu.VMEM((groups, N_DEV, 128), jnp.float32)] + _gather_sems(1)
        + _all_devices_sems() + _all_devices_sems(),
        compiler_params=pltpu.CompilerParams(vmem_limit_bytes=VMEM_LIMIT),
    )(cond_rows, w_ada_blk, b_blk, w_in_t)


def _exchange_out_shapes(arrays, gather):
    return tuple(jax.ShapeDtypeStruct((N_CHIP,) + a.shape if gather else a.shape, a.dtype) for a in arrays)


def _scatter_sems(n):
    n_sem = n * (N_CHIP - 1)
    return [pltpu.SemaphoreType.DMA((n_sem,)), pltpu.SemaphoreType.DMA((n_sem,)), pltpu.SemaphoreType.DMA((n,))]


def _gather_sems(n):
    n_sem = n * (N_CHIP - 1)
    return [pltpu.SemaphoreType.DMA((n_sem,))] * 4 + [pltpu.SemaphoreType.DMA((n,))]


def _peer_chips(x, y):
    out = []
    for k in range(1, N_CHIP):
        px, py = _flip(x, (k >> 1) & 1), _flip(y, k & 1)
        out.append((px, py, 2 * px + py))
    return out


class _ChipScatter:
    def __init__(self, ins, outs, sems):
        send_sems, recv_sems, local_sems = sems
        x, y, c = _mesh_pos()
        chip = 2 * x + y
        self.local, self.sends, self.recvs = [], [], []
        for i in range(len(ins)):
            self.local.append(pltpu.make_async_copy(ins[i].at[chip], outs[i].at[chip], local_sems.at[i]))
            for k, (px, py, peer_chip) in enumerate(_peer_chips(x, y)):
                sem = i * (N_CHIP - 1) + k
                src = ins[i].at[peer_chip]
                self.sends.append(pltpu.make_async_remote_copy(
                    src_ref=src, dst_ref=outs[i].at[chip], send_sem=send_sems.at[sem], recv_sem=recv_sems.at[sem],
                    device_id=(px, py, c), device_id_type=MESH))
                self.recvs.append(pltpu.make_async_remote_copy(
                    src_ref=src, dst_ref=outs[i].at[peer_chip], send_sem=send_sems.at[sem], recv_sem=recv_sems.at[sem],
                    device_id=(px, py, c), device_id_type=MESH))

    def start(self):
        for cp in self.local + self.sends:
            cp.start()

    def wait(self):
        for cp in self.recvs:
            cp.wait_recv()
        for cp in self.sends:
            cp.wait_send()
        for cp in self.local:
            cp.wait()


class _ChipGather:
    def __init__(self, ins, outs, sems):
        ici_send, ici_recv, d2d_send, d2d_recv, local_sems = sems
        x, y, c = _mesh_pos()
        chip = 2 * x + y
        self.local, self.ici_sends, self.ici_recvs, self.d2d_sends, self.d2d_recvs = [], [], [], [], []
        for i in range(len(ins)):
            half = ins[i].shape[-1] // 2
            assert half % 128 == 0
            lead = (slice(None),) * (len(ins[i].shape) - 1)
            mine = lead + (pl.ds(pl.multiple_of(c * half, 128), half),)
            theirs = lead + (pl.ds(pl.multiple_of((1 - c) * half, 128), half),)
            self.local.append(pltpu.make_async_copy(ins[i], outs[i].at[chip], local_sems.at[i]))
            for k, (px, py, peer_chip) in enumerate(_peer_chips(x, y)):
                sem = i * (N_CHIP - 1) + k
                self.ici_sends.append(pltpu.make_async_remote_copy(
                    src_ref=ins[i].at[mine], dst_ref=outs[i].at[chip].at[mine],
                    send_sem=ici_send.at[sem], recv_sem=ici_recv.at[sem], device_id=(px, py, c), device_id_type=MESH))
                landed = outs[i].at[peer_chip].at[mine]
                self.ici_recvs.append(pltpu.make_async_remote_copy(
                    src_ref=ins[i].at[mine], dst_ref=landed,
                    send_sem=ici_send.at[sem], recv_sem=ici_recv.at[sem], device_id=(px, py, c), device_id_type=MESH))
                self.d2d_sends.append(pltpu.make_async_remote_copy(
                    src_ref=landed, dst_ref=landed,
                    send_sem=d2d_send.at[sem], recv_sem=d2d_recv.at[sem], device_id=(x, y, 1 - c), device_id_type=MESH))
                self.d2d_recvs.append(pltpu.make_async_remote_copy(
                    src_ref=landed, dst_ref=outs[i].at[peer_chip].at[theirs],
                    send_sem=d2d_send.at[sem], recv_sem=d2d_recv.at[sem], device_id=(x, y, 1 - c), device_id_type=MESH))

    def start(self):
        for cp in self.local + self.ici_sends:
            cp.start()

    def forward(self):
        for landed, onward in zip(self.ici_recvs, self.d2d_sends):
            landed.wait_recv()
            onward.start()

    def finish(self):
        for cp in self.d2d_recvs:
            cp.wait_recv()
        for cp in self.d2d_sends + self.ici_sends:
            cp.wait_send()
        for cp in self.local:
            cp.wait()


def _swap_sems(n):
    return [pltpu.SemaphoreType.DMA((n,)), pltpu.SemaphoreType.DMA((n,))]


class _SiblingSwap:
    def __init__(self, ins, outs, sems):
        send_sems, recv_sems = sems
        x, y, c = _mesh_pos()
        self.copies = [pltpu.make_async_remote_copy(
            src_ref=ins[i], dst_ref=outs[i], send_sem=send_sems.at[i], recv_sem=recv_sems.at[i],
            device_id=(x, y, 1 - c), device_id_type=MESH) for i in range(len(ins))]

    def start(self):
        for cp in self.copies:
            cp.start()

    def wait(self):
        for cp in self.copies:
            cp.wait_recv()
        for cp in self.copies:
            cp.wait_send()


def _adam(w, g, m, v):
    m2 = ADAM_B1 * m + (1.0 - ADAM_B1) * g
    v2 = ADAM_B2 * v + (1.0 - ADAM_B2) * (g * g)
    m_hat = m2 / (1.0 - ADAM_B1 ** ADAM_STEP)
    v_hat = v2 / (1.0 - ADAM_B2 ** ADAM_STEP)
    delta = -ADAM_LR * (m_hat / (jnp.sqrt(v_hat) + ADAM_EPS) + ADAM_WD * w)
    return delta, m2, v2


def _ada_bwd_adam(c_t, dmod_blk, w, m, v):
    rows, cols = w.shape
    tile = 512
    assert cols % tile == 0

    def body(c_ref, d_ref, w_ref, m_ref, v_ref, g_ref, dl_ref, m2_ref, v2_ref):
        sc = c_ref[...]
        sc = sc * _sigmoid(sc)
        dm = d_ref[...]
        g = sc[:, 0:1] * dm[0:1, :]
        for b in range(1, N_DEV):
            g = g + sc[:, b:b + 1] * dm[b:b + 1, :]
        delta, m2, v2 = _adam(w_ref[...], g, m_ref[...], v_ref[...])
        g_ref[...] = g
        dl_ref[...] = delta
        m2_ref[...] = m2
        v2_ref[...] = v2

    blk = pl.BlockSpec((rows, tile), lambda j: (0, j))
    out = jax.ShapeDtypeStruct((rows, cols), jnp.float32)
    return pl.pallas_call(
        body, name="ada_bwd_adam", grid=(cols // tile,),
        out_shape=(out, out, out, out),
        in_specs=[pl.BlockSpec((rows, N_DEV), lambda j: (0, 0)), pl.BlockSpec((N_DEV, tile), lambda j: (0, j)),
                  blk, blk, blk],
        out_specs=(blk, blk, blk, blk),
        compiler_params=_params(("arbitrary",)),
    )(c_t, dmod_blk, w, m, v)


MOD_SHIFT1, MOD_SCALE1, MOD_GATE1, MOD_SHIFT2, MOD_SCALE2, MOD_GATE2 = range(6)


def _mod(mod_ref, segment):
    return mod_ref[:, segment * D_MODEL:(segment + 1) * D_MODEL]


def _inproj_fwd(x2, vecs, w_in_p, tm, riders):
    seq = x2.shape[0]
    n_tiles = seq // tm
    n_ride = len(riders)

    def body(*refs):
        x_ref, vec_ref, w_ref = refs[:3]
        ride_in, refs = refs[3:3 + n_ride], refs[3 + n_ride:]
        p_ref, u_ref = refs[:2]
        ride_out, sems = refs[2:2 + n_ride], refs[2 + n_ride:]
        gather = _ChipGather(ride_in, ride_out, sems)

        @pl.when(pl.program_id(0) == 0)
        def _():
            gather.start()

        xh, _ = _ln(x_ref[...])
        u = (xh * (1.0 + _mod(vec_ref, MOD_SCALE1)) + _mod(vec_ref, MOD_SHIFT1)).astype(MXU_DTYPE)
        u_ref[...] = u
        p_ref[...] = _mm(u, w_ref[...]).astype(p_ref.dtype)

        @pl.when(pl.program_id(0) == (3 * n_tiles) // 4)
        def _():
            gather.forward()

        @pl.when(pl.program_id(0) == n_tiles - 1)
        def _():
            gather.finish()

    hbm = pl.BlockSpec(memory_space=pl.ANY)
    return pl.pallas_call(
        body, name="inproj_fwd", grid=(n_tiles,),
        out_shape=(jax.ShapeDtypeStruct((seq, N_PROJ), MXU_DTYPE), jax.ShapeDtypeStruct((seq, D_MODEL), MXU_DTYPE))
        + _exchange_out_shapes(riders, True),
        in_specs=[pl.BlockSpec((tm, D_MODEL), lambda i: (i, 0)), _const_spec(vecs.shape), _const_spec(w_in_p.shape)]
        + [hbm] * n_ride,
        out_specs=(pl.BlockSpec((tm, N_PROJ), lambda i: (i, 0)), pl.BlockSpec((tm, D_MODEL), lambda i: (i, 0)))
        + (hbm,) * n_ride,
        scratch_shapes=_gather_sems(n_ride),
        compiler_params=_params(("arbitrary",)),
    )(x2, vecs, w_in_p, *riders)


def _inproj_bwd(dproj, x2, dxa, vecs, w_in_pt, tm, riders):
    seq = x2.shape[0]
    n_tiles = seq // tm
    n_ride = len(riders)

    def body(*refs):
        dp_ref, x_ref, dxa_ref, vec_ref, w_ref = refs[:5]
        ride_in, refs = refs[5:5 + n_ride], refs[5 + n_ride:]
        gx_ref, sums_ref = refs[:2]
        ride_out, sems = refs[2:2 + n_ride], refs[2 + n_ride:]
        exchange = _ChipScatter(ride_in, ride_out, sems)

        @pl.when(pl.program_id(0) == 0)
        def _():
            exchange.start()
            sums_ref[...] = jnp.zeros_like(sums_ref)

        du = _mm(dp_ref[...], w_ref[...])
        xh, rstd = _ln(x_ref[...])
        sums_ref[0:1, :] += _colsum(du)
        sums_ref[1:2, :] += _colsum(du * xh)
        gx_ref[...] = dxa_ref[...] + _ln_bwd(du * (1.0 + _mod(vec_ref, MOD_SCALE1)), xh, rstd)

        @pl.when(pl.program_id(0) == n_tiles - 1)
        def _():
            exchange.wait()

    tile = pl.BlockSpec((tm, D_MODEL), lambda i: (i, 0))
    hbm = pl.BlockSpec(memory_space=pl.ANY)
    return pl.pallas_call(
        body, name="inproj_bwd", grid=(n_tiles,),
        out_shape=(jax.ShapeDtypeStruct((seq, D_MODEL), jnp.float32), jax.ShapeDtypeStruct((8, D_MODEL), jnp.float32))
        + _exchange_out_shapes(riders, False),
        in_specs=[pl.BlockSpec((tm, N_PROJ), lambda i: (i, 0)), tile, tile, _const_spec(vecs.shape),
                  _const_spec(w_in_pt.shape)] + [hbm] * n_ride,
        out_specs=(tile, pl.BlockSpec((8, D_MODEL), lambda i: (0, 0))) + (hbm,) * n_ride,
        scratch_shapes=_scatter_sems(n_ride),
        compiler_params=_params(("arbitrary",)),
    )(dproj, x2, dxa, vecs, w_in_pt, *riders)


def _head(h):
    return slice(h * HEAD_W, (h + 1) * HEAD_W)


def _cols(ref, off, h):
    return ref[:, off + h * HEAD_W:off + (h + 1) * HEAD_W].astype(jnp.float32)


HEADS = range(N_HEADS)


def _mixer_chunk_forward(p_ref, cc, ss, dm_ref, qdec_ref, kdec_ref, wg_ref, bg_ref, states):
    row, col = _tri_masks()
    lower = row >= col
    f = {}
    f["glr"] = p_ref[:, OFF_LR:OFF_LR + HEAD_W]
    f["logit"] = _mm(f["glr"], wg_ref[...]) + bg_ref[...]
    rq = [_cols(p_ref, OFF_RQ, h) for h in HEADS]
    rk = [_cols(p_ref, OFF_RK, h) for h in HEADS]
    f["rv"] = [_cols(p_ref, OFF_RV, h) for h in HEADS]
    f["qr"] = [(rq[h] * cc + _swap_halves(rq[h]) * ss) * RET_SCALE for h in HEADS]
    f["kr"] = [rk[h] * cc + _swap_halves(rk[h]) * ss for h in HEADS]
    s_raw = [_mm_nt(f["qr"][h], f["kr"][h]) for h in HEADS]
    yield
    la = _log_sigmoid(f["logit"]) * (1.0 / GATE_TAU)
    b = _running_sum(lower, la)
    f["qd"] = [f["qr"][h] * qdec_ref[:, _head(h)] for h in HEADS]
    f["kd"] = [f["kr"][h] * kdec_ref[:, _head(h)] for h in HEADS]
    f["scores"] = [s_raw[h] * dm_ref[h] for h in HEADS]
    yield
    b_last = b[CHUNK - 1:CHUNK, :]
    b_mid = b[CHUNK // 2 - 1:CHUNK // 2, :]
    f["e"], f["ei"] = jnp.exp(b - b_mid), jnp.exp(b_mid - b)
    f["eb"], f["ek"], f["ebl"] = jnp.exp(b), jnp.exp(b_last - b), jnp.exp(b_last)
    gq = [_cols(p_ref, OFF_GQ, h) for h in HEADS]
    gk = [_cols(p_ref, OFF_GK, h) for h in HEADS]
    f["gv"] = [_cols(p_ref, OFF_GV, h) for h in HEADS]
    f["q_e"] = [gq[h] * f["e"][:, _head(h)] for h in HEADS]
    f["q_i"] = [gq[h] * f["ei"][:, _head(h)] for h in HEADS]
    f["k_e"] = [gk[h] * f["e"][:, _head(h)] for h in HEADS]
    f["k_i"] = [gk[h] * f["ei"][:, _head(h)] for h in HEADS]
    low = [_mm_nt(f["q_e"][h], f["k_i"][h]) for h in HEADS]
    up = [_mm_nt(f["q_i"][h], f["k_e"][h]) for h in HEADS]
    yield
    f["att"] = [jnp.where(lower, low[h], up[h]) for h in HEADS]
    f["qb"] = [gq[h] * f["eb"][:, _head(h)] for h in HEADS]
    f["kb"] = [gk[h] * f["ek"][:, _head(h)] for h in HEADS]
    ret_state, gla_state_t = states()
    f["o_ret"] = [_mm(f["scores"][h], f["rv"][h]) + _mm(f["qd"][h], ret_state[h]) for h in HEADS]
    f["o_gla"] = [_mm(f["att"][h], f["gv"][h]) + _mm_nt(f["qb"][h], gla_state_t[h]) for h in HEADS]
    return f


def _interleave(generators):
    live = list(generators)
    while live:
        for g in list(live):
            try:
                next(g)
            except StopIteration:
                live.remove(g)


def _mixer_fwd(proj, tables, wg_p, bg_p, ret_norm_w, gla_norm_w, riders):
    seq = proj.shape[0]
    n_chunks = seq // CHUNK
    per_step = min(n_chunks, CHUNKS_PER_STEP)
    n_steps = n_chunks // per_step
    n_ride = len(riders)
    rot_a, rot_b, dm_t, qdec_t, kdec_t, chunk_decay = tables

    def body(*refs):
        p_ref, ra_ref, rb_ref, dm_ref, qdec_ref, kdec_ref, wg_ref, bg_ref, wr_ref, wl_ref = refs[:10]
        ride_in, refs = refs[10:10 + n_ride], refs[10 + n_ride:]
        mix_ref, rsave_ref, ssave_ref = refs[:3]
        ride_out, refs = refs[3:3 + n_ride], refs[3 + n_ride:]
        r_sc, s_sc = refs[:2]
        gather = _ChipGather(ride_in, ride_out, refs[2:])

        @pl.when(pl.program_id(0) == 0)
        def _():
            gather.start()
            r_sc[...] = jnp.zeros_like(r_sc)
            s_sc[...] = jnp.zeros_like(s_sc)

        def one_chunk(c):
            p_c = p_ref.at[c * CHUNK:(c + 1) * CHUNK, :]
            mix_c = mix_ref.at[c * CHUNK:(c + 1) * CHUNK, :]
            before = {}

            def states():
                before["ret"] = [r_sc[h] for h in HEADS]
                before["gla"] = [s_sc[h] for h in HEADS]
                for h in HEADS:
                    rsave_ref[c, h] = before["ret"][h].astype(rsave_ref.dtype)
                    ssave_ref[c, h] = before["gla"][h]
                return before["ret"], before["gla"]

            cc, ss = _rotary_chunk(ra_ref, c, rb_ref)
            f = yield from _mixer_chunk_forward(p_c, cc, ss, dm_ref, qdec_ref, kdec_ref, wg_ref, bg_ref, states)
            for h in HEADS:
                r_sc[h] = chunk_decay[h] * before["ret"][h] + _mm_tn(f["kd"][h], f["rv"][h])
            for h in HEADS:
                s_sc[h] = before["gla"][h] * f["ebl"][:, _head(h)] + _mm_tn(f["gv"][h], f["kb"][h])
            yield
            for h in HEADS:
                on, _ = _ln(f["o_ret"][h])
                g = _cols(p_c, OFF_RG, h)
                mix_c[:, _head(h)] = (on * wr_ref[:, _head(h)] * (g * _sigmoid(g))).astype(mix_ref.dtype)
            for h in HEADS:
                o = f["o_gla"][h]
                on = o * lax.rsqrt(_rowmean(o * o) + LN_EPS)
                g = _cols(p_c, OFF_GG, h)
                mix_c[:, _head(N_HEADS + h)] = (on * wl_ref[:, _head(h)] * (g * _sigmoid(g))).astype(mix_ref.dtype)

        for c0 in range(0, per_step, CHUNKS_IN_LOCKSTEP):
            _interleave([one_chunk(c) for c in range(c0, min(per_step, c0 + CHUNKS_IN_LOCKSTEP))])

        @pl.when(pl.program_id(0) == (3 * n_steps) // 4)
        def _():
            gather.forward()

        @pl.when(pl.program_id(0) == n_steps - 1)
        def _():
            gather.finish()

    state_shape = (n_chunks, N_HEADS, HEAD_W, HEAD_W)
    state_blk = pl.BlockSpec((per_step, N_HEADS, HEAD_W, HEAD_W), lambda i: (i, 0, 0, 0))
    rot_blk = pl.BlockSpec((per_step, 8, HEAD_W), lambda i: (i, 0, 0))
    rows = per_step * CHUNK
    hbm = pl.BlockSpec(memory_space=pl.ANY)
    return pl.pallas_call(
        body, name="mixer_fwd", grid=(n_steps,),
        out_shape=(jax.ShapeDtypeStruct((seq, D_MODEL), MXU_DTYPE),
                   jax.ShapeDtypeStruct(state_shape, MXU_DTYPE), jax.ShapeDtypeStruct(state_shape, jnp.float32))
        + _exchange_out_shapes(riders, True),
        in_specs=[pl.BlockSpec((rows, N_PROJ), lambda i: (i, 0)), rot_blk, _const_spec(rot_b.shape),
                  _const_spec(dm_t.shape), _const_spec(qdec_t.shape), _const_spec(kdec_t.shape),
                  _const_spec(wg_p.shape), _const_spec(bg_p.shape), _const_spec(ret_norm_w.shape),
                  _const_spec(gla_norm_w.shape)] + [hbm] * n_ride,
        out_specs=(pl.BlockSpec((rows, D_MODEL), lambda i: (i, 0)), state_blk, state_blk) + (hbm,) * n_ride,
        scratch_shapes=[pltpu.VMEM((N_HEADS, HEAD_W, HEAD_W), jnp.float32),
                        pltpu.VMEM((N_HEADS, HEAD_W, HEAD_W), jnp.float32)] + _gather_sems(n_ride),
        compiler_params=_params(("arbitrary",)),
    )(proj, rot_a, rot_b, dm_t, qdec_t, kdec_t, wg_p, bg_p, ret_norm_w, gla_norm_w, *riders)


def _mixer_bwd(proj, dmixed, rsave, ssave, tables, wg_p, bg_p, ret_norm_w, gla_norm_w, riders):
    seq = proj.shape[0]
    n_chunks = seq // CHUNK
    per_step = min(n_chunks, CHUNKS_PER_STEP)
    n_steps = n_chunks // per_step
    n_ride = len(riders)
    rot_a, rot_b, dm_t, qdec_t, kdec_t, chunk_decay = tables
    last = n_steps - 1

    def body(*refs):
        p_blk, dmx_blk = refs[:2]
        shared_in = refs[2:13]
        ride_in, refs = refs[13:13 + n_ride], refs[13 + n_ride:]
        dp_blk, dwr_ref, dwl_ref, dwg_ref, dbg_ref = refs[:5]
        ride_out, refs = refs[5:5 + n_ride], refs[5 + n_ride:]
        dr_sc, ds_sc = refs[:2]
        exchange = _ChipScatter(ride_in, ride_out, refs[2:])

        @pl.when(pl.program_id(0) == 0)
        def _():
            exchange.start()
            dr_sc[...] = jnp.zeros_like(dr_sc)
            ds_sc[...] = jnp.zeros_like(ds_sc)
            dwr_ref[...] = jnp.zeros_like(dwr_ref)
            dwl_ref[...] = jnp.zeros_like(dwl_ref)
            dwg_ref[...] = jnp.zeros_like(dwg_ref)
            dbg_ref[...] = jnp.zeros_like(dbg_ref)

        def chunk_stages(c):
            rows = slice(c * CHUNK, (c + 1) * CHUNK)
            return one_chunk(c, p_blk.at[rows, :], dmx_blk.at[rows, :], dp_blk.at[rows, :], *shared_in,
                             dwr_ref, dwl_ref, dwg_ref, dbg_ref, dr_sc, ds_sc)

        for c0 in range(per_step, 0, -CHUNKS_IN_LOCKSTEP):
            _interleave([chunk_stages(c) for c in reversed(range(max(0, c0 - CHUNKS_IN_LOCKSTEP), c0))])

        @pl.when(pl.program_id(0) == last)
        def _():
            exchange.wait()

    def one_chunk(c, p_ref, dmx_ref, dp_ref, rsave_ref, ssave_ref, ra_ref, rb_ref, dm_ref, qdec_ref, kdec_ref,
                  wg_ref, bg_ref, wr_ref, wl_ref, dwr_ref, dwl_ref, dwg_ref, dbg_ref, dr_sc, ds_sc):
        def put(off, h, val):
            dp_ref[:, off + h * HEAD_W:off + (h + 1) * HEAD_W] = val.astype(dp_ref.dtype)

        cc, ss = _rotary_chunk(ra_ref, c, rb_ref)
        row, col = _tri_masks()
        ret_state = [rsave_ref[c, h] for h in HEADS]
        gla_state_t = [ssave_ref[c, h] for h in HEADS]
        f = yield from _mixer_chunk_forward(p_ref, cc, ss, dm_ref, qdec_ref, kdec_ref, wg_ref, bg_ref,
                                            lambda: (ret_state, gla_state_t))
        yield

        do_ret, do_gla = [], []
        for h in HEADS:
            on, rstd = _ln(f["o_ret"][h])
            g = _cols(p_ref, OFF_RG, h)
            sg = _sigmoid(g)
            dy = dmx_ref[:, _head(h)].astype(jnp.float32)
            wr = wr_ref[:, _head(h)]
            dwr_ref[:, _head(h)] += _colsum(dy * on * (g * sg))
            put(OFF_RG, h, dy * on * wr * (sg * (1.0 + g * (1.0 - sg))))
            do_ret.append(_ln_bwd(dy * wr * (g * sg), on, rstd))
        for h in HEADS:
            o = f["o_gla"][h]
            rstd = lax.rsqrt(_rowmean(o * o) + LN_EPS)
            on = o * rstd
            g = _cols(p_ref, OFF_GG, h)
            sg = _sigmoid(g)
            dy = dmx_ref[:, _head(N_HEADS + h)].astype(jnp.float32)
            wl = wl_ref[:, _head(h)]
            dwl_ref[:, _head(h)] += _colsum(dy * on * (g * sg))
            put(OFF_GG, h, dy * on * wl * (sg * (1.0 + g * (1.0 - sg))))
            don = dy * wl * (g * sg)
            do_gla.append(rstd * (don - on * _rowmean(don * on)))

        yield

        d_ret_new = [dr_sc[h] for h in HEADS]
        d_gla_new = [ds_sc[h] for h in HEADS]
        ds_raw = [_mm_nt(do_ret[h], f["rv"][h]) * dm_ref[h] for h in HEADS]
        d_att = [_mm_nt(do_gla[h], f["gv"][h]) for h in HEADS]
        dq_state = [_mm_nt(do_ret[h], ret_state[h]) for h in HEADS]
        dk_state = [_mm_nt(f["rv"][h], d_ret_new[h]) for h in HEADS]
        dqb = [_mm(do_gla[h], gla_state_t[h]) for h in HEADS]
        dkb = [_mm(f["gv"][h], d_gla_new[h]) for h in HEADS]
        for h in HEADS:
            put(OFF_RV, h, _mm_tn(f["scores"][h], do_ret[h]) + _mm(f["kd"][h], d_ret_new[h]))
        for h in HEADS:
            put(OFF_GV, h, _mm_tn(f["att"][h], do_gla[h]) + _mm_nt(f["kb"][h], d_gla_new[h]))
        for h in HEADS:
            dr_sc[h] = chunk_decay[h] * d_ret_new[h] + _mm_tn(f["qd"][h], do_ret[h])
        for h in HEADS:
            ds_sc[h] = d_gla_new[h] * f["ebl"][:, _head(h)] + _mm_tn(do_gla[h], f["qb"][h])
        yield

        dqr = [_mm(ds_raw[h], f["kr"][h]) + dq_state[h] * qdec_ref[:, _head(h)] for h in HEADS]
        dkr = [_mm_tn(ds_raw[h], f["qr"][h]) + dk_state[h] * kdec_ref[:, _head(h)] for h in HEADS]
        d_low = [jnp.where(row >= col, d_att[h], 0.0) for h in HEADS]
        d_up = [jnp.where(row < col, d_att[h], 0.0) for h in HEADS]
        dq_e = [_mm(d_low[h], f["k_i"][h]) for h in HEADS]
        dk_i = [_mm_tn(d_low[h], f["q_e"][h]) for h in HEADS]
        dq_i = [_mm(d_up[h], f["k_e"][h]) for h in HEADS]
        dk_e = [_mm_tn(d_up[h], f["q_i"][h]) for h in HEADS]
        yield
        for h in HEADS:
            put(OFF_RQ, h, (dqr[h] * cc + _swap_halves(dqr[h] * ss)) * RET_SCALE)
            put(OFF_RK, h, dkr[h] * cc + _swap_halves(dkr[h] * ss))
        row_id = lax.broadcasted_iota(jnp.int32, (CHUNK, HEAD_W), 0)
        db_heads = []
        for h in HEADS:
            hs = _head(h)
            e, ei, eb, ek, ebl = f["e"][:, hs], f["ei"][:, hs], f["eb"][:, hs], f["ek"][:, hs], f["ebl"][:, hs]
            put(OFF_GQ, h, dq_e[h] * e + dq_i[h] * ei + dqb[h] * eb)
            put(OFF_GK, h, dk_e[h] * e + dk_i[h] * ei + dkb[h] * ek)
            db = (dq_e[h] * f["q_e"][h] - dq_i[h] * f["q_i"][h] + dk_e[h] * f["k_e"][h] - dk_i[h] * f["k_i"][h]
                  + dqb[h] * f["qb"][h] - dkb[h] * f["kb"][h])
            db_last = _colsum(dkb[h] * f["kb"][h]) + ebl * _colsum(gla_state_t[h] * d_gla_new[h])
            db_heads.append(db + jnp.where(row_id == CHUNK - 1, db_last, 0.0))
        db = jnp.concatenate(db_heads, axis=1)
        d_la = _running_sum(col >= row, db)
        d_logit = d_la * (1.0 / GATE_TAU) * (1.0 - _sigmoid(f["logit"]))
        put(OFF_LR, 0, _mm_nt(d_logit, wg_ref[...]))
        dwg_ref[...] += _mm_tn(f["glr"], d_logit)
        dbg_ref[...] += _colsum(d_logit)

    state_blk = pl.BlockSpec((per_step, N_HEADS, HEAD_W, HEAD_W), lambda i: (last - i, 0, 0, 0))
    rot_blk = pl.BlockSpec((per_step, 8, HEAD_W), lambda i: (last - i, 0, 0))
    width = N_HEADS * HEAD_W
    vec_out = pl.BlockSpec((1, width), lambda i: (0, 0))
    hbm = pl.BlockSpec(memory_space=pl.ANY)
    rows_blk = per_step * CHUNK
    return pl.pallas_call(
        body, name="mixer_bwd", grid=(n_steps,),
        out_shape=(jax.ShapeDtypeStruct((seq, N_PROJ), MXU_DTYPE),
                   jax.ShapeDtypeStruct((1, width), jnp.float32), jax.ShapeDtypeStruct((1, width), jnp.float32),
                   jax.ShapeDtypeStruct((HEAD_W, width), jnp.float32), jax.ShapeDtypeStruct((1, width), jnp.float32))
        + _exchange_out_shapes(riders, False),
        in_specs=[pl.BlockSpec((rows_blk, N_PROJ), lambda i: (last - i, 0)),
                  pl.BlockSpec((rows_blk, D_MODEL), lambda i: (last - i, 0)), state_blk, state_blk, rot_blk,
                  _const_spec(rot_b.shape),
                  _const_spec(dm_t.shape), _const_spec(qdec_t.shape), _const_spec(kdec_t.shape),
                  _const_spec(wg_p.shape), _const_spec(bg_p.shape), _const_spec(ret_norm_w.shape),
                  _const_spec(gla_norm_w.shape)] + [hbm] * n_ride,
        out_specs=(pl.BlockSpec((rows_blk, N_PROJ), lambda i: (last - i, 0)), vec_out, vec_out,
                   pl.BlockSpec((HEAD_W, width), lambda i: (0, 0)), vec_out) + (hbm,) * n_ride,
        scratch_shapes=[pltpu.VMEM((N_HEADS, HEAD_W, HEAD_W), jnp.float32),
                        pltpu.VMEM((N_HEADS, HEAD_W, HEAD_W), jnp.float32)] + _scatter_sems(n_ride),
        compiler_params=_params(("arbitrary",)),
    )(proj, dmixed, rsave, ssave, rot_a, rot_b, dm_t, qdec_t, kdec_t, wg_p, bg_p, ret_norm_w, gla_norm_w, *riders)


V_GATE1, V_SCALE2, V_SHIFT2, V_GATE2, V_LN1W, V_LN1B, V_LN2W, V_LN2B = range(8)
S_GATE1, S_SCALE2, S_SHIFT2, S_GATE2, S_LN1W, S_LN1B, S_LN2W, S_LN2B, S_LOSS = range(9)


def _mlp_fwd_bwd(x2, mixed, target, mod, ln_rows, w_out, w1_chunks, w2_chunks, tm):
    seq = x2.shape[0]
    n_fc, _, fc = w1_chunks.shape
    segment_of = {V_GATE1: MOD_GATE1, V_SCALE2: MOD_SCALE2, V_SHIFT2: MOD_SHIFT2, V_GATE2: MOD_GATE2}

    def body(x_ref, mx_ref, t_ref, mod_ref, ln_ref, wo_ref, w1_ref, w2_ref,
             dmx_ref, dxa_ref, a_ref, dh_ref, u2_ref, df_ref, dm_ref, sums_ref, relu_sc):
        @pl.when(pl.program_id(0) == 0)
        def _():
            sums_ref[...] = jnp.zeros_like(sums_ref)

        def vec(r):
            if r in segment_of:
                return _mod(mod_ref, segment_of[r])
            return ln_ref[r - V_LN1W:r - V_LN1W + 1, :]

        def acc(r, val):
            sums_ref[r:r + 1, :] += _colsum(val)

        xx = x_ref[...]
        m = _mm(mx_ref[...], wo_ref[...])
        z1h, rstd1 = _ln(ALPHA * xx + vec(V_GATE1) * m)
        x1 = z1h * vec(V_LN1W) + vec(V_LN1B)
        x1h, rstd0 = _ln(x1)
        u2 = (x1h * (1.0 + vec(V_SCALE2)) + vec(V_SHIFT2)).astype(MXU_DTYPE)
        u2_ref[...] = u2
        f = jnp.zeros((tm, D_MODEL), jnp.float32)
        for j in range(n_fc):
            r = jnp.maximum(_mm(u2, w1_ref[j]), 0.0)
            relu_sc[:, j * fc:(j + 1) * fc] = r
            a = (r * r).astype(MXU_DTYPE)
            a_ref[:, j * fc:(j + 1) * fc] = a
            f = f + _mm(a, w2_ref[j])
        z2h, rstd2 = _ln(ALPHA * x1 + vec(V_GATE2) * f)
        err = z2h * vec(V_LN2W) + vec(V_LN2B) - t_ref[...]
        acc(S_LOSS, err * err)
        dy = err * (1.0 / D_MODEL)
        acc(S_LN2W, dy * z2h)
        acc(S_LN2B, dy)
        dz2 = _ln_bwd(dy * vec(V_LN2W), z2h, rstd2)
        acc(S_GATE2, dz2 * f)
        df = (vec(V_GATE2) * dz2).astype(MXU_DTYPE)
        df_ref[...] = df
        du2 = jnp.zeros((tm, D_MODEL), jnp.float32)
        for j in range(n_fc):
            dh = (_mm_nt(df, w2_ref[j]) * (2.0 * relu_sc[:, j * fc:(j + 1) * fc])).astype(MXU_DTYPE)
            dh_ref[:, j * fc:(j + 1) * fc] = dh
            du2 = du2 + _mm_nt(dh, w1_ref[j])
        acc(S_SCALE2, du2 * x1h)
        acc(S_SHIFT2, du2)
        dx1 = ALPHA * dz2 + _ln_bwd(du2 * (1.0 + vec(V_SCALE2)), x1h, rstd0)
        acc(S_LN1W, dx1 * z1h)
        acc(S_LN1B, dx1)
        dz1 = _ln_bwd(dx1 * vec(V_LN1W), z1h, rstd1)
        acc(S_GATE1, dz1 * m)
        dxa_ref[...] = ALPHA * dz1
        dm = (vec(V_GATE1) * dz1).astype(MXU_DTYPE)
        dm_ref[...] = dm
        dmx_ref[...] = _mm_nt(dm, wo_ref[...])

    tile = lambda width: pl.BlockSpec((tm, width), lambda i: (i, 0))
    f32 = lambda width: jax.ShapeDtypeStruct((seq, width), jnp.float32)
    b16 = lambda width: jax.ShapeDtypeStruct((seq, width), MXU_DTYPE)
    return pl.pallas_call(
        body, name="mlp_fwd_bwd", grid=(seq // tm,),
        out_shape=(f32(D_MODEL), f32(D_MODEL), b16(D_FF), b16(D_FF), b16(D_MODEL), b16(D_MODEL), b16(D_MODEL),
                   jax.ShapeDtypeStruct((16, D_MODEL), jnp.float32)),
        in_specs=[tile(D_MODEL), tile(D_MODEL), tile(D_MODEL), _const_spec(mod.shape), _const_spec(ln_rows.shape),
                  _const_spec(w_out.shape), _const_spec(w1_chunks.shape), _const_spec(w2_chunks.shape)],
        out_specs=(tile(D_MODEL), tile(D_MODEL), tile(D_FF), tile(D_FF), tile(D_MODEL), tile(D_MODEL),
                   tile(D_MODEL), pl.BlockSpec((16, D_MODEL), lambda i: (0, 0))),
        scratch_shapes=[pltpu.VMEM((tm, D_FF), jnp.float32)],
        compiler_params=_params(("arbitrary",)),
    )(x2, mixed, target, mod, ln_rows, w_out, w1_chunks, w2_chunks)


def _grad_matmul(a, b, name, tn, blocks_are_rows, riders=()):
    seq, m_dim = a.shape
    n_dim = b.shape[1]
    tk = min(seq, GRAD_TOKEN_TILE)
    nk = seq // tk
    n_ride = len(riders)
    if blocks_are_rows:
        tm = m_dim // N_CHIP
        assert tn == n_dim
        per_step = N_CHIP if m_dim <= GRAD_ROWS_PER_STEP else 1
        grid = (N_CHIP // per_step, 1, nk)
        out_map = lambda i, j, k: (i, 0, 0)
    else:
        tm = m_dim
        assert tn * N_CHIP == n_dim
        per_step = 1
        grid = (1, N_CHIP, nk)
        out_map = lambda i, j, k: (j, 0, 0)
    n_blocks = grid[0] * grid[1]
    rows = per_step * tm

    def body(*refs):
        a_ref, b_ref = refs[:2]
        ride_in, refs = refs[2:2 + n_ride], refs[2 + n_ride:]
        o_ref = refs[0]
        ride_out, refs = refs[1:1 + n_ride], refs[1 + n_ride:]
        acc_sc = refs[0]
        exchange = _ChipScatter(ride_in, ride_out, refs[1:]) if n_ride else None
        block = pl.program_id(0) + pl.program_id(1)
        k = pl.program_id(2)

        if exchange is not None:
            @pl.when((block == 0) & (k == 0))
            def _():
                exchange.start()

        @pl.when(k == 0)
        def _():
            acc_sc[...] = jnp.zeros_like(acc_sc)

        acc_sc[...] += _mm_tn(a_ref[...], b_ref[...])

        @pl.when(k == nk - 1)
        def _():
            for p in range(per_step):
                o_ref[p] = acc_sc[p * tm:(p + 1) * tm, :].astype(o_ref.dtype)

        if exchange is not None:
            @pl.when((block == n_blocks - 1) & (k == nk - 1))
            def _():
                exchange.wait()

    hbm = pl.BlockSpec(memory_space=pl.ANY)
    out = pl.pallas_call(
        body, name=name, grid=grid,
        out_shape=(jax.ShapeDtypeStruct((N_CHIP, tm, tn), WIRE_DTYPE),) + _exchange_out_shapes(riders, False),
        in_specs=[pl.BlockSpec((tk, rows), lambda i, j, k: (k, i)), pl.BlockSpec((tk, tn), lambda i, j, k: (k, j))]
        + [hbm] * n_ride,
        out_specs=(pl.BlockSpec((per_step, tm, tn), out_map),) + (hbm,) * n_ride,
        scratch_shapes=[pltpu.VMEM((rows, tn), jnp.float32)] + (_scatter_sems(n_ride) if n_ride else []),
        compiler_params=_params(("arbitrary", "arbitrary", "arbitrary")),
    )(a, b, *riders)
    return out if n_ride else out[0]


def _grad_matmul_full(a, b, name, tm, riders):
    seq, m_dim = a.shape
    n_dim = b.shape[1]
    tk = min(seq, GRAD_TOKEN_TILE)
    nk = seq // tk
    n_blocks = m_dim // tm
    n_ride = len(riders)
    assert m_dim % tm == 0

    def body(*refs):
        a_ref, b_ref = refs[:2]
        ride_in, refs = refs[2:2 + n_ride], refs[2 + n_ride:]
        o_ref = refs[0]
        ride_out, refs = refs[1:1 + n_ride], refs[1 + n_ride:]
        acc_sc = refs[0]
        swap = _SiblingSwap(ride_in, ride_out, refs[1:])
        i, k = pl.program_id(0), pl.program_id(1)

        @pl.when((i == 0) & (k == 0))
        def _():
            swap.start()

        @pl.when(k == 0)
        def _():
            acc_sc[...] = jnp.zeros_like(acc_sc)

        acc_sc[...] += _mm_tn(a_ref[...], b_ref[...])

        @pl.when(k == nk - 1)
        def _():
            o_ref[...] = acc_sc[...].astype(o_ref.dtype)

        @pl.when((i == n_blocks - 1) & (k == nk - 1))
        def _():
            swap.wait()

    hbm = pl.BlockSpec(memory_space=pl.ANY)
    return pl.pallas_call(
        body, name=name, grid=(n_blocks, nk),
        out_shape=(jax.ShapeDtypeStruct((m_dim, n_dim), WIRE_DTYPE),)
        + tuple(jax.ShapeDtypeStruct(r.shape, r.dtype) for r in riders),
        in_specs=[pl.BlockSpec((tk, tm), lambda i, k: (k, i)), pl.BlockSpec((tk, n_dim), lambda i, k: (k, 0))]
        + [hbm] * n_ride,
        out_specs=(pl.BlockSpec((tm, n_dim), lambda i, k: (i, 0)),) + (hbm,) * n_ride,
        scratch_shapes=[pltpu.VMEM((tm, n_dim), jnp.float32)] + _swap_sems(n_ride),
        compiler_params=_params(("arbitrary", "arbitrary")),
    )(a, b, *riders)


def _adam_pair(w, g_mine, g_sibling, m, v, name):
    rows, cols = w.shape
    tc = min(cols, ELEMENTWISE_COLS)

    def total(ref):
        if len(ref.shape) == 2:
            return ref[...]
        acc = ref[0].astype(jnp.float32)
        for j in range(1, ref.shape[0]):
            acc = acc + ref[j].astype(jnp.float32)
        return acc

    def body(w_ref, ga_ref, gb_ref, m_ref, v_ref, g_ref, dl_ref, m2_ref, v2_ref):
        g = total(ga_ref) + total(gb_ref)
        delta, m2, v2 = _adam(w_ref[...], g, m_ref[...], v_ref[...])
        g_ref[...] = g
        dl_ref[...] = delta
        m2_ref[...] = m2
        v2_ref[...] = v2

    blk = pl.BlockSpec((rows, tc), lambda i: (0, i))
    g_blk = lambda a: blk if a.ndim == 2 else pl.BlockSpec((a.shape[0], rows, tc), lambda i: (0, 0, i))
    out = jax.ShapeDtypeStruct((rows, cols), jnp.float32)
    return pl.pallas_call(
        body, name=name, grid=(cols // tc,),
        out_shape=(out, out, out, out),
        in_specs=[blk, g_blk(g_mine), g_blk(g_sibling), blk, blk], out_specs=(blk,) * 4,
        compiler_params=_params(("arbitrary",)),
    )(w, g_mine, g_sibling, m, v)


def _sum_devices(gathered, layout):
    def body(g_ref, *o_refs):
        total = g_ref[0]
        for d in range(1, N_DEV):
            total = total + g_ref[d]
        for (first, (rows_out, cols_out)), o_ref in zip(layout, o_refs):
            per_row = cols_out // 128
            for r in range(rows_out):
                for k in range(per_row):
                    src = first + r * per_row + k
                    o_ref[r:r + 1, k * 128:(k + 1) * 128] = total[src:src + 1, :]
        tail = total[total.shape[0] - 8:, :]
        o_refs[-1][...] = jnp.full((1, 128), jnp.sum(tail), jnp.float32)

    out_shape = tuple(jax.ShapeDtypeStruct(shape, jnp.float32) for _, shape in layout)
    return pl.pallas_call(
        body, name="sum_devices",
        out_shape=out_shape + (jax.ShapeDtypeStruct((1, 128), jnp.float32),),
    )(gathered)


def _adam_small(params):
    n = len(params)

    def body(*refs):
        ins, outs = refs[:4 * n], refs[4 * n:]
        for i in range(n):
            w_ref, g_ref, m_ref, v_ref = ins[4 * i:4 * i + 4]
            delta, m2, v2 = _adam(w_ref[...], g_ref[...], m_ref[...], v_ref[...])
            outs[3 * i][...] = delta
            outs[3 * i + 1][...] = m2
            outs[3 * i + 2][...] = v2

    out_shape = tuple(jax.ShapeDtypeStruct(p[0].shape, jnp.float32) for p in params for _ in range(3))
    out = pl.pallas_call(body, name="adam_small", out_shape=out_shape)(*[t for p in params for t in p])
    return [out[3 * i:3 * i + 3] for i in range(n)]


def _pad_heads(w):
    lead = w.shape[:-1]
    w = w.reshape(lead + (N_HEADS, GLA_DK))
    w = jnp.pad(w, [(0, 0)] * len(lead) + [(0, 0), (0, HEAD_W - GLA_DK)])
    return w.reshape(lead + (N_HEADS * HEAD_W,))


def _unpad_heads(w):
    lead = w.shape[:-1]
    return w.reshape(lead + (N_HEADS, HEAD_W))[..., :GLA_DK].reshape(lead + (N_HEADS * GLA_DK,))


def _pad_head_rows(w):
    w = w.reshape(N_HEADS, GLA_DK, w.shape[-1])
    return jnp.pad(w, ((0, 0), (0, HEAD_W - GLA_DK), (0, 0))).reshape(N_HEADS * HEAD_W, w.shape[-1])


def _unpad_head_rows(w):
    return w.reshape(N_HEADS, HEAD_W, w.shape[-1])[:, :GLA_DK].reshape(N_HEADS * GLA_DK, w.shape[-1])


def _pad_w_in_rows(stack):
    w = stack.reshape(-1, stack.shape[-1])
    return jnp.concatenate([
        w[:2048], _pad_head_rows(w[2048:2304] * GLA_SCALE), _pad_head_rows(w[2304:2560]), w[2560:3584],
        jnp.pad(w[3584:3600], ((0, HEAD_W - GATE_RANK), (0, 0)))], axis=0)


def _unpad_w_in_stack(g, per):
    segments = [(0, g[:2048]), (2048, _unpad_head_rows(g[OFF_GQ:OFF_GQ + 512]) * GLA_SCALE),
                (2304, _unpad_head_rows(g[OFF_GK:OFF_GK + 512])), (2560, g[OFF_GV:OFF_LR]),
                (3584, g[OFF_LR:OFF_LR + GATE_RANK])]
    blocks = []
    for j in range(N_CHIP):
        lo, hi = j * per, (j + 1) * per
        pieces = []
        for start, rows in segments:
            a, b = max(lo, start), min(hi, start + rows.shape[0])
            if a < b:
                pieces.append(rows[a - start:b - start])
        blocks.append(jnp.concatenate(pieces, axis=0))
    return jnp.stack(blocks)


def _col_major(w):
    return jnp.transpose(w, (2, 0, 1)).reshape(w.shape[2], w.shape[1])


def _rows128(a):
    return a.reshape(-1, 128)


def kernel(x, c, w_ada, b_ada, w_in, ret_norm_w, gla_gate_w, gla_gate_b, gla_norm_w, w_out, ln1_w, ln1_b, w_ff1, w_ff2, ln2_w, ln2_b, loss_target, m_w_ada, m_b_ada, m_w_in, m_ret_norm_w, m_gla_gate_w, m_gla_gate_b, m_gla_norm_w, m_w_out, m_ln1_w, m_ln1_b, m_w_ff1, m_w_ff2, m_ln2_w, m_ln2_b, v_w_ada, v_b_ada, v_w_in, v_ret_norm_w, v_gla_gate_w, v_gla_gate_b, v_gla_norm_w, v_w_out, v_ln1_w, v_ln1_b, v_w_ff1, v_w_ff2, v_ln2_w, v_ln2_b):
    seq = x.shape[1]
    tm = min(seq, TOKEN_TILE)
    tm_in = min(seq, INPROJ_TOKEN_TILE)
    xi, yi, _ = _mesh_pos()
    chip = 2 * xi + yi
    x2, target = x[0], loss_target[0]
    ada_cols = w_ada.shape[2]
    in_cols = w_in.shape[2]
    gate_cols = gla_gate_w.shape[2]

    b_blk = lax.dynamic_slice(b_ada, (0, chip * ada_cols), (1, ada_cols))
    g0, _, mod, w_in_stack = _prologue(jnp.concatenate([_rows128(c), _rows128(gla_gate_w[0])], axis=0), w_ada[0],
                                       b_blk, _col_major(w_in.astype(WIRE_DTYPE)))
    c_all = g0[:, :8].reshape(N_DEV, D_MODEL)
    gate_w_full = jnp.concatenate([g0[2 * j, 8:16].reshape(GATE_RANK, gate_cols) for j in range(N_CHIP)], axis=1)
    wg_p = jnp.pad(_pad_heads(gate_w_full), ((0, HEAD_W - GATE_RANK), (0, 0)))
    bg_p = _pad_heads(gla_gate_b)
    w_in_pt = _pad_w_in_rows(w_in_stack).astype(MXU_DTYPE)
    w_in_p = jnp.transpose(w_in_pt)

    proj, u, w2_stack = _inproj_fwd(x2, mod, w_in_p, tm_in, [w_ff2[0].astype(WIRE_DTYPE)])
    rot_a, rot_b = _rotary_tables(seq)
    dm_t, qdec_t, kdec_t, chunk_decay = _decay_tables()
    tables = (rot_a, rot_b, dm_t, qdec_t, kdec_t, chunk_decay)
    mixed, rsave, ssave, w_out_stack, w1_stack = _mixer_fwd(
        proj, tables, wg_p, bg_p, ret_norm_w, gla_norm_w,
        [w_out[0].astype(WIRE_DTYPE), w_ff1[0].astype(WIRE_DTYPE)])
    w_out_full = w_out_stack.reshape(D_MODEL, D_MODEL).astype(MXU_DTYPE)
    w1_chunks = w1_stack.astype(MXU_DTYPE)
    w2_chunks = w2_stack.astype(MXU_DTYPE)

    ln_rows = jnp.concatenate([ln1_w, ln1_b, ln2_w, ln2_b], axis=0)
    dmixed, dxa, act, dh, u2, df, dm, sums2 = _mlp_fwd_bwd(x2, mixed, target, mod, ln_rows, w_out_full, w1_chunks,
                                                           w2_chunks, tm)

    g_out_stack = _grad_matmul(mixed, dm, "grad_w_out", D_MODEL, True)
    g_ff1_stack, r_out = _grad_matmul(u2, dh, "grad_w_ff1", D_FF // N_CHIP, False, [g_out_stack])
    g_ff2_stack = _grad_matmul(act, df, "grad_w_ff2", D_MODEL, True)
    dproj, d_ret_norm, d_gla_norm, d_wg_p, d_bg_p, r_ff1, r_ff2 = _mixer_bwd(
        proj, dmixed, rsave, ssave, tables, wg_p, bg_p, ret_norm_w, gla_norm_w, [g_ff1_stack, g_ff2_stack])
    early = ["w_out", "w_ff1", "w_ff2"]
    partial = dict(zip(early, [r_out, r_ff1, r_ff2]))
    g_in_t, *swapped_early = _grad_matmul_full(dproj, u, "grad_w_in", N_PROJ // 3, [partial[n] for n in early])
    swapped = dict(zip(early, swapped_early))
    g_in_stack = _unpad_w_in_stack(g_in_t, in_cols)
    grad_x, sums1, r_in = _inproj_bwd(dproj, x2, dxa, mod, w_in_pt, tm_in, [g_in_stack])

    sources = [sums1, sums2, d_ret_norm, _unpad_heads(d_bg_p), d_gla_norm, _unpad_heads(d_wg_p[:GATE_RANK])]
    pieces = [(0, 0, 0), (0, 1, 8), (1, S_GATE1, 16), (1, S_SHIFT2, 24), (1, S_SCALE2, 32), (1, S_GATE2, 40),
              (1, S_LN1W, 48), (1, S_LN1B, 56), (1, S_LN2W, 64), (1, S_LN2B, 72), (2, 0, 80), (3, 0, 88), (4, 0, 96)]
    pieces += [(5, r, 104 + 2 * r) for r in range(GATE_RANK)] + [(1, S_LOSS, 136)]
    partial["w_in"] = r_in
    g2, swapped["w_in"] = _gather_rows(sources, pieces, 144, "gather_small", [r_in])
    (grad_b_ada, grad_ln1_w, grad_ln1_b, grad_ln2_w, grad_ln2_b, grad_ret_norm, grad_gate_b, grad_gla_norm,
     grad_gate_w_full, loss_sum) = _sum_devices(g2, [
         (0, (1, 6 * D_MODEL)), (48, (1, D_MODEL)), (56, (1, D_MODEL)), (64, (1, D_MODEL)), (72, (1, D_MODEL)),
         (80, (1, 512)), (88, (1, 256)), (96, (1, 512)), (104, (GATE_RANK, 256))])
    loss = 0.5 / D_MODEL * loss_sum[0, 0]
    grad_gate_w = lax.dynamic_slice(grad_gate_w_full, (0, chip * gate_cols), (GATE_RANK, gate_cols))

    small_grads = [grad_b_ada, grad_ln1_w, grad_ln1_b, grad_ln2_w, grad_ln2_b, grad_ret_norm, grad_gate_b,
                   grad_gla_norm, grad_gate_w[None]]
    small_out = _adam_small(list(zip(
        [b_ada, ln1_w, ln1_b, ln2_w, ln2_b, ret_norm_w, gla_gate_b, gla_norm_w, gla_gate_w], small_grads,
        [m_b_ada, m_ln1_w, m_ln1_b, m_ln2_w, m_ln2_b, m_ret_norm_w, m_gla_gate_b, m_gla_norm_w, m_gla_gate_w],
        [v_b_ada, v_ln1_w, v_ln1_b, v_ln2_w, v_ln2_b, v_ret_norm_w, v_gla_gate_b, v_gla_norm_w, v_gla_gate_w])))
    sm_delta, sm_m, sm_v = [[o[k] for o in small_out] for k in range(3)]

    dmod_all = g2[:, 0:48].reshape(N_DEV, 6 * D_MODEL)
    dmod_blk = lax.dynamic_slice(dmod_all, (0, chip * ada_cols), (N_DEV, ada_cols))
    ada_out = _ada_bwd_adam(jnp.transpose(c_all), dmod_blk, w_ada[0], m_w_ada[0], v_w_ada[0])
    ada_g, ada_delta, ada_m, ada_v = [t[None] for t in ada_out]

    big = {}
    for n, w, m, v in zip(["w_in", "w_out", "w_ff1", "w_ff2"], [w_in, w_out, w_ff1, w_ff2],
                          [m_w_in, m_w_out, m_w_ff1, m_w_ff2], [v_w_in, v_w_out, v_w_ff1, v_w_ff2]):
        mine, theirs = partial[n], swapped[n]
        if n == "w_in":
            out = _adam_pair(_col_major(w), mine, theirs, _col_major(m), _col_major(v), "adam_" + n)
            big[n] = [jnp.transpose(t.reshape(t.shape[0], 1, t.shape[1]), (1, 2, 0)) for t in out]
        else:
            big[n] = [t[None] for t in _adam_pair(w[0], mine, theirs, m[0], v[0], "adam_" + n)]

    def assemble(ada, smalls, k):
        b_ada_o, ln1w_o, ln1b_o, ln2w_o, ln2b_o, ret_o, gb_o, gln_o, gw_o = smalls
        return [ada, b_ada_o, big["w_in"][k], ret_o, gw_o, gb_o, gln_o, big["w_out"][k], ln1w_o, ln1b_o,
                big["w_ff1"][k], big["w_ff2"][k], ln2w_o, ln2b_o]

    grads = assemble(ada_g, small_grads, 0)
    deltas = assemble(ada_delta, sm_delta, 1)
    new_m = assemble(ada_m, sm_m, 2)
    new_v = assemble(ada_v, sm_v, 3)
    return (loss, grad_x[None], *grads, *deltas, *new_m, *new_v)
```

```python
import numpy as np
import jax
import jax.numpy as jnp
from jax import lax
from jax.experimental import pallas as pl
from jax.experimental.pallas import tpu as pltpu

D_MODEL = 1024
D_FF = 4096
CHUNK = 64
N_HEADS = 4
HEAD_W = 128
GLA_DK = 64
GATE_RANK = 16
GATE_TAU = 16.0
LN_EPS = 1e-5
ALPHA = 2.0 ** 0.25
ROPE_BASE = 10000.0
RET_SCALE = float(HEAD_W) ** -0.5
GLA_SCALE = float(GLA_DK) ** -0.5

ADAM_LR = 0.001
ADAM_B1 = 0.9
ADAM_B2 = 0.999
ADAM_EPS = 1e-08
ADAM_WD = 0.01
ADAM_STEP = 10

OFF_RQ, OFF_RK, OFF_RV, OFF_RG = 0, 512, 1024, 1536
OFF_GQ, OFF_GK, OFF_GV, OFF_GG, OFF_LR = 2048, 2560, 3072, 3584, 4096
N_PROJ = 4224

N_DEV = 8
N_CHIP = 4
MESH = pl.DeviceIdType.MESH
MXU_DTYPE = jnp.bfloat16
WIRE_DTYPE = jnp.bfloat16
VMEM_LIMIT = 60 * 1024 * 1024
TOKEN_TILE = 256
INPROJ_TOKEN_TILE = 512
CHUNKS_PER_STEP = 8
CHUNKS_IN_LOCKSTEP = 4
GRAD_ROWS_PER_STEP = 1024
GRAD_TOKEN_TILE = 2048
ELEMENTWISE_COLS = 512
HIGHEST = lax.Precision.HIGHEST


def _mm(a, b):
    return jnp.dot(a.astype(MXU_DTYPE), b.astype(MXU_DTYPE), preferred_element_type=jnp.float32)


def _mm_nt(a, b):
    return lax.dot_general(a.astype(MXU_DTYPE), b.astype(MXU_DTYPE), (((1,), (1,)), ((), ())),
                           preferred_element_type=jnp.float32)


def _mm_tn(a, b):
    return lax.dot_general(a.astype(MXU_DTYPE), b.astype(MXU_DTYPE), (((0,), (0,)), ((), ())),
                           preferred_element_type=jnp.float32)


def _mm32(a, b):
    return jnp.dot(a, b, precision=HIGHEST, preferred_element_type=jnp.float32)


def _running_sum(mask, a):
    m = mask.astype(jnp.bfloat16)
    hi = a.astype(jnp.bfloat16)
    rest = a - hi.astype(jnp.float32)
    mid = rest.astype(jnp.bfloat16)
    lo = (rest - mid.astype(jnp.float32)).astype(jnp.bfloat16)
    dot = lambda t: jnp.dot(m, t, preferred_element_type=jnp.float32)
    return dot(hi) + dot(mid) + dot(lo)


def _rowmean(a):
    return jnp.mean(a, axis=-1, keepdims=True)


def _colsum(a):
    return jnp.sum(a, axis=0, keepdims=True)


def _ln(z):
    zc = z - _rowmean(z)
    rstd = lax.rsqrt(_rowmean(zc * zc) + LN_EPS)
    return zc * rstd, rstd


def _ln_bwd(dzh, zh, rstd):
    return rstd * (dzh - _rowmean(dzh) - zh * _rowmean(dzh * zh))


def _sigmoid(a):
    return 1.0 / (1.0 + jnp.exp(-a))


def _log_sigmoid(a):
    return jnp.minimum(a, 0.0) - jnp.log(1.0 + jnp.exp(-jnp.abs(a)))


def _swap_halves(a):
    return pltpu.roll(a, HEAD_W // 2, 1)


def _tri_masks():
    row = lax.broadcasted_iota(jnp.int32, (CHUNK, CHUNK), 0)
    col = lax.broadcasted_iota(jnp.int32, (CHUNK, CHUNK), 1)
    return row, col


def _const_spec(shape):
    zeros = (0,) * len(shape)
    return pl.BlockSpec(shape, lambda *_: zeros, pipeline_mode=pl.Buffered(1))


def _params(semantics):
    return pltpu.CompilerParams(dimension_semantics=semantics, vmem_limit_bytes=VMEM_LIMIT)


def _decay_tables():
    log_gamma = np.log(1.0 - 2.0 ** (-5.0 - np.arange(N_HEADS, dtype=np.float64)))
    idx = np.arange(CHUNK, dtype=np.float64)
    dist = np.abs(idx[:, None] - idx[None, :])
    intra = np.exp(log_gamma[:, None, None] * dist)
    kdec = np.exp(log_gamma[None, :] * (CHUNK - 1.0 - idx)[:, None])
    qdec = np.exp(log_gamma[None, :] * (idx + 1.0)[:, None])
    chunk_decay = np.exp(log_gamma * CHUNK)
    lanes = lambda t: np.repeat(t, HEAD_W, axis=1).astype(np.float32)
    return (jnp.asarray(intra.astype(np.float32)), jnp.asarray(lanes(qdec)), jnp.asarray(lanes(kdec)),
            [float(np.float32(v)) for v in chunk_decay])


def _rotary_tables(seq):
    half = HEAD_W // 2
    inv = (np.float32(1.0) / np.float32(ROPE_BASE) ** np.linspace(0.0, 1.0, half, dtype=np.float32)).astype(np.float32)
    both = lambda t: np.concatenate([t, t], axis=-1)
    ang_a = np.arange(0, seq, CHUNK, dtype=np.float32)[:, None] * inv[None, :]
    rot_a = np.stack([both(np.cos(ang_a)), both(np.sin(ang_a))], axis=1)
    rot_a = np.pad(rot_a, ((0, 0), (0, 6), (0, 0))).astype(np.float32)
    ang_b = np.arange(CHUNK, dtype=np.float32)[:, None] * inv[None, :]
    cos_b, sin_b = both(np.cos(ang_b)), both(np.sin(ang_b))
    sign = np.concatenate([-np.ones((half,), np.float32), np.ones((half,), np.float32)])
    rot_b = np.stack([cos_b, sin_b, cos_b * sign, sin_b * sign]).astype(np.float32)
    return jnp.asarray(rot_a), jnp.asarray(rot_b)


def _rotary_chunk(ra_ref, c, rb_ref):
    cos_a, sin_a = ra_ref[c, 0:1, :], ra_ref[c, 1:2, :]
    return cos_a * rb_ref[0] - sin_a * rb_ref[1], sin_a * rb_ref[2] + cos_a * rb_ref[3]


def _mesh_pos():
    return lax.axis_index("x"), lax.axis_index("y"), lax.axis_index("c")


def _flip(v, bit):
    return 1 - v if bit else v


def _gather_rows(sources, pieces, rows, name, swaps):
    n_src, n = len(sources), len(swaps)

    def body(*refs):
        src_refs, refs = refs[:n_src], refs[n_src:]
        out_ref = refs[n]
        v_sc = refs[1 + 2 * n]
        swap = _SiblingSwap(refs[:n], refs[1 + n:1 + 2 * n], refs[4 + 2 * n:])
        swap.start()
        v_sc[...] = jnp.zeros_like(v_sc)
        for s, row, first in pieces:
            for k in range(src_refs[s].shape[1] // 128):
                v_sc[first + k:first + k + 1, :] = src_refs[s][row:row + 1, k * 128:(k + 1) * 128]
        _all_devices_exchange(v_sc, out_ref, refs[2 + 2 * n], refs[3 + 2 * n])
        swap.wait()

    hbm = pl.BlockSpec(memory_space=pl.ANY)
    vmem = pl.BlockSpec(memory_space=pltpu.VMEM)
    return pl.pallas_call(
        body, name=name,
        out_shape=(jax.ShapeDtypeStruct((N_DEV, rows, 128), jnp.float32),)
        + tuple(jax.ShapeDtypeStruct(a.shape, a.dtype) for a in swaps),
        in_specs=[vmem] * n_src + [hbm] * n,
        out_specs=(vmem,) + (hbm,) * n,
        scratch_shapes=[pltpu.VMEM((rows, 128), jnp.float32)] + _all_devices_sems() + _swap_sems(n),
    )(*sources, *swaps)


def _all_devices_sems():
    return [pltpu.SemaphoreType.DMA((N_DEV - 1,)), pltpu.SemaphoreType.DMA((N_DEV - 1,))]


def _all_devices_exchange(v_ref, out_ref, send_sems, recv_sems):
    x, y, c = _mesh_pos()
    me = 4 * x + 2 * y + c
    out_ref[me] = v_ref[...]
    sends, recvs = [], []
    for k in range(1, N_DEV):
        px, py, pc = _flip(x, (k >> 2) & 1), _flip(y, (k >> 1) & 1), _flip(c, k & 1)
        peer = 4 * px + 2 * py + pc
        sends.append(pltpu.make_async_remote_copy(
            src_ref=v_ref, dst_ref=out_ref.at[me], send_sem=send_sems.at[k - 1], recv_sem=recv_sems.at[k - 1],
            device_id=(px, py, pc), device_id_type=MESH))
        recvs.append(pltpu.make_async_remote_copy(
            src_ref=v_ref, dst_ref=out_ref.at[peer], send_sem=send_sems.at[k - 1], recv_sem=recv_sems.at[k - 1],
            device_id=(px, py, pc), device_id_type=MESH))
    for cp in sends:
        cp.start()
    for cp in recvs:
        cp.wait_recv()
    for cp in sends:
        cp.wait_send()


def _prologue(cond_rows, w_ada_blk, b_blk, w_in_t):
    cols = w_ada_blk.shape[1]
    groups = cols // 128
    c_rows = D_MODEL // 128

    def body(cond_ref, w_ref, b_ref, win_ref, cond_all_ref, mod_all_ref, mod_ref, stack_ref, mod_sc, *sems):
        gather = _ChipGather([win_ref], [stack_ref], sems[:5])
        gather.start()
        _all_devices_exchange(cond_ref, cond_all_ref, sems[5], sems[6])
        acc = jnp.broadcast_to(b_ref[...], (N_DEV, cols))
        for r in range(c_rows):
            cv = cond_all_ref[:, r, :]
            acc = acc + _mm32(cv * _sigmoid(cv), w_ref[r * 128:(r + 1) * 128, :])
        for k in range(groups):
            mod_sc[k] = acc[:, k * 128:(k + 1) * 128]
        _all_devices_exchange(mod_sc, mod_all_ref, sems[7], sems[8])
        x, y, c = _mesh_pos()
        me = 4 * x + 2 * y + c
        for j in range(N_CHIP):
            for k in range(groups):
                lane = j * cols + k * 128
                mod_ref[:, lane:lane + 128] = mod_all_ref[2 * j, k, pl.ds(me, 1), :]
        gather.forward()
        gather.finish()

    vmem = pl.BlockSpec(memory_space=pltpu.VMEM)
    hbm = pl.BlockSpec(memory_space=pl.ANY)
    return pl.pallas_call(
        body, name="prologue",
        out_shape=(jax.ShapeDtypeStruct((N_DEV,) + cond_rows.shape, jnp.float32),
                   jax.ShapeDtypeStruct((N_DEV, groups, N_DEV, 128), jnp.float32),
                   jax.ShapeDtypeStruct((1, N_CHIP * cols), jnp.float32))
        + _exchange_out_shapes([w_in_t], True),
        in_specs=[vmem, vmem, vmem, hbm],
        out_specs=(vmem, vmem, vmem, hbm),
        scratch_shapes=[pltpu.VMEM((groups, N_DEV, 128), jnp.float32)] + _gather_sems(1)
        + _all_devices_sems() + _all_devices_sems(),
        compiler_params=pltpu.CompilerParams(vmem_limit_bytes=VMEM_LIMIT),
    )(cond_rows, w_ada_blk, b_blk, w_in_t)


def _exchange_out_shapes(arrays, gather):
    return tuple(jax.ShapeDtypeStruct((N_CHIP,) + a.shape if gather else a.shape, a.dtype) for a in arrays)


def _scatter_sems(n):
    n_sem = n * (N_CHIP - 1)
    return [pltpu.SemaphoreType.DMA((n_sem,)), pltpu.SemaphoreType.DMA((n_sem,)), pltpu.SemaphoreType.DMA((n,))]


def _gather_sems(n):
    n_sem = n * (N_CHIP - 1)
    return [pltpu.SemaphoreType.DMA((n_sem,))] * 4 + [pltpu.SemaphoreType.DMA((n,))]


def _peer_chips(x, y):
    out = []
    for k in range(1, N_CHIP):
        px, py = _flip(x, (k >> 1) & 1), _flip(y, k & 1)
        out.append((px, py, 2 * px + py))
    return out


class _ChipScatter:
    def __init__(self, ins, outs, sems):
        send_sems, recv_sems, local_sems = sems
        x, y, c = _mesh_pos()
        chip = 2 * x + y
        self.local, self.sends, self.recvs = [], [], []
        for i in range(len(ins)):
            self.local.append(pltpu.make_async_copy(ins[i].at[chip], outs[i].at[chip], local_sems.at[i]))
            for k, (px, py, peer_chip) in enumerate(_peer_chips(x, y)):
                sem = i * (N_CHIP - 1) + k
                src = ins[i].at[peer_chip]
                self.sends.append(pltpu.make_async_remote_copy(
                    src_ref=src, dst_ref=outs[i].at[chip], send_sem=send_sems.at[sem], recv_sem=recv_sems.at[sem],
                    device_id=(px, py, c), device_id_type=MESH))
                self.recvs.append(pltpu.make_async_remote_copy(
                    src_ref=src, dst_ref=outs[i].at[peer_chip], send_sem=send_sems.at[sem], recv_sem=recv_sems.at[sem],
                    device_id=(px, py, c), device_id_type=MESH))

    def start(self):
        for cp in self.local + self.sends:
            cp.start()

    def wait(self):
        for cp in self.recvs:
            cp.wait_recv()
        for cp in self.sends:
            cp.wait_send()
        for cp in self.local:
            cp.wait()


class _ChipGather:
    def __init__(self, ins, outs, sems):
        ici_send, ici_recv, d2d_send, d2d_recv, local_sems = sems
        x, y, c = _mesh_pos()
        chip = 2 * x + y
        self.local, self.ici_sends, self.ici_recvs, self.d2d_sends, self.d2d_recvs = [], [], [], [], []
        for i in range(len(ins)):
            half = ins[i].shape[-1] // 2
            assert half % 128 == 0
            lead = (slice(None),) * (len(ins[i].shape) - 1)
            mine = lead + (pl.ds(pl.multiple_of(c * half, 128), half),)
            theirs = lead + (pl.ds(pl.multiple_of((1 - c) * half, 128), half),)
            self.local.append(pltpu.make_async_copy(ins[i], outs[i].at[chip], local_sems.at[i]))
            for k, (px, py, peer_chip) in enumerate(_peer_chips(x, y)):
                sem = i * (N_CHIP - 1) + k
                self.ici_sends.append(pltpu.make_async_remote_copy(
                    src_ref=ins[i].at[mine], dst_ref=outs[i].at[chip].at[mine],
                    send_sem=ici_send.at[sem], recv_sem=ici_recv.at[sem], device_id=(px, py, c), device_id_type=MESH))
                landed = outs[i].at[peer_chip].at[mine]
                self.ici_recvs.append(pltpu.make_async_remote_copy(
                    src_ref=ins[i].at[mine], dst_ref=landed,
                    send_sem=ici_send.at[sem], recv_sem=ici_recv.at[sem], device_id=(px, py, c), device_id_type=MESH))
                self.d2d_sends.append(pltpu.make_async_remote_copy(
                    src_ref=landed, dst_ref=landed,
                    send_sem=d2d_send.at[sem], recv_sem=d2d_recv.at[sem], device_id=(x, y, 1 - c), device_id_type=MESH))
                self.d2d_recvs.append(pltpu.make_async_remote_copy(
                    src_ref=landed, dst_ref=outs[i].at[peer_chip].at[theirs],
                    send_sem=d2d_send.at[sem], recv_sem=d2d_recv.at[sem], device_id=(x, y, 1 - c), device_id_type=MESH))

    def start(self):
        for cp in self.local + self.ici_sends:
            cp.start()

    def forward(self):
        for landed, onward in zip(self.ici_recvs, self.d2d_sends):
            landed.wait_recv()
            onward.start()

    def finish(self):
        for cp in self.d2d_recvs:
            cp.wait_recv()
        for cp in self.d2d_sends + self.ici_sends:
            cp.wait_send()
        for cp in self.local:
            cp.wait()


def _swap_sems(n):
    return [pltpu.SemaphoreType.DMA((n,)), pltpu.SemaphoreType.DMA((n,))]


class _SiblingSwap:
    def __init__(self, ins, outs, sems):
        send_sems, recv_sems = sems
        x, y, c = _mesh_pos()
        self.copies = [pltpu.make_async_remote_copy(
            src_ref=ins[i], dst_ref=outs[i], send_sem=send_sems.at[i], recv_sem=recv_sems.at[i],
            device_id=(x, y, 1 - c), device_id_type=MESH) for i in range(len(ins))]

    def start(self):
        for cp in self.copies:
            cp.start()

    def wait(self):
        for cp in self.copies:
            cp.wait_recv()
        for cp in self.copies:
            cp.wait_send()


def _adam(w, g, m, v):
    m2 = ADAM_B1 * m + (1.0 - ADAM_B1) * g
    v2 = ADAM_B2 * v + (1.0 - ADAM_B2) * (g * g)
    m_hat = m2 / (1.0 - ADAM_B1 ** ADAM_STEP)
    v_hat = v2 / (1.0 - ADAM_B2 ** ADAM_STEP)
    delta = -ADAM_LR * (m_hat / (jnp.sqrt(v_hat) + ADAM_EPS) + ADAM_WD * w)
    return delta, m2, v2


def _ada_bwd_adam(c_t, dmod_blk, w, m, v):
    rows, cols = w.shape
    tile = 512
    assert cols % tile == 0

    def body(c_ref, d_ref, w_ref, m_ref, v_ref, g_ref, dl_ref, m2_ref, v2_ref):
        sc = c_ref[...]
        sc = sc * _sigmoid(sc)
        dm = d_ref[...]
        g = sc[:, 0:1] * dm[0:1, :]
        for b in range(1, N_DEV):
            g = g + sc[:, b:b + 1] * dm[b:b + 1, :]
        delta, m2, v2 = _adam(w_ref[...], g, m_ref[...], v_ref[...])
        g_ref[...] = g
        dl_ref[...] = delta
        m2_ref[...] = m2
        v2_ref[...] = v2

    blk = pl.BlockSpec((rows, tile), lambda j: (0, j))
    out = jax.ShapeDtypeStruct((rows, cols), jnp.float32)
    return pl.pallas_call(
        body, name="ada_bwd_adam", grid=(cols // tile,),
        out_shape=(out, out, out, out),
        in_specs=[pl.BlockSpec((rows, N_DEV), lambda j: (0, 0)), pl.BlockSpec((N_DEV, tile), lambda j: (0, j)),
                  blk, blk, blk],
        out_specs=(blk, blk, blk, blk),
        compiler_params=_params(("arbitrary",)),
    )(c_t, dmod_blk, w, m, v)


MOD_SHIFT1, MOD_SCALE1, MOD_GATE1, MOD_SHIFT2, MOD_SCALE2, MOD_GATE2 = range(6)


def _mod(mod_ref, segment):
    return mod_ref[:, segment * D_MODEL:(segment + 1) * D_MODEL]


def _inproj_fwd(x2, vecs, w_in_p, tm, riders):
    seq = x2.shape[0]
    n_tiles = seq // tm
    n_ride = len(riders)

    def body(*refs):
        x_ref, vec_ref, w_ref = refs[:3]
        ride_in, refs = refs[3:3 + n_ride], refs[3 + n_ride:]
        p_ref, u_ref = refs[:2]
        ride_out, sems = refs[2:2 + n_ride], refs[2 + n_ride:]
        gather = _ChipGather(ride_in, ride_out, sems)

        @pl.when(pl.program_id(0) == 0)
        def _():
            gather.start()

        xh, _ = _ln(x_ref[...])
        u = (xh * (1.0 + _mod(vec_ref, MOD_SCALE1)) + _mod(vec_ref, MOD_SHIFT1)).astype(MXU_DTYPE)
        u_ref[...] = u
        p_ref[...] = _mm(u, w_ref[...]).astype(p_ref.dtype)

        @pl.when(pl.program_id(0) == (3 * n_tiles) // 4)
        def _():
            gather.forward()

        @pl.when(pl.program_id(0) == n_tiles - 1)
        def _():
            gather.finish()

    hbm = pl.BlockSpec(memory_space=pl.ANY)
    return pl.pallas_call(
        body, name="inproj_fwd", grid=(n_tiles,),
        out_shape=(jax.ShapeDtypeStruct((seq, N_PROJ), MXU_DTYPE), jax.ShapeDtypeStruct((seq, D_MODEL), MXU_DTYPE))
        + _exchange_out_shapes(riders, True),
        in_specs=[pl.BlockSpec((tm, D_MODEL), lambda i: (i, 0)), _const_spec(vecs.shape), _const_spec(w_in_p.shape)]
        + [hbm] * n_ride,
        out_specs=(pl.BlockSpec((tm, N_PROJ), lambda i: (i, 0)), pl.BlockSpec((tm, D_MODEL), lambda i: (i, 0)))
        + (hbm,) * n_ride,
        scratch_shapes=_gather_sems(n_ride),
        compiler_params=_params(("arbitrary",)),
    )(x2, vecs, w_in_p, *riders)


def _inproj_bwd(dproj, x2, dxa, vecs, w_in_pt, tm, riders):
    seq = x2.shape[0]
    n_tiles = seq // tm
    n_ride = len(riders)

    def body(*refs):
        dp_ref, x_ref, dxa_ref, vec_ref, w_ref = refs[:5]
        ride_in, refs = refs[5:5 + n_ride], refs[5 + n_ride:]
        gx_ref, sums_ref = refs[:2]
        ride_out, sems = refs[2:2 + n_ride], refs[2 + n_ride:]
        exchange = _ChipScatter(ride_in, ride_out, sems)

        @pl.when(pl.program_id(0) == 0)
        def _():
            exchange.start()
            sums_ref[...] = jnp.zeros_like(sums_ref)

        du = _mm(dp_ref[...], w_ref[...])
        xh, rstd = _ln(x_ref[...])
        sums_ref[0:1, :] += _colsum(du)
        sums_ref[1:2, :] += _colsum(du * xh)
        gx_ref[...] = dxa_ref[...] + _ln_bwd(du * (1.0 + _mod(vec_ref, MOD_SCALE1)), xh, rstd)

        @pl.when(pl.program_id(0) == n_tiles - 1)
        def _():
            exchange.wait()

    tile = pl.BlockSpec((tm, D_MODEL), lambda i: (i, 0))
    hbm = pl.BlockSpec(memory_space=pl.ANY)
    return pl.pallas_call(
        body, name="inproj_bwd", grid=(n_tiles,),
        out_shape=(jax.ShapeDtypeStruct((seq, D_MODEL), jnp.float32), jax.ShapeDtypeStruct((8, D_MODEL), jnp.float32))
        + _exchange_out_shapes(riders, False),
        in_specs=[pl.BlockSpec((tm, N_PROJ), lambda i: (i, 0)), tile, tile, _const_spec(vecs.shape),
                  _const_spec(w_in_pt.shape)] + [hbm] * n_ride,
        out_specs=(tile, pl.BlockSpec((8, D_MODEL), lambda i: (0, 0))) + (hbm,) * n_ride,
        scratch_shapes=_scatter_sems(n_ride),
        compiler_params=_params(("arbitrary",)),
    )(dproj, x2, dxa, vecs, w_in_pt, *riders)


def _head(h):
    return slice(h * HEAD_W, (h + 1) * HEAD_W)


def _cols(ref, off, h):
    return ref[:, off + h * HEAD_W:off + (h + 1) * HEAD_W].astype(jnp.float32)


HEADS = range(N_HEADS)


def _mixer_chunk_forward(p_ref, cc, ss, dm_ref, qdec_ref, kdec_ref, wg_ref, bg_ref, states):
    row, col = _tri_masks()
    lower = row >= col
    f = {}
    f["glr"] = p_ref[:, OFF_LR:OFF_LR + HEAD_W]
    f["logit"] = _mm(f["glr"], wg_ref[...]) + bg_ref[...]
    rq = [_cols(p_ref, OFF_RQ, h) for h in HEADS]
    rk = [_cols(p_ref, OFF_RK, h) for h in HEADS]
    f["rv"] = [p_ref[:, OFF_RV + h * HEAD_W:OFF_RV + (h + 1) * HEAD_W] for h in HEADS]
    f["qr"] = [(rq[h] * cc + _swap_halves(rq[h]) * ss) * RET_SCALE for h in HEADS]
    f["kr"] = [rk[h] * cc + _swap_halves(rk[h]) * ss for h in HEADS]
    s_raw = [_mm_nt(f["qr"][h], f["kr"][h]) for h in HEADS]
    yield
    la = _log_sigmoid(f["logit"]) * (1.0 / GATE_TAU)
    b = _running_sum(lower, la)
    f["qd"] = [f["qr"][h] * qdec_ref[:, _head(h)] for h in HEADS]
    f["kd"] = [f["kr"][h] * kdec_ref[:, _head(h)] for h in HEADS]
    f["scores"] = [s_raw[h] * dm_ref[h] for h in HEADS]
    yield
    b_last = b[CHUNK - 1:CHUNK, :]
    b_mid = b[CHUNK // 2 - 1:CHUNK // 2, :]
    f["e"], f["ei"] = jnp.exp(b - b_mid), jnp.exp(b_mid - b)
    f["eb"], f["ek"], f["ebl"] = jnp.exp(b), jnp.exp(b_last - b), jnp.exp(b_last)
    gq = [_cols(p_ref, OFF_GQ, h) for h in HEADS]
    gk = [_cols(p_ref, OFF_GK, h) for h in HEADS]
    f["gv"] = [p_ref[:, OFF_GV + h * HEAD_W:OFF_GV + (h + 1) * HEAD_W] for h in HEADS]
    f["q_e"] = [gq[h] * f["e"][:, _head(h)] for h in HEADS]
    f["q_i"] = [gq[h] * f["ei"][:, _head(h)] for h in HEADS]
    f["k_e"] = [gk[h] * f["e"][:, _head(h)] for h in HEADS]
    f["k_i"] = [gk[h] * f["ei"][:, _head(h)] for h in HEADS]
    low = [_mm_nt(f["q_e"][h], f["k_i"][h]) for h in HEADS]
    up = [_mm_nt(f["q_i"][h], f["k_e"][h]) for h in HEADS]
    yield
    f["att"] = [jnp.where(lower, low[h], up[h]) for h in HEADS]
    f["qb"] = [gq[h] * f["eb"][:, _head(h)] for h in HEADS]
    f["kb"] = [gk[h] * f["ek"][:, _head(h)] for h in HEADS]
    ret_state, gla_state_t = states()
    f["o_ret"] = [_mm(f["scores"][h], f["rv"][h]) + _mm(f["qd"][h], ret_state[h]) for h in HEADS]
    f["o_gla"] = [_mm(f["att"][h], f["gv"][h]) + _mm_nt(f["qb"][h], gla_state_t[h]) for h in HEADS]
    return f


def _interleave(generators):
    live = list(generators)
    while live:
        for g in list(live):
            try:
                next(g)
            except StopIteration:
                live.remove(g)


def _mixer_fwd(proj, tables, wg_p, bg_p, ret_norm_w, gla_norm_w, riders):
    seq = proj.shape[0]
    n_chunks = seq // CHUNK
    per_step = min(n_chunks, CHUNKS_PER_STEP)
    n_steps = n_chunks // per_step
    n_ride = len(riders)
    rot_a, rot_b, dm_t, qdec_t, kdec_t, chunk_decay = tables

    def body(*refs):
        p_ref, ra_ref, rb_ref, dm_ref, qdec_ref, kdec_ref, wg_ref, bg_ref, wr_ref, wl_ref = refs[:10]
        ride_in, refs = refs[10:10 + n_ride], refs[10 + n_ride:]
        mix_ref, rsave_ref, ssave_ref = refs[:3]
        ride_out, refs = refs[3:3 + n_ride], refs[3 + n_ride:]
        r_sc, s_sc = refs[:2]
        gather = _ChipGather(ride_in, ride_out, refs[2:])

        @pl.when(pl.program_id(0) == 0)
        def _():
            gather.start()
            r_sc[...] = jnp.zeros_like(r_sc)
            s_sc[...] = jnp.zeros_like(s_sc)

        def one_chunk(c):
            p_c = p_ref.at[c * CHUNK:(c + 1) * CHUNK, :]
            mix_c = mix_ref.at[c * CHUNK:(c + 1) * CHUNK, :]
            before = {}

            def states():
                before["ret"] = [r_sc[h] for h in HEADS]
                before["gla"] = [s_sc[h] for h in HEADS]
                for h in HEADS:
                    rsave_ref[c, h] = before["ret"][h].astype(rsave_ref.dtype)
                    ssave_ref[c, h] = before["gla"][h]
                return before["ret"], before["gla"]

            cc, ss = _rotary_chunk(ra_ref, c, rb_ref)
            f = yield from _mixer_chunk_forward(p_c, cc, ss, dm_ref, qdec_ref, kdec_ref, wg_ref, bg_ref, states)
            for h in HEADS:
                r_sc[h] = chunk_decay[h] * before["ret"][h] + _mm_tn(f["kd"][h], f["rv"][h])
            for h in HEADS:
                s_sc[h] = before["gla"][h] * f["ebl"][:, _head(h)] + _mm_tn(f["gv"][h], f["kb"][h])
            yield
            for h in HEADS:
                on, _ = _ln(f["o_ret"][h])
                g = _cols(p_c, OFF_RG, h)
                mix_c[:, _head(h)] = (on * wr_ref[:, _head(h)] * (g * _sigmoid(g))).astype(mix_ref.dtype)
            for h in HEADS:
                o = f["o_gla"][h]
                on = o * lax.rsqrt(_rowmean(o * o) + LN_EPS)
                g = _cols(p_c, OFF_GG, h)
                mix_c[:, _head(N_HEADS + h)] = (on * wl_ref[:, _head(h)] * (g * _sigmoid(g))).astype(mix_ref.dtype)

        for c0 in range(0, per_step, CHUNKS_IN_LOCKSTEP):
            _interleave([one_chunk(c) for c in range(c0, min(per_step, c0 + CHUNKS_IN_LOCKSTEP))])

        @pl.when(pl.program_id(0) == (3 * n_steps) // 4)
        def _():
            gather.forward()

        @pl.when(pl.program_id(0) == n_steps - 1)
        def _():
            gather.finish()

    state_shape = (n_chunks, N_HEADS, HEAD_W, HEAD_W)
    state_blk = pl.BlockSpec((per_step, N_HEADS, HEAD_W, HEAD_W), lambda i: (i, 0, 0, 0))
    rot_blk = pl.BlockSpec((per_step, 8, HEAD_W), lambda i: (i, 0, 0))
    rows = per_step * CHUNK
    hbm = pl.BlockSpec(memory_space=pl.ANY)
    return pl.pallas_call(
        body, name="mixer_fwd", grid=(n_steps,),
        out_shape=(jax.ShapeDtypeStruct((seq, D_MODEL), MXU_DTYPE),
                   jax.ShapeDtypeStruct(state_shape, MXU_DTYPE), jax.ShapeDtypeStruct(state_shape, jnp.float32))
        + _exchange_out_shapes(riders, True),
        in_specs=[pl.BlockSpec((rows, N_PROJ), lambda i: (i, 0)), rot_blk, _const_spec(rot_b.shape),
                  _const_spec(dm_t.shape), _const_spec(qdec_t.shape), _const_spec(kdec_t.shape),
                  _const_spec(wg_p.shape), _const_spec(bg_p.shape), _const_spec(ret_norm_w.shape),
                  _const_spec(gla_norm_w.shape)] + [hbm] * n_ride,
        out_specs=(pl.BlockSpec((rows, D_MODEL), lambda i: (i, 0)), state_blk, state_blk) + (hbm,) * n_ride,
        scratch_shapes=[pltpu.VMEM((N_HEADS, HEAD_W, HEAD_W), jnp.float32),
                        pltpu.VMEM((N_HEADS, HEAD_W, HEAD_W), jnp.float32)] + _gather_sems(n_ride),
        compiler_params=_params(("arbitrary",)),
    )(proj, rot_a, rot_b, dm_t, qdec_t, kdec_t, wg_p, bg_p, ret_norm_w, gla_norm_w, *riders)


def _mixer_bwd(proj, dmixed, rsave, ssave, tables, wg_p, bg_p, ret_norm_w, gla_norm_w, riders):
    seq = proj.shape[0]
    n_chunks = seq // CHUNK
    per_step = min(n_chunks, CHUNKS_PER_STEP)
    n_steps = n_chunks // per_step
    n_ride = len(riders)
    rot_a, rot_b, dm_t, qdec_t, kdec_t, chunk_decay = tables
    last = n_steps - 1

    def body(*refs):
        p_blk, dmx_blk = refs[:2]
        shared_in = refs[2:13]
        ride_in, refs = refs[13:13 + n_ride], refs[13 + n_ride:]
        dp_blk, dwr_ref, dwl_ref, dwg_ref, dbg_ref = refs[:5]
        ride_out, refs = refs[5:5 + n_ride], refs[5 + n_ride:]
        dr_sc, ds_sc = refs[:2]
        exchange = _ChipScatter(ride_in, ride_out, refs[2:])

        @pl.when(pl.program_id(0) == 0)
        def _():
            exchange.start()
            dr_sc[...] = jnp.zeros_like(dr_sc)
            ds_sc[...] = jnp.zeros_like(ds_sc)
            dwr_ref[...] = jnp.zeros_like(dwr_ref)
            dwl_ref[...] = jnp.zeros_like(dwl_ref)
            dwg_ref[...] = jnp.zeros_like(dwg_ref)
            dbg_ref[...] = jnp.zeros_like(dbg_ref)

        def chunk_stages(c):
            rows = slice(c * CHUNK, (c + 1) * CHUNK)
            return one_chunk(c, p_blk.at[rows, :], dmx_blk.at[rows, :], dp_blk.at[rows, :], *shared_in,
                             dwr_ref, dwl_ref, dwg_ref, dbg_ref, dr_sc, ds_sc)

        for c0 in range(per_step, 0, -CHUNKS_IN_LOCKSTEP):
            _interleave([chunk_stages(c) for c in reversed(range(max(0, c0 - CHUNKS_IN_LOCKSTEP), c0))])

        @pl.when(pl.program_id(0) == last)
        def _():
            exchange.wait()

    def one_chunk(c, p_ref, dmx_ref, dp_ref, rsave_ref, ssave_ref, ra_ref, rb_ref, dm_ref, qdec_ref, kdec_ref,
                  wg_ref, bg_ref, wr_ref, wl_ref, dwr_ref, dwl_ref, dwg_ref, dbg_ref, dr_sc, ds_sc):
        def put(off, h, val):
            dp_ref[:, off + h * HEAD_W:off + (h + 1) * HEAD_W] = val.astype(dp_ref.dtype)

        cc, ss = _rotary_chunk(ra_ref, c, rb_ref)
        row, col = _tri_masks()
        ret_state = [rsave_ref[c, h] for h in HEADS]
        gla_state_t = [ssave_ref[c, h] for h in HEADS]
        f = yield from _mixer_chunk_forward(p_ref, cc, ss, dm_ref, qdec_ref, kdec_ref, wg_ref, bg_ref,
                                            lambda: (ret_state, gla_state_t))
        yield

        do_ret, do_gla = [], []
        for h in HEADS:
            on, rstd = _ln(f["o_ret"][h])
            g = _cols(p_ref, OFF_RG, h)
            sg = _sigmoid(g)
            dy = dmx_ref[:, _head(h)].astype(jnp.float32)
            wr = wr_ref[:, _head(h)]
            dwr_ref[:, _head(h)] += _colsum(dy * on * (g * sg))
            put(OFF_RG, h, dy * on * wr * (sg * (1.0 + g * (1.0 - sg))))
            do_ret.append(_ln_bwd(dy * wr * (g * sg), on, rstd))
        for h in HEADS:
            o = f["o_gla"][h]
            rstd = lax.rsqrt(_rowmean(o * o) + LN_EPS)
            on = o * rstd
            g = _cols(p_ref, OFF_GG, h)
            sg = _sigmoid(g)
            dy = dmx_ref[:, _head(N_HEADS + h)].astype(jnp.float32)
            wl = wl_ref[:, _head(h)]
            dwl_ref[:, _head(h)] += _colsum(dy * on * (g * sg))
            put(OFF_GG, h, dy * on * wl * (sg * (1.0 + g * (1.0 - sg))))
            don = dy * wl * (g * sg)
            do_gla.append(rstd * (don - on * _rowmean(don * on)))

        yield

        d_ret_new = [dr_sc[h] for h in HEADS]
        d_gla_new = [ds_sc[h] for h in HEADS]
        ds_raw = [_mm_nt(do_ret[h], f["rv"][h]) * dm_ref[h] for h in HEADS]
        d_att = [_mm_nt(do_gla[h], f["gv"][h]) for h in HEADS]
        dq_state = [_mm_nt(do_ret[h], ret_state[h]) for h in HEADS]
        dk_state = [_mm_nt(f["rv"][h], d_ret_new[h]) for h in HEADS]
        dqb = [_mm(do_gla[h], gla_state_t[h]) for h in HEADS]
        dkb = [_mm(f["gv"][h], d_gla_new[h]) for h in HEADS]
        for h in HEADS:
            put(OFF_RV, h, _mm_tn(f["scores"][h], do_ret[h]) + _mm(f["kd"][h], d_ret_new[h]))
        for h in HEADS:
            put(OFF_GV, h, _mm_tn(f["att"][h], do_gla[h]) + _mm_nt(f["kb"][h], d_gla_new[h]))
        for h in HEADS:
            dr_sc[h] = chunk_decay[h] * d_ret_new[h] + _mm_tn(f["qd"][h], do_ret[h])
        for h in HEADS:
            ds_sc[h] = d_gla_new[h] * f["ebl"][:, _head(h)] + _mm_tn(do_gla[h], f["qb"][h])
        yield

        dqr = [_mm(ds_raw[h], f["kr"][h]) + dq_state[h] * qdec_ref[:, _head(h)] for h in HEADS]
        dkr = [_mm_tn(ds_raw[h], f["qr"][h]) + dk_state[h] * kdec_ref[:, _head(h)] for h in HEADS]
        d_low = [jnp.where(row >= col, d_att[h], 0.0) for h in HEADS]
        d_up = [jnp.where(row < col, d_att[h], 0.0) for h in HEADS]
        dq_e = [_mm(d_low[h], f["k_i"][h]) for h in HEADS]
        dk_i = [_mm_tn(d_low[h], f["q_e"][h]) for h in HEADS]
        dq_i = [_mm(d_up[h], f["k_e"][h]) for h in HEADS]
        dk_e = [_mm_tn(d_up[h], f["q_i"][h]) for h in HEADS]
        yield
        for h in HEADS:
            put(OFF_RQ, h, (dqr[h] * cc + _swap_halves(dqr[h] * ss)) * RET_SCALE)
            put(OFF_RK, h, dkr[h] * cc + _swap_halves(dkr[h] * ss))
        row_id = lax.broadcasted_iota(jnp.int32, (CHUNK, HEAD_W), 0)
        db_heads = []
        for h in HEADS:
            hs = _head(h)
            e, ei, eb, ek, ebl = f["e"][:, hs], f["ei"][:, hs], f["eb"][:, hs], f["ek"][:, hs], f["ebl"][:, hs]
            put(OFF_GQ, h, dq_e[h] * e + dq_i[h] * ei + dqb[h] * eb)
            put(OFF_GK, h, dk_e[h] * e + dk_i[h] * ei + dkb[h] * ek)
            db = (dq_e[h] * f["q_e"][h] - dq_i[h] * f["q_i"][h] + dk_e[h] * f["k_e"][h] - dk_i[h] * f["k_i"][h]
                  + dqb[h] * f["qb"][h] - dkb[h] * f["kb"][h])
            db_last = _colsum(dkb[h] * f["kb"][h]) + ebl * _colsum(gla_state_t[h] * d_gla_new[h])
            db_heads.append(db + jnp.where(row_id == CHUNK - 1, db_last, 0.0))
        db = jnp.concatenate(db_heads, axis=1)
        d_la = _running_sum(col >= row, db)
        d_logit = d_la * (1.0 / GATE_TAU) * (1.0 - _sigmoid(f["logit"]))
        put(OFF_LR, 0, _mm_nt(d_logit, wg_ref[...]))
        dwg_ref[...] += _mm_tn(f["glr"], d_logit)
        dbg_ref[...] += _colsum(d_logit)

    state_blk = pl.BlockSpec((per_step, N_HEADS, HEAD_W, HEAD_W), lambda i: (last - i, 0, 0, 0))
    rot_blk = pl.BlockSpec((per_step, 8, HEAD_W), lambda i: (last - i, 0, 0))
    width = N_HEADS * HEAD_W
    vec_out = pl.BlockSpec((1, width), lambda i: (0, 0))
    hbm = pl.BlockSpec(memory_space=pl.ANY)
    rows_blk = per_step * CHUNK
    return pl.pallas_call(
        body, name="mixer_bwd", grid=(n_steps,),
        out_shape=(jax.ShapeDtypeStruct((seq, N_PROJ), MXU_DTYPE),
                   jax.ShapeDtypeStruct((1, width), jnp.float32), jax.ShapeDtypeStruct((1, width), jnp.float32),
                   jax.ShapeDtypeStruct((HEAD_W, width), jnp.float32), jax.ShapeDtypeStruct((1, width), jnp.float32))
        + _exchange_out_shapes(riders, False),
        in_specs=[pl.BlockSpec((rows_blk, N_PROJ), lambda i: (last - i, 0)),
                  pl.BlockSpec((rows_blk, D_MODEL), lambda i: (last - i, 0)), state_blk, state_blk, rot_blk,
                  _const_spec(rot_b.shape),
                  _const_spec(dm_t.shape), _const_spec(qdec_t.shape), _const_spec(kdec_t.shape),
                  _const_spec(wg_p.shape), _const_spec(bg_p.shape), _const_spec(ret_norm_w.shape),
                  _const_spec(gla_norm_w.shape)] + [hbm] * n_ride,
        out_specs=(pl.BlockSpec((rows_blk, N_PROJ), lambda i: (last - i, 0)), vec_out, vec_out,
                   pl.BlockSpec((HEAD_W, width), lambda i: (0, 0)), vec_out) + (hbm,) * n_ride,
        scratch_shapes=[pltpu.VMEM((N_HEADS, HEAD_W, HEAD_W), jnp.float32),
                        pltpu.VMEM((N_HEADS, HEAD_W, HEAD_W), jnp.float32)] + _scatter_sems(n_ride),
        compiler_params=_params(("arbitrary",)),
    )(proj, dmixed, rsave, ssave, rot_a, rot_b, dm_t, qdec_t, kdec_t, wg_p, bg_p, ret_norm_w, gla_norm_w, *riders)


V_GATE1, V_SCALE2, V_SHIFT2, V_GATE2, V_LN1W, V_LN1B, V_LN2W, V_LN2B = range(8)
S_GATE1, S_SCALE2, S_SHIFT2, S_GATE2, S_LN1W, S_LN1B, S_LN2W, S_LN2B, S_LOSS = range(9)


def _mlp_fwd_bwd(x2, mixed, target, mod, ln_rows, w_out, w1_chunks, w2_chunks, tm):
    seq = x2.shape[0]
    n_fc, _, fc = w1_chunks.shape
    segment_of = {V_GATE1: MOD_GATE1, V_SCALE2: MOD_SCALE2, V_SHIFT2: MOD_SHIFT2, V_GATE2: MOD_GATE2}

    def body(x_ref, mx_ref, t_ref, mod_ref, ln_ref, wo_ref, w1_ref, w2_ref,
             dmx_ref, dxa_ref, a_ref, dh_ref, u2_ref, df_ref, dm_ref, sums_ref, relu_sc):
        @pl.when(pl.program_id(0) == 0)
        def _():
            sums_ref[...] = jnp.zeros_like(sums_ref)

        def vec(r):
            if r in segment_of:
                return _mod(mod_ref, segment_of[r])
            return ln_ref[r - V_LN1W:r - V_LN1W + 1, :]

        def acc(r, val):
            sums_ref[r:r + 1, :] += _colsum(val)

        xx = x_ref[...]
        m = _mm(mx_ref[...], wo_ref[...])
        z1h, rstd1 = _ln(ALPHA * xx + vec(V_GATE1) * m)
        x1 = z1h * vec(V_LN1W) + vec(V_LN1B)
        x1h, rstd0 = _ln(x1)
        u2 = (x1h * (1.0 + vec(V_SCALE2)) + vec(V_SHIFT2)).astype(MXU_DTYPE)
        u2_ref[...] = u2
        f = jnp.zeros((tm, D_MODEL), jnp.float32)
        for j in range(n_fc):
            r = jnp.maximum(_mm(u2, w1_ref[j]), 0.0)
            relu_sc[:, j * fc:(j + 1) * fc] = r
            a = (r * r).astype(MXU_DTYPE)
            a_ref[:, j * fc:(j + 1) * fc] = a
            f = f + _mm(a, w2_ref[j])
        z2h, rstd2 = _ln(ALPHA * x1 + vec(V_GATE2) * f)
        err = z2h * vec(V_LN2W) + vec(V_LN2B) - t_ref[...]
        acc(S_LOSS, err * err)
        dy = err * (1.0 / D_MODEL)
        acc(S_LN2W, dy * z2h)
        acc(S_LN2B, dy)
        dz2 = _ln_bwd(dy * vec(V_LN2W), z2h, rstd2)
        acc(S_GATE2, dz2 * f)
        df = (vec(V_GATE2) * dz2).astype(MXU_DTYPE)
        df_ref[...] = df
        du2 = jnp.zeros((tm, D_MODEL), jnp.float32)
        for j in range(n_fc):
            dh = (_mm_nt(df, w2_ref[j]) * (2.0 * relu_sc[:, j * fc:(j + 1) * fc])).astype(MXU_DTYPE)
            dh_ref[:, j * fc:(j + 1) * fc] = dh
            du2 = du2 + _mm_nt(dh, w1_ref[j])
        acc(S_SCALE2, du2 * x1h)
        acc(S_SHIFT2, du2)
        dx1 = ALPHA * dz2 + _ln_bwd(du2 * (1.0 + vec(V_SCALE2)), x1h, rstd0)
        acc(S_LN1W, dx1 * z1h)
        acc(S_LN1B, dx1)
        dz1 = _ln_bwd(dx1 * vec(V_LN1W), z1h, rstd1)
        acc(S_GATE1, dz1 * m)
        dxa_ref[...] = ALPHA * dz1
        dm = (vec(V_GATE1) * dz1).astype(MXU_DTYPE)
        dm_ref[...] = dm
        dmx_ref[...] = _mm_nt(dm, wo_ref[...])

    tile = lambda width: pl.BlockSpec((tm, width), lambda i: (i, 0))
    f32 = lambda width: jax.ShapeDtypeStruct((seq, width), jnp.float32)
    b16 = lambda width: jax.ShapeDtypeStruct((seq, width), MXU_DTYPE)
    return pl.pallas_call(
        body, name="mlp_fwd_bwd", grid=(seq // tm,),
        out_shape=(f32(D_MODEL), f32(D_MODEL), b16(D_FF), b16(D_FF), b16(D_MODEL), b16(D_MODEL), b16(D_MODEL),
                   jax.ShapeDtypeStruct((16, D_MODEL), jnp.float32)),
        in_specs=[tile(D_MODEL), tile(D_MODEL), tile(D_MODEL), _const_spec(mod.shape), _const_spec(ln_rows.shape),
                  _const_spec(w_out.shape), _const_spec(w1_chunks.shape), _const_spec(w2_chunks.shape)],
        out_specs=(tile(D_MODEL), tile(D_MODEL), tile(D_FF), tile(D_FF), tile(D_MODEL), tile(D_MODEL),
                   tile(D_MODEL), pl.BlockSpec((16, D_MODEL), lambda i: (0, 0))),
        scratch_shapes=[pltpu.VMEM((tm, D_FF), jnp.float32)],
        compiler_params=_params(("arbitrary",)),
    )(x2, mixed, target, mod, ln_rows, w_out, w1_chunks, w2_chunks)


def _grad_matmul(a, b, name, tn, blocks_are_rows, riders=()):
    seq, m_dim = a.shape
    n_dim = b.shape[1]
    tk = min(seq, GRAD_TOKEN_TILE)
    nk = seq // tk
    n_ride = len(riders)
    if blocks_are_rows:
        tm = m_dim // N_CHIP
        assert tn == n_dim
        per_step = N_CHIP if m_dim <= GRAD_ROWS_PER_STEP else 1
        grid = (N_CHIP // per_step, 1, nk)
        out_map = lambda i, j, k: (i, 0, 0)
    else:
        tm = m_dim
        assert tn * N_CHIP == n_dim
        per_step = 1
        grid = (1, N_CHIP, nk)
        out_map = lambda i, j, k: (j, 0, 0)
    n_blocks = grid[0] * grid[1]
    rows = per_step * tm

    def body(*refs):
        a_ref, b_ref = refs[:2]
        ride_in, refs = refs[2:2 + n_ride], refs[2 + n_ride:]
        o_ref = refs[0]
        ride_out, refs = refs[1:1 + n_ride], refs[1 + n_ride:]
        acc_sc = refs[0]
        exchange = _ChipScatter(ride_in, ride_out, refs[1:]) if n_ride else None
        block = pl.program_id(0) + pl.program_id(1)
        k = pl.program_id(2)

        if exchange is not None:
            @pl.when((block == 0) & (k == 0))
            def _():
                exchange.start()

        @pl.when(k == 0)
        def _():
            acc_sc[...] = jnp.zeros_like(acc_sc)

        acc_sc[...] += _mm_tn(a_ref[...], b_ref[...])

        @pl.when(k == nk - 1)
        def _():
            for p in range(per_step):
                o_ref[p] = acc_sc[p * tm:(p + 1) * tm, :].astype(o_ref.dtype)

        if exchange is not None:
            @pl.when((block == n_blocks - 1) & (k == nk - 1))
            def _():
                exchange.wait()

    hbm = pl.BlockSpec(memory_space=pl.ANY)
    out = pl.pallas_call(
        body, name=name, grid=grid,
        out_shape=(jax.ShapeDtypeStruct((N_CHIP, tm, tn), WIRE_DTYPE),) + _exchange_out_shapes(riders, False),
        in_specs=[pl.BlockSpec((tk, rows), lambda i, j, k: (k, i)), pl.BlockSpec((tk, tn), lambda i, j, k: (k, j))]
        + [hbm] * n_ride,
        out_specs=(pl.BlockSpec((per_step, tm, tn), out_map),) + (hbm,) * n_ride,
        scratch_shapes=[pltpu.VMEM((rows, tn), jnp.float32)] + (_scatter_sems(n_ride) if n_ride else []),
        compiler_params=_params(("arbitrary", "arbitrary", "arbitrary")),
    )(a, b, *riders)
    return out if n_ride else out[0]


def _grad_matmul_full(a, b, name, tm, riders):
    seq, m_dim = a.shape
    n_dim = b.shape[1]
    tk = min(seq, GRAD_TOKEN_TILE)
    nk = seq // tk
    n_blocks = m_dim // tm
    n_ride = len(riders)
    assert m_dim % tm == 0

    def body(*refs):
        a_ref, b_ref = refs[:2]
        ride_in, refs = refs[2:2 + n_ride], refs[2 + n_ride:]
        o_ref = refs[0]
        ride_out, refs = refs[1:1 + n_ride], refs[1 + n_ride:]
        acc_sc = refs[0]
        swap = _SiblingSwap(ride_in, ride_out, refs[1:])
        i, k = pl.program_id(0), pl.program_id(1)

        @pl.when((i == 0) & (k == 0))
        def _():
            swap.start()

        @pl.when(k == 0)
        def _():
            acc_sc[...] = jnp.zeros_like(acc_sc)

        acc_sc[...] += _mm_tn(a_ref[...], b_ref[...])

        @pl.when(k == nk - 1)
        def _():
            o_ref[...] = acc_sc[...].astype(o_ref.dtype)

        @pl.when((i == n_blocks - 1) & (k == nk - 1))
        def _():
            swap.wait()

    hbm = pl.BlockSpec(memory_space=pl.ANY)
    return pl.pallas_call(
        body, name=name, grid=(n_blocks, nk),
        out_shape=(jax.ShapeDtypeStruct((m_dim, n_dim), WIRE_DTYPE),)
        + tuple(jax.ShapeDtypeStruct(r.shape, r.dtype) for r in riders),
        in_specs=[pl.BlockSpec((tk, tm), lambda i, k: (k, i)), pl.BlockSpec((tk, n_dim), lambda i, k: (k, 0))]
        + [hbm] * n_ride,
        out_specs=(pl.BlockSpec((tm, n_dim), lambda i, k: (i, 0)),) + (hbm,) * n_ride,
        scratch_shapes=[pltpu.VMEM((tm, n_dim), jnp.float32)] + _swap_sems(n_ride),
        compiler_params=_params(("arbitrary", "arbitrary")),
    )(a, b, *riders)


def _adam_pair(w, g_mine, g_sibling, m, v, name):
    rows, cols = w.shape
    tc = min(cols, ELEMENTWISE_COLS)

    def total(ref):
        if len(ref.shape) == 2:
            return ref[...]
        acc = ref[0].astype(jnp.float32)
        for j in range(1, ref.shape[0]):
            acc = acc + ref[j].astype(jnp.float32)
        return acc

    def body(w_ref, ga_ref, gb_ref, m_ref, v_ref, g_ref, dl_ref, m2_ref, v2_ref):
        g = total(ga_ref) + total(gb_ref)
        delta, m2, v2 = _adam(w_ref[...], g, m_ref[...], v_ref[...])
        g_ref[...] = g
        dl_ref[...] = delta
        m2_ref[...] = m2
        v2_ref[...] = v2

    blk = pl.BlockSpec((rows, tc), lambda i: (0, i))
    g_blk = lambda a: blk if a.ndim == 2 else pl.BlockSpec((a.shape[0], rows, tc), lambda i: (0, 0, i))
    out = jax.ShapeDtypeStruct((rows, cols), jnp.float32)
    return pl.pallas_call(
        body, name=name, grid=(cols // tc,),
        out_shape=(out, out, out, out),
        in_specs=[blk, g_blk(g_mine), g_blk(g_sibling), blk, blk], out_specs=(blk,) * 4,
        compiler_params=_params(("arbitrary",)),
    )(w, g_mine, g_sibling, m, v)


def _sum_devices(gathered, layout):
    def body(g_ref, *o_refs):
        total = g_ref[0]
        for d in range(1, N_DEV):
            total = total + g_ref[d]
        for (first, (rows_out, cols_out)), o_ref in zip(layout, o_refs):
            per_row = cols_out // 128
            for r in range(rows_out):
                for k in range(per_row):
                    src = first + r * per_row + k
                    o_ref[r:r + 1, k * 128:(k + 1) * 128] = total[src:src + 1, :]
        tail = total[total.shape[0] - 8:, :]
        o_refs[-1][...] = jnp.full((1, 128), jnp.sum(tail), jnp.float32)

    out_shape = tuple(jax.ShapeDtypeStruct(shape, jnp.float32) for _, shape in layout)
    return pl.pallas_call(
        body, name="sum_devices",
        out_shape=out_shape + (jax.ShapeDtypeStruct((1, 128), jnp.float32),),
    )(gathered)


def _adam_small(params):
    n = len(params)

    def body(*refs):
        ins, outs = refs[:4 * n], refs[4 * n:]
        for i in range(n):
            w_ref, g_ref, m_ref, v_ref = ins[4 * i:4 * i + 4]
            delta, m2, v2 = _adam(w_ref[...], g_ref[...], m_ref[...], v_ref[...])
            outs[3 * i][...] = delta
            outs[3 * i + 1][...] = m2
            outs[3 * i + 2][...] = v2

    out_shape = tuple(jax.ShapeDtypeStruct(p[0].shape, jnp.float32) for p in params for _ in range(3))
    out = pl.pallas_call(body, name="adam_small", out_shape=out_shape)(*[t for p in params for t in p])
    return [out[3 * i:3 * i + 3] for i in range(n)]


def _pad_heads(w):
    lead = w.shape[:-1]
    w = w.reshape(lead + (N_HEADS, GLA_DK))
    w = jnp.pad(w, [(0, 0)] * len(lead) + [(0, 0), (0, HEAD_W - GLA_DK)])
    return w.reshape(lead + (N_HEADS * HEAD_W,))


def _unpad_heads(w):
    lead = w.shape[:-1]
    return w.reshape(lead + (N_HEADS, HEAD_W))[..., :GLA_DK].reshape(lead + (N_HEADS * GLA_DK,))


def _pad_head_rows(w):
    w = w.reshape(N_HEADS, GLA_DK, w.shape[-1])
    return jnp.pad(w, ((0, 0), (0, HEAD_W - GLA_DK), (0, 0))).reshape(N_HEADS * HEAD_W, w.shape[-1])


def _unpad_head_rows(w):
    return w.reshape(N_HEADS, HEAD_W, w.shape[-1])[:, :GLA_DK].reshape(N_HEADS * GLA_DK, w.shape[-1])


def _pad_w_in_rows(stack):
    w = stack.reshape(-1, stack.shape[-1])
    return jnp.concatenate([
        w[:2048], _pad_head_rows(w[2048:2304] * GLA_SCALE), _pad_head_rows(w[2304:2560]), w[2560:3584],
        jnp.pad(w[3584:3600], ((0, HEAD_W - GATE_RANK), (0, 0)))], axis=0)


def _unpad_w_in_stack(g, per):
    segments = [(0, g[:2048]), (2048, _unpad_head_rows(g[OFF_GQ:OFF_GQ + 512]) * GLA_SCALE),
                (2304, _unpad_head_rows(g[OFF_GK:OFF_GK + 512])), (2560, g[OFF_GV:OFF_LR]),
                (3584, g[OFF_LR:OFF_LR + GATE_RANK])]
    blocks = []
    for j in range(N_CHIP):
        lo, hi = j * per, (j + 1) * per
        pieces = []
        for start, rows in segments:
            a, b = max(lo, start), min(hi, start + rows.shape[0])
            if a < b:
                pieces.append(rows[a - start:b - start])
        blocks.append(jnp.concatenate(pieces, axis=0))
    return jnp.stack(blocks)


def _col_major(w):
    return jnp.transpose(w, (2, 0, 1)).reshape(w.shape[2], w.shape[1])


def _rows128(a):
    return a.reshape(-1, 128)


def kernel(x, c, w_ada, b_ada, w_in, ret_norm_w, gla_gate_w, gla_gate_b, gla_norm_w, w_out, ln1_w, ln1_b, w_ff1, w_ff2, ln2_w, ln2_b, loss_target, m_w_ada, m_b_ada, m_w_in, m_ret_norm_w, m_gla_gate_w, m_gla_gate_b, m_gla_norm_w, m_w_out, m_ln1_w, m_ln1_b, m_w_ff1, m_w_ff2, m_ln2_w, m_ln2_b, v_w_ada, v_b_ada, v_w_in, v_ret_norm_w, v_gla_gate_w, v_gla_gate_b, v_gla_norm_w, v_w_out, v_ln1_w, v_ln1_b, v_w_ff1, v_w_ff2, v_ln2_w, v_ln2_b):
    seq = x.shape[1]
    tm = min(seq, TOKEN_TILE)
    tm_in = min(seq, INPROJ_TOKEN_TILE)
    xi, yi, _ = _mesh_pos()
    chip = 2 * xi + yi
    x2, target = x[0], loss_target[0]
    ada_cols = w_ada.shape[2]
    in_cols = w_in.shape[2]
    gate_cols = gla_gate_w.shape[2]

    b_blk = lax.dynamic_slice(b_ada, (0, chip * ada_cols), (1, ada_cols))
    g0, _, mod, w_in_stack = _prologue(jnp.concatenate([_rows128(c), _rows128(gla_gate_w[0])], axis=0), w_ada[0],
                                       b_blk, _col_major(w_in.astype(WIRE_DTYPE)))
    c_all = g0[:, :8].reshape(N_DEV, D_MODEL)
    gate_w_full = jnp.concatenate([g0[2 * j, 8:16].reshape(GATE_RANK, gate_cols) for j in range(N_CHIP)], axis=1)
    wg_p = jnp.pad(_pad_heads(gate_w_full), ((0, HEAD_W - GATE_RANK), (0, 0)))
    bg_p = _pad_heads(gla_gate_b)
    w_in_pt = _pad_w_in_rows(w_in_stack).astype(MXU_DTYPE)
    w_in_p = jnp.transpose(w_in_pt)

    proj, u, w2_stack = _inproj_fwd(x2, mod, w_in_p, tm_in, [w_ff2[0].astype(WIRE_DTYPE)])
    rot_a, rot_b = _rotary_tables(seq)
    dm_t, qdec_t, kdec_t, chunk_decay = _decay_tables()
    tables = (rot_a, rot_b, dm_t, qdec_t, kdec_t, chunk_decay)
    mixed, rsave, ssave, w_out_stack, w1_stack = _mixer_fwd(
        proj, tables, wg_p, bg_p, ret_norm_w, gla_norm_w,
        [w_out[0].astype(WIRE_DTYPE), w_ff1[0].astype(WIRE_DTYPE)])
    w_out_full = w_out_stack.reshape(D_MODEL, D_MODEL).astype(MXU_DTYPE)
    w1_chunks = w1_stack.astype(MXU_DTYPE)
    w2_chunks = w2_stack.astype(MXU_DTYPE)

    ln_rows = jnp.concatenate([ln1_w, ln1_b, ln2_w, ln2_b], axis=0)
    dmixed, dxa, act, dh, u2, df, dm, sums2 = _mlp_fwd_bwd(x2, mixed, target, mod, ln_rows, w_out_full, w1_chunks,
                                                           w2_chunks, tm)

    g_out_stack = _grad_matmul(mixed, dm, "grad_w_out", D_MODEL, True)
    g_ff1_stack, r_out = _grad_matmul(u2, dh, "grad_w_ff1", D_FF // N_CHIP, False, [g_out_stack])
    g_ff2_stack = _grad_matmul(act, df, "grad_w_ff2", D_MODEL, True)
    dproj, d_ret_norm, d_gla_norm, d_wg_p, d_bg_p, r_ff1, r_ff2 = _mixer_bwd(
        proj, dmixed, rsave, ssave, tables, wg_p, bg_p, ret_norm_w, gla_norm_w, [g_ff1_stack, g_ff2_stack])
    early = ["w_out", "w_ff1", "w_ff2"]
    partial = dict(zip(early, [r_out, r_ff1, r_ff2]))
    g_in_t, *swapped_early = _grad_matmul_full(dproj, u, "grad_w_in", N_PROJ // 3, [partial[n] for n in early])
    swapped = dict(zip(early, swapped_early))
    g_in_stack = _unpad_w_in_stack(g_in_t, in_cols)
    grad_x, sums1, r_in = _inproj_bwd(dproj, x2, dxa, mod, w_in_pt, tm_in, [g_in_stack])

    sources = [sums1, sums2, d_ret_norm, _unpad_heads(d_bg_p), d_gla_norm, _unpad_heads(d_wg_p[:GATE_RANK])]
    pieces = [(0, 0, 0), (0, 1, 8), (1, S_GATE1, 16), (1, S_SHIFT2, 24), (1, S_SCALE2, 32), (1, S_GATE2, 40),
              (1, S_LN1W, 48), (1, S_LN1B, 56), (1, S_LN2W, 64), (1, S_LN2B, 72), (2, 0, 80), (3, 0, 88), (4, 0, 96)]
    pieces += [(5, r, 104 + 2 * r) for r in range(GATE_RANK)] + [(1, S_LOSS, 136)]
    partial["w_in"] = r_in
    g2, swapped["w_in"] = _gather_rows(sources, pieces, 144, "gather_small", [r_in])
    (grad_b_ada, grad_ln1_w, grad_ln1_b, grad_ln2_w, grad_ln2_b, grad_ret_norm, grad_gate_b, grad_gla_norm,
     grad_gate_w_full, loss_sum) = _sum_devices(g2, [
         (0, (1, 6 * D_MODEL)), (48, (1, D_MODEL)), (56, (1, D_MODEL)), (64, (1, D_MODEL)), (72, (1, D_MODEL)),
         (80, (1, 512)), (88, (1, 256)), (96, (1, 512)), (104, (GATE_RANK, 256))])
    loss = 0.5 / D_MODEL * loss_sum[0, 0]
    grad_gate_w = lax.dynamic_slice(grad_gate_w_full, (0, chip * gate_cols), (GATE_RANK, gate_cols))

    small_grads = [grad_b_ada, grad_ln1_w, grad_ln1_b, grad_ln2_w, grad_ln2_b, grad_ret_norm, grad_gate_b,
                   grad_gla_norm, grad_gate_w[None]]
    small_out = _adam_small(list(zip(
        [b_ada, ln1_w, ln1_b, ln2_w, ln2_b, ret_norm_w, gla_gate_b, gla_norm_w, gla_gate_w], small_grads,
        [m_b_ada, m_ln1_w, m_ln1_b, m_ln2_w, m_ln2_b, m_ret_norm_w, m_gla_gate_b, m_gla_norm_w, m_gla_gate_w],
        [v_b_ada, v_ln1_w, v_ln1_b, v_ln2_w, v_ln2_b, v_ret_norm_w, v_gla_gate_b, v_gla_norm_w, v_gla_gate_w])))
    sm_delta, sm_m, sm_v = [[o[k] for o in small_out] for k in range(3)]

    dmod_all = g2[:, 0:48].reshape(N_DEV, 6 * D_MODEL)
    dmod_blk = lax.dynamic_slice(dmod_all, (0, chip * ada_cols), (N_DEV, ada_cols))
    ada_out = _ada_bwd_adam(jnp.transpose(c_all), dmod_blk, w_ada[0], m_w_ada[0], v_w_ada[0])
    ada_g, ada_delta, ada_m, ada_v = [t[None] for t in ada_out]

    big = {}
    for n, w, m, v in zip(["w_in", "w_out", "w_ff1", "w_ff2"], [w_in, w_out, w_ff1, w_ff2],
                          [m_w_in, m_w_out, m_w_ff1, m_w_ff2], [v_w_in, v_w_out, v_w_ff1, v_w_ff2]):
        mine, theirs = partial[n], swapped[n]
        if n == "w_in":
            out = _adam_pair(_col_major(w), mine, theirs, _col_major(m), _col_major(v), "adam_" + n)
            big[n] = [jnp.transpose(t.reshape(t.shape[0], 1, t.shape[1]), (1, 2, 0)) for t in out]
        else:
            big[n] = [t[None] for t in _adam_pair(w[0], mine, theirs, m[0], v[0], "adam_" + n)]

    def assemble(ada, smalls, k):
        b_ada_o, ln1w_o, ln1b_o, ln2w_o, ln2b_o, ret_o, gb_o, gln_o, gw_o = smalls
        return [ada, b_ada_o, big["w_in"][k], ret_o, gw_o, gb_o, gln_o, big["w_out"][k], ln1w_o, ln1b_o,
                big["w_ff1"][k], big["w_ff2"][k], ln2w_o, ln2b_o]

    grads = assemble(ada_g, small_grads, 0)
    deltas = assemble(ada_delta, sm_delta, 1)
    new_m = assemble(ada_m, sm_m, 2)
    new_v = assemble(ada_v, sm_v, 3)
    return (loss, grad_x[None], *grads, *deltas, *new_m, *new_v)
```
